```python
import jax, jax.numpy as jnp
from jax import lax
import numpy as np

D_MODEL = 1024
BATCH = 8
SEQ = 4096
DEPTH = 4

N_MIXERS = 2
N_LAYERS_A = (DEPTH + 1) // 2
N_LAYERS_B = DEPTH // 2
MIX_WIDTH = D_MODEL
CONV_WIDTH = 3
CONV_GROUPS = 8
POOL_WINDOWS = (2, 4, 8, 16)
N_POOL_GROUPS = len(POOL_WINDOWS)
POOL_GROUP_DIM = MIX_WIDTH // N_POOL_GROUPS
PLE_DIM = 256
EPS = 1e-6

kernel_name = "hybrid_shortconv_pool_ple_trunk"


def rmsnorm(x, g):
    xf = x.astype(jnp.float32)
    r = lax.rsqrt(jnp.mean(xf * xf, axis=-1, keepdims=True) + EPS)
    return (xf * r).astype(x.dtype) * g


def causal_conv3(u, w):
    s = u.shape[1]
    up = jnp.pad(u, ((0, 0), (CONV_WIDTH - 1, 0), (0, 0)))
    return up[:, 0:s] * w[0] + up[:, 1:s + 1] * w[1] + up[:, 2:s + 2] * w[2]


def short_conv_mixer(h, w_in, w_conv, w_out):
    proj = h @ w_in
    b_g, c_g, v, z = jnp.split(proj, 4, axis=-1)
    y = b_g * causal_conv3(c_g * v, w_conv)
    return (jax.nn.silu(z) * y) @ w_out


def causal_window_mean(u, window):
    s = u.shape[1]
    cs = jnp.cumsum(u.astype(jnp.float32), axis=1)
    csp = jnp.pad(cs, ((0, 0), (window, 0), (0, 0)))
    win_sum = csp[:, window:window + s] - csp[:, 0:s]
    count = jnp.minimum(jnp.arange(1, s + 1, dtype=jnp.float32), float(window))
    return (win_sum / count[None, :, None]).astype(u.dtype)


def pool_mixer(h, w_in, w_grp, scale, w_out):
    proj = h @ w_in
    u, z = jnp.split(proj, 2, axis=-1)
    bsz, s, _ = u.shape
    u4 = u.reshape(bsz, s, N_POOL_GROUPS, POOL_GROUP_DIM)
    pooled = jnp.stack([causal_window_mean(u4[:, :, g], w) for g, w in enumerate(POOL_WINDOWS)], axis=2)
    d = pooled - u4
    mixed = jnp.einsum('bsgc,gcd->bsgd', d, w_grp).reshape(bsz, s, MIX_WIDTH) * scale
    return (jax.nn.silu(z) * mixed) @ w_out


def _fwd_setup_inputs(seed: int = 0) -> dict:
    key = jax.random.key(seed)
    ks = jax.random.split(key, 16)
    f32 = jnp.float32
    E, D, G = MIX_WIDTH, D_MODEL, POOL_GROUP_DIM
    nrm = lambda k, shape, fan_in: jax.random.normal(k, shape, f32) * (fan_in ** -0.5)
    gain = lambda k, shape: 1.0 + 0.02 * jax.random.normal(k, shape, f32)
    return {
        "x": jax.random.normal(ks[0], (BATCH, SEQ, D), f32),
        "p": jax.random.normal(ks[1], (DEPTH, BATCH, SEQ, PLE_DIM), f32),
        "norm_mix": gain(ks[2], (DEPTH, D)),
        "a_w_in": nrm(ks[3], (N_LAYERS_A, D, 4 * E), D),
        "a_w_conv": nrm(ks[4], (N_LAYERS_A, CONV_WIDTH, E), CONV_WIDTH),
        "a_w_out": nrm(ks[5], (N_LAYERS_A, E, D), E),
        "b_w_in": nrm(ks[6], (N_LAYERS_B, D, 2 * E), D),
        "b_w_grp": nrm(ks[7], (N_LAYERS_B, N_POOL_GROUPS, G, G), G),
        "b_scale": gain(ks[8], (N_LAYERS_B, E)),
        "b_w_out": nrm(ks[9], (N_LAYERS_B, E, D), E),
        "ple_norm": gain(ks[10], (DEPTH, D)),
        "ple_w_gate": nrm(ks[11], (DEPTH, D, D), D),
        "ple_w_proj": nrm(ks[12], (DEPTH, PLE_DIM, D), PLE_DIM),
        "final_norm": gain(ks[13], (D,)),
    }


def _fwd_reference(x, p, norm_mix, a_w_in, a_w_conv, a_w_out, b_w_in, b_w_grp, b_scale, b_w_out,
              ple_norm, ple_w_gate, ple_w_proj, final_norm):
    h = x
    for i in range(DEPTH):
        hn = rmsnorm(h, norm_mix[i])
        j = i // N_MIXERS
        if i % N_MIXERS == 0:
            h = h + short_conv_mixer(hn, a_w_in[j], a_w_conv[j], a_w_out[j])
        else:
            h = h + pool_mixer(hn, b_w_in[j], b_w_grp[j], b_scale[j], b_w_out[j])
        gate = jax.nn.sigmoid(rmsnorm(h, ple_norm[i]) @ ple_w_gate[i])
        h = h + gate * (p[i] @ ple_w_proj[i])
    return rmsnorm(h, final_norm)


import jax as _jax
import jax.numpy as _jnp

TWIN_FORMAT = 'train_step'
FWD_PARAMS = ['x', 'p', 'norm_mix', 'a_w_in', 'a_w_conv', 'a_w_out', 'b_w_in', 'b_w_grp', 'b_scale', 'b_w_out', 'ple_norm', 'ple_w_gate', 'ple_w_proj', 'final_norm']
TWIN_WEIGHTS = ['norm_mix', 'a_w_in', 'a_w_conv', 'a_w_out', 'b_w_in', 'b_w_grp', 'b_scale', 'b_w_out', 'ple_norm', 'ple_w_gate', 'ple_w_proj', 'final_norm']
TWIN_DIFF_INPUT = 'x'
TWIN_INPUTS = ['x', 'p', 'norm_mix', 'a_w_in', 'a_w_conv', 'a_w_out', 'b_w_in', 'b_w_grp', 'b_scale', 'b_w_out', 'ple_norm', 'ple_w_gate', 'ple_w_proj', 'final_norm', 'loss_target', 'm_norm_mix', 'm_a_w_in', 'm_a_w_conv', 'm_a_w_out', 'm_b_w_in', 'm_b_w_grp', 'm_b_scale', 'm_b_w_out', 'm_ple_norm', 'm_ple_w_gate', 'm_ple_w_proj', 'm_final_norm', 'v_norm_mix', 'v_a_w_in', 'v_a_w_conv', 'v_a_w_out', 'v_b_w_in', 'v_b_w_grp', 'v_b_scale', 'v_b_w_out', 'v_ple_norm', 'v_ple_w_gate', 'v_ple_w_proj', 'v_final_norm']
TWIN_OUTPUTS = ['loss', 'grad_x', 'grad_norm_mix', 'grad_a_w_in', 'grad_a_w_conv', 'grad_a_w_out', 'grad_b_w_in', 'grad_b_w_grp', 'grad_b_scale', 'grad_b_w_out', 'grad_ple_norm', 'grad_ple_w_gate', 'grad_ple_w_proj', 'grad_final_norm', 'delta_norm_mix', 'delta_a_w_in', 'delta_a_w_conv', 'delta_a_w_out', 'delta_b_w_in', 'delta_b_w_grp', 'delta_b_scale', 'delta_b_w_out', 'delta_ple_norm', 'delta_ple_w_gate', 'delta_ple_w_proj', 'delta_final_norm', 'new_m_norm_mix', 'new_m_a_w_in', 'new_m_a_w_conv', 'new_m_a_w_out', 'new_m_b_w_in', 'new_m_b_w_grp', 'new_m_b_scale', 'new_m_b_w_out', 'new_m_ple_norm', 'new_m_ple_w_gate', 'new_m_ple_w_proj', 'new_m_final_norm', 'new_v_norm_mix', 'new_v_a_w_in', 'new_v_a_w_conv', 'new_v_a_w_out', 'new_v_b_w_in', 'new_v_b_w_grp', 'new_v_b_scale', 'new_v_b_w_out', 'new_v_ple_norm', 'new_v_ple_w_gate', 'new_v_ple_w_proj', 'new_v_final_norm']
TWIN_LEAF_KINDS = {'loss': 'loss', 'grad_x': 'grad_x', 'grad_norm_mix': 'grad_w', 'grad_a_w_in': 'grad_w', 'grad_a_w_conv': 'grad_w', 'grad_a_w_out': 'grad_w', 'grad_b_w_in': 'grad_w', 'grad_b_w_grp': 'grad_w', 'grad_b_scale': 'grad_w', 'grad_b_w_out': 'grad_w', 'grad_ple_norm': 'grad_w', 'grad_ple_w_gate': 'grad_w', 'grad_ple_w_proj': 'grad_w', 'grad_final_norm': 'grad_w', 'delta_norm_mix': 'delta_w', 'delta_a_w_in': 'delta_w', 'delta_a_w_conv': 'delta_w', 'delta_a_w_out': 'delta_w', 'delta_b_w_in': 'delta_w', 'delta_b_w_grp': 'delta_w', 'delta_b_scale': 'delta_w', 'delta_b_w_out': 'delta_w', 'delta_ple_norm': 'delta_w', 'delta_ple_w_gate': 'delta_w', 'delta_ple_w_proj': 'delta_w', 'delta_final_norm': 'delta_w', 'new_m_norm_mix': 'new_m', 'new_m_a_w_in': 'new_m', 'new_m_a_w_conv': 'new_m', 'new_m_a_w_out': 'new_m', 'new_m_b_w_in': 'new_m', 'new_m_b_w_grp': 'new_m', 'new_m_b_scale': 'new_m', 'new_m_b_w_out': 'new_m', 'new_m_ple_norm': 'new_m', 'new_m_ple_w_gate': 'new_m', 'new_m_ple_w_proj': 'new_m', 'new_m_final_norm': 'new_m', 'new_v_norm_mix': 'new_v', 'new_v_a_w_in': 'new_v', 'new_v_a_w_conv': 'new_v', 'new_v_a_w_out': 'new_v', 'new_v_b_w_in': 'new_v', 'new_v_b_w_grp': 'new_v', 'new_v_b_scale': 'new_v', 'new_v_b_w_out': 'new_v', 'new_v_ple_norm': 'new_v', 'new_v_ple_w_gate': 'new_v', 'new_v_ple_w_proj': 'new_v', 'new_v_final_norm': 'new_v'}


def _forward(args):
    return _fwd_reference(*[args[k] for k in FWD_PARAMS])


def _output_shape():
    out = _jax.eval_shape(lambda: _forward(_fwd_setup_inputs(0)))
    return out.shape, out.dtype

N_MICROBATCH = 1
ADAM_LR = 0.001
ADAM_B1 = 0.9
ADAM_B2 = 0.999
ADAM_EPS = 1e-08
ADAM_WD = 0.01
ADAM_STEP = 10
PER_EXAMPLE_BATCH_AXIS = {'x': 0, 'p': 1, 'loss_target': 0}
SHARED_INPUTS = []
_WEIGHT_DTYPES = {'norm_mix': _jnp.float32, 'a_w_in': _jnp.float32, 'a_w_conv': _jnp.float32, 'a_w_out': _jnp.float32, 'b_w_in': _jnp.float32, 'b_w_grp': _jnp.float32, 'b_scale': _jnp.float32, 'b_w_out': _jnp.float32, 'ple_norm': _jnp.float32, 'ple_w_gate': _jnp.float32, 'ple_w_proj': _jnp.float32, 'final_norm': _jnp.float32}
MOMENT_SCALE = {'norm_mix': 1.373024e-01, 'a_w_in': 8.383467e-02, 'a_w_conv': 8.180741e-02, 'a_w_out': 8.189016e-02, 'b_w_in': 6.365100e-02, 'b_w_grp': 6.250405e-02, 'b_scale': 6.374404e-02, 'b_w_out': 6.243455e-02, 'ple_norm': 2.724649e-02, 'ple_w_gate': 2.701956e-02, 'ple_w_proj': 6.929007e-02, 'final_norm': 3.200452e+01}


def _to_microbatches(a, axis):
    t = _jnp.moveaxis(a, axis, 0)
    t = t.reshape((N_MICROBATCH, t.shape[0] // N_MICROBATCH) + t.shape[1:])
    return _jnp.moveaxis(t, 1, axis + 1)


def setup_inputs(seed: int = 0) -> dict:
    inp = _fwd_setup_inputs(seed)
    key = _jax.random.fold_in(_jax.random.key(seed), 7919)
    shape, _ = _output_shape()
    out = dict(inp)
    out["loss_target"] = _jax.random.normal(_jax.random.fold_in(key, 0), shape, _jnp.float32)
    for i, name in enumerate(TWIN_WEIGHTS):
        w = inp[name].astype(_jnp.float32)
        if MOMENT_SCALE is None:
            s = _jnp.sqrt(_jnp.mean(_jnp.square(w)) + 1e-30)
        else:
            s = MOMENT_SCALE[name]
        km, kv = _jax.random.split(_jax.random.fold_in(key, i + 1))
        out[name] = w
        out["m_" + name] = s * _jax.random.normal(km, w.shape, _jnp.float32)
        out["v_" + name] = (s * s) * _jax.random.uniform(kv, w.shape, _jnp.float32, 0.5, 1.5)
    if N_MICROBATCH > 1:
        for name, axis in PER_EXAMPLE_BATCH_AXIS.items():
            out[name] = _to_microbatches(out[name], axis)
    return {'x': out['x'], 'p': out['p'], 'norm_mix': out['norm_mix'], 'a_w_in': out['a_w_in'], 'a_w_conv': out['a_w_conv'], 'a_w_out': out['a_w_out'], 'b_w_in': out['b_w_in'], 'b_w_grp': out['b_w_grp'], 'b_scale': out['b_scale'], 'b_w_out': out['b_w_out'], 'ple_norm': out['ple_norm'], 'ple_w_gate': out['ple_w_gate'], 'ple_w_proj': out['ple_w_proj'], 'final_norm': out['final_norm'], 'loss_target': out['loss_target'], 'm_norm_mix': out['m_norm_mix'], 'm_a_w_in': out['m_a_w_in'], 'm_a_w_conv': out['m_a_w_conv'], 'm_a_w_out': out['m_a_w_out'], 'm_b_w_in': out['m_b_w_in'], 'm_b_w_grp': out['m_b_w_grp'], 'm_b_scale': out['m_b_scale'], 'm_b_w_out': out['m_b_w_out'], 'm_ple_norm': out['m_ple_norm'], 'm_ple_w_gate': out['m_ple_w_gate'], 'm_ple_w_proj': out['m_ple_w_proj'], 'm_final_norm': out['m_final_norm'], 'v_norm_mix': out['v_norm_mix'], 'v_a_w_in': out['v_a_w_in'], 'v_a_w_conv': out['v_a_w_conv'], 'v_a_w_out': out['v_a_w_out'], 'v_b_w_in': out['v_b_w_in'], 'v_b_w_grp': out['v_b_w_grp'], 'v_b_scale': out['v_b_scale'], 'v_b_w_out': out['v_b_w_out'], 'v_ple_norm': out['v_ple_norm'], 'v_ple_w_gate': out['v_ple_w_gate'], 'v_ple_w_proj': out['v_ple_w_proj'], 'v_final_norm': out['v_final_norm']}


def _loss(weights, diff, rest, loss_target):
    with _jax.named_scope("forward"):
        args = {**rest, TWIN_DIFF_INPUT: diff, **{k: w.astype(_WEIGHT_DTYPES[k]) for k, w in weights.items()}}
        y = _forward(args)
    with _jax.named_scope("loss_head"):
        err = _jnp.square(y.astype(_jnp.float32) - loss_target)
        return 0.5 * _jnp.sum(_jnp.mean(err, axis=-1)) if err.ndim else 0.5 * err


def _adamw(w, g, m, v):
    m = ADAM_B1 * m + (1.0 - ADAM_B1) * g
    v = ADAM_B2 * v + (1.0 - ADAM_B2) * _jnp.square(g)
    m_hat = m / (1.0 - ADAM_B1 ** ADAM_STEP)
    v_hat = v / (1.0 - ADAM_B2 ** ADAM_STEP)
    delta = -ADAM_LR * (m_hat / (_jnp.sqrt(v_hat) + ADAM_EPS) + ADAM_WD * w)
    return delta, m, v


def reference(x, p, norm_mix, a_w_in, a_w_conv, a_w_out, b_w_in, b_w_grp, b_scale, b_w_out, ple_norm, ple_w_gate, ple_w_proj, final_norm, loss_target, m_norm_mix, m_a_w_in, m_a_w_conv, m_a_w_out, m_b_w_in, m_b_w_grp, m_b_scale, m_b_w_out, m_ple_norm, m_ple_w_gate, m_ple_w_proj, m_final_norm, v_norm_mix, v_a_w_in, v_a_w_conv, v_a_w_out, v_b_w_in, v_b_w_grp, v_b_scale, v_b_w_out, v_ple_norm, v_ple_w_gate, v_ple_w_proj, v_final_norm):
    given = dict(x=x, p=p, norm_mix=norm_mix, a_w_in=a_w_in, a_w_conv=a_w_conv, a_w_out=a_w_out, b_w_in=b_w_in, b_w_grp=b_w_grp, b_scale=b_scale, b_w_out=b_w_out, ple_norm=ple_norm, ple_w_gate=ple_w_gate, ple_w_proj=ple_w_proj, final_norm=final_norm, loss_target=loss_target, m_norm_mix=m_norm_mix, m_a_w_in=m_a_w_in, m_a_w_conv=m_a_w_conv, m_a_w_out=m_a_w_out, m_b_w_in=m_b_w_in, m_b_w_grp=m_b_w_grp, m_b_scale=m_b_scale, m_b_w_out=m_b_w_out, m_ple_norm=m_ple_norm, m_ple_w_gate=m_ple_w_gate, m_ple_w_proj=m_ple_w_proj, m_final_norm=m_final_norm, v_norm_mix=v_norm_mix, v_a_w_in=v_a_w_in, v_a_w_conv=v_a_w_conv, v_a_w_out=v_a_w_out, v_b_w_in=v_b_w_in, v_b_w_grp=v_b_w_grp, v_b_scale=v_b_scale, v_b_w_out=v_b_w_out, v_ple_norm=v_ple_norm, v_ple_w_gate=v_ple_w_gate, v_ple_w_proj=v_ple_w_proj, v_final_norm=v_final_norm)
    weights = {n: given[n] for n in TWIN_WEIGHTS}
    shared = {n: given[n] for n in SHARED_INPUTS}
    per_example = {n: given[n] for n in ['x', 'p']}
    grad_fn = _jax.value_and_grad(_loss, argnums=(0, 1))

    def one_microbatch(ex, loss_target):
        ex = dict(ex)
        diff = ex.pop(TWIN_DIFF_INPUT)
        return grad_fn(weights, diff, {**shared, **ex}, loss_target)

    if N_MICROBATCH == 1:
        loss, (grad_w, grad_x) = one_microbatch(per_example, given["loss_target"])
    else:
        def body(carry, xs):
            loss_sum, grad_sum = carry
            l_k, (gw_k, gx_k) = one_microbatch(xs[0], xs[1])
            with _jax.named_scope("update"):
                return (loss_sum + l_k, _jax.tree.map(_jnp.add, grad_sum, gw_k)), gx_k

        init = (_jnp.zeros((), _jnp.float32), _jax.tree.map(_jnp.zeros_like, weights))
        (loss, grad_w), grad_x = _jax.lax.scan(body, init, (per_example, given["loss_target"]))
    with _jax.named_scope("update"):
        delta_w, new_m, new_v = {}, {}, {}
        for n in TWIN_WEIGHTS:
            delta_w[n], new_m[n], new_v[n] = _adamw(weights[n], grad_w[n], given["m_" + n], given["v_" + n])
    return (loss, grad_x, *[grad_w[n] for n in TWIN_WEIGHTS], *[delta_w[n] for n in TWIN_WEIGHTS],
            *[new_m[n] for n in TWIN_WEIGHTS], *[new_v[n] for n in TWIN_WEIGHTS])
```

```python
import functools

import jax
import jax.numpy as jnp
from jax import lax
from jax.experimental import pallas as pl
from jax.experimental.pallas import tpu as pltpu

F32 = jnp.float32
BF16 = jnp.bfloat16
MESH = pl.DeviceIdType.MESH

D_MODEL = 1024
MIX_WIDTH = 1024
PLE_DIM = 256
N_GROUPS = 4
GROUP_DIM = 256
POOL_WINDOWS = (2, 4, 8, 16)
DEPTH = 4
EPS = 1e-6

ADAM_LR = 0.001
ADAM_B1 = 0.9
ADAM_B2 = 0.999
ADAM_EPS = 1e-08
ADAM_WD = 0.01
ADAM_STEP = 10

HALO = 8
TS_MIX = 256
TS_PLE = 512
TK_WGRAD = 2048
TR_EW = 512
VMEM_LIMIT = 56 * 1024 * 1024
SMALL_ROWS = 24

ANY = pl.BlockSpec(memory_space=pl.ANY)


def _sds(shape, dtype):
    return jax.ShapeDtypeStruct(shape, dtype)


def _full(shape):
    nd = len(shape)
    return pl.BlockSpec(shape, lambda *_: (0,) * nd)


def _params(n_axes=1):
    return pltpu.CompilerParams(dimension_semantics=("arbitrary",) * n_axes, vmem_limit_bytes=VMEM_LIMIT)


def _dot(a, b):
    return jnp.dot(a, b, preferred_element_type=F32)


def _dot_nt(a, b):
    return lax.dot_general(a, b, (((1,), (1,)), ((), ())), preferred_element_type=F32)


def _dot_tn(a, b):
    return lax.dot_general(a, b, (((0,), (0,)), ((), ())), preferred_element_type=F32)


def _sigmoid(z):
    return 1.0 / (1.0 + jnp.exp(-z))


def _shift_down(x, k, tail):
    rolled = pltpu.roll(x, k, 0)
    rt = tail if k % HALO == 0 else pltpu.roll(tail, k % HALO, 0)
    row = lax.broadcasted_iota(jnp.int32, rt.shape, 0)
    head = jnp.where(row < k, rt, rolled[0:HALO])
    return jnp.concatenate([head, rolled[HALO:]], axis=0)


def _shift_up(x, k, head_next):
    n = x.shape[0]
    rolled = pltpu.roll(x, n - k, 0)
    rh = head_next if k % HALO == 0 else pltpu.roll(head_next, HALO - k % HALO, 0)
    row = lax.broadcasted_iota(jnp.int32, rh.shape, 0)
    tail = jnp.where(row >= HALO - k, rh, rolled[n - HALO:n])
    return jnp.concatenate([rolled[:n - HALO], tail], axis=0)


def _inv_counts(tile, ts):
    t = tile * ts + lax.broadcasted_iota(jnp.int32, (ts, 1), 0)
    return [1.0 / jnp.minimum(t + 1, w).astype(F32) for w in POOL_WINDOWS]


def _pool_fwd(u, carry, tile, ts):
    inv = _inv_counts(tile, ts)
    outs = []
    for g, w in enumerate(POOL_WINDOWS):
        cols = slice(g * GROUP_DIM, (g + 1) * GROUP_DIM)
        s = u[:, cols]
        level, k = 0, 1
        while k < w:
            tail = carry[level, :, cols]
            carry[level, :, cols] = s[ts - HALO:ts]
            s = s + _shift_down(s, k, tail)
            level, k = level + 1, k * 2
        outs.append(s * inv[g])
    return jnp.concatenate(outs, axis=1)


def _pool_bwd(dd, carry, tile, ts):
    inv = _inv_counts(tile, ts)
    outs = []
    for g, w in enumerate(POOL_WINDOWS):
        cols = slice(g * GROUP_DIM, (g + 1) * GROUP_DIM)
        q = dd[:, cols] * inv[g]
        level, k = 0, 1
        while k < w:
            head = carry[level, :, cols]
            carry[level, :, cols] = q[0:HALO]
            q = q + _shift_up(q, k, head)
            level, k = level + 1, k * 2
        outs.append(q)
    return jnp.concatenate(outs, axis=1)


def _load_resident(pairs, sems):
    copies = [pltpu.make_async_copy(src, dst, sems.at[n]) for n, (src, dst) in enumerate(pairs)]
    for cp in copies:
        cp.start()
    for cp in copies:
        cp.wait()


def _rms(h):
    r = lax.rsqrt(jnp.mean(h * h, axis=-1, keepdims=True) + EPS)
    return h * r, r


def _rms_bwd(dhn, xn, r, gain):
    dgain = jnp.sum(dhn * xn, axis=0, keepdims=True)
    dxn = dhn * gain
    dh = r * (dxn - xn * jnp.mean(dxn * xn, axis=-1, keepdims=True))
    return dh, dgain


def _fwd_mix_a(h, gain, conv_w, w_in, w_out, layer):
    s, d = h.shape
    e = MIX_WIDTH
    ts = min(TS_MIX, s)
    nt = s // ts

    def body(h_ref, gain_ref, cw_ref, win_hbm, wout_hbm, h1_ref, proj_ref, hn_ref, m_ref,
             win_v, wout_v, carry, sems):
        i = pl.program_id(0)

        @pl.when(i == 0)
        def _():
            _load_resident([(win_hbm.at[layer], win_v), (wout_hbm.at[layer], wout_v)], sems)
            carry[...] = jnp.zeros_like(carry)

        hh = h_ref[...]
        xn, _ = _rms(hh)
        hnb = (xn * gain_ref[...]).astype(BF16)
        hn_ref[...] = hnb
        b = _dot(hnb, win_v[0])
        c = _dot(hnb, win_v[1])
        v = _dot(hnb, win_v[2])
        z = _dot(hnb, win_v[3])
        proj_ref[:, 0 * e:1 * e] = b.astype(BF16)
        proj_ref[:, 1 * e:2 * e] = c.astype(BF16)
        proj_ref[:, 2 * e:3 * e] = v.astype(BF16)
        proj_ref[:, 3 * e:4 * e] = z.astype(BF16)
        cv = c * v
        tail = carry[...]
        carry[...] = cv[ts - HALO:ts]
        conv = cw_ref[0:1, :] * _shift_down(cv, 2, tail) + cw_ref[1:2, :] * _shift_down(cv, 1, tail) + cw_ref[2:3, :] * cv
        mb = ((z * _sigmoid(z)) * (b * conv)).astype(BF16)
        m_ref[...] = mb
        h1_ref[...] = hh + _dot(mb, wout_v[...])

    row = lambda width: pl.BlockSpec((ts, width), lambda i: (i, 0))
    return pl.pallas_call(
        body, name=f"fwd_mix_a{layer}", grid=(nt,),
        in_specs=[row(d), _full((1, d)), _full((8, e)), ANY, ANY],
        out_specs=[row(d), row(4 * e), row(d), row(e)],
        out_shape=[_sds((s, d), F32), _sds((s, 4 * e), BF16), _sds((s, d), BF16), _sds((s, e), BF16)],
        scratch_shapes=[pltpu.VMEM((4, d, e), BF16), pltpu.VMEM((e, d), BF16), pltpu.VMEM((HALO, e), F32),
                        pltpu.SemaphoreType.DMA((2,))],
        compiler_params=_params(),
    )(h, gain, conv_w, w_in, w_out)


def _load_grp(wgrp_hbm, wgrp_v, layer, sems, first_sem):
    rows = GROUP_DIM // 4
    copies = []
    for k in range(4):
        for g in range(N_GROUPS):
            copies.append(pltpu.make_async_copy(wgrp_hbm.at[layer, k, g], wgrp_v.at[g, pl.ds(k * rows, rows), :],
                                                sems.at[first_sem + 4 * k + g]))
    for cp in copies:
        cp.start()
    for cp in copies:
        cp.wait()


def _fwd_mix_b(h, gain, scale, w_in, w_grp, w_out, layer):
    s, d = h.shape
    e = MIX_WIDTH
    ts = min(TS_MIX, s)
    nt = s // ts

    def body(h_ref, gain_ref, scale_ref, win_hbm, wgrp_hbm, wout_hbm, h1_ref, z_ref, mx_ref, dd_ref, hn_ref, m_ref,
             win_v, wgrp_v, wout_v, carry, sems):
        i = pl.program_id(0)

        @pl.when(i == 0)
        def _():
            _load_resident([(win_hbm.at[layer], win_v), (wout_hbm.at[layer], wout_v)], sems)
            _load_grp(wgrp_hbm, wgrp_v, layer, sems, 2)
            carry[...] = jnp.zeros_like(carry)

        hh = h_ref[...]
        xn, _ = _rms(hh)
        hnb = (xn * gain_ref[...]).astype(BF16)
        hn_ref[...] = hnb
        u = jnp.concatenate([_dot(hnb, win_v[0]), _dot(hnb, win_v[1])], axis=1)
        z = jnp.concatenate([_dot(hnb, win_v[2]), _dot(hnb, win_v[3])], axis=1)
        z_ref[...] = z.astype(BF16)
        diff = (_pool_fwd(u, carry, i, ts) - u).astype(BF16)
        dd_ref[...] = diff
        mx = jnp.concatenate(
            [_dot(diff[:, g * GROUP_DIM:(g + 1) * GROUP_DIM], wgrp_v[g]) for g in range(N_GROUPS)], axis=1)
        mx_ref[...] = mx.astype(BF16)
        mb = ((z * _sigmoid(z)) * (mx * scale_ref[...])).astype(BF16)
        m_ref[...] = mb
        h1_ref[...] = hh + _dot(mb, wout_v[...])

    row = lambda width: pl.BlockSpec((ts, width), lambda i: (i, 0))
    return pl.pallas_call(
        body, name=f"fwd_mix_b{layer}", grid=(nt,),
        in_specs=[row(d), _full((1, d)), _full((1, e)), ANY, ANY, ANY],
        out_specs=[row(d), row(e), row(e), row(e), row(d), row(e)],
        out_shape=[_sds((s, d), F32)] + [_sds((s, e), BF16)] * 3 + [_sds((s, d), BF16), _sds((s, e), BF16)],
        scratch_shapes=[pltpu.VMEM((4, d, e // 2), BF16), pltpu.VMEM((N_GROUPS, GROUP_DIM, GROUP_DIM), BF16),
                        pltpu.VMEM((e, d), BF16), pltpu.VMEM((4, HALO, e), F32), pltpu.SemaphoreType.DMA((18,))],
        compiler_params=_params(),
    )(h, gain, scale, w_in, w_grp, w_out)


def _fwd_ple(h1, p, gain, w_gate, w_proj, layer):
    s, d = h1.shape
    pd = p.shape[-1]
    ts = min(TS_PLE, s)
    nt = s // ts

    def body(h1_ref, p_ref, gain_ref, wg_hbm, wp_hbm, h2_ref, gate_ref, pe_ref, hp_ref, pb_ref, wg_v, wp_v, sems):
        @pl.when(pl.program_id(0) == 0)
        def _():
            _load_resident([(wg_hbm.at[layer], wg_v), (wp_hbm.at[layer], wp_v)], sems)

        hh = h1_ref[...]
        xn, _ = _rms(hh)
        hpb = (xn * gain_ref[...]).astype(BF16)
        hp_ref[...] = hpb
        gate = _sigmoid(_dot(hpb, wg_v[...]))
        pb = p_ref[...].astype(BF16)
        pb_ref[...] = pb
        pe = jnp.concatenate([_dot(pb, wp_v[k]) for k in range(4)], axis=1)
        gate_ref[...] = gate.astype(BF16)
        pe_ref[...] = pe.astype(BF16)
        h2_ref[...] = hh + gate * pe

    row = lambda width: pl.BlockSpec((ts, width), lambda i: (i, 0))
    return pl.pallas_call(
        body, name=f"fwd_ple{layer}", grid=(nt,),
        in_specs=[row(d), pl.BlockSpec((None, ts, pd), lambda i: (layer, i, 0)), _full((1, d)), ANY, ANY],
        out_specs=[row(d), row(d), row(d), row(d), row(pd)],
        out_shape=[_sds((s, d), F32)] + [_sds((s, d), BF16)] * 3 + [_sds((s, pd), BF16)],
        scratch_shapes=[pltpu.VMEM((d, d), BF16), pltpu.VMEM((4, pd, d // 4), BF16), pltpu.SemaphoreType.DMA((2,))],
        compiler_params=_params(),
    )(h1, p, gain, w_gate, w_proj)


def _loss_head(h, target, gain):
    s, d = h.shape
    ts = min(TS_PLE, s)
    nt = s // ts

    def body(h_ref, t_ref, gain_ref, dh_ref, loss_ref, dgain_ref):
        @pl.when(pl.program_id(0) == 0)
        def _():
            loss_ref[...] = jnp.zeros_like(loss_ref)
            dgain_ref[...] = jnp.zeros_like(dgain_ref)

        xn, r = _rms(h_ref[...])
        err = xn * gain_ref[...] - t_ref[...]
        part = 0.5 * jnp.sum(jnp.mean(err * err, axis=-1, keepdims=True), axis=0, keepdims=True)
        loss_ref[...] += jnp.broadcast_to(part, loss_ref.shape)
        dh, dgain = _rms_bwd(err * (1.0 / d), xn, r, gain_ref[...])
        dh_ref[...] = dh
        dgain_ref[...] += dgain

    row = pl.BlockSpec((ts, d), lambda i: (i, 0))
    return pl.pallas_call(
        body, name="loss_head", grid=(nt,),
        in_specs=[row, row, _full((1, d))],
        out_specs=[row, _full((8, 128)), _full((1, d))],
        out_shape=[_sds((s, d), F32), _sds((8, 128), F32), _sds((1, d), F32)],
        compiler_params=_params(),
    )(h, target, gain)


def _bwd_ple(dh2, h1, gate, pe, gain, w_gate, layer):
    s, d = dh2.shape
    ts = min(TS_PLE, s)
    nt = s // ts

    def body(dh2_ref, h1_ref, gate_ref, pe_ref, gain_ref, wg_hbm, dh1_ref, dh1b_ref, da_ref, dpe_ref, dgain_ref,
             wg_v, sems):
        @pl.when(pl.program_id(0) == 0)
        def _():
            _load_resident([(wg_hbm.at[layer], wg_v)], sems)
            dgain_ref[...] = jnp.zeros_like(dgain_ref)

        g2 = dh2_ref[...]
        gate_f = gate_ref[...].astype(F32)
        dpe_ref[...] = (g2 * gate_f).astype(BF16)
        dab = ((g2 * pe_ref[...].astype(F32)) * (gate_f * (1.0 - gate_f))).astype(BF16)
        da_ref[...] = dab
        dhp = _dot_nt(dab, wg_v[...])
        xn, r = _rms(h1_ref[...])
        dh, dgain = _rms_bwd(dhp, xn, r, gain_ref[...])
        dh1 = g2 + dh
        dh1_ref[...] = dh1
        dh1b_ref[...] = dh1.astype(BF16)
        dgain_ref[...] += dgain

    row = pl.BlockSpec((ts, d), lambda i: (i, 0))
    return pl.pallas_call(
        body, name=f"bwd_ple{layer}", grid=(nt,),
        in_specs=[row, row, row, row, _full((1, d)), ANY],
        out_specs=[row, row, row, row, _full((1, d))],
        out_shape=[_sds((s, d), F32), _sds((s, d), BF16), _sds((s, d), BF16), _sds((s, d), BF16), _sds((1, d), F32)],
        scratch_shapes=[pltpu.VMEM((d, d), BF16), pltpu.SemaphoreType.DMA((1,))],
        compiler_params=_params(),
    )(dh2, h1, gate, pe, gain, w_gate)


def _bwd_mix_a(dh1, dh1b, h, proj, gain, conv_w, w_in, w_out, layer):
    s, d = dh1.shape
    e = MIX_WIDTH
    ts = min(TS_MIX, s)
    nt = s // ts
    hb = 16
    per = ts // hb

    def body(dh1_ref, dh1b_ref, h_ref, proj_ref, ch_ref, vh_ref, gain_ref, cw_ref, win_hbm, wout_hbm,
             dh_ref, dproj_ref, dcw_ref, dgain_ref, win_v, wout_v, carry, sems):
        i = pl.program_id(0)
        tile = nt - 1 - i

        @pl.when(i == 0)
        def _():
            _load_resident([(win_hbm.at[layer], win_v), (wout_hbm.at[layer], wout_v)], sems)
            carry[...] = jnp.zeros_like(carry)
            dcw_ref[...] = jnp.zeros_like(dcw_ref)
            dgain_ref[...] = jnp.zeros_like(dgain_ref)

        b = proj_ref[:, 0 * e:1 * e].astype(F32)
        c = proj_ref[:, 1 * e:2 * e].astype(F32)
        v = proj_ref[:, 2 * e:3 * e].astype(F32)
        z = proj_ref[:, 3 * e:4 * e].astype(F32)
        cv = c * v
        prev = (ch_ref[...].astype(F32) * vh_ref[...].astype(F32))[hb - HALO:hb]
        tail = jnp.where(tile > 0, prev, jnp.zeros_like(prev))
        cv1 = _shift_down(cv, 1, tail)
        cv2 = _shift_down(cv, 2, tail)
        conv = cw_ref[0:1, :] * cv2 + cw_ref[1:2, :] * cv1 + cw_ref[2:3, :] * cv
        sig = _sigmoid(z)
        sz = z * sig
        y = b * conv
        dm = _dot_nt(dh1b_ref[...], wout_v[...])
        dz = (dm * y) * (sig * (1.0 + z * (1.0 - sig)))
        dy = dm * sz
        db = dy * conv
        dconv = dy * b
        head = carry[...]
        carry[...] = dconv[0:HALO]
        dcv = cw_ref[2:3, :] * dconv + cw_ref[1:2, :] * _shift_up(dconv, 1, head) + cw_ref[0:1, :] * _shift_up(dconv, 2, head)
        dcw_ref[0:1, :] += jnp.sum(dconv * cv2, axis=0, keepdims=True)
        dcw_ref[1:2, :] += jnp.sum(dconv * cv1, axis=0, keepdims=True)
        dcw_ref[2:3, :] += jnp.sum(dconv * cv, axis=0, keepdims=True)
        dbb = db.astype(BF16)
        dcb = (dcv * v).astype(BF16)
        dvb = (dcv * c).astype(BF16)
        dzb = dz.astype(BF16)
        dproj_ref[:, 0 * e:1 * e] = dbb
        dproj_ref[:, 1 * e:2 * e] = dcb
        dproj_ref[:, 2 * e:3 * e] = dvb
        dproj_ref[:, 3 * e:4 * e] = dzb
        dhn = _dot_nt(dbb, win_v[0]) + _dot_nt(dcb, win_v[1]) + _dot_nt(dvb, win_v[2]) + _dot_nt(dzb, win_v[3])
        xn, r = _rms(h_ref[...])
        dh, dgain = _rms_bwd(dhn, xn, r, gain_ref[...])
        dh_ref[...] = dh1_ref[...] + dh
        dgain_ref[...] += dgain

    row = lambda width: pl.BlockSpec((ts, width), lambda i: (nt - 1 - i, 0))
    halo = lambda col: pl.BlockSpec((hb, e), lambda i: (jnp.maximum((nt - 1 - i) * per - 1, 0), col))
    return pl.pallas_call(
        body, name=f"bwd_mix_a{layer}", grid=(nt,),
        in_specs=[row(d), row(d), row(d), row(4 * e), halo(1), halo(2), _full((1, d)), _full((8, e)), ANY, ANY],
        out_specs=[row(d), row(4 * e), _full((8, e)), _full((1, d))],
        out_shape=[_sds((s, d), F32), _sds((s, 4 * e), BF16), _sds((8, e), F32), _sds((1, d), F32)],
        scratch_shapes=[pltpu.VMEM((4, d, e), BF16), pltpu.VMEM((e, d), BF16), pltpu.VMEM((HALO, e), F32),
                        pltpu.SemaphoreType.DMA((2,))],
        compiler_params=_params(),
    )(dh1, dh1b, h, proj, proj, proj, gain, conv_w, w_in, w_out)


def _bwd_mix_b(dh1, dh1b, h, z, mx, gain, scale, w_in, w_grp, w_out, layer):
    s, d = dh1.shape
    e = MIX_WIDTH
    ts = min(TS_MIX, s)
    nt = s // ts

    def body(dh1_ref, dh1b_ref, h_ref, z_ref, mx_ref, gain_ref, scale_ref, win_hbm, wgrp_hbm, wout_hbm,
             dh_ref, dproj_ref, dmx_ref, dscale_ref, dgain_ref, win_v, wgrp_v, wout_v, carry, sems):
        i = pl.program_id(0)
        tile = nt - 1 - i

        @pl.when(i == 0)
        def _():
            _load_resident([(win_hbm.at[layer], win_v), (wout_hbm.at[layer], wout_v)], sems)
            _load_grp(wgrp_hbm, wgrp_v, layer, sems, 2)
            carry[...] = jnp.zeros_like(carry)
            dscale_ref[...] = jnp.zeros_like(dscale_ref)
            dgain_ref[...] = jnp.zeros_like(dgain_ref)

        zf = z_ref[...].astype(F32)
        mxf = mx_ref[...].astype(F32)
        sig = _sigmoid(zf)
        dm = _dot_nt(dh1b_ref[...], wout_v[...])
        dz = (dm * (mxf * scale_ref[...])) * (sig * (1.0 + zf * (1.0 - sig)))
        dmixed = dm * (zf * sig)
        dscale_ref[...] += jnp.sum(dmixed * mxf, axis=0, keepdims=True)
        dmxb = (dmixed * scale_ref[...]).astype(BF16)
        dmx_ref[...] = dmxb
        ddiff = jnp.concatenate(
            [_dot_nt(dmxb[:, g * GROUP_DIM:(g + 1) * GROUP_DIM], wgrp_v[g]) for g in range(N_GROUPS)], axis=1)
        dub = (_pool_bwd(ddiff, carry, tile, ts) - ddiff).astype(BF16)
        dzb = dz.astype(BF16)
        dproj_ref[:, 0:e] = dub
        dproj_ref[:, e:2 * e] = dzb
        half = e // 2
        dhn = (_dot_nt(dub[:, 0:half], win_v[0]) + _dot_nt(dub[:, half:e], win_v[1])
               + _dot_nt(dzb[:, 0:half], win_v[2]) + _dot_nt(dzb[:, half:e], win_v[3]))
        xn, r = _rms(h_ref[...])
        dh, dgain = _rms_bwd(dhn, xn, r, gain_ref[...])
        dh_ref[...] = dh1_ref[...] + dh
        dgain_ref[...] += dgain

    row = lambda width: pl.BlockSpec((ts, width), lambda i: (nt - 1 - i, 0))
    return pl.pallas_call(
        body, name=f"bwd_mix_b{layer}", grid=(nt,),
        in_specs=[row(d), row(d), row(d), row(e), row(e), _full((1, d)), _full((1, e)), ANY, ANY, ANY],
        out_specs=[row(d), row(2 * e), row(e), _full((1, e)), _full((1, d))],
        out_shape=[_sds((s, d), F32), _sds((s, 2 * e), BF16), _sds((s, e), BF16), _sds((1, e), F32), _sds((1, d), F32)],
        scratch_shapes=[pltpu.VMEM((4, d, e // 2), BF16), pltpu.VMEM((N_GROUPS, GROUP_DIM, GROUP_DIM), BF16),
                        pltpu.VMEM((e, d), BF16), pltpu.VMEM((4, HALO, e), F32), pltpu.SemaphoreType.DMA((18,))],
        compiler_params=_params(),
    )(dh1, dh1b, h, z, mx, gain, scale, w_in, w_grp, w_out)


def _wgrad(a, b, n_blocks, a_blocked, layer, n_layers, prev, name, split_rows=None):
    s = a.shape[0]
    m = a.shape[1] // n_blocks if a_blocked else a.shape[1]
    n = b.shape[1] // n_blocks
    tk = min(TK_WGRAD, s)
    nk = s // tk

    def body(*refs):
        a_ref, b_ref = refs[0], refs[1]
        out_ref, outb_ref, acc = refs[-3], refs[-2], refs[-1]
        k = pl.program_id(1)

        @pl.when(k == 0)
        def _():
            acc[...] = jnp.zeros_like(acc)

        acc[...] += _dot_tn(a_ref[...], b_ref[...])

        @pl.when(k == nk - 1)
        def _():
            if split_rows is None:
                out_ref[...] = acc[...]
                outb_ref[...] = acc[...].astype(BF16)
            else:
                for q in range(m // split_rows):
                    out_ref[q] = acc[q * split_rows:(q + 1) * split_rows, :]
                    outb_ref[q] = acc[q * split_rows:(q + 1) * split_rows, :].astype(BF16)

    a_spec = pl.BlockSpec((tk, m), (lambda j, k: (k, j)) if a_blocked else (lambda j, k: (k, 0)))
    b_spec = pl.BlockSpec((tk, n), lambda j, k: (k, j))
    if split_rows is None:
        shape = (n_layers, n_blocks, m, n)
        o_spec = pl.BlockSpec((None, None, m, n), lambda j, k: (layer, j, 0, 0))
    else:
        shape = (n_layers, m // split_rows, n_blocks, split_rows, n)
        o_spec = pl.BlockSpec((None, m // split_rows, None, split_rows, n), lambda j, k: (layer, 0, j, 0, 0))
    operands, in_specs, aliases = [a, b], [a_spec, b_spec], {}
    if prev is not None:
        operands += list(prev)
        in_specs += [ANY, ANY]
        aliases = {2: 0, 3: 1}
    return pl.pallas_call(
        body, name=name, grid=(n_blocks, nk),
        in_specs=in_specs, out_specs=[o_spec, o_spec],
        out_shape=[_sds(shape, F32), _sds(shape, BF16)],
        scratch_shapes=[pltpu.VMEM((m, n), F32)],
        input_output_aliases=aliases,
        compiler_params=_params(2),
    )(*operands)


def _place():
    x, y, c = lax.axis_index("x"), lax.axis_index("y"), lax.axis_index("c")
    chips = [(1 - x, y), (x, 1 - y), (1 - x, 1 - y)]
    return x, y, c, chips


def _gather_weights(shards, small):
    ni = len(shards)

    def body(*refs):
        srcs, small_src = refs[:ni], refs[ni]
        dsts, small_dst = refs[ni + 1:2 * ni + 1], refs[2 * ni + 1]
        send_sems, recv_sems, local_sems = refs[2 * ni + 2:]
        x, y, c, chips = _place()
        me = 2 * x + y
        sibling = (x, y, 1 - c)

        def remote(src, dst, sem, to):
            return pltpu.make_async_remote_copy(src_ref=src, dst_ref=dst, send_sem=send_sems.at[sem],
                                                recv_sem=recv_sems.at[sem], device_id=to, device_id_type=MESH)

        local = [pltpu.make_async_copy(srcs[t], dsts[t].at[:, me], local_sems.at[t]) for t in range(ni)]
        local.append(pltpu.make_async_copy(small_src, small_dst.at[me], local_sems.at[ni]))
        for cp in local:
            cp.start()
        sends = []
        for j, (cx, cy) in enumerate(chips):
            sends.append(remote(small_src, small_dst.at[me], 6 * ni + j, (cx, cy, c)))
            for t in range(ni):
                sends.append(remote(srcs[t].at[:, c], dsts[t].at[:, me, c], 6 * t + j, (cx, cy, c)))
        for cp in sends:
            cp.start()
        for j, (cx, cy) in enumerate(chips):
            it = 2 * cx + cy
            remote(small_src, small_dst.at[it], 6 * ni + j, sibling).wait_recv()
            for t in range(ni):
                landed = dsts[t].at[:, it, c]
                remote(landed, landed, 6 * t + j, sibling).wait_recv()
                fwd = remote(landed, landed, 6 * t + 3 + j, sibling)
                fwd.start()
                sends.append(fwd)
        for j, (cx, cy) in enumerate(chips):
            it = 2 * cx + cy
            for t in range(ni):
                other = dsts[t].at[:, it, 1 - c]
                remote(other, other, 6 * t + 3 + j, sibling).wait_recv()
        for cp in sends:
            cp.wait_send()
        for cp in local:
            cp.wait()

    out_shape = [_sds((a.shape[0], 4) + a.shape[1:], a.dtype) for a in shards] + [_sds((4,) + small.shape, small.dtype)]
    n_sems = 6 * ni + 3
    return pl.pallas_call(
        body, name="gather_weights",
        in_specs=[ANY] * (ni + 1), out_specs=[ANY] * (ni + 1), out_shape=out_shape,
        scratch_shapes=[pltpu.SemaphoreType.DMA((n_sems,)), pltpu.SemaphoreType.DMA((n_sems,)),
                        pltpu.SemaphoreType.DMA((ni + 1,))],
    )(*shards, small)


def _pair_exchange(grads_bf16):
    ni = len(grads_bf16)

    def body(*refs):
        srcs, dsts = refs[:ni], refs[ni:2 * ni]
        send_sems, recv_sems = refs[2 * ni:]
        x, y, c, _ = _place()
        copies = [pltpu.make_async_remote_copy(src_ref=srcs[t].at[:, :, 1 - c], dst_ref=dsts[t],
                                               send_sem=send_sems.at[t], recv_sem=recv_sems.at[t],
                                               device_id=(x, y, 1 - c), device_id_type=MESH) for t in range(ni)]
        for cp in copies:
            cp.start()
        for cp in copies:
            cp.wait()

    return pl.pallas_call(
        body, name="pair_exchange",
        in_specs=[ANY] * ni, out_specs=[ANY] * ni,
        out_shape=[_sds(g.shape[:2] + g.shape[3:], g.dtype) for g in grads_bf16],
        scratch_shapes=[pltpu.SemaphoreType.DMA((ni,)), pltpu.SemaphoreType.DMA((ni,))],
    )(*grads_bf16)


def _ici_exchange(pair_sums):
    ni = len(pair_sums)

    def body(*refs):
        srcs, dsts = refs[:ni], refs[ni:2 * ni]
        send_sems, recv_sems = refs[2 * ni:]
        x, y, c, chips = _place()
        copies = []
        for j, (cx, cy) in enumerate(chips):
            for t in range(ni):
                copies.append(pltpu.make_async_remote_copy(
                    src_ref=srcs[t].at[:, 2 * cx + cy], dst_ref=dsts[t].at[j],
                    send_sem=send_sems.at[3 * t + j], recv_sem=recv_sems.at[3 * t + j],
                    device_id=(cx, cy, c), device_id_type=MESH))
        for cp in copies:
            cp.start()
        for cp in copies:
            cp.wait()

    return pl.pallas_call(
        body, name="ici_exchange",
        in_specs=[ANY] * ni, out_specs=[ANY] * ni,
        out_shape=[_sds((3, g.shape[0]) + g.shape[2:], g.dtype) for g in pair_sums],
        scratch_shapes=[pltpu.SemaphoreType.DMA((3 * ni,)), pltpu.SemaphoreType.DMA((3 * ni,))],
    )(*pair_sums)


def _final_exchange(halves):
    ni = len(halves)

    def body(*refs):
        srcs, dsts = refs[:ni], refs[ni:2 * ni]
        send_sems, recv_sems, local_sems = refs[2 * ni:]
        x, y, c, _ = _place()
        local = [pltpu.make_async_copy(srcs[t], dsts[t].at[:, c], local_sems.at[t]) for t in range(ni)]
        for cp in local:
            cp.start()
        copies = [pltpu.make_async_remote_copy(src_ref=srcs[t], dst_ref=dsts[t].at[:, c],
                                               send_sem=send_sems.at[t], recv_sem=recv_sems.at[t],
                                               device_id=(x, y, 1 - c), device_id_type=MESH) for t in range(ni)]
        for cp in copies:
            cp.start()
        for t in range(ni):
            copies[t].wait_send()
            pltpu.make_async_remote_copy(src_ref=srcs[t], dst_ref=dsts[t].at[:, 1 - c],
                                         send_sem=send_sems.at[t], recv_sem=recv_sems.at[t],
                                         device_id=(x, y, 1 - c), device_id_type=MESH).wait_recv()
        for cp in local:
            cp.wait()

    return pl.pallas_call(
        body, name="final_exchange",
        in_specs=[ANY] * ni, out_specs=[ANY] * ni,
        out_shape=[_sds((h.shape[0], 2) + h.shape[1:], h.dtype) for h in halves],
        scratch_shapes=[pltpu.SemaphoreType.DMA((ni,)), pltpu.SemaphoreType.DMA((ni,)), pltpu.SemaphoreType.DMA((ni,))],
    )(*halves)


def _small_allreduce(pack):
    rows, d = pack.shape
    flips = [(fx, fy, fc) for fx in (0, 1) for fy in (0, 1) for fc in (0, 1)][1:]

    def body(pack_ref, out_ref, land, send_sems, recv_sems):
        x, y, c, _ = _place()
        me = 4 * x + 2 * y + c
        land[pl.ds(me, 1)] = pack_ref[...][None]
        peers = [(1 - x if fx else x, 1 - y if fy else y, 1 - c if fc else c) for fx, fy, fc in flips]
        copies = []
        for r, peer in enumerate(peers):
            copies.append(pltpu.make_async_remote_copy(
                src_ref=pack_ref, dst_ref=land.at[me], send_sem=send_sems.at[r], recv_sem=recv_sems.at[r],
                device_id=peer, device_id_type=MESH))
        for cp in copies:
            cp.start()
        for r, (px, py, pc) in enumerate(peers):
            pltpu.make_async_remote_copy(
                src_ref=pack_ref, dst_ref=land.at[4 * px + 2 * py + pc], send_sem=send_sems.at[r],
                recv_sem=recv_sems.at[r], device_id=(px, py, pc), device_id_type=MESH).wait_recv()
        for cp in copies:
            cp.wait_send()
        total = land[0]
        for dev in range(1, 8):
            total = total + land[dev]
        out_ref[...] = total

    vmem = pl.BlockSpec(memory_space=pltpu.VMEM)
    return pl.pallas_call(
        body, name="small_allreduce",
        in_specs=[vmem], out_specs=vmem, out_shape=_sds((rows, d), F32),
        scratch_shapes=[pltpu.VMEM((8, rows, d), F32), pltpu.SemaphoreType.DMA((7,)), pltpu.SemaphoreType.DMA((7,))],
    )(pack)


def _ew_rows(rows):
    return min(TR_EW, rows)


def _pair_sum(grad, sibling_rows, place, name):
    n, _, _, rh, cols = grad.shape
    tr = _ew_rows(rh)

    def body(place_ref, g_ref, s_ref, out_ref):
        out_ref[...] = (g_ref[...] + s_ref[...].astype(F32)).astype(BF16)

    grid_spec = pltpu.PrefetchScalarGridSpec(
        num_scalar_prefetch=1, grid=(n, 4, rh // tr),
        in_specs=[pl.BlockSpec((None, None, None, tr, cols), lambda j, k, r, pos: (j, k, pos[1], r, 0)),
                  pl.BlockSpec((None, None, tr, cols), lambda j, k, r, pos: (j, k, r, 0))],
        out_specs=pl.BlockSpec((None, None, tr, cols), lambda j, k, r, pos: (j, k, r, 0)))
    return pl.pallas_call(body, name=name, grid_spec=grid_spec, out_shape=_sds((n, 4, rh, cols), BF16),
                          compiler_params=_params(3))(place, grad, sibling_rows)


def _final_sum(grad, sibling_rows, landed, place, name):
    n, _, _, rh, cols = grad.shape
    tr = _ew_rows(rh)

    def body(place_ref, g_ref, s_ref, l_ref, out_ref):
        total = g_ref[...] + s_ref[...].astype(F32)
        for j in range(3):
            total = total + l_ref[j].astype(F32)
        out_ref[...] = total

    grid_spec = pltpu.PrefetchScalarGridSpec(
        num_scalar_prefetch=1, grid=(n, rh // tr),
        in_specs=[pl.BlockSpec((None, None, None, tr, cols), lambda j, r, pos: (j, pos[0], pos[1], r, 0)),
                  pl.BlockSpec((None, None, tr, cols), lambda j, r, pos: (j, pos[0], r, 0)),
                  pl.BlockSpec((3, None, tr, cols), lambda j, r, pos: (0, j, r, 0))],
        out_specs=pl.BlockSpec((None, tr, cols), lambda j, r, pos: (j, r, 0)))
    return pl.pallas_call(body, name=name, grid_spec=grid_spec, out_shape=_sds((n, rh, cols), F32),
                          compiler_params=_params(2))(place, grad, sibling_rows, landed)


def _adamw(g, w, m, v, name):
    rows, cols = g.shape
    tr = _ew_rows(rows)

    def body(g_ref, w_ref, m_ref, v_ref, delta_ref, nm_ref, nv_ref):
        gg = g_ref[...]
        nm = ADAM_B1 * m_ref[...] + (1.0 - ADAM_B1) * gg
        nv = ADAM_B2 * v_ref[...] + (1.0 - ADAM_B2) * (gg * gg)
        m_hat = nm / (1.0 - ADAM_B1 ** ADAM_STEP)
        v_hat = nv / (1.0 - ADAM_B2 ** ADAM_STEP)
        delta_ref[...] = -ADAM_LR * (m_hat / (jnp.sqrt(v_hat) + ADAM_EPS) + ADAM_WD * w_ref[...])
        nm_ref[...] = nm
        nv_ref[...] = nv

    spec = pl.BlockSpec((tr, cols), lambda i: (i, 0))
    return pl.pallas_call(
        body, name=name, grid=(rows // tr,), in_specs=[spec] * 4, out_specs=[spec] * 3,
        out_shape=[_sds((rows, cols), F32)] * 3, compiler_params=_params(),
    )(g, w, m, v)


def kernel(x, p, norm_mix, a_w_in, a_w_conv, a_w_out, b_w_in, b_w_grp, b_scale, b_w_out, ple_norm, ple_w_gate, ple_w_proj, final_norm, loss_target, m_norm_mix, m_a_w_in, m_a_w_conv, m_a_w_out, m_b_w_in, m_b_w_grp, m_b_scale, m_b_w_out, m_ple_norm, m_ple_w_gate, m_ple_w_proj, m_final_norm, v_norm_mix, v_a_w_in, v_a_w_conv, v_a_w_out, v_b_w_in, v_b_w_grp, v_b_scale, v_b_w_out, v_ple_norm, v_ple_w_gate, v_ple_w_proj, v_final_norm):
    d, e = D_MODEL, MIX_WIDTH
    s = x.shape[1]
    cx, cy, cc = lax.axis_index("x"), lax.axis_index("y"), lax.axis_index("c")
    chip = 2 * cx + cy
    place = jnp.stack([chip, cc]).astype(jnp.int32)

    def as_2d(w):
        return {
            "a_w_in": lambda a: a, "a_w_out": lambda a: a, "b_w_in": lambda a: a,
            "b_w_grp": lambda a: a.reshape(a.shape[0], N_GROUPS * (GROUP_DIM // 4), GROUP_DIM),
            "b_w_out": lambda a: a, "ple_w_gate": lambda a: a, "ple_w_proj": lambda a: a,
        }[w]

    names = ["a_w_in", "a_w_out", "b_w_in", "b_w_grp", "b_w_out", "ple_w_gate", "ple_w_proj"]
    weights = dict(a_w_in=a_w_in, a_w_out=a_w_out, b_w_in=b_w_in, b_w_grp=b_w_grp, b_w_out=b_w_out,
                   ple_w_gate=ple_w_gate, ple_w_proj=ple_w_proj)
    moms = dict(a_w_in=m_a_w_in, a_w_out=m_a_w_out, b_w_in=m_b_w_in, b_w_grp=m_b_w_grp, b_w_out=m_b_w_out,
                ple_w_gate=m_ple_w_gate, ple_w_proj=m_ple_w_proj)
    vars_ = dict(a_w_in=v_a_w_in, a_w_out=v_a_w_out, b_w_in=v_b_w_in, b_w_grp=v_b_w_grp, b_w_out=v_b_w_out,
                 ple_w_gate=v_ple_w_gate, ple_w_proj=v_ple_w_proj)

    def halves(a):
        return a.reshape(a.shape[0], 2, a.shape[1] // 2, a.shape[2])

    shards = [halves(as_2d(nm)(weights[nm]).astype(BF16)) for nm in names]
    small = jnp.concatenate([a_w_conv.reshape(6, e // 4), b_scale], axis=0)
    *full, small_full = _gather_weights(shards, small)
    full = dict(zip(names, full))
    small_full = small_full.transpose(1, 0, 2).reshape(8, e)
    conv_w = [jnp.concatenate([small_full[3 * j:3 * j + 3], jnp.zeros((5, e), F32)], axis=0) for j in range(2)]
    scale_w = [small_full[6 + j:7 + j] for j in range(2)]

    w_in_a = full["a_w_in"].reshape(2, 4, d, e)
    w_out_a = full["a_w_out"].reshape(2, e, d)
    w_in_b = full["b_w_in"].reshape(2, 4, d, e // 2)
    w_grp_b = full["b_w_grp"].reshape(2, 4, N_GROUPS, GROUP_DIM // 4, GROUP_DIM)
    w_out_b = full["b_w_out"].reshape(2, e, d)
    w_gate = full["ple_w_gate"].reshape(DEPTH, d, d)
    w_proj = full["ple_w_proj"].reshape(DEPTH, 4, PLE_DIM, d // 4)

    p3 = p.reshape(DEPTH, s, PLE_DIM)
    mix_gain = [norm_mix[i:i + 1] for i in range(DEPTH)]
    ple_gain = [ple_norm[i:i + 1] for i in range(DEPTH)]

    h = x.reshape(s, d)
    saved = []
    for i in range(DEPTH):
        j = i // 2
        if i % 2 == 0:
            h1, proj, hn, mb = _fwd_mix_a(h, mix_gain[i], conv_w[j], w_in_a, w_out_a, j)
            mix = dict(proj=proj)
        else:
            h1, zb, mx, diff, hn, mb = _fwd_mix_b(h, mix_gain[i], scale_w[j], w_in_b, w_grp_b, w_out_b, j)
            mix = dict(z=zb, mx=mx, diff=diff)
        h2, gate, pe, hp, pb = _fwd_ple(h1, p3, ple_gain[i], w_gate, w_proj, i)
        saved.append(dict(h=h, h1=h1, hn=hn, m=mb, gate=gate, pe=pe, hp=hp, pb=pb, **mix))
        h = h2

    dh, loss_part, d_final = _loss_head(h, loss_target.reshape(s, d), final_norm.reshape(1, d))
    loss = lax.psum(loss_part[0, 0], ("x", "y", "c"))

    grads = {nm: None for nm in names}
    d_mix_gain, d_ple_gain = [None] * DEPTH, [None] * DEPTH
    d_conv, d_scale = [None] * 2, [None] * 2
    for i in reversed(range(DEPTH)):
        j = i // 2
        sv = saved[i]
        dh1, dh1b, da, dpe, d_ple_gain[i] = _bwd_ple(dh, sv["h1"], sv["gate"], sv["pe"], ple_gain[i], w_gate, i)
        grads["ple_w_gate"] = _wgrad(sv["hp"], da, 1, False, i, DEPTH, grads["ple_w_gate"], f"wgrad_gate{i}")
        grads["ple_w_proj"] = _wgrad(sv["pb"], dpe, 4, False, i, DEPTH, grads["ple_w_proj"], f"wgrad_proj{i}")
        if i % 2 == 0:
            dh, dproj, d_conv[j], d_mix_gain[i] = _bwd_mix_a(dh1, dh1b, sv["h"], sv["proj"], mix_gain[i], conv_w[j],
                                                            w_in_a, w_out_a, j)
            grads["a_w_out"] = _wgrad(sv["m"], dh1b, 1, False, j, 2, grads["a_w_out"], f"wgrad_a_out{j}")
            grads["a_w_in"] = _wgrad(sv["hn"], dproj, 4, False, j, 2, grads["a_w_in"], f"wgrad_a_in{j}")
        else:
            dh, dproj, dmx, d_scale[j], d_mix_gain[i] = _bwd_mix_b(dh1, dh1b, sv["h"], sv["z"], sv["mx"], mix_gain[i],
                                                                  scale_w[j], w_in_b, w_grp_b, w_out_b, j)
            grads["b_w_out"] = _wgrad(sv["m"], dh1b, 1, False, j, 2, grads["b_w_out"], f"wgrad_b_out{j}")
            grads["b_w_grp"] = _wgrad(sv["diff"], dmx, 4, True, j, 2, grads["b_w_grp"], f"wgrad_b_grp{j}",
                                      split_rows=GROUP_DIM // 4)
            grads["b_w_in"] = _wgrad(sv["hn"], dproj, 4, False, j, 2, grads["b_w_in"], f"wgrad_b_in{j}")
    grad_x = dh.reshape(1, s, d)

    def owner_view(nm, g):
        n = g.shape[0]
        rows, cols = as_2d(nm)(weights[nm]).shape[1:]
        return g.reshape(n, 4, 2, rows // 2, cols)

    g32 = [owner_view(nm, grads[nm][0]) for nm in names]
    g16 = [owner_view(nm, grads[nm][1]) for nm in names]
    from_sibling = _pair_exchange(g16)
    pair_sums = [_pair_sum(g32[t], from_sibling[t], place, f"pair_sum_{nm}") for t, nm in enumerate(names)]
    landed = _ici_exchange(pair_sums)
    mine = [_final_sum(g32[t], from_sibling[t], landed[t], place, f"final_sum_{nm}") for t, nm in enumerate(names)]
    summed = _final_exchange(mine)

    out_grad, out_delta, out_m, out_v = {}, {}, {}, {}
    for t, nm in enumerate(names):
        shape = weights[nm].shape
        w2 = as_2d(nm)(weights[nm])
        flat = (w2.shape[0] * w2.shape[1], w2.shape[2])
        g2 = summed[t].reshape(flat)
        delta, new_m, new_v = _adamw(g2, w2.reshape(flat), as_2d(nm)(moms[nm]).reshape(flat),
                                     as_2d(nm)(vars_[nm]).reshape(flat), f"adamw_{nm}")
        out_grad[nm], out_delta[nm] = g2.reshape(shape), delta.reshape(shape)
        out_m[nm], out_v[nm] = new_m.reshape(shape), new_v.reshape(shape)

    pack = jnp.concatenate(
        d_mix_gain + d_ple_gain + [d_final] + [d_conv[0][0:3], d_conv[1][0:3]] + d_scale
        + [jnp.zeros((SMALL_ROWS - 17, d), F32)], axis=0)
    total = _small_allreduce(pack)
    rep_rows = 16
    rep = lambda a, b_, c_: jnp.concatenate([a, b_, c_.reshape(1, d), jnp.zeros((rep_rows - 9, d), F32)], axis=0)
    rep_delta, rep_m, rep_v = _adamw(
        jnp.concatenate([total[0:9], jnp.zeros((rep_rows - 9, d), F32)], axis=0),
        rep(norm_mix, ple_norm, final_norm), rep(m_norm_mix, m_ple_norm, m_final_norm),
        rep(v_norm_mix, v_ple_norm, v_final_norm), "adamw_gains")
    mine_cols = lax.dynamic_slice_in_dim(total[9:17], chip * (e // 4), e // 4, axis=1)
    col = lambda a, b_: jnp.concatenate([a.reshape(6, e // 4), b_], axis=0)
    col_delta, col_m, col_v = _adamw(mine_cols, col(a_w_conv, b_scale), col(m_a_w_conv, m_b_scale),
                                     col(v_a_w_conv, v_b_scale), "adamw_cols")

    def unpack(rep_a, col_a):
        return dict(norm_mix=rep_a[0:4], ple_norm=rep_a[4:8], final_norm=rep_a[8],
                    a_w_conv=col_a[0:6].reshape(2, 3, e // 4), b_scale=col_a[6:8])

    small_out = [unpack(total, mine_cols), unpack(rep_delta, col_delta), unpack(rep_m, col_m), unpack(rep_v, col_v)]
    order = ["norm_mix", "a_w_in", "a_w_conv", "a_w_out", "b_w_in", "b_w_grp", "b_scale", "b_w_out", "ple_norm",
             "ple_w_gate", "ple_w_proj", "final_norm"]
    outs = [loss, grad_x]
    for big, small_d in zip([out_grad, out_delta, out_m, out_v], small_out):
        outs += [big[nm] if nm in big else small_d[nm] for nm in order]
    return tuple(outs)
```

```python
import jax
import jax.numpy as jnp
from jax import lax
from jax.experimental import pallas as pl
from jax.experimental.pallas import tpu as pltpu

F32 = jnp.float32
BF16 = jnp.bfloat16
MESH = pl.DeviceIdType.MESH

D_MODEL = 1024
MIX_WIDTH = 1024
PLE_DIM = 256
N_GROUPS = 4
GROUP_DIM = 256
POOL_WINDOWS = (2, 4, 8, 16)
DEPTH = 4
EPS = 1e-6

ADAM_LR = 0.001
ADAM_B1 = 0.9
ADAM_B2 = 0.999
ADAM_EPS = 1e-08
ADAM_WD = 0.01
ADAM_STEP = 10

HALO = 8
TS_MIX = 256
TS_PLE = 512
TK_WGRAD = 2048
TR_EW = 512
VMEM_LIMIT = 56 * 1024 * 1024
SMALL_ROWS = 24
MIDDLE_STEPS_BEFORE_END = 3

ANY = pl.BlockSpec(memory_space=pl.ANY)


def _sds(shape, dtype):
    return jax.ShapeDtypeStruct(shape, dtype)


def _full(shape):
    nd = len(shape)
    return pl.BlockSpec(shape, lambda *_: (0,) * nd)


def _params(n_axes=1):
    return pltpu.CompilerParams(dimension_semantics=("arbitrary",) * n_axes, vmem_limit_bytes=VMEM_LIMIT)


def _dot(a, b):
    return jnp.dot(a, b, preferred_element_type=F32)


def _dot_nt(a, b):
    return lax.dot_general(a, b, (((1,), (1,)), ((), ())), preferred_element_type=F32)


def _dot_tn(a, b):
    return lax.dot_general(a, b, (((0,), (0,)), ((), ())), preferred_element_type=F32)


def _sigmoid(z):
    return 1.0 / (1.0 + jnp.exp(-z))


def _shift_down(x, k, tail):
    rolled = pltpu.roll(x, k, 0)
    rt = tail if k % HALO == 0 else pltpu.roll(tail, k % HALO, 0)
    row = lax.broadcasted_iota(jnp.int32, rt.shape, 0)
    head = jnp.where(row < k, rt, rolled[0:HALO])
    return jnp.concatenate([head, rolled[HALO:]], axis=0)


def _shift_up(x, k, head_next):
    n = x.shape[0]
    rolled = pltpu.roll(x, n - k, 0)
    rh = head_next if k % HALO == 0 else pltpu.roll(head_next, HALO - k % HALO, 0)
    row = lax.broadcasted_iota(jnp.int32, rh.shape, 0)
    tail = jnp.where(row >= HALO - k, rh, rolled[n - HALO:n])
    return jnp.concatenate([rolled[:n - HALO], tail], axis=0)


def _inv_counts(tile, ts):
    t = tile * ts + lax.broadcasted_iota(jnp.int32, (ts, 1), 0)
    return [1.0 / jnp.minimum(t + 1, w).astype(F32) for w in POOL_WINDOWS]


def _pool_fwd(u, carry, tile, ts):
    inv = _inv_counts(tile, ts)
    outs = []
    for g, w in enumerate(POOL_WINDOWS):
        cols = slice(g * GROUP_DIM, (g + 1) * GROUP_DIM)
        s = u[:, cols]
        level, k = 0, 1
        while k < w:
            tail = carry[level, :, cols]
            carry[level, :, cols] = s[ts - HALO:ts]
            s = s + _shift_down(s, k, tail)
            level, k = level + 1, k * 2
        outs.append(s * inv[g])
    return jnp.concatenate(outs, axis=1)


def _pool_bwd(dd, carry, tile, ts):
    inv = _inv_counts(tile, ts)
    outs = []
    for g, w in enumerate(POOL_WINDOWS):
        cols = slice(g * GROUP_DIM, (g + 1) * GROUP_DIM)
        q = dd[:, cols] * inv[g]
        level, k = 0, 1
        while k < w:
            head = carry[level, :, cols]
            carry[level, :, cols] = q[0:HALO]
            q = q + _shift_up(q, k, head)
            level, k = level + 1, k * 2
        outs.append(q)
    return jnp.concatenate(outs, axis=1)


def _load_resident(pairs, sems):
    copies = [pltpu.make_async_copy(src, dst, sems.at[n]) for n, (src, dst) in enumerate(pairs)]
    for cp in copies:
        cp.start()
    for cp in copies:
        cp.wait()


def _grp_pairs(wgrp_hbm, wgrp_v):
    rows = GROUP_DIM // 4
    return [(wgrp_hbm.at[k, g], wgrp_v.at[g, pl.ds(k * rows, rows), :]) for k in range(4) for g in range(N_GROUPS)]


def _rms(h):
    r = lax.rsqrt(jnp.mean(h * h, axis=-1, keepdims=True) + EPS)
    return h * r, r


def _rms_bwd(dhn, xn, r, gain):
    dgain = jnp.sum(dhn * xn, axis=0, keepdims=True)
    dxn = dhn * gain
    dh = r * (dxn - xn * jnp.mean(dxn * xn, axis=-1, keepdims=True))
    return dh, dgain


class _Rider:
    def __init__(self, inputs, out_shapes, n_sems, start, finish, middle=None, aliases=None):
        self.inputs, self.out_shapes, self.n_sems = list(inputs), list(out_shapes), n_sems
        self.start, self.middle, self.finish = start, middle, finish
        self.aliases = dict(aliases or {})


def _place():
    x, y, c = lax.axis_index("x"), lax.axis_index("y"), lax.axis_index("c")
    chips = [(1 - x, y), (x, 1 - y), (1 - x, 1 - y)]
    return x, y, c, chips


def _remote(src, dst, send_sems, recv_sems, sem, to):
    return pltpu.make_async_remote_copy(src_ref=src, dst_ref=dst, send_sem=send_sems.at[sem], recv_sem=recv_sems.at[sem],
                                        device_id=to, device_id_type=MESH)


def _gather_rider(shards):
    ni = len(shards)

    def first_hops(rin, rout, send, recv, x, y, c, chips):
        me = 2 * x + y
        return [_remote(rin[t].at[c], rout[t].at[me, c], send, recv, 7 * t + j, (cx, cy, c))
                for j, (cx, cy) in enumerate(chips) for t in range(ni)]

    def passes(rout, send, recv, x, y, c, chips):
        out = []
        for j, (cx, cy) in enumerate(chips):
            for t in range(ni):
                landed = rout[t].at[2 * cx + cy, c]
                out.append((_remote(landed, landed, send, recv, 7 * t + j, (x, y, 1 - c)),
                            _remote(landed, landed, send, recv, 7 * t + 3 + j, (x, y, 1 - c))))
        return out

    def own(rin, rout, send, recv, x, y, c):
        return [_remote(rin[t], rout[t].at[2 * x + y], send, recv, 7 * t + 6, (x, y, 1 - c)) for t in range(ni)]

    def start(rin, rout, send, recv):
        x, y, c, chips = _place()
        for cp in first_hops(rin, rout, send, recv, x, y, c, chips) + own(rin, rout, send, recv, x, y, c):
            cp.start()

    def middle(rin, rout, send, recv):
        x, y, c, chips = _place()
        for arrival, onward in passes(rout, send, recv, x, y, c, chips):
            arrival.wait_recv()
            onward.start()

    def finish(rin, rout, send, recv):
        x, y, c, chips = _place()
        for j, (cx, cy) in enumerate(chips):
            for t in range(ni):
                other = rout[t].at[2 * cx + cy, 1 - c]
                _remote(other, other, send, recv, 7 * t + 3 + j, (x, y, 1 - c)).wait_recv()
        for cp in own(rin, rout, send, recv, x, y, c):
            cp.wait_recv()
            cp.wait_send()
        for cp in first_hops(rin, rout, send, recv, x, y, c, chips):
            cp.wait_send()
        for _, onward in passes(rout, send, recv, x, y, c, chips):
            onward.wait_send()

    return _Rider(shards, [_sds((4,) + a.shape, a.dtype) for a in shards], 7 * ni, start, finish, middle)


def _pair_rider(grads_bf16):
    ni = len(grads_bf16)

    def copies(rin, rout, send, recv):
        x, y, c, _ = _place()
        return [_remote(rin[t].at[:, 1 - c], rout[t], send, recv, t, (x, y, 1 - c)) for t in range(ni)]

    def start(rin, rout, send, recv):
        for cp in copies(rin, rout, send, recv):
            cp.start()

    def finish(rin, rout, send, recv):
        for cp in copies(rin, rout, send, recv):
            cp.wait()

    return _Rider(grads_bf16, [_sds(g.shape[:1] + g.shape[2:], g.dtype) for g in grads_bf16], ni, start, finish)


def _ici_rider(pair_sums):
    ni = len(pair_sums)

    def copies(rin, rout, send, recv):
        x, y, c, chips = _place()
        return [_remote(rin[t].at[2 * cx + cy], rout[t].at[j], send, recv, 3 * t + j, (cx, cy, c))
                for j, (cx, cy) in enumerate(chips) for t in range(ni)]

    def start(rin, rout, send, recv):
        for cp in copies(rin, rout, send, recv):
            cp.start()

    def finish(rin, rout, send, recv):
        for cp in copies(rin, rout, send, recv):
            cp.wait()

    return _Rider(pair_sums, [_sds((3,) + g.shape[1:], g.dtype) for g in pair_sums], 3 * ni, start, finish)


def _final_rider(summed, slots):
    ni = len(summed)

    def copies(rout, send, recv):
        x, y, c, _ = _place()
        return [(_remote(rout[t].at[slots[t], c], rout[t].at[slots[t], c], send, recv, t, (x, y, 1 - c)),
                 _remote(rout[t].at[slots[t], 1 - c], rout[t].at[slots[t], 1 - c], send, recv, t, (x, y, 1 - c)))
                for t in range(ni)]

    def start(rin, rout, send, recv):
        for mine, _ in copies(rout, send, recv):
            mine.start()

    def finish(rin, rout, send, recv):
        for mine, theirs in copies(rout, send, recv):
            mine.wait_send()
            theirs.wait_recv()

    return _Rider(summed, [_sds(a.shape, a.dtype) for a in summed], ni, start, finish,
                  aliases={t: t for t in range(ni)})


def _call(body, *, name, grid, in_specs, out_specs, out_shape, scratch_shapes, operands, rider=None):
    if rider is None:
        outs = pl.pallas_call(body, name=name, grid=grid, in_specs=in_specs, out_specs=out_specs, out_shape=out_shape,
                              scratch_shapes=scratch_shapes, compiler_params=_params(len(grid)))(*operands)
        return list(outs), []
    n_in, n_out, n_scr = len(in_specs), len(out_specs), len(scratch_shapes)
    r_in, r_out = len(rider.inputs), len(rider.out_shapes)
    steps = 1
    for g in grid:
        steps *= g
    mid = max(steps - 1 - MIDDLE_STEPS_BEFORE_END, 0)

    def full_body(*refs):
        own_in, rin = refs[:n_in], refs[n_in:n_in + r_in]
        own_out = refs[n_in + r_in:n_in + r_in + n_out]
        rout = refs[n_in + r_in + n_out:n_in + r_in + n_out + r_out]
        own_scr = refs[n_in + r_in + n_out + r_out:n_in + r_in + n_out + r_out + n_scr]
        send, recv = refs[-2], refs[-1]
        step = pl.program_id(0)
        for axis in range(1, len(grid)):
            step = step * grid[axis] + pl.program_id(axis)

        @pl.when(step == 0)
        def _():
            rider.start(rin, rout, send, recv)

        body(*own_in, *own_out, *own_scr)

        if rider.middle is not None:
            @pl.when(step == mid)
            def _():
                rider.middle(rin, rout, send, recv)

        @pl.when(step == steps - 1)
        def _():
            rider.finish(rin, rout, send, recv)

    outs = pl.pallas_call(
        full_body, name=name, grid=grid,
        in_specs=list(in_specs) + [ANY] * r_in, out_specs=list(out_specs) + [ANY] * r_out,
        out_shape=list(out_shape) + rider.out_shapes,
        scratch_shapes=list(scratch_shapes) + [pltpu.SemaphoreType.DMA((rider.n_sems,)), pltpu.SemaphoreType.DMA((rider.n_sems,))],
        input_output_aliases={n_in + a: n_out + b for a, b in rider.aliases.items()},
        compiler_params=_params(len(grid)),
    )(*operands, *rider.inputs)
    return list(outs[:n_out]), list(outs[n_out:])


def _run_rider(rider, name):
    r_in, r_out = len(rider.inputs), len(rider.out_shapes)

    def body(*refs):
        rin, rout, send, recv = refs[:r_in], refs[r_in:r_in + r_out], refs[-2], refs[-1]
        rider.start(rin, rout, send, recv)
        if rider.middle is not None:
            rider.middle(rin, rout, send, recv)
        rider.finish(rin, rout, send, recv)

    outs = pl.pallas_call(
        body, name=name, in_specs=[ANY] * r_in, out_specs=[ANY] * r_out, out_shape=rider.out_shapes,
        scratch_shapes=[pltpu.SemaphoreType.DMA((rider.n_sems,)), pltpu.SemaphoreType.DMA((rider.n_sems,))],
        input_output_aliases=rider.aliases,
    )(*rider.inputs)
    return list(outs)


def _fwd_mix_a(h, gain, conv_w, w_in, w_out, name, rider=None):
    s, d = h.shape
    e = MIX_WIDTH
    ts = min(TS_MIX, s)
    nt = s // ts

    def body(h_ref, gain_ref, cw_ref, win_hbm, wout_hbm, h1_ref, proj_ref, hn_ref, m_ref,
             win_v, wout_v, carry, sems):
        i = pl.program_id(0)

        @pl.when(i == 0)
        def _():
            _load_resident([(win_hbm, win_v), (wout_hbm, wout_v)], sems)
            carry[...] = jnp.zeros_like(carry)

        hh = h_ref[...]
        xn, _ = _rms(hh)
        hnb = (xn * gain_ref[...]).astype(BF16)
        hn_ref[...] = hnb
        b = _dot(hnb, win_v[0])
        c = _dot(hnb, win_v[1])
        v = _dot(hnb, win_v[2])
        z = _dot(hnb, win_v[3])
        proj_ref[:, 0 * e:1 * e] = b.astype(BF16)
        proj_ref[:, 1 * e:2 * e] = c.astype(BF16)
        proj_ref[:, 2 * e:3 * e] = v.astype(BF16)
        proj_ref[:, 3 * e:4 * e] = z.astype(BF16)
        cv = c * v
        tail = carry[...]
        carry[...] = cv[ts - HALO:ts]
        conv = cw_ref[0:1, :] * _shift_down(cv, 2, tail) + cw_ref[1:2, :] * _shift_down(cv, 1, tail) + cw_ref[2:3, :] * cv
        mb = ((z * _sigmoid(z)) * (b * conv)).astype(BF16)
        m_ref[...] = mb
        h1_ref[...] = hh + _dot(mb, wout_v[...])

    row = lambda width: pl.BlockSpec((ts, width), lambda i: (i, 0))
    return _call(
        body, name=name, grid=(nt,),
        in_specs=[row(d), _full((1, d)), _full((8, e)), ANY, ANY],
        out_specs=[row(d), row(4 * e), row(d), row(e)],
        out_shape=[_sds((s, d), F32), _sds((s, 4 * e), BF16), _sds((s, d), BF16), _sds((s, e), BF16)],
        scratch_shapes=[pltpu.VMEM((4, d, e), BF16), pltpu.VMEM((e, d), BF16), pltpu.VMEM((HALO, e), F32),
                        pltpu.SemaphoreType.DMA((2,))],
        operands=[h, gain, conv_w, w_in, w_out], rider=rider)


def _fwd_mix_b(h, gain, scale, w_in, w_grp, w_out, name, rider=None):
    s, d = h.shape
    e = MIX_WIDTH
    ts = min(TS_MIX, s)
    nt = s // ts

    def body(h_ref, gain_ref, scale_ref, win_hbm, wgrp_hbm, wout_hbm, h1_ref, z_ref, mx_ref, dd_ref, hn_ref, m_ref,
             win_v, wgrp_v, wout_v, carry, sems):
        i = pl.program_id(0)

        @pl.when(i == 0)
        def _():
            _load_resident([(win_hbm, win_v), (wout_hbm, wout_v)] + _grp_pairs(wgrp_hbm, wgrp_v), sems)
            carry[...] = jnp.zeros_like(carry)

        hh = h_ref[...]
        xn, _ = _rms(hh)
        hnb = (xn * gain_ref[...]).astype(BF16)
        hn_ref[...] = hnb
        u = jnp.concatenate([_dot(hnb, win_v[0]), _dot(hnb, win_v[1])], axis=1)
        z = jnp.concatenate([_dot(hnb, win_v[2]), _dot(hnb, win_v[3])], axis=1)
        z_ref[...] = z.astype(BF16)
        diff = (_pool_fwd(u, carry, i, ts) - u).astype(BF16)
        dd_ref[...] = diff
        mx = jnp.concatenate(
            [_dot(diff[:, g * GROUP_DIM:(g + 1) * GROUP_DIM], wgrp_v[g]) for g in range(N_GROUPS)], axis=1)
        mx_ref[...] = mx.astype(BF16)
        mb = ((z * _sigmoid(z)) * (mx * scale_ref[...])).astype(BF16)
        m_ref[...] = mb
        h1_ref[...] = hh + _dot(mb, wout_v[...])

    row = lambda width: pl.BlockSpec((ts, width), lambda i: (i, 0))
    return _call(
        body, name=name, grid=(nt,),
        in_specs=[row(d), _full((1, d)), _full((1, e)), ANY, ANY, ANY],
        out_specs=[row(d), row(e), row(e), row(e), row(d), row(e)],
        out_shape=[_sds((s, d), F32)] + [_sds((s, e), BF16)] * 3 + [_sds((s, d), BF16), _sds((s, e), BF16)],
        scratch_shapes=[pltpu.VMEM((4, d, e // 2), BF16), pltpu.VMEM((N_GROUPS, GROUP_DIM, GROUP_DIM), BF16),
                        pltpu.VMEM((e, d), BF16), pltpu.VMEM((4, HALO, e), F32), pltpu.SemaphoreType.DMA((18,))],
        operands=[h, gain, scale, w_in, w_grp, w_out], rider=rider)


def _fwd_ple(h1, p, gain, w_gate, w_proj, layer, rider=None):
    s, d = h1.shape
    pd = p.shape[-1]
    ts = min(TS_PLE, s)
    nt = s // ts

    def body(h1_ref, p_ref, gain_ref, wg_hbm, wp_hbm, h2_ref, gate_ref, pe_ref, hp_ref, pb_ref, wg_v, wp_v, sems):
        @pl.when(pl.program_id(0) == 0)
        def _():
            _load_resident([(wg_hbm, wg_v), (wp_hbm, wp_v)], sems)

        hh = h1_ref[...]
        xn, _ = _rms(hh)
        hpb = (xn * gain_ref[...]).astype(BF16)
        hp_ref[...] = hpb
        gate = _sigmoid(_dot(hpb, wg_v[...]))
        pb = p_ref[...].astype(BF16)
        pb_ref[...] = pb
        pe = jnp.concatenate([_dot(pb, wp_v[k]) for k in range(4)], axis=1)
        gate_ref[...] = gate.astype(BF16)
        pe_ref[...] = pe.astype(BF16)
        h2_ref[...] = hh + gate * pe

    row = lambda width: pl.BlockSpec((ts, width), lambda i: (i, 0))
    return _call(
        body, name=f"fwd_ple{layer}", grid=(nt,),
        in_specs=[row(d), pl.BlockSpec((None, ts, pd), lambda i: (layer, i, 0)), _full((1, d)), ANY, ANY],
        out_specs=[row(d), row(d), row(d), row(d), row(pd)],
        out_shape=[_sds((s, d), F32)] + [_sds((s, d), BF16)] * 3 + [_sds((s, pd), BF16)],
        scratch_shapes=[pltpu.VMEM((d, d), BF16), pltpu.VMEM((4, pd, d // 4), BF16), pltpu.SemaphoreType.DMA((2,))],
        operands=[h1, p, gain, w_gate, w_proj], rider=rider)


def _loss_head(h, target, gain):
    s, d = h.shape
    ts = min(TS_PLE, s)
    nt = s // ts

    def body(h_ref, t_ref, gain_ref, dh_ref, loss_ref, dgain_ref):
        @pl.when(pl.program_id(0) == 0)
        def _():
            loss_ref[...] = jnp.zeros_like(loss_ref)
            dgain_ref[...] = jnp.zeros_like(dgain_ref)

        xn, r = _rms(h_ref[...])
        err = xn * gain_ref[...] - t_ref[...]
        part = 0.5 * jnp.sum(jnp.mean(err * err, axis=-1, keepdims=True), axis=0, keepdims=True)
        loss_ref[...] += jnp.broadcast_to(part, loss_ref.shape)
        dh, dgain = _rms_bwd(err * (1.0 / d), xn, r, gain_ref[...])
        dh_ref[...] = dh
        dgain_ref[...] += dgain

    row = pl.BlockSpec((ts, d), lambda i: (i, 0))
    outs, _ = _call(
        body, name="loss_head", grid=(nt,),
        in_specs=[row, row, _full((1, d))],
        out_specs=[row, _full((8, 128)), _full((1, d))],
        out_shape=[_sds((s, d), F32), _sds((8, 128), F32), _sds((1, d), F32)],
        scratch_shapes=[], operands=[h, target, gain])
    return outs


def _bwd_ple(dh2, h1, gate, pe, gain, w_gate, layer, rider=None):
    s, d = dh2.shape
    ts = min(TS_PLE, s)
    nt = s // ts

    def body(dh2_ref, h1_ref, gate_ref, pe_ref, gain_ref, wg_hbm, dh1_ref, dh1b_ref, da_ref, dpe_ref, dgain_ref,
             wg_v, sems):
        @pl.when(pl.program_id(0) == 0)
        def _():
            _load_resident([(wg_hbm, wg_v)], sems)
            dgain_ref[...] = jnp.zeros_like(dgain_ref)

        g2 = dh2_ref[...]
        gate_f = gate_ref[...].astype(F32)
        dpe_ref[...] = (g2 * gate_f).astype(BF16)
        dab = ((g2 * pe_ref[...].astype(F32)) * (gate_f * (1.0 - gate_f))).astype(BF16)
        da_ref[...] = dab
        dhp = _dot_nt(dab, wg_v[...])
        xn, r = _rms(h1_ref[...])
        dh, dgain = _rms_bwd(dhp, xn, r, gain_ref[...])
        dh1 = g2 + dh
        dh1_ref[...] = dh1
        dh1b_ref[...] = dh1.astype(BF16)
        dgain_ref[...] += dgain

    row = pl.BlockSpec((ts, d), lambda i: (i, 0))
    return _call(
        body, name=f"bwd_ple{layer}", grid=(nt,),
        in_specs=[row, row, row, row, _full((1, d)), ANY],
        out_specs=[row, row, row, row, _full((1, d))],
        out_shape=[_sds((s, d), F32), _sds((s, d), BF16), _sds((s, d), BF16), _sds((s, d), BF16), _sds((1, d), F32)],
        scratch_shapes=[pltpu.VMEM((d, d), BF16), pltpu.SemaphoreType.DMA((1,))],
        operands=[dh2, h1, gate, pe, gain, w_gate], rider=rider)


def _bwd_mix_a(dh1, dh1b, h, proj, gain, conv_w, w_in, w_out, name, rider=None):
    s, d = dh1.shape
    e = MIX_WIDTH
    ts = min(TS_MIX, s)
    nt = s // ts
    hb = 16
    per = ts // hb

    def body(dh1_ref, dh1b_ref, h_ref, proj_ref, ch_ref, vh_ref, gain_ref, cw_ref, win_hbm, wout_hbm,
             dh_ref, dproj_ref, dcw_ref, dgain_ref, win_v, wout_v, carry, sems):
        i = pl.program_id(0)
        tile = nt - 1 - i

        @pl.when(i == 0)
        def _():
            _load_resident([(win_hbm, win_v), (wout_hbm, wout_v)], sems)
            carry[...] = jnp.zeros_like(carry)
            dcw_ref[...] = jnp.zeros_like(dcw_ref)
            dgain_ref[...] = jnp.zeros_like(dgain_ref)

        b = proj_ref[:, 0 * e:1 * e].astype(F32)
        c = proj_ref[:, 1 * e:2 * e].astype(F32)
        v = proj_ref[:, 2 * e:3 * e].astype(F32)
        z = proj_ref[:, 3 * e:4 * e].astype(F32)
        cv = c * v
        prev = (ch_ref[...].astype(F32) * vh_ref[...].astype(F32))[hb - HALO:hb]
        tail = jnp.where(tile > 0, prev, jnp.zeros_like(prev))
        cv1 = _shift_down(cv, 1, tail)
        cv2 = _shift_down(cv, 2, tail)
        conv = cw_ref[0:1, :] * cv2 + cw_ref[1:2, :] * cv1 + cw_ref[2:3, :] * cv
        sig = _sigmoid(z)
        sz = z * sig
        y = b * conv
        dm = _dot_nt(dh1b_ref[...], wout_v[...])
        dz = (dm * y) * (sig * (1.0 + z * (1.0 - sig)))
        dy = dm * sz
        db = dy * conv
        dconv = dy * b
        head = carry[...]
        carry[...] = dconv[0:HALO]
        dcv = cw_ref[2:3, :] * dconv + cw_ref[1:2, :] * _shift_up(dconv, 1, head) + cw_ref[0:1, :] * _shift_up(dconv, 2, head)
        dcw_ref[0:1, :] += jnp.sum(dconv * cv2, axis=0, keepdims=True)
        dcw_ref[1:2, :] += jnp.sum(dconv * cv1, axis=0, keepdims=True)
        dcw_ref[2:3, :] += jnp.sum(dconv * cv, axis=0, keepdims=True)
        dbb = db.astype(BF16)
        dcb = (dcv * v).astype(BF16)
        dvb = (dcv * c).astype(BF16)
        dzb = dz.astype(BF16)
        dproj_ref[:, 0 * e:1 * e] = dbb
        dproj_ref[:, 1 * e:2 * e] = dcb
        dproj_ref[:, 2 * e:3 * e] = dvb
        dproj_ref[:, 3 * e:4 * e] = dzb
        dhn = _dot_nt(dbb, win_v[0]) + _dot_nt(dcb, win_v[1]) + _dot_nt(dvb, win_v[2]) + _dot_nt(dzb, win_v[3])
        xn, r = _rms(h_ref[...])
        dh, dgain = _rms_bwd(dhn, xn, r, gain_ref[...])
        dh_ref[...] = dh1_ref[...] + dh
        dgain_ref[...] += dgain

    row = lambda width: pl.BlockSpec((ts, width), lambda i: (nt - 1 - i, 0))
    halo = lambda col: pl.BlockSpec((hb, e), lambda i: (jnp.maximum((nt - 1 - i) * per - 1, 0), col))
    return _call(
        body, name=name, grid=(nt,),
        in_specs=[row(d), row(d), row(d), row(4 * e), halo(1), halo(2), _full((1, d)), _full((8, e)), ANY, ANY],
        out_specs=[row(d), row(4 * e), _full((8, e)), _full((1, d))],
        out_shape=[_sds((s, d), F32), _sds((s, 4 * e), BF16), _sds((8, e), F32), _sds((1, d), F32)],
        scratch_shapes=[pltpu.VMEM((4, d, e), BF16), pltpu.VMEM((e, d), BF16), pltpu.VMEM((HALO, e), F32),
                        pltpu.SemaphoreType.DMA((2,))],
        operands=[dh1, dh1b, h, proj, proj, proj, gain, conv_w, w_in, w_out], rider=rider)


def _bwd_mix_b(dh1, dh1b, h, z, mx, gain, scale, w_in, w_grp, w_out, name, rider=None):
    s, d = dh1.shape
    e = MIX_WIDTH
    ts = min(TS_MIX, s)
    nt = s // ts

    def body(dh1_ref, dh1b_ref, h_ref, z_ref, mx_ref, gain_ref, scale_ref, win_hbm, wgrp_hbm, wout_hbm,
             dh_ref, dproj_ref, dmx_ref, dscale_ref, dgain_ref, win_v, wgrp_v, wout_v, carry, sems):
        i = pl.program_id(0)
        tile = nt - 1 - i

        @pl.when(i == 0)
        def _():
            _load_resident([(win_hbm, win_v), (wout_hbm, wout_v)] + _grp_pairs(wgrp_hbm, wgrp_v), sems)
            carry[...] = jnp.zeros_like(carry)
            dscale_ref[...] = jnp.zeros_like(dscale_ref)
            dgain_ref[...] = jnp.zeros_like(dgain_ref)

        zf = z_ref[...].astype(F32)
        mxf = mx_ref[...].astype(F32)
        sig = _sigmoid(zf)
        dm = _dot_nt(dh1b_ref[...], wout_v[...])
        dz = (dm * (mxf * scale_ref[...])) * (sig * (1.0 + zf * (1.0 - sig)))
        dmixed = dm * (zf * sig)
        dscale_ref[...] += jnp.sum(dmixed * mxf, axis=0, keepdims=True)
        dmxb = (dmixed * scale_ref[...]).astype(BF16)
        dmx_ref[...] = dmxb
        ddiff = jnp.concatenate(
            [_dot_nt(dmxb[:, g * GROUP_DIM:(g + 1) * GROUP_DIM], wgrp_v[g]) for g in range(N_GROUPS)], axis=1)
        dub = (_pool_bwd(ddiff, carry, tile, ts) - ddiff).astype(BF16)
        dzb = dz.astype(BF16)
        dproj_ref[:, 0:e] = dub
        dproj_ref[:, e:2 * e] = dzb
        half = e // 2
        dhn = (_dot_nt(dub[:, 0:half], win_v[0]) + _dot_nt(dub[:, half:e], win_v[1])
               + _dot_nt(dzb[:, 0:half], win_v[2]) + _dot_nt(dzb[:, half:e], win_v[3]))
        xn, r = _rms(h_ref[...])
        dh, dgain = _rms_bwd(dhn, xn, r, gain_ref[...])
        dh_ref[...] = dh1_ref[...] + dh
        dgain_ref[...] += dgain

    row = lambda width: pl.BlockSpec((ts, width), lambda i: (nt - 1 - i, 0))
    return _call(
        body, name=name, grid=(nt,),
        in_specs=[row(d), row(d), row(d), row(e), row(e), _full((1, d)), _full((1, e)), ANY, ANY, ANY],
        out_specs=[row(d), row(2 * e), row(e), _full((1, e)), _full((1, d))],
        out_shape=[_sds((s, d), F32), _sds((s, 2 * e), BF16), _sds((s, e), BF16), _sds((1, e), F32), _sds((1, d), F32)],
        scratch_shapes=[pltpu.VMEM((4, d, e // 2), BF16), pltpu.VMEM((N_GROUPS, GROUP_DIM, GROUP_DIM), BF16),
                        pltpu.VMEM((e, d), BF16), pltpu.VMEM((4, HALO, e), F32), pltpu.SemaphoreType.DMA((18,))],
        operands=[dh1, dh1b, h, z, mx, gain, scale, w_in, w_grp, w_out], rider=rider)


def _wgrad(a, b, n_blocks, a_blocked, name, split_rows=None, rider=None):
    s = a.shape[0]
    m = a.shape[1] // n_blocks if a_blocked else a.shape[1]
    n = b.shape[1] // n_blocks
    tk = min(TK_WGRAD, s)
    nk = s // tk

    def body(a_ref, b_ref, out_ref, outb_ref, acc):
        k = pl.program_id(1)

        @pl.when(k == 0)
        def _():
            acc[...] = jnp.zeros_like(acc)

        acc[...] += _dot_tn(a_ref[...], b_ref[...])

        @pl.when(k == nk - 1)
        def _():
            if split_rows is None:
                out_ref[...] = acc[...]
                outb_ref[...] = acc[...].astype(BF16)
            else:
                for q in range(m // split_rows):
                    out_ref[q] = acc[q * split_rows:(q + 1) * split_rows, :]
                    outb_ref[q] = acc[q * split_rows:(q + 1) * split_rows, :].astype(BF16)

    a_spec = pl.BlockSpec((tk, m), (lambda j, k: (k, j)) if a_blocked else (lambda j, k: (k, 0)))
    b_spec = pl.BlockSpec((tk, n), lambda j, k: (k, j))
    if split_rows is None:
        shape = (n_blocks, m, n)
        o_spec = pl.BlockSpec((None, m, n), lambda j, k: (j, 0, 0))
    else:
        shape = (m // split_rows, n_blocks, split_rows, n)
        o_spec = pl.BlockSpec((m // split_rows, None, split_rows, n), lambda j, k: (0, j, 0, 0))
    return _call(body, name=name, grid=(n_blocks, nk), in_specs=[a_spec, b_spec], out_specs=[o_spec, o_spec],
                 out_shape=[_sds(shape, F32), _sds(shape, BF16)], scratch_shapes=[pltpu.VMEM((m, n), F32)],
                 operands=[a, b], rider=rider)


def _first_gather(shards, small):
    rider = _gather_rider(shards)
    ni = len(shards)

    def body(*refs):
        rin, small_src = refs[:ni], refs[ni]
        rout, small_dst = refs[ni + 1:2 * ni + 1], refs[2 * ni + 1]
        send, recv, ssend, srecv = refs[2 * ni + 2:]
        x, y, c, chips = _place()
        me = 2 * x + y
        peers = [(cx, cy, c) for cx, cy in chips] + [(x, y, 1 - c)]
        vec = [_remote(small_src, small_dst.at[me], ssend, srecv, j, to) for j, to in enumerate(peers)]
        for cp in vec:
            cp.start()
        rider.start(rin, rout, send, recv)
        rider.middle(rin, rout, send, recv)
        rider.finish(rin, rout, send, recv)
        for j, (px, py, _) in enumerate(peers):
            _remote(small_src, small_dst.at[2 * px + py], ssend, srecv, j, peers[j]).wait_recv()
        for cp in vec:
            cp.wait_send()

    outs = pl.pallas_call(
        body, name="first_gather", in_specs=[ANY] * (ni + 1), out_specs=[ANY] * (ni + 1),
        out_shape=rider.out_shapes + [_sds((4,) + small.shape, small.dtype)],
        scratch_shapes=[pltpu.SemaphoreType.DMA((rider.n_sems,)), pltpu.SemaphoreType.DMA((rider.n_sems,)),
                        pltpu.SemaphoreType.DMA((4,)), pltpu.SemaphoreType.DMA((4,))],
    )(*shards, small)
    return list(outs[:ni]), outs[ni]


def _small_allreduce(pack):
    rows, d = pack.shape
    flips = [(fx, fy, fc) for fx in (0, 1) for fy in (0, 1) for fc in (0, 1)][1:]

    def body(pack_ref, out_ref, land, send_sems, recv_sems):
        x, y, c, _ = _place()
        me = 4 * x + 2 * y + c
        land[pl.ds(me, 1)] = pack_ref[...][None]
        peers = [(1 - x if fx else x, 1 - y if fy else y, 1 - c if fc else c) for fx, fy, fc in flips]
        copies = [_remote(pack_ref, land.at[me], send_sems, recv_sems, r, peer) for r, peer in enumerate(peers)]
        for cp in copies:
            cp.start()
        for r, (px, py, pc) in enumerate(peers):
            _remote(pack_ref, land.at[4 * px + 2 * py + pc], send_sems, recv_sems, r, (px, py, pc)).wait_recv()
        for cp in copies:
            cp.wait_send()
        total = land[0]
        for dev in range(1, 8):
            total = total + land[dev]
        out_ref[...] = total

    vmem = pl.BlockSpec(memory_space=pltpu.VMEM)
    return pl.pallas_call(
        body, name="small_allreduce",
        in_specs=[vmem], out_specs=vmem, out_shape=_sds((rows, d), F32),
        scratch_shapes=[pltpu.VMEM((8, rows, d), F32), pltpu.SemaphoreType.DMA((7,)), pltpu.SemaphoreType.DMA((7,))],
    )(pack)


def _ew_rows(rows):
    return min(TR_EW, rows)


def _pair_sum(grad, sibling_rows, place, name):
    _, _, rh, cols = grad.shape
    tr = _ew_rows(rh)

    def body(place_ref, g_ref, s_ref, out_ref):
        out_ref[...] = (g_ref[...] + s_ref[...].astype(F32)).astype(BF16)

    grid_spec = pltpu.PrefetchScalarGridSpec(
        num_scalar_prefetch=1, grid=(4, rh // tr),
        in_specs=[pl.BlockSpec((None, None, tr, cols), lambda k, r, pos: (k, pos[1], r, 0)),
                  pl.BlockSpec((None, tr, cols), lambda k, r, pos: (k, r, 0))],
        out_specs=pl.BlockSpec((None, tr, cols), lambda k, r, pos: (k, r, 0)))
    return pl.pallas_call(body, name=name, grid_spec=grid_spec, out_shape=_sds((4, rh, cols), BF16),
                          compiler_params=_params(2))(place, grad, sibling_rows)


def _final_sum(grad, sibling_rows, landed, place, stack, slot, n_slots, name):
    _, _, rh, cols = grad.shape
    tr = _ew_rows(rh)

    def body(place_ref, g_ref, s_ref, l_ref, *rest):
        out_ref = rest[-1]
        total = g_ref[...] + s_ref[...].astype(F32)
        for j in range(3):
            total = total + l_ref[j].astype(F32)
        out_ref[...] = total

    in_specs = [pl.BlockSpec((None, None, tr, cols), lambda r, pos: (pos[0], pos[1], r, 0)),
                pl.BlockSpec((None, tr, cols), lambda r, pos: (pos[0], r, 0)),
                pl.BlockSpec((3, tr, cols), lambda r, pos: (0, r, 0))]
    operands = [place, grad, sibling_rows, landed]
    aliases = {}
    if stack is not None:
        in_specs.append(ANY)
        operands.append(stack)
        aliases = {4: 0}
    grid_spec = pltpu.PrefetchScalarGridSpec(
        num_scalar_prefetch=1, grid=(rh // tr,), in_specs=in_specs,
        out_specs=pl.BlockSpec((None, None, tr, cols), lambda r, pos: (slot, pos[1], r, 0)))
    return pl.pallas_call(body, name=name, grid_spec=grid_spec, out_shape=_sds((n_slots, 2, rh, cols), F32),
                          input_output_aliases=aliases, compiler_params=_params(1))(*operands)


def _adamw(g, w, m, v, name):
    rows, cols = g.shape
    tr = _ew_rows(rows)

    def body(g_ref, w_ref, m_ref, v_ref, delta_ref, nm_ref, nv_ref):
        gg = g_ref[...]
        nm = ADAM_B1 * m_ref[...] + (1.0 - ADAM_B1) * gg
        nv = ADAM_B2 * v_ref[...] + (1.0 - ADAM_B2) * (gg * gg)
        m_hat = nm / (1.0 - ADAM_B1 ** ADAM_STEP)
        v_hat = nv / (1.0 - ADAM_B2 ** ADAM_STEP)
        delta_ref[...] = -ADAM_LR * (m_hat / (jnp.sqrt(v_hat) + ADAM_EPS) + ADAM_WD * w_ref[...])
        nm_ref[...] = nm
        nv_ref[...] = nv

    spec = pl.BlockSpec((tr, cols), lambda i: (i, 0))
    return pl.pallas_call(
        body, name=name, grid=(rows // tr,), in_specs=[spec] * 4, out_specs=[spec] * 3,
        out_shape=[_sds((rows, cols), F32)] * 3, compiler_params=_params(),
    )(g, w, m, v)


BIG = ["a_w_in", "a_w_out", "b_w_in", "b_w_grp", "b_w_out", "ple_w_gate", "ple_w_proj"]


def _as_2d(name, a):
    if name == "b_w_grp":
        return a.reshape(a.shape[0], N_GROUPS * (GROUP_DIM // 4), GROUP_DIM)
    return a


def _mixer_tensors(layer):
    j = layer // 2
    return [("a_w_in", j), ("a_w_out", j)] if layer % 2 == 0 else [("b_w_in", j), ("b_w_grp", j), ("b_w_out", j)]


def _ple_tensors(layer):
    return [("ple_w_gate", layer), ("ple_w_proj", layer)]


def kernel(x, p, norm_mix, a_w_in, a_w_conv, a_w_out, b_w_in, b_w_grp, b_scale, b_w_out, ple_norm, ple_w_gate, ple_w_proj, final_norm, loss_target, m_norm_mix, m_a_w_in, m_a_w_conv, m_a_w_out, m_b_w_in, m_b_w_grp, m_b_scale, m_b_w_out, m_ple_norm, m_ple_w_gate, m_ple_w_proj, m_final_norm, v_norm_mix, v_a_w_in, v_a_w_conv, v_a_w_out, v_b_w_in, v_b_w_grp, v_b_scale, v_b_w_out, v_ple_norm, v_ple_w_gate, v_ple_w_proj, v_final_norm):
    d, e = D_MODEL, MIX_WIDTH
    s = x.shape[1]
    cx, cy, cc = lax.axis_index("x"), lax.axis_index("y"), lax.axis_index("c")
    chip = 2 * cx + cy
    place = jnp.stack([chip, cc]).astype(jnp.int32)

    weights = dict(a_w_in=a_w_in, a_w_out=a_w_out, b_w_in=b_w_in, b_w_grp=b_w_grp, b_w_out=b_w_out,
                   ple_w_gate=ple_w_gate, ple_w_proj=ple_w_proj)
    moms = dict(a_w_in=m_a_w_in, a_w_out=m_a_w_out, b_w_in=m_b_w_in, b_w_grp=m_b_w_grp, b_w_out=m_b_w_out,
                ple_w_gate=m_ple_w_gate, ple_w_proj=m_ple_w_proj)
    vars_ = dict(a_w_in=v_a_w_in, a_w_out=v_a_w_out, b_w_in=v_b_w_in, b_w_grp=v_b_w_grp, b_w_out=v_b_w_out,
                 ple_w_gate=v_ple_w_gate, ple_w_proj=v_ple_w_proj)
    w2d = {nm: _as_2d(nm, weights[nm]) for nm in BIG}
    bf = {nm: w2d[nm].astype(BF16) for nm in BIG}

    def shard_of(key):
        nm, j = key
        a = bf[nm][j]
        return a.reshape(2, a.shape[0] // 2, a.shape[1])

    gathered = {}

    def gather_rider(keys):
        return _gather_rider([shard_of(k) for k in keys])

    def keep(keys, landed):
        for k, a in zip(keys, landed):
            gathered[k] = a

    def weight(key):
        nm, j = key
        a = gathered[key]
        return {
            "a_w_in": lambda: a.reshape(4, d, e), "a_w_out": lambda: a.reshape(e, d),
            "b_w_in": lambda: a.reshape(4, d, e // 2), "b_w_grp": lambda: a.reshape(4, N_GROUPS, GROUP_DIM // 4, GROUP_DIM),
            "b_w_out": lambda: a.reshape(e, d), "ple_w_gate": lambda: a.reshape(d, d),
            "ple_w_proj": lambda: a.reshape(4, PLE_DIM, d // 4),
        }[nm]()

    small = jnp.concatenate([a_w_conv.reshape(6, e // 4), b_scale], axis=0)
    keys0 = _mixer_tensors(0) + _ple_tensors(0)
    landed, small_full = _first_gather([shard_of(k) for k in keys0], small)
    keep(keys0, landed)
    small_full = small_full.transpose(1, 0, 2).reshape(8, e)
    conv_w = [jnp.concatenate([small_full[3 * j:3 * j + 3], jnp.zeros((5, e), F32)], axis=0) for j in range(2)]
    scale_w = [small_full[6 + j:7 + j] for j in range(2)]

    p3 = p.reshape(DEPTH, s, PLE_DIM)
    mix_gain = [norm_mix[i:i + 1] for i in range(DEPTH)]
    ple_gain = [ple_norm[i:i + 1] for i in range(DEPTH)]

    h = x.reshape(s, d)
    saved = []
    for i in range(DEPTH):
        j = i // 2
        nxt_mix = _mixer_tensors(i + 1) if i + 1 < DEPTH else None
        nxt_ple = _ple_tensors(i + 1) if i + 1 < DEPTH else None
        rider = gather_rider(nxt_mix) if nxt_mix else None
        if i % 2 == 0:
            (h1, proj, hn, mb), landed = _fwd_mix_a(h, mix_gain[i], conv_w[j], weight(("a_w_in", j)),
                                                    weight(("a_w_out", j)), f"fwd_mix_a{j}", rider)
            mix = dict(proj=proj)
        else:
            (h1, zb, mx, diff, hn, mb), landed = _fwd_mix_b(h, mix_gain[i], scale_w[j], weight(("b_w_in", j)),
                                                            weight(("b_w_grp", j)), weight(("b_w_out", j)),
                                                            f"fwd_mix_b{j}", rider)
            mix = dict(z=zb, mx=mx, diff=diff)
        if nxt_mix:
            keep(nxt_mix, landed)
        rider = gather_rider(nxt_ple) if nxt_ple else None
        (h2, gate, pe, hp, pb), landed = _fwd_ple(h1, p3, ple_gain[i], weight(("ple_w_gate", i)),
                                                  weight(("ple_w_proj", i)), i, rider)
        if nxt_ple:
            keep(nxt_ple, landed)
        saved.append(dict(h=h, h1=h1, hn=hn, m=mb, gate=gate, pe=pe, hp=hp, pb=pb, **mix))
        h = h2

    dh, loss_part, d_final = _loss_head(h, loss_target.reshape(s, d), final_norm.reshape(1, d))
    loss = lax.psum(loss_part[0, 0], ("x", "y", "c"))

    n_slots = {nm: weights[nm].shape[0] for nm in BIG}
    stacks = {nm: None for nm in BIG}

    def owner_view(key, g):
        rows, cols = w2d[key[0]].shape[1:]
        return g.reshape(4, 2, rows // 2, cols)

    class Reduction:
        def __init__(self, keys, g32, g16):
            self.keys, self.g32, self.g16 = keys, g32, g16

        def pair_rider(self):
            return _pair_rider(self.g16)

        def after_pair(self, from_sibling):
            self.from_sibling = from_sibling
            self.pair_sums = [_pair_sum(g, sb, place, f"pair_sum_{nm}{j}")
                              for (nm, j), g, sb in zip(self.keys, self.g32, from_sibling)]

        def ici_rider(self):
            return _ici_rider(self.pair_sums)

        def after_ici(self, landed):
            for (nm, j), g, sb, ld in zip(self.keys, self.g32, self.from_sibling, landed):
                stacks[nm] = _final_sum(g, sb, ld, place, stacks[nm], j, n_slots[nm], f"final_sum_{nm}{j}")

        def final_rider(self):
            return _final_rider([stacks[nm] for nm, _ in self.keys], [j for _, j in self.keys])

        def after_final(self, swapped):
            for (nm, _), a in zip(self.keys, swapped):
                stacks[nm] = a

    d_mix_gain, d_ple_gain = [None] * DEPTH, [None] * DEPTH
    d_conv, d_scale = [None] * 2, [None] * 2
    pending = None
    for i in reversed(range(DEPTH)):
        j = i // 2
        sv = saved[i]
        keys, g32, g16 = [], [], []

        def add(key, pair):
            keys.append(key)
            g32.append(owner_view(key, pair[0]))
            g16.append(owner_view(key, pair[1]))

        (dh1, dh1b, da, dpe, d_ple_gain[i]), out = _bwd_ple(
            dh, sv["h1"], sv["gate"], sv["pe"], ple_gain[i], weight(("ple_w_gate", i)), i,
            pending.pair_rider() if pending else None)
        if pending:
            pending.after_pair(out)
        add(("ple_w_gate", i), _wgrad(sv["hp"], da, 1, False, f"wgrad_gate{i}")[0])
        add(("ple_w_proj", i), _wgrad(sv["pb"], dpe, 4, False, f"wgrad_proj{i}")[0])
        rider = pending.ici_rider() if pending else None
        if i % 2 == 0:
            (dh, dproj, d_conv[j], d_mix_gain[i]), out = _bwd_mix_a(
                dh1, dh1b, sv["h"], sv["proj"], mix_gain[i], conv_w[j], weight(("a_w_in", j)), weight(("a_w_out", j)),
                f"bwd_mix_a{j}", rider)
        else:
            (dh, dproj, dmx, d_scale[j], d_mix_gain[i]), out = _bwd_mix_b(
                dh1, dh1b, sv["h"], sv["z"], sv["mx"], mix_gain[i], scale_w[j], weight(("b_w_in", j)),
                weight(("b_w_grp", j)), weight(("b_w_out", j)), f"bwd_mix_b{j}", rider)
        if pending:
            pending.after_ici(out)
        mixer = "a" if i % 2 == 0 else "b"
        pair, out = _wgrad(sv["m"], dh1b, 1, False, f"wgrad_{mixer}_out{j}",
                           rider=pending.final_rider() if pending else None)
        if pending:
            pending.after_final(out)
        add((f"{mixer}_w_out", j), pair)
        if i % 2 == 1:
            add(("b_w_grp", j), _wgrad(sv["diff"], dmx, 4, True, f"wgrad_b_grp{j}", split_rows=GROUP_DIM // 4)[0])
        add((f"{mixer}_w_in", j), _wgrad(sv["hn"], dproj, 4, False, f"wgrad_{mixer}_in{j}")[0])
        pending = Reduction(keys, g32, g16)
    grad_x = dh.reshape(1, s, d)

    pending.after_pair(_run_rider(pending.pair_rider(), "pair_exchange"))
    pending.after_ici(_run_rider(pending.ici_rider(), "ici_exchange"))
    pending.after_final(_run_rider(pending.final_rider(), "final_exchange"))

    out_grad, out_delta, out_m, out_v = {}, {}, {}, {}
    for nm in BIG:
        shape = weights[nm].shape
        flat = (w2d[nm].shape[0] * w2d[nm].shape[1], w2d[nm].shape[2])
        g2 = stacks[nm].reshape(flat)
        delta, new_m, new_v = _adamw(g2, w2d[nm].reshape(flat), _as_2d(nm, moms[nm]).reshape(flat),
                                     _as_2d(nm, vars_[nm]).reshape(flat), f"adamw_{nm}")
        out_grad[nm], out_delta[nm] = g2.reshape(shape), delta.reshape(shape)
        out_m[nm], out_v[nm] = new_m.reshape(shape), new_v.reshape(shape)

    pack = jnp.concatenate(
        d_mix_gain + d_ple_gain + [d_final] + [d_conv[0][0:3], d_conv[1][0:3]] + d_scale
        + [jnp.zeros((SMALL_ROWS - 17, d), F32)], axis=0)
    total = _small_allreduce(pack)
    rep_rows = 16
    rep = lambda a, b_, c_: jnp.concatenate([a, b_, c_.reshape(1, d), jnp.zeros((rep_rows - 9, d), F32)], axis=0)
    rep_delta, rep_m, rep_v = _adamw(
        jnp.concatenate([total[0:9], jnp.zeros((rep_rows - 9, d), F32)], axis=0),
        rep(norm_mix, ple_norm, final_norm), rep(m_norm_mix, m_ple_norm, m_final_norm),
        rep(v_norm_mix, v_ple_norm, v_final_norm), "adamw_gains")
    mine_cols = lax.dynamic_slice_in_dim(total[9:17], chip * (e // 4), e // 4, axis=1)
    col = lambda a, b_: jnp.concatenate([a.reshape(6, e // 4), b_], axis=0)
    col_delta, col_m, col_v = _adamw(mine_cols, col(a_w_conv, b_scale), col(m_a_w_conv, m_b_scale),
                                     col(v_a_w_conv, v_b_scale), "adamw_cols")

    def unpack(rep_a, col_a):
        return dict(norm_mix=rep_a[0:4], ple_norm=rep_a[4:8], final_norm=rep_a[8],
                    a_w_conv=col_a[0:6].reshape(2, 3, e // 4), b_scale=col_a[6:8])

    small_out = [unpack(total, mine_cols), unpack(rep_delta, col_delta), unpack(rep_m, col_m), unpack(rep_v, col_v)]
    order = ["norm_mix", "a_w_in", "a_w_conv", "a_w_out", "b_w_in", "b_w_grp", "b_scale", "b_w_out", "ple_norm",
             "ple_w_gate", "ple_w_proj", "final_norm"]
    outs = [loss, grad_x]
    for big, small_d in zip([out_grad, out_delta, out_m, out_v], small_out):
        outs += [big[nm] if nm in big else small_d[nm] for nm in order]
    return tuple(outs)
```

```python
import jax
import jax.numpy as jnp
from jax import lax
from jax.experimental import pallas as pl
from jax.experimental.pallas import tpu as pltpu

F32 = jnp.float32
BF16 = jnp.bfloat16
MESH = pl.DeviceIdType.MESH

D_MODEL = 1024
MIX_WIDTH = 1024
PLE_DIM = 256
N_GROUPS = 4
GROUP_DIM = 256
POOL_WINDOWS = (2, 4, 8, 16)
DEPTH = 4
EPS = 1e-6

ADAM_LR = 0.001
ADAM_B1 = 0.9
ADAM_B2 = 0.999
ADAM_EPS = 1e-08
ADAM_WD = 0.01
ADAM_STEP = 10

HALO = 8
TS_MIX = 256
TS_PLE = 512
TR_EW = 512
VMEM_LIMIT = 56 * 1024 * 1024
SMALL_ROWS = 24
MIDDLE_STEPS_BEFORE_END = 3

ANY = pl.BlockSpec(memory_space=pl.ANY)


def _sds(shape, dtype):
    return jax.ShapeDtypeStruct(shape, dtype)


def _full(shape):
    nd = len(shape)
    return pl.BlockSpec(shape, lambda *_: (0,) * nd)


def _params(n_axes=1):
    return pltpu.CompilerParams(dimension_semantics=("arbitrary",) * n_axes, vmem_limit_bytes=VMEM_LIMIT)


def _dot(a, b):
    return jnp.dot(a, b, preferred_element_type=F32)


def _dot_nt(a, b):
    return lax.dot_general(a, b, (((1,), (1,)), ((), ())), preferred_element_type=F32)


def _dot_tn(a, b):
    return lax.dot_general(a, b, (((0,), (0,)), ((), ())), preferred_element_type=F32)


def _sigmoid(z):
    return 1.0 / (1.0 + jnp.exp(-z))


def _shift_down(x, k, tail):
    rolled = pltpu.roll(x, k, 0)
    rt = tail if k % HALO == 0 else pltpu.roll(tail, k % HALO, 0)
    row = lax.broadcasted_iota(jnp.int32, rt.shape, 0)
    head = jnp.where(row < k, rt, rolled[0:HALO])
    return jnp.concatenate([head, rolled[HALO:]], axis=0)


def _shift_up(x, k, head_next):
    n = x.shape[0]
    rolled = pltpu.roll(x, n - k, 0)
    rh = head_next if k % HALO == 0 else pltpu.roll(head_next, HALO - k % HALO, 0)
    row = lax.broadcasted_iota(jnp.int32, rh.shape, 0)
    tail = jnp.where(row >= HALO - k, rh, rolled[n - HALO:n])
    return jnp.concatenate([rolled[:n - HALO], tail], axis=0)


def _inv_counts(tile, ts):
    t = tile * ts + lax.broadcasted_iota(jnp.int32, (ts, 1), 0)
    return [1.0 / jnp.minimum(t + 1, w).astype(F32) for w in POOL_WINDOWS]


def _pool_fwd(u, carry, tile, ts):
    inv = _inv_counts(tile, ts)
    outs = []
    for g, w in enumerate(POOL_WINDOWS):
        cols = slice(g * GROUP_DIM, (g + 1) * GROUP_DIM)
        s = u[:, cols]
        level, k = 0, 1
        while k < w:
            tail = carry[level, :, cols]
            carry[level, :, cols] = s[ts - HALO:ts]
            s = s + _shift_down(s, k, tail)
            level, k = level + 1, k * 2
        outs.append(s * inv[g])
    return jnp.concatenate(outs, axis=1)


def _pool_bwd(dd, carry, tile, ts):
    inv = _inv_counts(tile, ts)
    outs = []
    for g, w in enumerate(POOL_WINDOWS):
        cols = slice(g * GROUP_DIM, (g + 1) * GROUP_DIM)
        q = dd[:, cols] * inv[g]
        level, k = 0, 1
        while k < w:
            head = carry[level, :, cols]
            carry[level, :, cols] = q[0:HALO]
            q = q + _shift_up(q, k, head)
            level, k = level + 1, k * 2
        outs.append(q)
    return jnp.concatenate(outs, axis=1)


def _copy_all(pairs, sems):
    copies = [pltpu.make_async_copy(src, dst, sems.at[n]) for n, (src, dst) in enumerate(pairs)]
    for cp in copies:
        cp.start()
    for cp in copies:
        cp.wait()


def _grp_pairs(wgrp_hbm, wgrp_v):
    rows = GROUP_DIM // 4
    return [(wgrp_hbm.at[k, g], wgrp_v.at[g, pl.ds(k * rows, rows), :]) for k in range(4) for g in range(N_GROUPS)]


def _rms(h):
    r = lax.rsqrt(jnp.mean(h * h, axis=-1, keepdims=True) + EPS)
    return h * r, r


def _rms_bwd(dhn, xn, r, gain):
    dgain = jnp.sum(dhn * xn, axis=0, keepdims=True)
    dxn = dhn * gain
    dh = r * (dxn - xn * jnp.mean(dxn * xn, axis=-1, keepdims=True))
    return dh, dgain


class _Rider:
    def __init__(self, inputs, out_shapes, n_sems, start, finish, middle=None, aliases=None):
        self.inputs, self.out_shapes, self.n_sems = list(inputs), list(out_shapes), n_sems
        self.start, self.middle, self.finish = start, middle, finish
        self.aliases = dict(aliases or {})


def _merge(riders):
    riders = [r for r in riders if r is not None]
    if not riders:
        return None
    if len(riders) == 1:
        return riders[0]

    def phase(which):
        def run(rin, rout, send, recv, base=0):
            i0 = o0 = s0 = 0
            for r in riders:
                fn = getattr(r, which)
                if fn is not None:
                    fn(rin[i0:i0 + len(r.inputs)], rout[o0:o0 + len(r.out_shapes)], send, recv, base + s0)
                i0, o0, s0 = i0 + len(r.inputs), o0 + len(r.out_shapes), s0 + r.n_sems
        return run

    aliases, i0, o0 = {}, 0, 0
    for r in riders:
        aliases.update({i0 + a: o0 + b for a, b in r.aliases.items()})
        i0, o0 = i0 + len(r.inputs), o0 + len(r.out_shapes)
    return _Rider(sum([r.inputs for r in riders], []), sum([r.out_shapes for r in riders], []),
                  sum(r.n_sems for r in riders), phase("start"), phase("finish"),
                  phase("middle") if any(r.middle for r in riders) else None, aliases)


def _split(landed, riders):
    out, o0 = [], 0
    for r in riders:
        if r is None:
            out.append(None)
        else:
            out.append(landed[o0:o0 + len(r.out_shapes)])
            o0 += len(r.out_shapes)
    return out


def _place():
    x, y, c = lax.axis_index("x"), lax.axis_index("y"), lax.axis_index("c")
    chips = [(1 - x, y), (x, 1 - y), (1 - x, 1 - y)]
    return x, y, c, chips


def _remote(src, dst, send_sems, recv_sems, sem, to):
    return pltpu.make_async_remote_copy(src_ref=src, dst_ref=dst, send_sem=send_sems.at[sem], recv_sem=recv_sems.at[sem],
                                        device_id=to, device_id_type=MESH)


def _gather_rider(shards):
    ni = len(shards)

    def first_hops(rin, rout, send, recv, base, x, y, c, chips):
        me = 2 * x + y
        return [_remote(rin[t].at[c], rout[t].at[me, c], send, recv, base + 7 * t + j, (cx, cy, c))
                for j, (cx, cy) in enumerate(chips) for t in range(ni)]

    def passes(rout, send, recv, base, x, y, c, chips):
        out = []
        for j, (cx, cy) in enumerate(chips):
            for t in range(ni):
                landed = rout[t].at[2 * cx + cy, c]
                out.append((_remote(landed, landed, send, recv, base + 7 * t + j, (x, y, 1 - c)),
                            _remote(landed, landed, send, recv, base + 7 * t + 3 + j, (x, y, 1 - c))))
        return out

    def own(rin, rout, send, recv, base, x, y, c):
        return [_remote(rin[t], rout[t].at[2 * x + y], send, recv, base + 7 * t + 6, (x, y, 1 - c)) for t in range(ni)]

    def start(rin, rout, send, recv, base=0):
        x, y, c, chips = _place()
        for cp in first_hops(rin, rout, send, recv, base, x, y, c, chips) + own(rin, rout, send, recv, base, x, y, c):
            cp.start()

    def middle(rin, rout, send, recv, base=0):
        x, y, c, chips = _place()
        for arrival, onward in passes(rout, send, recv, base, x, y, c, chips):
            arrival.wait_recv()
            onward.start()

    def finish(rin, rout, send, recv, base=0):
        x, y, c, chips = _place()
        for j, (cx, cy) in enumerate(chips):
            for t in range(ni):
                other = rout[t].at[2 * cx + cy, 1 - c]
                _remote(other, other, send, recv, base + 7 * t + 3 + j, (x, y, 1 - c)).wait_recv()
        for cp in own(rin, rout, send, recv, base, x, y, c):
            cp.wait_recv()
            cp.wait_send()
        for cp in first_hops(rin, rout, send, recv, base, x, y, c, chips):
            cp.wait_send()
        for _, onward in passes(rout, send, recv, base, x, y, c, chips):
            onward.wait_send()

    return _Rider(shards, [_sds((4,) + a.shape, a.dtype) for a in shards], 7 * ni, start, finish, middle)


def _pair_rider(grads):
    ni = len(grads)

    def copies(rin, rout, send, recv, base):
        x, y, c, _ = _place()
        return [_remote(rin[t].at[:, 1 - c], rout[t], send, recv, base + t, (x, y, 1 - c)) for t in range(ni)]

    def start(rin, rout, send, recv, base=0):
        for cp in copies(rin, rout, send, recv, base):
            cp.start()

    def finish(rin, rout, send, recv, base=0):
        for cp in copies(rin, rout, send, recv, base):
            cp.wait()

    return _Rider(grads, [_sds(g.shape[:1] + g.shape[2:], g.dtype) for g in grads], ni, start, finish)


def _ici_rider(pair_sums):
    ni = len(pair_sums)

    def copies(rin, rout, send, recv, base):
        x, y, c, chips = _place()
        return [_remote(rin[t].at[2 * cx + cy], rout[t].at[j], send, recv, base + 3 * t + j, (cx, cy, c))
                for j, (cx, cy) in enumerate(chips) for t in range(ni)]

    def start(rin, rout, send, recv, base=0):
        for cp in copies(rin, rout, send, recv, base):
            cp.start()

    def finish(rin, rout, send, recv, base=0):
        for cp in copies(rin, rout, send, recv, base):
            cp.wait()

    return _Rider(pair_sums, [_sds((3,) + g.shape[1:], g.dtype) for g in pair_sums], 3 * ni, start, finish)


def _final_rider(summed, slots):
    ni = len(summed)

    def copies(rout, send, recv, base):
        x, y, c, _ = _place()
        return [(_remote(rout[t].at[slots[t], c], rout[t].at[slots[t], c], send, recv, base + t, (x, y, 1 - c)),
                 _remote(rout[t].at[slots[t], 1 - c], rout[t].at[slots[t], 1 - c], send, recv, base + t, (x, y, 1 - c)))
                for t in range(ni)]

    def start(rin, rout, send, recv, base=0):
        for mine, _ in copies(rout, send, recv, base):
            mine.start()

    def finish(rin, rout, send, recv, base=0):
        for mine, theirs in copies(rout, send, recv, base):
            mine.wait_send()
            theirs.wait_recv()

    return _Rider(summed, [_sds(a.shape, a.dtype) for a in summed], ni, start, finish,
                  aliases={t: t for t in range(ni)})


def _call(body, *, name, grid, in_specs, out_specs, out_shape, scratch_shapes, operands, rider=None):
    if rider is None:
        outs = pl.pallas_call(body, name=name, grid=grid, in_specs=in_specs, out_specs=out_specs, out_shape=out_shape,
                              scratch_shapes=scratch_shapes, compiler_params=_params(len(grid)))(*operands)
        return list(outs), []
    n_in, n_out, n_scr = len(in_specs), len(out_specs), len(scratch_shapes)
    r_in, r_out = len(rider.inputs), len(rider.out_shapes)
    steps = 1
    for g in grid:
        steps *= g
    mid = max(steps - 1 - MIDDLE_STEPS_BEFORE_END, 0)

    def full_body(*refs):
        own_in, rin = refs[:n_in], refs[n_in:n_in + r_in]
        own_out = refs[n_in + r_in:n_in + r_in + n_out]
        rout = refs[n_in + r_in + n_out:n_in + r_in + n_out + r_out]
        own_scr = refs[n_in + r_in + n_out + r_out:n_in + r_in + n_out + r_out + n_scr]
        send, recv = refs[-2], refs[-1]
        step = pl.program_id(0)
        for axis in range(1, len(grid)):
            step = step * grid[axis] + pl.program_id(axis)

        @pl.when(step == 0)
        def _():
            rider.start(rin, rout, send, recv)

        body(*own_in, *own_out, *own_scr)

        if rider.middle is not None:
            @pl.when(step == mid)
            def _():
                rider.middle(rin, rout, send, recv)

        @pl.when(step == steps - 1)
        def _():
            rider.finish(rin, rout, send, recv)

    outs = pl.pallas_call(
        full_body, name=name, grid=grid,
        in_specs=list(in_specs) + [ANY] * r_in, out_specs=list(out_specs) + [ANY] * r_out,
        out_shape=list(out_shape) + rider.out_shapes,
        scratch_shapes=list(scratch_shapes) + [pltpu.SemaphoreType.DMA((rider.n_sems,)), pltpu.SemaphoreType.DMA((rider.n_sems,))],
        input_output_aliases={n_in + a: n_out + b for a, b in rider.aliases.items()},
        compiler_params=_params(len(grid)),
    )(*operands, *rider.inputs)
    return list(outs[:n_out]), list(outs[n_out:])


def _run_rider(rider, name):
    r_in, r_out = len(rider.inputs), len(rider.out_shapes)

    def body(*refs):
        rin, rout, send, recv = refs[:r_in], refs[r_in:r_in + r_out], refs[-2], refs[-1]
        rider.start(rin, rout, send, recv)
        if rider.middle is not None:
            rider.middle(rin, rout, send, recv)
        rider.finish(rin, rout, send, recv)

    outs = pl.pallas_call(
        body, name=name, in_specs=[ANY] * r_in, out_specs=[ANY] * r_out, out_shape=rider.out_shapes,
        scratch_shapes=[pltpu.SemaphoreType.DMA((rider.n_sems,)), pltpu.SemaphoreType.DMA((rider.n_sems,))],
        input_output_aliases=rider.aliases,
    )(*rider.inputs)
    return list(outs)


def _fwd_mix_a(h, gain, conv_w, w_in, w_out, name, rider=None):
    s, d = h.shape
    e = MIX_WIDTH
    ts = min(TS_MIX, s)
    nt = s // ts

    def body(h_ref, gain_ref, cw_ref, win_hbm, wout_hbm, h1_ref, proj_ref, win_v, wout_v, carry, sems):
        i = pl.program_id(0)

        @pl.when(i == 0)
        def _():
            _copy_all([(win_hbm, win_v), (wout_hbm, wout_v)], sems)
            carry[...] = jnp.zeros_like(carry)

        hh = h_ref[...]
        xn, _ = _rms(hh)
        hnb = (xn * gain_ref[...]).astype(BF16)
        b = _dot(hnb, win_v[0])
        c = _dot(hnb, win_v[1])
        v = _dot(hnb, win_v[2])
        z = _dot(hnb, win_v[3])
        proj_ref[:, 0 * e:1 * e] = b.astype(BF16)
        proj_ref[:, 1 * e:2 * e] = c.astype(BF16)
        proj_ref[:, 2 * e:3 * e] = v.astype(BF16)
        proj_ref[:, 3 * e:4 * e] = z.astype(BF16)
        cv = c * v
        tail = carry[...]
        carry[...] = cv[ts - HALO:ts]
        conv = cw_ref[0:1, :] * _shift_down(cv, 2, tail) + cw_ref[1:2, :] * _shift_down(cv, 1, tail) + cw_ref[2:3, :] * cv
        mb = ((z * _sigmoid(z)) * (b * conv)).astype(BF16)
        h1_ref[...] = hh + _dot(mb, wout_v[...])

    row = lambda width: pl.BlockSpec((ts, width), lambda i: (i, 0))
    return _call(
        body, name=name, grid=(nt,),
        in_specs=[row(d), _full((1, d)), _full((8, e)), ANY, ANY],
        out_specs=[row(d), row(4 * e)],
        out_shape=[_sds((s, d), F32), _sds((s, 4 * e), BF16)],
        scratch_shapes=[pltpu.VMEM((4, d, e), BF16), pltpu.VMEM((e, d), BF16), pltpu.VMEM((HALO, e), F32),
                        pltpu.SemaphoreType.DMA((2,))],
        operands=[h, gain, conv_w, w_in, w_out], rider=rider)


def _fwd_mix_b(h, gain, scale, w_in, w_grp, w_out, name, rider=None):
    s, d = h.shape
    e = MIX_WIDTH
    ts = min(TS_MIX, s)
    nt = s // ts

    def body(h_ref, gain_ref, scale_ref, win_hbm, wgrp_hbm, wout_hbm, h1_ref, z_ref, mx_ref, dd_ref,
             win_v, wgrp_v, wout_v, carry, sems):
        i = pl.program_id(0)

        @pl.when(i == 0)
        def _():
            _copy_all([(win_hbm, win_v), (wout_hbm, wout_v)] + _grp_pairs(wgrp_hbm, wgrp_v), sems)
            carry[...] = jnp.zeros_like(carry)

        hh = h_ref[...]
        xn, _ = _rms(hh)
        hnb = (xn * gain_ref[...]).astype(BF16)
        u = jnp.concatenate([_dot(hnb, win_v[0]), _dot(hnb, win_v[1])], axis=1)
        z = jnp.concatenate([_dot(hnb, win_v[2]), _dot(hnb, win_v[3])], axis=1)
        z_ref[...] = z.astype(BF16)
        diff = (_pool_fwd(u, carry, i, ts) - u).astype(BF16)
        dd_ref[...] = diff
        mx = jnp.concatenate(
            [_dot(diff[:, g * GROUP_DIM:(g + 1) * GROUP_DIM], wgrp_v[g]) for g in range(N_GROUPS)], axis=1)
        mx_ref[...] = mx.astype(BF16)
        mb = ((z * _sigmoid(z)) * (mx * scale_ref[...])).astype(BF16)
        h1_ref[...] = hh + _dot(mb, wout_v[...])

    row = lambda width: pl.BlockSpec((ts, width), lambda i: (i, 0))
    return _call(
        body, name=name, grid=(nt,),
        in_specs=[row(d), _full((1, d)), _full((1, e)), ANY, ANY, ANY],
        out_specs=[row(d), row(e), row(e), row(e)],
        out_shape=[_sds((s, d), F32)] + [_sds((s, e), BF16)] * 3,
        scratch_shapes=[pltpu.VMEM((4, d, e // 2), BF16), pltpu.VMEM((N_GROUPS, GROUP_DIM, GROUP_DIM), BF16),
                        pltpu.VMEM((e, d), BF16), pltpu.VMEM((4, HALO, e), F32), pltpu.SemaphoreType.DMA((18,))],
        operands=[h, gain, scale, w_in, w_grp, w_out], rider=rider)


def _fwd_ple(h1, p, gain, w_gate, w_proj, layer, rider=None):
    s, d = h1.shape
    pd = p.shape[-1]
    ts = min(TS_PLE, s)
    nt = s // ts

    def body(h1_ref, p_ref, gain_ref, wg_hbm, wp_hbm, h2_ref, gate_ref, wg_v, wp_v, sems):
        @pl.when(pl.program_id(0) == 0)
        def _():
            _copy_all([(wg_hbm, wg_v), (wp_hbm, wp_v)], sems)

        hh = h1_ref[...]
        xn, _ = _rms(hh)
        hpb = (xn * gain_ref[...]).astype(BF16)
        gate = _sigmoid(_dot(hpb, wg_v[...]))
        pb = p_ref[...].astype(BF16)
        pe = jnp.concatenate([_dot(pb, wp_v[k]) for k in range(4)], axis=1)
        gate_ref[...] = gate.astype(BF16)
        h2_ref[...] = hh + gate * pe

    row = lambda width: pl.BlockSpec((ts, width), lambda i: (i, 0))
    return _call(
        body, name=f"fwd_ple{layer}", grid=(nt,),
        in_specs=[row(d), pl.BlockSpec((None, ts, pd), lambda i: (layer, i, 0)), _full((1, d)), ANY, ANY],
        out_specs=[row(d), row(d)],
        out_shape=[_sds((s, d), F32), _sds((s, d), BF16)],
        scratch_shapes=[pltpu.VMEM((d, d), BF16), pltpu.VMEM((4, pd, d // 4), BF16), pltpu.SemaphoreType.DMA((2,))],
        operands=[h1, p, gain, w_gate, w_proj], rider=rider)


def _loss_head(h, target, gain):
    s, d = h.shape
    ts = min(TS_PLE, s)
    nt = s // ts

    def body(h_ref, t_ref, gain_ref, dh_ref, loss_ref, dgain_ref):
        @pl.when(pl.program_id(0) == 0)
        def _():
            loss_ref[...] = jnp.zeros_like(loss_ref)
            dgain_ref[...] = jnp.zeros_like(dgain_ref)

        xn, r = _rms(h_ref[...])
        err = xn * gain_ref[...] - t_ref[...]
        part = 0.5 * jnp.sum(jnp.mean(err * err, axis=-1, keepdims=True), axis=0, keepdims=True)
        loss_ref[...] += jnp.broadcast_to(part, loss_ref.shape)
        dh, dgain = _rms_bwd(err * (1.0 / d), xn, r, gain_ref[...])
        dh_ref[...] = dh
        dgain_ref[...] += dgain

    row = pl.BlockSpec((ts, d), lambda i: (i, 0))
    outs, _ = _call(
        body, name="loss_head", grid=(nt,),
        in_specs=[row, row, _full((1, d))],
        out_specs=[row, _full((8, 128)), _full((1, d))],
        out_shape=[_sds((s, d), F32), _sds((8, 128), F32), _sds((1, d), F32)],
        scratch_shapes=[], operands=[h, target, gain])
    return outs


def _bwd_ple(dh2, h1, gate, p, gain, w_gate, w_proj, layer, rider=None):
    s, d = dh2.shape
    pd = p.shape[-1]
    ts = min(TS_PLE, s)
    nt = s // ts
    qd = d // 4

    def body(dh2_ref, h1_ref, gate_ref, p_ref, gain_ref, wg_hbm, wp_hbm, dh1_ref, dgain_ref, dwg_hbm, dwp_hbm,
             wg_v, wp_v, acc_g, acc_p, sems):
        i = pl.program_id(0)

        @pl.when(i == 0)
        def _():
            _copy_all([(wg_hbm, wg_v), (wp_hbm, wp_v)], sems)
            dgain_ref[...] = jnp.zeros_like(dgain_ref)
            acc_g[...] = jnp.zeros_like(acc_g)
            acc_p[...] = jnp.zeros_like(acc_p)

        g2 = dh2_ref[...]
        gate_f = gate_ref[...].astype(F32)
        xn, r = _rms(h1_ref[...])
        hpb = (xn * gain_ref[...]).astype(BF16)
        pb = p_ref[...].astype(BF16)
        pe = jnp.concatenate([_dot(pb, wp_v[k]) for k in range(4)], axis=1)
        dpeb = (g2 * gate_f).astype(BF16)
        dab = ((g2 * pe) * (gate_f * (1.0 - gate_f))).astype(BF16)
        acc_g[...] += _dot_tn(hpb, dab)
        for k in range(4):
            acc_p[k] += _dot_tn(pb, dpeb[:, k * qd:(k + 1) * qd])
        dhp = _dot_nt(dab, wg_v[...])
        dh, dgain = _rms_bwd(dhp, xn, r, gain_ref[...])
        dh1_ref[...] = g2 + dh
        dgain_ref[...] += dgain

        @pl.when(i == nt - 1)
        def _():
            _copy_all([(acc_g, dwg_hbm), (acc_p, dwp_hbm)], sems)

    row = pl.BlockSpec((ts, d), lambda i: (i, 0))
    return _call(
        body, name=f"bwd_ple{layer}", grid=(nt,),
        in_specs=[row, row, row, pl.BlockSpec((None, ts, pd), lambda i: (layer, i, 0)), _full((1, d)), ANY, ANY],
        out_specs=[row, _full((1, d)), ANY, ANY],
        out_shape=[_sds((s, d), F32), _sds((1, d), F32), _sds((d, d), F32), _sds((4, pd, qd), F32)],
        scratch_shapes=[pltpu.VMEM((d, d), BF16), pltpu.VMEM((4, pd, qd), BF16), pltpu.VMEM((d, d), F32),
                        pltpu.VMEM((4, pd, qd), F32), pltpu.SemaphoreType.DMA((2,))],
        operands=[dh2, h1, gate, p, gain, w_gate, w_proj], rider=rider)


def _bwd_mix_a(dh1, h, proj, gain, conv_w, w_in, w_out, name, rider=None):
    s, d = dh1.shape
    e = MIX_WIDTH
    ts = min(TS_MIX, s)
    nt = s // ts
    hb = 16
    per = ts // hb

    def body(dh1_ref, h_ref, proj_ref, ch_ref, vh_ref, gain_ref, cw_ref, win_hbm, wout_hbm,
             dh_ref, dcw_ref, dgain_ref, dwin_hbm, dwout_hbm, win_v, wout_v, acc_in, acc_out, carry, sems):
        i = pl.program_id(0)
        tile = nt - 1 - i

        @pl.when(i == 0)
        def _():
            _copy_all([(win_hbm, win_v), (wout_hbm, wout_v)], sems)
            carry[...] = jnp.zeros_like(carry)
            dcw_ref[...] = jnp.zeros_like(dcw_ref)
            dgain_ref[...] = jnp.zeros_like(dgain_ref)
            acc_in[...] = jnp.zeros_like(acc_in)
            acc_out[...] = jnp.zeros_like(acc_out)

        b = proj_ref[:, 0 * e:1 * e].astype(F32)
        c = proj_ref[:, 1 * e:2 * e].astype(F32)
        v = proj_ref[:, 2 * e:3 * e].astype(F32)
        z = proj_ref[:, 3 * e:4 * e].astype(F32)
        cv = c * v
        prev = (ch_ref[...].astype(F32) * vh_ref[...].astype(F32))[hb - HALO:hb]
        tail = jnp.where(tile > 0, prev, jnp.zeros_like(prev))
        cv1 = _shift_down(cv, 1, tail)
        cv2 = _shift_down(cv, 2, tail)
        conv = cw_ref[0:1, :] * cv2 + cw_ref[1:2, :] * cv1 + cw_ref[2:3, :] * cv
        sig = _sigmoid(z)
        sz = z * sig
        y = b * conv
        dh1 = dh1_ref[...]
        dh1b = dh1.astype(BF16)
        acc_out[...] += _dot_tn((sz * y).astype(BF16), dh1b)
        dm = _dot_nt(dh1b, wout_v[...])
        dz = (dm * y) * (sig * (1.0 + z * (1.0 - sig)))
        dy = dm * sz
        db = dy * conv
        dconv = dy * b
        head = carry[...]
        carry[...] = dconv[0:HALO]
        dcv = cw_ref[2:3, :] * dconv + cw_ref[1:2, :] * _shift_up(dconv, 1, head) + cw_ref[0:1, :] * _shift_up(dconv, 2, head)
        dcw_ref[0:1, :] += jnp.sum(dconv * cv2, axis=0, keepdims=True)
        dcw_ref[1:2, :] += jnp.sum(dconv * cv1, axis=0, keepdims=True)
        dcw_ref[2:3, :] += jnp.sum(dconv * cv, axis=0, keepdims=True)
        parts = [db.astype(BF16), (dcv * v).astype(BF16), (dcv * c).astype(BF16), dz.astype(BF16)]
        xn, r = _rms(h_ref[...])
        hnb = (xn * gain_ref[...]).astype(BF16)
        for q in range(4):
            acc_in[q] += _dot_tn(hnb, parts[q])
        dhn = _dot_nt(parts[0], win_v[0]) + _dot_nt(parts[1], win_v[1]) + _dot_nt(parts[2], win_v[2]) + _dot_nt(parts[3], win_v[3])
        dh, dgain = _rms_bwd(dhn, xn, r, gain_ref[...])
        dh_ref[...] = dh1 + dh
        dgain_ref[...] += dgain

        @pl.when(i == nt - 1)
        def _():
            _copy_all([(acc_in, dwin_hbm), (acc_out, dwout_hbm)], sems)

    row = lambda width: pl.BlockSpec((ts, width), lambda i: (nt - 1 - i, 0))
    halo = lambda col: pl.BlockSpec((hb, e), lambda i: (jnp.maximum((nt - 1 - i) * per - 1, 0), col))
    return _call(
        body, name=name, grid=(nt,),
        in_specs=[row(d), row(d), row(4 * e), halo(1), halo(2), _full((1, d)), _full((8, e)), ANY, ANY],
        out_specs=[row(d), _full((8, e)), _full((1, d)), ANY, ANY],
        out_shape=[_sds((s, d), F32), _sds((8, e), F32), _sds((1, d), F32), _sds((4, d, e), F32), _sds((e, d), F32)],
        scratch_shapes=[pltpu.VMEM((4, d, e), BF16), pltpu.VMEM((e, d), BF16), pltpu.VMEM((4, d, e), F32),
                        pltpu.VMEM((e, d), F32), pltpu.VMEM((HALO, e), F32), pltpu.SemaphoreType.DMA((2,))],
        operands=[dh1, h, proj, proj, proj, gain, conv_w, w_in, w_out], rider=rider)


def _bwd_mix_b(dh1, h, z, mx, diff, gain, scale, w_in, w_grp, w_out, name, rider=None):
    s, d = dh1.shape
    e = MIX_WIDTH
    ts = min(TS_MIX, s)
    nt = s // ts
    half = e // 2

    def body(dh1_ref, h_ref, z_ref, mx_ref, dd_ref, gain_ref, scale_ref, win_hbm, wgrp_hbm, wout_hbm,
             dh_ref, dscale_ref, dgain_ref, dwin_hbm, dwgrp_hbm, dwout_hbm,
             win_v, wgrp_v, wout_v, acc_in, acc_grp, acc_out, carry, sems):
        i = pl.program_id(0)
        tile = nt - 1 - i

        @pl.when(i == 0)
        def _():
            _copy_all([(win_hbm, win_v), (wout_hbm, wout_v)] + _grp_pairs(wgrp_hbm, wgrp_v), sems)
            carry[...] = jnp.zeros_like(carry)
            dscale_ref[...] = jnp.zeros_like(dscale_ref)
            dgain_ref[...] = jnp.zeros_like(dgain_ref)
            acc_in[...] = jnp.zeros_like(acc_in)
            acc_grp[...] = jnp.zeros_like(acc_grp)
            acc_out[...] = jnp.zeros_like(acc_out)

        zf = z_ref[...].astype(F32)
        mxf = mx_ref[...].astype(F32)
        sig = _sigmoid(zf)
        sz = zf * sig
        mixed = mxf * scale_ref[...]
        dh1 = dh1_ref[...]
        dh1b = dh1.astype(BF16)
        acc_out[...] += _dot_tn((sz * mixed).astype(BF16), dh1b)
        dm = _dot_nt(dh1b, wout_v[...])
        dz = (dm * mixed) * (sig * (1.0 + zf * (1.0 - sig)))
        dmixed = dm * sz
        dscale_ref[...] += jnp.sum(dmixed * mxf, axis=0, keepdims=True)
        dmxb = (dmixed * scale_ref[...]).astype(BF16)
        diff = dd_ref[...]
        for g in range(N_GROUPS):
            cols = slice(g * GROUP_DIM, (g + 1) * GROUP_DIM)
            acc_grp[g] += _dot_tn(diff[:, cols], dmxb[:, cols])
        ddiff = jnp.concatenate(
            [_dot_nt(dmxb[:, g * GROUP_DIM:(g + 1) * GROUP_DIM], wgrp_v[g]) for g in range(N_GROUPS)], axis=1)
        dub = (_pool_bwd(ddiff, carry, tile, ts) - ddiff).astype(BF16)
        dzb = dz.astype(BF16)
        parts = [dub[:, 0:half], dub[:, half:e], dzb[:, 0:half], dzb[:, half:e]]
        xn, r = _rms(h_ref[...])
        hnb = (xn * gain_ref[...]).astype(BF16)
        for k in range(4):
            acc_in[k] += _dot_tn(hnb, parts[k])
        dhn = _dot_nt(parts[0], win_v[0]) + _dot_nt(parts[1], win_v[1]) + _dot_nt(parts[2], win_v[2]) + _dot_nt(parts[3], win_v[3])
        dh, dgain = _rms_bwd(dhn, xn, r, gain_ref[...])
        dh_ref[...] = dh1 + dh
        dgain_ref[...] += dgain

        @pl.when(i == nt - 1)
        def _():
            _copy_all([(acc_in, dwin_hbm), (acc_out, dwout_hbm)] + [(v, hb_) for hb_, v in _grp_pairs(dwgrp_hbm, acc_grp)], sems)

    row = lambda width: pl.BlockSpec((ts, width), lambda i: (nt - 1 - i, 0))
    return _call(
        body, name=name, grid=(nt,),
        in_specs=[row(d), row(d), row(e), row(e), row(e), _full((1, d)), _full((1, e)), ANY, ANY, ANY],
        out_specs=[row(d), _full((1, e)), _full((1, d)), ANY, ANY, ANY],
        out_shape=[_sds((s, d), F32), _sds((1, e), F32), _sds((1, d), F32), _sds((4, d, half), F32),
                   _sds((4, N_GROUPS, GROUP_DIM // 4, GROUP_DIM), F32), _sds((e, d), F32)],
        scratch_shapes=[pltpu.VMEM((4, d, half), BF16), pltpu.VMEM((N_GROUPS, GROUP_DIM, GROUP_DIM), BF16),
                        pltpu.VMEM((e, d), BF16), pltpu.VMEM((4, d, half), F32),
                        pltpu.VMEM((N_GROUPS, GROUP_DIM, GROUP_DIM), F32), pltpu.VMEM((e, d), F32),
                        pltpu.VMEM((4, HALO, e), F32), pltpu.SemaphoreType.DMA((18,))],
        operands=[dh1, h, z, mx, diff, gain, scale, w_in, w_grp, w_out], rider=rider)


def _first_gather(shards, small):
    rider = _gather_rider(shards)
    ni = len(shards)

    def body(*refs):
        rin, small_src = refs[:ni], refs[ni]
        rout, small_dst = refs[ni + 1:2 * ni + 1], refs[2 * ni + 1]
        send, recv, ssend, srecv = refs[2 * ni + 2:]
        x, y, c, chips = _place()
        me = 2 * x + y
        peers = [(cx, cy, c) for cx, cy in chips] + [(x, y, 1 - c)]
        vec = [_remote(small_src, small_dst.at[me], ssend, srecv, j, to) for j, to in enumerate(peers)]
        for cp in vec:
            cp.start()
        rider.start(rin, rout, send, recv)
        rider.middle(rin, rout, send, recv)
        rider.finish(rin, rout, send, recv)
        for j, (px, py, _) in enumerate(peers):
            _remote(small_src, small_dst.at[2 * px + py], ssend, srecv, j, peers[j]).wait_recv()
        for cp in vec:
            cp.wait_send()

    outs = pl.pallas_call(
        body, name="first_gather", in_specs=[ANY] * (ni + 1), out_specs=[ANY] * (ni + 1),
        out_shape=rider.out_shapes + [_sds((4,) + small.shape, small.dtype)],
        scratch_shapes=[pltpu.SemaphoreType.DMA((rider.n_sems,)), pltpu.SemaphoreType.DMA((rider.n_sems,)),
                        pltpu.SemaphoreType.DMA((4,)), pltpu.SemaphoreType.DMA((4,))],
    )(*shards, small)
    return list(outs[:ni]), outs[ni]


def _small_allreduce(pack):
    rows, d = pack.shape
    flips = [(fx, fy, fc) for fx in (0, 1) for fy in (0, 1) for fc in (0, 1)][1:]

    def body(pack_ref, out_ref, land, send_sems, recv_sems):
        x, y, c, _ = _place()
        me = 4 * x + 2 * y + c
        land[pl.ds(me, 1)] = pack_ref[...][None]
        peers = [(1 - x if fx else x, 1 - y if fy else y, 1 - c if fc else c) for fx, fy, fc in flips]
        copies = [_remote(pack_ref, land.at[me], send_sems, recv_sems, r, peer) for r, peer in enumerate(peers)]
        for cp in copies:
            cp.start()
        for r, (px, py, pc) in enumerate(peers):
            _remote(pack_ref, land.at[4 * px + 2 * py + pc], send_sems, recv_sems, r, (px, py, pc)).wait_recv()
        for cp in copies:
            cp.wait_send()
        total = land[0]
        for dev in range(1, 8):
            total = total + land[dev]
        out_ref[...] = total

    vmem = pl.BlockSpec(memory_space=pltpu.VMEM)
    return pl.pallas_call(
        body, name="small_allreduce",
        in_specs=[vmem], out_specs=vmem, out_shape=_sds((rows, d), F32),
        scratch_shapes=[pltpu.VMEM((8, rows, d), F32), pltpu.SemaphoreType.DMA((7,)), pltpu.SemaphoreType.DMA((7,))],
    )(pack)


def _ew_rows(rows):
    return min(TR_EW, rows)


def _pair_sum(grad, sibling_rows, place, name):
    _, _, rh, cols = grad.shape
    tr = _ew_rows(rh)

    def body(place_ref, g_ref, s_ref, out_ref):
        out_ref[...] = (g_ref[...] + s_ref[...]).astype(BF16)

    grid_spec = pltpu.PrefetchScalarGridSpec(
        num_scalar_prefetch=1, grid=(4, rh // tr),
        in_specs=[pl.BlockSpec((None, None, tr, cols), lambda k, r, pos: (k, pos[1], r, 0)),
                  pl.BlockSpec((None, tr, cols), lambda k, r, pos: (k, r, 0))],
        out_specs=pl.BlockSpec((None, tr, cols), lambda k, r, pos: (k, r, 0)))
    return pl.pallas_call(body, name=name, grid_spec=grid_spec, out_shape=_sds((4, rh, cols), BF16),
                          compiler_params=_params(2))(place, grad, sibling_rows)


def _final_sum(grad, sibling_rows, landed, place, stack, slot, n_slots, name):
    _, _, rh, cols = grad.shape
    tr = _ew_rows(rh)

    def body(place_ref, g_ref, s_ref, l_ref, *rest):
        out_ref = rest[-1]
        total = g_ref[...] + s_ref[...]
        for j in range(3):
            total = total + l_ref[j].astype(F32)
        out_ref[...] = total

    in_specs = [pl.BlockSpec((None, None, tr, cols), lambda r, pos: (pos[0], pos[1], r, 0)),
                pl.BlockSpec((None, tr, cols), lambda r, pos: (pos[0], r, 0)),
                pl.BlockSpec((3, tr, cols), lambda r, pos: (0, r, 0))]
    operands = [place, grad, sibling_rows, landed]
    aliases = {}
    if stack is not None:
        in_specs.append(ANY)
        operands.append(stack)
        aliases = {4: 0}
    grid_spec = pltpu.PrefetchScalarGridSpec(
        num_scalar_prefetch=1, grid=(rh // tr,), in_specs=in_specs,
        out_specs=pl.BlockSpec((None, None, tr, cols), lambda r, pos: (slot, pos[1], r, 0)))
    return pl.pallas_call(body, name=name, grid_spec=grid_spec, out_shape=_sds((n_slots, 2, rh, cols), F32),
                          input_output_aliases=aliases, compiler_params=_params(1))(*operands)


def _adamw(g, w, m, v, name):
    rows, cols = g.shape
    tr = _ew_rows(rows)

    def body(g_ref, w_ref, m_ref, v_ref, delta_ref, nm_ref, nv_ref):
        gg = g_ref[...]
        nm = ADAM_B1 * m_ref[...] + (1.0 - ADAM_B1) * gg
        nv = ADAM_B2 * v_ref[...] + (1.0 - ADAM_B2) * (gg * gg)
        m_hat = nm / (1.0 - ADAM_B1 ** ADAM_STEP)
        v_hat = nv / (1.0 - ADAM_B2 ** ADAM_STEP)
        delta_ref[...] = -ADAM_LR * (m_hat / (jnp.sqrt(v_hat) + ADAM_EPS) + ADAM_WD * w_ref[...])
        nm_ref[...] = nm
        nv_ref[...] = nv

    spec = pl.BlockSpec((tr, cols), lambda i: (i, 0))
    return pl.pallas_call(
        body, name=name, grid=(rows // tr,), in_specs=[spec] * 4, out_specs=[spec] * 3,
        out_shape=[_sds((rows, cols), F32)] * 3, compiler_params=_params(),
    )(g, w, m, v)


BIG = ["a_w_in", "a_w_out", "b_w_in", "b_w_grp", "b_w_out", "ple_w_gate", "ple_w_proj"]

GATHER_PLAN = {
    "first": [("a_w_in", 0), ("a_w_out", 0)],
    "mix0": [("ple_w_gate", 0), ("ple_w_proj", 0), ("b_w_in", 0), ("b_w_grp", 0), ("b_w_out", 0)],
    "ple0": [("ple_w_gate", 1), ("ple_w_proj", 1), ("a_w_out", 1)],
    "mix1": [("a_w_in", 1)],
    "ple1": [("ple_w_gate", 2), ("ple_w_proj", 2), ("b_w_out", 1)],
    "mix2": [("b_w_in", 1), ("b_w_grp", 1), ("ple_w_gate", 3), ("ple_w_proj", 3)],
}


def _as_2d(name, a):
    if name == "b_w_grp":
        return a.reshape(a.shape[0], N_GROUPS * (GROUP_DIM // 4), GROUP_DIM)
    return a


def kernel(x, p, norm_mix, a_w_in, a_w_conv, a_w_out, b_w_in, b_w_grp, b_scale, b_w_out, ple_norm, ple_w_gate, ple_w_proj, final_norm, loss_target, m_norm_mix, m_a_w_in, m_a_w_conv, m_a_w_out, m_b_w_in, m_b_w_grp, m_b_scale, m_b_w_out, m_ple_norm, m_ple_w_gate, m_ple_w_proj, m_final_norm, v_norm_mix, v_a_w_in, v_a_w_conv, v_a_w_out, v_b_w_in, v_b_w_grp, v_b_scale, v_b_w_out, v_ple_norm, v_ple_w_gate, v_ple_w_proj, v_final_norm):
    d, e = D_MODEL, MIX_WIDTH
    s = x.shape[1]
    cx, cy, cc = lax.axis_index("x"), lax.axis_index("y"), lax.axis_index("c")
    chip = 2 * cx + cy
    place = jnp.stack([chip, cc]).astype(jnp.int32)

    weights = dict(a_w_in=a_w_in, a_w_out=a_w_out, b_w_in=b_w_in, b_w_grp=b_w_grp, b_w_out=b_w_out,
                   ple_w_gate=ple_w_gate, ple_w_proj=ple_w_proj)
    moms = dict(a_w_in=m_a_w_in, a_w_out=m_a_w_out, b_w_in=m_b_w_in, b_w_grp=m_b_w_grp, b_w_out=m_b_w_out,
                ple_w_gate=m_ple_w_gate, ple_w_proj=m_ple_w_proj)
    vars_ = dict(a_w_in=v_a_w_in, a_w_out=v_a_w_out, b_w_in=v_b_w_in, b_w_grp=v_b_w_grp, b_w_out=v_b_w_out,
                 ple_w_gate=v_ple_w_gate, ple_w_proj=v_ple_w_proj)
    w2d = {nm: _as_2d(nm, weights[nm]) for nm in BIG}
    bf = {nm: w2d[nm].astype(BF16) for nm in BIG}

    def shard_of(key):
        nm, j = key
        a = bf[nm][j]
        return a.reshape(2, a.shape[0] // 2, a.shape[1])

    gathered = {}

    def gather_rider(host):
        keys = GATHER_PLAN.get(host)
        return _gather_rider([shard_of(k) for k in keys]) if keys else None

    def keep(host, landed):
        for k, a in zip(GATHER_PLAN.get(host, []), landed):
            gathered[k] = a

    def weight(nm, j):
        a = gathered[(nm, j)]
        shapes = {"a_w_in": (4, d, e), "a_w_out": (e, d), "b_w_in": (4, d, e // 2),
                  "b_w_grp": (4, N_GROUPS, GROUP_DIM // 4, GROUP_DIM), "b_w_out": (e, d), "ple_w_gate": (d, d),
                  "ple_w_proj": (4, PLE_DIM, d // 4)}
        return a.reshape(shapes[nm])

    small = jnp.concatenate([a_w_conv.reshape(6, e // 4), b_scale], axis=0)
    landed, small_full = _first_gather([shard_of(k) for k in GATHER_PLAN["first"]], small)
    keep("first", landed)
    small_full = small_full.transpose(1, 0, 2).reshape(8, e)
    conv_w = [jnp.concatenate([small_full[3 * j:3 * j + 3], jnp.zeros((5, e), F32)], axis=0) for j in range(2)]
    scale_w = [small_full[6 + j:7 + j] for j in range(2)]

    p3 = p.reshape(DEPTH, s, PLE_DIM)
    mix_gain = [norm_mix[i:i + 1] for i in range(DEPTH)]
    ple_gain = [ple_norm[i:i + 1] for i in range(DEPTH)]

    h = x.reshape(s, d)
    saved = []
    for i in range(DEPTH):
        j = i // 2
        rider = gather_rider(f"mix{i}")
        if i % 2 == 0:
            (h1, proj), landed = _fwd_mix_a(h, mix_gain[i], conv_w[j], weight("a_w_in", j), weight("a_w_out", j),
                                            f"fwd_mix_a{j}", rider)
            mix = dict(proj=proj)
        else:
            (h1, zb, mx, diff), landed = _fwd_mix_b(h, mix_gain[i], scale_w[j], weight("b_w_in", j), weight("b_w_grp", j),
                                                    weight("b_w_out", j), f"fwd_mix_b{j}", rider)
            mix = dict(z=zb, mx=mx, diff=diff)
        keep(f"mix{i}", landed)
        (h2, gate), landed = _fwd_ple(h1, p3, ple_gain[i], weight("ple_w_gate", i), weight("ple_w_proj", i), i,
                                      gather_rider(f"ple{i}"))
        keep(f"ple{i}", landed)
        saved.append(dict(h=h, h1=h1, gate=gate, **mix))
        h = h2

    dh, loss_part, d_final = _loss_head(h, loss_target.reshape(s, d), final_norm.reshape(1, d))
    loss = lax.psum(loss_part[0, 0], ("x", "y", "c"))

    n_slots = {nm: weights[nm].shape[0] for nm in BIG}
    stacks = {nm: None for nm in BIG}

    class Group:
        def __init__(self, keys, grads):
            self.keys, self.stage = keys, 0
            self.g32 = [g.reshape(4, 2, w2d[nm].shape[1] // 2, w2d[nm].shape[2]) for (nm, _), g in zip(keys, grads)]

        def rider(self):
            if self.stage == 0:
                return _pair_rider(self.g32)
            if self.stage == 1:
                return _ici_rider(self.pair_sums)
            return _final_rider([stacks[nm] for nm, _ in self.keys], [j for _, j in self.keys])

        def advance(self, landed):
            if self.stage == 0:
                self.from_sibling = landed
                self.pair_sums = [_pair_sum(g, sb, place, f"pair_sum_{nm}{j}")
                                  for (nm, j), g, sb in zip(self.keys, self.g32, landed)]
            elif self.stage == 1:
                for (nm, j), g, sb, ld in zip(self.keys, self.g32, self.from_sibling, landed):
                    stacks[nm] = _final_sum(g, sb, ld, place, stacks[nm], j, n_slots[nm], f"final_sum_{nm}{j}")
            else:
                for (nm, _), a in zip(self.keys, landed):
                    stacks[nm] = a
            self.stage += 1

    active = []

    def riders_now():
        parts = [g.rider() for g in active]
        return parts, _merge(parts)

    def advance_all(parts, landed):
        for g, l in zip(list(active), _split(landed, parts)):
            g.advance(l)
            if g.stage == 3:
                active.remove(g)

    d_mix_gain, d_ple_gain = [None] * DEPTH, [None] * DEPTH
    d_conv, d_scale = [None] * 2, [None] * 2
    for i in reversed(range(DEPTH)):
        j = i // 2
        sv = saved[i]
        parts, rider = riders_now()
        (dh1, d_ple_gain[i], dwg, dwp), landed = _bwd_ple(
            dh, sv["h1"], sv["gate"], p3, ple_gain[i], weight("ple_w_gate", i), weight("ple_w_proj", i), i, rider)
        advance_all(parts, landed)
        active.append(Group([("ple_w_gate", i), ("ple_w_proj", i)], [dwg, dwp]))
        parts, rider = riders_now()
        if i % 2 == 0:
            (dh, d_conv[j], d_mix_gain[i], dwin, dwout), landed = _bwd_mix_a(
                dh1, sv["h"], sv["proj"], mix_gain[i], conv_w[j], weight("a_w_in", j), weight("a_w_out", j),
                f"bwd_mix_a{j}", rider)
            new = Group([("a_w_in", j), ("a_w_out", j)], [dwin, dwout])
        else:
            (dh, d_scale[j], d_mix_gain[i], dwin, dwgrp, dwout), landed = _bwd_mix_b(
                dh1, sv["h"], sv["z"], sv["mx"], sv["diff"], mix_gain[i], scale_w[j], weight("b_w_in", j),
                weight("b_w_grp", j), weight("b_w_out", j), f"bwd_mix_b{j}", rider)
            new = Group([("b_w_in", j), ("b_w_grp", j), ("b_w_out", j)], [dwin, dwgrp, dwout])
        advance_all(parts, landed)
        active.append(new)
    grad_x = dh.reshape(1, s, d)

    tail = 0
    while active:
        parts, rider = riders_now()
        advance_all(parts, _run_rider(rider, f"tail_exchange{tail}"))
        tail += 1

    out_grad, out_delta, out_m, out_v = {}, {}, {}, {}
    for nm in BIG:
        shape = weights[nm].shape
        flat = (w2d[nm].shape[0] * w2d[nm].shape[1], w2d[nm].shape[2])
        g2 = stacks[nm].reshape(flat)
        delta, new_m, new_v = _adamw(g2, w2d[nm].reshape(flat), _as_2d(nm, moms[nm]).reshape(flat),
                                     _as_2d(nm, vars_[nm]).reshape(flat), f"adamw_{nm}")
        out_grad[nm], out_delta[nm] = g2.reshape(shape), delta.reshape(shape)
        out_m[nm], out_v[nm] = new_m.reshape(shape), new_v.reshape(shape)

    pack = jnp.concatenate(
        d_mix_gain + d_ple_gain + [d_final] + [d_conv[0][0:3], d_conv[1][0:3]] + d_scale
        + [jnp.zeros((SMALL_ROWS - 17, d), F32)], axis=0)
    total = _small_allreduce(pack)
    rep_rows = 16
    rep = lambda a, b_, c_: jnp.concatenate([a, b_, c_.reshape(1, d), jnp.zeros((rep_rows - 9, d), F32)], axis=0)
    rep_delta, rep_m, rep_v = _adamw(
        jnp.concatenate([total[0:9], jnp.zeros((rep_rows - 9, d), F32)], axis=0),
        rep(norm_mix, ple_norm, final_norm), rep(m_norm_mix, m_ple_norm, m_final_norm),
        rep(v_norm_mix, v_ple_norm, v_final_norm), "adamw_gains")
    mine_cols = lax.dynamic_slice_in_dim(total[9:17], chip * (e // 4), e // 4, axis=1)
    col = lambda a, b_: jnp.concatenate([a.reshape(6, e // 4), b_], axis=0)
    col_delta, col_m, col_v = _adamw(mine_cols, col(a_w_conv, b_scale), col(m_a_w_conv, m_b_scale),
                                     col(v_a_w_conv, v_b_scale), "adamw_cols")

    def unpack(rep_a, col_a):
        return dict(norm_mix=rep_a[0:4], ple_norm=rep_a[4:8], final_norm=rep_a[8],
                    a_w_conv=col_a[0:6].reshape(2, 3, e // 4), b_scale=col_a[6:8])

    small_out = [unpack(total, mine_cols), unpack(rep_delta, col_delta), unpack(rep_m, col_m), unpack(rep_v, col_v)]
    order = ["norm_mix", "a_w_in", "a_w_conv", "a_w_out", "b_w_in", "b_w_grp", "b_scale", "b_w_out", "ple_norm",
             "ple_w_gate", "ple_w_proj", "final_norm"]
    outs = [loss, grad_x]
    for big, small_d in zip([out_grad, out_delta, out_m, out_v], small_out):
        outs += [big[nm] if nm in big else small_d[nm] for nm in order]
    return tuple(outs)
```

```python
import jax
import jax.numpy as jnp
from jax import lax
from jax.experimental import pallas as pl
from jax.experimental.pallas import tpu as pltpu

F32 = jnp.float32
BF16 = jnp.bfloat16
MESH = pl.DeviceIdType.MESH

D_MODEL = 1024
MIX_WIDTH = 1024
PLE_DIM = 256
N_GROUPS = 4
GROUP_DIM = 256
POOL_WINDOWS = (2, 4, 8, 16)
DEPTH = 4
EPS = 1e-6

ADAM_LR = 0.001
ADAM_B1 = 0.9
ADAM_B2 = 0.999
ADAM_EPS = 1e-08
ADAM_WD = 0.01
ADAM_STEP = 10

HALO = 8
TS_MIX = 256
TS_PLE = 512
TR_EW = 512
VMEM_LIMIT = 56 * 1024 * 1024
SMALL_ROWS = 24
MIDDLE_STEPS_BEFORE_END = 3

ANY = pl.BlockSpec(memory_space=pl.ANY)


def _sds(shape, dtype):
    return jax.ShapeDtypeStruct(shape, dtype)


def _full(shape):
    nd = len(shape)
    return pl.BlockSpec(shape, lambda *_: (0,) * nd)


def _params(n_axes=1):
    return pltpu.CompilerParams(dimension_semantics=("arbitrary",) * n_axes, vmem_limit_bytes=VMEM_LIMIT)


def _dot(a, b):
    return jnp.dot(a, b, preferred_element_type=F32)


def _dot_nt(a, b):
    return lax.dot_general(a, b, (((1,), (1,)), ((), ())), preferred_element_type=F32)


def _dot_tn(a, b):
    return lax.dot_general(a, b, (((0,), (0,)), ((), ())), preferred_element_type=F32)


def _sigmoid(z):
    return 1.0 / (1.0 + jnp.exp(-z))


def _shift_down(x, k, tail):
    rolled = pltpu.roll(x, k, 0)
    rt = tail if k % HALO == 0 else pltpu.roll(tail, k % HALO, 0)
    row = lax.broadcasted_iota(jnp.int32, rt.shape, 0)
    head = jnp.where(row < k, rt, rolled[0:HALO])
    return jnp.concatenate([head, rolled[HALO:]], axis=0)


def _shift_up(x, k, head_next):
    n = x.shape[0]
    rolled = pltpu.roll(x, n - k, 0)
    rh = head_next if k % HALO == 0 else pltpu.roll(head_next, HALO - k % HALO, 0)
    row = lax.broadcasted_iota(jnp.int32, rh.shape, 0)
    tail = jnp.where(row >= HALO - k, rh, rolled[n - HALO:n])
    return jnp.concatenate([rolled[:n - HALO], tail], axis=0)


def _inv_counts(tile, ts):
    t = tile * ts + lax.broadcasted_iota(jnp.int32, (ts, 1), 0)
    return [1.0 / jnp.minimum(t + 1, w).astype(F32) for w in POOL_WINDOWS]


def _pool_fwd(u, carry, tile, ts):
    inv = _inv_counts(tile, ts)
    outs = []
    for g, w in enumerate(POOL_WINDOWS):
        cols = slice(g * GROUP_DIM, (g + 1) * GROUP_DIM)
        s = u[:, cols]
        level, k = 0, 1
        while k < w:
            tail = carry[level, :, cols]
            carry[level, :, cols] = s[ts - HALO:ts]
            s = s + _shift_down(s, k, tail)
            level, k = level + 1, k * 2
        outs.append(s * inv[g])
    return jnp.concatenate(outs, axis=1)


def _pool_bwd(dd, carry, tile, ts):
    inv = _inv_counts(tile, ts)
    outs = []
    for g, w in enumerate(POOL_WINDOWS):
        cols = slice(g * GROUP_DIM, (g + 1) * GROUP_DIM)
        q = dd[:, cols] * inv[g]
        level, k = 0, 1
        while k < w:
            head = carry[level, :, cols]
            carry[level, :, cols] = q[0:HALO]
            q = q + _shift_up(q, k, head)
            level, k = level + 1, k * 2
        outs.append(q)
    return jnp.concatenate(outs, axis=1)


def _copy_all(pairs, sems):
    copies = [pltpu.make_async_copy(src, dst, sems.at[n]) for n, (src, dst) in enumerate(pairs)]
    for cp in copies:
        cp.start()
    for cp in copies:
        cp.wait()


def _grp_pairs(wgrp_hbm, wgrp_v):
    rows = GROUP_DIM // 4
    return [(wgrp_hbm.at[k, g], wgrp_v.at[g, pl.ds(k * rows, rows), :]) for k in range(4) for g in range(N_GROUPS)]


def _rms(h):
    r = lax.rsqrt(jnp.mean(h * h, axis=-1, keepdims=True) + EPS)
    return h * r, r


def _rms_bwd(dhn, xn, r, gain):
    dgain = jnp.sum(dhn * xn, axis=0, keepdims=True)
    dxn = dhn * gain
    dh = r * (dxn - xn * jnp.mean(dxn * xn, axis=-1, keepdims=True))
    return dh, dgain


class _Rider:
    def __init__(self, inputs, out_shapes, n_sems, start, finish, middle=None, aliases=None):
        self.inputs, self.out_shapes, self.n_sems = list(inputs), list(out_shapes), n_sems
        self.start, self.middle, self.finish = start, middle, finish
        self.aliases = dict(aliases or {})


def _merge(riders):
    riders = [r for r in riders if r is not None]
    if not riders:
        return None
    if len(riders) == 1:
        return riders[0]

    def phase(which):
        def run(rin, rout, send, recv, base=0):
            i0 = o0 = s0 = 0
            for r in riders:
                fn = getattr(r, which)
                if fn is not None:
                    fn(rin[i0:i0 + len(r.inputs)], rout[o0:o0 + len(r.out_shapes)], send, recv, base + s0)
                i0, o0, s0 = i0 + len(r.inputs), o0 + len(r.out_shapes), s0 + r.n_sems
        return run

    aliases, i0, o0 = {}, 0, 0
    for r in riders:
        aliases.update({i0 + a: o0 + b for a, b in r.aliases.items()})
        i0, o0 = i0 + len(r.inputs), o0 + len(r.out_shapes)
    return _Rider(sum([r.inputs for r in riders], []), sum([r.out_shapes for r in riders], []),
                  sum(r.n_sems for r in riders), phase("start"), phase("finish"),
                  phase("middle") if any(r.middle for r in riders) else None, aliases)


def _split(landed, riders):
    out, o0 = [], 0
    for r in riders:
        if r is None:
            out.append(None)
        else:
            out.append(landed[o0:o0 + len(r.out_shapes)])
            o0 += len(r.out_shapes)
    return out


def _place():
    x, y, c = lax.axis_index("x"), lax.axis_index("y"), lax.axis_index("c")
    chips = [(1 - x, y), (x, 1 - y), (1 - x, 1 - y)]
    return x, y, c, chips


def _remote(src, dst, send_sems, recv_sems, sem, to):
    return pltpu.make_async_remote_copy(src_ref=src, dst_ref=dst, send_sem=send_sems.at[sem], recv_sem=recv_sems.at[sem],
                                        device_id=to, device_id_type=MESH)


def _gather_rider(shards):
    ni = len(shards)

    def first_hops(rin, rout, send, recv, base, x, y, c, chips):
        me = 2 * x + y
        return [_remote(rin[t].at[c], rout[t].at[me, c], send, recv, base + 7 * t + j, (cx, cy, c))
                for j, (cx, cy) in enumerate(chips) for t in range(ni)]

    def passes(rout, send, recv, base, x, y, c, chips):
        out = []
        for j, (cx, cy) in enumerate(chips):
            for t in range(ni):
                landed = rout[t].at[2 * cx + cy, c]
                out.append((_remote(landed, landed, send, recv, base + 7 * t + j, (x, y, 1 - c)),
                            _remote(landed, landed, send, recv, base + 7 * t + 3 + j, (x, y, 1 - c))))
        return out

    def own(rin, rout, send, recv, base, x, y, c):
        return [_remote(rin[t], rout[t].at[2 * x + y], send, recv, base + 7 * t + 6, (x, y, 1 - c)) for t in range(ni)]

    def start(rin, rout, send, recv, base=0):
        x, y, c, chips = _place()
        for cp in first_hops(rin, rout, send, recv, base, x, y, c, chips) + own(rin, rout, send, recv, base, x, y, c):
            cp.start()

    def middle(rin, rout, send, recv, base=0):
        x, y, c, chips = _place()
        for arrival, onward in passes(rout, send, recv, base, x, y, c, chips):
            arrival.wait_recv()
            onward.start()

    def finish(rin, rout, send, recv, base=0):
        x, y, c, chips = _place()
        for j, (cx, cy) in enumerate(chips):
            for t in range(ni):
                other = rout[t].at[2 * cx + cy, 1 - c]
                _remote(other, other, send, recv, base + 7 * t + 3 + j, (x, y, 1 - c)).wait_recv()
        for cp in own(rin, rout, send, recv, base, x, y, c):
            cp.wait_recv()
            cp.wait_send()
        for cp in first_hops(rin, rout, send, recv, base, x, y, c, chips):
            cp.wait_send()
        for _, onward in passes(rout, send, recv, base, x, y, c, chips):
            onward.wait_send()

    return _Rider(shards, [_sds((4,) + a.shape, a.dtype) for a in shards], 7 * ni, start, finish, middle)


def _pair_rider(grads):
    ni = len(grads)

    def copies(rin, rout, send, recv, base):
        x, y, c, _ = _place()
        return [_remote(rin[t].at[:, 1 - c], rout[t], send, recv, base + t, (x, y, 1 - c)) for t in range(ni)]

    def start(rin, rout, send, recv, base=0):
        for cp in copies(rin, rout, send, recv, base):
            cp.start()

    def finish(rin, rout, send, recv, base=0):
        for cp in copies(rin, rout, send, recv, base):
            cp.wait()

    return _Rider(grads, [_sds(g.shape[:1] + g.shape[2:], g.dtype) for g in grads], ni, start, finish)


def _ici_rider(pair_sums):
    ni = len(pair_sums)

    def copies(rin, rout, send, recv, base):
        x, y, c, chips = _place()
        return [_remote(rin[t].at[j], rout[t].at[j], send, recv, base + 3 * t + j, (cx, cy, c))
                for j, (cx, cy) in enumerate(chips) for t in range(ni)]

    def start(rin, rout, send, recv, base=0):
        for cp in copies(rin, rout, send, recv, base):
            cp.start()

    def finish(rin, rout, send, recv, base=0):
        for cp in copies(rin, rout, send, recv, base):
            cp.wait()

    return _Rider(pair_sums, [_sds((3,) + g.shape[1:], g.dtype) for g in pair_sums], 3 * ni, start, finish)


def _final_rider(summed, slots):
    ni = len(summed)

    def copies(rout, send, recv, base):
        x, y, c, _ = _place()
        return [(_remote(rout[t].at[slots[t], c], rout[t].at[slots[t], c], send, recv, base + t, (x, y, 1 - c)),
                 _remote(rout[t].at[slots[t], 1 - c], rout[t].at[slots[t], 1 - c], send, recv, base + t, (x, y, 1 - c)))
                for t in range(ni)]

    def start(rin, rout, send, recv, base=0):
        for mine, _ in copies(rout, send, recv, base):
            mine.start()

    def finish(rin, rout, send, recv, base=0):
        for mine, theirs in copies(rout, send, recv, base):
            mine.wait_send()
            theirs.wait_recv()

    return _Rider(summed, [_sds(a.shape, a.dtype) for a in summed], ni, start, finish,
                  aliases={t: t for t in range(ni)})


def _call(body, *, name, grid, in_specs, out_specs, out_shape, scratch_shapes, operands, rider=None):
    if rider is None:
        outs = pl.pallas_call(body, name=name, grid=grid, in_specs=in_specs, out_specs=out_specs, out_shape=out_shape,
                              scratch_shapes=scratch_shapes, compiler_params=_params(len(grid)))(*operands)
        return list(outs), []
    n_in, n_out, n_scr = len(in_specs), len(out_specs), len(scratch_shapes)
    r_in, r_out = len(rider.inputs), len(rider.out_shapes)
    steps = 1
    for g in grid:
        steps *= g
    mid = max(steps - 1 - MIDDLE_STEPS_BEFORE_END, 0)

    def full_body(*refs):
        own_in, rin = refs[:n_in], refs[n_in:n_in + r_in]
        own_out = refs[n_in + r_in:n_in + r_in + n_out]
        rout = refs[n_in + r_in + n_out:n_in + r_in + n_out + r_out]
        own_scr = refs[n_in + r_in + n_out + r_out:n_in + r_in + n_out + r_out + n_scr]
        send, recv = refs[-2], refs[-1]
        step = pl.program_id(0)
        for axis in range(1, len(grid)):
            step = step * grid[axis] + pl.program_id(axis)

        @pl.when(step == 0)
        def _():
            rider.start(rin, rout, send, recv)

        body(*own_in, *own_out, *own_scr)

        if rider.middle is not None:
            @pl.when(step == mid)
            def _():
                rider.middle(rin, rout, send, recv)

        @pl.when(step == steps - 1)
        def _():
            rider.finish(rin, rout, send, recv)

    outs = pl.pallas_call(
        full_body, name=name, grid=grid,
        in_specs=list(in_specs) + [ANY] * r_in, out_specs=list(out_specs) + [ANY] * r_out,
        out_shape=list(out_shape) + rider.out_shapes,
        scratch_shapes=list(scratch_shapes) + [pltpu.SemaphoreType.DMA((rider.n_sems,)), pltpu.SemaphoreType.DMA((rider.n_sems,))],
        input_output_aliases={n_in + a: n_out + b for a, b in rider.aliases.items()},
        compiler_params=_params(len(grid)),
    )(*operands, *rider.inputs)
    return list(outs[:n_out]), list(outs[n_out:])


def _run_rider(rider, name):
    r_in, r_out = len(rider.inputs), len(rider.out_shapes)

    def body(*refs):
        rin, rout, send, recv = refs[:r_in], refs[r_in:r_in + r_out], refs[-2], refs[-1]
        rider.start(rin, rout, send, recv)
        if rider.middle is not None:
            rider.middle(rin, rout, send, recv)
        rider.finish(rin, rout, send, recv)

    outs = pl.pallas_call(
        body, name=name, in_specs=[ANY] * r_in, out_specs=[ANY] * r_out, out_shape=rider.out_shapes,
        scratch_shapes=[pltpu.SemaphoreType.DMA((rider.n_sems,)), pltpu.SemaphoreType.DMA((rider.n_sems,))],
        input_output_aliases=rider.aliases,
    )(*rider.inputs)
    return list(outs)


def _fwd_mix_a(h, gain, conv_w, w_in, w_out, name, rider=None):
    s, d = h.shape
    e = MIX_WIDTH
    ts = min(TS_MIX, s)
    nt = s // ts

    def body(h_ref, gain_ref, cw_ref, win_hbm, wout_hbm, h1_ref, proj_ref, win_v, wout_v, carry, sems):
        i = pl.program_id(0)

        @pl.when(i == 0)
        def _():
            _copy_all([(win_hbm, win_v), (wout_hbm, wout_v)], sems)
            carry[...] = jnp.zeros_like(carry)

        hh = h_ref[...]
        xn, _ = _rms(hh)
        hnb = (xn * gain_ref[...]).astype(BF16)
        b = _dot(hnb, win_v[0])
        c = _dot(hnb, win_v[1])
        v = _dot(hnb, win_v[2])
        z = _dot(hnb, win_v[3])
        proj_ref[:, 0 * e:1 * e] = b.astype(BF16)
        proj_ref[:, 1 * e:2 * e] = c.astype(BF16)
        proj_ref[:, 2 * e:3 * e] = v.astype(BF16)
        proj_ref[:, 3 * e:4 * e] = z.astype(BF16)
        cv = c * v
        tail = carry[...]
        carry[...] = cv[ts - HALO:ts]
        conv = cw_ref[0:1, :] * _shift_down(cv, 2, tail) + cw_ref[1:2, :] * _shift_down(cv, 1, tail) + cw_ref[2:3, :] * cv
        mb = ((z * _sigmoid(z)) * (b * conv)).astype(BF16)
        h1_ref[...] = hh + _dot(mb, wout_v[...])

    row = lambda width: pl.BlockSpec((ts, width), lambda i: (i, 0))
    return _call(
        body, name=name, grid=(nt,),
        in_specs=[row(d), _full((1, d)), _full((8, e)), ANY, ANY],
        out_specs=[row(d), row(4 * e)],
        out_shape=[_sds((s, d), F32), _sds((s, 4 * e), BF16)],
        scratch_shapes=[pltpu.VMEM((4, d, e), BF16), pltpu.VMEM((e, d), BF16), pltpu.VMEM((HALO, e), F32),
                        pltpu.SemaphoreType.DMA((2,))],
        operands=[h, gain, conv_w, w_in, w_out], rider=rider)


def _fwd_mix_b(h, gain, scale, w_in, w_grp, w_out, name, rider=None):
    s, d = h.shape
    e = MIX_WIDTH
    ts = min(TS_MIX, s)
    nt = s // ts

    def body(h_ref, gain_ref, scale_ref, win_hbm, wgrp_hbm, wout_hbm, h1_ref, z_ref, mx_ref, dd_ref,
             win_v, wgrp_v, wout_v, carry, sems):
        i = pl.program_id(0)

        @pl.when(i == 0)
        def _():
            _copy_all([(win_hbm, win_v), (wout_hbm, wout_v)] + _grp_pairs(wgrp_hbm, wgrp_v), sems)
            carry[...] = jnp.zeros_like(carry)

        hh = h_ref[...]
        xn, _ = _rms(hh)
        hnb = (xn * gain_ref[...]).astype(BF16)
        u = jnp.concatenate([_dot(hnb, win_v[0]), _dot(hnb, win_v[1])], axis=1)
        z = jnp.concatenate([_dot(hnb, win_v[2]), _dot(hnb, win_v[3])], axis=1)
        z_ref[...] = z.astype(BF16)
        diff = (_pool_fwd(u, carry, i, ts) - u).astype(BF16)
        dd_ref[...] = diff
        mx = jnp.concatenate(
            [_dot(diff[:, g * GROUP_DIM:(g + 1) * GROUP_DIM], wgrp_v[g]) for g in range(N_GROUPS)], axis=1)
        mx_ref[...] = mx.astype(BF16)
        mb = ((z * _sigmoid(z)) * (mx * scale_ref[...])).astype(BF16)
        h1_ref[...] = hh + _dot(mb, wout_v[...])

    row = lambda width: pl.BlockSpec((ts, width), lambda i: (i, 0))
    return _call(
        body, name=name, grid=(nt,),
        in_specs=[row(d), _full((1, d)), _full((1, e)), ANY, ANY, ANY],
        out_specs=[row(d), row(e), row(e), row(e)],
        out_shape=[_sds((s, d), F32)] + [_sds((s, e), BF16)] * 3,
        scratch_shapes=[pltpu.VMEM((4, d, e // 2), BF16), pltpu.VMEM((N_GROUPS, GROUP_DIM, GROUP_DIM), BF16),
                        pltpu.VMEM((e, d), BF16), pltpu.VMEM((4, HALO, e), F32), pltpu.SemaphoreType.DMA((18,))],
        operands=[h, gain, scale, w_in, w_grp, w_out], rider=rider)


def _fwd_ple(h1, p, gain, w_gate, w_proj, layer, rider=None):
    s, d = h1.shape
    pd = p.shape[-1]
    ts = min(TS_PLE, s)
    nt = s // ts

    def body(h1_ref, p_ref, gain_ref, wg_hbm, wp_hbm, h2_ref, gate_ref, wg_v, wp_v, sems):
        @pl.when(pl.program_id(0) == 0)
        def _():
            _copy_all([(wg_hbm, wg_v), (wp_hbm, wp_v)], sems)

        hh = h1_ref[...]
        xn, _ = _rms(hh)
        hpb = (xn * gain_ref[...]).astype(BF16)
        gate = _sigmoid(_dot(hpb, wg_v[...]))
        pb = p_ref[...].astype(BF16)
        pe = jnp.concatenate([_dot(pb, wp_v[k]) for k in range(4)], axis=1)
        gate_ref[...] = gate.astype(BF16)
        h2_ref[...] = hh + gate * pe

    row = lambda width: pl.BlockSpec((ts, width), lambda i: (i, 0))
    return _call(
        body, name=f"fwd_ple{layer}", grid=(nt,),
        in_specs=[row(d), pl.BlockSpec((None, ts, pd), lambda i: (layer, i, 0)), _full((1, d)), ANY, ANY],
        out_specs=[row(d), row(d)],
        out_shape=[_sds((s, d), F32), _sds((s, d), BF16)],
        scratch_shapes=[pltpu.VMEM((d, d), BF16), pltpu.VMEM((4, pd, d // 4), BF16), pltpu.SemaphoreType.DMA((2,))],
        operands=[h1, p, gain, w_gate, w_proj], rider=rider)


def _loss_head(h, target, gain):
    s, d = h.shape
    ts = min(TS_PLE, s)
    nt = s // ts

    def body(h_ref, t_ref, gain_ref, dh_ref, loss_ref, dgain_ref):
        @pl.when(pl.program_id(0) == 0)
        def _():
            loss_ref[...] = jnp.zeros_like(loss_ref)
            dgain_ref[...] = jnp.zeros_like(dgain_ref)

        xn, r = _rms(h_ref[...])
        err = xn * gain_ref[...] - t_ref[...]
        part = 0.5 * jnp.sum(jnp.mean(err * err, axis=-1, keepdims=True), axis=0, keepdims=True)
        loss_ref[...] += jnp.broadcast_to(part, loss_ref.shape)
        dh, dgain = _rms_bwd(err * (1.0 / d), xn, r, gain_ref[...])
        dh_ref[...] = dh
        dgain_ref[...] += dgain

    row = pl.BlockSpec((ts, d), lambda i: (i, 0))
    outs, _ = _call(
        body, name="loss_head", grid=(nt,),
        in_specs=[row, row, _full((1, d))],
        out_specs=[row, _full((8, 128)), _full((1, d))],
        out_shape=[_sds((s, d), F32), _sds((8, 128), F32), _sds((1, d), F32)],
        scratch_shapes=[], operands=[h, target, gain])
    return outs


def _bwd_ple(dh2, h1, gate, p, gain, w_gate, w_proj, layer, rider=None):
    s, d = dh2.shape
    pd = p.shape[-1]
    ts = min(TS_PLE, s)
    nt = s // ts
    qd = d // 4

    def body(dh2_ref, h1_ref, gate_ref, p_ref, gain_ref, wg_hbm, wp_hbm, dh1_ref, dgain_ref, dwg_hbm, dwp_hbm,
             wg_v, wp_v, acc_g, acc_p, sems):
        i = pl.program_id(0)

        @pl.when(i == 0)
        def _():
            _copy_all([(wg_hbm, wg_v), (wp_hbm, wp_v)], sems)
            dgain_ref[...] = jnp.zeros_like(dgain_ref)
            acc_g[...] = jnp.zeros_like(acc_g)
            acc_p[...] = jnp.zeros_like(acc_p)

        g2 = dh2_ref[...]
        gate_f = gate_ref[...].astype(F32)
        xn, r = _rms(h1_ref[...])
        hpb = (xn * gain_ref[...]).astype(BF16)
        pb = p_ref[...].astype(BF16)
        pe = jnp.concatenate([_dot(pb, wp_v[k]) for k in range(4)], axis=1)
        dpeb = (g2 * gate_f).astype(BF16)
        dab = ((g2 * pe) * (gate_f * (1.0 - gate_f))).astype(BF16)
        acc_g[...] += _dot_tn(hpb, dab)
        for k in range(4):
            acc_p[k] += _dot_tn(pb, dpeb[:, k * qd:(k + 1) * qd])
        dhp = _dot_nt(dab, wg_v[...])
        dh, dgain = _rms_bwd(dhp, xn, r, gain_ref[...])
        dh1_ref[...] = g2 + dh
        dgain_ref[...] += dgain

        @pl.when(i == nt - 1)
        def _():
            _copy_all([(acc_g, dwg_hbm), (acc_p, dwp_hbm)], sems)

    row = pl.BlockSpec((ts, d), lambda i: (i, 0))
    return _call(
        body, name=f"bwd_ple{layer}", grid=(nt,),
        in_specs=[row, row, row, pl.BlockSpec((None, ts, pd), lambda i: (layer, i, 0)), _full((1, d)), ANY, ANY],
        out_specs=[row, _full((1, d)), ANY, ANY],
        out_shape=[_sds((s, d), F32), _sds((1, d), F32), _sds((d, d), F32), _sds((4, pd, qd), F32)],
        scratch_shapes=[pltpu.VMEM((d, d), BF16), pltpu.VMEM((4, pd, qd), BF16), pltpu.VMEM((d, d), F32),
                        pltpu.VMEM((4, pd, qd), F32), pltpu.SemaphoreType.DMA((2,))],
        operands=[dh2, h1, gate, p, gain, w_gate, w_proj], rider=rider)


def _bwd_mix_a(dh1, h, proj, gain, conv_w, w_in, w_out, name, rider=None):
    s, d = dh1.shape
    e = MIX_WIDTH
    ts = min(TS_MIX, s)
    nt = s // ts
    hb = 16
    per = ts // hb

    def body(dh1_ref, h_ref, proj_ref, ch_ref, vh_ref, gain_ref, cw_ref, win_hbm, wout_hbm,
             dh_ref, dcw_ref, dgain_ref, dwin_hbm, dwout_hbm, win_v, wout_v, acc_in, acc_out, carry, sems):
        i = pl.program_id(0)
        tile = nt - 1 - i

        @pl.when(i == 0)
        def _():
            _copy_all([(win_hbm, win_v), (wout_hbm, wout_v)], sems)
            carry[...] = jnp.zeros_like(carry)
            dcw_ref[...] = jnp.zeros_like(dcw_ref)
            dgain_ref[...] = jnp.zeros_like(dgain_ref)
            acc_in[...] = jnp.zeros_like(acc_in)
            acc_out[...] = jnp.zeros_like(acc_out)

        b = proj_ref[:, 0 * e:1 * e].astype(F32)
        c = proj_ref[:, 1 * e:2 * e].astype(F32)
        v = proj_ref[:, 2 * e:3 * e].astype(F32)
        z = proj_ref[:, 3 * e:4 * e].astype(F32)
        cv = c * v
        prev = (ch_ref[...].astype(F32) * vh_ref[...].astype(F32))[hb - HALO:hb]
        tail = jnp.where(tile > 0, prev, jnp.zeros_like(prev))
        cv1 = _shift_down(cv, 1, tail)
        cv2 = _shift_down(cv, 2, tail)
        conv = cw_ref[0:1, :] * cv2 + cw_ref[1:2, :] * cv1 + cw_ref[2:3, :] * cv
        sig = _sigmoid(z)
        sz = z * sig
        y = b * conv
        dh1 = dh1_ref[...]
        dh1b = dh1.astype(BF16)
        acc_out[...] += _dot_tn((sz * y).astype(BF16), dh1b)
        dm = _dot_nt(dh1b, wout_v[...])
        dz = (dm * y) * (sig * (1.0 + z * (1.0 - sig)))
        dy = dm * sz
        db = dy * conv
        dconv = dy * b
        head = carry[...]
        carry[...] = dconv[0:HALO]
        dcv = cw_ref[2:3, :] * dconv + cw_ref[1:2, :] * _shift_up(dconv, 1, head) + cw_ref[0:1, :] * _shift_up(dconv, 2, head)
        dcw_ref[0:1, :] += jnp.sum(dconv * cv2, axis=0, keepdims=True)
        dcw_ref[1:2, :] += jnp.sum(dconv * cv1, axis=0, keepdims=True)
        dcw_ref[2:3, :] += jnp.sum(dconv * cv, axis=0, keepdims=True)
        parts = [db.astype(BF16), (dcv * v).astype(BF16), (dcv * c).astype(BF16), dz.astype(BF16)]
        xn, r = _rms(h_ref[...])
        hnb = (xn * gain_ref[...]).astype(BF16)
        for q in range(4):
            acc_in[q] += _dot_tn(hnb, parts[q])
        dhn = _dot_nt(parts[0], win_v[0]) + _dot_nt(parts[1], win_v[1]) + _dot_nt(parts[2], win_v[2]) + _dot_nt(parts[3], win_v[3])
        dh, dgain = _rms_bwd(dhn, xn, r, gain_ref[...])
        dh_ref[...] = dh1 + dh
        dgain_ref[...] += dgain

        @pl.when(i == nt - 1)
        def _():
            _copy_all([(acc_in, dwin_hbm), (acc_out, dwout_hbm)], sems)

    row = lambda width: pl.BlockSpec((ts, width), lambda i: (nt - 1 - i, 0))
    halo = lambda col: pl.BlockSpec((hb, e), lambda i: (jnp.maximum((nt - 1 - i) * per - 1, 0), col))
    return _call(
        body, name=name, grid=(nt,),
        in_specs=[row(d), row(d), row(4 * e), halo(1), halo(2), _full((1, d)), _full((8, e)), ANY, ANY],
        out_specs=[row(d), _full((8, e)), _full((1, d)), ANY, ANY],
        out_shape=[_sds((s, d), F32), _sds((8, e), F32), _sds((1, d), F32), _sds((4, d, e), F32), _sds((e, d), F32)],
        scratch_shapes=[pltpu.VMEM((4, d, e), BF16), pltpu.VMEM((e, d), BF16), pltpu.VMEM((4, d, e), F32),
                        pltpu.VMEM((e, d), F32), pltpu.VMEM((HALO, e), F32), pltpu.SemaphoreType.DMA((2,))],
        operands=[dh1, h, proj, proj, proj, gain, conv_w, w_in, w_out], rider=rider)


def _bwd_mix_b(dh1, h, z, mx, diff, gain, scale, w_in, w_grp, w_out, name, rider=None):
    s, d = dh1.shape
    e = MIX_WIDTH
    ts = min(TS_MIX, s)
    nt = s // ts
    half = e // 2

    def body(dh1_ref, h_ref, z_ref, mx_ref, dd_ref, gain_ref, scale_ref, win_hbm, wgrp_hbm, wout_hbm,
             dh_ref, dscale_ref, dgain_ref, dwin_hbm, dwgrp_hbm, dwout_hbm,
             win_v, wgrp_v, wout_v, acc_in, acc_grp, acc_out, carry, sems):
        i = pl.program_id(0)
        tile = nt - 1 - i

        @pl.when(i == 0)
        def _():
            _copy_all([(win_hbm, win_v), (wout_hbm, wout_v)] + _grp_pairs(wgrp_hbm, wgrp_v), sems)
            carry[...] = jnp.zeros_like(carry)
            dscale_ref[...] = jnp.zeros_like(dscale_ref)
            dgain_ref[...] = jnp.zeros_like(dgain_ref)
            acc_in[...] = jnp.zeros_like(acc_in)
            acc_grp[...] = jnp.zeros_like(acc_grp)
            acc_out[...] = jnp.zeros_like(acc_out)

        zf = z_ref[...].astype(F32)
        mxf = mx_ref[...].astype(F32)
        sig = _sigmoid(zf)
        sz = zf * sig
        mixed = mxf * scale_ref[...]
        dh1 = dh1_ref[...]
        dh1b = dh1.astype(BF16)
        acc_out[...] += _dot_tn((sz * mixed).astype(BF16), dh1b)
        dm = _dot_nt(dh1b, wout_v[...])
        dz = (dm * mixed) * (sig * (1.0 + zf * (1.0 - sig)))
        dmixed = dm * sz
        dscale_ref[...] += jnp.sum(dmixed * mxf, axis=0, keepdims=True)
        dmxb = (dmixed * scale_ref[...]).astype(BF16)
        diff = dd_ref[...]
        for g in range(N_GROUPS):
            cols = slice(g * GROUP_DIM, (g + 1) * GROUP_DIM)
            acc_grp[g] += _dot_tn(diff[:, cols], dmxb[:, cols])
        ddiff = jnp.concatenate(
            [_dot_nt(dmxb[:, g * GROUP_DIM:(g + 1) * GROUP_DIM], wgrp_v[g]) for g in range(N_GROUPS)], axis=1)
        dub = (_pool_bwd(ddiff, carry, tile, ts) - ddiff).astype(BF16)
        dzb = dz.astype(BF16)
        parts = [dub[:, 0:half], dub[:, half:e], dzb[:, 0:half], dzb[:, half:e]]
        xn, r = _rms(h_ref[...])
        hnb = (xn * gain_ref[...]).astype(BF16)
        for k in range(4):
            acc_in[k] += _dot_tn(hnb, parts[k])
        dhn = _dot_nt(parts[0], win_v[0]) + _dot_nt(parts[1], win_v[1]) + _dot_nt(parts[2], win_v[2]) + _dot_nt(parts[3], win_v[3])
        dh, dgain = _rms_bwd(dhn, xn, r, gain_ref[...])
        dh_ref[...] = dh1 + dh
        dgain_ref[...] += dgain

        @pl.when(i == nt - 1)
        def _():
            _copy_all([(acc_in, dwin_hbm), (acc_out, dwout_hbm)] + [(v, hb_) for hb_, v in _grp_pairs(dwgrp_hbm, acc_grp)], sems)

    row = lambda width: pl.BlockSpec((ts, width), lambda i: (nt - 1 - i, 0))
    return _call(
        body, name=name, grid=(nt,),
        in_specs=[row(d), row(d), row(e), row(e), row(e), _full((1, d)), _full((1, e)), ANY, ANY, ANY],
        out_specs=[row(d), _full((1, e)), _full((1, d)), ANY, ANY, ANY],
        out_shape=[_sds((s, d), F32), _sds((1, e), F32), _sds((1, d), F32), _sds((4, d, half), F32),
                   _sds((4, N_GROUPS, GROUP_DIM // 4, GROUP_DIM), F32), _sds((e, d), F32)],
        scratch_shapes=[pltpu.VMEM((4, d, half), BF16), pltpu.VMEM((N_GROUPS, GROUP_DIM, GROUP_DIM), BF16),
                        pltpu.VMEM((e, d), BF16), pltpu.VMEM((4, d, half), F32),
                        pltpu.VMEM((N_GROUPS, GROUP_DIM, GROUP_DIM), F32), pltpu.VMEM((e, d), F32),
                        pltpu.VMEM((4, HALO, e), F32), pltpu.SemaphoreType.DMA((18,))],
        operands=[dh1, h, z, mx, diff, gain, scale, w_in, w_grp, w_out], rider=rider)


def _first_gather(shards, small):
    rider = _gather_rider(shards)
    ni = len(shards)

    def body(*refs):
        rin, small_src = refs[:ni], refs[ni]
        rout, small_dst = refs[ni + 1:2 * ni + 1], refs[2 * ni + 1]
        send, recv, ssend, srecv = refs[2 * ni + 2:]
        x, y, c, chips = _place()
        me = 2 * x + y
        peers = [(cx, cy, c) for cx, cy in chips] + [(x, y, 1 - c)]
        vec = [_remote(small_src, small_dst.at[me], ssend, srecv, j, to) for j, to in enumerate(peers)]
        for cp in vec:
            cp.start()
        rider.start(rin, rout, send, recv)
        rider.middle(rin, rout, send, recv)
        rider.finish(rin, rout, send, recv)
        for j, (px, py, _) in enumerate(peers):
            _remote(small_src, small_dst.at[2 * px + py], ssend, srecv, j, peers[j]).wait_recv()
        for cp in vec:
            cp.wait_send()

    outs = pl.pallas_call(
        body, name="first_gather", in_specs=[ANY] * (ni + 1), out_specs=[ANY] * (ni + 1),
        out_shape=rider.out_shapes + [_sds((4,) + small.shape, small.dtype)],
        scratch_shapes=[pltpu.SemaphoreType.DMA((rider.n_sems,)), pltpu.SemaphoreType.DMA((rider.n_sems,)),
                        pltpu.SemaphoreType.DMA((4,)), pltpu.SemaphoreType.DMA((4,))],
    )(*shards, small)
    return list(outs[:ni]), outs[ni]


def _vector_rider(pack):
    flips = [(fx, fy, fc) for fx in (0, 1) for fy in (0, 1) for fc in (0, 1)][1:]

    def copies(rin, rout, send, recv, base):
        x, y, c, _ = _place()
        me = 4 * x + 2 * y + c
        peers = [(1 - x if fx else x, 1 - y if fy else y, 1 - c if fc else c) for fx, fy, fc in flips]
        own = pltpu.make_async_copy(rin[0], rout[0].at[me], send.at[base + 7])
        out = [_remote(rin[0], rout[0].at[me], send, recv, base + r, peer) for r, peer in enumerate(peers)]
        back = [_remote(rin[0], rout[0].at[4 * px + 2 * py + pc], send, recv, base + r, (px, py, pc))
                for r, (px, py, pc) in enumerate(peers)]
        return own, out, back

    def start(rin, rout, send, recv, base=0):
        own, out, _ = copies(rin, rout, send, recv, base)
        own.start()
        for cp in out:
            cp.start()

    def finish(rin, rout, send, recv, base=0):
        own, out, back = copies(rin, rout, send, recv, base)
        for cp in back:
            cp.wait_recv()
        for cp in out:
            cp.wait_send()
        own.wait()

    return _Rider([pack], [_sds((8,) + pack.shape, pack.dtype)], 8, start, finish)


def _vector_sum(landed):
    _, rows, d = landed.shape

    def body(l_ref, out_ref):
        total = l_ref[0]
        for dev in range(1, 8):
            total = total + l_ref[dev]
        out_ref[...] = total

    vmem = pl.BlockSpec(memory_space=pltpu.VMEM)
    return pl.pallas_call(body, name="vector_sum", in_specs=[vmem], out_specs=vmem, out_shape=_sds((rows, d), F32))(landed)


def _ew_rows(rows):
    return min(TR_EW, rows)


def _pair_sum(grad, sibling_rows, place, name):
    _, _, rh, cols = grad.shape
    tr = _ew_rows(rh)

    def body(place_ref, g_ref, s_ref, out_ref):
        out_ref[...] = (g_ref[...] + s_ref[...]).astype(BF16)

    def chip_of(j, pos):
        return jnp.bitwise_xor(pos[0], jnp.where(j == 2, 3, 2 - j))

    grid_spec = pltpu.PrefetchScalarGridSpec(
        num_scalar_prefetch=1, grid=(3, rh // tr),
        in_specs=[pl.BlockSpec((None, None, tr, cols), lambda j, r, pos: (chip_of(j, pos), pos[1], r, 0)),
                  pl.BlockSpec((None, tr, cols), lambda j, r, pos: (chip_of(j, pos), r, 0))],
        out_specs=pl.BlockSpec((None, tr, cols), lambda j, r, pos: (j, r, 0)))
    return pl.pallas_call(body, name=name, grid_spec=grid_spec, out_shape=_sds((3, rh, cols), BF16),
                          compiler_params=_params(2))(place, grad, sibling_rows)


def _final_sum(grad, sibling_rows, landed, place, stack, slot, n_slots, name):
    _, _, rh, cols = grad.shape
    tr = _ew_rows(rh)

    def body(place_ref, g_ref, s_ref, l_ref, *rest):
        out_ref = rest[-1]
        total = g_ref[...] + s_ref[...]
        for j in range(3):
            total = total + l_ref[j].astype(F32)
        out_ref[...] = total

    in_specs = [pl.BlockSpec((None, None, tr, cols), lambda r, pos: (pos[0], pos[1], r, 0)),
                pl.BlockSpec((None, tr, cols), lambda r, pos: (pos[0], r, 0)),
                pl.BlockSpec((3, tr, cols), lambda r, pos: (0, r, 0))]
    operands = [place, grad, sibling_rows, landed]
    aliases = {}
    if stack is not None:
        in_specs.append(ANY)
        operands.append(stack)
        aliases = {4: 0}
    grid_spec = pltpu.PrefetchScalarGridSpec(
        num_scalar_prefetch=1, grid=(rh // tr,), in_specs=in_specs,
        out_specs=pl.BlockSpec((None, None, tr, cols), lambda r, pos: (slot, pos[1], r, 0)))
    return pl.pallas_call(body, name=name, grid_spec=grid_spec, out_shape=_sds((n_slots, 2, rh, cols), F32),
                          input_output_aliases=aliases, compiler_params=_params(1))(*operands)


def _adamw(g, w, m, v, name):
    rows, cols = g.shape
    tr = _ew_rows(rows)

    def body(g_ref, w_ref, m_ref, v_ref, delta_ref, nm_ref, nv_ref):
        gg = g_ref[...]
        nm = ADAM_B1 * m_ref[...] + (1.0 - ADAM_B1) * gg
        nv = ADAM_B2 * v_ref[...] + (1.0 - ADAM_B2) * (gg * gg)
        m_hat = nm / (1.0 - ADAM_B1 ** ADAM_STEP)
        v_hat = nv / (1.0 - ADAM_B2 ** ADAM_STEP)
        delta_ref[...] = -ADAM_LR * (m_hat / (jnp.sqrt(v_hat) + ADAM_EPS) + ADAM_WD * w_ref[...])
        nm_ref[...] = nm
        nv_ref[...] = nv

    spec = pl.BlockSpec((tr, cols), lambda i: (i, 0))
    return pl.pallas_call(
        body, name=name, grid=(rows // tr,), in_specs=[spec] * 4, out_specs=[spec] * 3,
        out_shape=[_sds((rows, cols), F32)] * 3, compiler_params=_params(),
    )(g, w, m, v)


BIG = ["a_w_in", "a_w_out", "b_w_in", "b_w_grp", "b_w_out", "ple_w_gate", "ple_w_proj"]

GATHER_PLAN = {
    "first": [("a_w_in", 0), ("a_w_out", 0)],
    "mix0": [("ple_w_gate", 0), ("ple_w_proj", 0), ("b_w_in", 0), ("b_w_grp", 0), ("b_w_out", 0)],
    "ple0": [("ple_w_gate", 1), ("ple_w_proj", 1)],
    "mix1": [("a_w_in", 1)],
    "ple1": [("ple_w_gate", 2), ("ple_w_proj", 2), ("a_w_out", 1)],
    "mix2": [("b_w_in", 1), ("b_w_grp", 1), ("b_w_out", 1), ("ple_w_gate", 3), ("ple_w_proj", 3)],
}


def _as_2d(name, a):
    if name == "b_w_grp":
        return a.reshape(a.shape[0], N_GROUPS * (GROUP_DIM // 4), GROUP_DIM)
    return a


def kernel(x, p, norm_mix, a_w_in, a_w_conv, a_w_out, b_w_in, b_w_grp, b_scale, b_w_out, ple_norm, ple_w_gate, ple_w_proj, final_norm, loss_target, m_norm_mix, m_a_w_in, m_a_w_conv, m_a_w_out, m_b_w_in, m_b_w_grp, m_b_scale, m_b_w_out, m_ple_norm, m_ple_w_gate, m_ple_w_proj, m_final_norm, v_norm_mix, v_a_w_in, v_a_w_conv, v_a_w_out, v_b_w_in, v_b_w_grp, v_b_scale, v_b_w_out, v_ple_norm, v_ple_w_gate, v_ple_w_proj, v_final_norm):
    d, e = D_MODEL, MIX_WIDTH
    s = x.shape[1]
    cx, cy, cc = lax.axis_index("x"), lax.axis_index("y"), lax.axis_index("c")
    chip = 2 * cx + cy
    place = jnp.stack([chip, cc]).astype(jnp.int32)

    weights = dict(a_w_in=a_w_in, a_w_out=a_w_out, b_w_in=b_w_in, b_w_grp=b_w_grp, b_w_out=b_w_out,
                   ple_w_gate=ple_w_gate, ple_w_proj=ple_w_proj)
    moms = dict(a_w_in=m_a_w_in, a_w_out=m_a_w_out, b_w_in=m_b_w_in, b_w_grp=m_b_w_grp, b_w_out=m_b_w_out,
                ple_w_gate=m_ple_w_gate, ple_w_proj=m_ple_w_proj)
    vars_ = dict(a_w_in=v_a_w_in, a_w_out=v_a_w_out, b_w_in=v_b_w_in, b_w_grp=v_b_w_grp, b_w_out=v_b_w_out,
                 ple_w_gate=v_ple_w_gate, ple_w_proj=v_ple_w_proj)
    w2d = {nm: _as_2d(nm, weights[nm]) for nm in BIG}
    bf = {nm: w2d[nm].astype(BF16) for nm in BIG}

    def shard_of(key):
        nm, j = key
        a = bf[nm][j]
        return a.reshape(2, a.shape[0] // 2, a.shape[1])

    gathered = {}

    def gather_rider(host):
        keys = GATHER_PLAN.get(host)
        return _gather_rider([shard_of(k) for k in keys]) if keys else None

    def keep(host, landed):
        for k, a in zip(GATHER_PLAN.get(host, []), landed):
            gathered[k] = a

    def weight(nm, j):
        a = gathered[(nm, j)]
        shapes = {"a_w_in": (4, d, e), "a_w_out": (e, d), "b_w_in": (4, d, e // 2),
                  "b_w_grp": (4, N_GROUPS, GROUP_DIM // 4, GROUP_DIM), "b_w_out": (e, d), "ple_w_gate": (d, d),
                  "ple_w_proj": (4, PLE_DIM, d // 4)}
        return a.reshape(shapes[nm])

    small = jnp.concatenate([a_w_conv.reshape(6, e // 4), b_scale], axis=0)
    landed, small_full = _first_gather([shard_of(k) for k in GATHER_PLAN["first"]], small)
    keep("first", landed)
    small_full = small_full.transpose(1, 0, 2).reshape(8, e)
    conv_w = [jnp.concatenate([small_full[3 * j:3 * j + 3], jnp.zeros((5, e), F32)], axis=0) for j in range(2)]
    scale_w = [small_full[6 + j:7 + j] for j in range(2)]

    p3 = p.reshape(DEPTH, s, PLE_DIM)
    mix_gain = [norm_mix[i:i + 1] for i in range(DEPTH)]
    ple_gain = [ple_norm[i:i + 1] for i in range(DEPTH)]

    h = x.reshape(s, d)
    saved = []
    for i in range(DEPTH):
        j = i // 2
        rider = gather_rider(f"mix{i}")
        if i % 2 == 0:
            (h1, proj), landed = _fwd_mix_a(h, mix_gain[i], conv_w[j], weight("a_w_in", j), weight("a_w_out", j),
                                            f"fwd_mix_a{j}", rider)
            mix = dict(proj=proj)
        else:
            (h1, zb, mx, diff), landed = _fwd_mix_b(h, mix_gain[i], scale_w[j], weight("b_w_in", j), weight("b_w_grp", j),
                                                    weight("b_w_out", j), f"fwd_mix_b{j}", rider)
            mix = dict(z=zb, mx=mx, diff=diff)
        keep(f"mix{i}", landed)
        (h2, gate), landed = _fwd_ple(h1, p3, ple_gain[i], weight("ple_w_gate", i), weight("ple_w_proj", i), i,
                                      gather_rider(f"ple{i}"))
        keep(f"ple{i}", landed)
        saved.append(dict(h=h, h1=h1, gate=gate, **mix))
        h = h2

    dh, loss_part, d_final = _loss_head(h, loss_target.reshape(s, d), final_norm.reshape(1, d))

    n_slots = {nm: weights[nm].shape[0] for nm in BIG}
    stacks = {nm: None for nm in BIG}

    class Group:
        def __init__(self, keys, grads):
            self.keys, self.stage = keys, 0
            self.g32 = [g.reshape(4, 2, w2d[nm].shape[1] // 2, w2d[nm].shape[2]) for (nm, _), g in zip(keys, grads)]

        def rider(self):
            if self.stage == 0:
                return _pair_rider(self.g32)
            if self.stage == 1:
                return _ici_rider(self.pair_sums)
            return _final_rider([stacks[nm] for nm, _ in self.keys], [j for _, j in self.keys])

        def advance(self, landed):
            if self.stage == 0:
                self.from_sibling = landed
                self.pair_sums = [_pair_sum(g, sb, place, f"pair_sum_{nm}{j}")
                                  for (nm, j), g, sb in zip(self.keys, self.g32, landed)]
            elif self.stage == 1:
                for (nm, j), g, sb, ld in zip(self.keys, self.g32, self.from_sibling, landed):
                    stacks[nm] = _final_sum(g, sb, ld, place, stacks[nm], j, n_slots[nm], f"final_sum_{nm}{j}")
            else:
                for (nm, _), a in zip(self.keys, landed):
                    stacks[nm] = a
            self.stage += 1

    active = []

    def riders_now():
        parts = [g.rider() for g in active]
        return parts, _merge(parts)

    def advance_all(parts, landed):
        for g, l in zip(list(active), _split(landed, parts)):
            g.advance(l)
            if g.stage == 3:
                active.remove(g)

    d_mix_gain, d_ple_gain = [None] * DEPTH, [None] * DEPTH
    d_conv, d_scale = [None] * 2, [None] * 2
    for i in reversed(range(DEPTH)):
        j = i // 2
        sv = saved[i]
        parts, rider = riders_now()
        (dh1, d_ple_gain[i], dwg, dwp), landed = _bwd_ple(
            dh, sv["h1"], sv["gate"], p3, ple_gain[i], weight("ple_w_gate", i), weight("ple_w_proj", i), i, rider)
        advance_all(parts, landed)
        active.append(Group([("ple_w_gate", i), ("ple_w_proj", i)], [dwg, dwp]))
        parts, rider = riders_now()
        if i % 2 == 0:
            (dh, d_conv[j], d_mix_gain[i], dwin, dwout), landed = _bwd_mix_a(
                dh1, sv["h"], sv["proj"], mix_gain[i], conv_w[j], weight("a_w_in", j), weight("a_w_out", j),
                f"bwd_mix_a{j}", rider)
            new = Group([("a_w_in", j), ("a_w_out", j)], [dwin, dwout])
        else:
            (dh, d_scale[j], d_mix_gain[i], dwin, dwgrp, dwout), landed = _bwd_mix_b(
                dh1, sv["h"], sv["z"], sv["mx"], sv["diff"], mix_gain[i], scale_w[j], weight("b_w_in", j),
                weight("b_w_grp", j), weight("b_w_out", j), f"bwd_mix_b{j}", rider)
            new = Group([("b_w_in", j), ("b_w_grp", j), ("b_w_out", j)], [dwin, dwgrp, dwout])
        advance_all(parts, landed)
        active.append(new)
    grad_x = dh.reshape(1, s, d)

    pack = jnp.concatenate(
        d_mix_gain + d_ple_gain + [d_final] + [d_conv[0][0:3], d_conv[1][0:3]] + d_scale
        + [jnp.tile(loss_part[0:1], (1, d // 128)), jnp.zeros((SMALL_ROWS - 18, d), F32)], axis=0)
    vectors = _vector_rider(pack)
    tail = 0
    while active:
        parts, _ = riders_now()
        extra = [vectors] if tail == 0 else []
        landed = _run_rider(_merge(parts + extra), f"tail_exchange{tail}")
        if extra:
            total = _vector_sum(_split(landed, parts + extra)[-1][0])
        advance_all(parts, landed)
        tail += 1
    loss = total[17, 0]

    out_grad, out_delta, out_m, out_v = {}, {}, {}, {}
    for nm in BIG:
        shape = weights[nm].shape
        flat = (w2d[nm].shape[0] * w2d[nm].shape[1], w2d[nm].shape[2])
        g2 = stacks[nm].reshape(flat)
        delta, new_m, new_v = _adamw(g2, w2d[nm].reshape(flat), _as_2d(nm, moms[nm]).reshape(flat),
                                     _as_2d(nm, vars_[nm]).reshape(flat), f"adamw_{nm}")
        out_grad[nm], out_delta[nm] = g2.reshape(shape), delta.reshape(shape)
        out_m[nm], out_v[nm] = new_m.reshape(shape), new_v.reshape(shape)

    rep_rows = 16
    rep = lambda a, b_, c_: jnp.concatenate([a, b_, c_.reshape(1, d), jnp.zeros((rep_rows - 9, d), F32)], axis=0)
    rep_delta, rep_m, rep_v = _adamw(
        jnp.concatenate([total[0:9], jnp.zeros((rep_rows - 9, d), F32)], axis=0),
        rep(norm_mix, ple_norm, final_norm), rep(m_norm_mix, m_ple_norm, m_final_norm),
        rep(v_norm_mix, v_ple_norm, v_final_norm), "adamw_gains")
    mine_cols = lax.dynamic_slice_in_dim(total[9:17], chip * (e // 4), e // 4, axis=1)
    col = lambda a, b_: jnp.concatenate([a.reshape(6, e // 4), b_], axis=0)
    col_delta, col_m, col_v = _adamw(mine_cols, col(a_w_conv, b_scale), col(m_a_w_conv, m_b_scale),
                                     col(v_a_w_conv, v_b_scale), "adamw_cols")

    def unpack(rep_a, col_a):
        return dict(norm_mix=rep_a[0:4], ple_norm=rep_a[4:8], final_norm=rep_a[8],
                    a_w_conv=col_a[0:6].reshape(2, 3, e // 4), b_scale=col_a[6:8])

    small_out = [unpack(total, mine_cols), unpack(rep_delta, col_delta), unpack(rep_m, col_m), unpack(rep_v, col_v)]
    order = ["norm_mix", "a_w_in", "a_w_conv", "a_w_out", "b_w_in", "b_w_grp", "b_scale", "b_w_out", "ple_norm",
             "ple_w_gate", "ple_w_proj", "final_norm"]
    outs = [loss, grad_x]
    for big, small_d in zip([out_grad, out_delta, out_m, out_v], small_out):
        outs += [big[nm] if nm in big else small_d[nm] for nm in order]
    return tuple(outs)
```

```python
import jax
import jax.numpy as jnp
from jax import lax
from jax.experimental import pallas as pl
from jax.experimental.pallas import tpu as pltpu

F32 = jnp.float32
BF16 = jnp.bfloat16
MESH = pl.DeviceIdType.MESH

D_MODEL = 1024
MIX_WIDTH = 1024
PLE_DIM = 256
N_GROUPS = 4
GROUP_DIM = 256
POOL_WINDOWS = (2, 4, 8, 16)
DEPTH = 4
EPS = 1e-6

ADAM_LR = 0.001
ADAM_B1 = 0.9
ADAM_B2 = 0.999
ADAM_EPS = 1e-08
ADAM_WD = 0.01
ADAM_STEP = 10

HALO = 8
TS_MIX = 256
TS_PLE = 512
TR_EW = 512
VMEM_LIMIT = 56 * 1024 * 1024
SMALL_ROWS = 24
JOB_BLOCK_BYTES = 512 * 1024
MIDDLE_STEPS_BEFORE_END = 1

ANY = pl.BlockSpec(memory_space=pl.ANY)


def _sds(shape, dtype):
    return jax.ShapeDtypeStruct(shape, dtype)


def _full(shape):
    nd = len(shape)
    return pl.BlockSpec(shape, lambda *_: (0,) * nd)


def _params(n_axes=1):
    return pltpu.CompilerParams(dimension_semantics=("arbitrary",) * n_axes, vmem_limit_bytes=VMEM_LIMIT)


def _dot(a, b):
    return jnp.dot(a, b, preferred_element_type=F32)


def _dot_nt(a, b):
    return lax.dot_general(a, b, (((1,), (1,)), ((), ())), preferred_element_type=F32)


def _dot_tn(a, b):
    return lax.dot_general(a, b, (((0,), (0,)), ((), ())), preferred_element_type=F32)


def _sigmoid(z):
    return 1.0 / (1.0 + jnp.exp(-z))


def _shift_down(x, k, tail):
    rolled = pltpu.roll(x, k, 0)
    rt = tail if k % HALO == 0 else pltpu.roll(tail, k % HALO, 0)
    row = lax.broadcasted_iota(jnp.int32, rt.shape, 0)
    head = jnp.where(row < k, rt, rolled[0:HALO])
    return jnp.concatenate([head, rolled[HALO:]], axis=0)


def _shift_up(x, k, head_next):
    n = x.shape[0]
    rolled = pltpu.roll(x, n - k, 0)
    rh = head_next if k % HALO == 0 else pltpu.roll(head_next, HALO - k % HALO, 0)
    row = lax.broadcasted_iota(jnp.int32, rh.shape, 0)
    tail = jnp.where(row >= HALO - k, rh, rolled[n - HALO:n])
    return jnp.concatenate([rolled[:n - HALO], tail], axis=0)


def _inv_counts(tile, ts):
    t = tile * ts + lax.broadcasted_iota(jnp.int32, (ts, 1), 0)
    return [1.0 / jnp.minimum(t + 1, w).astype(F32) for w in POOL_WINDOWS]


def _pool_fwd(u, carry, tile, ts):
    inv = _inv_counts(tile, ts)
    outs = []
    for g, w in enumerate(POOL_WINDOWS):
        cols = slice(g * GROUP_DIM, (g + 1) * GROUP_DIM)
        s = u[:, cols]
        level, k = 0, 1
        while k < w:
            tail = carry[level, :, cols]
            carry[level, :, cols] = s[ts - HALO:ts]
            s = s + _shift_down(s, k, tail)
            level, k = level + 1, k * 2
        outs.append(s * inv[g])
    return jnp.concatenate(outs, axis=1)


def _pool_bwd(dd, carry, tile, ts):
    inv = _inv_counts(tile, ts)
    outs = []
    for g, w in enumerate(POOL_WINDOWS):
        cols = slice(g * GROUP_DIM, (g + 1) * GROUP_DIM)
        q = dd[:, cols] * inv[g]
        level, k = 0, 1
        while k < w:
            head = carry[level, :, cols]
            carry[level, :, cols] = q[0:HALO]
            q = q + _shift_up(q, k, head)
            level, k = level + 1, k * 2
        outs.append(q)
    return jnp.concatenate(outs, axis=1)


def _copy_all(pairs, sems):
    copies = [pltpu.make_async_copy(src, dst, sems.at[n]) for n, (src, dst) in enumerate(pairs)]
    for cp in copies:
        cp.start()
    for cp in copies:
        cp.wait()


def _grp_pairs(wgrp_hbm, wgrp_v):
    rows = GROUP_DIM // 4
    return [(wgrp_hbm.at[k, g], wgrp_v.at[g, pl.ds(k * rows, rows), :]) for k in range(4) for g in range(N_GROUPS)]


def _rms(h):
    r = lax.rsqrt(jnp.mean(h * h, axis=-1, keepdims=True) + EPS)
    return h * r, r


def _rms_bwd(dhn, xn, r, gain):
    dgain = jnp.sum(dhn * xn, axis=0, keepdims=True)
    dxn = dhn * gain
    dh = r * (dxn - xn * jnp.mean(dxn * xn, axis=-1, keepdims=True))
    return dh, dgain


class _Rider:
    def __init__(self, inputs, out_shapes, n_sems, start, finish, middle=None, aliases=None):
        self.inputs, self.out_shapes, self.n_sems = list(inputs), list(out_shapes), n_sems
        self.start, self.middle, self.finish = start, middle, finish
        self.aliases = dict(aliases or {})


def _merge(riders):
    riders = [r for r in riders if r is not None]
    if not riders:
        return None
    if len(riders) == 1:
        return riders[0]

    def phase(which):
        def run(rin, rout, send, recv, base=0):
            i0 = o0 = s0 = 0
            for r in riders:
                fn = getattr(r, which)
                if fn is not None:
                    fn(rin[i0:i0 + len(r.inputs)], rout[o0:o0 + len(r.out_shapes)], send, recv, base + s0)
                i0, o0, s0 = i0 + len(r.inputs), o0 + len(r.out_shapes), s0 + r.n_sems
        return run

    aliases, i0, o0 = {}, 0, 0
    for r in riders:
        aliases.update({i0 + a: o0 + b for a, b in r.aliases.items()})
        i0, o0 = i0 + len(r.inputs), o0 + len(r.out_shapes)
    return _Rider(sum([r.inputs for r in riders], []), sum([r.out_shapes for r in riders], []),
                  sum(r.n_sems for r in riders), phase("start"), phase("finish"),
                  phase("middle") if any(r.middle for r in riders) else None, aliases)


def _split(landed, riders):
    out, o0 = [], 0
    for r in riders:
        if r is None:
            out.append(None)
        else:
            out.append(landed[o0:o0 + len(r.out_shapes)])
            o0 += len(r.out_shapes)
    return out


def _place():
    x, y, c = lax.axis_index("x"), lax.axis_index("y"), lax.axis_index("c")
    chips = [(1 - x, y), (x, 1 - y), (1 - x, 1 - y)]
    return x, y, c, chips


def _remote(src, dst, send_sems, recv_sems, sem, to):
    return pltpu.make_async_remote_copy(src_ref=src, dst_ref=dst, send_sem=send_sems.at[sem], recv_sem=recv_sems.at[sem],
                                        device_id=to, device_id_type=MESH)


def _gather_rider(shards):
    ni = len(shards)

    def first_hops(rin, rout, send, recv, base, x, y, c, chips):
        me = 2 * x + y
        return [_remote(rin[t].at[c], rout[t].at[me, c], send, recv, base + 7 * t + j, (cx, cy, c))
                for j, (cx, cy) in enumerate(chips) for t in range(ni)]

    def passes(rout, send, recv, base, x, y, c, chips):
        out = []
        for j, (cx, cy) in enumerate(chips):
            for t in range(ni):
                landed = rout[t].at[2 * cx + cy, c]
                out.append((_remote(landed, landed, send, recv, base + 7 * t + j, (x, y, 1 - c)),
                            _remote(landed, landed, send, recv, base + 7 * t + 3 + j, (x, y, 1 - c))))
        return out

    def own(rin, rout, send, recv, base, x, y, c):
        return [_remote(rin[t], rout[t].at[2 * x + y], send, recv, base + 7 * t + 6, (x, y, 1 - c)) for t in range(ni)]

    def start(rin, rout, send, recv, base=0):
        x, y, c, chips = _place()
        for cp in first_hops(rin, rout, send, recv, base, x, y, c, chips) + own(rin, rout, send, recv, base, x, y, c):
            cp.start()

    def middle(rin, rout, send, recv, base=0):
        x, y, c, chips = _place()
        for arrival, onward in passes(rout, send, recv, base, x, y, c, chips):
            arrival.wait_recv()
            onward.start()

    def finish(rin, rout, send, recv, base=0):
        x, y, c, chips = _place()
        for j, (cx, cy) in enumerate(chips):
            for t in range(ni):
                other = rout[t].at[2 * cx + cy, 1 - c]
                _remote(other, other, send, recv, base + 7 * t + 3 + j, (x, y, 1 - c)).wait_recv()
        for cp in own(rin, rout, send, recv, base, x, y, c):
            cp.wait_recv()
            cp.wait_send()
        for cp in first_hops(rin, rout, send, recv, base, x, y, c, chips):
            cp.wait_send()
        for _, onward in passes(rout, send, recv, base, x, y, c, chips):
            onward.wait_send()

    return _Rider(shards, [_sds((4,) + a.shape, a.dtype) for a in shards], 7 * ni, start, finish, middle)


def _pair_rider(grads):
    ni = len(grads)

    def copies(rin, rout, send, recv, base):
        x, y, c, _ = _place()
        return [_remote(rin[t].at[:, 1 - c], rout[t], send, recv, base + t, (x, y, 1 - c)) for t in range(ni)]

    def start(rin, rout, send, recv, base=0):
        for cp in copies(rin, rout, send, recv, base):
            cp.start()

    def finish(rin, rout, send, recv, base=0):
        for cp in copies(rin, rout, send, recv, base):
            cp.wait()

    return _Rider(grads, [_sds(g.shape[:1] + g.shape[2:], g.dtype) for g in grads], ni, start, finish)


def _ici_rider(pair_sums):
    ni = len(pair_sums)

    def copies(rin, rout, send, recv, base):
        x, y, c, chips = _place()
        return [_remote(rin[t].at[j], rout[t].at[j], send, recv, base + 3 * t + j, (cx, cy, c))
                for j, (cx, cy) in enumerate(chips) for t in range(ni)]

    def start(rin, rout, send, recv, base=0):
        for cp in copies(rin, rout, send, recv, base):
            cp.start()

    def finish(rin, rout, send, recv, base=0):
        for cp in copies(rin, rout, send, recv, base):
            cp.wait()

    return _Rider(pair_sums, [_sds((3,) + g.shape[1:], g.dtype) for g in pair_sums], 3 * ni, start, finish)


def _final_rider(summed, slots):
    ni = len(summed)

    def copies(rout, send, recv, base):
        x, y, c, _ = _place()
        return [(_remote(rout[t].at[slots[t], c], rout[t].at[slots[t], c], send, recv, base + t, (x, y, 1 - c)),
                 _remote(rout[t].at[slots[t], 1 - c], rout[t].at[slots[t], 1 - c], send, recv, base + t, (x, y, 1 - c)))
                for t in range(ni)]

    def start(rin, rout, send, recv, base=0):
        for mine, _ in copies(rout, send, recv, base):
            mine.start()

    def finish(rin, rout, send, recv, base=0):
        for mine, theirs in copies(rout, send, recv, base):
            mine.wait_send()
            theirs.wait_recv()

    return _Rider(summed, [_sds(a.shape, a.dtype) for a in summed], ni, start, finish,
                  aliases={t: t for t in range(ni)})


def _call(body, *, name, grid, in_specs, out_specs, out_shape, scratch_shapes, operands, rider=None):
    if rider is None:
        outs = pl.pallas_call(body, name=name, grid=grid, in_specs=in_specs, out_specs=out_specs, out_shape=out_shape,
                              scratch_shapes=scratch_shapes, compiler_params=_params(len(grid)))(*operands)
        return list(outs), []
    n_in, n_out, n_scr = len(in_specs), len(out_specs), len(scratch_shapes)
    r_in, r_out = len(rider.inputs), len(rider.out_shapes)
    steps = 1
    for g in grid:
        steps *= g
    mid = max(steps - 1 - MIDDLE_STEPS_BEFORE_END, 0)

    def full_body(*refs):
        own_in, rin = refs[:n_in], refs[n_in:n_in + r_in]
        own_out = refs[n_in + r_in:n_in + r_in + n_out]
        rout = refs[n_in + r_in + n_out:n_in + r_in + n_out + r_out]
        own_scr = refs[n_in + r_in + n_out + r_out:n_in + r_in + n_out + r_out + n_scr]
        send, recv = refs[-2], refs[-1]
        step = pl.program_id(0)
        for axis in range(1, len(grid)):
            step = step * grid[axis] + pl.program_id(axis)

        @pl.when(step == 0)
        def _():
            rider.start(rin, rout, send, recv)

        body(*own_in, *own_out, *own_scr)

        if rider.middle is not None:
            @pl.when(step == mid)
            def _():
                rider.middle(rin, rout, send, recv)

        @pl.when(step == steps - 1)
        def _():
            rider.finish(rin, rout, send, recv)

    outs = pl.pallas_call(
        full_body, name=name, grid=grid,
        in_specs=list(in_specs) + [ANY] * r_in, out_specs=list(out_specs) + [ANY] * r_out,
        out_shape=list(out_shape) + rider.out_shapes,
        scratch_shapes=list(scratch_shapes) + [pltpu.SemaphoreType.DMA((rider.n_sems,)), pltpu.SemaphoreType.DMA((rider.n_sems,))],
        input_output_aliases={n_in + a: n_out + b for a, b in rider.aliases.items()},
        compiler_params=_params(len(grid)),
    )(*operands, *rider.inputs)
    return list(outs[:n_out]), list(outs[n_out:])


def _run_rider(rider, name):
    r_in, r_out = len(rider.inputs), len(rider.out_shapes)

    def body(*refs):
        rin, rout, send, recv = refs[:r_in], refs[r_in:r_in + r_out], refs[-2], refs[-1]
        rider.start(rin, rout, send, recv)
        if rider.middle is not None:
            rider.middle(rin, rout, send, recv)
        rider.finish(rin, rout, send, recv)

    outs = pl.pallas_call(
        body, name=name, in_specs=[ANY] * r_in, out_specs=[ANY] * r_out, out_shape=rider.out_shapes,
        scratch_shapes=[pltpu.SemaphoreType.DMA((rider.n_sems,)), pltpu.SemaphoreType.DMA((rider.n_sems,))],
        input_output_aliases=rider.aliases,
    )(*rider.inputs)
    return list(outs)


def _fwd_mix_a(h, gain, conv_w, w_in, w_out, name, rider=None):
    s, d = h.shape
    e = MIX_WIDTH
    ts = min(TS_MIX, s)
    nt = s // ts

    def body(h_ref, gain_ref, cw_ref, win_hbm, wout_hbm, h1_ref, proj_ref, win_v, wout_v, carry, sems):
        i = pl.program_id(0)

        @pl.when(i == 0)
        def _():
            _copy_all([(win_hbm, win_v), (wout_hbm, wout_v)], sems)
            carry[...] = jnp.zeros_like(carry)

        hh = h_ref[...]
        xn, _ = _rms(hh)
        hnb = (xn * gain_ref[...]).astype(BF16)
        b = _dot(hnb, win_v[0])
        c = _dot(hnb, win_v[1])
        v = _dot(hnb, win_v[2])
        z = _dot(hnb, win_v[3])
        proj_ref[:, 0 * e:1 * e] = b.astype(BF16)
        proj_ref[:, 1 * e:2 * e] = c.astype(BF16)
        proj_ref[:, 2 * e:3 * e] = v.astype(BF16)
        proj_ref[:, 3 * e:4 * e] = z.astype(BF16)
        cv = c * v
        tail = carry[...]
        carry[...] = cv[ts - HALO:ts]
        conv = cw_ref[0:1, :] * _shift_down(cv, 2, tail) + cw_ref[1:2, :] * _shift_down(cv, 1, tail) + cw_ref[2:3, :] * cv
        mb = ((z * _sigmoid(z)) * (b * conv)).astype(BF16)
        h1_ref[...] = hh + _dot(mb, wout_v[...])

    row = lambda width: pl.BlockSpec((ts, width), lambda i: (i, 0))
    return _call(
        body, name=name, grid=(nt,),
        in_specs=[row(d), _full((1, d)), _full((8, e)), ANY, ANY],
        out_specs=[row(d), row(4 * e)],
        out_shape=[_sds((s, d), F32), _sds((s, 4 * e), BF16)],
        scratch_shapes=[pltpu.VMEM((4, d, e), BF16), pltpu.VMEM((e, d), BF16), pltpu.VMEM((HALO, e), F32),
                        pltpu.SemaphoreType.DMA((2,))],
        operands=[h, gain, conv_w, w_in, w_out], rider=rider)


def _fwd_mix_b(h, gain, scale, w_in, w_grp, w_out, name, rider=None):
    s, d = h.shape
    e = MIX_WIDTH
    ts = min(TS_MIX, s)
    nt = s // ts

    def body(h_ref, gain_ref, scale_ref, win_hbm, wgrp_hbm, wout_hbm, h1_ref, z_ref, mx_ref, dd_ref,
             win_v, wgrp_v, wout_v, carry, sems):
        i = pl.program_id(0)

        @pl.when(i == 0)
        def _():
            _copy_all([(win_hbm, win_v), (wout_hbm, wout_v)] + _grp_pairs(wgrp_hbm, wgrp_v), sems)
            carry[...] = jnp.zeros_like(carry)

        hh = h_ref[...]
        xn, _ = _rms(hh)
        hnb = (xn * gain_ref[...]).astype(BF16)
        u = jnp.concatenate([_dot(hnb, win_v[0]), _dot(hnb, win_v[1])], axis=1)
        z = jnp.concatenate([_dot(hnb, win_v[2]), _dot(hnb, win_v[3])], axis=1)
        z_ref[...] = z.astype(BF16)
        diff = (_pool_fwd(u, carry, i, ts) - u).astype(BF16)
        dd_ref[...] = diff
        mx = jnp.concatenate(
            [_dot(diff[:, g * GROUP_DIM:(g + 1) * GROUP_DIM], wgrp_v[g]) for g in range(N_GROUPS)], axis=1)
        mx_ref[...] = mx.astype(BF16)
        mb = ((z * _sigmoid(z)) * (mx * scale_ref[...])).astype(BF16)
        h1_ref[...] = hh + _dot(mb, wout_v[...])

    row = lambda width: pl.BlockSpec((ts, width), lambda i: (i, 0))
    return _call(
        body, name=name, grid=(nt,),
        in_specs=[row(d), _full((1, d)), _full((1, e)), ANY, ANY, ANY],
        out_specs=[row(d), row(e), row(e), row(e)],
        out_shape=[_sds((s, d), F32)] + [_sds((s, e), BF16)] * 3,
        scratch_shapes=[pltpu.VMEM((4, d, e // 2), BF16), pltpu.VMEM((N_GROUPS, GROUP_DIM, GROUP_DIM), BF16),
                        pltpu.VMEM((e, d), BF16), pltpu.VMEM((4, HALO, e), F32), pltpu.SemaphoreType.DMA((18,))],
        operands=[h, gain, scale, w_in, w_grp, w_out], rider=rider)


def _fwd_ple(h1, p, gain, w_gate, w_proj, layer, rider=None):
    s, d = h1.shape
    pd = p.shape[-1]
    ts = min(TS_PLE, s)
    nt = s // ts

    def body(h1_ref, p_ref, gain_ref, wg_hbm, wp_hbm, h2_ref, gate_ref, wg_v, wp_v, sems):
        @pl.when(pl.program_id(0) == 0)
        def _():
            _copy_all([(wg_hbm, wg_v), (wp_hbm, wp_v)], sems)

        hh = h1_ref[...]
        xn, _ = _rms(hh)
        hpb = (xn * gain_ref[...]).astype(BF16)
        gate = _sigmoid(_dot(hpb, wg_v[...]))
        pb = p_ref[...].astype(BF16)
        pe = jnp.concatenate([_dot(pb, wp_v[k]) for k in range(4)], axis=1)
        gate_ref[...] = gate.astype(BF16)
        h2_ref[...] = hh + gate * pe

    row = lambda width: pl.BlockSpec((ts, width), lambda i: (i, 0))
    return _call(
        body, name=f"fwd_ple{layer}", grid=(nt,),
        in_specs=[row(d), pl.BlockSpec((None, ts, pd), lambda i: (layer, i, 0)), _full((1, d)), ANY, ANY],
        out_specs=[row(d), row(d)],
        out_shape=[_sds((s, d), F32), _sds((s, d), BF16)],
        scratch_shapes=[pltpu.VMEM((d, d), BF16), pltpu.VMEM((4, pd, d // 4), BF16), pltpu.SemaphoreType.DMA((2,))],
        operands=[h1, p, gain, w_gate, w_proj], rider=rider)


def _loss_head(h, target, gain):
    s, d = h.shape
    ts = min(TS_PLE, s)
    nt = s // ts

    def body(h_ref, t_ref, gain_ref, dh_ref, loss_ref, dgain_ref):
        @pl.when(pl.program_id(0) == 0)
        def _():
            loss_ref[...] = jnp.zeros_like(loss_ref)
            dgain_ref[...] = jnp.zeros_like(dgain_ref)

        xn, r = _rms(h_ref[...])
        err = xn * gain_ref[...] - t_ref[...]
        part = 0.5 * jnp.sum(jnp.mean(err * err, axis=-1, keepdims=True), axis=0, keepdims=True)
        loss_ref[...] += jnp.broadcast_to(part, loss_ref.shape)
        dh, dgain = _rms_bwd(err * (1.0 / d), xn, r, gain_ref[...])
        dh_ref[...] = dh
        dgain_ref[...] += dgain

    row = pl.BlockSpec((ts, d), lambda i: (i, 0))
    outs, _ = _call(
        body, name="loss_head", grid=(nt,),
        in_specs=[row, row, _full((1, d))],
        out_specs=[row, _full((8, 128)), _full((1, d))],
        out_shape=[_sds((s, d), F32), _sds((8, 128), F32), _sds((1, d), F32)],
        scratch_shapes=[], operands=[h, target, gain])
    return outs


def _bwd_ple(dh2, h1, gate, p, gain, w_gate, w_proj, layer, rider=None):
    s, d = dh2.shape
    pd = p.shape[-1]
    ts = min(TS_PLE, s)
    nt = s // ts
    qd = d // 4

    def body(dh2_ref, h1_ref, gate_ref, p_ref, gain_ref, wg_hbm, wp_hbm, dh1_ref, dgain_ref, dwg_hbm, dwp_hbm,
             wg_v, wp_v, acc_g, acc_p, sems):
        i = pl.program_id(0)

        @pl.when(i == 0)
        def _():
            _copy_all([(wg_hbm, wg_v), (wp_hbm, wp_v)], sems)
            dgain_ref[...] = jnp.zeros_like(dgain_ref)
            acc_g[...] = jnp.zeros_like(acc_g)
            acc_p[...] = jnp.zeros_like(acc_p)

        g2 = dh2_ref[...]
        gate_f = gate_ref[...].astype(F32)
        xn, r = _rms(h1_ref[...])
        hpb = (xn * gain_ref[...]).astype(BF16)
        pb = p_ref[...].astype(BF16)
        pe = jnp.concatenate([_dot(pb, wp_v[k]) for k in range(4)], axis=1)
        dpeb = (g2 * gate_f).astype(BF16)
        dab = ((g2 * pe) * (gate_f * (1.0 - gate_f))).astype(BF16)
        acc_g[...] += _dot_tn(hpb, dab)
        for k in range(4):
            acc_p[k] += _dot_tn(pb, dpeb[:, k * qd:(k + 1) * qd])
        dhp = _dot_nt(dab, wg_v[...])
        dh, dgain = _rms_bwd(dhp, xn, r, gain_ref[...])
        dh1_ref[...] = g2 + dh
        dgain_ref[...] += dgain

        @pl.when(i == nt - 1)
        def _():
            _copy_all([(acc_g, dwg_hbm), (acc_p, dwp_hbm)], sems)

    row = pl.BlockSpec((ts, d), lambda i: (i, 0))
    return _call(
        body, name=f"bwd_ple{layer}", grid=(nt,),
        in_specs=[row, row, row, pl.BlockSpec((None, ts, pd), lambda i: (layer, i, 0)), _full((1, d)), ANY, ANY],
        out_specs=[row, _full((1, d)), ANY, ANY],
        out_shape=[_sds((s, d), F32), _sds((1, d), F32), _sds((d, d), F32), _sds((4, pd, qd), F32)],
        scratch_shapes=[pltpu.VMEM((d, d), BF16), pltpu.VMEM((4, pd, qd), BF16), pltpu.VMEM((d, d), F32),
                        pltpu.VMEM((4, pd, qd), F32), pltpu.SemaphoreType.DMA((2,))],
        operands=[dh2, h1, gate, p, gain, w_gate, w_proj], rider=rider)


def _bwd_mix_a(dh1, h, proj, gain, conv_w, w_in, w_out, name, rider=None):
    s, d = dh1.shape
    e = MIX_WIDTH
    ts = min(TS_MIX, s)
    nt = s // ts
    hb = 16
    per = ts // hb

    def body(dh1_ref, h_ref, proj_ref, ch_ref, vh_ref, gain_ref, cw_ref, win_hbm, wout_hbm,
             dh_ref, dcw_ref, dgain_ref, dwin_hbm, dwout_hbm, win_v, wout_v, acc_in, acc_out, carry, sems):
        i = pl.program_id(0)
        tile = nt - 1 - i

        @pl.when(i == 0)
        def _():
            _copy_all([(win_hbm, win_v), (wout_hbm, wout_v)], sems)
            carry[...] = jnp.zeros_like(carry)
            dcw_ref[...] = jnp.zeros_like(dcw_ref)
            dgain_ref[...] = jnp.zeros_like(dgain_ref)
            acc_in[...] = jnp.zeros_like(acc_in)
            acc_out[...] = jnp.zeros_like(acc_out)

        b = proj_ref[:, 0 * e:1 * e].astype(F32)
        c = proj_ref[:, 1 * e:2 * e].astype(F32)
        v = proj_ref[:, 2 * e:3 * e].astype(F32)
        z = proj_ref[:, 3 * e:4 * e].astype(F32)
        cv = c * v
        prev = (ch_ref[...].astype(F32) * vh_ref[...].astype(F32))[hb - HALO:hb]
        tail = jnp.where(tile > 0, prev, jnp.zeros_like(prev))
        cv1 = _shift_down(cv, 1, tail)
        cv2 = _shift_down(cv, 2, tail)
        conv = cw_ref[0:1, :] * cv2 + cw_ref[1:2, :] * cv1 + cw_ref[2:3, :] * cv
        sig = _sigmoid(z)
        sz = z * sig
        y = b * conv
        dh1 = dh1_ref[...]
        dh1b = dh1.astype(BF16)
        acc_out[...] += _dot_tn((sz * y).astype(BF16), dh1b)
        dm = _dot_nt(dh1b, wout_v[...])
        dz = (dm * y) * (sig * (1.0 + z * (1.0 - sig)))
        dy = dm * sz
        db = dy * conv
        dconv = dy * b
        head = carry[...]
        carry[...] = dconv[0:HALO]
        dcv = cw_ref[2:3, :] * dconv + cw_ref[1:2, :] * _shift_up(dconv, 1, head) + cw_ref[0:1, :] * _shift_up(dconv, 2, head)
        dcw_ref[0:1, :] += jnp.sum(dconv * cv2, axis=0, keepdims=True)
        dcw_ref[1:2, :] += jnp.sum(dconv * cv1, axis=0, keepdims=True)
        dcw_ref[2:3, :] += jnp.sum(dconv * cv, axis=0, keepdims=True)
        parts = [db.astype(BF16), (dcv * v).astype(BF16), (dcv * c).astype(BF16), dz.astype(BF16)]
        xn, r = _rms(h_ref[...])
        hnb = (xn * gain_ref[...]).astype(BF16)
        for q in range(4):
            acc_in[q] += _dot_tn(hnb, parts[q])
        dhn = _dot_nt(parts[0], win_v[0]) + _dot_nt(parts[1], win_v[1]) + _dot_nt(parts[2], win_v[2]) + _dot_nt(parts[3], win_v[3])
        dh, dgain = _rms_bwd(dhn, xn, r, gain_ref[...])
        dh_ref[...] = dh1 + dh
        dgain_ref[...] += dgain

        @pl.when(i == nt - 1)
        def _():
            _copy_all([(acc_in, dwin_hbm), (acc_out, dwout_hbm)], sems)

    row = lambda width: pl.BlockSpec((ts, width), lambda i: (nt - 1 - i, 0))
    halo = lambda col: pl.BlockSpec((hb, e), lambda i: (jnp.maximum((nt - 1 - i) * per - 1, 0), col))
    return _call(
        body, name=name, grid=(nt,),
        in_specs=[row(d), row(d), row(4 * e), halo(1), halo(2), _full((1, d)), _full((8, e)), ANY, ANY],
        out_specs=[row(d), _full((8, e)), _full((1, d)), ANY, ANY],
        out_shape=[_sds((s, d), F32), _sds((8, e), F32), _sds((1, d), F32), _sds((4, d, e), F32), _sds((e, d), F32)],
        scratch_shapes=[pltpu.VMEM((4, d, e), BF16), pltpu.VMEM((e, d), BF16), pltpu.VMEM((4, d, e), F32),
                        pltpu.VMEM((e, d), F32), pltpu.VMEM((HALO, e), F32), pltpu.SemaphoreType.DMA((2,))],
        operands=[dh1, h, proj, proj, proj, gain, conv_w, w_in, w_out], rider=rider)


def _bwd_mix_b(dh1, h, z, mx, diff, gain, scale, w_in, w_grp, w_out, name, rider=None):
    s, d = dh1.shape
    e = MIX_WIDTH
    ts = min(TS_MIX, s)
    nt = s // ts
    half = e // 2

    def body(dh1_ref, h_ref, z_ref, mx_ref, dd_ref, gain_ref, scale_ref, win_hbm, wgrp_hbm, wout_hbm,
             dh_ref, dscale_ref, dgain_ref, dwin_hbm, dwgrp_hbm, dwout_hbm,
             win_v, wgrp_v, wout_v, acc_in, acc_grp, acc_out, carry, sems):
        i = pl.program_id(0)
        tile = nt - 1 - i

        @pl.when(i == 0)
        def _():
            _copy_all([(win_hbm, win_v), (wout_hbm, wout_v)] + _grp_pairs(wgrp_hbm, wgrp_v), sems)
            carry[...] = jnp.zeros_like(carry)
            dscale_ref[...] = jnp.zeros_like(dscale_ref)
            dgain_ref[...] = jnp.zeros_like(dgain_ref)
            acc_in[...] = jnp.zeros_like(acc_in)
            acc_grp[...] = jnp.zeros_like(acc_grp)
            acc_out[...] = jnp.zeros_like(acc_out)

        zf = z_ref[...].astype(F32)
        mxf = mx_ref[...].astype(F32)
        sig = _sigmoid(zf)
        sz = zf * sig
        mixed = mxf * scale_ref[...]
        dh1 = dh1_ref[...]
        dh1b = dh1.astype(BF16)
        acc_out[...] += _dot_tn((sz * mixed).astype(BF16), dh1b)
        dm = _dot_nt(dh1b, wout_v[...])
        dz = (dm * mixed) * (sig * (1.0 + zf * (1.0 - sig)))
        dmixed = dm * sz
        dscale_ref[...] += jnp.sum(dmixed * mxf, axis=0, keepdims=True)
        dmxb = (dmixed * scale_ref[...]).astype(BF16)
        diff = dd_ref[...]
        for g in range(N_GROUPS):
            cols = slice(g * GROUP_DIM, (g + 1) * GROUP_DIM)
            acc_grp[g] += _dot_tn(diff[:, cols], dmxb[:, cols])
        ddiff = jnp.concatenate(
            [_dot_nt(dmxb[:, g * GROUP_DIM:(g + 1) * GROUP_DIM], wgrp_v[g]) for g in range(N_GROUPS)], axis=1)
        dub = (_pool_bwd(ddiff, carry, tile, ts) - ddiff).astype(BF16)
        dzb = dz.astype(BF16)
        parts = [dub[:, 0:half], dub[:, half:e], dzb[:, 0:half], dzb[:, half:e]]
        xn, r = _rms(h_ref[...])
        hnb = (xn * gain_ref[...]).astype(BF16)
        for k in range(4):
            acc_in[k] += _dot_tn(hnb, parts[k])
        dhn = _dot_nt(parts[0], win_v[0]) + _dot_nt(parts[1], win_v[1]) + _dot_nt(parts[2], win_v[2]) + _dot_nt(parts[3], win_v[3])
        dh, dgain = _rms_bwd(dhn, xn, r, gain_ref[...])
        dh_ref[...] = dh1 + dh
        dgain_ref[...] += dgain

        @pl.when(i == nt - 1)
        def _():
            _copy_all([(acc_in, dwin_hbm), (acc_out, dwout_hbm)] + [(v, hb_) for hb_, v in _grp_pairs(dwgrp_hbm, acc_grp)], sems)

    row = lambda width: pl.BlockSpec((ts, width), lambda i: (nt - 1 - i, 0))
    return _call(
        body, name=name, grid=(nt,),
        in_specs=[row(d), row(d), row(e), row(e), row(e), _full((1, d)), _full((1, e)), ANY, ANY, ANY],
        out_specs=[row(d), _full((1, e)), _full((1, d)), ANY, ANY, ANY],
        out_shape=[_sds((s, d), F32), _sds((1, e), F32), _sds((1, d), F32), _sds((4, d, half), F32),
                   _sds((4, N_GROUPS, GROUP_DIM // 4, GROUP_DIM), F32), _sds((e, d), F32)],
        scratch_shapes=[pltpu.VMEM((4, d, half), BF16), pltpu.VMEM((N_GROUPS, GROUP_DIM, GROUP_DIM), BF16),
                        pltpu.VMEM((e, d), BF16), pltpu.VMEM((4, d, half), F32),
                        pltpu.VMEM((N_GROUPS, GROUP_DIM, GROUP_DIM), F32), pltpu.VMEM((e, d), F32),
                        pltpu.VMEM((4, HALO, e), F32), pltpu.SemaphoreType.DMA((18,))],
        operands=[dh1, h, z, mx, diff, gain, scale, w_in, w_grp, w_out], rider=rider)


def _first_gather(shards, small):
    rider = _gather_rider(shards)
    ni = len(shards)

    def body(*refs):
        rin, small_src = refs[:ni], refs[ni]
        rout, small_dst = refs[ni + 1:2 * ni + 1], refs[2 * ni + 1]
        send, recv, ssend, srecv = refs[2 * ni + 2:]
        x, y, c, chips = _place()
        me = 2 * x + y
        peers = [(cx, cy, c) for cx, cy in chips] + [(x, y, 1 - c)]
        vec = [_remote(small_src, small_dst.at[me], ssend, srecv, j, to) for j, to in enumerate(peers)]
        for cp in vec:
            cp.start()
        rider.start(rin, rout, send, recv)
        rider.middle(rin, rout, send, recv)
        rider.finish(rin, rout, send, recv)
        for j, (px, py, _) in enumerate(peers):
            _remote(small_src, small_dst.at[2 * px + py], ssend, srecv, j, peers[j]).wait_recv()
        for cp in vec:
            cp.wait_send()

    outs = pl.pallas_call(
        body, name="first_gather", in_specs=[ANY] * (ni + 1), out_specs=[ANY] * (ni + 1),
        out_shape=rider.out_shapes + [_sds((4,) + small.shape, small.dtype)],
        scratch_shapes=[pltpu.SemaphoreType.DMA((rider.n_sems,)), pltpu.SemaphoreType.DMA((rider.n_sems,)),
                        pltpu.SemaphoreType.DMA((4,)), pltpu.SemaphoreType.DMA((4,))],
    )(*shards, small)
    return list(outs[:ni]), outs[ni]


def _vector_rider(pack):
    flips = [(fx, fy, fc) for fx in (0, 1) for fy in (0, 1) for fc in (0, 1)][1:]

    def copies(rin, rout, send, recv, base):
        x, y, c, _ = _place()
        me = 4 * x + 2 * y + c
        peers = [(1 - x if fx else x, 1 - y if fy else y, 1 - c if fc else c) for fx, fy, fc in flips]
        own = pltpu.make_async_copy(rin[0], rout[0].at[me], send.at[base + 7])
        out = [_remote(rin[0], rout[0].at[me], send, recv, base + r, peer) for r, peer in enumerate(peers)]
        back = [_remote(rin[0], rout[0].at[4 * px + 2 * py + pc], send, recv, base + r, (px, py, pc))
                for r, (px, py, pc) in enumerate(peers)]
        return own, out, back

    def start(rin, rout, send, recv, base=0):
        own, out, _ = copies(rin, rout, send, recv, base)
        own.start()
        for cp in out:
            cp.start()

    def finish(rin, rout, send, recv, base=0):
        own, out, back = copies(rin, rout, send, recv, base)
        for cp in back:
            cp.wait_recv()
        for cp in out:
            cp.wait_send()
        own.wait()

    return _Rider([pack], [_sds((8,) + pack.shape, pack.dtype)], 8, start, finish)


def _vector_sum(landed):
    _, rows, d = landed.shape

    def body(l_ref, out_ref):
        total = l_ref[0]
        for dev in range(1, 8):
            total = total + l_ref[dev]
        out_ref[...] = total

    vmem = pl.BlockSpec(memory_space=pltpu.VMEM)
    return pl.pallas_call(body, name="vector_sum", in_specs=[vmem], out_specs=vmem, out_shape=_sds((rows, d), F32))(landed)


def _ew_rows(rows):
    return min(TR_EW, rows)


def _job_rows(rows, cols):
    return min(rows, max(8, JOB_BLOCK_BYTES // (4 * cols)))


def _pair_sum_job(grad, sibling_rows):
    _, _, rh, cols = grad.shape
    tr = _job_rows(rh, cols)
    nr = rh // tr

    def chip_of(j, pos):
        return jnp.bitwise_xor(pos[0], jnp.where(j == 2, 3, 2 - j))

    return dict(
        ins=[(grad, (None, None, tr, cols), lambda l, pos: (chip_of(l // nr, pos), pos[1], l % nr, 0)),
             (sibling_rows, (None, tr, cols), lambda l, pos: (chip_of(l // nr, pos), l % nr, 0))],
        out=((3, rh, cols), BF16, (None, tr, cols), lambda l, pos: (l // nr, l % nr, 0)),
        steps=3 * nr, fn=lambda g, sb: (g + sb).astype(BF16), alias=None)


def _final_sum_job(grad, sibling_rows, landed, stack, slot, n_slots):
    _, _, rh, cols = grad.shape
    tr = _job_rows(rh, cols)

    def fn(g, sb, ld):
        total = g + sb
        for j in range(3):
            total = total + ld[j].astype(F32)
        return total

    return dict(
        ins=[(grad, (None, None, tr, cols), lambda l, pos: (pos[0], pos[1], l, 0)),
             (sibling_rows, (None, tr, cols), lambda l, pos: (pos[0], l, 0)),
             (landed, (3, tr, cols), lambda l, pos: (0, l, 0))],
        out=((n_slots, 2, rh, cols), F32, (None, None, tr, cols), lambda l, pos: (slot, pos[1], l, 0)),
        steps=rh // tr, fn=fn, alias=stack)


def _run_jobs(jobs, place, name):
    starts, total = [], 0
    for jb in jobs:
        starts.append(total)
        total += jb["steps"]

    def clamped(fn, start, steps):
        return lambda s, pos: fn(jnp.clip(s - start, 0, steps - 1), pos)

    in_specs, operands = [], [place]
    for jb, start in zip(jobs, starts):
        for arr, block, fn in jb["ins"]:
            in_specs.append(pl.BlockSpec(block, clamped(fn, start, jb["steps"])))
            operands.append(arr)
    n_ins = len(in_specs)
    aliases = {}
    for t, jb in enumerate(jobs):
        if jb["alias"] is not None:
            in_specs.append(ANY)
            operands.append(jb["alias"])
            aliases[len(operands) - 1] = t
    out_specs = [pl.BlockSpec(jb["out"][2], clamped(jb["out"][3], start, jb["steps"])) for jb, start in zip(jobs, starts)]

    def body(place_ref, *refs):
        in_refs, out_refs = refs[:n_ins], refs[len(in_specs):]
        s = pl.program_id(0)
        first = 0
        for t, (jb, start) in enumerate(zip(jobs, starts)):
            mine = in_refs[first:first + len(jb["ins"])]
            first += len(jb["ins"])

            @pl.when((s >= start) & (s < start + jb["steps"]))
            def _(mine=mine, t=t, jb=jb):
                out_refs[t][...] = jb["fn"](*[r[...] for r in mine])

    grid_spec = pltpu.PrefetchScalarGridSpec(num_scalar_prefetch=1, grid=(total,), in_specs=in_specs, out_specs=out_specs)
    outs = pl.pallas_call(body, name=name, grid_spec=grid_spec,
                          out_shape=[_sds(jb["out"][0], jb["out"][1]) for jb in jobs],
                          input_output_aliases=aliases, compiler_params=_params(1))(*operands)
    return list(outs)


def _adamw(g, w, m, v, name):
    rows, cols = g.shape
    tr = _ew_rows(rows)

    def body(g_ref, w_ref, m_ref, v_ref, delta_ref, nm_ref, nv_ref):
        gg = g_ref[...]
        nm = ADAM_B1 * m_ref[...] + (1.0 - ADAM_B1) * gg
        nv = ADAM_B2 * v_ref[...] + (1.0 - ADAM_B2) * (gg * gg)
        m_hat = nm / (1.0 - ADAM_B1 ** ADAM_STEP)
        v_hat = nv / (1.0 - ADAM_B2 ** ADAM_STEP)
        delta_ref[...] = -ADAM_LR * (m_hat / (jnp.sqrt(v_hat) + ADAM_EPS) + ADAM_WD * w_ref[...])
        nm_ref[...] = nm
        nv_ref[...] = nv

    spec = pl.BlockSpec((tr, cols), lambda i: (i, 0))
    return pl.pallas_call(
        body, name=name, grid=(rows // tr,), in_specs=[spec] * 4, out_specs=[spec] * 3,
        out_shape=[_sds((rows, cols), F32)] * 3, compiler_params=_params(),
    )(g, w, m, v)


BIG = ["a_w_in", "a_w_out", "b_w_in", "b_w_grp", "b_w_out", "ple_w_gate", "ple_w_proj"]

GATHER_PLAN = {
    "first": [("a_w_in", 0), ("a_w_out", 0)],
    "mix0": [("ple_w_gate", 0), ("ple_w_proj", 0), ("b_w_in", 0), ("b_w_grp", 0), ("b_w_out", 0)],
    "ple0": [("ple_w_gate", 1), ("ple_w_proj", 1)],
    "mix1": [("a_w_in", 1)],
    "ple1": [("ple_w_gate", 2), ("ple_w_proj", 2), ("a_w_out", 1)],
    "mix2": [("b_w_in", 1), ("b_w_grp", 1), ("b_w_out", 1), ("ple_w_gate", 3), ("ple_w_proj", 3)],
}


def _as_2d(name, a):
    if name == "b_w_grp":
        return a.reshape(a.shape[0], N_GROUPS * (GROUP_DIM // 4), GROUP_DIM)
    return a


def kernel(x, p, norm_mix, a_w_in, a_w_conv, a_w_out, b_w_in, b_w_grp, b_scale, b_w_out, ple_norm, ple_w_gate, ple_w_proj, final_norm, loss_target, m_norm_mix, m_a_w_in, m_a_w_conv, m_a_w_out, m_b_w_in, m_b_w_grp, m_b_scale, m_b_w_out, m_ple_norm, m_ple_w_gate, m_ple_w_proj, m_final_norm, v_norm_mix, v_a_w_in, v_a_w_conv, v_a_w_out, v_b_w_in, v_b_w_grp, v_b_scale, v_b_w_out, v_ple_norm, v_ple_w_gate, v_ple_w_proj, v_final_norm):
    d, e = D_MODEL, MIX_WIDTH
    s = x.shape[1]
    cx, cy, cc = lax.axis_index("x"), lax.axis_index("y"), lax.axis_index("c")
    chip = 2 * cx + cy
    place = jnp.stack([chip, cc]).astype(jnp.int32)

    weights = dict(a_w_in=a_w_in, a_w_out=a_w_out, b_w_in=b_w_in, b_w_grp=b_w_grp, b_w_out=b_w_out,
                   ple_w_gate=ple_w_gate, ple_w_proj=ple_w_proj)
    moms = dict(a_w_in=m_a_w_in, a_w_out=m_a_w_out, b_w_in=m_b_w_in, b_w_grp=m_b_w_grp, b_w_out=m_b_w_out,
                ple_w_gate=m_ple_w_gate, ple_w_proj=m_ple_w_proj)
    vars_ = dict(a_w_in=v_a_w_in, a_w_out=v_a_w_out, b_w_in=v_b_w_in, b_w_grp=v_b_w_grp, b_w_out=v_b_w_out,
                 ple_w_gate=v_ple_w_gate, ple_w_proj=v_ple_w_proj)
    w2d = {nm: _as_2d(nm, weights[nm]) for nm in BIG}
    bf = {nm: w2d[nm].astype(BF16) for nm in BIG}

    def shard_of(key):
        nm, j = key
        a = bf[nm][j]
        return a.reshape(2, a.shape[0] // 2, a.shape[1])

    gathered = {}

    def gather_rider(host):
        keys = GATHER_PLAN.get(host)
        return _gather_rider([shard_of(k) for k in keys]) if keys else None

    def keep(host, landed):
        for k, a in zip(GATHER_PLAN.get(host, []), landed):
            gathered[k] = a

    def weight(nm, j):
        a = gathered[(nm, j)]
        shapes = {"a_w_in": (4, d, e), "a_w_out": (e, d), "b_w_in": (4, d, e // 2),
                  "b_w_grp": (4, N_GROUPS, GROUP_DIM // 4, GROUP_DIM), "b_w_out": (e, d), "ple_w_gate": (d, d),
                  "ple_w_proj": (4, PLE_DIM, d // 4)}
        return a.reshape(shapes[nm])

    small = jnp.concatenate([a_w_conv.reshape(6, e // 4), b_scale], axis=0)
    landed, small_full = _first_gather([shard_of(k) for k in GATHER_PLAN["first"]], small)
    keep("first", landed)
    small_full = small_full.transpose(1, 0, 2).reshape(8, e)
    conv_w = [jnp.concatenate([small_full[3 * j:3 * j + 3], jnp.zeros((5, e), F32)], axis=0) for j in range(2)]
    scale_w = [small_full[6 + j:7 + j] for j in range(2)]

    p3 = p.reshape(DEPTH, s, PLE_DIM)
    mix_gain = [norm_mix[i:i + 1] for i in range(DEPTH)]
    ple_gain = [ple_norm[i:i + 1] for i in range(DEPTH)]

    h = x.reshape(s, d)
    saved = []
    for i in range(DEPTH):
        j = i // 2
        rider = gather_rider(f"mix{i}")
        if i % 2 == 0:
            (h1, proj), landed = _fwd_mix_a(h, mix_gain[i], conv_w[j], weight("a_w_in", j), weight("a_w_out", j),
                                            f"fwd_mix_a{j}", rider)
            mix = dict(proj=proj)
        else:
            (h1, zb, mx, diff), landed = _fwd_mix_b(h, mix_gain[i], scale_w[j], weight("b_w_in", j), weight("b_w_grp", j),
                                                    weight("b_w_out", j), f"fwd_mix_b{j}", rider)
            mix = dict(z=zb, mx=mx, diff=diff)
        keep(f"mix{i}", landed)
        (h2, gate), landed = _fwd_ple(h1, p3, ple_gain[i], weight("ple_w_gate", i), weight("ple_w_proj", i), i,
                                      gather_rider(f"ple{i}"))
        keep(f"ple{i}", landed)
        saved.append(dict(h=h, h1=h1, gate=gate, **mix))
        h = h2

    dh, loss_part, d_final = _loss_head(h, loss_target.reshape(s, d), final_norm.reshape(1, d))

    n_slots = {nm: weights[nm].shape[0] for nm in BIG}
    stacks = {nm: None for nm in BIG}

    class Group:
        def __init__(self, keys, grads):
            self.keys, self.stage = keys, 0
            self.g32 = [g.reshape(4, 2, w2d[nm].shape[1] // 2, w2d[nm].shape[2]) for (nm, _), g in zip(keys, grads)]

        def rider(self):
            if self.stage == 0:
                return _pair_rider(self.g32)
            if self.stage == 1:
                return _ici_rider(self.pair_sums)
            return _final_rider([stacks[nm] for nm, _ in self.keys], [j for _, j in self.keys])

        def jobs_after(self, landed):
            if self.stage == 0:
                self.from_sibling = landed
                return [_pair_sum_job(g, sb) for g, sb in zip(self.g32, landed)]
            if self.stage == 1:
                return [_final_sum_job(g, sb, ld, stacks[nm], j, n_slots[nm])
                        for (nm, j), g, sb, ld in zip(self.keys, self.g32, self.from_sibling, landed)]
            return []

        def advance(self, landed, summed):
            if self.stage == 0:
                self.pair_sums = summed
            else:
                for (nm, _), a in zip(self.keys, summed if self.stage == 1 else landed):
                    stacks[nm] = a
            self.stage += 1

    active = []
    batches = [0]

    def riders_now():
        parts = [g.rider() for g in active]
        return parts, _merge(parts)

    def advance_all(parts, landed):
        groups = list(active)
        pieces = _split(landed, parts)
        jobs = [g.jobs_after(l) for g, l in zip(groups, pieces)]
        flat = sum(jobs, [])
        outs = _run_jobs(flat, place, f"reduce_sums{batches[0]}") if flat else []
        batches[0] += 1
        for g, l, jb in zip(groups, pieces, jobs):
            g.advance(l, outs[:len(jb)])
            outs = outs[len(jb):]
            if g.stage == 3:
                active.remove(g)

    d_mix_gain, d_ple_gain = [None] * DEPTH, [None] * DEPTH
    d_conv, d_scale = [None] * 2, [None] * 2
    for i in reversed(range(DEPTH)):
        j = i // 2
        sv = saved[i]
        parts, rider = riders_now()
        (dh1, d_ple_gain[i], dwg, dwp), landed = _bwd_ple(
            dh, sv["h1"], sv["gate"], p3, ple_gain[i], weight("ple_w_gate", i), weight("ple_w_proj", i), i, rider)
        advance_all(parts, landed)
        active.append(Group([("ple_w_gate", i), ("ple_w_proj", i)], [dwg, dwp]))
        parts, rider = riders_now()
        if i % 2 == 0:
            (dh, d_conv[j], d_mix_gain[i], dwin, dwout), landed = _bwd_mix_a(
                dh1, sv["h"], sv["proj"], mix_gain[i], conv_w[j], weight("a_w_in", j), weight("a_w_out", j),
                f"bwd_mix_a{j}", rider)
            new = Group([("a_w_in", j), ("a_w_out", j)], [dwin, dwout])
        else:
            (dh, d_scale[j], d_mix_gain[i], dwin, dwgrp, dwout), landed = _bwd_mix_b(
                dh1, sv["h"], sv["z"], sv["mx"], sv["diff"], mix_gain[i], scale_w[j], weight("b_w_in", j),
                weight("b_w_grp", j), weight("b_w_out", j), f"bwd_mix_b{j}", rider)
            new = Group([("b_w_in", j), ("b_w_grp", j), ("b_w_out", j)], [dwin, dwgrp, dwout])
        advance_all(parts, landed)
        active.append(new)
    grad_x = dh.reshape(1, s, d)

    pack = jnp.concatenate(
        d_mix_gain + d_ple_gain + [d_final] + [d_conv[0][0:3], d_conv[1][0:3]] + d_scale
        + [jnp.tile(loss_part[0:1], (1, d // 128)), jnp.zeros((SMALL_ROWS - 18, d), F32)], axis=0)
    vectors = _vector_rider(pack)
    tail = 0
    while active:
        parts, _ = riders_now()
        extra = [vectors] if tail == 0 else []
        landed = _run_rider(_merge(parts + extra), f"tail_exchange{tail}")
        if extra:
            total = _vector_sum(_split(landed, parts + extra)[-1][0])
        advance_all(parts, landed)
        tail += 1
    loss = total[17, 0]

    out_grad, out_delta, out_m, out_v = {}, {}, {}, {}
    for nm in BIG:
        shape = weights[nm].shape
        flat = (w2d[nm].shape[0] * w2d[nm].shape[1], w2d[nm].shape[2])
        g2 = stacks[nm].reshape(flat)
        delta, new_m, new_v = _adamw(g2, w2d[nm].reshape(flat), _as_2d(nm, moms[nm]).reshape(flat),
                                     _as_2d(nm, vars_[nm]).reshape(flat), f"adamw_{nm}")
        out_grad[nm], out_delta[nm] = g2.reshape(shape), delta.reshape(shape)
        out_m[nm], out_v[nm] = new_m.reshape(shape), new_v.reshape(shape)

    rep_rows = 16
    rep = lambda a, b_, c_: jnp.concatenate([a, b_, c_.reshape(1, d), jnp.zeros((rep_rows - 9, d), F32)], axis=0)
    rep_delta, rep_m, rep_v = _adamw(
        jnp.concatenate([total[0:9], jnp.zeros((rep_rows - 9, d), F32)], axis=0),
        rep(norm_mix, ple_norm, final_norm), rep(m_norm_mix, m_ple_norm, m_final_norm),
        rep(v_norm_mix, v_ple_norm, v_final_norm), "adamw_gains")
    mine_cols = lax.dynamic_slice_in_dim(total[9:17], chip * (e // 4), e // 4, axis=1)
    col = lambda a, b_: jnp.concatenate([a.reshape(6, e // 4), b_], axis=0)
    col_delta, col_m, col_v = _adamw(mine_cols, col(a_w_conv, b_scale), col(m_a_w_conv, m_b_scale),
                                     col(v_a_w_conv, v_b_scale), "adamw_cols")

    def unpack(rep_a, col_a):
        return dict(norm_mix=rep_a[0:4], ple_norm=rep_a[4:8], final_norm=rep_a[8],
                    a_w_conv=col_a[0:6].reshape(2, 3, e // 4), b_scale=col_a[6:8])

    small_out = [unpack(total, mine_cols), unpack(rep_delta, col_delta), unpack(rep_m, col_m), unpack(rep_v, col_v)]
    order = ["norm_mix", "a_w_in", "a_w_conv", "a_w_out", "b_w_in", "b_w_grp", "b_scale", "b_w_out", "ple_norm",
             "ple_w_gate", "ple_w_proj", "final_norm"]
    outs = [loss, grad_x]
    for big, small_d in zip([out_grad, out_delta, out_m, out_v], small_out):
        outs += [big[nm] if nm in big else small_d[nm] for nm in order]
    return tuple(outs)
```

```python
import jax
import jax.numpy as jnp
from jax import lax
from jax.experimental import pallas as pl
from jax.experimental.pallas import tpu as pltpu

F32 = jnp.float32
BF16 = jnp.bfloat16
MESH = pl.DeviceIdType.MESH

D_MODEL = 1024
MIX_WIDTH = 1024
PLE_DIM = 256
N_GROUPS = 4
GROUP_DIM = 256
POOL_WINDOWS = (2, 4, 8, 16)
DEPTH = 4
EPS = 1e-6

ADAM_LR = 0.001
ADAM_B1 = 0.9
ADAM_B2 = 0.999
ADAM_EPS = 1e-08
ADAM_WD = 0.01
ADAM_STEP = 10

HALO = 8
TS_MIX = 256
TS_PLE = 512
TR_EW = 512
VMEM_LIMIT = 56 * 1024 * 1024
SMALL_ROWS = 24
JOB_BLOCK_BYTES = 512 * 1024
MIDDLE_STEPS_BEFORE_END = 1

ANY = pl.BlockSpec(memory_space=pl.ANY)


def _sds(shape, dtype):
    return jax.ShapeDtypeStruct(shape, dtype)


def _full(shape):
    nd = len(shape)
    return pl.BlockSpec(shape, lambda *_: (0,) * nd)


def _params(n_axes=1):
    return pltpu.CompilerParams(dimension_semantics=("arbitrary",) * n_axes, vmem_limit_bytes=VMEM_LIMIT)


def _dot(a, b):
    return jnp.dot(a, b, preferred_element_type=F32)


def _dot_nt(a, b):
    return lax.dot_general(a, b, (((1,), (1,)), ((), ())), preferred_element_type=F32)


def _dot_tn(a, b):
    return lax.dot_general(a, b, (((0,), (0,)), ((), ())), preferred_element_type=F32)


def _sigmoid(z):
    return 1.0 / (1.0 + jnp.exp(-z))


def _shift_down(x, k, tail):
    rolled = pltpu.roll(x, k, 0)
    rt = tail if k % HALO == 0 else pltpu.roll(tail, k % HALO, 0)
    row = lax.broadcasted_iota(jnp.int32, rt.shape, 0)
    head = jnp.where(row < k, rt, rolled[0:HALO])
    return jnp.concatenate([head, rolled[HALO:]], axis=0)


def _shift_up(x, k, head_next):
    n = x.shape[0]
    rolled = pltpu.roll(x, n - k, 0)
    rh = head_next if k % HALO == 0 else pltpu.roll(head_next, HALO - k % HALO, 0)
    row = lax.broadcasted_iota(jnp.int32, rh.shape, 0)
    tail = jnp.where(row >= HALO - k, rh, rolled[n - HALO:n])
    return jnp.concatenate([rolled[:n - HALO], tail], axis=0)


def _inv_counts(tile, ts):
    t = tile * ts + lax.broadcasted_iota(jnp.int32, (ts, 1), 0)
    return [1.0 / jnp.minimum(t + 1, w).astype(F32) for w in POOL_WINDOWS]


def _pool_fwd(u, carry, tile, ts):
    inv = _inv_counts(tile, ts)
    outs = []
    for g, w in enumerate(POOL_WINDOWS):
        cols = slice(g * GROUP_DIM, (g + 1) * GROUP_DIM)
        s = u[:, cols]
        level, k = 0, 1
        while k < w:
            tail = carry[level, :, cols]
            carry[level, :, cols] = s[ts - HALO:ts]
            s = s + _shift_down(s, k, tail)
            level, k = level + 1, k * 2
        outs.append(s * inv[g])
    return jnp.concatenate(outs, axis=1)


def _pool_bwd(dd, carry, tile, ts):
    inv = _inv_counts(tile, ts)
    outs = []
    for g, w in enumerate(POOL_WINDOWS):
        cols = slice(g * GROUP_DIM, (g + 1) * GROUP_DIM)
        q = dd[:, cols] * inv[g]
        level, k = 0, 1
        while k < w:
            head = carry[level, :, cols]
            carry[level, :, cols] = q[0:HALO]
            q = q + _shift_up(q, k, head)
            level, k = level + 1, k * 2
        outs.append(q)
    return jnp.concatenate(outs, axis=1)


def _copy_all(pairs, sems):
    copies = [pltpu.make_async_copy(src, dst, sems.at[n]) for n, (src, dst) in enumerate(pairs)]
    for cp in copies:
        cp.start()
    for cp in copies:
        cp.wait()


def _grp_pairs(wgrp_hbm, wgrp_v):
    rows = GROUP_DIM // 4
    return [(wgrp_hbm.at[k, g], wgrp_v.at[g, pl.ds(k * rows, rows), :]) for k in range(4) for g in range(N_GROUPS)]


def _rms(h):
    r = lax.rsqrt(jnp.mean(h * h, axis=-1, keepdims=True) + EPS)
    return h * r, r


def _rms_bwd(dhn, xn, r, gain):
    dgain = jnp.sum(dhn * xn, axis=0, keepdims=True)
    dxn = dhn * gain
    dh = r * (dxn - xn * jnp.mean(dxn * xn, axis=-1, keepdims=True))
    return dh, dgain


class _Rider:
    def __init__(self, inputs, out_shapes, n_sems, start, finish, middle=None, aliases=None):
        self.inputs, self.out_shapes, self.n_sems = list(inputs), list(out_shapes), n_sems
        self.start, self.middle, self.finish = start, middle, finish
        self.aliases = dict(aliases or {})


def _merge(riders):
    riders = [r for r in riders if r is not None]
    if not riders:
        return None
    if len(riders) == 1:
        return riders[0]

    def phase(which):
        def run(rin, rout, send, recv, base=0):
            i0 = o0 = s0 = 0
            for r in riders:
                fn = getattr(r, which)
                if fn is not None:
                    fn(rin[i0:i0 + len(r.inputs)], rout[o0:o0 + len(r.out_shapes)], send, recv, base + s0)
                i0, o0, s0 = i0 + len(r.inputs), o0 + len(r.out_shapes), s0 + r.n_sems
        return run

    aliases, i0, o0 = {}, 0, 0
    for r in riders:
        aliases.update({i0 + a: o0 + b for a, b in r.aliases.items()})
        i0, o0 = i0 + len(r.inputs), o0 + len(r.out_shapes)
    return _Rider(sum([r.inputs for r in riders], []), sum([r.out_shapes for r in riders], []),
                  sum(r.n_sems for r in riders), phase("start"), phase("finish"),
                  phase("middle") if any(r.middle for r in riders) else None, aliases)


def _split(landed, riders):
    out, o0 = [], 0
    for r in riders:
        if r is None:
            out.append(None)
        else:
            out.append(landed[o0:o0 + len(r.out_shapes)])
            o0 += len(r.out_shapes)
    return out


def _place():
    x, y, c = lax.axis_index("x"), lax.axis_index("y"), lax.axis_index("c")
    chips = [(1 - x, y), (x, 1 - y), (1 - x, 1 - y)]
    return x, y, c, chips


def _remote(src, dst, send_sems, recv_sems, sem, to):
    return pltpu.make_async_remote_copy(src_ref=src, dst_ref=dst, send_sem=send_sems.at[sem], recv_sem=recv_sems.at[sem],
                                        device_id=to, device_id_type=MESH)


def _gather_rider(shards):
    ni = len(shards)

    def first_hops(rin, rout, send, recv, base, x, y, c, chips):
        me = 2 * x + y
        return [_remote(rin[t].at[c], rout[t].at[me, c], send, recv, base + 7 * t + j, (cx, cy, c))
                for j, (cx, cy) in enumerate(chips) for t in range(ni)]

    def passes(rout, send, recv, base, x, y, c, chips):
        out = []
        for j, (cx, cy) in enumerate(chips):
            for t in range(ni):
                landed = rout[t].at[2 * cx + cy, c]
                out.append((_remote(landed, landed, send, recv, base + 7 * t + j, (x, y, 1 - c)),
                            _remote(landed, landed, send, recv, base + 7 * t + 3 + j, (x, y, 1 - c))))
        return out

    def own(rin, rout, send, recv, base, x, y, c):
        return [_remote(rin[t], rout[t].at[2 * x + y], send, recv, base + 7 * t + 6, (x, y, 1 - c)) for t in range(ni)]

    def start(rin, rout, send, recv, base=0):
        x, y, c, chips = _place()
        for cp in first_hops(rin, rout, send, recv, base, x, y, c, chips) + own(rin, rout, send, recv, base, x, y, c):
            cp.start()

    def middle(rin, rout, send, recv, base=0):
        x, y, c, chips = _place()
        for arrival, onward in passes(rout, send, recv, base, x, y, c, chips):
            arrival.wait_recv()
            onward.start()

    def finish(rin, rout, send, recv, base=0):
        x, y, c, chips = _place()
        for j, (cx, cy) in enumerate(chips):
            for t in range(ni):
                other = rout[t].at[2 * cx + cy, 1 - c]
                _remote(other, other, send, recv, base + 7 * t + 3 + j, (x, y, 1 - c)).wait_recv()
        for cp in own(rin, rout, send, recv, base, x, y, c):
            cp.wait_recv()
            cp.wait_send()
        for cp in first_hops(rin, rout, send, recv, base, x, y, c, chips):
            cp.wait_send()
        for _, onward in passes(rout, send, recv, base, x, y, c, chips):
            onward.wait_send()

    return _Rider(shards, [_sds((4,) + a.shape, a.dtype) for a in shards], 7 * ni, start, finish, middle)


def _pair_rider(grads):
    ni = len(grads)

    def copies(rin, rout, send, recv, base):
        x, y, c, _ = _place()
        return [_remote(rin[t].at[:, 1 - c], rout[t], send, recv, base + t, (x, y, 1 - c)) for t in range(ni)]

    def start(rin, rout, send, recv, base=0):
        for cp in copies(rin, rout, send, recv, base):
            cp.start()

    def finish(rin, rout, send, recv, base=0):
        for cp in copies(rin, rout, send, recv, base):
            cp.wait()

    return _Rider(grads, [_sds(g.shape[:1] + g.shape[2:], g.dtype) for g in grads], ni, start, finish)


def _ici_rider(pair_sums):
    ni = len(pair_sums)

    def copies(rin, rout, send, recv, base):
        x, y, c, chips = _place()
        return [_remote(rin[t].at[j], rout[t].at[j], send, recv, base + 3 * t + j, (cx, cy, c))
                for j, (cx, cy) in enumerate(chips) for t in range(ni)]

    def start(rin, rout, send, recv, base=0):
        for cp in copies(rin, rout, send, recv, base):
            cp.start()

    def finish(rin, rout, send, recv, base=0):
        for cp in copies(rin, rout, send, recv, base):
            cp.wait()

    return _Rider(pair_sums, [_sds((3,) + g.shape[1:], g.dtype) for g in pair_sums], 3 * ni, start, finish)


def _final_rider(summed, slots):
    ni = len(summed)

    def copies(rout, send, recv, base):
        x, y, c, _ = _place()
        return [(_remote(rout[t].at[slots[t], c], rout[t].at[slots[t], c], send, recv, base + t, (x, y, 1 - c)),
                 _remote(rout[t].at[slots[t], 1 - c], rout[t].at[slots[t], 1 - c], send, recv, base + t, (x, y, 1 - c)))
                for t in range(ni)]

    def start(rin, rout, send, recv, base=0):
        for mine, _ in copies(rout, send, recv, base):
            mine.start()

    def finish(rin, rout, send, recv, base=0):
        for mine, theirs in copies(rout, send, recv, base):
            mine.wait_send()
            theirs.wait_recv()

    return _Rider(summed, [_sds(a.shape, a.dtype) for a in summed], ni, start, finish,
                  aliases={t: t for t in range(ni)})


def _call(body, *, name, grid, in_specs, out_specs, out_shape, scratch_shapes, operands, rider=None):
    if rider is None:
        outs = pl.pallas_call(body, name=name, grid=grid, in_specs=in_specs, out_specs=out_specs, out_shape=out_shape,
                              scratch_shapes=scratch_shapes, compiler_params=_params(len(grid)))(*operands)
        return list(outs), []
    n_in, n_out, n_scr = len(in_specs), len(out_specs), len(scratch_shapes)
    r_in, r_out = len(rider.inputs), len(rider.out_shapes)
    steps = 1
    for g in grid:
        steps *= g
    mid = max(steps - 1 - MIDDLE_STEPS_BEFORE_END, 0)

    def full_body(*refs):
        own_in, rin = refs[:n_in], refs[n_in:n_in + r_in]
        own_out = refs[n_in + r_in:n_in + r_in + n_out]
        rout = refs[n_in + r_in + n_out:n_in + r_in + n_out + r_out]
        own_scr = refs[n_in + r_in + n_out + r_out:n_in + r_in + n_out + r_out + n_scr]
        send, recv = refs[-2], refs[-1]
        step = pl.program_id(0)
        for axis in range(1, len(grid)):
            step = step * grid[axis] + pl.program_id(axis)

        @pl.when(step == 0)
        def _():
            rider.start(rin, rout, send, recv)

        body(*own_in, *own_out, *own_scr)

        if rider.middle is not None:
            @pl.when(step == mid)
            def _():
                rider.middle(rin, rout, send, recv)

        @pl.when(step == steps - 1)
        def _():
            rider.finish(rin, rout, send, recv)

    outs = pl.pallas_call(
        full_body, name=name, grid=grid,
        in_specs=list(in_specs) + [ANY] * r_in, out_specs=list(out_specs) + [ANY] * r_out,
        out_shape=list(out_shape) + rider.out_shapes,
        scratch_shapes=list(scratch_shapes) + [pltpu.SemaphoreType.DMA((rider.n_sems,)), pltpu.SemaphoreType.DMA((rider.n_sems,))],
        input_output_aliases={n_in + a: n_out + b for a, b in rider.aliases.items()},
        compiler_params=_params(len(grid)),
    )(*operands, *rider.inputs)
    return list(outs[:n_out]), list(outs[n_out:])


def _run_rider(rider, name):
    r_in, r_out = len(rider.inputs), len(rider.out_shapes)

    def body(*refs):
        rin, rout, send, recv = refs[:r_in], refs[r_in:r_in + r_out], refs[-2], refs[-1]
        rider.start(rin, rout, send, recv)
        if rider.middle is not None:
            rider.middle(rin, rout, send, recv)
        rider.finish(rin, rout, send, recv)

    outs = pl.pallas_call(
        body, name=name, in_specs=[ANY] * r_in, out_specs=[ANY] * r_out, out_shape=rider.out_shapes,
        scratch_shapes=[pltpu.SemaphoreType.DMA((rider.n_sems,)), pltpu.SemaphoreType.DMA((rider.n_sems,))],
        input_output_aliases=rider.aliases,
    )(*rider.inputs)
    return list(outs)


def _fwd_mix_a(h, gain, conv_w, w_in, w_out, name, rider=None):
    s, d = h.shape
    e = MIX_WIDTH
    ts = min(TS_MIX, s)
    nt = s // ts

    def body(h_ref, gain_ref, cw_ref, win_hbm, wout_hbm, h1_ref, proj_ref, win_v, wout_v, carry, sems):
        i = pl.program_id(0)

        @pl.when(i == 0)
        def _():
            _copy_all([(win_hbm, win_v), (wout_hbm, wout_v)], sems)
            carry[...] = jnp.zeros_like(carry)

        hh = h_ref[...]
        xn, _ = _rms(hh)
        hnb = (xn * gain_ref[...]).astype(BF16)
        b = _dot(hnb, win_v[0])
        c = _dot(hnb, win_v[1])
        v = _dot(hnb, win_v[2])
        z = _dot(hnb, win_v[3])
        proj_ref[:, 0 * e:1 * e] = b.astype(BF16)
        proj_ref[:, 1 * e:2 * e] = c.astype(BF16)
        proj_ref[:, 2 * e:3 * e] = v.astype(BF16)
        proj_ref[:, 3 * e:4 * e] = z.astype(BF16)
        cv = c * v
        tail = carry[...]
        carry[...] = cv[ts - HALO:ts]
        conv = cw_ref[0:1, :] * _shift_down(cv, 2, tail) + cw_ref[1:2, :] * _shift_down(cv, 1, tail) + cw_ref[2:3, :] * cv
        mb = ((z * _sigmoid(z)) * (b * conv)).astype(BF16)
        h1_ref[...] = hh + _dot(mb, wout_v[...])

    row = lambda width: pl.BlockSpec((ts, width), lambda i: (i, 0))
    return _call(
        body, name=name, grid=(nt,),
        in_specs=[row(d), _full((1, d)), _full((8, e)), ANY, ANY],
        out_specs=[row(d), row(4 * e)],
        out_shape=[_sds((s, d), F32), _sds((s, 4 * e), BF16)],
        scratch_shapes=[pltpu.VMEM((4, d, e), BF16), pltpu.VMEM((e, d), BF16), pltpu.VMEM((HALO, e), F32),
                        pltpu.SemaphoreType.DMA((2,))],
        operands=[h, gain, conv_w, w_in, w_out], rider=rider)


def _fwd_mix_b(h, gain, scale, w_in, w_grp, w_out, name, rider=None):
    s, d = h.shape
    e = MIX_WIDTH
    ts = min(TS_MIX, s)
    nt = s // ts

    def body(h_ref, gain_ref, scale_ref, win_hbm, wgrp_hbm, wout_hbm, h1_ref, z_ref, mx_ref, dd_ref,
             win_v, wgrp_v, wout_v, carry, sems):
        i = pl.program_id(0)

        @pl.when(i == 0)
        def _():
            _copy_all([(win_hbm, win_v), (wout_hbm, wout_v)] + _grp_pairs(wgrp_hbm, wgrp_v), sems)
            carry[...] = jnp.zeros_like(carry)

        hh = h_ref[...]
        xn, _ = _rms(hh)
        hnb = (xn * gain_ref[...]).astype(BF16)
        u = jnp.concatenate([_dot(hnb, win_v[0]), _dot(hnb, win_v[1])], axis=1)
        z = jnp.concatenate([_dot(hnb, win_v[2]), _dot(hnb, win_v[3])], axis=1)
        z_ref[...] = z.astype(BF16)
        diff = (_pool_fwd(u, carry, i, ts) - u).astype(BF16)
        dd_ref[...] = diff
        mx = jnp.concatenate(
            [_dot(diff[:, g * GROUP_DIM:(g + 1) * GROUP_DIM], wgrp_v[g]) for g in range(N_GROUPS)], axis=1)
        mx_ref[...] = mx.astype(BF16)
        mb = ((z * _sigmoid(z)) * (mx * scale_ref[...])).astype(BF16)
        h1_ref[...] = hh + _dot(mb, wout_v[...])

    row = lambda width: pl.BlockSpec((ts, width), lambda i: (i, 0))
    return _call(
        body, name=name, grid=(nt,),
        in_specs=[row(d), _full((1, d)), _full((1, e)), ANY, ANY, ANY],
        out_specs=[row(d), row(e), row(e), row(e)],
        out_shape=[_sds((s, d), F32)] + [_sds((s, e), BF16)] * 3,
        scratch_shapes=[pltpu.VMEM((4, d, e // 2), BF16), pltpu.VMEM((N_GROUPS, GROUP_DIM, GROUP_DIM), BF16),
                        pltpu.VMEM((e, d), BF16), pltpu.VMEM((4, HALO, e), F32), pltpu.SemaphoreType.DMA((18,))],
        operands=[h, gain, scale, w_in, w_grp, w_out], rider=rider)


def _fwd_ple(h1, p, gain, w_gate, w_proj, layer, rider=None):
    s, d = h1.shape
    pd = p.shape[-1]
    ts = min(TS_PLE, s)
    nt = s // ts

    def body(h1_ref, p_ref, gain_ref, wg_hbm, wp_hbm, h2_ref, gate_ref, wg_v, wp_v, sems):
        @pl.when(pl.program_id(0) == 0)
        def _():
            _copy_all([(wg_hbm, wg_v), (wp_hbm, wp_v)], sems)

        hh = h1_ref[...]
        xn, _ = _rms(hh)
        hpb = (xn * gain_ref[...]).astype(BF16)
        gate = _sigmoid(_dot(hpb, wg_v[...]))
        pb = p_ref[...].astype(BF16)
        pe = jnp.concatenate([_dot(pb, wp_v[k]) for k in range(4)], axis=1)
        gate_ref[...] = gate.astype(BF16)
        h2_ref[...] = hh + gate * pe

    row = lambda width: pl.BlockSpec((ts, width), lambda i: (i, 0))
    return _call(
        body, name=f"fwd_ple{layer}", grid=(nt,),
        in_specs=[row(d), pl.BlockSpec((None, ts, pd), lambda i: (layer, i, 0)), _full((1, d)), ANY, ANY],
        out_specs=[row(d), row(d)],
        out_shape=[_sds((s, d), F32), _sds((s, d), BF16)],
        scratch_shapes=[pltpu.VMEM((d, d), BF16), pltpu.VMEM((4, pd, d // 4), BF16), pltpu.SemaphoreType.DMA((2,))],
        operands=[h1, p, gain, w_gate, w_proj], rider=rider)


def _bwd_ple(dh2, h1, gate, p, gain, w_gate, w_proj, layer, rider=None, loss_head=None):
    s, d = dh2.shape
    pd = p.shape[-1]
    ts = min(TS_PLE, s)
    nt = s // ts
    qd = d // 4
    n_head = 0 if loss_head is None else 2

    def body(*refs):
        dh2_ref = refs[0]
        h1_ref, gate_ref, p_ref, gain_ref, wg_hbm, wp_hbm, dh1_ref, dgain_ref, dwg_hbm, dwp_hbm = refs[1 + n_head:11 + n_head]
        wg_v, wp_v, acc_g, acc_p, sems = refs[-5:]
        i = pl.program_id(0)

        @pl.when(i == 0)
        def _():
            _copy_all([(wg_hbm, wg_v), (wp_hbm, wp_v)], sems)
            dgain_ref[...] = jnp.zeros_like(dgain_ref)
            acc_g[...] = jnp.zeros_like(acc_g)
            acc_p[...] = jnp.zeros_like(acc_p)

        if loss_head is None:
            g2 = dh2_ref[...]
        else:
            t_ref, fgain_ref, loss_ref, dfgain_ref = refs[1], refs[2], refs[11 + n_head], refs[12 + n_head]

            @pl.when(i == 0)
            def _():
                loss_ref[...] = jnp.zeros_like(loss_ref)
                dfgain_ref[...] = jnp.zeros_like(dfgain_ref)

            xf, rf = _rms(dh2_ref[...])
            err = xf * fgain_ref[...] - t_ref[...]
            part = 0.5 * jnp.sum(jnp.mean(err * err, axis=-1, keepdims=True), axis=0, keepdims=True)
            loss_ref[...] += jnp.broadcast_to(part, loss_ref.shape)
            g2, dfgain = _rms_bwd(err * (1.0 / d), xf, rf, fgain_ref[...])
            dfgain_ref[...] += dfgain
        gate_f = gate_ref[...].astype(F32)
        xn, r = _rms(h1_ref[...])
        hpb = (xn * gain_ref[...]).astype(BF16)
        pb = p_ref[...].astype(BF16)
        pe = jnp.concatenate([_dot(pb, wp_v[k]) for k in range(4)], axis=1)
        dpeb = (g2 * gate_f).astype(BF16)
        dab = ((g2 * pe) * (gate_f * (1.0 - gate_f))).astype(BF16)
        acc_g[...] += _dot_tn(hpb, dab)
        for k in range(4):
            acc_p[k] += _dot_tn(pb, dpeb[:, k * qd:(k + 1) * qd])
        dhp = _dot_nt(dab, wg_v[...])
        dh, dgain = _rms_bwd(dhp, xn, r, gain_ref[...])
        dh1_ref[...] = g2 + dh
        dgain_ref[...] += dgain

        @pl.when(i == nt - 1)
        def _():
            _copy_all([(acc_g, dwg_hbm), (acc_p, dwp_hbm)], sems)

    row = pl.BlockSpec((ts, d), lambda i: (i, 0))
    head = loss_head is not None
    return _call(
        body, name=f"bwd_ple{layer}", grid=(nt,),
        in_specs=[row] + ([row, _full((1, d))] if head else [])
        + [row, row, pl.BlockSpec((None, ts, pd), lambda i: (layer, i, 0)), _full((1, d)), ANY, ANY],
        out_specs=[row, _full((1, d)), ANY, ANY] + ([_full((8, 128)), _full((1, d))] if head else []),
        out_shape=[_sds((s, d), F32), _sds((1, d), F32), _sds((d, d), F32), _sds((4, pd, qd), F32)]
        + ([_sds((8, 128), F32), _sds((1, d), F32)] if head else []),
        scratch_shapes=[pltpu.VMEM((d, d), BF16), pltpu.VMEM((4, pd, qd), BF16), pltpu.VMEM((d, d), F32),
                        pltpu.VMEM((4, pd, qd), F32), pltpu.SemaphoreType.DMA((2,))],
        operands=[dh2] + (list(loss_head) if head else []) + [h1, gate, p, gain, w_gate, w_proj], rider=rider)


def _mix_a_tile_grads(proj_ref, ch_ref, vh_ref, cw_ref, dh1b, wout_v, carry, dcw_ref, tile, hb):
    e = MIX_WIDTH
    b = proj_ref[:, 0 * e:1 * e].astype(F32)
    c = proj_ref[:, 1 * e:2 * e].astype(F32)
    v = proj_ref[:, 2 * e:3 * e].astype(F32)
    z = proj_ref[:, 3 * e:4 * e].astype(F32)
    cv = c * v
    prev = (ch_ref[...].astype(F32) * vh_ref[...].astype(F32))[hb - HALO:hb]
    tail = jnp.where(tile > 0, prev, jnp.zeros_like(prev))
    cv1 = _shift_down(cv, 1, tail)
    cv2 = _shift_down(cv, 2, tail)
    conv = cw_ref[0:1, :] * cv2 + cw_ref[1:2, :] * cv1 + cw_ref[2:3, :] * cv
    sig = _sigmoid(z)
    sz = z * sig
    y = b * conv
    dm = _dot_nt(dh1b, wout_v[...])
    dz = (dm * y) * (sig * (1.0 + z * (1.0 - sig)))
    dy = dm * sz
    db = dy * conv
    dconv = dy * b
    head = carry[...]
    carry[...] = dconv[0:HALO]
    dcv = cw_ref[2:3, :] * dconv + cw_ref[1:2, :] * _shift_up(dconv, 1, head) + cw_ref[0:1, :] * _shift_up(dconv, 2, head)
    dcw_ref[0:1, :] += jnp.sum(dconv * cv2, axis=0, keepdims=True)
    dcw_ref[1:2, :] += jnp.sum(dconv * cv1, axis=0, keepdims=True)
    dcw_ref[2:3, :] += jnp.sum(dconv * cv, axis=0, keepdims=True)
    parts = [db.astype(BF16), (dcv * v).astype(BF16), (dcv * c).astype(BF16), dz.astype(BF16)]
    return parts, (sz * y).astype(BF16)


def _bwd_mix_a(dh1, h, proj, gain, conv_w, w_in, w_out, name, rider=None):
    s, d = dh1.shape
    e = MIX_WIDTH
    ts = min(TS_MIX, s)
    nt = s // ts
    hb = 16
    per = ts // hb

    def body(dh1_ref, h_ref, proj_ref, ch_ref, vh_ref, gain_ref, cw_ref, win_hbm, wout_hbm,
             dh_ref, dcw_ref, dgain_ref, dwin_hbm, dwout_hbm, win_v, wout_v, acc_in, acc_out, carry, sems):
        i = pl.program_id(0)

        @pl.when(i == 0)
        def _():
            _copy_all([(win_hbm, win_v), (wout_hbm, wout_v)], sems)
            carry[...] = jnp.zeros_like(carry)
            dcw_ref[...] = jnp.zeros_like(dcw_ref)
            dgain_ref[...] = jnp.zeros_like(dgain_ref)
            acc_in[...] = jnp.zeros_like(acc_in)
            acc_out[...] = jnp.zeros_like(acc_out)

        dh1 = dh1_ref[...]
        dh1b = dh1.astype(BF16)
        parts, mb = _mix_a_tile_grads(proj_ref, ch_ref, vh_ref, cw_ref, dh1b, wout_v, carry, dcw_ref, nt - 1 - i, hb)
        acc_out[...] += _dot_tn(mb, dh1b)
        xn, r = _rms(h_ref[...])
        hnb = (xn * gain_ref[...]).astype(BF16)
        for q in range(4):
            acc_in[q] += _dot_tn(hnb, parts[q])
        dhn = _dot_nt(parts[0], win_v[0]) + _dot_nt(parts[1], win_v[1]) + _dot_nt(parts[2], win_v[2]) + _dot_nt(parts[3], win_v[3])
        dh, dgain = _rms_bwd(dhn, xn, r, gain_ref[...])
        dh_ref[...] = dh1 + dh
        dgain_ref[...] += dgain

        @pl.when(i == nt - 1)
        def _():
            _copy_all([(acc_in, dwin_hbm), (acc_out, dwout_hbm)], sems)

    row = lambda width: pl.BlockSpec((ts, width), lambda i: (nt - 1 - i, 0))
    halo = lambda col: pl.BlockSpec((hb, e), lambda i: (jnp.maximum((nt - 1 - i) * per - 1, 0), col))
    return _call(
        body, name=name, grid=(nt,),
        in_specs=[row(d), row(d), row(4 * e), halo(1), halo(2), _full((1, d)), _full((8, e)), ANY, ANY],
        out_specs=[row(d), _full((8, e)), _full((1, d)), ANY, ANY],
        out_shape=[_sds((s, d), F32), _sds((8, e), F32), _sds((1, d), F32), _sds((4, d, e), F32), _sds((e, d), F32)],
        scratch_shapes=[pltpu.VMEM((4, d, e), BF16), pltpu.VMEM((e, d), BF16), pltpu.VMEM((4, d, e), F32),
                        pltpu.VMEM((e, d), F32), pltpu.VMEM((HALO, e), F32), pltpu.SemaphoreType.DMA((2,))],
        operands=[dh1, h, proj, proj, proj, gain, conv_w, w_in, w_out], rider=rider)


def _bwd_mix_a_weights(dh1, h, proj, gain, conv_w, w_out, name, rider=None):
    s, d = dh1.shape
    e = MIX_WIDTH
    ts = min(TS_MIX, s)
    nt = s // ts
    hb = 16
    per = ts // hb

    def body(dh1_ref, h_ref, proj_ref, ch_ref, vh_ref, gain_ref, cw_ref, wout_hbm,
             dproj_ref, dcw_ref, dwin_hbm, dwout_hbm, wout_v, acc_in, acc_out, carry, sems):
        i = pl.program_id(0)

        @pl.when(i == 0)
        def _():
            _copy_all([(wout_hbm, wout_v)], sems)
            carry[...] = jnp.zeros_like(carry)
            dcw_ref[...] = jnp.zeros_like(dcw_ref)
            acc_in[...] = jnp.zeros_like(acc_in)
            acc_out[...] = jnp.zeros_like(acc_out)

        dh1b = dh1_ref[...].astype(BF16)
        parts, mb = _mix_a_tile_grads(proj_ref, ch_ref, vh_ref, cw_ref, dh1b, wout_v, carry, dcw_ref, nt - 1 - i, hb)
        acc_out[...] += _dot_tn(mb, dh1b)
        xn, _ = _rms(h_ref[...])
        hnb = (xn * gain_ref[...]).astype(BF16)
        for q in range(4):
            acc_in[q] += _dot_tn(hnb, parts[q])
            dproj_ref[:, q * e:(q + 1) * e] = parts[q]

        @pl.when(i == nt - 1)
        def _():
            _copy_all([(acc_in, dwin_hbm), (acc_out, dwout_hbm)], sems)

    row = lambda width: pl.BlockSpec((ts, width), lambda i: (nt - 1 - i, 0))
    halo = lambda col: pl.BlockSpec((hb, e), lambda i: (jnp.maximum((nt - 1 - i) * per - 1, 0), col))
    return _call(
        body, name=name, grid=(nt,),
        in_specs=[row(d), row(d), row(4 * e), halo(1), halo(2), _full((1, d)), _full((8, e)), ANY],
        out_specs=[row(4 * e), _full((8, e)), ANY, ANY],
        out_shape=[_sds((s, 4 * e), BF16), _sds((8, e), F32), _sds((4, d, e), F32), _sds((e, d), F32)],
        scratch_shapes=[pltpu.VMEM((e, d), BF16), pltpu.VMEM((4, d, e), F32), pltpu.VMEM((e, d), F32),
                        pltpu.VMEM((HALO, e), F32), pltpu.SemaphoreType.DMA((2,))],
        operands=[dh1, h, proj, proj, proj, gain, conv_w, w_out], rider=rider)


def _bwd_mix_a_input(dproj, h, dh1, gain, w_in, name, rider=None):
    s, d = dh1.shape
    e = MIX_WIDTH
    ts = min(TS_PLE, s)
    nt = s // ts

    def body(dproj_ref, h_ref, dh1_ref, gain_ref, win_hbm, dh_ref, dgain_ref, win_v, sems):
        @pl.when(pl.program_id(0) == 0)
        def _():
            _copy_all([(win_hbm, win_v)], sems)
            dgain_ref[...] = jnp.zeros_like(dgain_ref)

        dhn = _dot_nt(dproj_ref[:, 0:e], win_v[0])
        for q in range(1, 4):
            dhn = dhn + _dot_nt(dproj_ref[:, q * e:(q + 1) * e], win_v[q])
        xn, r = _rms(h_ref[...])
        dh, dgain = _rms_bwd(dhn, xn, r, gain_ref[...])
        dh_ref[...] = dh1_ref[...] + dh
        dgain_ref[...] += dgain

    row = lambda width: pl.BlockSpec((ts, width), lambda i: (i, 0))
    return _call(
        body, name=name, grid=(nt,),
        in_specs=[row(4 * e), row(d), row(d), _full((1, d)), ANY],
        out_specs=[row(d), _full((1, d))],
        out_shape=[_sds((s, d), F32), _sds((1, d), F32)],
        scratch_shapes=[pltpu.VMEM((4, d, e), BF16), pltpu.SemaphoreType.DMA((1,))],
        operands=[dproj, h, dh1, gain, w_in], rider=rider)


def _bwd_mix_b(dh1, h, z, mx, diff, gain, scale, w_in, w_grp, w_out, name, rider=None):
    s, d = dh1.shape
    e = MIX_WIDTH
    ts = min(TS_MIX, s)
    nt = s // ts
    half = e // 2

    def body(dh1_ref, h_ref, z_ref, mx_ref, dd_ref, gain_ref, scale_ref, win_hbm, wgrp_hbm, wout_hbm,
             dh_ref, dscale_ref, dgain_ref, dwin_hbm, dwgrp_hbm, dwout_hbm,
             win_v, wgrp_v, wout_v, acc_in, acc_grp, acc_out, carry, sems):
        i = pl.program_id(0)
        tile = nt - 1 - i

        @pl.when(i == 0)
        def _():
            _copy_all([(win_hbm, win_v), (wout_hbm, wout_v)] + _grp_pairs(wgrp_hbm, wgrp_v), sems)
            carry[...] = jnp.zeros_like(carry)
            dscale_ref[...] = jnp.zeros_like(dscale_ref)
            dgain_ref[...] = jnp.zeros_like(dgain_ref)
            acc_in[...] = jnp.zeros_like(acc_in)
            acc_grp[...] = jnp.zeros_like(acc_grp)
            acc_out[...] = jnp.zeros_like(acc_out)

        zf = z_ref[...].astype(F32)
        mxf = mx_ref[...].astype(F32)
        sig = _sigmoid(zf)
        sz = zf * sig
        mixed = mxf * scale_ref[...]
        dh1 = dh1_ref[...]
        dh1b = dh1.astype(BF16)
        acc_out[...] += _dot_tn((sz * mixed).astype(BF16), dh1b)
        dm = _dot_nt(dh1b, wout_v[...])
        dz = (dm * mixed) * (sig * (1.0 + zf * (1.0 - sig)))
        dmixed = dm * sz
        dscale_ref[...] += jnp.sum(dmixed * mxf, axis=0, keepdims=True)
        dmxb = (dmixed * scale_ref[...]).astype(BF16)
        diff = dd_ref[...]
        for g in range(N_GROUPS):
            cols = slice(g * GROUP_DIM, (g + 1) * GROUP_DIM)
            acc_grp[g] += _dot_tn(diff[:, cols], dmxb[:, cols])
        ddiff = jnp.concatenate(
            [_dot_nt(dmxb[:, g * GROUP_DIM:(g + 1) * GROUP_DIM], wgrp_v[g]) for g in range(N_GROUPS)], axis=1)
        dub = (_pool_bwd(ddiff, carry, tile, ts) - ddiff).astype(BF16)
        dzb = dz.astype(BF16)
        parts = [dub[:, 0:half], dub[:, half:e], dzb[:, 0:half], dzb[:, half:e]]
        xn, r = _rms(h_ref[...])
        hnb = (xn * gain_ref[...]).astype(BF16)
        for k in range(4):
            acc_in[k] += _dot_tn(hnb, parts[k])
        dhn = _dot_nt(parts[0], win_v[0]) + _dot_nt(parts[1], win_v[1]) + _dot_nt(parts[2], win_v[2]) + _dot_nt(parts[3], win_v[3])
        dh, dgain = _rms_bwd(dhn, xn, r, gain_ref[...])
        dh_ref[...] = dh1 + dh
        dgain_ref[...] += dgain

        @pl.when(i == nt - 1)
        def _():
            _copy_all([(acc_in, dwin_hbm), (acc_out, dwout_hbm)] + [(v, hb_) for hb_, v in _grp_pairs(dwgrp_hbm, acc_grp)], sems)

    row = lambda width: pl.BlockSpec((ts, width), lambda i: (nt - 1 - i, 0))
    return _call(
        body, name=name, grid=(nt,),
        in_specs=[row(d), row(d), row(e), row(e), row(e), _full((1, d)), _full((1, e)), ANY, ANY, ANY],
        out_specs=[row(d), _full((1, e)), _full((1, d)), ANY, ANY, ANY],
        out_shape=[_sds((s, d), F32), _sds((1, e), F32), _sds((1, d), F32), _sds((4, d, half), F32),
                   _sds((4, N_GROUPS, GROUP_DIM // 4, GROUP_DIM), F32), _sds((e, d), F32)],
        scratch_shapes=[pltpu.VMEM((4, d, half), BF16), pltpu.VMEM((N_GROUPS, GROUP_DIM, GROUP_DIM), BF16),
                        pltpu.VMEM((e, d), BF16), pltpu.VMEM((4, d, half), F32),
                        pltpu.VMEM((N_GROUPS, GROUP_DIM, GROUP_DIM), F32), pltpu.VMEM((e, d), F32),
                        pltpu.VMEM((4, HALO, e), F32), pltpu.SemaphoreType.DMA((18,))],
        operands=[dh1, h, z, mx, diff, gain, scale, w_in, w_grp, w_out], rider=rider)


def _first_gather(shards, small):
    rider = _gather_rider(shards)
    ni = len(shards)

    def body(*refs):
        rin, small_src = refs[:ni], refs[ni]
        rout, small_dst = refs[ni + 1:2 * ni + 1], refs[2 * ni + 1]
        send, recv, ssend, srecv = refs[2 * ni + 2:]
        x, y, c, chips = _place()
        me = 2 * x + y
        peers = [(cx, cy, c) for cx, cy in chips] + [(x, y, 1 - c)]
        vec = [_remote(small_src, small_dst.at[me], ssend, srecv, j, to) for j, to in enumerate(peers)]
        for cp in vec:
            cp.start()
        rider.start(rin, rout, send, recv)
        rider.middle(rin, rout, send, recv)
        rider.finish(rin, rout, send, recv)
        for j, (px, py, _) in enumerate(peers):
            _remote(small_src, small_dst.at[2 * px + py], ssend, srecv, j, peers[j]).wait_recv()
        for cp in vec:
            cp.wait_send()

    outs = pl.pallas_call(
        body, name="first_gather", in_specs=[ANY] * (ni + 1), out_specs=[ANY] * (ni + 1),
        out_shape=rider.out_shapes + [_sds((4,) + small.shape, small.dtype)],
        scratch_shapes=[pltpu.SemaphoreType.DMA((rider.n_sems,)), pltpu.SemaphoreType.DMA((rider.n_sems,)),
                        pltpu.SemaphoreType.DMA((4,)), pltpu.SemaphoreType.DMA((4,))],
    )(*shards, small)
    return list(outs[:ni]), outs[ni]


def _vector_rider(pack):
    flips = [(fx, fy, fc) for fx in (0, 1) for fy in (0, 1) for fc in (0, 1)][1:]

    def copies(rin, rout, send, recv, base):
        x, y, c, _ = _place()
        me = 4 * x + 2 * y + c
        peers = [(1 - x if fx else x, 1 - y if fy else y, 1 - c if fc else c) for fx, fy, fc in flips]
        own = pltpu.make_async_copy(rin[0], rout[0].at[me], send.at[base + 7])
        out = [_remote(rin[0], rout[0].at[me], send, recv, base + r, peer) for r, peer in enumerate(peers)]
        back = [_remote(rin[0], rout[0].at[4 * px + 2 * py + pc], send, recv, base + r, (px, py, pc))
                for r, (px, py, pc) in enumerate(peers)]
        return own, out, back

    def start(rin, rout, send, recv, base=0):
        own, out, _ = copies(rin, rout, send, recv, base)
        own.start()
        for cp in out:
            cp.start()

    def finish(rin, rout, send, recv, base=0):
        own, out, back = copies(rin, rout, send, recv, base)
        for cp in back:
            cp.wait_recv()
        for cp in out:
            cp.wait_send()
        own.wait()

    return _Rider([pack], [_sds((8,) + pack.shape, pack.dtype)], 8, start, finish)


def _vector_sum(landed):
    _, rows, d = landed.shape

    def body(l_ref, out_ref):
        total = l_ref[0]
        for dev in range(1, 8):
            total = total + l_ref[dev]
        out_ref[...] = total

    vmem = pl.BlockSpec(memory_space=pltpu.VMEM)
    return pl.pallas_call(body, name="vector_sum", in_specs=[vmem], out_specs=vmem, out_shape=_sds((rows, d), F32))(landed)


def _ew_rows(rows):
    return min(TR_EW, rows)


def _job_rows(rows, cols):
    return min(rows, max(8, JOB_BLOCK_BYTES // (4 * cols)))


def _pair_sum_job(grad, sibling_rows):
    _, _, rh, cols = grad.shape
    tr = _job_rows(rh, cols)
    nr = rh // tr

    def chip_of(j, pos):
        return jnp.bitwise_xor(pos[0], jnp.where(j == 2, 3, 2 - j))

    return dict(
        ins=[(grad, (None, None, tr, cols), lambda l, pos: (chip_of(l // nr, pos), pos[1], l % nr, 0)),
             (sibling_rows, (None, tr, cols), lambda l, pos: (chip_of(l // nr, pos), l % nr, 0))],
        out=((3, rh, cols), BF16, (None, tr, cols), lambda l, pos: (l // nr, l % nr, 0)),
        steps=3 * nr, fn=lambda g, sb: (g + sb).astype(BF16), alias=None)


def _final_sum_job(grad, sibling_rows, landed, stack, slot, n_slots):
    _, _, rh, cols = grad.shape
    tr = _job_rows(rh, cols)

    def fn(g, sb, ld):
        total = g + sb
        for j in range(3):
            total = total + ld[j].astype(F32)
        return total

    return dict(
        ins=[(grad, (None, None, tr, cols), lambda l, pos: (pos[0], pos[1], l, 0)),
             (sibling_rows, (None, tr, cols), lambda l, pos: (pos[0], l, 0)),
             (landed, (3, tr, cols), lambda l, pos: (0, l, 0))],
        out=((n_slots, 2, rh, cols), F32, (None, None, tr, cols), lambda l, pos: (slot, pos[1], l, 0)),
        steps=rh // tr, fn=fn, alias=stack)


def _run_jobs(jobs, place, name):
    starts, total = [], 0
    for jb in jobs:
        starts.append(total)
        total += jb["steps"]

    def clamped(fn, start, steps):
        return lambda s, pos: fn(jnp.clip(s - start, 0, steps - 1), pos)

    in_specs, operands = [], [place]
    for jb, start in zip(jobs, starts):
        for arr, block, fn in jb["ins"]:
            in_specs.append(pl.BlockSpec(block, clamped(fn, start, jb["steps"])))
            operands.append(arr)
    n_ins = len(in_specs)
    aliases = {}
    for t, jb in enumerate(jobs):
        if jb["alias"] is not None:
            in_specs.append(ANY)
            operands.append(jb["alias"])
            aliases[len(operands) - 1] = t
    out_specs = [pl.BlockSpec(jb["out"][2], clamped(jb["out"][3], start, jb["steps"])) for jb, start in zip(jobs, starts)]

    def body(place_ref, *refs):
        in_refs, out_refs = refs[:n_ins], refs[len(in_specs):]
        s = pl.program_id(0)
        first = 0
        for t, (jb, start) in enumerate(zip(jobs, starts)):
            mine = in_refs[first:first + len(jb["ins"])]
            first += len(jb["ins"])

            @pl.when((s >= start) & (s < start + jb["steps"]))
            def _(mine=mine, t=t, jb=jb):
                out_refs[t][...] = jb["fn"](*[r[...] for r in mine])

    grid_spec = pltpu.PrefetchScalarGridSpec(num_scalar_prefetch=1, grid=(total,), in_specs=in_specs, out_specs=out_specs)
    outs = pl.pallas_call(body, name=name, grid_spec=grid_spec,
                          out_shape=[_sds(jb["out"][0], jb["out"][1]) for jb in jobs],
                          input_output_aliases=aliases, compiler_params=_params(1))(*operands)
    return list(outs)


def _adamw(g, w, m, v, name):
    rows, cols = g.shape
    tr = _ew_rows(rows)

    def body(g_ref, w_ref, m_ref, v_ref, delta_ref, nm_ref, nv_ref):
        gg = g_ref[...]
        nm = ADAM_B1 * m_ref[...] + (1.0 - ADAM_B1) * gg
        nv = ADAM_B2 * v_ref[...] + (1.0 - ADAM_B2) * (gg * gg)
        m_hat = nm / (1.0 - ADAM_B1 ** ADAM_STEP)
        v_hat = nv / (1.0 - ADAM_B2 ** ADAM_STEP)
        delta_ref[...] = -ADAM_LR * (m_hat / (jnp.sqrt(v_hat) + ADAM_EPS) + ADAM_WD * w_ref[...])
        nm_ref[...] = nm
        nv_ref[...] = nv

    spec = pl.BlockSpec((tr, cols), lambda i: (i, 0))
    return pl.pallas_call(
        body, name=name, grid=(rows // tr,), in_specs=[spec] * 4, out_specs=[spec] * 3,
        out_shape=[_sds((rows, cols), F32)] * 3, compiler_params=_params(),
    )(g, w, m, v)


BIG = ["a_w_in", "a_w_out", "b_w_in", "b_w_grp", "b_w_out", "ple_w_gate", "ple_w_proj"]

GATHER_PLAN = {
    "first": [("a_w_in", 0), ("a_w_out", 0)],
    "mix0": [("ple_w_gate", 0), ("ple_w_proj", 0), ("b_w_in", 0), ("b_w_grp", 0), ("b_w_out", 0)],
    "ple0": [("ple_w_gate", 1), ("ple_w_proj", 1)],
    "mix1": [("a_w_in", 1)],
    "ple1": [("ple_w_gate", 2), ("ple_w_proj", 2), ("a_w_out", 1)],
    "mix2": [("b_w_in", 1), ("b_w_grp", 1), ("b_w_out", 1), ("ple_w_gate", 3), ("ple_w_proj", 3)],
}


def _as_2d(name, a):
    if name == "b_w_grp":
        return a.reshape(a.shape[0], N_GROUPS * (GROUP_DIM // 4), GROUP_DIM)
    return a


def kernel(x, p, norm_mix, a_w_in, a_w_conv, a_w_out, b_w_in, b_w_grp, b_scale, b_w_out, ple_norm, ple_w_gate, ple_w_proj, final_norm, loss_target, m_norm_mix, m_a_w_in, m_a_w_conv, m_a_w_out, m_b_w_in, m_b_w_grp, m_b_scale, m_b_w_out, m_ple_norm, m_ple_w_gate, m_ple_w_proj, m_final_norm, v_norm_mix, v_a_w_in, v_a_w_conv, v_a_w_out, v_b_w_in, v_b_w_grp, v_b_scale, v_b_w_out, v_ple_norm, v_ple_w_gate, v_ple_w_proj, v_final_norm):
    d, e = D_MODEL, MIX_WIDTH
    s = x.shape[1]
    cx, cy, cc = lax.axis_index("x"), lax.axis_index("y"), lax.axis_index("c")
    chip = 2 * cx + cy
    place = jnp.stack([chip, cc]).astype(jnp.int32)

    weights = dict(a_w_in=a_w_in, a_w_out=a_w_out, b_w_in=b_w_in, b_w_grp=b_w_grp, b_w_out=b_w_out,
                   ple_w_gate=ple_w_gate, ple_w_proj=ple_w_proj)
    moms = dict(a_w_in=m_a_w_in, a_w_out=m_a_w_out, b_w_in=m_b_w_in, b_w_grp=m_b_w_grp, b_w_out=m_b_w_out,
                ple_w_gate=m_ple_w_gate, ple_w_proj=m_ple_w_proj)
    vars_ = dict(a_w_in=v_a_w_in, a_w_out=v_a_w_out, b_w_in=v_b_w_in, b_w_grp=v_b_w_grp, b_w_out=v_b_w_out,
                 ple_w_gate=v_ple_w_gate, ple_w_proj=v_ple_w_proj)
    w2d = {nm: _as_2d(nm, weights[nm]) for nm in BIG}
    bf = {nm: w2d[nm].astype(BF16) for nm in BIG}

    def shard_of(key):
        nm, j = key
        a = bf[nm][j]
        return a.reshape(2, a.shape[0] // 2, a.shape[1])

    gathered = {}

    def gather_rider(host):
        keys = GATHER_PLAN.get(host)
        return _gather_rider([shard_of(k) for k in keys]) if keys else None

    def keep(host, landed):
        for k, a in zip(GATHER_PLAN.get(host, []), landed):
            gathered[k] = a

    def weight(nm, j):
        a = gathered[(nm, j)]
        shapes = {"a_w_in": (4, d, e), "a_w_out": (e, d), "b_w_in": (4, d, e // 2),
                  "b_w_grp": (4, N_GROUPS, GROUP_DIM // 4, GROUP_DIM), "b_w_out": (e, d), "ple_w_gate": (d, d),
                  "ple_w_proj": (4, PLE_DIM, d // 4)}
        return a.reshape(shapes[nm])

    small = jnp.concatenate([a_w_conv.reshape(6, e // 4), b_scale], axis=0)
    landed, small_full = _first_gather([shard_of(k) for k in GATHER_PLAN["first"]], small)
    keep("first", landed)
    small_full = small_full.transpose(1, 0, 2).reshape(8, e)
    conv_w = [jnp.concatenate([small_full[3 * j:3 * j + 3], jnp.zeros((5, e), F32)], axis=0) for j in range(2)]
    scale_w = [small_full[6 + j:7 + j] for j in range(2)]

    p3 = p.reshape(DEPTH, s, PLE_DIM)
    mix_gain = [norm_mix[i:i + 1] for i in range(DEPTH)]
    ple_gain = [ple_norm[i:i + 1] for i in range(DEPTH)]

    h = x.reshape(s, d)
    saved = []
    for i in range(DEPTH):
        j = i // 2
        rider = gather_rider(f"mix{i}")
        if i % 2 == 0:
            (h1, proj), landed = _fwd_mix_a(h, mix_gain[i], conv_w[j], weight("a_w_in", j), weight("a_w_out", j),
                                            f"fwd_mix_a{j}", rider)
            mix = dict(proj=proj)
        else:
            (h1, zb, mx, diff), landed = _fwd_mix_b(h, mix_gain[i], scale_w[j], weight("b_w_in", j), weight("b_w_grp", j),
                                                    weight("b_w_out", j), f"fwd_mix_b{j}", rider)
            mix = dict(z=zb, mx=mx, diff=diff)
        keep(f"mix{i}", landed)
        (h2, gate), landed = _fwd_ple(h1, p3, ple_gain[i], weight("ple_w_gate", i), weight("ple_w_proj", i), i,
                                      gather_rider(f"ple{i}"))
        keep(f"ple{i}", landed)
        saved.append(dict(h=h, h1=h1, gate=gate, **mix))
        h = h2

    n_slots = {nm: weights[nm].shape[0] for nm in BIG}
    stacks = {nm: None for nm in BIG}

    class Group:
        def __init__(self, keys, grads):
            self.keys, self.stage = keys, 0
            self.g32 = [g.reshape(4, 2, w2d[nm].shape[1] // 2, w2d[nm].shape[2]) for (nm, _), g in zip(keys, grads)]

        def rider(self):
            if self.stage == 0:
                return _pair_rider(self.g32)
            if self.stage == 1:
                return _ici_rider(self.pair_sums)
            return _final_rider([stacks[nm] for nm, _ in self.keys], [j for _, j in self.keys])

        def jobs_after(self, landed):
            if self.stage == 0:
                self.from_sibling = landed
                return [_pair_sum_job(g, sb) for g, sb in zip(self.g32, landed)]
            if self.stage == 1:
                return [_final_sum_job(g, sb, ld, stacks[nm], j, n_slots[nm])
                        for (nm, j), g, sb, ld in zip(self.keys, self.g32, self.from_sibling, landed)]
            return []

        def advance(self, landed, summed):
            if self.stage == 0:
                self.pair_sums = summed
            else:
                for (nm, _), a in zip(self.keys, summed if self.stage == 1 else landed):
                    stacks[nm] = a
            self.stage += 1

    active = []
    batches = [0]

    def riders_now():
        parts = [g.rider() for g in active]
        return parts, _merge(parts)

    def advance_all(parts, landed):
        groups = list(active)
        pieces = _split(landed, parts)
        jobs = [g.jobs_after(l) for g, l in zip(groups, pieces)]
        flat = sum(jobs, [])
        outs = _run_jobs(flat, place, f"reduce_sums{batches[0]}") if flat else []
        batches[0] += 1
        for g, l, jb in zip(groups, pieces, jobs):
            g.advance(l, outs[:len(jb)])
            outs = outs[len(jb):]
            if g.stage == 3:
                active.remove(g)

    d_mix_gain, d_ple_gain = [None] * DEPTH, [None] * DEPTH
    d_conv, d_scale = [None] * 2, [None] * 2
    for i in reversed(range(DEPTH)):
        j = i // 2
        sv = saved[i]
        parts, rider = riders_now()
        if i == DEPTH - 1:
            (dh1, d_ple_gain[i], dwg, dwp, loss_part, d_final), landed = _bwd_ple(
                h, sv["h1"], sv["gate"], p3, ple_gain[i], weight("ple_w_gate", i), weight("ple_w_proj", i), i, rider,
                loss_head=(loss_target.reshape(s, d), final_norm.reshape(1, d)))
        else:
            (dh1, d_ple_gain[i], dwg, dwp), landed = _bwd_ple(
                dh, sv["h1"], sv["gate"], p3, ple_gain[i], weight("ple_w_gate", i), weight("ple_w_proj", i), i, rider)
        advance_all(parts, landed)
        active.append(Group([("ple_w_gate", i), ("ple_w_proj", i)], [dwg, dwp]))
        parts, rider = riders_now()
        if i == 0:
            (dproj, d_conv[0], dwin, dwout), landed = _bwd_mix_a_weights(
                dh1, sv["h"], sv["proj"], mix_gain[0], conv_w[0], weight("a_w_out", 0), "bwd_mix_a0_weights", rider)
            advance_all(parts, landed)
            active.append(Group([("a_w_in", 0), ("a_w_out", 0)], [dwin, dwout]))
            parts, rider = riders_now()
            advance_all(parts, _run_rider(rider, "pair_exchange0"))
            parts, rider = riders_now()
            (dh, d_mix_gain[0]), landed = _bwd_mix_a_input(dproj, sv["h"], dh1, mix_gain[0], weight("a_w_in", 0),
                                                          "bwd_mix_a0_input", rider)
            advance_all(parts, landed)
            continue
        if i % 2 == 0:
            (dh, d_conv[j], d_mix_gain[i], dwin, dwout), landed = _bwd_mix_a(
                dh1, sv["h"], sv["proj"], mix_gain[i], conv_w[j], weight("a_w_in", j), weight("a_w_out", j),
                f"bwd_mix_a{j}", rider)
            new = Group([("a_w_in", j), ("a_w_out", j)], [dwin, dwout])
        else:
            (dh, d_scale[j], d_mix_gain[i], dwin, dwgrp, dwout), landed = _bwd_mix_b(
                dh1, sv["h"], sv["z"], sv["mx"], sv["diff"], mix_gain[i], scale_w[j], weight("b_w_in", j),
                weight("b_w_grp", j), weight("b_w_out", j), f"bwd_mix_b{j}", rider)
            new = Group([("b_w_in", j), ("b_w_grp", j), ("b_w_out", j)], [dwin, dwgrp, dwout])
        advance_all(parts, landed)
        active.append(new)
    grad_x = dh.reshape(1, s, d)

    pack = jnp.concatenate(
        d_mix_gain + d_ple_gain + [d_final] + [d_conv[0][0:3], d_conv[1][0:3]] + d_scale
        + [jnp.tile(loss_part[0:1], (1, d // 128)), jnp.zeros((SMALL_ROWS - 18, d), F32)], axis=0)
    vectors = _vector_rider(pack)
    tail = 0
    while active:
        parts, _ = riders_now()
        extra = [vectors] if tail == 0 else []
        landed = _run_rider(_merge(parts + extra), f"tail_exchange{tail}")
        if extra:
            total = _vector_sum(_split(landed, parts + extra)[-1][0])
        advance_all(parts, landed)
        tail += 1
    loss = total[17, 0]

    out_grad, out_delta, out_m, out_v = {}, {}, {}, {}
    for nm in BIG:
        shape = weights[nm].shape
        flat = (w2d[nm].shape[0] * w2d[nm].shape[1], w2d[nm].shape[2])
        g2 = stacks[nm].reshape(flat)
        delta, new_m, new_v = _adamw(g2, w2d[nm].reshape(flat), _as_2d(nm, moms[nm]).reshape(flat),
                                     _as_2d(nm, vars_[nm]).reshape(flat), f"adamw_{nm}")
        out_grad[nm], out_delta[nm] = g2.reshape(shape), delta.reshape(shape)
        out_m[nm], out_v[nm] = new_m.reshape(shape), new_v.reshape(shape)

    rep_rows = 16
    rep = lambda a, b_, c_: jnp.concatenate([a, b_, c_.reshape(1, d), jnp.zeros((rep_rows - 9, d), F32)], axis=0)
    rep_delta, rep_m, rep_v = _adamw(
        jnp.concatenate([total[0:9], jnp.zeros((rep_rows - 9, d), F32)], axis=0),
        rep(norm_mix, ple_norm, final_norm), rep(m_norm_mix, m_ple_norm, m_final_norm),
        rep(v_norm_mix, v_ple_norm, v_final_norm), "adamw_gains")
    mine_cols = lax.dynamic_slice_in_dim(total[9:17], chip * (e // 4), e // 4, axis=1)
    col = lambda a, b_: jnp.concatenate([a.reshape(6, e // 4), b_], axis=0)
    col_delta, col_m, col_v = _adamw(mine_cols, col(a_w_conv, b_scale), col(m_a_w_conv, m_b_scale),
                                     col(v_a_w_conv, v_b_scale), "adamw_cols")

    def unpack(rep_a, col_a):
        return dict(norm_mix=rep_a[0:4], ple_norm=rep_a[4:8], final_norm=rep_a[8],
                    a_w_conv=col_a[0:6].reshape(2, 3, e // 4), b_scale=col_a[6:8])

    small_out = [unpack(total, mine_cols), unpack(rep_delta, col_delta), unpack(rep_m, col_m), unpack(rep_v, col_v)]
    order = ["norm_mix", "a_w_in", "a_w_conv", "a_w_out", "b_w_in", "b_w_grp", "b_scale", "b_w_out", "ple_norm",
             "ple_w_gate", "ple_w_proj", "final_norm"]
    outs = [loss, grad_x]
    for big, small_d in zip([out_grad, out_delta, out_m, out_v], small_out):
        outs += [big[nm] if nm in big else small_d[nm] for nm in order]
    return tuple(outs)
```

```python
import jax
import jax.numpy as jnp
from jax import lax
from jax.experimental import pallas as pl
from jax.experimental.pallas import tpu as pltpu

F32 = jnp.float32
BF16 = jnp.bfloat16
MESH = pl.DeviceIdType.MESH

D_MODEL = 1024
MIX_WIDTH = 1024
PLE_DIM = 256
N_GROUPS = 4
GROUP_DIM = 256
POOL_WINDOWS = (2, 4, 8, 16)
DEPTH = 4
EPS = 1e-6

ADAM_LR = 0.001
ADAM_B1 = 0.9
ADAM_B2 = 0.999
ADAM_EPS = 1e-08
ADAM_WD = 0.01
ADAM_STEP = 10

HALO = 8
TS_MIX = 256
TS_PLE = 512
TR_EW = 512
VMEM_LIMIT = 56 * 1024 * 1024
SMALL_ROWS = 24
JOB_BLOCK_BYTES = 1024 * 1024
MIDDLE_STEPS_BEFORE_END = 1

ANY = pl.BlockSpec(memory_space=pl.ANY)


def _sds(shape, dtype):
    return jax.ShapeDtypeStruct(shape, dtype)


def _full(shape):
    nd = len(shape)
    return pl.BlockSpec(shape, lambda *_: (0,) * nd)


def _params(n_axes=1):
    return pltpu.CompilerParams(dimension_semantics=("arbitrary",) * n_axes, vmem_limit_bytes=VMEM_LIMIT)


def _dot(a, b):
    return jnp.dot(a, b, preferred_element_type=F32)


def _dot_nt(a, b):
    return lax.dot_general(a, b, (((1,), (1,)), ((), ())), preferred_element_type=F32)


def _dot_tn(a, b):
    return lax.dot_general(a, b, (((0,), (0,)), ((), ())), preferred_element_type=F32)


def _sigmoid(z):
    return 1.0 / (1.0 + jnp.exp(-z))


def _shift_down(x, k, tail):
    rolled = pltpu.roll(x, k, 0)
    rt = tail if k % HALO == 0 else pltpu.roll(tail, k % HALO, 0)
    row = lax.broadcasted_iota(jnp.int32, rt.shape, 0)
    head = jnp.where(row < k, rt, rolled[0:HALO])
    return jnp.concatenate([head, rolled[HALO:]], axis=0)


def _shift_up(x, k, head_next):
    n = x.shape[0]
    rolled = pltpu.roll(x, n - k, 0)
    rh = head_next if k % HALO == 0 else pltpu.roll(head_next, HALO - k % HALO, 0)
    row = lax.broadcasted_iota(jnp.int32, rh.shape, 0)
    tail = jnp.where(row >= HALO - k, rh, rolled[n - HALO:n])
    return jnp.concatenate([rolled[:n - HALO], tail], axis=0)


def _inv_counts(tile, ts):
    t = tile * ts + lax.broadcasted_iota(jnp.int32, (ts, 1), 0)
    return [1.0 / jnp.minimum(t + 1, w).astype(F32) for w in POOL_WINDOWS]


def _pool_fwd(u, carry, tile, ts):
    inv = _inv_counts(tile, ts)
    outs = []
    for g, w in enumerate(POOL_WINDOWS):
        cols = slice(g * GROUP_DIM, (g + 1) * GROUP_DIM)
        s = u[:, cols]
        level, k = 0, 1
        while k < w:
            tail = carry[level, :, cols]
            carry[level, :, cols] = s[ts - HALO:ts]
            s = s + _shift_down(s, k, tail)
            level, k = level + 1, k * 2
        outs.append(s * inv[g])
    return jnp.concatenate(outs, axis=1)


def _pool_bwd(dd, carry, tile, ts):
    inv = _inv_counts(tile, ts)
    outs = []
    for g, w in enumerate(POOL_WINDOWS):
        cols = slice(g * GROUP_DIM, (g + 1) * GROUP_DIM)
        q = dd[:, cols] * inv[g]
        level, k = 0, 1
        while k < w:
            head = carry[level, :, cols]
            carry[level, :, cols] = q[0:HALO]
            q = q + _shift_up(q, k, head)
            level, k = level + 1, k * 2
        outs.append(q)
    return jnp.concatenate(outs, axis=1)


def _copy_all(pairs, sems):
    copies = [pltpu.make_async_copy(src, dst, sems.at[n]) for n, (src, dst) in enumerate(pairs)]
    for cp in copies:
        cp.start()
    for cp in copies:
        cp.wait()


def _grp_pairs(wgrp_hbm, wgrp_v):
    rows = GROUP_DIM // 4
    return [(wgrp_hbm.at[k, g], wgrp_v.at[g, pl.ds(k * rows, rows), :]) for k in range(4) for g in range(N_GROUPS)]


def _rms(h):
    r = lax.rsqrt(jnp.mean(h * h, axis=-1, keepdims=True) + EPS)
    return h * r, r


def _rms_bwd(dhn, xn, r, gain):
    dgain = jnp.sum(dhn * xn, axis=0, keepdims=True)
    dxn = dhn * gain
    dh = r * (dxn - xn * jnp.mean(dxn * xn, axis=-1, keepdims=True))
    return dh, dgain


class _Rider:
    def __init__(self, inputs, out_shapes, n_sems, start, finish, middle=None, aliases=None):
        self.inputs, self.out_shapes, self.n_sems = list(inputs), list(out_shapes), n_sems
        self.start, self.middle, self.finish = start, middle, finish
        self.aliases = dict(aliases or {})


def _merge(riders):
    riders = [r for r in riders if r is not None]
    if not riders:
        return None
    if len(riders) == 1:
        return riders[0]

    def phase(which):
        def run(rin, rout, send, recv, base=0):
            i0 = o0 = s0 = 0
            for r in riders:
                fn = getattr(r, which)
                if fn is not None:
                    fn(rin[i0:i0 + len(r.inputs)], rout[o0:o0 + len(r.out_shapes)], send, recv, base + s0)
                i0, o0, s0 = i0 + len(r.inputs), o0 + len(r.out_shapes), s0 + r.n_sems
        return run

    aliases, i0, o0 = {}, 0, 0
    for r in riders:
        aliases.update({i0 + a: o0 + b for a, b in r.aliases.items()})
        i0, o0 = i0 + len(r.inputs), o0 + len(r.out_shapes)
    return _Rider(sum([r.inputs for r in riders], []), sum([r.out_shapes for r in riders], []),
                  sum(r.n_sems for r in riders), phase("start"), phase("finish"),
                  phase("middle") if any(r.middle for r in riders) else None, aliases)


def _split(landed, riders):
    out, o0 = [], 0
    for r in riders:
        if r is None:
            out.append(None)
        else:
            out.append(landed[o0:o0 + len(r.out_shapes)])
            o0 += len(r.out_shapes)
    return out


def _place():
    x, y, c = lax.axis_index("x"), lax.axis_index("y"), lax.axis_index("c")
    chips = [(1 - x, y), (x, 1 - y), (1 - x, 1 - y)]
    return x, y, c, chips


def _remote(src, dst, send_sems, recv_sems, sem, to):
    return pltpu.make_async_remote_copy(src_ref=src, dst_ref=dst, send_sem=send_sems.at[sem], recv_sem=recv_sems.at[sem],
                                        device_id=to, device_id_type=MESH)


def _gather_rider(stacked, slots):
    ni = len(stacked)

    def first_hops(rin, rout, send, recv, base, x, y, c, chips):
        me = 2 * x + y
        return [_remote(rin[t].at[slots[t], c], rout[t].at[me, c], send, recv, base + 7 * t + j, (cx, cy, c))
                for j, (cx, cy) in enumerate(chips) for t in range(ni)]

    def passes(rout, send, recv, base, x, y, c, chips):
        out = []
        for j, (cx, cy) in enumerate(chips):
            for t in range(ni):
                landed = rout[t].at[2 * cx + cy, c]
                out.append((_remote(landed, landed, send, recv, base + 7 * t + j, (x, y, 1 - c)),
                            _remote(landed, landed, send, recv, base + 7 * t + 3 + j, (x, y, 1 - c))))
        return out

    def own(rin, rout, send, recv, base, x, y, c):
        return [_remote(rin[t].at[slots[t]], rout[t].at[2 * x + y], send, recv, base + 7 * t + 6, (x, y, 1 - c))
                for t in range(ni)]

    def start(rin, rout, send, recv, base=0):
        x, y, c, chips = _place()
        for cp in first_hops(rin, rout, send, recv, base, x, y, c, chips) + own(rin, rout, send, recv, base, x, y, c):
            cp.start()

    def middle(rin, rout, send, recv, base=0):
        x, y, c, chips = _place()
        for arrival, onward in passes(rout, send, recv, base, x, y, c, chips):
            arrival.wait_recv()
            onward.start()

    def finish(rin, rout, send, recv, base=0):
        x, y, c, chips = _place()
        for j, (cx, cy) in enumerate(chips):
            for t in range(ni):
                other = rout[t].at[2 * cx + cy, 1 - c]
                _remote(other, other, send, recv, base + 7 * t + 3 + j, (x, y, 1 - c)).wait_recv()
        for cp in own(rin, rout, send, recv, base, x, y, c):
            cp.wait_recv()
            cp.wait_send()
        for cp in first_hops(rin, rout, send, recv, base, x, y, c, chips):
            cp.wait_send()
        for _, onward in passes(rout, send, recv, base, x, y, c, chips):
            onward.wait_send()

    return _Rider(stacked, [_sds((4,) + a.shape[1:], a.dtype) for a in stacked], 7 * ni, start, finish, middle)


def _pair_rider(grads):
    ni = len(grads)

    def copies(rin, rout, send, recv, base):
        x, y, c, _ = _place()
        return [_remote(rin[t].at[:, 1 - c], rout[t], send, recv, base + t, (x, y, 1 - c)) for t in range(ni)]

    def start(rin, rout, send, recv, base=0):
        for cp in copies(rin, rout, send, recv, base):
            cp.start()

    def finish(rin, rout, send, recv, base=0):
        for cp in copies(rin, rout, send, recv, base):
            cp.wait()

    return _Rider(grads, [_sds(g.shape[:1] + g.shape[2:], g.dtype) for g in grads], ni, start, finish)


def _ici_rider(pair_sums):
    ni = len(pair_sums)

    def copies(rin, rout, send, recv, base):
        x, y, c, chips = _place()
        return [_remote(rin[t].at[j], rout[t].at[j], send, recv, base + 3 * t + j, (cx, cy, c))
                for j, (cx, cy) in enumerate(chips) for t in range(ni)]

    def start(rin, rout, send, recv, base=0):
        for cp in copies(rin, rout, send, recv, base):
            cp.start()

    def finish(rin, rout, send, recv, base=0):
        for cp in copies(rin, rout, send, recv, base):
            cp.wait()

    return _Rider(pair_sums, [_sds((3,) + g.shape[1:], g.dtype) for g in pair_sums], 3 * ni, start, finish)


def _final_rider(summed, slots):
    ni = len(summed)

    def copies(rout, send, recv, base):
        x, y, c, _ = _place()
        return [(_remote(rout[t].at[slots[t], c], rout[t].at[slots[t], c], send, recv, base + t, (x, y, 1 - c)),
                 _remote(rout[t].at[slots[t], 1 - c], rout[t].at[slots[t], 1 - c], send, recv, base + t, (x, y, 1 - c)))
                for t in range(ni)]

    def start(rin, rout, send, recv, base=0):
        for mine, _ in copies(rout, send, recv, base):
            mine.start()

    def finish(rin, rout, send, recv, base=0):
        for mine, theirs in copies(rout, send, recv, base):
            mine.wait_send()
            theirs.wait_recv()

    return _Rider(summed, [_sds(a.shape, a.dtype) for a in summed], ni, start, finish,
                  aliases={t: t for t in range(ni)})


def _call(body, *, name, grid, in_specs, out_specs, out_shape, scratch_shapes, operands, rider=None):
    if rider is None:
        outs = pl.pallas_call(body, name=name, grid=grid, in_specs=in_specs, out_specs=out_specs, out_shape=out_shape,
                              scratch_shapes=scratch_shapes, compiler_params=_params(len(grid)))(*operands)
        return list(outs), []
    n_in, n_out, n_scr = len(in_specs), len(out_specs), len(scratch_shapes)
    r_in, r_out = len(rider.inputs), len(rider.out_shapes)
    steps = 1
    for g in grid:
        steps *= g
    mid = max(steps - 1 - MIDDLE_STEPS_BEFORE_END, 0)

    def full_body(*refs):
        own_in, rin = refs[:n_in], refs[n_in:n_in + r_in]
        own_out = refs[n_in + r_in:n_in + r_in + n_out]
        rout = refs[n_in + r_in + n_out:n_in + r_in + n_out + r_out]
        own_scr = refs[n_in + r_in + n_out + r_out:n_in + r_in + n_out + r_out + n_scr]
        send, recv = refs[-2], refs[-1]
        step = pl.program_id(0)
        for axis in range(1, len(grid)):
            step = step * grid[axis] + pl.program_id(axis)

        @pl.when(step == 0)
        def _():
            rider.start(rin, rout, send, recv)

        body(*own_in, *own_out, *own_scr)

        if rider.middle is not None:
            @pl.when(step == mid)
            def _():
                rider.middle(rin, rout, send, recv)

        @pl.when(step == steps - 1)
        def _():
            rider.finish(rin, rout, send, recv)

    outs = pl.pallas_call(
        full_body, name=name, grid=grid,
        in_specs=list(in_specs) + [ANY] * r_in, out_specs=list(out_specs) + [ANY] * r_out,
        out_shape=list(out_shape) + rider.out_shapes,
        scratch_shapes=list(scratch_shapes) + [pltpu.SemaphoreType.DMA((rider.n_sems,)), pltpu.SemaphoreType.DMA((rider.n_sems,))],
        input_output_aliases={n_in + a: n_out + b for a, b in rider.aliases.items()},
        compiler_params=_params(len(grid)),
    )(*operands, *rider.inputs)
    return list(outs[:n_out]), list(outs[n_out:])


def _run_rider(rider, name):
    r_in, r_out = len(rider.inputs), len(rider.out_shapes)

    def body(*refs):
        rin, rout, send, recv = refs[:r_in], refs[r_in:r_in + r_out], refs[-2], refs[-1]
        rider.start(rin, rout, send, recv)
        if rider.middle is not None:
            rider.middle(rin, rout, send, recv)
        rider.finish(rin, rout, send, recv)

    outs = pl.pallas_call(
        body, name=name, in_specs=[ANY] * r_in, out_specs=[ANY] * r_out, out_shape=rider.out_shapes,
        scratch_shapes=[pltpu.SemaphoreType.DMA((rider.n_sems,)), pltpu.SemaphoreType.DMA((rider.n_sems,))],
        input_output_aliases=rider.aliases,
    )(*rider.inputs)
    return list(outs)


def _fwd_mix_a(h, gain, conv_w, w_in, w_out, name, rider=None):
    s, d = h.shape
    e = MIX_WIDTH
    ts = min(TS_MIX, s)
    nt = s // ts

    def body(h_ref, gain_ref, cw_ref, win_hbm, wout_hbm, h1_ref, proj_ref, win_v, wout_v, carry, sems):
        i = pl.program_id(0)

        @pl.when(i == 0)
        def _():
            _copy_all([(win_hbm, win_v), (wout_hbm, wout_v)], sems)
            carry[...] = jnp.zeros_like(carry)

        hh = h_ref[...]
        xn, _ = _rms(hh)
        hnb = (xn * gain_ref[...]).astype(BF16)
        b = _dot(hnb, win_v[0])
        c = _dot(hnb, win_v[1])
        v = _dot(hnb, win_v[2])
        z = _dot(hnb, win_v[3])
        proj_ref[:, 0 * e:1 * e] = b.astype(BF16)
        proj_ref[:, 1 * e:2 * e] = c.astype(BF16)
        proj_ref[:, 2 * e:3 * e] = v.astype(BF16)
        proj_ref[:, 3 * e:4 * e] = z.astype(BF16)
        cv = c * v
        tail = carry[...]
        carry[...] = cv[ts - HALO:ts]
        conv = cw_ref[0:1, :] * _shift_down(cv, 2, tail) + cw_ref[1:2, :] * _shift_down(cv, 1, tail) + cw_ref[2:3, :] * cv
        mb = ((z * _sigmoid(z)) * (b * conv)).astype(BF16)
        h1_ref[...] = hh + _dot(mb, wout_v[...])

    row = lambda width: pl.BlockSpec((ts, width), lambda i: (i, 0))
    return _call(
        body, name=name, grid=(nt,),
        in_specs=[row(d), _full((1, d)), _full((8, e)), ANY, ANY],
        out_specs=[row(d), row(4 * e)],
        out_shape=[_sds((s, d), F32), _sds((s, 4 * e), BF16)],
        scratch_shapes=[pltpu.VMEM((4, d, e), BF16), pltpu.VMEM((e, d), BF16), pltpu.VMEM((HALO, e), F32),
                        pltpu.SemaphoreType.DMA((2,))],
        operands=[h, gain, conv_w, w_in, w_out], rider=rider)


def _fwd_mix_b(h, gain, scale, w_in, w_grp, w_out, name, rider=None):
    s, d = h.shape
    e = MIX_WIDTH
    ts = min(TS_MIX, s)
    nt = s // ts

    def body(h_ref, gain_ref, scale_ref, win_hbm, wgrp_hbm, wout_hbm, h1_ref, z_ref, mx_ref, dd_ref,
             win_v, wgrp_v, wout_v, carry, sems):
        i = pl.program_id(0)

        @pl.when(i == 0)
        def _():
            _copy_all([(win_hbm, win_v), (wout_hbm, wout_v)] + _grp_pairs(wgrp_hbm, wgrp_v), sems)
            carry[...] = jnp.zeros_like(carry)

        hh = h_ref[...]
        xn, _ = _rms(hh)
        hnb = (xn * gain_ref[...]).astype(BF16)
        u = jnp.concatenate([_dot(hnb, win_v[0]), _dot(hnb, win_v[1])], axis=1)
        z = jnp.concatenate([_dot(hnb, win_v[2]), _dot(hnb, win_v[3])], axis=1)
        z_ref[...] = z.astype(BF16)
        diff = (_pool_fwd(u, carry, i, ts) - u).astype(BF16)
        dd_ref[...] = diff
        mx = jnp.concatenate(
            [_dot(diff[:, g * GROUP_DIM:(g + 1) * GROUP_DIM], wgrp_v[g]) for g in range(N_GROUPS)], axis=1)
        mx_ref[...] = mx.astype(BF16)
        mb = ((z * _sigmoid(z)) * (mx * scale_ref[...])).astype(BF16)
        h1_ref[...] = hh + _dot(mb, wout_v[...])

    row = lambda width: pl.BlockSpec((ts, width), lambda i: (i, 0))
    return _call(
        body, name=name, grid=(nt,),
        in_specs=[row(d), _full((1, d)), _full((1, e)), ANY, ANY, ANY],
        out_specs=[row(d), row(e), row(e), row(e)],
        out_shape=[_sds((s, d), F32)] + [_sds((s, e), BF16)] * 3,
        scratch_shapes=[pltpu.VMEM((4, d, e // 2), BF16), pltpu.VMEM((N_GROUPS, GROUP_DIM, GROUP_DIM), BF16),
                        pltpu.VMEM((e, d), BF16), pltpu.VMEM((4, HALO, e), F32), pltpu.SemaphoreType.DMA((18,))],
        operands=[h, gain, scale, w_in, w_grp, w_out], rider=rider)


def _fwd_ple(h1, p, gain, w_gate, w_proj, layer, rider=None):
    s, d = h1.shape
    pd = p.shape[-1]
    ts = min(TS_PLE, s)
    nt = s // ts

    def body(h1_ref, p_ref, gain_ref, wg_hbm, wp_hbm, h2_ref, gate_ref, wg_v, wp_v, sems):
        @pl.when(pl.program_id(0) == 0)
        def _():
            _copy_all([(wg_hbm, wg_v), (wp_hbm, wp_v)], sems)

        hh = h1_ref[...]
        xn, _ = _rms(hh)
        hpb = (xn * gain_ref[...]).astype(BF16)
        gate = _sigmoid(_dot(hpb, wg_v[...]))
        pb = p_ref[...].astype(BF16)
        pe = jnp.concatenate([_dot(pb, wp_v[k]) for k in range(4)], axis=1)
        gate_ref[...] = gate.astype(BF16)
        h2_ref[...] = hh + gate * pe

    row = lambda width: pl.BlockSpec((ts, width), lambda i: (i, 0))
    return _call(
        body, name=f"fwd_ple{layer}", grid=(nt,),
        in_specs=[row(d), pl.BlockSpec((None, ts, pd), lambda i: (layer, i, 0)), _full((1, d)), ANY, ANY],
        out_specs=[row(d), row(d)],
        out_shape=[_sds((s, d), F32), _sds((s, d), BF16)],
        scratch_shapes=[pltpu.VMEM((d, d), BF16), pltpu.VMEM((4, pd, d // 4), BF16), pltpu.SemaphoreType.DMA((2,))],
        operands=[h1, p, gain, w_gate, w_proj], rider=rider)


def _bwd_ple(dh2, h1, gate, p, gain, w_gate, w_proj, layer, rider=None, loss_head=None):
    s, d = dh2.shape
    pd = p.shape[-1]
    ts = min(TS_PLE, s)
    nt = s // ts
    qd = d // 4
    n_head = 0 if loss_head is None else 2

    def body(*refs):
        dh2_ref = refs[0]
        h1_ref, gate_ref, p_ref, gain_ref, wg_hbm, wp_hbm, dh1_ref, dgain_ref, dwg_hbm, dwp_hbm = refs[1 + n_head:11 + n_head]
        wg_v, wp_v, acc_g, acc_p, sems = refs[-5:]
        i = pl.program_id(0)

        @pl.when(i == 0)
        def _():
            _copy_all([(wg_hbm, wg_v), (wp_hbm, wp_v)], sems)
            dgain_ref[...] = jnp.zeros_like(dgain_ref)
            acc_g[...] = jnp.zeros_like(acc_g)
            acc_p[...] = jnp.zeros_like(acc_p)

        if loss_head is None:
            g2 = dh2_ref[...]
        else:
            t_ref, fgain_ref, loss_ref, dfgain_ref = refs[1], refs[2], refs[11 + n_head], refs[12 + n_head]

            @pl.when(i == 0)
            def _():
                loss_ref[...] = jnp.zeros_like(loss_ref)
                dfgain_ref[...] = jnp.zeros_like(dfgain_ref)

            xf, rf = _rms(dh2_ref[...])
            err = xf * fgain_ref[...] - t_ref[...]
            part = 0.5 * jnp.sum(jnp.mean(err * err, axis=-1, keepdims=True), axis=0, keepdims=True)
            loss_ref[...] += jnp.broadcast_to(part, loss_ref.shape)
            g2, dfgain = _rms_bwd(err * (1.0 / d), xf, rf, fgain_ref[...])
            dfgain_ref[...] += dfgain
        gate_f = gate_ref[...].astype(F32)
        xn, r = _rms(h1_ref[...])
        hpb = (xn * gain_ref[...]).astype(BF16)
        pb = p_ref[...].astype(BF16)
        pe = jnp.concatenate([_dot(pb, wp_v[k]) for k in range(4)], axis=1)
        dpeb = (g2 * gate_f).astype(BF16)
        dab = ((g2 * pe) * (gate_f * (1.0 - gate_f))).astype(BF16)
        acc_g[...] += _dot_tn(hpb, dab)
        for k in range(4):
            acc_p[k] += _dot_tn(pb, dpeb[:, k * qd:(k + 1) * qd])
        dhp = _dot_nt(dab, wg_v[...])
        dh, dgain = _rms_bwd(dhp, xn, r, gain_ref[...])
        dh1_ref[...] = g2 + dh
        dgain_ref[...] += dgain

        @pl.when(i == nt - 1)
        def _():
            _copy_all([(acc_g, dwg_hbm), (acc_p, dwp_hbm)], sems)

    row = pl.BlockSpec((ts, d), lambda i: (i, 0))
    head = loss_head is not None
    return _call(
        body, name=f"bwd_ple{layer}", grid=(nt,),
        in_specs=[row] + ([row, _full((1, d))] if head else [])
        + [row, row, pl.BlockSpec((None, ts, pd), lambda i: (layer, i, 0)), _full((1, d)), ANY, ANY],
        out_specs=[row, _full((1, d)), ANY, ANY] + ([_full((8, 128)), _full((1, d))] if head else []),
        out_shape=[_sds((s, d), F32), _sds((1, d), F32), _sds((d, d), F32), _sds((4, pd, qd), F32)]
        + ([_sds((8, 128), F32), _sds((1, d), F32)] if head else []),
        scratch_shapes=[pltpu.VMEM((d, d), BF16), pltpu.VMEM((4, pd, qd), BF16), pltpu.VMEM((d, d), F32),
                        pltpu.VMEM((4, pd, qd), F32), pltpu.SemaphoreType.DMA((2,))],
        operands=[dh2] + (list(loss_head) if head else []) + [h1, gate, p, gain, w_gate, w_proj], rider=rider)


def _mix_a_tile_grads(proj_ref, ch_ref, vh_ref, cw_ref, dh1b, wout_v, carry, dcw_ref, tile, hb):
    e = MIX_WIDTH
    b = proj_ref[:, 0 * e:1 * e].astype(F32)
    c = proj_ref[:, 1 * e:2 * e].astype(F32)
    v = proj_ref[:, 2 * e:3 * e].astype(F32)
    z = proj_ref[:, 3 * e:4 * e].astype(F32)
    cv = c * v
    prev = (ch_ref[...].astype(F32) * vh_ref[...].astype(F32))[hb - HALO:hb]
    tail = jnp.where(tile > 0, prev, jnp.zeros_like(prev))
    cv1 = _shift_down(cv, 1, tail)
    cv2 = _shift_down(cv, 2, tail)
    conv = cw_ref[0:1, :] * cv2 + cw_ref[1:2, :] * cv1 + cw_ref[2:3, :] * cv
    sig = _sigmoid(z)
    sz = z * sig
    y = b * conv
    dm = _dot_nt(dh1b, wout_v[...])
    dz = (dm * y) * (sig * (1.0 + z * (1.0 - sig)))
    dy = dm * sz
    db = dy * conv
    dconv = dy * b
    head = carry[...]
    carry[...] = dconv[0:HALO]
    dcv = cw_ref[2:3, :] * dconv + cw_ref[1:2, :] * _shift_up(dconv, 1, head) + cw_ref[0:1, :] * _shift_up(dconv, 2, head)
    dcw_ref[0:1, :] += jnp.sum(dconv * cv2, axis=0, keepdims=True)
    dcw_ref[1:2, :] += jnp.sum(dconv * cv1, axis=0, keepdims=True)
    dcw_ref[2:3, :] += jnp.sum(dconv * cv, axis=0, keepdims=True)
    parts = [db.astype(BF16), (dcv * v).astype(BF16), (dcv * c).astype(BF16), dz.astype(BF16)]
    return parts, (sz * y).astype(BF16)


def _bwd_mix_a(dh1, h, proj, gain, conv_w, w_in, w_out, name, rider=None):
    s, d = dh1.shape
    e = MIX_WIDTH
    ts = min(TS_MIX, s)
    nt = s // ts
    hb = 16
    per = ts // hb

    def body(dh1_ref, h_ref, proj_ref, ch_ref, vh_ref, gain_ref, cw_ref, win_hbm, wout_hbm,
             dh_ref, dcw_ref, dgain_ref, dwin_hbm, dwout_hbm, win_v, wout_v, acc_in, acc_out, carry, sems):
        i = pl.program_id(0)

        @pl.when(i == 0)
        def _():
            _copy_all([(win_hbm, win_v), (wout_hbm, wout_v)], sems)
            carry[...] = jnp.zeros_like(carry)
            dcw_ref[...] = jnp.zeros_like(dcw_ref)
            dgain_ref[...] = jnp.zeros_like(dgain_ref)
            acc_in[...] = jnp.zeros_like(acc_in)
            acc_out[...] = jnp.zeros_like(acc_out)

        dh1 = dh1_ref[...]
        dh1b = dh1.astype(BF16)
        parts, mb = _mix_a_tile_grads(proj_ref, ch_ref, vh_ref, cw_ref, dh1b, wout_v, carry, dcw_ref, nt - 1 - i, hb)
        acc_out[...] += _dot_tn(mb, dh1b)
        xn, r = _rms(h_ref[...])
        hnb = (xn * gain_ref[...]).astype(BF16)
        for q in range(4):
            acc_in[q] += _dot_tn(hnb, parts[q])
        dhn = _dot_nt(parts[0], win_v[0]) + _dot_nt(parts[1], win_v[1]) + _dot_nt(parts[2], win_v[2]) + _dot_nt(parts[3], win_v[3])
        dh, dgain = _rms_bwd(dhn, xn, r, gain_ref[...])
        dh_ref[...] = dh1 + dh
        dgain_ref[...] += dgain

        @pl.when(i == nt - 1)
        def _():
            _copy_all([(acc_in, dwin_hbm), (acc_out, dwout_hbm)], sems)

    row = lambda width: pl.BlockSpec((ts, width), lambda i: (nt - 1 - i, 0))
    halo = lambda col: pl.BlockSpec((hb, e), lambda i: (jnp.maximum((nt - 1 - i) * per - 1, 0), col))
    return _call(
        body, name=name, grid=(nt,),
        in_specs=[row(d), row(d), row(4 * e), halo(1), halo(2), _full((1, d)), _full((8, e)), ANY, ANY],
        out_specs=[row(d), _full((8, e)), _full((1, d)), ANY, ANY],
        out_shape=[_sds((s, d), F32), _sds((8, e), F32), _sds((1, d), F32), _sds((4, d, e), F32), _sds((e, d), F32)],
        scratch_shapes=[pltpu.VMEM((4, d, e), BF16), pltpu.VMEM((e, d), BF16), pltpu.VMEM((4, d, e), F32),
                        pltpu.VMEM((e, d), F32), pltpu.VMEM((HALO, e), F32), pltpu.SemaphoreType.DMA((2,))],
        operands=[dh1, h, proj, proj, proj, gain, conv_w, w_in, w_out], rider=rider)


def _bwd_mix_a_weights(dh1, h, proj, gain, conv_w, w_out, name, rider=None):
    s, d = dh1.shape
    e = MIX_WIDTH
    ts = min(TS_MIX, s)
    nt = s // ts
    hb = 16
    per = ts // hb

    def body(dh1_ref, h_ref, proj_ref, ch_ref, vh_ref, gain_ref, cw_ref, wout_hbm,
             dproj_ref, dcw_ref, dwin_hbm, dwout_hbm, wout_v, acc_in, acc_out, carry, sems):
        i = pl.program_id(0)

        @pl.when(i == 0)
        def _():
            _copy_all([(wout_hbm, wout_v)], sems)
            carry[...] = jnp.zeros_like(carry)
            dcw_ref[...] = jnp.zeros_like(dcw_ref)
            acc_in[...] = jnp.zeros_like(acc_in)
            acc_out[...] = jnp.zeros_like(acc_out)

        dh1b = dh1_ref[...].astype(BF16)
        parts, mb = _mix_a_tile_grads(proj_ref, ch_ref, vh_ref, cw_ref, dh1b, wout_v, carry, dcw_ref, nt - 1 - i, hb)
        acc_out[...] += _dot_tn(mb, dh1b)
        xn, _ = _rms(h_ref[...])
        hnb = (xn * gain_ref[...]).astype(BF16)
        for q in range(4):
            acc_in[q] += _dot_tn(hnb, parts[q])
            dproj_ref[:, q * e:(q + 1) * e] = parts[q]

        @pl.when(i == nt - 1)
        def _():
            _copy_all([(acc_in, dwin_hbm), (acc_out, dwout_hbm)], sems)

    row = lambda width: pl.BlockSpec((ts, width), lambda i: (nt - 1 - i, 0))
    halo = lambda col: pl.BlockSpec((hb, e), lambda i: (jnp.maximum((nt - 1 - i) * per - 1, 0), col))
    return _call(
        body, name=name, grid=(nt,),
        in_specs=[row(d), row(d), row(4 * e), halo(1), halo(2), _full((1, d)), _full((8, e)), ANY],
        out_specs=[row(4 * e), _full((8, e)), ANY, ANY],
        out_shape=[_sds((s, 4 * e), BF16), _sds((8, e), F32), _sds((4, d, e), F32), _sds((e, d), F32)],
        scratch_shapes=[pltpu.VMEM((e, d), BF16), pltpu.VMEM((4, d, e), F32), pltpu.VMEM((e, d), F32),
                        pltpu.VMEM((HALO, e), F32), pltpu.SemaphoreType.DMA((2,))],
        operands=[dh1, h, proj, proj, proj, gain, conv_w, w_out], rider=rider)


def _bwd_mix_a_input(dproj, h, dh1, gain, w_in, name, rider=None):
    s, d = dh1.shape
    e = MIX_WIDTH
    ts = min(TS_PLE, s)
    nt = s // ts

    def body(dproj_ref, h_ref, dh1_ref, gain_ref, win_hbm, dh_ref, dgain_ref, win_v, sems):
        @pl.when(pl.program_id(0) == 0)
        def _():
            _copy_all([(win_hbm, win_v)], sems)
            dgain_ref[...] = jnp.zeros_like(dgain_ref)

        dhn = _dot_nt(dproj_ref[:, 0:e], win_v[0])
        for q in range(1, 4):
            dhn = dhn + _dot_nt(dproj_ref[:, q * e:(q + 1) * e], win_v[q])
        xn, r = _rms(h_ref[...])
        dh, dgain = _rms_bwd(dhn, xn, r, gain_ref[...])
        dh_ref[...] = dh1_ref[...] + dh
        dgain_ref[...] += dgain

    row = lambda width: pl.BlockSpec((ts, width), lambda i: (i, 0))
    return _call(
        body, name=name, grid=(nt,),
        in_specs=[row(4 * e), row(d), row(d), _full((1, d)), ANY],
        out_specs=[row(d), _full((1, d))],
        out_shape=[_sds((s, d), F32), _sds((1, d), F32)],
        scratch_shapes=[pltpu.VMEM((4, d, e), BF16), pltpu.SemaphoreType.DMA((1,))],
        operands=[dproj, h, dh1, gain, w_in], rider=rider)


def _bwd_mix_b(dh1, h, z, mx, diff, gain, scale, w_in, w_grp, w_out, name, rider=None):
    s, d = dh1.shape
    e = MIX_WIDTH
    ts = min(TS_MIX, s)
    nt = s // ts
    half = e // 2

    def body(dh1_ref, h_ref, z_ref, mx_ref, dd_ref, gain_ref, scale_ref, win_hbm, wgrp_hbm, wout_hbm,
             dh_ref, dscale_ref, dgain_ref, dwin_hbm, dwgrp_hbm, dwout_hbm,
             win_v, wgrp_v, wout_v, acc_in, acc_grp, acc_out, carry, sems):
        i = pl.program_id(0)
        tile = nt - 1 - i

        @pl.when(i == 0)
        def _():
            _copy_all([(win_hbm, win_v), (wout_hbm, wout_v)] + _grp_pairs(wgrp_hbm, wgrp_v), sems)
            carry[...] = jnp.zeros_like(carry)
            dscale_ref[...] = jnp.zeros_like(dscale_ref)
            dgain_ref[...] = jnp.zeros_like(dgain_ref)
            acc_in[...] = jnp.zeros_like(acc_in)
            acc_grp[...] = jnp.zeros_like(acc_grp)
            acc_out[...] = jnp.zeros_like(acc_out)

        zf = z_ref[...].astype(F32)
        mxf = mx_ref[...].astype(F32)
        sig = _sigmoid(zf)
        sz = zf * sig
        mixed = mxf * scale_ref[...]
        dh1 = dh1_ref[...]
        dh1b = dh1.astype(BF16)
        acc_out[...] += _dot_tn((sz * mixed).astype(BF16), dh1b)
        dm = _dot_nt(dh1b, wout_v[...])
        dz = (dm * mixed) * (sig * (1.0 + zf * (1.0 - sig)))
        dmixed = dm * sz
        dscale_ref[...] += jnp.sum(dmixed * mxf, axis=0, keepdims=True)
        dmxb = (dmixed * scale_ref[...]).astype(BF16)
        diff = dd_ref[...]
        for g in range(N_GROUPS):
            cols = slice(g * GROUP_DIM, (g + 1) * GROUP_DIM)
            acc_grp[g] += _dot_tn(diff[:, cols], dmxb[:, cols])
        ddiff = jnp.concatenate(
            [_dot_nt(dmxb[:, g * GROUP_DIM:(g + 1) * GROUP_DIM], wgrp_v[g]) for g in range(N_GROUPS)], axis=1)
        dub = (_pool_bwd(ddiff, carry, tile, ts) - ddiff).astype(BF16)
        dzb = dz.astype(BF16)
        parts = [dub[:, 0:half], dub[:, half:e], dzb[:, 0:half], dzb[:, half:e]]
        xn, r = _rms(h_ref[...])
        hnb = (xn * gain_ref[...]).astype(BF16)
        for k in range(4):
            acc_in[k] += _dot_tn(hnb, parts[k])
        dhn = _dot_nt(parts[0], win_v[0]) + _dot_nt(parts[1], win_v[1]) + _dot_nt(parts[2], win_v[2]) + _dot_nt(parts[3], win_v[3])
        dh, dgain = _rms_bwd(dhn, xn, r, gain_ref[...])
        dh_ref[...] = dh1 + dh
        dgain_ref[...] += dgain

        @pl.when(i == nt - 1)
        def _():
            _copy_all([(acc_in, dwin_hbm), (acc_out, dwout_hbm)] + [(v, hb_) for hb_, v in _grp_pairs(dwgrp_hbm, acc_grp)], sems)

    row = lambda width: pl.BlockSpec((ts, width), lambda i: (nt - 1 - i, 0))
    return _call(
        body, name=name, grid=(nt,),
        in_specs=[row(d), row(d), row(e), row(e), row(e), _full((1, d)), _full((1, e)), ANY, ANY, ANY],
        out_specs=[row(d), _full((1, e)), _full((1, d)), ANY, ANY, ANY],
        out_shape=[_sds((s, d), F32), _sds((1, e), F32), _sds((1, d), F32), _sds((4, d, half), F32),
                   _sds((4, N_GROUPS, GROUP_DIM // 4, GROUP_DIM), F32), _sds((e, d), F32)],
        scratch_shapes=[pltpu.VMEM((4, d, half), BF16), pltpu.VMEM((N_GROUPS, GROUP_DIM, GROUP_DIM), BF16),
                        pltpu.VMEM((e, d), BF16), pltpu.VMEM((4, d, half), F32),
                        pltpu.VMEM((N_GROUPS, GROUP_DIM, GROUP_DIM), F32), pltpu.VMEM((e, d), F32),
                        pltpu.VMEM((4, HALO, e), F32), pltpu.SemaphoreType.DMA((18,))],
        operands=[dh1, h, z, mx, diff, gain, scale, w_in, w_grp, w_out], rider=rider)


def _first_gather(rider, small):
    shards = rider.inputs
    ni = len(shards)

    def body(*refs):
        rin, small_src = refs[:ni], refs[ni]
        rout, small_dst = refs[ni + 1:2 * ni + 1], refs[2 * ni + 1]
        send, recv, ssend, srecv = refs[2 * ni + 2:]
        x, y, c, chips = _place()
        me = 2 * x + y
        peers = [(cx, cy, c) for cx, cy in chips] + [(x, y, 1 - c)]
        vec = [_remote(small_src, small_dst.at[me], ssend, srecv, j, to) for j, to in enumerate(peers)]
        for cp in vec:
            cp.start()
        rider.start(rin, rout, send, recv)
        rider.middle(rin, rout, send, recv)
        rider.finish(rin, rout, send, recv)
        for j, (px, py, _) in enumerate(peers):
            _remote(small_src, small_dst.at[2 * px + py], ssend, srecv, j, peers[j]).wait_recv()
        for cp in vec:
            cp.wait_send()

    outs = pl.pallas_call(
        body, name="first_gather", in_specs=[ANY] * (ni + 1), out_specs=[ANY] * (ni + 1),
        out_shape=rider.out_shapes + [_sds((4,) + small.shape, small.dtype)],
        scratch_shapes=[pltpu.SemaphoreType.DMA((rider.n_sems,)), pltpu.SemaphoreType.DMA((rider.n_sems,)),
                        pltpu.SemaphoreType.DMA((4,)), pltpu.SemaphoreType.DMA((4,))],
    )(*shards, small)
    return list(outs[:ni]), outs[ni]


def _vector_rider(pack):
    flips = [(fx, fy, fc) for fx in (0, 1) for fy in (0, 1) for fc in (0, 1)][1:]

    def copies(rin, rout, send, recv, base):
        x, y, c, _ = _place()
        me = 4 * x + 2 * y + c
        peers = [(1 - x if fx else x, 1 - y if fy else y, 1 - c if fc else c) for fx, fy, fc in flips]
        own = pltpu.make_async_copy(rin[0], rout[0].at[me], send.at[base + 7])
        out = [_remote(rin[0], rout[0].at[me], send, recv, base + r, peer) for r, peer in enumerate(peers)]
        back = [_remote(rin[0], rout[0].at[4 * px + 2 * py + pc], send, recv, base + r, (px, py, pc))
                for r, (px, py, pc) in enumerate(peers)]
        return own, out, back

    def start(rin, rout, send, recv, base=0):
        own, out, _ = copies(rin, rout, send, recv, base)
        own.start()
        for cp in out:
            cp.start()

    def finish(rin, rout, send, recv, base=0):
        own, out, back = copies(rin, rout, send, recv, base)
        for cp in back:
            cp.wait_recv()
        for cp in out:
            cp.wait_send()
        own.wait()

    return _Rider([pack], [_sds((8,) + pack.shape, pack.dtype)], 8, start, finish)


def _vector_sum(landed):
    _, rows, d = landed.shape

    def body(l_ref, out_ref):
        total = l_ref[0]
        for dev in range(1, 8):
            total = total + l_ref[dev]
        out_ref[...] = total

    vmem = pl.BlockSpec(memory_space=pltpu.VMEM)
    return pl.pallas_call(body, name="vector_sum", in_specs=[vmem], out_specs=vmem, out_shape=_sds((rows, d), F32))(landed)


def _ew_rows(rows):
    return min(TR_EW, rows)


def _job_rows(rows, cols):
    return min(rows, max(8, JOB_BLOCK_BYTES // (4 * cols)))


def _pair_sum_job(grad, sibling_rows):
    _, _, rh, cols = grad.shape
    tr = _job_rows(rh, cols)
    nr = rh // tr

    def chip_of(j, pos):
        return jnp.bitwise_xor(pos[0], jnp.where(j == 2, 3, 2 - j))

    return dict(
        ins=[(grad, (None, None, tr, cols), lambda l, pos: (chip_of(l // nr, pos), pos[1], l % nr, 0)),
             (sibling_rows, (None, tr, cols), lambda l, pos: (chip_of(l // nr, pos), l % nr, 0))],
        out=((3, rh, cols), BF16, (None, tr, cols), lambda l, pos: (l // nr, l % nr, 0)),
        steps=3 * nr, fn=lambda g, sb: (g + sb).astype(BF16), alias=None)


def _final_sum_job(grad, sibling_rows, landed, stack, slot, n_slots):
    _, _, rh, cols = grad.shape
    tr = _job_rows(rh, cols)

    def fn(g, sb, ld):
        total = g + sb
        for j in range(3):
            total = total + ld[j].astype(F32)
        return total

    return dict(
        ins=[(grad, (None, None, tr, cols), lambda l, pos: (pos[0], pos[1], l, 0)),
             (sibling_rows, (None, tr, cols), lambda l, pos: (pos[0], l, 0)),
             (landed, (3, tr, cols), lambda l, pos: (0, l, 0))],
        out=((n_slots, 2, rh, cols), F32, (None, None, tr, cols), lambda l, pos: (slot, pos[1], l, 0)),
        steps=rh // tr, fn=fn, alias=stack)


def _run_jobs(jobs, place, name):
    starts, total = [], 0
    for jb in jobs:
        starts.append(total)
        total += jb["steps"]

    def clamped(fn, start, steps):
        return lambda s, pos: fn(jnp.clip(s - start, 0, steps - 1), pos)

    in_specs, operands = [], [place]
    for jb, start in zip(jobs, starts):
        for arr, block, fn in jb["ins"]:
            in_specs.append(pl.BlockSpec(block, clamped(fn, start, jb["steps"])))
            operands.append(arr)
    n_ins = len(in_specs)
    aliases = {}
    for t, jb in enumerate(jobs):
        if jb["alias"] is not None:
            in_specs.append(ANY)
            operands.append(jb["alias"])
            aliases[len(operands) - 1] = t
    out_specs = [pl.BlockSpec(jb["out"][2], clamped(jb["out"][3], start, jb["steps"])) for jb, start in zip(jobs, starts)]

    def body(place_ref, *refs):
        in_refs, out_refs = refs[:n_ins], refs[len(in_specs):]
        s = pl.program_id(0)
        first = 0
        for t, (jb, start) in enumerate(zip(jobs, starts)):
            mine = in_refs[first:first + len(jb["ins"])]
            first += len(jb["ins"])

            @pl.when((s >= start) & (s < start + jb["steps"]))
            def _(mine=mine, t=t, jb=jb):
                out_refs[t][...] = jb["fn"](*[r[...] for r in mine])

    grid_spec = pltpu.PrefetchScalarGridSpec(num_scalar_prefetch=1, grid=(total,), in_specs=in_specs, out_specs=out_specs)
    outs = pl.pallas_call(body, name=name, grid_spec=grid_spec,
                          out_shape=[_sds(jb["out"][0], jb["out"][1]) for jb in jobs],
                          input_output_aliases=aliases, compiler_params=_params(1))(*operands)
    return list(outs)


def _adamw(g, w, m, v, name):
    rows, cols = g.shape
    tr = _ew_rows(rows)

    def body(g_ref, w_ref, m_ref, v_ref, delta_ref, nm_ref, nv_ref):
        gg = g_ref[...]
        nm = ADAM_B1 * m_ref[...] + (1.0 - ADAM_B1) * gg
        nv = ADAM_B2 * v_ref[...] + (1.0 - ADAM_B2) * (gg * gg)
        m_hat = nm / (1.0 - ADAM_B1 ** ADAM_STEP)
        v_hat = nv / (1.0 - ADAM_B2 ** ADAM_STEP)
        delta_ref[...] = -ADAM_LR * (m_hat / (jnp.sqrt(v_hat) + ADAM_EPS) + ADAM_WD * w_ref[...])
        nm_ref[...] = nm
        nv_ref[...] = nv

    spec = pl.BlockSpec((tr, cols), lambda i: (i, 0))
    return pl.pallas_call(
        body, name=name, grid=(rows // tr,), in_specs=[spec] * 4, out_specs=[spec] * 3,
        out_shape=[_sds((rows, cols), F32)] * 3, compiler_params=_params(),
    )(g, w, m, v)


BIG = ["a_w_in", "a_w_out", "b_w_in", "b_w_grp", "b_w_out", "ple_w_gate", "ple_w_proj"]

GATHER_PLAN = {
    "first": [("a_w_in", 0), ("a_w_out", 0)],
    "mix0": [("ple_w_gate", 0), ("ple_w_proj", 0), ("b_w_in", 0), ("b_w_grp", 0), ("b_w_out", 0)],
    "ple0": [("ple_w_gate", 1), ("ple_w_proj", 1)],
    "mix1": [("a_w_in", 1)],
    "ple1": [("ple_w_gate", 2), ("ple_w_proj", 2), ("a_w_out", 1)],
    "mix2": [("b_w_in", 1), ("b_w_grp", 1), ("b_w_out", 1)],
    "ple2": [("ple_w_gate", 3), ("ple_w_proj", 3)],
}


def _as_2d(name, a):
    if name == "b_w_grp":
        return a.reshape(a.shape[0], N_GROUPS * (GROUP_DIM // 4), GROUP_DIM)
    return a


def kernel(x, p, norm_mix, a_w_in, a_w_conv, a_w_out, b_w_in, b_w_grp, b_scale, b_w_out, ple_norm, ple_w_gate, ple_w_proj, final_norm, loss_target, m_norm_mix, m_a_w_in, m_a_w_conv, m_a_w_out, m_b_w_in, m_b_w_grp, m_b_scale, m_b_w_out, m_ple_norm, m_ple_w_gate, m_ple_w_proj, m_final_norm, v_norm_mix, v_a_w_in, v_a_w_conv, v_a_w_out, v_b_w_in, v_b_w_grp, v_b_scale, v_b_w_out, v_ple_norm, v_ple_w_gate, v_ple_w_proj, v_final_norm):
    d, e = D_MODEL, MIX_WIDTH
    s = x.shape[1]
    cx, cy, cc = lax.axis_index("x"), lax.axis_index("y"), lax.axis_index("c")
    chip = 2 * cx + cy
    place = jnp.stack([chip, cc]).astype(jnp.int32)

    weights = dict(a_w_in=a_w_in, a_w_out=a_w_out, b_w_in=b_w_in, b_w_grp=b_w_grp, b_w_out=b_w_out,
                   ple_w_gate=ple_w_gate, ple_w_proj=ple_w_proj)
    moms = dict(a_w_in=m_a_w_in, a_w_out=m_a_w_out, b_w_in=m_b_w_in, b_w_grp=m_b_w_grp, b_w_out=m_b_w_out,
                ple_w_gate=m_ple_w_gate, ple_w_proj=m_ple_w_proj)
    vars_ = dict(a_w_in=v_a_w_in, a_w_out=v_a_w_out, b_w_in=v_b_w_in, b_w_grp=v_b_w_grp, b_w_out=v_b_w_out,
                 ple_w_gate=v_ple_w_gate, ple_w_proj=v_ple_w_proj)
    w2d = {nm: _as_2d(nm, weights[nm]) for nm in BIG}
    bf = {nm: w2d[nm].astype(BF16).reshape(w2d[nm].shape[0], 2, w2d[nm].shape[1] // 2, w2d[nm].shape[2]) for nm in BIG}
    gathered = {}

    def gather_rider(host):
        keys = GATHER_PLAN.get(host)
        return _gather_rider([bf[nm] for nm, _ in keys], [j for _, j in keys]) if keys else None

    def keep(host, landed):
        for k, a in zip(GATHER_PLAN.get(host, []), landed):
            gathered[k] = a

    def weight(nm, j):
        a = gathered[(nm, j)]
        shapes = {"a_w_in": (4, d, e), "a_w_out": (e, d), "b_w_in": (4, d, e // 2),
                  "b_w_grp": (4, N_GROUPS, GROUP_DIM // 4, GROUP_DIM), "b_w_out": (e, d), "ple_w_gate": (d, d),
                  "ple_w_proj": (4, PLE_DIM, d // 4)}
        return a.reshape(shapes[nm])

    small = jnp.concatenate([a_w_conv.reshape(6, e // 4), b_scale], axis=0)
    landed, small_full = _first_gather(gather_rider("first"), small)
    keep("first", landed)
    small_full = small_full.transpose(1, 0, 2).reshape(8, e)
    conv_w = [jnp.concatenate([small_full[3 * j:3 * j + 3], jnp.zeros((5, e), F32)], axis=0) for j in range(2)]
    scale_w = [small_full[6 + j:7 + j] for j in range(2)]

    p3 = p.reshape(DEPTH, s, PLE_DIM)
    mix_gain = [norm_mix[i:i + 1] for i in range(DEPTH)]
    ple_gain = [ple_norm[i:i + 1] for i in range(DEPTH)]

    h = x.reshape(s, d)
    saved = []
    for i in range(DEPTH):
        j = i // 2
        rider = gather_rider(f"mix{i}")
        if i % 2 == 0:
            (h1, proj), landed = _fwd_mix_a(h, mix_gain[i], conv_w[j], weight("a_w_in", j), weight("a_w_out", j),
                                            f"fwd_mix_a{j}", rider)
            mix = dict(proj=proj)
        else:
            (h1, zb, mx, diff), landed = _fwd_mix_b(h, mix_gain[i], scale_w[j], weight("b_w_in", j), weight("b_w_grp", j),
                                                    weight("b_w_out", j), f"fwd_mix_b{j}", rider)
            mix = dict(z=zb, mx=mx, diff=diff)
        keep(f"mix{i}", landed)
        (h2, gate), landed = _fwd_ple(h1, p3, ple_gain[i], weight("ple_w_gate", i), weight("ple_w_proj", i), i,
                                      gather_rider(f"ple{i}"))
        keep(f"ple{i}", landed)
        saved.append(dict(h=h, h1=h1, gate=gate, **mix))
        h = h2

    n_slots = {nm: weights[nm].shape[0] for nm in BIG}
    stacks = {nm: None for nm in BIG}

    class Group:
        def __init__(self, keys, grads):
            self.keys, self.stage = keys, 0
            self.g32 = [g.reshape(4, 2, w2d[nm].shape[1] // 2, w2d[nm].shape[2]) for (nm, _), g in zip(keys, grads)]

        def rider(self):
            if self.stage == 0:
                return _pair_rider(self.g32)
            if self.stage == 1:
                return _ici_rider(self.pair_sums)
            return _final_rider([stacks[nm] for nm, _ in self.keys], [j for _, j in self.keys])

        def jobs_after(self, landed):
            if self.stage == 0:
                self.from_sibling = landed
                return [_pair_sum_job(g, sb) for g, sb in zip(self.g32, landed)]
            if self.stage == 1:
                return [_final_sum_job(g, sb, ld, stacks[nm], j, n_slots[nm])
                        for (nm, j), g, sb, ld in zip(self.keys, self.g32, self.from_sibling, landed)]
            return []

        def advance(self, landed, summed):
            if self.stage == 0:
                self.pair_sums = summed
            else:
                for (nm, _), a in zip(self.keys, summed if self.stage == 1 else landed):
                    stacks[nm] = a
            self.stage += 1

    active = []
    batches = [0]

    def riders_now():
        parts = [g.rider() for g in active]
        return parts, _merge(parts)

    def advance_all(parts, landed):
        groups = list(active)
        pieces = _split(landed, parts)
        jobs = [g.jobs_after(l) for g, l in zip(groups, pieces)]
        flat = sum(jobs, [])
        outs = _run_jobs(flat, place, f"reduce_sums{batches[0]}") if flat else []
        batches[0] += 1
        for g, l, jb in zip(groups, pieces, jobs):
            g.advance(l, outs[:len(jb)])
            outs = outs[len(jb):]
            if g.stage == 3:
                active.remove(g)

    d_mix_gain, d_ple_gain = [None] * DEPTH, [None] * DEPTH
    d_conv, d_scale = [None] * 2, [None] * 2
    for i in reversed(range(DEPTH)):
        j = i // 2
        sv = saved[i]
        parts, rider = riders_now()
        if i == DEPTH - 1:
            (dh1, d_ple_gain[i], dwg, dwp, loss_part, d_final), landed = _bwd_ple(
                h, sv["h1"], sv["gate"], p3, ple_gain[i], weight("ple_w_gate", i), weight("ple_w_proj", i), i, rider,
                loss_head=(loss_target.reshape(s, d), final_norm.reshape(1, d)))
        else:
            (dh1, d_ple_gain[i], dwg, dwp), landed = _bwd_ple(
                dh, sv["h1"], sv["gate"], p3, ple_gain[i], weight("ple_w_gate", i), weight("ple_w_proj", i), i, rider)
        advance_all(parts, landed)
        active.append(Group([("ple_w_gate", i), ("ple_w_proj", i)], [dwg, dwp]))
        parts, rider = riders_now()
        if i == 0:
            (dproj, d_conv[0], dwin, dwout), landed = _bwd_mix_a_weights(
                dh1, sv["h"], sv["proj"], mix_gain[0], conv_w[0], weight("a_w_out", 0), "bwd_mix_a0_weights", rider)
            advance_all(parts, landed)
            active.append(Group([("a_w_in", 0), ("a_w_out", 0)], [dwin, dwout]))
            parts, rider = riders_now()
            advance_all(parts, _run_rider(rider, "pair_exchange0"))
            parts, rider = riders_now()
            (dh, d_mix_gain[0]), landed = _bwd_mix_a_input(dproj, sv["h"], dh1, mix_gain[0], weight("a_w_in", 0),
                                                          "bwd_mix_a0_input", rider)
            advance_all(parts, landed)
            continue
        if i % 2 == 0:
            (dh, d_conv[j], d_mix_gain[i], dwin, dwout), landed = _bwd_mix_a(
                dh1, sv["h"], sv["proj"], mix_gain[i], conv_w[j], weight("a_w_in", j), weight("a_w_out", j),
                f"bwd_mix_a{j}", rider)
            new = Group([("a_w_in", j), ("a_w_out", j)], [dwin, dwout])
        else:
            (dh, d_scale[j], d_mix_gain[i], dwin, dwgrp, dwout), landed = _bwd_mix_b(
                dh1, sv["h"], sv["z"], sv["mx"], sv["diff"], mix_gain[i], scale_w[j], weight("b_w_in", j),
                weight("b_w_grp", j), weight("b_w_out", j), f"bwd_mix_b{j}", rider)
            new = Group([("b_w_in", j), ("b_w_grp", j), ("b_w_out", j)], [dwin, dwgrp, dwout])
        advance_all(parts, landed)
        active.append(new)
    grad_x = dh.reshape(1, s, d)

    pack = jnp.concatenate(
        d_mix_gain + d_ple_gain + [d_final] + [d_conv[0][0:3], d_conv[1][0:3]] + d_scale
        + [jnp.tile(loss_part[0:1], (1, d // 128)), jnp.zeros((SMALL_ROWS - 18, d), F32)], axis=0)
    vectors = _vector_rider(pack)
    tail = 0
    while active:
        parts, _ = riders_now()
        extra = [vectors] if tail == 0 else []
        landed = _run_rider(_merge(parts + extra), f"tail_exchange{tail}")
        if extra:
            total = _vector_sum(_split(landed, parts + extra)[-1][0])
        advance_all(parts, landed)
        tail += 1
    loss = total[17, 0]

    out_grad, out_delta, out_m, out_v = {}, {}, {}, {}
    for nm in BIG:
        shape = weights[nm].shape
        flat = (w2d[nm].shape[0] * w2d[nm].shape[1], w2d[nm].shape[2])
        g2 = stacks[nm].reshape(flat)
        delta, new_m, new_v = _adamw(g2, w2d[nm].reshape(flat), _as_2d(nm, moms[nm]).reshape(flat),
                                     _as_2d(nm, vars_[nm]).reshape(flat), f"adamw_{nm}")
        out_grad[nm], out_delta[nm] = g2.reshape(shape), delta.reshape(shape)
        out_m[nm], out_v[nm] = new_m.reshape(shape), new_v.reshape(shape)

    rep_rows = 16
    rep = lambda a, b_, c_: jnp.concatenate([a, b_, c_.reshape(1, d), jnp.zeros((rep_rows - 9, d), F32)], axis=0)
    rep_delta, rep_m, rep_v = _adamw(
        jnp.concatenate([total[0:9], jnp.zeros((rep_rows - 9, d), F32)], axis=0),
        rep(norm_mix, ple_norm, final_norm), rep(m_norm_mix, m_ple_norm, m_final_norm),
        rep(v_norm_mix, v_ple_norm, v_final_norm), "adamw_gains")
    mine_cols = lax.dynamic_slice_in_dim(total[9:17], chip * (e // 4), e // 4, axis=1)
    col = lambda a, b_: jnp.concatenate([a.reshape(6, e // 4), b_], axis=0)
    col_delta, col_m, col_v = _adamw(mine_cols, col(a_w_conv, b_scale), col(m_a_w_conv, m_b_scale),
                                     col(v_a_w_conv, v_b_scale), "adamw_cols")

    def unpack(rep_a, col_a):
        return dict(norm_mix=rep_a[0:4], ple_norm=rep_a[4:8], final_norm=rep_a[8],
                    a_w_conv=col_a[0:6].reshape(2, 3, e // 4), b_scale=col_a[6:8])

    small_out = [unpack(total, mine_cols), unpack(rep_delta, col_delta), unpack(rep_m, col_m), unpack(rep_v, col_v)]
    order = ["norm_mix", "a_w_in", "a_w_conv", "a_w_out", "b_w_in", "b_w_grp", "b_scale", "b_w_out", "ple_norm",
             "ple_w_gate", "ple_w_proj", "final_norm"]
    outs = [loss, grad_x]
    for big, small_d in zip([out_grad, out_delta, out_m, out_v], small_out):
        outs += [big[nm] if nm in big else small_d[nm] for nm in order]
    return tuple(outs)
```

```python
import jax
import jax.numpy as jnp
from jax import lax
from jax.experimental import pallas as pl
from jax.experimental.pallas import tpu as pltpu

F32 = jnp.float32
BF16 = jnp.bfloat16
MESH = pl.DeviceIdType.MESH

D_MODEL = 1024
MIX_WIDTH = 1024
PLE_DIM = 256
N_GROUPS = 4
GROUP_DIM = 256
POOL_WINDOWS = (2, 4, 8, 16)
DEPTH = 4
EPS = 1e-6

ADAM_LR = 0.001
ADAM_B1 = 0.9
ADAM_B2 = 0.999
ADAM_EPS = 1e-08
ADAM_WD = 0.01
ADAM_STEP = 10

HALO = 8
TS_MIX = 256
TS_PLE = 512
TR_EW = 512
VMEM_LIMIT = 56 * 1024 * 1024
SMALL_ROWS = 24
JOB_BLOCK_BYTES = 1024 * 1024
MIDDLE_STEPS_BEFORE_END = 1

ANY = pl.BlockSpec(memory_space=pl.ANY)


def _sds(shape, dtype):
    return jax.ShapeDtypeStruct(shape, dtype)


def _full(shape):
    nd = len(shape)
    return pl.BlockSpec(shape, lambda *_: (0,) * nd)


def _params(n_axes=1):
    return pltpu.CompilerParams(dimension_semantics=("arbitrary",) * n_axes, vmem_limit_bytes=VMEM_LIMIT)


def _dot(a, b):
    return jnp.dot(a, b, preferred_element_type=F32)


def _dot_nt(a, b):
    return lax.dot_general(a, b, (((1,), (1,)), ((), ())), preferred_element_type=F32)


def _dot_tn(a, b):
    return lax.dot_general(a, b, (((0,), (0,)), ((), ())), preferred_element_type=F32)


def _sigmoid(z):
    return 1.0 / (1.0 + jnp.exp(-z))


def _shift_down(x, k, tail):
    rolled = pltpu.roll(x, k, 0)
    rt = tail if k % HALO == 0 else pltpu.roll(tail, k % HALO, 0)
    row = lax.broadcasted_iota(jnp.int32, rt.shape, 0)
    head = jnp.where(row < k, rt, rolled[0:HALO])
    return jnp.concatenate([head, rolled[HALO:]], axis=0)


def _shift_up(x, k, head_next):
    n = x.shape[0]
    rolled = pltpu.roll(x, n - k, 0)
    rh = head_next if k % HALO == 0 else pltpu.roll(head_next, HALO - k % HALO, 0)
    row = lax.broadcasted_iota(jnp.int32, rh.shape, 0)
    tail = jnp.where(row >= HALO - k, rh, rolled[n - HALO:n])
    return jnp.concatenate([rolled[:n - HALO], tail], axis=0)


def _inv_counts(tile, ts):
    t = tile * ts + lax.broadcasted_iota(jnp.int32, (ts, 1), 0)
    return [1.0 / jnp.minimum(t + 1, w).astype(F32) for w in POOL_WINDOWS]


def _pool_fwd(u, carry, tile, ts):
    inv = _inv_counts(tile, ts)
    outs = []
    for g, w in enumerate(POOL_WINDOWS):
        cols = slice(g * GROUP_DIM, (g + 1) * GROUP_DIM)
        s = u[:, cols]
        level, k = 0, 1
        while k < w:
            tail = carry[level, :, cols]
            carry[level, :, cols] = s[ts - HALO:ts]
            s = s + _shift_down(s, k, tail)
            level, k = level + 1, k * 2
        outs.append(s * inv[g])
    return jnp.concatenate(outs, axis=1)


def _pool_bwd(dd, carry, tile, ts):
    inv = _inv_counts(tile, ts)
    outs = []
    for g, w in enumerate(POOL_WINDOWS):
        cols = slice(g * GROUP_DIM, (g + 1) * GROUP_DIM)
        q = dd[:, cols] * inv[g]
        level, k = 0, 1
        while k < w:
            head = carry[level, :, cols]
            carry[level, :, cols] = q[0:HALO]
            q = q + _shift_up(q, k, head)
            level, k = level + 1, k * 2
        outs.append(q)
    return jnp.concatenate(outs, axis=1)


def _copy_all(pairs, sems):
    copies = [pltpu.make_async_copy(src, dst, sems.at[n]) for n, (src, dst) in enumerate(pairs)]
    for cp in copies:
        cp.start()
    for cp in copies:
        cp.wait()


def _grp_pairs(wgrp_hbm, wgrp_v):
    rows = GROUP_DIM // 4
    return [(wgrp_hbm.at[k, g], wgrp_v.at[g, pl.ds(k * rows, rows), :]) for k in range(4) for g in range(N_GROUPS)]


def _rms(h):
    r = lax.rsqrt(jnp.mean(h * h, axis=-1, keepdims=True) + EPS)
    return h * r, r


def _rms_bwd(dhn, xn, r, gain):
    dgain = jnp.sum(dhn * xn, axis=0, keepdims=True)
    dxn = dhn * gain
    dh = r * (dxn - xn * jnp.mean(dxn * xn, axis=-1, keepdims=True))
    return dh, dgain


class _Rider:
    def __init__(self, inputs, out_shapes, n_sems, start, finish, middle=None, aliases=None):
        self.inputs, self.out_shapes, self.n_sems = list(inputs), list(out_shapes), n_sems
        self.start, self.middle, self.finish = start, middle, finish
        self.aliases = dict(aliases or {})


def _merge(riders):
    riders = [r for r in riders if r is not None]
    if not riders:
        return None
    if len(riders) == 1:
        return riders[0]

    def phase(which):
        def run(rin, rout, send, recv, base=0):
            i0 = o0 = s0 = 0
            for r in riders:
                fn = getattr(r, which)
                if fn is not None:
                    fn(rin[i0:i0 + len(r.inputs)], rout[o0:o0 + len(r.out_shapes)], send, recv, base + s0)
                i0, o0, s0 = i0 + len(r.inputs), o0 + len(r.out_shapes), s0 + r.n_sems
        return run

    aliases, i0, o0 = {}, 0, 0
    for r in riders:
        aliases.update({i0 + a: o0 + b for a, b in r.aliases.items()})
        i0, o0 = i0 + len(r.inputs), o0 + len(r.out_shapes)
    return _Rider(sum([r.inputs for r in riders], []), sum([r.out_shapes for r in riders], []),
                  sum(r.n_sems for r in riders), phase("start"), phase("finish"),
                  phase("middle") if any(r.middle for r in riders) else None, aliases)


def _split(landed, riders):
    out, o0 = [], 0
    for r in riders:
        if r is None:
            out.append(None)
        else:
            out.append(landed[o0:o0 + len(r.out_shapes)])
            o0 += len(r.out_shapes)
    return out


def _place():
    x, y, c = lax.axis_index("x"), lax.axis_index("y"), lax.axis_index("c")
    chips = [(1 - x, y), (x, 1 - y), (1 - x, 1 - y)]
    return x, y, c, chips


def _remote(src, dst, send_sems, recv_sems, sem, to):
    return pltpu.make_async_remote_copy(src_ref=src, dst_ref=dst, send_sem=send_sems.at[sem], recv_sem=recv_sems.at[sem],
                                        device_id=to, device_id_type=MESH)


def _gather_rider(stacked, slots):
    ni = len(stacked)

    def first_hops(rin, rout, send, recv, base, x, y, c, chips):
        me = 2 * x + y
        return [_remote(rin[t].at[slots[t], c], rout[t].at[me, c], send, recv, base + 7 * t + j, (cx, cy, c))
                for j, (cx, cy) in enumerate(chips) for t in range(ni)]

    def passes(rout, send, recv, base, x, y, c, chips):
        out = []
        for j, (cx, cy) in enumerate(chips):
            for t in range(ni):
                landed = rout[t].at[2 * cx + cy, c]
                out.append((_remote(landed, landed, send, recv, base + 7 * t + j, (x, y, 1 - c)),
                            _remote(landed, landed, send, recv, base + 7 * t + 3 + j, (x, y, 1 - c))))
        return out

    def own(rin, rout, send, recv, base, x, y, c):
        return [_remote(rin[t].at[slots[t]], rout[t].at[2 * x + y], send, recv, base + 7 * t + 6, (x, y, 1 - c))
                for t in range(ni)]

    def start(rin, rout, send, recv, base=0):
        x, y, c, chips = _place()
        for cp in first_hops(rin, rout, send, recv, base, x, y, c, chips) + own(rin, rout, send, recv, base, x, y, c):
            cp.start()

    def middle(rin, rout, send, recv, base=0):
        x, y, c, chips = _place()
        for arrival, onward in passes(rout, send, recv, base, x, y, c, chips):
            arrival.wait_recv()
            onward.start()

    def finish(rin, rout, send, recv, base=0):
        x, y, c, chips = _place()
        for j, (cx, cy) in enumerate(chips):
            for t in range(ni):
                other = rout[t].at[2 * cx + cy, 1 - c]
                _remote(other, other, send, recv, base + 7 * t + 3 + j, (x, y, 1 - c)).wait_recv()
        for cp in own(rin, rout, send, recv, base, x, y, c):
            cp.wait_recv()
            cp.wait_send()
        for cp in first_hops(rin, rout, send, recv, base, x, y, c, chips):
            cp.wait_send()
        for _, onward in passes(rout, send, recv, base, x, y, c, chips):
            onward.wait_send()

    return _Rider(stacked, [_sds((4,) + a.shape[1:], a.dtype) for a in stacked], 7 * ni, start, finish, middle)


def _pair_rider(grads):
    ni = len(grads)

    def copies(rin, rout, send, recv, base):
        x, y, c, _ = _place()
        return [_remote(rin[t].at[:, 1 - c], rout[t], send, recv, base + t, (x, y, 1 - c)) for t in range(ni)]

    def start(rin, rout, send, recv, base=0):
        for cp in copies(rin, rout, send, recv, base):
            cp.start()

    def finish(rin, rout, send, recv, base=0):
        for cp in copies(rin, rout, send, recv, base):
            cp.wait()

    return _Rider(grads, [_sds(g.shape[:1] + g.shape[2:], g.dtype) for g in grads], ni, start, finish)


def _ici_rider(pair_sums):
    ni = len(pair_sums)

    def copies(rin, rout, send, recv, base):
        x, y, c, chips = _place()
        return [_remote(rin[t].at[j], rout[t].at[j], send, recv, base + 3 * t + j, (cx, cy, c))
                for j, (cx, cy) in enumerate(chips) for t in range(ni)]

    def start(rin, rout, send, recv, base=0):
        for cp in copies(rin, rout, send, recv, base):
            cp.start()

    def finish(rin, rout, send, recv, base=0):
        for cp in copies(rin, rout, send, recv, base):
            cp.wait()

    return _Rider(pair_sums, [_sds((3,) + g.shape[1:], g.dtype) for g in pair_sums], 3 * ni, start, finish)


def _final_rider(summed, slots):
    ni = len(summed)

    def copies(rout, send, recv, base):
        x, y, c, _ = _place()
        return [(_remote(rout[t].at[slots[t], c], rout[t].at[slots[t], c], send, recv, base + t, (x, y, 1 - c)),
                 _remote(rout[t].at[slots[t], 1 - c], rout[t].at[slots[t], 1 - c], send, recv, base + t, (x, y, 1 - c)))
                for t in range(ni)]

    def start(rin, rout, send, recv, base=0):
        for mine, _ in copies(rout, send, recv, base):
            mine.start()

    def finish(rin, rout, send, recv, base=0):
        for mine, theirs in copies(rout, send, recv, base):
            mine.wait_send()
            theirs.wait_recv()

    return _Rider(summed, [_sds(a.shape, a.dtype) for a in summed], ni, start, finish,
                  aliases={t: t for t in range(ni)})


def _call(body, *, name, grid, in_specs, out_specs, out_shape, scratch_shapes, operands, rider=None):
    if rider is None:
        outs = pl.pallas_call(body, name=name, grid=grid, in_specs=in_specs, out_specs=out_specs, out_shape=out_shape,
                              scratch_shapes=scratch_shapes, compiler_params=_params(len(grid)))(*operands)
        return list(outs), []
    n_in, n_out, n_scr = len(in_specs), len(out_specs), len(scratch_shapes)
    r_in, r_out = len(rider.inputs), len(rider.out_shapes)
    steps = 1
    for g in grid:
        steps *= g
    mid = max(steps - 1 - MIDDLE_STEPS_BEFORE_END, 0)

    def full_body(*refs):
        own_in, rin = refs[:n_in], refs[n_in:n_in + r_in]
        own_out = refs[n_in + r_in:n_in + r_in + n_out]
        rout = refs[n_in + r_in + n_out:n_in + r_in + n_out + r_out]
        own_scr = refs[n_in + r_in + n_out + r_out:n_in + r_in + n_out + r_out + n_scr]
        send, recv = refs[-2], refs[-1]
        step = pl.program_id(0)
        for axis in range(1, len(grid)):
            step = step * grid[axis] + pl.program_id(axis)

        @pl.when(step == 0)
        def _():
            rider.start(rin, rout, send, recv)

        body(*own_in, *own_out, *own_scr)

        if rider.middle is not None:
            @pl.when(step == mid)
            def _():
                rider.middle(rin, rout, send, recv)

        @pl.when(step == steps - 1)
        def _():
            rider.finish(rin, rout, send, recv)

    outs = pl.pallas_call(
        full_body, name=name, grid=grid,
        in_specs=list(in_specs) + [ANY] * r_in, out_specs=list(out_specs) + [ANY] * r_out,
        out_shape=list(out_shape) + rider.out_shapes,
        scratch_shapes=list(scratch_shapes) + [pltpu.SemaphoreType.DMA((rider.n_sems,)), pltpu.SemaphoreType.DMA((rider.n_sems,))],
        input_output_aliases={n_in + a: n_out + b for a, b in rider.aliases.items()},
        compiler_params=_params(len(grid)),
    )(*operands, *rider.inputs)
    return list(outs[:n_out]), list(outs[n_out:])


def _run_rider(rider, name):
    r_in, r_out = len(rider.inputs), len(rider.out_shapes)

    def body(*refs):
        rin, rout, send, recv = refs[:r_in], refs[r_in:r_in + r_out], refs[-2], refs[-1]
        rider.start(rin, rout, send, recv)
        if rider.middle is not None:
            rider.middle(rin, rout, send, recv)
        rider.finish(rin, rout, send, recv)

    outs = pl.pallas_call(
        body, name=name, in_specs=[ANY] * r_in, out_specs=[ANY] * r_out, out_shape=rider.out_shapes,
        scratch_shapes=[pltpu.SemaphoreType.DMA((rider.n_sems,)), pltpu.SemaphoreType.DMA((rider.n_sems,))],
        input_output_aliases=rider.aliases,
    )(*rider.inputs)
    return list(outs)


def _ple_tile(h1, p_ref, gain_ref, wg_v, wp_v):
    xn, _ = _rms(h1)
    hpb = (xn * gain_ref[...]).astype(BF16)
    gate = _sigmoid(_dot(hpb, wg_v[...]))
    pb = p_ref[...].astype(BF16)
    pe = jnp.concatenate([_dot(pb, wp_v[k]) for k in range(4)], axis=1)
    return h1 + gate * pe, gate


def _ple_parts(ple, ts, d):
    p, layer, gain, w_gate, w_proj = ple
    s, pd = p.shape[1:]
    row = pl.BlockSpec((ts, d), lambda i: (i, 0))
    return dict(
        operands=[p, gain, w_gate, w_proj],
        in_specs=[pl.BlockSpec((None, ts, pd), lambda i: (layer, i, 0)), _full((1, d)), ANY, ANY],
        out_specs=[row, row], out_shape=[_sds((s, d), F32), _sds((s, d), BF16)],
        scratch=[pltpu.VMEM((d, d), BF16), pltpu.VMEM((4, pd, d // 4), BF16)])


def _fwd_mix_a(h, gain, conv_w, w_in, w_out, name, rider=None, ple=None):
    s, d = h.shape
    e = MIX_WIDTH
    ts = min(TS_MIX, s)
    nt = s // ts
    extra = _ple_parts(ple, ts, d) if ple else None

    def body(*refs):
        h_ref, gain_ref, cw_ref, win_hbm, wout_hbm = refs[:5]
        n_in = 9 if ple else 5
        h1_ref, proj_ref = refs[n_in:n_in + 2]
        win_v, wout_v, carry, sems = refs[n_in + (4 if ple else 2):][:4]
        i = pl.program_id(0)

        @pl.when(i == 0)
        def _():
            loads = [(win_hbm, win_v), (wout_hbm, wout_v)]
            if ple:
                loads += [(refs[7], refs[-2]), (refs[8], refs[-1])]
            _copy_all(loads, sems)
            carry[...] = jnp.zeros_like(carry)

        hh = h_ref[...]
        xn, _ = _rms(hh)
        hnb = (xn * gain_ref[...]).astype(BF16)
        b = _dot(hnb, win_v[0])
        c = _dot(hnb, win_v[1])
        v = _dot(hnb, win_v[2])
        z = _dot(hnb, win_v[3])
        proj_ref[:, 0 * e:1 * e] = b.astype(BF16)
        proj_ref[:, 1 * e:2 * e] = c.astype(BF16)
        proj_ref[:, 2 * e:3 * e] = v.astype(BF16)
        proj_ref[:, 3 * e:4 * e] = z.astype(BF16)
        cv = c * v
        tail = carry[...]
        carry[...] = cv[ts - HALO:ts]
        conv = cw_ref[0:1, :] * _shift_down(cv, 2, tail) + cw_ref[1:2, :] * _shift_down(cv, 1, tail) + cw_ref[2:3, :] * cv
        mb = ((z * _sigmoid(z)) * (b * conv)).astype(BF16)
        h1 = hh + _dot(mb, wout_v[...])
        h1_ref[...] = h1
        if ple:
            h2, gate = _ple_tile(h1, refs[5], refs[6], refs[-2], refs[-1])
            refs[n_in + 2][...] = h2
            refs[n_in + 3][...] = gate.astype(BF16)

    row = lambda width: pl.BlockSpec((ts, width), lambda i: (i, 0))
    return _call(
        body, name=name, grid=(nt,),
        in_specs=[row(d), _full((1, d)), _full((8, e)), ANY, ANY] + (extra["in_specs"] if ple else []),
        out_specs=[row(d), row(4 * e)] + (extra["out_specs"] if ple else []),
        out_shape=[_sds((s, d), F32), _sds((s, 4 * e), BF16)] + (extra["out_shape"] if ple else []),
        scratch_shapes=[pltpu.VMEM((4, d, e), BF16), pltpu.VMEM((e, d), BF16), pltpu.VMEM((HALO, e), F32),
                        pltpu.SemaphoreType.DMA((4,))] + (extra["scratch"] if ple else []),
        operands=[h, gain, conv_w, w_in, w_out] + (extra["operands"] if ple else []), rider=rider)


def _fwd_mix_b(h, gain, scale, w_in, w_grp, w_out, name, rider=None, ple=None):
    s, d = h.shape
    e = MIX_WIDTH
    ts = min(TS_MIX, s)
    nt = s // ts
    extra = _ple_parts(ple, ts, d) if ple else None

    def body(*refs):
        h_ref, gain_ref, scale_ref, win_hbm, wgrp_hbm, wout_hbm = refs[:6]
        n_in = 10 if ple else 6
        h1_ref, z_ref, mx_ref, dd_ref = refs[n_in:n_in + 4]
        win_v, wgrp_v, wout_v, carry, sems = refs[n_in + (6 if ple else 4):][:5]
        i = pl.program_id(0)

        @pl.when(i == 0)
        def _():
            loads = [(win_hbm, win_v), (wout_hbm, wout_v)] + _grp_pairs(wgrp_hbm, wgrp_v)
            if ple:
                loads += [(refs[8], refs[-2]), (refs[9], refs[-1])]
            _copy_all(loads, sems)
            carry[...] = jnp.zeros_like(carry)

        hh = h_ref[...]
        xn, _ = _rms(hh)
        hnb = (xn * gain_ref[...]).astype(BF16)
        u = jnp.concatenate([_dot(hnb, win_v[0]), _dot(hnb, win_v[1])], axis=1)
        z = jnp.concatenate([_dot(hnb, win_v[2]), _dot(hnb, win_v[3])], axis=1)
        z_ref[...] = z.astype(BF16)
        diff = (_pool_fwd(u, carry, i, ts) - u).astype(BF16)
        dd_ref[...] = diff
        mx = jnp.concatenate(
            [_dot(diff[:, g * GROUP_DIM:(g + 1) * GROUP_DIM], wgrp_v[g]) for g in range(N_GROUPS)], axis=1)
        mx_ref[...] = mx.astype(BF16)
        mb = ((z * _sigmoid(z)) * (mx * scale_ref[...])).astype(BF16)
        h1 = hh + _dot(mb, wout_v[...])
        h1_ref[...] = h1
        if ple:
            h2, gate = _ple_tile(h1, refs[6], refs[7], refs[-2], refs[-1])
            refs[n_in + 4][...] = h2
            refs[n_in + 5][...] = gate.astype(BF16)

    row = lambda width: pl.BlockSpec((ts, width), lambda i: (i, 0))
    return _call(
        body, name=name, grid=(nt,),
        in_specs=[row(d), _full((1, d)), _full((1, e)), ANY, ANY, ANY] + (extra["in_specs"] if ple else []),
        out_specs=[row(d), row(e), row(e), row(e)] + (extra["out_specs"] if ple else []),
        out_shape=[_sds((s, d), F32)] + [_sds((s, e), BF16)] * 3 + (extra["out_shape"] if ple else []),
        scratch_shapes=[pltpu.VMEM((4, d, e // 2), BF16), pltpu.VMEM((N_GROUPS, GROUP_DIM, GROUP_DIM), BF16),
                        pltpu.VMEM((e, d), BF16), pltpu.VMEM((4, HALO, e), F32), pltpu.SemaphoreType.DMA((20,))]
        + (extra["scratch"] if ple else []),
        operands=[h, gain, scale, w_in, w_grp, w_out] + (extra["operands"] if ple else []), rider=rider)


def _fwd_ple(h1, p, gain, w_gate, w_proj, layer, rider=None):
    s, d = h1.shape
    pd = p.shape[-1]
    ts = min(TS_PLE, s)
    nt = s // ts

    def body(h1_ref, p_ref, gain_ref, wg_hbm, wp_hbm, h2_ref, gate_ref, wg_v, wp_v, sems):
        @pl.when(pl.program_id(0) == 0)
        def _():
            _copy_all([(wg_hbm, wg_v), (wp_hbm, wp_v)], sems)

        hh = h1_ref[...]
        xn, _ = _rms(hh)
        hpb = (xn * gain_ref[...]).astype(BF16)
        gate = _sigmoid(_dot(hpb, wg_v[...]))
        pb = p_ref[...].astype(BF16)
        pe = jnp.concatenate([_dot(pb, wp_v[k]) for k in range(4)], axis=1)
        gate_ref[...] = gate.astype(BF16)
        h2_ref[...] = hh + gate * pe

    row = lambda width: pl.BlockSpec((ts, width), lambda i: (i, 0))
    return _call(
        body, name=f"fwd_ple{layer}", grid=(nt,),
        in_specs=[row(d), pl.BlockSpec((None, ts, pd), lambda i: (layer, i, 0)), _full((1, d)), ANY, ANY],
        out_specs=[row(d), row(d)],
        out_shape=[_sds((s, d), F32), _sds((s, d), BF16)],
        scratch_shapes=[pltpu.VMEM((d, d), BF16), pltpu.VMEM((4, pd, d // 4), BF16), pltpu.SemaphoreType.DMA((2,))],
        operands=[h1, p, gain, w_gate, w_proj], rider=rider)


def _bwd_ple(dh2, h1, gate, p, gain, w_gate, w_proj, layer, rider=None, loss_head=None):
    s, d = dh2.shape
    pd = p.shape[-1]
    ts = min(TS_PLE, s)
    nt = s // ts
    qd = d // 4
    n_head = 0 if loss_head is None else 2

    def body(*refs):
        dh2_ref = refs[0]
        h1_ref, gate_ref, p_ref, gain_ref, wg_hbm, wp_hbm, dh1_ref, dgain_ref, dwg_hbm, dwp_hbm = refs[1 + n_head:11 + n_head]
        wg_v, wp_v, acc_g, acc_p, sems = refs[-5:]
        i = pl.program_id(0)

        @pl.when(i == 0)
        def _():
            _copy_all([(wg_hbm, wg_v), (wp_hbm, wp_v)], sems)
            dgain_ref[...] = jnp.zeros_like(dgain_ref)
            acc_g[...] = jnp.zeros_like(acc_g)
            acc_p[...] = jnp.zeros_like(acc_p)

        if loss_head is None:
            g2 = dh2_ref[...]
        else:
            t_ref, fgain_ref, loss_ref, dfgain_ref = refs[1], refs[2], refs[11 + n_head], refs[12 + n_head]

            @pl.when(i == 0)
            def _():
                loss_ref[...] = jnp.zeros_like(loss_ref)
                dfgain_ref[...] = jnp.zeros_like(dfgain_ref)

            xf, rf = _rms(dh2_ref[...])
            err = xf * fgain_ref[...] - t_ref[...]
            part = 0.5 * jnp.sum(jnp.mean(err * err, axis=-1, keepdims=True), axis=0, keepdims=True)
            loss_ref[...] += jnp.broadcast_to(part, loss_ref.shape)
            g2, dfgain = _rms_bwd(err * (1.0 / d), xf, rf, fgain_ref[...])
            dfgain_ref[...] += dfgain
        gate_f = gate_ref[...].astype(F32)
        xn, r = _rms(h1_ref[...])
        hpb = (xn * gain_ref[...]).astype(BF16)
        pb = p_ref[...].astype(BF16)
        pe = jnp.concatenate([_dot(pb, wp_v[k]) for k in range(4)], axis=1)
        dpeb = (g2 * gate_f).astype(BF16)
        dab = ((g2 * pe) * (gate_f * (1.0 - gate_f))).astype(BF16)
        acc_g[...] += _dot_tn(hpb, dab)
        for k in range(4):
            acc_p[k] += _dot_tn(pb, dpeb[:, k * qd:(k + 1) * qd])
        dhp = _dot_nt(dab, wg_v[...])
        dh, dgain = _rms_bwd(dhp, xn, r, gain_ref[...])
        dh1_ref[...] = g2 + dh
        dgain_ref[...] += dgain

        @pl.when(i == nt - 1)
        def _():
            _copy_all([(acc_g, dwg_hbm), (acc_p, dwp_hbm)], sems)

    row = pl.BlockSpec((ts, d), lambda i: (i, 0))
    head = loss_head is not None
    return _call(
        body, name=f"bwd_ple{layer}", grid=(nt,),
        in_specs=[row] + ([row, _full((1, d))] if head else [])
        + [row, row, pl.BlockSpec((None, ts, pd), lambda i: (layer, i, 0)), _full((1, d)), ANY, ANY],
        out_specs=[row, _full((1, d)), ANY, ANY] + ([_full((8, 128)), _full((1, d))] if head else []),
        out_shape=[_sds((s, d), F32), _sds((1, d), F32), _sds((d, d), F32), _sds((4, pd, qd), F32)]
        + ([_sds((8, 128), F32), _sds((1, d), F32)] if head else []),
        scratch_shapes=[pltpu.VMEM((d, d), BF16), pltpu.VMEM((4, pd, qd), BF16), pltpu.VMEM((d, d), F32),
                        pltpu.VMEM((4, pd, qd), F32), pltpu.SemaphoreType.DMA((2,))],
        operands=[dh2] + (list(loss_head) if head else []) + [h1, gate, p, gain, w_gate, w_proj], rider=rider)


def _mix_a_tile_grads(proj_ref, ch_ref, vh_ref, cw_ref, dh1b, wout_v, carry, dcw_ref, tile, hb):
    e = MIX_WIDTH
    b = proj_ref[:, 0 * e:1 * e].astype(F32)
    c = proj_ref[:, 1 * e:2 * e].astype(F32)
    v = proj_ref[:, 2 * e:3 * e].astype(F32)
    z = proj_ref[:, 3 * e:4 * e].astype(F32)
    cv = c * v
    prev = (ch_ref[...].astype(F32) * vh_ref[...].astype(F32))[hb - HALO:hb]
    tail = jnp.where(tile > 0, prev, jnp.zeros_like(prev))
    cv1 = _shift_down(cv, 1, tail)
    cv2 = _shift_down(cv, 2, tail)
    conv = cw_ref[0:1, :] * cv2 + cw_ref[1:2, :] * cv1 + cw_ref[2:3, :] * cv
    sig = _sigmoid(z)
    sz = z * sig
    y = b * conv
    dm = _dot_nt(dh1b, wout_v[...])
    dz = (dm * y) * (sig * (1.0 + z * (1.0 - sig)))
    dy = dm * sz
    db = dy * conv
    dconv = dy * b
    head = carry[...]
    carry[...] = dconv[0:HALO]
    dcv = cw_ref[2:3, :] * dconv + cw_ref[1:2, :] * _shift_up(dconv, 1, head) + cw_ref[0:1, :] * _shift_up(dconv, 2, head)
    dcw_ref[0:1, :] += jnp.sum(dconv * cv2, axis=0, keepdims=True)
    dcw_ref[1:2, :] += jnp.sum(dconv * cv1, axis=0, keepdims=True)
    dcw_ref[2:3, :] += jnp.sum(dconv * cv, axis=0, keepdims=True)
    parts = [db.astype(BF16), (dcv * v).astype(BF16), (dcv * c).astype(BF16), dz.astype(BF16)]
    return parts, (sz * y).astype(BF16)


def _bwd_mix_a(dh1, h, proj, gain, conv_w, w_in, w_out, name, rider=None):
    s, d = dh1.shape
    e = MIX_WIDTH
    ts = min(TS_MIX, s)
    nt = s // ts
    hb = 16
    per = ts // hb

    def body(dh1_ref, h_ref, proj_ref, ch_ref, vh_ref, gain_ref, cw_ref, win_hbm, wout_hbm,
             dh_ref, dcw_ref, dgain_ref, dwin_hbm, dwout_hbm, win_v, wout_v, acc_in, acc_out, carry, sems):
        i = pl.program_id(0)

        @pl.when(i == 0)
        def _():
            _copy_all([(win_hbm, win_v), (wout_hbm, wout_v)], sems)
            carry[...] = jnp.zeros_like(carry)
            dcw_ref[...] = jnp.zeros_like(dcw_ref)
            dgain_ref[...] = jnp.zeros_like(dgain_ref)
            acc_in[...] = jnp.zeros_like(acc_in)
            acc_out[...] = jnp.zeros_like(acc_out)

        dh1 = dh1_ref[...]
        dh1b = dh1.astype(BF16)
        parts, mb = _mix_a_tile_grads(proj_ref, ch_ref, vh_ref, cw_ref, dh1b, wout_v, carry, dcw_ref, nt - 1 - i, hb)
        acc_out[...] += _dot_tn(mb, dh1b)
        xn, r = _rms(h_ref[...])
        hnb = (xn * gain_ref[...]).astype(BF16)
        for q in range(4):
            acc_in[q] += _dot_tn(hnb, parts[q])
        dhn = _dot_nt(parts[0], win_v[0]) + _dot_nt(parts[1], win_v[1]) + _dot_nt(parts[2], win_v[2]) + _dot_nt(parts[3], win_v[3])
        dh, dgain = _rms_bwd(dhn, xn, r, gain_ref[...])
        dh_ref[...] = dh1 + dh
        dgain_ref[...] += dgain

        @pl.when(i == nt - 1)
        def _():
            _copy_all([(acc_in, dwin_hbm), (acc_out, dwout_hbm)], sems)

    row = lambda width: pl.BlockSpec((ts, width), lambda i: (nt - 1 - i, 0))
    halo = lambda col: pl.BlockSpec((hb, e), lambda i: (jnp.maximum((nt - 1 - i) * per - 1, 0), col))
    return _call(
        body, name=name, grid=(nt,),
        in_specs=[row(d), row(d), row(4 * e), halo(1), halo(2), _full((1, d)), _full((8, e)), ANY, ANY],
        out_specs=[row(d), _full((8, e)), _full((1, d)), ANY, ANY],
        out_shape=[_sds((s, d), F32), _sds((8, e), F32), _sds((1, d), F32), _sds((4, d, e), F32), _sds((e, d), F32)],
        scratch_shapes=[pltpu.VMEM((4, d, e), BF16), pltpu.VMEM((e, d), BF16), pltpu.VMEM((4, d, e), F32),
                        pltpu.VMEM((e, d), F32), pltpu.VMEM((HALO, e), F32), pltpu.SemaphoreType.DMA((2,))],
        operands=[dh1, h, proj, proj, proj, gain, conv_w, w_in, w_out], rider=rider)


def _bwd_mix_a_weights(dh1, h, proj, gain, conv_w, w_out, name, rider=None):
    s, d = dh1.shape
    e = MIX_WIDTH
    ts = min(TS_MIX, s)
    nt = s // ts
    hb = 16
    per = ts // hb

    def body(dh1_ref, h_ref, proj_ref, ch_ref, vh_ref, gain_ref, cw_ref, wout_hbm,
             dproj_ref, dcw_ref, dwin_hbm, dwout_hbm, wout_v, acc_in, acc_out, carry, sems):
        i = pl.program_id(0)

        @pl.when(i == 0)
        def _():
            _copy_all([(wout_hbm, wout_v)], sems)
            carry[...] = jnp.zeros_like(carry)
            dcw_ref[...] = jnp.zeros_like(dcw_ref)
            acc_in[...] = jnp.zeros_like(acc_in)
            acc_out[...] = jnp.zeros_like(acc_out)

        dh1b = dh1_ref[...].astype(BF16)
        parts, mb = _mix_a_tile_grads(proj_ref, ch_ref, vh_ref, cw_ref, dh1b, wout_v, carry, dcw_ref, nt - 1 - i, hb)
        acc_out[...] += _dot_tn(mb, dh1b)
        xn, _ = _rms(h_ref[...])
        hnb = (xn * gain_ref[...]).astype(BF16)
        for q in range(4):
            acc_in[q] += _dot_tn(hnb, parts[q])
            dproj_ref[:, q * e:(q + 1) * e] = parts[q]

        @pl.when(i == nt - 1)
        def _():
            _copy_all([(acc_in, dwin_hbm), (acc_out, dwout_hbm)], sems)

    row = lambda width: pl.BlockSpec((ts, width), lambda i: (nt - 1 - i, 0))
    halo = lambda col: pl.BlockSpec((hb, e), lambda i: (jnp.maximum((nt - 1 - i) * per - 1, 0), col))
    return _call(
        body, name=name, grid=(nt,),
        in_specs=[row(d), row(d), row(4 * e), halo(1), halo(2), _full((1, d)), _full((8, e)), ANY],
        out_specs=[row(4 * e), _full((8, e)), ANY, ANY],
        out_shape=[_sds((s, 4 * e), BF16), _sds((8, e), F32), _sds((4, d, e), F32), _sds((e, d), F32)],
        scratch_shapes=[pltpu.VMEM((e, d), BF16), pltpu.VMEM((4, d, e), F32), pltpu.VMEM((e, d), F32),
                        pltpu.VMEM((HALO, e), F32), pltpu.SemaphoreType.DMA((2,))],
        operands=[dh1, h, proj, proj, proj, gain, conv_w, w_out], rider=rider)


def _bwd_mix_a_input(dproj, h, dh1, gain, w_in, name, rider=None):
    s, d = dh1.shape
    e = MIX_WIDTH
    ts = min(TS_PLE, s)
    nt = s // ts

    def body(dproj_ref, h_ref, dh1_ref, gain_ref, win_hbm, dh_ref, dgain_ref, win_v, sems):
        @pl.when(pl.program_id(0) == 0)
        def _():
            _copy_all([(win_hbm, win_v)], sems)
            dgain_ref[...] = jnp.zeros_like(dgain_ref)

        dhn = _dot_nt(dproj_ref[:, 0:e], win_v[0])
        for q in range(1, 4):
            dhn = dhn + _dot_nt(dproj_ref[:, q * e:(q + 1) * e], win_v[q])
        xn, r = _rms(h_ref[...])
        dh, dgain = _rms_bwd(dhn, xn, r, gain_ref[...])
        dh_ref[...] = dh1_ref[...] + dh
        dgain_ref[...] += dgain

    row = lambda width: pl.BlockSpec((ts, width), lambda i: (i, 0))
    return _call(
        body, name=name, grid=(nt,),
        in_specs=[row(4 * e), row(d), row(d), _full((1, d)), ANY],
        out_specs=[row(d), _full((1, d))],
        out_shape=[_sds((s, d), F32), _sds((1, d), F32)],
        scratch_shapes=[pltpu.VMEM((4, d, e), BF16), pltpu.SemaphoreType.DMA((1,))],
        operands=[dproj, h, dh1, gain, w_in], rider=rider)


def _bwd_mix_b(dh1, h, z, mx, diff, gain, scale, w_in, w_grp, w_out, name, rider=None):
    s, d = dh1.shape
    e = MIX_WIDTH
    ts = min(TS_MIX, s)
    nt = s // ts
    half = e // 2

    def body(dh1_ref, h_ref, z_ref, mx_ref, dd_ref, gain_ref, scale_ref, win_hbm, wgrp_hbm, wout_hbm,
             dh_ref, dscale_ref, dgain_ref, dwin_hbm, dwgrp_hbm, dwout_hbm,
             win_v, wgrp_v, wout_v, acc_in, acc_grp, acc_out, carry, sems):
        i = pl.program_id(0)
        tile = nt - 1 - i

        @pl.when(i == 0)
        def _():
            _copy_all([(win_hbm, win_v), (wout_hbm, wout_v)] + _grp_pairs(wgrp_hbm, wgrp_v), sems)
            carry[...] = jnp.zeros_like(carry)
            dscale_ref[...] = jnp.zeros_like(dscale_ref)
            dgain_ref[...] = jnp.zeros_like(dgain_ref)
            acc_in[...] = jnp.zeros_like(acc_in)
            acc_grp[...] = jnp.zeros_like(acc_grp)
            acc_out[...] = jnp.zeros_like(acc_out)

        zf = z_ref[...].astype(F32)
        mxf = mx_ref[...].astype(F32)
        sig = _sigmoid(zf)
        sz = zf * sig
        mixed = mxf * scale_ref[...]
        dh1 = dh1_ref[...]
        dh1b = dh1.astype(BF16)
        acc_out[...] += _dot_tn((sz * mixed).astype(BF16), dh1b)
        dm = _dot_nt(dh1b, wout_v[...])
        dz = (dm * mixed) * (sig * (1.0 + zf * (1.0 - sig)))
        dmixed = dm * sz
        dscale_ref[...] += jnp.sum(dmixed * mxf, axis=0, keepdims=True)
        dmxb = (dmixed * scale_ref[...]).astype(BF16)
        diff = dd_ref[...]
        for g in range(N_GROUPS):
            cols = slice(g * GROUP_DIM, (g + 1) * GROUP_DIM)
            acc_grp[g] += _dot_tn(diff[:, cols], dmxb[:, cols])
        ddiff = jnp.concatenate(
            [_dot_nt(dmxb[:, g * GROUP_DIM:(g + 1) * GROUP_DIM], wgrp_v[g]) for g in range(N_GROUPS)], axis=1)
        dub = (_pool_bwd(ddiff, carry, tile, ts) - ddiff).astype(BF16)
        dzb = dz.astype(BF16)
        parts = [dub[:, 0:half], dub[:, half:e], dzb[:, 0:half], dzb[:, half:e]]
        xn, r = _rms(h_ref[...])
        hnb = (xn * gain_ref[...]).astype(BF16)
        for k in range(4):
            acc_in[k] += _dot_tn(hnb, parts[k])
        dhn = _dot_nt(parts[0], win_v[0]) + _dot_nt(parts[1], win_v[1]) + _dot_nt(parts[2], win_v[2]) + _dot_nt(parts[3], win_v[3])
        dh, dgain = _rms_bwd(dhn, xn, r, gain_ref[...])
        dh_ref[...] = dh1 + dh
        dgain_ref[...] += dgain

        @pl.when(i == nt - 1)
        def _():
            _copy_all([(acc_in, dwin_hbm), (acc_out, dwout_hbm)] + [(v, hb_) for hb_, v in _grp_pairs(dwgrp_hbm, acc_grp)], sems)

    row = lambda width: pl.BlockSpec((ts, width), lambda i: (nt - 1 - i, 0))
    return _call(
        body, name=name, grid=(nt,),
        in_specs=[row(d), row(d), row(e), row(e), row(e), _full((1, d)), _full((1, e)), ANY, ANY, ANY],
        out_specs=[row(d), _full((1, e)), _full((1, d)), ANY, ANY, ANY],
        out_shape=[_sds((s, d), F32), _sds((1, e), F32), _sds((1, d), F32), _sds((4, d, half), F32),
                   _sds((4, N_GROUPS, GROUP_DIM // 4, GROUP_DIM), F32), _sds((e, d), F32)],
        scratch_shapes=[pltpu.VMEM((4, d, half), BF16), pltpu.VMEM((N_GROUPS, GROUP_DIM, GROUP_DIM), BF16),
                        pltpu.VMEM((e, d), BF16), pltpu.VMEM((4, d, half), F32),
                        pltpu.VMEM((N_GROUPS, GROUP_DIM, GROUP_DIM), F32), pltpu.VMEM((e, d), F32),
                        pltpu.VMEM((4, HALO, e), F32), pltpu.SemaphoreType.DMA((18,))],
        operands=[dh1, h, z, mx, diff, gain, scale, w_in, w_grp, w_out], rider=rider)


def _first_gather(rider, small):
    shards = rider.inputs
    ni = len(shards)

    def body(*refs):
        rin, small_src = refs[:ni], refs[ni]
        rout, small_dst = refs[ni + 1:2 * ni + 1], refs[2 * ni + 1]
        send, recv, ssend, srecv = refs[2 * ni + 2:]
        x, y, c, chips = _place()
        me = 2 * x + y
        peers = [(cx, cy, c) for cx, cy in chips] + [(x, y, 1 - c)]
        vec = [_remote(small_src, small_dst.at[me], ssend, srecv, j, to) for j, to in enumerate(peers)]
        for cp in vec:
            cp.start()
        rider.start(rin, rout, send, recv)
        rider.middle(rin, rout, send, recv)
        rider.finish(rin, rout, send, recv)
        for j, (px, py, _) in enumerate(peers):
            _remote(small_src, small_dst.at[2 * px + py], ssend, srecv, j, peers[j]).wait_recv()
        for cp in vec:
            cp.wait_send()

    outs = pl.pallas_call(
        body, name="first_gather", in_specs=[ANY] * (ni + 1), out_specs=[ANY] * (ni + 1),
        out_shape=rider.out_shapes + [_sds((4,) + small.shape, small.dtype)],
        scratch_shapes=[pltpu.SemaphoreType.DMA((rider.n_sems,)), pltpu.SemaphoreType.DMA((rider.n_sems,)),
                        pltpu.SemaphoreType.DMA((4,)), pltpu.SemaphoreType.DMA((4,))],
    )(*shards, small)
    return list(outs[:ni]), outs[ni]


def _vector_rider(pack):
    flips = [(fx, fy, fc) for fx in (0, 1) for fy in (0, 1) for fc in (0, 1)][1:]

    def copies(rin, rout, send, recv, base):
        x, y, c, _ = _place()
        me = 4 * x + 2 * y + c
        peers = [(1 - x if fx else x, 1 - y if fy else y, 1 - c if fc else c) for fx, fy, fc in flips]
        own = pltpu.make_async_copy(rin[0], rout[0].at[me], send.at[base + 7])
        out = [_remote(rin[0], rout[0].at[me], send, recv, base + r, peer) for r, peer in enumerate(peers)]
        back = [_remote(rin[0], rout[0].at[4 * px + 2 * py + pc], send, recv, base + r, (px, py, pc))
                for r, (px, py, pc) in enumerate(peers)]
        return own, out, back

    def start(rin, rout, send, recv, base=0):
        own, out, _ = copies(rin, rout, send, recv, base)
        own.start()
        for cp in out:
            cp.start()

    def finish(rin, rout, send, recv, base=0):
        own, out, back = copies(rin, rout, send, recv, base)
        for cp in back:
            cp.wait_recv()
        for cp in out:
            cp.wait_send()
        own.wait()

    return _Rider([pack], [_sds((8,) + pack.shape, pack.dtype)], 8, start, finish)


def _vector_sum(landed):
    _, rows, d = landed.shape

    def body(l_ref, out_ref):
        total = l_ref[0]
        for dev in range(1, 8):
            total = total + l_ref[dev]
        out_ref[...] = total

    vmem = pl.BlockSpec(memory_space=pltpu.VMEM)
    return pl.pallas_call(body, name="vector_sum", in_specs=[vmem], out_specs=vmem, out_shape=_sds((rows, d), F32))(landed)


def _ew_rows(rows):
    return min(TR_EW, rows)


def _job_rows(rows, cols):
    return min(rows, max(8, JOB_BLOCK_BYTES // (4 * cols)))


def _pair_sum_job(grad, sibling_rows):
    _, _, rh, cols = grad.shape
    tr = _job_rows(rh, cols)
    nr = rh // tr

    def chip_of(j, pos):
        return jnp.bitwise_xor(pos[0], jnp.where(j == 2, 3, 2 - j))

    return dict(
        ins=[(grad, (None, None, tr, cols), lambda l, pos: (chip_of(l // nr, pos), pos[1], l % nr, 0)),
             (sibling_rows, (None, tr, cols), lambda l, pos: (chip_of(l // nr, pos), l % nr, 0))],
        out=((3, rh, cols), BF16, (None, tr, cols), lambda l, pos: (l // nr, l % nr, 0)),
        steps=3 * nr, fn=lambda g, sb: (g + sb).astype(BF16), alias=None)


def _final_sum_job(grad, sibling_rows, landed, stack, slot, n_slots):
    _, _, rh, cols = grad.shape
    tr = _job_rows(rh, cols)

    def fn(g, sb, ld):
        total = g + sb
        for j in range(3):
            total = total + ld[j].astype(F32)
        return total

    return dict(
        ins=[(grad, (None, None, tr, cols), lambda l, pos: (pos[0], pos[1], l, 0)),
             (sibling_rows, (None, tr, cols), lambda l, pos: (pos[0], l, 0)),
             (landed, (3, tr, cols), lambda l, pos: (0, l, 0))],
        out=((n_slots, 2, rh, cols), F32, (None, None, tr, cols), lambda l, pos: (slot, pos[1], l, 0)),
        steps=rh // tr, fn=fn, alias=stack)


def _run_jobs(jobs, place, name):
    starts, total = [], 0
    for jb in jobs:
        starts.append(total)
        total += jb["steps"]

    def clamped(fn, start, steps):
        return lambda s, pos: fn(jnp.clip(s - start, 0, steps - 1), pos)

    in_specs, operands = [], [place]
    for jb, start in zip(jobs, starts):
        for arr, block, fn in jb["ins"]:
            in_specs.append(pl.BlockSpec(block, clamped(fn, start, jb["steps"])))
            operands.append(arr)
    n_ins = len(in_specs)
    aliases = {}
    for t, jb in enumerate(jobs):
        if jb["alias"] is not None:
            in_specs.append(ANY)
            operands.append(jb["alias"])
            aliases[len(operands) - 1] = t
    out_specs = [pl.BlockSpec(jb["out"][2], clamped(jb["out"][3], start, jb["steps"])) for jb, start in zip(jobs, starts)]

    def body(place_ref, *refs):
        in_refs, out_refs = refs[:n_ins], refs[len(in_specs):]
        s = pl.program_id(0)
        first = 0
        for t, (jb, start) in enumerate(zip(jobs, starts)):
            mine = in_refs[first:first + len(jb["ins"])]
            first += len(jb["ins"])

            @pl.when((s >= start) & (s < start + jb["steps"]))
            def _(mine=mine, t=t, jb=jb):
                out_refs[t][...] = jb["fn"](*[r[...] for r in mine])

    grid_spec = pltpu.PrefetchScalarGridSpec(num_scalar_prefetch=1, grid=(total,), in_specs=in_specs, out_specs=out_specs)
    outs = pl.pallas_call(body, name=name, grid_spec=grid_spec,
                          out_shape=[_sds(jb["out"][0], jb["out"][1]) for jb in jobs],
                          input_output_aliases=aliases, compiler_params=_params(1))(*operands)
    return list(outs)


def _adamw(g, w, m, v, name):
    rows, cols = g.shape
    tr = _ew_rows(rows)

    def body(g_ref, w_ref, m_ref, v_ref, delta_ref, nm_ref, nv_ref):
        gg = g_ref[...]
        nm = ADAM_B1 * m_ref[...] + (1.0 - ADAM_B1) * gg
        nv = ADAM_B2 * v_ref[...] + (1.0 - ADAM_B2) * (gg * gg)
        m_hat = nm / (1.0 - ADAM_B1 ** ADAM_STEP)
        v_hat = nv / (1.0 - ADAM_B2 ** ADAM_STEP)
        delta_ref[...] = -ADAM_LR * (m_hat / (jnp.sqrt(v_hat) + ADAM_EPS) + ADAM_WD * w_ref[...])
        nm_ref[...] = nm
        nv_ref[...] = nv

    spec = pl.BlockSpec((tr, cols), lambda i: (i, 0))
    return pl.pallas_call(
        body, name=name, grid=(rows // tr,), in_specs=[spec] * 4, out_specs=[spec] * 3,
        out_shape=[_sds((rows, cols), F32)] * 3, compiler_params=_params(),
    )(g, w, m, v)


BIG = ["a_w_in", "a_w_out", "b_w_in", "b_w_grp", "b_w_out", "ple_w_gate", "ple_w_proj"]

GATHER_PLAN = {
    "first": [("a_w_in", 0), ("a_w_out", 0)],
    "mix0": [("ple_w_gate", 0), ("ple_w_proj", 0), ("b_w_in", 0), ("b_w_grp", 0), ("b_w_out", 0)],
    "ple0": [("ple_w_gate", 1), ("ple_w_proj", 1)],
    "mix1": [("a_w_in", 1), ("a_w_out", 1), ("ple_w_gate", 2), ("ple_w_proj", 2)],
    "mix2": [("b_w_in", 1), ("b_w_grp", 1), ("b_w_out", 1), ("ple_w_gate", 3), ("ple_w_proj", 3)],
}


def _as_2d(name, a):
    if name == "b_w_grp":
        return a.reshape(a.shape[0], N_GROUPS * (GROUP_DIM // 4), GROUP_DIM)
    return a


def kernel(x, p, norm_mix, a_w_in, a_w_conv, a_w_out, b_w_in, b_w_grp, b_scale, b_w_out, ple_norm, ple_w_gate, ple_w_proj, final_norm, loss_target, m_norm_mix, m_a_w_in, m_a_w_conv, m_a_w_out, m_b_w_in, m_b_w_grp, m_b_scale, m_b_w_out, m_ple_norm, m_ple_w_gate, m_ple_w_proj, m_final_norm, v_norm_mix, v_a_w_in, v_a_w_conv, v_a_w_out, v_b_w_in, v_b_w_grp, v_b_scale, v_b_w_out, v_ple_norm, v_ple_w_gate, v_ple_w_proj, v_final_norm):
    d, e = D_MODEL, MIX_WIDTH
    s = x.shape[1]
    cx, cy, cc = lax.axis_index("x"), lax.axis_index("y"), lax.axis_index("c")
    chip = 2 * cx + cy
    place = jnp.stack([chip, cc]).astype(jnp.int32)

    weights = dict(a_w_in=a_w_in, a_w_out=a_w_out, b_w_in=b_w_in, b_w_grp=b_w_grp, b_w_out=b_w_out,
                   ple_w_gate=ple_w_gate, ple_w_proj=ple_w_proj)
    moms = dict(a_w_in=m_a_w_in, a_w_out=m_a_w_out, b_w_in=m_b_w_in, b_w_grp=m_b_w_grp, b_w_out=m_b_w_out,
                ple_w_gate=m_ple_w_gate, ple_w_proj=m_ple_w_proj)
    vars_ = dict(a_w_in=v_a_w_in, a_w_out=v_a_w_out, b_w_in=v_b_w_in, b_w_grp=v_b_w_grp, b_w_out=v_b_w_out,
                 ple_w_gate=v_ple_w_gate, ple_w_proj=v_ple_w_proj)
    w2d = {nm: _as_2d(nm, weights[nm]) for nm in BIG}
    bf = {nm: w2d[nm].astype(BF16).reshape(w2d[nm].shape[0], 2, w2d[nm].shape[1] // 2, w2d[nm].shape[2]) for nm in BIG}
    gathered = {}

    def gather_rider(host):
        keys = GATHER_PLAN.get(host)
        return _gather_rider([bf[nm] for nm, _ in keys], [j for _, j in keys]) if keys else None

    def keep(host, landed):
        for k, a in zip(GATHER_PLAN.get(host, []), landed):
            gathered[k] = a

    def weight(nm, j):
        a = gathered[(nm, j)]
        shapes = {"a_w_in": (4, d, e), "a_w_out": (e, d), "b_w_in": (4, d, e // 2),
                  "b_w_grp": (4, N_GROUPS, GROUP_DIM // 4, GROUP_DIM), "b_w_out": (e, d), "ple_w_gate": (d, d),
                  "ple_w_proj": (4, PLE_DIM, d // 4)}
        return a.reshape(shapes[nm])

    small = jnp.concatenate([a_w_conv.reshape(6, e // 4), b_scale], axis=0)
    landed, small_full = _first_gather(gather_rider("first"), small)
    keep("first", landed)
    small_full = small_full.transpose(1, 0, 2).reshape(8, e)
    conv_w = [jnp.concatenate([small_full[3 * j:3 * j + 3], jnp.zeros((5, e), F32)], axis=0) for j in range(2)]
    scale_w = [small_full[6 + j:7 + j] for j in range(2)]

    p3 = p.reshape(DEPTH, s, PLE_DIM)
    mix_gain = [norm_mix[i:i + 1] for i in range(DEPTH)]
    ple_gain = [ple_norm[i:i + 1] for i in range(DEPTH)]

    h = x.reshape(s, d)
    saved = []
    for i in range(DEPTH):
        j = i // 2
        rider = gather_rider(f"mix{i}")
        ple = (p3, i, ple_gain[i], weight("ple_w_gate", i), weight("ple_w_proj", i)) if i > 0 else None
        if i % 2 == 0:
            outs, landed = _fwd_mix_a(h, mix_gain[i], conv_w[j], weight("a_w_in", j), weight("a_w_out", j),
                                      f"fwd_mix_a{j}", rider, ple)
            mix = dict(proj=outs[1])
        else:
            outs, landed = _fwd_mix_b(h, mix_gain[i], scale_w[j], weight("b_w_in", j), weight("b_w_grp", j),
                                      weight("b_w_out", j), f"fwd_mix_b{j}", rider, ple)
            mix = dict(z=outs[1], mx=outs[2], diff=outs[3])
        keep(f"mix{i}", landed)
        h1 = outs[0]
        if ple:
            h2, gate = outs[-2:]
        else:
            (h2, gate), landed = _fwd_ple(h1, p3, ple_gain[i], weight("ple_w_gate", i), weight("ple_w_proj", i), i,
                                          gather_rider(f"ple{i}"))
            keep(f"ple{i}", landed)
        saved.append(dict(h=h, h1=h1, gate=gate, **mix))
        h = h2

    n_slots = {nm: weights[nm].shape[0] for nm in BIG}
    stacks = {nm: None for nm in BIG}

    class Group:
        def __init__(self, keys, grads):
            self.keys, self.stage = keys, 0
            self.g32 = [g.reshape(4, 2, w2d[nm].shape[1] // 2, w2d[nm].shape[2]) for (nm, _), g in zip(keys, grads)]

        def rider(self):
            if self.stage == 0:
                return _pair_rider(self.g32)
            if self.stage == 1:
                return _ici_rider(self.pair_sums)
            return _final_rider([stacks[nm] for nm, _ in self.keys], [j for _, j in self.keys])

        def jobs_after(self, landed):
            if self.stage == 0:
                self.from_sibling = landed
                return [_pair_sum_job(g, sb) for g, sb in zip(self.g32, landed)]
            if self.stage == 1:
                return [_final_sum_job(g, sb, ld, stacks[nm], j, n_slots[nm])
                        for (nm, j), g, sb, ld in zip(self.keys, self.g32, self.from_sibling, landed)]
            return []

        def advance(self, landed, summed):
            if self.stage == 0:
                self.pair_sums = summed
            else:
                for (nm, _), a in zip(self.keys, summed if self.stage == 1 else landed):
                    stacks[nm] = a
            self.stage += 1

    active = []
    batches = [0]

    def riders_now():
        parts = [g.rider() for g in active]
        return parts, _merge(parts)

    def advance_all(parts, landed):
        groups = list(active)
        pieces = _split(landed, parts)
        jobs = [g.jobs_after(l) for g, l in zip(groups, pieces)]
        flat = sum(jobs, [])
        outs = _run_jobs(flat, place, f"reduce_sums{batches[0]}") if flat else []
        batches[0] += 1
        for g, l, jb in zip(groups, pieces, jobs):
            g.advance(l, outs[:len(jb)])
            outs = outs[len(jb):]
            if g.stage == 3:
                active.remove(g)

    d_mix_gain, d_ple_gain = [None] * DEPTH, [None] * DEPTH
    d_conv, d_scale = [None] * 2, [None] * 2
    for i in reversed(range(DEPTH)):
        j = i // 2
        sv = saved[i]
        parts, rider = riders_now()
        if i == DEPTH - 1:
            (dh1, d_ple_gain[i], dwg, dwp, loss_part, d_final), landed = _bwd_ple(
                h, sv["h1"], sv["gate"], p3, ple_gain[i], weight("ple_w_gate", i), weight("ple_w_proj", i), i, rider,
                loss_head=(loss_target.reshape(s, d), final_norm.reshape(1, d)))
        else:
            (dh1, d_ple_gain[i], dwg, dwp), landed = _bwd_ple(
                dh, sv["h1"], sv["gate"], p3, ple_gain[i], weight("ple_w_gate", i), weight("ple_w_proj", i), i, rider)
        advance_all(parts, landed)
        active.append(Group([("ple_w_gate", i), ("ple_w_proj", i)], [dwg, dwp]))
        parts, rider = riders_now()
        if i == 0:
            (dproj, d_conv[0], dwin, dwout), landed = _bwd_mix_a_weights(
                dh1, sv["h"], sv["proj"], mix_gain[0], conv_w[0], weight("a_w_out", 0), "bwd_mix_a0_weights", rider)
            advance_all(parts, landed)
            active.append(Group([("a_w_in", 0), ("a_w_out", 0)], [dwin, dwout]))
            parts, rider = riders_now()
            advance_all(parts, _run_rider(rider, "pair_exchange0"))
            parts, rider = riders_now()
            (dh, d_mix_gain[0]), landed = _bwd_mix_a_input(dproj, sv["h"], dh1, mix_gain[0], weight("a_w_in", 0),
                                                          "bwd_mix_a0_input", rider)
            advance_all(parts, landed)
            continue
        if i % 2 == 0:
            (dh, d_conv[j], d_mix_gain[i], dwin, dwout), landed = _bwd_mix_a(
                dh1, sv["h"], sv["proj"], mix_gain[i], conv_w[j], weight("a_w_in", j), weight("a_w_out", j),
                f"bwd_mix_a{j}", rider)
            new = Group([("a_w_in", j), ("a_w_out", j)], [dwin, dwout])
        else:
            (dh, d_scale[j], d_mix_gain[i], dwin, dwgrp, dwout), landed = _bwd_mix_b(
                dh1, sv["h"], sv["z"], sv["mx"], sv["diff"], mix_gain[i], scale_w[j], weight("b_w_in", j),
                weight("b_w_grp", j), weight("b_w_out", j), f"bwd_mix_b{j}", rider)
            new = Group([("b_w_in", j), ("b_w_grp", j), ("b_w_out", j)], [dwin, dwgrp, dwout])
        advance_all(parts, landed)
        active.append(new)
    grad_x = dh.reshape(1, s, d)

    pack = jnp.concatenate(
        d_mix_gain + d_ple_gain + [d_final] + [d_conv[0][0:3], d_conv[1][0:3]] + d_scale
        + [jnp.tile(loss_part[0:1], (1, d // 128)), jnp.zeros((SMALL_ROWS - 18, d), F32)], axis=0)
    vectors = _vector_rider(pack)
    tail = 0
    while active:
        parts, _ = riders_now()
        extra = [vectors] if tail == 0 else []
        landed = _run_rider(_merge(parts + extra), f"tail_exchange{tail}")
        if extra:
            total = _vector_sum(_split(landed, parts + extra)[-1][0])
        advance_all(parts, landed)
        tail += 1
    loss = total[17, 0]

    out_grad, out_delta, out_m, out_v = {}, {}, {}, {}
    for nm in BIG:
        shape = weights[nm].shape
        flat = (w2d[nm].shape[0] * w2d[nm].shape[1], w2d[nm].shape[2])
        g2 = stacks[nm].reshape(flat)
        delta, new_m, new_v = _adamw(g2, w2d[nm].reshape(flat), _as_2d(nm, moms[nm]).reshape(flat),
                                     _as_2d(nm, vars_[nm]).reshape(flat), f"adamw_{nm}")
        out_grad[nm], out_delta[nm] = g2.reshape(shape), delta.reshape(shape)
        out_m[nm], out_v[nm] = new_m.reshape(shape), new_v.reshape(shape)

    rep_rows = 16
    rep = lambda a, b_, c_: jnp.concatenate([a, b_, c_.reshape(1, d), jnp.zeros((rep_rows - 9, d), F32)], axis=0)
    rep_delta, rep_m, rep_v = _adamw(
        jnp.concatenate([total[0:9], jnp.zeros((rep_rows - 9, d), F32)], axis=0),
        rep(norm_mix, ple_norm, final_norm), rep(m_norm_mix, m_ple_norm, m_final_norm),
        rep(v_norm_mix, v_ple_norm, v_final_norm), "adamw_gains")
    mine_cols = lax.dynamic_slice_in_dim(total[9:17], chip * (e // 4), e // 4, axis=1)
    col = lambda a, b_: jnp.concatenate([a.reshape(6, e // 4), b_], axis=0)
    col_delta, col_m, col_v = _adamw(mine_cols, col(a_w_conv, b_scale), col(m_a_w_conv, m_b_scale),
                                     col(v_a_w_conv, v_b_scale), "adamw_cols")

    def unpack(rep_a, col_a):
        return dict(norm_mix=rep_a[0:4], ple_norm=rep_a[4:8], final_norm=rep_a[8],
                    a_w_conv=col_a[0:6].reshape(2, 3, e // 4), b_scale=col_a[6:8])

    small_out = [unpack(total, mine_cols), unpack(rep_delta, col_delta), unpack(rep_m, col_m), unpack(rep_v, col_v)]
    order = ["norm_mix", "a_w_in", "a_w_conv", "a_w_out", "b_w_in", "b_w_grp", "b_scale", "b_w_out", "ple_norm",
             "ple_w_gate", "ple_w_proj", "final_norm"]
    outs = [loss, grad_x]
    for big, small_d in zip([out_grad, out_delta, out_m, out_v], small_out):
        outs += [big[nm] if nm in big else small_d[nm] for nm in order]
    return tuple(outs)
```

```python
import jax
import jax.numpy as jnp
from jax import lax
from jax.experimental import pallas as pl
from jax.experimental.pallas import tpu as pltpu

F32 = jnp.float32
BF16 = jnp.bfloat16
MESH = pl.DeviceIdType.MESH

D_MODEL = 1024
MIX_WIDTH = 1024
PLE_DIM = 256
N_GROUPS = 4
GROUP_DIM = 256
POOL_WINDOWS = (2, 4, 8, 16)
DEPTH = 4
EPS = 1e-6

ADAM_LR = 0.001
ADAM_B1 = 0.9
ADAM_B2 = 0.999
ADAM_EPS = 1e-08
ADAM_WD = 0.01
ADAM_STEP = 10

HALO = 8
TS_MIX = 256
TS_PLE = 512
VMEM_LIMIT = 56 * 1024 * 1024
SMALL_ROWS = 24
JOB_BLOCK_BYTES = 1024 * 1024
ADAMW_BLOCK_BYTES = 512 * 1024
ADAMW_BIG_BLOCK_BYTES = 2 * 1024 * 1024
ADAMW_ALONE = ("a_w_in", "b_w_in", "ple_w_gate")
MIDDLE_STEPS_BEFORE_END = 1

ANY = pl.BlockSpec(memory_space=pl.ANY)


def _sds(shape, dtype):
    return jax.ShapeDtypeStruct(shape, dtype)


def _full(shape):
    nd = len(shape)
    return pl.BlockSpec(shape, lambda *_: (0,) * nd)


def _params(n_axes=1):
    return pltpu.CompilerParams(dimension_semantics=("arbitrary",) * n_axes, vmem_limit_bytes=VMEM_LIMIT)


def _dot(a, b):
    return jnp.dot(a, b, preferred_element_type=F32)


def _dot_nt(a, b):
    return lax.dot_general(a, b, (((1,), (1,)), ((), ())), preferred_element_type=F32)


def _dot_tn(a, b):
    return lax.dot_general(a, b, (((0,), (0,)), ((), ())), preferred_element_type=F32)


def _sigmoid(z):
    return 1.0 / (1.0 + jnp.exp(-z))


def _shift_down(x, k, tail):
    rolled = pltpu.roll(x, k, 0)
    rt = tail if k % HALO == 0 else pltpu.roll(tail, k % HALO, 0)
    row = lax.broadcasted_iota(jnp.int32, rt.shape, 0)
    head = jnp.where(row < k, rt, rolled[0:HALO])
    return jnp.concatenate([head, rolled[HALO:]], axis=0)


def _shift_up(x, k, head_next):
    n = x.shape[0]
    rolled = pltpu.roll(x, n - k, 0)
    rh = head_next if k % HALO == 0 else pltpu.roll(head_next, HALO - k % HALO, 0)
    row = lax.broadcasted_iota(jnp.int32, rh.shape, 0)
    tail = jnp.where(row >= HALO - k, rh, rolled[n - HALO:n])
    return jnp.concatenate([rolled[:n - HALO], tail], axis=0)


def _inv_counts(tile, ts):
    t = tile * ts + lax.broadcasted_iota(jnp.int32, (ts, 1), 0)
    return [1.0 / jnp.minimum(t + 1, w).astype(F32) for w in POOL_WINDOWS]


def _pool_fwd(u, carry, tile, ts):
    inv = _inv_counts(tile, ts)
    outs = []
    for g, w in enumerate(POOL_WINDOWS):
        cols = slice(g * GROUP_DIM, (g + 1) * GROUP_DIM)
        s = u[:, cols]
        level, k = 0, 1
        while k < w:
            tail = carry[level, :, cols]
            carry[level, :, cols] = s[ts - HALO:ts]
            s = s + _shift_down(s, k, tail)
            level, k = level + 1, k * 2
        outs.append(s * inv[g])
    return jnp.concatenate(outs, axis=1)


def _pool_bwd(dd, carry, tile, ts):
    inv = _inv_counts(tile, ts)
    outs = []
    for g, w in enumerate(POOL_WINDOWS):
        cols = slice(g * GROUP_DIM, (g + 1) * GROUP_DIM)
        q = dd[:, cols] * inv[g]
        level, k = 0, 1
        while k < w:
            head = carry[level, :, cols]
            carry[level, :, cols] = q[0:HALO]
            q = q + _shift_up(q, k, head)
            level, k = level + 1, k * 2
        outs.append(q)
    return jnp.concatenate(outs, axis=1)


def _copy_all(pairs, sems):
    copies = [pltpu.make_async_copy(src, dst, sems.at[n]) for n, (src, dst) in enumerate(pairs)]
    for cp in copies:
        cp.start()
    for cp in copies:
        cp.wait()


def _grp_pairs(wgrp_hbm, wgrp_v):
    rows = GROUP_DIM // 4
    return [(wgrp_hbm.at[k, g], wgrp_v.at[g, pl.ds(k * rows, rows), :]) for k in range(4) for g in range(N_GROUPS)]


def _rms(h):
    r = lax.rsqrt(jnp.mean(h * h, axis=-1, keepdims=True) + EPS)
    return h * r, r


def _rms_bwd(dhn, xn, r, gain):
    dgain = jnp.sum(dhn * xn, axis=0, keepdims=True)
    dxn = dhn * gain
    dh = r * (dxn - xn * jnp.mean(dxn * xn, axis=-1, keepdims=True))
    return dh, dgain


class _Rider:
    def __init__(self, inputs, out_shapes, n_sems, start, finish, middle=None, aliases=None):
        self.inputs, self.out_shapes, self.n_sems = list(inputs), list(out_shapes), n_sems
        self.start, self.middle, self.finish = start, middle, finish
        self.aliases = dict(aliases or {})


def _merge(riders):
    riders = [r for r in riders if r is not None]
    if not riders:
        return None
    if len(riders) == 1:
        return riders[0]

    def phase(which):
        def run(rin, rout, send, recv, base=0):
            i0 = o0 = s0 = 0
            for r in riders:
                fn = getattr(r, which)
                if fn is not None:
                    fn(rin[i0:i0 + len(r.inputs)], rout[o0:o0 + len(r.out_shapes)], send, recv, base + s0)
                i0, o0, s0 = i0 + len(r.inputs), o0 + len(r.out_shapes), s0 + r.n_sems
        return run

    aliases, i0, o0 = {}, 0, 0
    for r in riders:
        aliases.update({i0 + a: o0 + b for a, b in r.aliases.items()})
        i0, o0 = i0 + len(r.inputs), o0 + len(r.out_shapes)
    return _Rider(sum([r.inputs for r in riders], []), sum([r.out_shapes for r in riders], []),
                  sum(r.n_sems for r in riders), phase("start"), phase("finish"),
                  phase("middle") if any(r.middle for r in riders) else None, aliases)


def _split(landed, riders):
    out, o0 = [], 0
    for r in riders:
        if r is None:
            out.append(None)
        else:
            out.append(landed[o0:o0 + len(r.out_shapes)])
            o0 += len(r.out_shapes)
    return out


def _place():
    x, y, c = lax.axis_index("x"), lax.axis_index("y"), lax.axis_index("c")
    chips = [(1 - x, y), (x, 1 - y), (1 - x, 1 - y)]
    return x, y, c, chips


def _remote(src, dst, send_sems, recv_sems, sem, to):
    return pltpu.make_async_remote_copy(src_ref=src, dst_ref=dst, send_sem=send_sems.at[sem], recv_sem=recv_sems.at[sem],
                                        device_id=to, device_id_type=MESH)


def _gather_rider(stacked, slots):
    ni = len(stacked)

    def first_hops(rin, rout, send, recv, base, x, y, c, chips):
        me = 2 * x + y
        return [_remote(rin[t].at[slots[t], c], rout[t].at[me, c], send, recv, base + 7 * t + j, (cx, cy, c))
                for j, (cx, cy) in enumerate(chips) for t in range(ni)]

    def passes(rout, send, recv, base, x, y, c, chips):
        out = []
        for j, (cx, cy) in enumerate(chips):
            for t in range(ni):
                landed = rout[t].at[2 * cx + cy, c]
                out.append((_remote(landed, landed, send, recv, base + 7 * t + j, (x, y, 1 - c)),
                            _remote(landed, landed, send, recv, base + 7 * t + 3 + j, (x, y, 1 - c))))
        return out

    def own(rin, rout, send, recv, base, x, y, c):
        return [_remote(rin[t].at[slots[t]], rout[t].at[2 * x + y], send, recv, base + 7 * t + 6, (x, y, 1 - c))
                for t in range(ni)]

    def start(rin, rout, send, recv, base=0):
        x, y, c, chips = _place()
        for cp in first_hops(rin, rout, send, recv, base, x, y, c, chips) + own(rin, rout, send, recv, base, x, y, c):
            cp.start()

    def middle(rin, rout, send, recv, base=0):
        x, y, c, chips = _place()
        for arrival, onward in passes(rout, send, recv, base, x, y, c, chips):
            arrival.wait_recv()
            onward.start()

    def finish(rin, rout, send, recv, base=0):
        x, y, c, chips = _place()
        for j, (cx, cy) in enumerate(chips):
            for t in range(ni):
                other = rout[t].at[2 * cx + cy, 1 - c]
                _remote(other, other, send, recv, base + 7 * t + 3 + j, (x, y, 1 - c)).wait_recv()
        for cp in own(rin, rout, send, recv, base, x, y, c):
            cp.wait_recv()
            cp.wait_send()
        for cp in first_hops(rin, rout, send, recv, base, x, y, c, chips):
            cp.wait_send()
        for _, onward in passes(rout, send, recv, base, x, y, c, chips):
            onward.wait_send()

    return _Rider(stacked, [_sds((4,) + a.shape[1:], a.dtype) for a in stacked], 7 * ni, start, finish, middle)


def _pair_rider(grads):
    ni = len(grads)

    def copies(rin, rout, send, recv, base):
        x, y, c, _ = _place()
        return [_remote(rin[t].at[:, 1 - c], rout[t], send, recv, base + t, (x, y, 1 - c)) for t in range(ni)]

    def start(rin, rout, send, recv, base=0):
        for cp in copies(rin, rout, send, recv, base):
            cp.start()

    def finish(rin, rout, send, recv, base=0):
        for cp in copies(rin, rout, send, recv, base):
            cp.wait()

    return _Rider(grads, [_sds(g.shape[:1] + g.shape[2:], g.dtype) for g in grads], ni, start, finish)


def _ici_rider(pair_sums):
    ni = len(pair_sums)

    def copies(rin, rout, send, recv, base):
        x, y, c, chips = _place()
        return [_remote(rin[t].at[j], rout[t].at[j], send, recv, base + 3 * t + j, (cx, cy, c))
                for j, (cx, cy) in enumerate(chips) for t in range(ni)]

    def start(rin, rout, send, recv, base=0):
        for cp in copies(rin, rout, send, recv, base):
            cp.start()

    def finish(rin, rout, send, recv, base=0):
        for cp in copies(rin, rout, send, recv, base):
            cp.wait()

    return _Rider(pair_sums, [_sds((3,) + g.shape[1:], g.dtype) for g in pair_sums], 3 * ni, start, finish)


def _final_rider(summed, slots):
    ni = len(summed)

    def copies(rout, send, recv, base):
        x, y, c, _ = _place()
        return [(_remote(rout[t].at[slots[t], c], rout[t].at[slots[t], c], send, recv, base + t, (x, y, 1 - c)),
                 _remote(rout[t].at[slots[t], 1 - c], rout[t].at[slots[t], 1 - c], send, recv, base + t, (x, y, 1 - c)))
                for t in range(ni)]

    def start(rin, rout, send, recv, base=0):
        for mine, _ in copies(rout, send, recv, base):
            mine.start()

    def finish(rin, rout, send, recv, base=0):
        for mine, theirs in copies(rout, send, recv, base):
            mine.wait_send()
            theirs.wait_recv()

    return _Rider(summed, [_sds(a.shape, a.dtype) for a in summed], ni, start, finish,
                  aliases={t: t for t in range(ni)})


class _Block:
    def __init__(self, array, index):
        self.array, self.index = array, index

    def spec(self):
        index = self.index
        return pl.BlockSpec((None,) + self.array.shape[1:], lambda *_: (index, 0, 0))


def _call(body, *, name, grid, in_specs, out_specs, out_shape, scratch_shapes, operands, rider=None):
    operands, in_specs = list(operands), list(in_specs)
    for n, op in enumerate(operands):
        if isinstance(op, _Block):
            operands[n], in_specs[n] = op.array, op.spec()
    if rider is None:
        outs = pl.pallas_call(body, name=name, grid=grid, in_specs=in_specs, out_specs=out_specs, out_shape=out_shape,
                              scratch_shapes=scratch_shapes, compiler_params=_params(len(grid)))(*operands)
        return list(outs), []
    n_in, n_out, n_scr = len(in_specs), len(out_specs), len(scratch_shapes)
    r_in, r_out = len(rider.inputs), len(rider.out_shapes)
    steps = 1
    for g in grid:
        steps *= g
    mid = max(steps - 1 - MIDDLE_STEPS_BEFORE_END, 0)

    def full_body(*refs):
        own_in, rin = refs[:n_in], refs[n_in:n_in + r_in]
        own_out = refs[n_in + r_in:n_in + r_in + n_out]
        rout = refs[n_in + r_in + n_out:n_in + r_in + n_out + r_out]
        own_scr = refs[n_in + r_in + n_out + r_out:n_in + r_in + n_out + r_out + n_scr]
        send, recv = refs[-2], refs[-1]
        step = pl.program_id(0)
        for axis in range(1, len(grid)):
            step = step * grid[axis] + pl.program_id(axis)

        @pl.when(step == 0)
        def _():
            rider.start(rin, rout, send, recv)

        body(*own_in, *own_out, *own_scr)

        if rider.middle is not None:
            @pl.when(step == mid)
            def _():
                rider.middle(rin, rout, send, recv)

        @pl.when(step == steps - 1)
        def _():
            rider.finish(rin, rout, send, recv)

    outs = pl.pallas_call(
        full_body, name=name, grid=grid,
        in_specs=list(in_specs) + [ANY] * r_in, out_specs=list(out_specs) + [ANY] * r_out,
        out_shape=list(out_shape) + rider.out_shapes,
        scratch_shapes=list(scratch_shapes) + [pltpu.SemaphoreType.DMA((rider.n_sems,)), pltpu.SemaphoreType.DMA((rider.n_sems,))],
        input_output_aliases={n_in + a: n_out + b for a, b in rider.aliases.items()},
        compiler_params=_params(len(grid)),
    )(*operands, *rider.inputs)
    return list(outs[:n_out]), list(outs[n_out:])


def _run_rider(rider, name):
    r_in, r_out = len(rider.inputs), len(rider.out_shapes)

    def body(*refs):
        rin, rout, send, recv = refs[:r_in], refs[r_in:r_in + r_out], refs[-2], refs[-1]
        rider.start(rin, rout, send, recv)
        if rider.middle is not None:
            rider.middle(rin, rout, send, recv)
        rider.finish(rin, rout, send, recv)

    outs = pl.pallas_call(
        body, name=name, in_specs=[ANY] * r_in, out_specs=[ANY] * r_out, out_shape=rider.out_shapes,
        scratch_shapes=[pltpu.SemaphoreType.DMA((rider.n_sems,)), pltpu.SemaphoreType.DMA((rider.n_sems,))],
        input_output_aliases=rider.aliases,
    )(*rider.inputs)
    return list(outs)


def _ple_tile(h1, p_ref, gain_ref, wg_v, wp_v):
    xn, _ = _rms(h1)
    hpb = (xn * gain_ref[...]).astype(BF16)
    gate = _sigmoid(_dot(hpb, wg_v[...]))
    pb = p_ref[...].astype(BF16)
    pe = jnp.concatenate([_dot(pb, wp_v[k]) for k in range(4)], axis=1)
    return h1 + gate * pe, gate


def _ple_parts(ple, ts, d):
    p, layer, gain, w_gate, w_proj = ple
    s, pd = p.shape[1:]
    row = pl.BlockSpec((ts, d), lambda i: (i, 0))
    return dict(
        operands=[p, gain, w_gate, w_proj],
        in_specs=[pl.BlockSpec((None, ts, pd), lambda i: (layer, i, 0)), _full((1, d)), ANY, ANY],
        out_specs=[row, row], out_shape=[_sds((s, d), F32), _sds((s, d), BF16)],
        scratch=[pltpu.VMEM((d, d), BF16), pltpu.VMEM((4, pd, d // 4), BF16)])


def _fwd_mix_a(h, gain, conv_w, w_in, w_out, name, rider=None, ple=None):
    s, d = h.shape
    e = MIX_WIDTH
    ts = min(TS_MIX, s)
    nt = s // ts
    extra = _ple_parts(ple, ts, d) if ple else None

    def body(*refs):
        h_ref, gain_ref, cw_ref, win_hbm, wout_hbm = refs[:5]
        n_in = 9 if ple else 5
        h1_ref, proj_ref = refs[n_in:n_in + 2]
        win_v, wout_v, carry, sems = refs[n_in + (4 if ple else 2):][:4]
        i = pl.program_id(0)

        @pl.when(i == 0)
        def _():
            loads = [(win_hbm, win_v), (wout_hbm, wout_v)]
            if ple:
                loads += [(refs[7], refs[-2]), (refs[8], refs[-1])]
            _copy_all(loads, sems)
            carry[...] = jnp.zeros_like(carry)

        hh = h_ref[...]
        xn, _ = _rms(hh)
        hnb = (xn * gain_ref[...]).astype(BF16)
        b = _dot(hnb, win_v[0])
        c = _dot(hnb, win_v[1])
        v = _dot(hnb, win_v[2])
        z = _dot(hnb, win_v[3])
        proj_ref[:, 0 * e:1 * e] = b.astype(BF16)
        proj_ref[:, 1 * e:2 * e] = c.astype(BF16)
        proj_ref[:, 2 * e:3 * e] = v.astype(BF16)
        proj_ref[:, 3 * e:4 * e] = z.astype(BF16)
        cv = c * v
        tail = carry[...]
        carry[...] = cv[ts - HALO:ts]
        conv = cw_ref[0:1, :] * _shift_down(cv, 2, tail) + cw_ref[1:2, :] * _shift_down(cv, 1, tail) + cw_ref[2:3, :] * cv
        mb = ((z * _sigmoid(z)) * (b * conv)).astype(BF16)
        h1 = hh + _dot(mb, wout_v[...])
        h1_ref[...] = h1
        if ple:
            h2, gate = _ple_tile(h1, refs[5], refs[6], refs[-2], refs[-1])
            refs[n_in + 2][...] = h2
            refs[n_in + 3][...] = gate.astype(BF16)

    row = lambda width: pl.BlockSpec((ts, width), lambda i: (i, 0))
    return _call(
        body, name=name, grid=(nt,),
        in_specs=[row(d), _full((1, d)), _full((8, e)), ANY, ANY] + (extra["in_specs"] if ple else []),
        out_specs=[row(d), row(4 * e)] + (extra["out_specs"] if ple else []),
        out_shape=[_sds((s, d), F32), _sds((s, 4 * e), BF16)] + (extra["out_shape"] if ple else []),
        scratch_shapes=[pltpu.VMEM((4, d, e), BF16), pltpu.VMEM((e, d), BF16), pltpu.VMEM((HALO, e), F32),
                        pltpu.SemaphoreType.DMA((4,))] + (extra["scratch"] if ple else []),
        operands=[h, gain, conv_w, w_in, w_out] + (extra["operands"] if ple else []), rider=rider)


def _fwd_mix_b(h, gain, scale, w_in, w_grp, w_out, name, rider=None, ple=None):
    s, d = h.shape
    e = MIX_WIDTH
    ts = min(TS_MIX, s)
    nt = s // ts
    extra = _ple_parts(ple, ts, d) if ple else None

    def body(*refs):
        h_ref, gain_ref, scale_ref, win_hbm, wgrp_hbm, wout_hbm = refs[:6]
        n_in = 10 if ple else 6
        h1_ref, z_ref, mx_ref, dd_ref = refs[n_in:n_in + 4]
        win_v, wgrp_v, wout_v, carry, sems = refs[n_in + (6 if ple else 4):][:5]
        i = pl.program_id(0)

        @pl.when(i == 0)
        def _():
            loads = [(win_hbm, win_v), (wout_hbm, wout_v)] + _grp_pairs(wgrp_hbm, wgrp_v)
            if ple:
                loads += [(refs[8], refs[-2]), (refs[9], refs[-1])]
            _copy_all(loads, sems)
            carry[...] = jnp.zeros_like(carry)

        hh = h_ref[...]
        xn, _ = _rms(hh)
        hnb = (xn * gain_ref[...]).astype(BF16)
        u = jnp.concatenate([_dot(hnb, win_v[0]), _dot(hnb, win_v[1])], axis=1)
        z = jnp.concatenate([_dot(hnb, win_v[2]), _dot(hnb, win_v[3])], axis=1)
        z_ref[...] = z.astype(BF16)
        diff = (_pool_fwd(u, carry, i, ts) - u).astype(BF16)
        dd_ref[...] = diff
        mx = jnp.concatenate(
            [_dot(diff[:, g * GROUP_DIM:(g + 1) * GROUP_DIM], wgrp_v[g]) for g in range(N_GROUPS)], axis=1)
        mx_ref[...] = mx.astype(BF16)
        mb = ((z * _sigmoid(z)) * (mx * scale_ref[...])).astype(BF16)
        h1 = hh + _dot(mb, wout_v[...])
        h1_ref[...] = h1
        if ple:
            h2, gate = _ple_tile(h1, refs[6], refs[7], refs[-2], refs[-1])
            refs[n_in + 4][...] = h2
            refs[n_in + 5][...] = gate.astype(BF16)

    row = lambda width: pl.BlockSpec((ts, width), lambda i: (i, 0))
    return _call(
        body, name=name, grid=(nt,),
        in_specs=[row(d), _full((1, d)), _full((1, e)), ANY, ANY, ANY] + (extra["in_specs"] if ple else []),
        out_specs=[row(d), row(e), row(e), row(e)] + (extra["out_specs"] if ple else []),
        out_shape=[_sds((s, d), F32)] + [_sds((s, e), BF16)] * 3 + (extra["out_shape"] if ple else []),
        scratch_shapes=[pltpu.VMEM((4, d, e // 2), BF16), pltpu.VMEM((N_GROUPS, GROUP_DIM, GROUP_DIM), BF16),
                        pltpu.VMEM((e, d), BF16), pltpu.VMEM((4, HALO, e), F32), pltpu.SemaphoreType.DMA((20,))]
        + (extra["scratch"] if ple else []),
        operands=[h, gain, scale, w_in, w_grp, w_out] + (extra["operands"] if ple else []), rider=rider)


def _fwd_ple(h1, p, gain, w_gate, w_proj, layer, rider=None):
    s, d = h1.shape
    pd = p.shape[-1]
    ts = min(TS_PLE, s)
    nt = s // ts

    def body(h1_ref, p_ref, gain_ref, wg_hbm, wp_hbm, h2_ref, gate_ref, wg_v, wp_v, sems):
        @pl.when(pl.program_id(0) == 0)
        def _():
            _copy_all([(wg_hbm, wg_v), (wp_hbm, wp_v)], sems)

        hh = h1_ref[...]
        xn, _ = _rms(hh)
        hpb = (xn * gain_ref[...]).astype(BF16)
        gate = _sigmoid(_dot(hpb, wg_v[...]))
        pb = p_ref[...].astype(BF16)
        pe = jnp.concatenate([_dot(pb, wp_v[k]) for k in range(4)], axis=1)
        gate_ref[...] = gate.astype(BF16)
        h2_ref[...] = hh + gate * pe

    row = lambda width: pl.BlockSpec((ts, width), lambda i: (i, 0))
    return _call(
        body, name=f"fwd_ple{layer}", grid=(nt,),
        in_specs=[row(d), pl.BlockSpec((None, ts, pd), lambda i: (layer, i, 0)), _full((1, d)), ANY, ANY],
        out_specs=[row(d), row(d)],
        out_shape=[_sds((s, d), F32), _sds((s, d), BF16)],
        scratch_shapes=[pltpu.VMEM((d, d), BF16), pltpu.VMEM((4, pd, d // 4), BF16), pltpu.SemaphoreType.DMA((2,))],
        operands=[h1, p, gain, w_gate, w_proj], rider=rider)


def _bwd_ple(dh2, h1, gate, p, gain, w_gate, w_proj, layer, rider=None, loss_head=None):
    s, d = dh2.shape
    pd = p.shape[-1]
    ts = min(TS_PLE, s)
    nt = s // ts
    qd = d // 4
    n_head = 0 if loss_head is None else 2

    def body(*refs):
        dh2_ref = refs[0]
        h1_ref, gate_ref, p_ref, gain_ref, wg_hbm, wp_hbm, dh1_ref, dgain_ref, dwg_hbm, dwp_hbm = refs[1 + n_head:11 + n_head]
        wg_v, wp_v, acc_g, acc_p, sems = refs[-5:]
        i = pl.program_id(0)

        @pl.when(i == 0)
        def _():
            _copy_all([(wg_hbm, wg_v), (wp_hbm, wp_v)], sems)
            dgain_ref[...] = jnp.zeros_like(dgain_ref)
            acc_g[...] = jnp.zeros_like(acc_g)
            acc_p[...] = jnp.zeros_like(acc_p)

        if loss_head is None:
            g2 = dh2_ref[...]
        else:
            t_ref, fgain_ref, loss_ref, dfgain_ref = refs[1], refs[2], refs[11 + n_head], refs[12 + n_head]

            @pl.when(i == 0)
            def _():
                loss_ref[...] = jnp.zeros_like(loss_ref)
                dfgain_ref[...] = jnp.zeros_like(dfgain_ref)

            xf, rf = _rms(dh2_ref[...])
            err = xf * fgain_ref[...] - t_ref[...]
            part = 0.5 * jnp.sum(jnp.mean(err * err, axis=-1, keepdims=True), axis=0, keepdims=True)
            loss_ref[...] += jnp.broadcast_to(part, loss_ref.shape)
            g2, dfgain = _rms_bwd(err * (1.0 / d), xf, rf, fgain_ref[...])
            dfgain_ref[...] += dfgain
        gate_f = gate_ref[...].astype(F32)
        xn, r = _rms(h1_ref[...])
        hpb = (xn * gain_ref[...]).astype(BF16)
        pb = p_ref[...].astype(BF16)
        pe = jnp.concatenate([_dot(pb, wp_v[k]) for k in range(4)], axis=1)
        dpeb = (g2 * gate_f).astype(BF16)
        dab = ((g2 * pe) * (gate_f * (1.0 - gate_f))).astype(BF16)
        acc_g[...] += _dot_tn(hpb, dab)
        for k in range(4):
            acc_p[k] += _dot_tn(pb, dpeb[:, k * qd:(k + 1) * qd])
        dhp = _dot_nt(dab, wg_v[...])
        dh, dgain = _rms_bwd(dhp, xn, r, gain_ref[...])
        dh1_ref[...] = g2 + dh
        dgain_ref[...] += dgain

        @pl.when(i == nt - 1)
        def _():
            _copy_all([(acc_g, dwg_hbm), (acc_p, dwp_hbm)], sems)

    row = pl.BlockSpec((ts, d), lambda i: (i, 0))
    head = loss_head is not None
    return _call(
        body, name=f"bwd_ple{layer}", grid=(nt,),
        in_specs=[row] + ([row, _full((1, d))] if head else [])
        + [row, row, pl.BlockSpec((None, ts, pd), lambda i: (layer, i, 0)), _full((1, d)), ANY, ANY],
        out_specs=[row, _full((1, d)), ANY, ANY] + ([_full((8, 128)), _full((1, d))] if head else []),
        out_shape=[_sds((s, d), F32), _sds((1, d), F32), _sds((d, d), F32), _sds((4, pd, qd), F32)]
        + ([_sds((8, 128), F32), _sds((1, d), F32)] if head else []),
        scratch_shapes=[pltpu.VMEM((d, d), BF16), pltpu.VMEM((4, pd, qd), BF16), pltpu.VMEM((d, d), F32),
                        pltpu.VMEM((4, pd, qd), F32), pltpu.SemaphoreType.DMA((2,))],
        operands=[dh2] + (list(loss_head) if head else []) + [h1, gate, p, gain, w_gate, w_proj], rider=rider)


def _mix_a_tile_grads(proj_ref, ch_ref, vh_ref, cw_ref, dh1b, wout_v, carry, dcw_ref, tile, hb):
    e = MIX_WIDTH
    b = proj_ref[:, 0 * e:1 * e].astype(F32)
    c = proj_ref[:, 1 * e:2 * e].astype(F32)
    v = proj_ref[:, 2 * e:3 * e].astype(F32)
    z = proj_ref[:, 3 * e:4 * e].astype(F32)
    cv = c * v
    prev = (ch_ref[...].astype(F32) * vh_ref[...].astype(F32))[hb - HALO:hb]
    tail = jnp.where(tile > 0, prev, jnp.zeros_like(prev))
    cv1 = _shift_down(cv, 1, tail)
    cv2 = _shift_down(cv, 2, tail)
    conv = cw_ref[0:1, :] * cv2 + cw_ref[1:2, :] * cv1 + cw_ref[2:3, :] * cv
    sig = _sigmoid(z)
    sz = z * sig
    y = b * conv
    dm = _dot_nt(dh1b, wout_v[...])
    dz = (dm * y) * (sig * (1.0 + z * (1.0 - sig)))
    dy = dm * sz
    db = dy * conv
    dconv = dy * b
    head = carry[...]
    carry[...] = dconv[0:HALO]
    dcv = cw_ref[2:3, :] * dconv + cw_ref[1:2, :] * _shift_up(dconv, 1, head) + cw_ref[0:1, :] * _shift_up(dconv, 2, head)
    dcw_ref[0:1, :] += jnp.sum(dconv * cv2, axis=0, keepdims=True)
    dcw_ref[1:2, :] += jnp.sum(dconv * cv1, axis=0, keepdims=True)
    dcw_ref[2:3, :] += jnp.sum(dconv * cv, axis=0, keepdims=True)
    parts = [db.astype(BF16), (dcv * v).astype(BF16), (dcv * c).astype(BF16), dz.astype(BF16)]
    return parts, (sz * y).astype(BF16)


def _bwd_mix_a(dh1, h, proj, gain, conv_w, w_in, w_out, name, rider=None):
    s, d = dh1.shape
    e = MIX_WIDTH
    ts = min(TS_MIX, s)
    nt = s // ts
    hb = 16
    per = ts // hb

    def body(dh1_ref, h_ref, proj_ref, ch_ref, vh_ref, gain_ref, cw_ref, win_hbm, wout_hbm,
             dh_ref, dcw_ref, dgain_ref, dwin_hbm, dwout_hbm, win_v, wout_v, acc_in, acc_out, carry, sems):
        i = pl.program_id(0)

        @pl.when(i == 0)
        def _():
            _copy_all([(win_hbm, win_v), (wout_hbm, wout_v)], sems)
            carry[...] = jnp.zeros_like(carry)
            dcw_ref[...] = jnp.zeros_like(dcw_ref)
            dgain_ref[...] = jnp.zeros_like(dgain_ref)
            acc_in[...] = jnp.zeros_like(acc_in)
            acc_out[...] = jnp.zeros_like(acc_out)

        dh1 = dh1_ref[...]
        dh1b = dh1.astype(BF16)
        parts, mb = _mix_a_tile_grads(proj_ref, ch_ref, vh_ref, cw_ref, dh1b, wout_v, carry, dcw_ref, nt - 1 - i, hb)
        acc_out[...] += _dot_tn(mb, dh1b)
        xn, r = _rms(h_ref[...])
        hnb = (xn * gain_ref[...]).astype(BF16)
        for q in range(4):
            acc_in[q] += _dot_tn(hnb, parts[q])
        dhn = _dot_nt(parts[0], win_v[0]) + _dot_nt(parts[1], win_v[1]) + _dot_nt(parts[2], win_v[2]) + _dot_nt(parts[3], win_v[3])
        dh, dgain = _rms_bwd(dhn, xn, r, gain_ref[...])
        dh_ref[...] = dh1 + dh
        dgain_ref[...] += dgain

        @pl.when(i == nt - 1)
        def _():
            _copy_all([(acc_in, dwin_hbm), (acc_out, dwout_hbm)], sems)

    row = lambda width: pl.BlockSpec((ts, width), lambda i: (nt - 1 - i, 0))
    halo = lambda col: pl.BlockSpec((hb, e), lambda i: (jnp.maximum((nt - 1 - i) * per - 1, 0), col))
    return _call(
        body, name=name, grid=(nt,),
        in_specs=[row(d), row(d), row(4 * e), halo(1), halo(2), _full((1, d)), _full((8, e)), ANY, ANY],
        out_specs=[row(d), _full((8, e)), _full((1, d)), ANY, ANY],
        out_shape=[_sds((s, d), F32), _sds((8, e), F32), _sds((1, d), F32), _sds((4, d, e), F32), _sds((e, d), F32)],
        scratch_shapes=[pltpu.VMEM((4, d, e), BF16), pltpu.VMEM((e, d), BF16), pltpu.VMEM((4, d, e), F32),
                        pltpu.VMEM((e, d), F32), pltpu.VMEM((HALO, e), F32), pltpu.SemaphoreType.DMA((2,))],
        operands=[dh1, h, proj, proj, proj, gain, conv_w, w_in, w_out], rider=rider)


def _bwd_mix_a_weights(dh1, h, proj, gain, conv_w, w_out, name, rider=None):
    s, d = dh1.shape
    e = MIX_WIDTH
    ts = min(TS_MIX, s)
    nt = s // ts
    hb = 16
    per = ts // hb

    def body(dh1_ref, h_ref, proj_ref, ch_ref, vh_ref, gain_ref, cw_ref, wout_hbm,
             dproj_ref, dcw_ref, dwin_hbm, dwout_hbm, wout_v, acc_in, acc_out, carry, sems):
        i = pl.program_id(0)

        @pl.when(i == 0)
        def _():
            _copy_all([(wout_hbm, wout_v)], sems)
            carry[...] = jnp.zeros_like(carry)
            dcw_ref[...] = jnp.zeros_like(dcw_ref)
            acc_in[...] = jnp.zeros_like(acc_in)
            acc_out[...] = jnp.zeros_like(acc_out)

        dh1b = dh1_ref[...].astype(BF16)
        parts, mb = _mix_a_tile_grads(proj_ref, ch_ref, vh_ref, cw_ref, dh1b, wout_v, carry, dcw_ref, nt - 1 - i, hb)
        acc_out[...] += _dot_tn(mb, dh1b)
        xn, _ = _rms(h_ref[...])
        hnb = (xn * gain_ref[...]).astype(BF16)
        for q in range(4):
            acc_in[q] += _dot_tn(hnb, parts[q])
            dproj_ref[:, q * e:(q + 1) * e] = parts[q]

        @pl.when(i == nt - 1)
        def _():
            _copy_all([(acc_in, dwin_hbm), (acc_out, dwout_hbm)], sems)

    row = lambda width: pl.BlockSpec((ts, width), lambda i: (nt - 1 - i, 0))
    halo = lambda col: pl.BlockSpec((hb, e), lambda i: (jnp.maximum((nt - 1 - i) * per - 1, 0), col))
    return _call(
        body, name=name, grid=(nt,),
        in_specs=[row(d), row(d), row(4 * e), halo(1), halo(2), _full((1, d)), _full((8, e)), ANY],
        out_specs=[row(4 * e), _full((8, e)), ANY, ANY],
        out_shape=[_sds((s, 4 * e), BF16), _sds((8, e), F32), _sds((4, d, e), F32), _sds((e, d), F32)],
        scratch_shapes=[pltpu.VMEM((e, d), BF16), pltpu.VMEM((4, d, e), F32), pltpu.VMEM((e, d), F32),
                        pltpu.VMEM((HALO, e), F32), pltpu.SemaphoreType.DMA((2,))],
        operands=[dh1, h, proj, proj, proj, gain, conv_w, w_out], rider=rider)


def _bwd_mix_a_input(dproj, h, dh1, gain, w_in, name, rider=None):
    s, d = dh1.shape
    e = MIX_WIDTH
    ts = min(TS_PLE, s)
    nt = s // ts

    def body(dproj_ref, h_ref, dh1_ref, gain_ref, win_hbm, dh_ref, dgain_ref, win_v, sems):
        @pl.when(pl.program_id(0) == 0)
        def _():
            _copy_all([(win_hbm, win_v)], sems)
            dgain_ref[...] = jnp.zeros_like(dgain_ref)

        dhn = _dot_nt(dproj_ref[:, 0:e], win_v[0])
        for q in range(1, 4):
            dhn = dhn + _dot_nt(dproj_ref[:, q * e:(q + 1) * e], win_v[q])
        xn, r = _rms(h_ref[...])
        dh, dgain = _rms_bwd(dhn, xn, r, gain_ref[...])
        dh_ref[...] = dh1_ref[...] + dh
        dgain_ref[...] += dgain

    row = lambda width: pl.BlockSpec((ts, width), lambda i: (i, 0))
    return _call(
        body, name=name, grid=(nt,),
        in_specs=[row(4 * e), row(d), row(d), _full((1, d)), ANY],
        out_specs=[row(d), _full((1, d))],
        out_shape=[_sds((s, d), F32), _sds((1, d), F32)],
        scratch_shapes=[pltpu.VMEM((4, d, e), BF16), pltpu.SemaphoreType.DMA((1,))],
        operands=[dproj, h, dh1, gain, w_in], rider=rider)


def _bwd_mix_b(dh1, h, z, mx, diff, gain, scale, w_in, w_grp, w_out, name, rider=None):
    s, d = dh1.shape
    e = MIX_WIDTH
    ts = min(TS_MIX, s)
    nt = s // ts
    half = e // 2

    def body(dh1_ref, h_ref, z_ref, mx_ref, dd_ref, gain_ref, scale_ref, win_hbm, wgrp_hbm, wout_hbm,
             dh_ref, dscale_ref, dgain_ref, dwin_hbm, dwgrp_hbm, dwout_hbm,
             win_v, wgrp_v, wout_v, acc_in, acc_grp, acc_out, carry, sems):
        i = pl.program_id(0)
        tile = nt - 1 - i

        @pl.when(i == 0)
        def _():
            _copy_all([(win_hbm, win_v), (wout_hbm, wout_v)] + _grp_pairs(wgrp_hbm, wgrp_v), sems)
            carry[...] = jnp.zeros_like(carry)
            dscale_ref[...] = jnp.zeros_like(dscale_ref)
            dgain_ref[...] = jnp.zeros_like(dgain_ref)
            acc_in[...] = jnp.zeros_like(acc_in)
            acc_grp[...] = jnp.zeros_like(acc_grp)
            acc_out[...] = jnp.zeros_like(acc_out)

        zf = z_ref[...].astype(F32)
        mxf = mx_ref[...].astype(F32)
        sig = _sigmoid(zf)
        sz = zf * sig
        mixed = mxf * scale_ref[...]
        dh1 = dh1_ref[...]
        dh1b = dh1.astype(BF16)
        acc_out[...] += _dot_tn((sz * mixed).astype(BF16), dh1b)
        dm = _dot_nt(dh1b, wout_v[...])
        dz = (dm * mixed) * (sig * (1.0 + zf * (1.0 - sig)))
        dmixed = dm * sz
        dscale_ref[...] += jnp.sum(dmixed * mxf, axis=0, keepdims=True)
        dmxb = (dmixed * scale_ref[...]).astype(BF16)
        diff = dd_ref[...]
        for g in range(N_GROUPS):
            cols = slice(g * GROUP_DIM, (g + 1) * GROUP_DIM)
            acc_grp[g] += _dot_tn(diff[:, cols], dmxb[:, cols])
        ddiff = jnp.concatenate(
            [_dot_nt(dmxb[:, g * GROUP_DIM:(g + 1) * GROUP_DIM], wgrp_v[g]) for g in range(N_GROUPS)], axis=1)
        dub = (_pool_bwd(ddiff, carry, tile, ts) - ddiff).astype(BF16)
        dzb = dz.astype(BF16)
        parts = [dub[:, 0:half], dub[:, half:e], dzb[:, 0:half], dzb[:, half:e]]
        xn, r = _rms(h_ref[...])
        hnb = (xn * gain_ref[...]).astype(BF16)
        for k in range(4):
            acc_in[k] += _dot_tn(hnb, parts[k])
        dhn = _dot_nt(parts[0], win_v[0]) + _dot_nt(parts[1], win_v[1]) + _dot_nt(parts[2], win_v[2]) + _dot_nt(parts[3], win_v[3])
        dh, dgain = _rms_bwd(dhn, xn, r, gain_ref[...])
        dh_ref[...] = dh1 + dh
        dgain_ref[...] += dgain

        @pl.when(i == nt - 1)
        def _():
            _copy_all([(acc_in, dwin_hbm), (acc_out, dwout_hbm)] + [(v, hb_) for hb_, v in _grp_pairs(dwgrp_hbm, acc_grp)], sems)

    row = lambda width: pl.BlockSpec((ts, width), lambda i: (nt - 1 - i, 0))
    return _call(
        body, name=name, grid=(nt,),
        in_specs=[row(d), row(d), row(e), row(e), row(e), _full((1, d)), _full((1, e)), ANY, ANY, ANY],
        out_specs=[row(d), _full((1, e)), _full((1, d)), ANY, ANY, ANY],
        out_shape=[_sds((s, d), F32), _sds((1, e), F32), _sds((1, d), F32), _sds((4, d, half), F32),
                   _sds((4, N_GROUPS, GROUP_DIM // 4, GROUP_DIM), F32), _sds((e, d), F32)],
        scratch_shapes=[pltpu.VMEM((4, d, half), BF16), pltpu.VMEM((N_GROUPS, GROUP_DIM, GROUP_DIM), BF16),
                        pltpu.VMEM((e, d), BF16), pltpu.VMEM((4, d, half), F32),
                        pltpu.VMEM((N_GROUPS, GROUP_DIM, GROUP_DIM), F32), pltpu.VMEM((e, d), F32),
                        pltpu.VMEM((4, HALO, e), F32), pltpu.SemaphoreType.DMA((18,))],
        operands=[dh1, h, z, mx, diff, gain, scale, w_in, w_grp, w_out], rider=rider)


def _first_gather(rider, small):
    shards = rider.inputs
    ni = len(shards)

    def body(*refs):
        rin, small_src = refs[:ni], refs[ni]
        rout, small_dst = refs[ni + 1:2 * ni + 1], refs[2 * ni + 1]
        send, recv, ssend, srecv = refs[2 * ni + 2:]
        x, y, c, chips = _place()
        me = 2 * x + y
        peers = [(cx, cy, c) for cx, cy in chips] + [(x, y, 1 - c)]
        vec = [_remote(small_src, small_dst.at[me], ssend, srecv, j, to) for j, to in enumerate(peers)]
        for cp in vec:
            cp.start()
        rider.start(rin, rout, send, recv)
        rider.middle(rin, rout, send, recv)
        rider.finish(rin, rout, send, recv)
        for j, (px, py, _) in enumerate(peers):
            _remote(small_src, small_dst.at[2 * px + py], ssend, srecv, j, peers[j]).wait_recv()
        for cp in vec:
            cp.wait_send()

    outs = pl.pallas_call(
        body, name="first_gather", in_specs=[ANY] * (ni + 1), out_specs=[ANY] * (ni + 1),
        out_shape=rider.out_shapes + [_sds((4,) + small.shape, small.dtype)],
        scratch_shapes=[pltpu.SemaphoreType.DMA((rider.n_sems,)), pltpu.SemaphoreType.DMA((rider.n_sems,)),
                        pltpu.SemaphoreType.DMA((4,)), pltpu.SemaphoreType.DMA((4,))],
    )(*shards, small)
    return list(outs[:ni]), outs[ni]


def _vector_rider(pack):
    flips = [(fx, fy, fc) for fx in (0, 1) for fy in (0, 1) for fc in (0, 1)][1:]

    def copies(rin, rout, send, recv, base):
        x, y, c, _ = _place()
        me = 4 * x + 2 * y + c
        peers = [(1 - x if fx else x, 1 - y if fy else y, 1 - c if fc else c) for fx, fy, fc in flips]
        own = pltpu.make_async_copy(rin[0], rout[0].at[me], send.at[base + 7])
        out = [_remote(rin[0], rout[0].at[me], send, recv, base + r, peer) for r, peer in enumerate(peers)]
        back = [_remote(rin[0], rout[0].at[4 * px + 2 * py + pc], send, recv, base + r, (px, py, pc))
                for r, (px, py, pc) in enumerate(peers)]
        return own, out, back

    def start(rin, rout, send, recv, base=0):
        own, out, _ = copies(rin, rout, send, recv, base)
        own.start()
        for cp in out:
            cp.start()

    def finish(rin, rout, send, recv, base=0):
        own, out, back = copies(rin, rout, send, recv, base)
        for cp in back:
            cp.wait_recv()
        for cp in out:
            cp.wait_send()
        own.wait()

    return _Rider([pack], [_sds((8,) + pack.shape, pack.dtype)], 8, start, finish)


def _vector_sum(landed):
    _, rows, d = landed.shape

    def body(l_ref, out_ref):
        total = l_ref[0]
        for dev in range(1, 8):
            total = total + l_ref[dev]
        out_ref[...] = total

    vmem = pl.BlockSpec(memory_space=pltpu.VMEM)
    return pl.pallas_call(body, name="vector_sum", in_specs=[vmem], out_specs=vmem, out_shape=_sds((rows, d), F32))(landed)


def _job_rows(rows, cols):
    return min(rows, max(8, JOB_BLOCK_BYTES // (4 * cols)))


def _pair_sum_job(grad, sibling_rows):
    _, _, rh, cols = grad.shape
    tr = _job_rows(rh, cols)
    nr = rh // tr

    def chip_of(j, pos):
        return jnp.bitwise_xor(pos[0], jnp.where(j == 2, 3, 2 - j))

    return dict(
        ins=[(grad, (None, None, tr, cols), lambda l, pos: (chip_of(l // nr, pos), pos[1], l % nr, 0)),
             (sibling_rows, (None, tr, cols), lambda l, pos: (chip_of(l // nr, pos), l % nr, 0))],
        outs=[((3, rh, cols), BF16, (None, tr, cols), lambda l, pos: (l // nr, l % nr, 0))],
        steps=3 * nr, fn=lambda g, sb: [(g + sb).astype(BF16)], alias=None)


def _final_sum_job(grad, sibling_rows, landed, stack, slot, n_slots):
    _, _, rh, cols = grad.shape
    tr = _job_rows(rh, cols)

    def fn(g, sb, ld):
        total = g + sb
        for j in range(3):
            total = total + ld[j].astype(F32)
        return [total]

    return dict(
        ins=[(grad, (None, None, tr, cols), lambda l, pos: (pos[0], pos[1], l, 0)),
             (sibling_rows, (None, tr, cols), lambda l, pos: (pos[0], l, 0)),
             (landed, (3, tr, cols), lambda l, pos: (0, l, 0))],
        outs=[((n_slots, 2, rh, cols), F32, (None, None, tr, cols), lambda l, pos: (slot, pos[1], l, 0))],
        steps=rh // tr, fn=fn, alias=stack)


def _adamw_job(g, w, m, v, block_bytes):
    rows, cols = g.shape
    tr = min(rows, max(8, block_bytes // (4 * cols)))

    def fn(gg, ww, mm, vv):
        nm = ADAM_B1 * mm + (1.0 - ADAM_B1) * gg
        nv = ADAM_B2 * vv + (1.0 - ADAM_B2) * (gg * gg)
        m_hat = nm / (1.0 - ADAM_B1 ** ADAM_STEP)
        v_hat = nv / (1.0 - ADAM_B2 ** ADAM_STEP)
        return [-ADAM_LR * (m_hat / (jnp.sqrt(v_hat) + ADAM_EPS) + ADAM_WD * ww), nm, nv]

    block = lambda l, pos: (l, 0)
    return dict(ins=[(a, (tr, cols), block) for a in (g, w, m, v)],
                outs=[((rows, cols), F32, (tr, cols), block)] * 3, steps=rows // tr, fn=fn, alias=None)


def _run_jobs(jobs, place, name):
    starts, total = [], 0
    for jb in jobs:
        starts.append(total)
        total += jb["steps"]

    def clamped(fn, start, steps):
        return lambda s, pos: fn(jnp.clip(s - start, 0, steps - 1), pos)

    in_specs, operands = [], [place]
    for jb, start in zip(jobs, starts):
        for arr, block, fn in jb["ins"]:
            in_specs.append(pl.BlockSpec(block, clamped(fn, start, jb["steps"])))
            operands.append(arr)
    n_ins = len(in_specs)
    first_out, n_outs = [], 0
    for jb in jobs:
        first_out.append(n_outs)
        n_outs += len(jb["outs"])
    aliases = {}
    for t, jb in enumerate(jobs):
        if jb["alias"] is not None:
            in_specs.append(ANY)
            operands.append(jb["alias"])
            aliases[len(operands) - 1] = first_out[t]
    out_specs = [pl.BlockSpec(block, clamped(fn, start, jb["steps"]))
                 for jb, start in zip(jobs, starts) for _, _, block, fn in jb["outs"]]

    def body(place_ref, *refs):
        in_refs, out_refs = refs[:n_ins], refs[len(in_specs):]
        s = pl.program_id(0)
        first = 0
        for t, (jb, start) in enumerate(zip(jobs, starts)):
            mine = in_refs[first:first + len(jb["ins"])]
            first += len(jb["ins"])

            @pl.when((s >= start) & (s < start + jb["steps"]))
            def _(mine=mine, t=t, jb=jb):
                values = jb["fn"](*[r[...] for r in mine])
                for n, value in enumerate(values):
                    out_refs[first_out[t] + n][...] = value

    grid_spec = pltpu.PrefetchScalarGridSpec(num_scalar_prefetch=1, grid=(total,), in_specs=in_specs, out_specs=out_specs)
    outs = pl.pallas_call(body, name=name, grid_spec=grid_spec,
                          out_shape=[_sds(shape, dtype) for jb in jobs for shape, dtype, _, _ in jb["outs"]],
                          input_output_aliases=aliases, compiler_params=_params(1))(*operands)
    return [list(outs[first_out[t]:first_out[t] + len(jb["outs"])]) for t, jb in enumerate(jobs)]


BIG = ["a_w_in", "a_w_out", "b_w_in", "b_w_grp", "b_w_out", "ple_w_gate", "ple_w_proj"]

GATHER_PLAN = {
    "first": [("a_w_in", 0), ("a_w_out", 0)],
    "mix0": [("ple_w_gate", 0), ("ple_w_proj", 0), ("b_w_in", 0), ("b_w_grp", 0), ("b_w_out", 0)],
    "ple0": [("ple_w_gate", 1), ("ple_w_proj", 1)],
    "mix1": [("a_w_in", 1), ("a_w_out", 1), ("ple_w_gate", 2), ("ple_w_proj", 2)],
    "mix2": [("b_w_in", 1), ("b_w_grp", 1), ("b_w_out", 1), ("ple_w_gate", 3), ("ple_w_proj", 3)],
}


def _as_2d(name, a):
    if name == "b_w_grp":
        return a.reshape(a.shape[0], N_GROUPS * (GROUP_DIM // 4), GROUP_DIM)
    return a


def kernel(x, p, norm_mix, a_w_in, a_w_conv, a_w_out, b_w_in, b_w_grp, b_scale, b_w_out, ple_norm, ple_w_gate, ple_w_proj, final_norm, loss_target, m_norm_mix, m_a_w_in, m_a_w_conv, m_a_w_out, m_b_w_in, m_b_w_grp, m_b_scale, m_b_w_out, m_ple_norm, m_ple_w_gate, m_ple_w_proj, m_final_norm, v_norm_mix, v_a_w_in, v_a_w_conv, v_a_w_out, v_b_w_in, v_b_w_grp, v_b_scale, v_b_w_out, v_ple_norm, v_ple_w_gate, v_ple_w_proj, v_final_norm):
    d, e = D_MODEL, MIX_WIDTH
    s = x.shape[1]
    cx, cy, cc = lax.axis_index("x"), lax.axis_index("y"), lax.axis_index("c")
    chip = 2 * cx + cy
    place = jnp.stack([chip, cc]).astype(jnp.int32)

    weights = dict(a_w_in=a_w_in, a_w_out=a_w_out, b_w_in=b_w_in, b_w_grp=b_w_grp, b_w_out=b_w_out,
                   ple_w_gate=ple_w_gate, ple_w_proj=ple_w_proj)
    moms = dict(a_w_in=m_a_w_in, a_w_out=m_a_w_out, b_w_in=m_b_w_in, b_w_grp=m_b_w_grp, b_w_out=m_b_w_out,
                ple_w_gate=m_ple_w_gate, ple_w_proj=m_ple_w_proj)
    vars_ = dict(a_w_in=v_a_w_in, a_w_out=v_a_w_out, b_w_in=v_b_w_in, b_w_grp=v_b_w_grp, b_w_out=v_b_w_out,
                 ple_w_gate=v_ple_w_gate, ple_w_proj=v_ple_w_proj)
    w2d = {nm: _as_2d(nm, weights[nm]) for nm in BIG}
    bf = {nm: w2d[nm].astype(BF16).reshape(w2d[nm].shape[0], 2, w2d[nm].shape[1] // 2, w2d[nm].shape[2]) for nm in BIG}
    gathered = {}

    def gather_rider(host):
        keys = GATHER_PLAN.get(host)
        return _gather_rider([bf[nm] for nm, _ in keys], [j for _, j in keys]) if keys else None

    def keep(host, landed):
        for k, a in zip(GATHER_PLAN.get(host, []), landed):
            gathered[k] = a

    def weight(nm, j):
        a = gathered[(nm, j)]
        shapes = {"a_w_in": (4, d, e), "a_w_out": (e, d), "b_w_in": (4, d, e // 2),
                  "b_w_grp": (4, N_GROUPS, GROUP_DIM // 4, GROUP_DIM), "b_w_out": (e, d), "ple_w_gate": (d, d),
                  "ple_w_proj": (4, PLE_DIM, d // 4)}
        return a.reshape(shapes[nm])

    pad = jnp.zeros((4, e // 4), F32)
    small = jnp.concatenate([a_w_conv[0], b_scale[0:1], pad, a_w_conv[1], b_scale[1:2], pad], axis=0)
    landed, small_full = _first_gather(gather_rider("first"), small)
    keep("first", landed)
    small_full = small_full.transpose(1, 0, 2).reshape(16, e)
    conv_w = [_Block(small_full.reshape(2, 8, e), j) for j in range(2)]
    scale_w = [_Block(small_full.reshape(16, 1, e), 8 * j + 3) for j in range(2)]

    p3 = p.reshape(DEPTH, s, PLE_DIM)
    mix_gain = [_Block(norm_mix.reshape(DEPTH, 1, d), i) for i in range(DEPTH)]
    ple_gain = [_Block(ple_norm.reshape(DEPTH, 1, d), i) for i in range(DEPTH)]

    h = x.reshape(s, d)
    saved = []
    for i in range(DEPTH):
        j = i // 2
        rider = gather_rider(f"mix{i}")
        ple = (p3, i, ple_gain[i], weight("ple_w_gate", i), weight("ple_w_proj", i)) if i > 0 else None
        if i % 2 == 0:
            outs, landed = _fwd_mix_a(h, mix_gain[i], conv_w[j], weight("a_w_in", j), weight("a_w_out", j),
                                      f"fwd_mix_a{j}", rider, ple)
            mix = dict(proj=outs[1])
        else:
            outs, landed = _fwd_mix_b(h, mix_gain[i], scale_w[j], weight("b_w_in", j), weight("b_w_grp", j),
                                      weight("b_w_out", j), f"fwd_mix_b{j}", rider, ple)
            mix = dict(z=outs[1], mx=outs[2], diff=outs[3])
        keep(f"mix{i}", landed)
        h1 = outs[0]
        if ple:
            h2, gate = outs[-2:]
        else:
            (h2, gate), landed = _fwd_ple(h1, p3, ple_gain[i], weight("ple_w_gate", i), weight("ple_w_proj", i), i,
                                          gather_rider(f"ple{i}"))
            keep(f"ple{i}", landed)
        saved.append(dict(h=h, h1=h1, gate=gate, **mix))
        h = h2

    n_slots = {nm: weights[nm].shape[0] for nm in BIG}
    stacks = {nm: None for nm in BIG}

    class Group:
        def __init__(self, keys, grads):
            self.keys, self.stage = keys, 0
            self.g32 = [g.reshape(4, 2, w2d[nm].shape[1] // 2, w2d[nm].shape[2]) for (nm, _), g in zip(keys, grads)]

        def rider(self):
            if self.stage == 0:
                return _pair_rider(self.g32)
            if self.stage == 1:
                return _ici_rider(self.pair_sums)
            return _final_rider([stacks[nm] for nm, _ in self.keys], [j for _, j in self.keys])

        def jobs_after(self, landed):
            if self.stage == 0:
                self.from_sibling = landed
                return [_pair_sum_job(g, sb) for g, sb in zip(self.g32, landed)]
            if self.stage == 1:
                return [_final_sum_job(g, sb, ld, stacks[nm], j, n_slots[nm])
                        for (nm, j), g, sb, ld in zip(self.keys, self.g32, self.from_sibling, landed)]
            return []

        def advance(self, landed, summed):
            if self.stage == 0:
                self.pair_sums = summed
            else:
                for (nm, _), a in zip(self.keys, summed if self.stage == 1 else landed):
                    stacks[nm] = a
            self.stage += 1

    active = []
    batches = [0]

    def riders_now():
        parts = [g.rider() for g in active]
        return parts, _merge(parts)

    def advance_all(parts, landed):
        groups = list(active)
        pieces = _split(landed, parts)
        jobs = [g.jobs_after(l) for g, l in zip(groups, pieces)]
        flat = sum(jobs, [])
        outs = [o[0] for o in _run_jobs(flat, place, f"reduce_sums{batches[0]}")] if flat else []
        batches[0] += 1
        for g, l, jb in zip(groups, pieces, jobs):
            g.advance(l, outs[:len(jb)])
            outs = outs[len(jb):]
            if g.stage == 3:
                active.remove(g)

    d_mix_gain, d_ple_gain = [None] * DEPTH, [None] * DEPTH
    d_conv, d_scale = [None] * 2, [None] * 2
    for i in reversed(range(DEPTH)):
        j = i // 2
        sv = saved[i]
        parts, rider = riders_now()
        if i == DEPTH - 1:
            (dh1, d_ple_gain[i], dwg, dwp, loss_part, d_final), landed = _bwd_ple(
                h, sv["h1"], sv["gate"], p3, ple_gain[i], weight("ple_w_gate", i), weight("ple_w_proj", i), i, rider,
                loss_head=(loss_target.reshape(s, d), final_norm.reshape(1, d)))
        else:
            (dh1, d_ple_gain[i], dwg, dwp), landed = _bwd_ple(
                dh, sv["h1"], sv["gate"], p3, ple_gain[i], weight("ple_w_gate", i), weight("ple_w_proj", i), i, rider)
        advance_all(parts, landed)
        active.append(Group([("ple_w_gate", i), ("ple_w_proj", i)], [dwg, dwp]))
        parts, rider = riders_now()
        if i == 0:
            (dproj, d_conv[0], dwin, dwout), landed = _bwd_mix_a_weights(
                dh1, sv["h"], sv["proj"], mix_gain[0], conv_w[0], weight("a_w_out", 0), "bwd_mix_a0_weights", rider)
            advance_all(parts, landed)
            active.append(Group([("a_w_in", 0), ("a_w_out", 0)], [dwin, dwout]))
            parts, rider = riders_now()
            advance_all(parts, _run_rider(rider, "pair_exchange0"))
            parts, rider = riders_now()
            (dh, d_mix_gain[0]), landed = _bwd_mix_a_input(dproj, sv["h"], dh1, mix_gain[0], weight("a_w_in", 0),
                                                          "bwd_mix_a0_input", rider)
            advance_all(parts, landed)
            continue
        if i % 2 == 0:
            (dh, d_conv[j], d_mix_gain[i], dwin, dwout), landed = _bwd_mix_a(
                dh1, sv["h"], sv["proj"], mix_gain[i], conv_w[j], weight("a_w_in", j), weight("a_w_out", j),
                f"bwd_mix_a{j}", rider)
            new = Group([("a_w_in", j), ("a_w_out", j)], [dwin, dwout])
        else:
            (dh, d_scale[j], d_mix_gain[i], dwin, dwgrp, dwout), landed = _bwd_mix_b(
                dh1, sv["h"], sv["z"], sv["mx"], sv["diff"], mix_gain[i], scale_w[j], weight("b_w_in", j),
                weight("b_w_grp", j), weight("b_w_out", j), f"bwd_mix_b{j}", rider)
            new = Group([("b_w_in", j), ("b_w_grp", j), ("b_w_out", j)], [dwin, dwgrp, dwout])
        advance_all(parts, landed)
        active.append(new)
    grad_x = dh.reshape(1, s, d)

    pack = jnp.concatenate(
        d_mix_gain + d_ple_gain + [d_final] + [d_conv[0][0:3], d_conv[1][0:3]] + d_scale
        + [jnp.tile(loss_part[0:1], (1, d // 128)), jnp.zeros((SMALL_ROWS - 18, d), F32)], axis=0)
    vectors = _vector_rider(pack)
    tail = 0
    while active:
        parts, _ = riders_now()
        extra = [vectors] if tail == 0 else []
        landed = _run_rider(_merge(parts + extra), f"tail_exchange{tail}")
        if extra:
            total = _vector_sum(_split(landed, parts + extra)[-1][0])
        advance_all(parts, landed)
        tail += 1
    loss = total[17, 0]

    rep_rows = 16
    rep = lambda a, b_, c_: jnp.concatenate([a, b_, c_.reshape(1, d), jnp.zeros((rep_rows - 9, d), F32)], axis=0)
    mine_cols = lax.dynamic_slice_in_dim(total[9:17], chip * (e // 4), e // 4, axis=1)
    col = lambda a, b_: jnp.concatenate([a.reshape(6, e // 4), b_], axis=0)
    flat = {nm: (w2d[nm].shape[0] * w2d[nm].shape[1], w2d[nm].shape[2]) for nm in BIG}
    def job_of(nm, block_bytes):
        return _adamw_job(stacks[nm].reshape(flat[nm]), w2d[nm].reshape(flat[nm]), _as_2d(nm, moms[nm]).reshape(flat[nm]),
                          _as_2d(nm, vars_[nm]).reshape(flat[nm]), block_bytes)

    updates = {nm: _run_jobs([job_of(nm, ADAMW_BIG_BLOCK_BYTES)], place, f"adamw_{nm}")[0] for nm in ADAMW_ALONE}
    rest = [nm for nm in BIG if nm not in ADAMW_ALONE]
    jobs = [job_of(nm, ADAMW_BLOCK_BYTES) for nm in rest]
    jobs.append(_adamw_job(jnp.concatenate([total[0:9], jnp.zeros((rep_rows - 9, d), F32)], axis=0),
                           rep(norm_mix, ple_norm, final_norm), rep(m_norm_mix, m_ple_norm, m_final_norm),
                           rep(v_norm_mix, v_ple_norm, v_final_norm), ADAMW_BLOCK_BYTES))
    jobs.append(_adamw_job(mine_cols, col(a_w_conv, b_scale), col(m_a_w_conv, m_b_scale), col(v_a_w_conv, v_b_scale),
                           ADAMW_BLOCK_BYTES))
    batch = _run_jobs(jobs, place, "adamw_rest")
    updates.update(zip(rest, batch))
    out_grad, out_delta, out_m, out_v = {}, {}, {}, {}
    for nm in BIG:
        delta, new_m, new_v = updates[nm]
        shape = weights[nm].shape
        out_grad[nm], out_delta[nm] = stacks[nm].reshape(shape), delta.reshape(shape)
        out_m[nm], out_v[nm] = new_m.reshape(shape), new_v.reshape(shape)
    (rep_delta, rep_m, rep_v), (col_delta, col_m, col_v) = batch[-2:]

    def unpack(rep_a, col_a):
        return dict(norm_mix=rep_a[0:4], ple_norm=rep_a[4:8], final_norm=rep_a[8],
                    a_w_conv=col_a[0:6].reshape(2, 3, e // 4), b_scale=col_a[6:8])

    small_out = [unpack(total, mine_cols), unpack(rep_delta, col_delta), unpack(rep_m, col_m), unpack(rep_v, col_v)]
    order = ["norm_mix", "a_w_in", "a_w_conv", "a_w_out", "b_w_in", "b_w_grp", "b_scale", "b_w_out", "ple_norm",
             "ple_w_gate", "ple_w_proj", "final_norm"]
    outs = [loss, grad_x]
    for big, small_d in zip([out_grad, out_delta, out_m, out_v], small_out):
        outs += [big[nm] if nm in big else small_d[nm] for nm in order]
    return tuple(outs)
```

```python
import jax
import jax.numpy as jnp
from jax import lax
from jax.experimental import pallas as pl
from jax.experimental.pallas import tpu as pltpu

F32 = jnp.float32
BF16 = jnp.bfloat16
MESH = pl.DeviceIdType.MESH

D_MODEL = 1024
MIX_WIDTH = 1024
PLE_DIM = 256
N_GROUPS = 4
GROUP_DIM = 256
POOL_WINDOWS = (2, 4, 8, 16)
DEPTH = 4
EPS = 1e-6

ADAM_LR = 0.001
ADAM_B1 = 0.9
ADAM_B2 = 0.999
ADAM_EPS = 1e-08
ADAM_WD = 0.01
ADAM_STEP = 10

HALO = 8
TS_MIX = 256
TS_PLE = 512
VMEM_LIMIT = 56 * 1024 * 1024
PACK_GROUP = 8
JOB_BLOCK_BYTES = 1024 * 1024
ADAMW_BLOCK_BYTES = 512 * 1024
ADAMW_BIG_BLOCK_BYTES = 2 * 1024 * 1024
ADAMW_ALONE = ("a_w_in", "b_w_in", "ple_w_gate")
MIDDLE_STEPS_BEFORE_END = 1

ANY = pl.BlockSpec(memory_space=pl.ANY)


def _sds(shape, dtype):
    return jax.ShapeDtypeStruct(shape, dtype)


def _full(shape):
    nd = len(shape)
    return pl.BlockSpec(shape, lambda *_: (0,) * nd)


def _params(n_axes=1):
    return pltpu.CompilerParams(dimension_semantics=("arbitrary",) * n_axes, vmem_limit_bytes=VMEM_LIMIT)


def _dot(a, b):
    return jnp.dot(a, b, preferred_element_type=F32)


def _dot_nt(a, b):
    return lax.dot_general(a, b, (((1,), (1,)), ((), ())), preferred_element_type=F32)


def _dot_tn(a, b):
    return lax.dot_general(a, b, (((0,), (0,)), ((), ())), preferred_element_type=F32)


def _sigmoid(z):
    return 1.0 / (1.0 + jnp.exp(-z))


def _shift_down(x, k, tail):
    rolled = pltpu.roll(x, k, 0)
    rt = tail if k % HALO == 0 else pltpu.roll(tail, k % HALO, 0)
    row = lax.broadcasted_iota(jnp.int32, rt.shape, 0)
    head = jnp.where(row < k, rt, rolled[0:HALO])
    return jnp.concatenate([head, rolled[HALO:]], axis=0)


def _shift_up(x, k, head_next):
    n = x.shape[0]
    rolled = pltpu.roll(x, n - k, 0)
    rh = head_next if k % HALO == 0 else pltpu.roll(head_next, HALO - k % HALO, 0)
    row = lax.broadcasted_iota(jnp.int32, rh.shape, 0)
    tail = jnp.where(row >= HALO - k, rh, rolled[n - HALO:n])
    return jnp.concatenate([rolled[:n - HALO], tail], axis=0)


def _inv_counts(tile, ts):
    t = tile * ts + lax.broadcasted_iota(jnp.int32, (ts, 1), 0)
    return [1.0 / jnp.minimum(t + 1, w).astype(F32) for w in POOL_WINDOWS]


def _pool_fwd(u, carry, tile, ts):
    inv = _inv_counts(tile, ts)
    outs = []
    for g, w in enumerate(POOL_WINDOWS):
        cols = slice(g * GROUP_DIM, (g + 1) * GROUP_DIM)
        s = u[:, cols]
        level, k = 0, 1
        while k < w:
            tail = carry[level, :, cols]
            carry[level, :, cols] = s[ts - HALO:ts]
            s = s + _shift_down(s, k, tail)
            level, k = level + 1, k * 2
        outs.append(s * inv[g])
    return jnp.concatenate(outs, axis=1)


def _pool_bwd(dd, carry, tile, ts):
    inv = _inv_counts(tile, ts)
    outs = []
    for g, w in enumerate(POOL_WINDOWS):
        cols = slice(g * GROUP_DIM, (g + 1) * GROUP_DIM)
        q = dd[:, cols] * inv[g]
        level, k = 0, 1
        while k < w:
            head = carry[level, :, cols]
            carry[level, :, cols] = q[0:HALO]
            q = q + _shift_up(q, k, head)
            level, k = level + 1, k * 2
        outs.append(q)
    return jnp.concatenate(outs, axis=1)


def _copy_all(pairs, sems):
    copies = [pltpu.make_async_copy(src, dst, sems.at[n]) for n, (src, dst) in enumerate(pairs)]
    for cp in copies:
        cp.start()
    for cp in copies:
        cp.wait()


def _grp_pairs(wgrp_hbm, wgrp_v):
    rows = GROUP_DIM // 4
    return [(wgrp_hbm.at[k, g], wgrp_v.at[g, pl.ds(k * rows, rows), :]) for k in range(4) for g in range(N_GROUPS)]


def _rms(h):
    r = lax.rsqrt(jnp.mean(h * h, axis=-1, keepdims=True) + EPS)
    return h * r, r


def _rms_bwd(dhn, xn, r, gain):
    dgain = jnp.sum(dhn * xn, axis=0, keepdims=True)
    dxn = dhn * gain
    dh = r * (dxn - xn * jnp.mean(dxn * xn, axis=-1, keepdims=True))
    return dh, dgain


class _Rider:
    def __init__(self, inputs, out_shapes, n_sems, start, finish, middle=None, aliases=None):
        self.inputs, self.out_shapes, self.n_sems = list(inputs), list(out_shapes), n_sems
        self.start, self.middle, self.finish = start, middle, finish
        self.aliases = dict(aliases or {})


def _merge(riders):
    riders = [r for r in riders if r is not None]
    if not riders:
        return None
    if len(riders) == 1:
        return riders[0]

    def phase(which):
        def run(rin, rout, send, recv, base=0):
            i0 = o0 = s0 = 0
            for r in riders:
                fn = getattr(r, which)
                if fn is not None:
                    fn(rin[i0:i0 + len(r.inputs)], rout[o0:o0 + len(r.out_shapes)], send, recv, base + s0)
                i0, o0, s0 = i0 + len(r.inputs), o0 + len(r.out_shapes), s0 + r.n_sems
        return run

    aliases, i0, o0 = {}, 0, 0
    for r in riders:
        aliases.update({i0 + a: o0 + b for a, b in r.aliases.items()})
        i0, o0 = i0 + len(r.inputs), o0 + len(r.out_shapes)
    return _Rider(sum([r.inputs for r in riders], []), sum([r.out_shapes for r in riders], []),
                  sum(r.n_sems for r in riders), phase("start"), phase("finish"),
                  phase("middle") if any(r.middle for r in riders) else None, aliases)


def _split(landed, riders):
    out, o0 = [], 0
    for r in riders:
        if r is None:
            out.append(None)
        else:
            out.append(landed[o0:o0 + len(r.out_shapes)])
            o0 += len(r.out_shapes)
    return out


def _place():
    x, y, c = lax.axis_index("x"), lax.axis_index("y"), lax.axis_index("c")
    chips = [(1 - x, y), (x, 1 - y), (1 - x, 1 - y)]
    return x, y, c, chips


def _remote(src, dst, send_sems, recv_sems, sem, to):
    return pltpu.make_async_remote_copy(src_ref=src, dst_ref=dst, send_sem=send_sems.at[sem], recv_sem=recv_sems.at[sem],
                                        device_id=to, device_id_type=MESH)


def _gather_rider(stacked, slots):
    ni = len(stacked)

    def first_hops(rin, rout, send, recv, base, x, y, c, chips):
        me = 2 * x + y
        return [_remote(rin[t].at[slots[t], c], rout[t].at[me, c], send, recv, base + 7 * t + j, (cx, cy, c))
                for j, (cx, cy) in enumerate(chips) for t in range(ni)]

    def passes(rout, send, recv, base, x, y, c, chips):
        out = []
        for j, (cx, cy) in enumerate(chips):
            for t in range(ni):
                landed = rout[t].at[2 * cx + cy, c]
                out.append((_remote(landed, landed, send, recv, base + 7 * t + j, (x, y, 1 - c)),
                            _remote(landed, landed, send, recv, base + 7 * t + 3 + j, (x, y, 1 - c))))
        return out

    def own(rin, rout, send, recv, base, x, y, c):
        return [_remote(rin[t].at[slots[t]], rout[t].at[2 * x + y], send, recv, base + 7 * t + 6, (x, y, 1 - c))
                for t in range(ni)]

    def start(rin, rout, send, recv, base=0):
        x, y, c, chips = _place()
        for cp in first_hops(rin, rout, send, recv, base, x, y, c, chips) + own(rin, rout, send, recv, base, x, y, c):
            cp.start()

    def middle(rin, rout, send, recv, base=0):
        x, y, c, chips = _place()
        for arrival, onward in passes(rout, send, recv, base, x, y, c, chips):
            arrival.wait_recv()
            onward.start()

    def finish(rin, rout, send, recv, base=0):
        x, y, c, chips = _place()
        for j, (cx, cy) in enumerate(chips):
            for t in range(ni):
                other = rout[t].at[2 * cx + cy, 1 - c]
                _remote(other, other, send, recv, base + 7 * t + 3 + j, (x, y, 1 - c)).wait_recv()
        for cp in own(rin, rout, send, recv, base, x, y, c):
            cp.wait_recv()
            cp.wait_send()
        for cp in first_hops(rin, rout, send, recv, base, x, y, c, chips):
            cp.wait_send()
        for _, onward in passes(rout, send, recv, base, x, y, c, chips):
            onward.wait_send()

    return _Rider(stacked, [_sds((4,) + a.shape[1:], a.dtype) for a in stacked], 7 * ni, start, finish, middle)


def _pair_rider(grads):
    ni = len(grads)

    def copies(rin, rout, send, recv, base):
        x, y, c, _ = _place()
        return [_remote(rin[t].at[:, 1 - c], rout[t], send, recv, base + t, (x, y, 1 - c)) for t in range(ni)]

    def start(rin, rout, send, recv, base=0):
        for cp in copies(rin, rout, send, recv, base):
            cp.start()

    def finish(rin, rout, send, recv, base=0):
        for cp in copies(rin, rout, send, recv, base):
            cp.wait()

    return _Rider(grads, [_sds(g.shape[:1] + g.shape[2:], g.dtype) for g in grads], ni, start, finish)


def _ici_rider(pair_sums):
    ni = len(pair_sums)

    def copies(rin, rout, send, recv, base):
        x, y, c, chips = _place()
        return [_remote(rin[t].at[j], rout[t].at[j], send, recv, base + 3 * t + j, (cx, cy, c))
                for j, (cx, cy) in enumerate(chips) for t in range(ni)]

    def start(rin, rout, send, recv, base=0):
        for cp in copies(rin, rout, send, recv, base):
            cp.start()

    def finish(rin, rout, send, recv, base=0):
        for cp in copies(rin, rout, send, recv, base):
            cp.wait()

    return _Rider(pair_sums, [_sds((3,) + g.shape[1:], g.dtype) for g in pair_sums], 3 * ni, start, finish)


def _final_rider(summed, slots):
    ni = len(summed)

    def copies(rout, send, recv, base):
        x, y, c, _ = _place()
        return [(_remote(rout[t].at[slots[t], c], rout[t].at[slots[t], c], send, recv, base + t, (x, y, 1 - c)),
                 _remote(rout[t].at[slots[t], 1 - c], rout[t].at[slots[t], 1 - c], send, recv, base + t, (x, y, 1 - c)))
                for t in range(ni)]

    def start(rin, rout, send, recv, base=0):
        for mine, _ in copies(rout, send, recv, base):
            mine.start()

    def finish(rin, rout, send, recv, base=0):
        for mine, theirs in copies(rout, send, recv, base):
            mine.wait_send()
            theirs.wait_recv()

    return _Rider(summed, [_sds(a.shape, a.dtype) for a in summed], ni, start, finish,
                  aliases={t: t for t in range(ni)})


class _Block:
    def __init__(self, array, index):
        self.array, self.index = array, index

    def spec(self):
        index = self.index
        return pl.BlockSpec((None,) + self.array.shape[1:], lambda *_: (index, 0, 0))


def _call(body, *, name, grid, in_specs, out_specs, out_shape, scratch_shapes, operands, rider=None):
    operands, in_specs = list(operands), list(in_specs)
    for n, op in enumerate(operands):
        if isinstance(op, _Block):
            operands[n], in_specs[n] = op.array, op.spec()
    if rider is None:
        outs = pl.pallas_call(body, name=name, grid=grid, in_specs=in_specs, out_specs=out_specs, out_shape=out_shape,
                              scratch_shapes=scratch_shapes, compiler_params=_params(len(grid)))(*operands)
        return list(outs), []
    n_in, n_out, n_scr = len(in_specs), len(out_specs), len(scratch_shapes)
    r_in, r_out = len(rider.inputs), len(rider.out_shapes)
    steps = 1
    for g in grid:
        steps *= g
    mid = max(steps - 1 - MIDDLE_STEPS_BEFORE_END, 0)

    def full_body(*refs):
        own_in, rin = refs[:n_in], refs[n_in:n_in + r_in]
        own_out = refs[n_in + r_in:n_in + r_in + n_out]
        rout = refs[n_in + r_in + n_out:n_in + r_in + n_out + r_out]
        own_scr = refs[n_in + r_in + n_out + r_out:n_in + r_in + n_out + r_out + n_scr]
        send, recv = refs[-2], refs[-1]
        step = pl.program_id(0)
        for axis in range(1, len(grid)):
            step = step * grid[axis] + pl.program_id(axis)

        @pl.when(step == 0)
        def _():
            rider.start(rin, rout, send, recv)

        body(*own_in, *own_out, *own_scr)

        if rider.middle is not None:
            @pl.when(step == mid)
            def _():
                rider.middle(rin, rout, send, recv)

        @pl.when(step == steps - 1)
        def _():
            rider.finish(rin, rout, send, recv)

    outs = pl.pallas_call(
        full_body, name=name, grid=grid,
        in_specs=list(in_specs) + [ANY] * r_in, out_specs=list(out_specs) + [ANY] * r_out,
        out_shape=list(out_shape) + rider.out_shapes,
        scratch_shapes=list(scratch_shapes) + [pltpu.SemaphoreType.DMA((rider.n_sems,)), pltpu.SemaphoreType.DMA((rider.n_sems,))],
        input_output_aliases={n_in + a: n_out + b for a, b in rider.aliases.items()},
        compiler_params=_params(len(grid)),
    )(*operands, *rider.inputs)
    return list(outs[:n_out]), list(outs[n_out:])


def _run_rider(rider, name):
    r_in, r_out = len(rider.inputs), len(rider.out_shapes)

    def body(*refs):
        rin, rout, send, recv = refs[:r_in], refs[r_in:r_in + r_out], refs[-2], refs[-1]
        rider.start(rin, rout, send, recv)
        if rider.middle is not None:
            rider.middle(rin, rout, send, recv)
        rider.finish(rin, rout, send, recv)

    outs = pl.pallas_call(
        body, name=name, in_specs=[ANY] * r_in, out_specs=[ANY] * r_out, out_shape=rider.out_shapes,
        scratch_shapes=[pltpu.SemaphoreType.DMA((rider.n_sems,)), pltpu.SemaphoreType.DMA((rider.n_sems,))],
        input_output_aliases=rider.aliases,
    )(*rider.inputs)
    return list(outs)


def _ple_tile(h1, p_ref, gain_ref, wg_v, wp_v):
    xn, _ = _rms(h1)
    hpb = (xn * gain_ref[...]).astype(BF16)
    gate = _sigmoid(_dot(hpb, wg_v[...]))
    pb = p_ref[...].astype(BF16)
    pe = jnp.concatenate([_dot(pb, wp_v[k]) for k in range(4)], axis=1)
    return h1 + gate * pe, gate


def _ple_parts(ple, ts, d):
    p, layer, gain, w_gate, w_proj = ple
    s, pd = p.shape[1:]
    row = pl.BlockSpec((ts, d), lambda i: (i, 0))
    return dict(
        operands=[p, gain, w_gate, w_proj],
        in_specs=[pl.BlockSpec((None, ts, pd), lambda i: (layer, i, 0)), _full((1, d)), ANY, ANY],
        out_specs=[row, row], out_shape=[_sds((s, d), F32), _sds((s, d), BF16)],
        scratch=[pltpu.VMEM((d, d), BF16), pltpu.VMEM((4, pd, d // 4), BF16)])


def _fwd_mix_a(h, gain, conv_w, w_in, w_out, name, rider=None, ple=None):
    s, d = h.shape
    e = MIX_WIDTH
    ts = min(TS_MIX, s)
    nt = s // ts
    extra = _ple_parts(ple, ts, d) if ple else None

    def body(*refs):
        h_ref, gain_ref, cw_ref, win_hbm, wout_hbm = refs[:5]
        n_in = 9 if ple else 5
        h1_ref, proj_ref = refs[n_in:n_in + 2]
        win_v, wout_v, carry, sems = refs[n_in + (4 if ple else 2):][:4]
        i = pl.program_id(0)

        @pl.when(i == 0)
        def _():
            loads = [(win_hbm, win_v), (wout_hbm, wout_v)]
            if ple:
                loads += [(refs[7], refs[-2]), (refs[8], refs[-1])]
            _copy_all(loads, sems)
            carry[...] = jnp.zeros_like(carry)

        hh = h_ref[...]
        xn, _ = _rms(hh)
        hnb = (xn * gain_ref[...]).astype(BF16)
        b = _dot(hnb, win_v[0])
        c = _dot(hnb, win_v[1])
        v = _dot(hnb, win_v[2])
        z = _dot(hnb, win_v[3])
        proj_ref[:, 0 * e:1 * e] = b.astype(BF16)
        proj_ref[:, 1 * e:2 * e] = c.astype(BF16)
        proj_ref[:, 2 * e:3 * e] = v.astype(BF16)
        proj_ref[:, 3 * e:4 * e] = z.astype(BF16)
        cv = c * v
        tail = carry[...]
        carry[...] = cv[ts - HALO:ts]
        conv = cw_ref[0:1, :] * _shift_down(cv, 2, tail) + cw_ref[1:2, :] * _shift_down(cv, 1, tail) + cw_ref[2:3, :] * cv
        mb = ((z * _sigmoid(z)) * (b * conv)).astype(BF16)
        h1 = hh + _dot(mb, wout_v[...])
        h1_ref[...] = h1
        if ple:
            h2, gate = _ple_tile(h1, refs[5], refs[6], refs[-2], refs[-1])
            refs[n_in + 2][...] = h2
            refs[n_in + 3][...] = gate.astype(BF16)

    row = lambda width: pl.BlockSpec((ts, width), lambda i: (i, 0))
    return _call(
        body, name=name, grid=(nt,),
        in_specs=[row(d), _full((1, d)), _full((8, e)), ANY, ANY] + (extra["in_specs"] if ple else []),
        out_specs=[row(d), row(4 * e)] + (extra["out_specs"] if ple else []),
        out_shape=[_sds((s, d), F32), _sds((s, 4 * e), BF16)] + (extra["out_shape"] if ple else []),
        scratch_shapes=[pltpu.VMEM((4, d, e), BF16), pltpu.VMEM((e, d), BF16), pltpu.VMEM((HALO, e), F32),
                        pltpu.SemaphoreType.DMA((4,))] + (extra["scratch"] if ple else []),
        operands=[h, gain, conv_w, w_in, w_out] + (extra["operands"] if ple else []), rider=rider)


def _fwd_mix_b(h, gain, scale, w_in, w_grp, w_out, name, rider=None, ple=None):
    s, d = h.shape
    e = MIX_WIDTH
    ts = min(TS_MIX, s)
    nt = s // ts
    extra = _ple_parts(ple, ts, d) if ple else None

    def body(*refs):
        h_ref, gain_ref, scale_ref, win_hbm, wgrp_hbm, wout_hbm = refs[:6]
        n_in = 10 if ple else 6
        h1_ref, z_ref, mx_ref, dd_ref = refs[n_in:n_in + 4]
        win_v, wgrp_v, wout_v, carry, sems = refs[n_in + (6 if ple else 4):][:5]
        i = pl.program_id(0)

        @pl.when(i == 0)
        def _():
            loads = [(win_hbm, win_v), (wout_hbm, wout_v)] + _grp_pairs(wgrp_hbm, wgrp_v)
            if ple:
                loads += [(refs[8], refs[-2]), (refs[9], refs[-1])]
            _copy_all(loads, sems)
            carry[...] = jnp.zeros_like(carry)

        hh = h_ref[...]
        xn, _ = _rms(hh)
        hnb = (xn * gain_ref[...]).astype(BF16)
        u = jnp.concatenate([_dot(hnb, win_v[0]), _dot(hnb, win_v[1])], axis=1)
        z = jnp.concatenate([_dot(hnb, win_v[2]), _dot(hnb, win_v[3])], axis=1)
        z_ref[...] = z.astype(BF16)
        diff = (_pool_fwd(u, carry, i, ts) - u).astype(BF16)
        dd_ref[...] = diff
        mx = jnp.concatenate(
            [_dot(diff[:, g * GROUP_DIM:(g + 1) * GROUP_DIM], wgrp_v[g]) for g in range(N_GROUPS)], axis=1)
        mx_ref[...] = mx.astype(BF16)
        mb = ((z * _sigmoid(z)) * (mx * scale_ref[...])).astype(BF16)
        h1 = hh + _dot(mb, wout_v[...])
        h1_ref[...] = h1
        if ple:
            h2, gate = _ple_tile(h1, refs[6], refs[7], refs[-2], refs[-1])
            refs[n_in + 4][...] = h2
            refs[n_in + 5][...] = gate.astype(BF16)

    row = lambda width: pl.BlockSpec((ts, width), lambda i: (i, 0))
    return _call(
        body, name=name, grid=(nt,),
        in_specs=[row(d), _full((1, d)), _full((1, e)), ANY, ANY, ANY] + (extra["in_specs"] if ple else []),
        out_specs=[row(d), row(e), row(e), row(e)] + (extra["out_specs"] if ple else []),
        out_shape=[_sds((s, d), F32)] + [_sds((s, e), BF16)] * 3 + (extra["out_shape"] if ple else []),
        scratch_shapes=[pltpu.VMEM((4, d, e // 2), BF16), pltpu.VMEM((N_GROUPS, GROUP_DIM, GROUP_DIM), BF16),
                        pltpu.VMEM((e, d), BF16), pltpu.VMEM((4, HALO, e), F32), pltpu.SemaphoreType.DMA((20,))]
        + (extra["scratch"] if ple else []),
        operands=[h, gain, scale, w_in, w_grp, w_out] + (extra["operands"] if ple else []), rider=rider)


def _fwd_ple(h1, p, gain, w_gate, w_proj, layer, rider=None):
    s, d = h1.shape
    pd = p.shape[-1]
    ts = min(TS_PLE, s)
    nt = s // ts

    def body(h1_ref, p_ref, gain_ref, wg_hbm, wp_hbm, h2_ref, gate_ref, wg_v, wp_v, sems):
        @pl.when(pl.program_id(0) == 0)
        def _():
            _copy_all([(wg_hbm, wg_v), (wp_hbm, wp_v)], sems)

        hh = h1_ref[...]
        xn, _ = _rms(hh)
        hpb = (xn * gain_ref[...]).astype(BF16)
        gate = _sigmoid(_dot(hpb, wg_v[...]))
        pb = p_ref[...].astype(BF16)
        pe = jnp.concatenate([_dot(pb, wp_v[k]) for k in range(4)], axis=1)
        gate_ref[...] = gate.astype(BF16)
        h2_ref[...] = hh + gate * pe

    row = lambda width: pl.BlockSpec((ts, width), lambda i: (i, 0))
    return _call(
        body, name=f"fwd_ple{layer}", grid=(nt,),
        in_specs=[row(d), pl.BlockSpec((None, ts, pd), lambda i: (layer, i, 0)), _full((1, d)), ANY, ANY],
        out_specs=[row(d), row(d)],
        out_shape=[_sds((s, d), F32), _sds((s, d), BF16)],
        scratch_shapes=[pltpu.VMEM((d, d), BF16), pltpu.VMEM((4, pd, d // 4), BF16), pltpu.SemaphoreType.DMA((2,))],
        operands=[h1, p, gain, w_gate, w_proj], rider=rider)


def _bwd_ple(dh2, h1, gate, p, gain, w_gate, w_proj, layer, rider=None, loss_head=None):
    s, d = dh2.shape
    pd = p.shape[-1]
    ts = min(TS_PLE, s)
    nt = s // ts
    qd = d // 4
    n_head = 0 if loss_head is None else 2

    def body(*refs):
        dh2_ref = refs[0]
        h1_ref, gate_ref, p_ref, gain_ref, wg_hbm, wp_hbm, dh1_ref, dgain_ref, dwg_hbm, dwp_hbm = refs[1 + n_head:11 + n_head]
        wg_v, wp_v, acc_g, acc_p, sems = refs[-5:]
        i = pl.program_id(0)

        @pl.when(i == 0)
        def _():
            _copy_all([(wg_hbm, wg_v), (wp_hbm, wp_v)], sems)
            dgain_ref[...] = jnp.zeros_like(dgain_ref)
            acc_g[...] = jnp.zeros_like(acc_g)
            acc_p[...] = jnp.zeros_like(acc_p)

        if loss_head is None:
            g2 = dh2_ref[...]
        else:
            t_ref, fgain_ref, loss_ref, dfgain_ref = refs[1], refs[2], refs[11 + n_head], refs[12 + n_head]

            @pl.when(i == 0)
            def _():
                loss_ref[...] = jnp.zeros_like(loss_ref)
                dfgain_ref[...] = jnp.zeros_like(dfgain_ref)

            xf, rf = _rms(dh2_ref[...])
            err = xf * fgain_ref[...] - t_ref[...]
            part = 0.5 * jnp.sum(jnp.mean(err * err, axis=-1, keepdims=True), axis=0, keepdims=True)
            loss_ref[...] += jnp.broadcast_to(part, loss_ref.shape)
            g2, dfgain = _rms_bwd(err * (1.0 / d), xf, rf, fgain_ref[...])
            dfgain_ref[...] += dfgain
        gate_f = gate_ref[...].astype(F32)
        xn, r = _rms(h1_ref[...])
        hpb = (xn * gain_ref[...]).astype(BF16)
        pb = p_ref[...].astype(BF16)
        pe = jnp.concatenate([_dot(pb, wp_v[k]) for k in range(4)], axis=1)
        dpeb = (g2 * gate_f).astype(BF16)
        dab = ((g2 * pe) * (gate_f * (1.0 - gate_f))).astype(BF16)
        acc_g[...] += _dot_tn(hpb, dab)
        for k in range(4):
            acc_p[k] += _dot_tn(pb, dpeb[:, k * qd:(k + 1) * qd])
        dhp = _dot_nt(dab, wg_v[...])
        dh, dgain = _rms_bwd(dhp, xn, r, gain_ref[...])
        dh1_ref[...] = g2 + dh
        dgain_ref[...] += dgain

        @pl.when(i == nt - 1)
        def _():
            _copy_all([(acc_g, dwg_hbm), (acc_p, dwp_hbm)], sems)

    row = pl.BlockSpec((ts, d), lambda i: (i, 0))
    head = loss_head is not None
    return _call(
        body, name=f"bwd_ple{layer}", grid=(nt,),
        in_specs=[row] + ([row, _full((1, d))] if head else [])
        + [row, row, pl.BlockSpec((None, ts, pd), lambda i: (layer, i, 0)), _full((1, d)), ANY, ANY],
        out_specs=[row, _full((1, d)), ANY, ANY] + ([_full((8, 128)), _full((1, d))] if head else []),
        out_shape=[_sds((s, d), F32), _sds((1, d), F32), _sds((d, d), F32), _sds((4, pd, qd), F32)]
        + ([_sds((8, 128), F32), _sds((1, d), F32)] if head else []),
        scratch_shapes=[pltpu.VMEM((d, d), BF16), pltpu.VMEM((4, pd, qd), BF16), pltpu.VMEM((d, d), F32),
                        pltpu.VMEM((4, pd, qd), F32), pltpu.SemaphoreType.DMA((2,))],
        operands=[dh2] + (list(loss_head) if head else []) + [h1, gate, p, gain, w_gate, w_proj], rider=rider)


def _mix_a_tile_grads(proj_ref, ch_ref, vh_ref, cw_ref, dh1b, wout_v, carry, dcw_ref, tile, hb):
    e = MIX_WIDTH
    b = proj_ref[:, 0 * e:1 * e].astype(F32)
    c = proj_ref[:, 1 * e:2 * e].astype(F32)
    v = proj_ref[:, 2 * e:3 * e].astype(F32)
    z = proj_ref[:, 3 * e:4 * e].astype(F32)
    cv = c * v
    prev = (ch_ref[...].astype(F32) * vh_ref[...].astype(F32))[hb - HALO:hb]
    tail = jnp.where(tile > 0, prev, jnp.zeros_like(prev))
    cv1 = _shift_down(cv, 1, tail)
    cv2 = _shift_down(cv, 2, tail)
    conv = cw_ref[0:1, :] * cv2 + cw_ref[1:2, :] * cv1 + cw_ref[2:3, :] * cv
    sig = _sigmoid(z)
    sz = z * sig
    y = b * conv
    dm = _dot_nt(dh1b, wout_v[...])
    dz = (dm * y) * (sig * (1.0 + z * (1.0 - sig)))
    dy = dm * sz
    db = dy * conv
    dconv = dy * b
    head = carry[...]
    carry[...] = dconv[0:HALO]
    dcv = cw_ref[2:3, :] * dconv + cw_ref[1:2, :] * _shift_up(dconv, 1, head) + cw_ref[0:1, :] * _shift_up(dconv, 2, head)
    dcw_ref[0:1, :] += jnp.sum(dconv * cv2, axis=0, keepdims=True)
    dcw_ref[1:2, :] += jnp.sum(dconv * cv1, axis=0, keepdims=True)
    dcw_ref[2:3, :] += jnp.sum(dconv * cv, axis=0, keepdims=True)
    parts = [db.astype(BF16), (dcv * v).astype(BF16), (dcv * c).astype(BF16), dz.astype(BF16)]
    return parts, (sz * y).astype(BF16)


def _bwd_mix_a(dh1, h, proj, gain, conv_w, w_in, w_out, name, rider=None):
    s, d = dh1.shape
    e = MIX_WIDTH
    ts = min(TS_MIX, s)
    nt = s // ts
    hb = 16
    per = ts // hb

    def body(dh1_ref, h_ref, proj_ref, ch_ref, vh_ref, gain_ref, cw_ref, win_hbm, wout_hbm,
             dh_ref, dcw_ref, dgain_ref, dwin_hbm, dwout_hbm, win_v, wout_v, acc_in, acc_out, carry, sems):
        i = pl.program_id(0)

        @pl.when(i == 0)
        def _():
            _copy_all([(win_hbm, win_v), (wout_hbm, wout_v)], sems)
            carry[...] = jnp.zeros_like(carry)
            dcw_ref[...] = jnp.zeros_like(dcw_ref)
            dgain_ref[...] = jnp.zeros_like(dgain_ref)
            acc_in[...] = jnp.zeros_like(acc_in)
            acc_out[...] = jnp.zeros_like(acc_out)

        dh1 = dh1_ref[...]
        dh1b = dh1.astype(BF16)
        parts, mb = _mix_a_tile_grads(proj_ref, ch_ref, vh_ref, cw_ref, dh1b, wout_v, carry, dcw_ref, nt - 1 - i, hb)
        acc_out[...] += _dot_tn(mb, dh1b)
        xn, r = _rms(h_ref[...])
        hnb = (xn * gain_ref[...]).astype(BF16)
        for q in range(4):
            acc_in[q] += _dot_tn(hnb, parts[q])
        dhn = _dot_nt(parts[0], win_v[0]) + _dot_nt(parts[1], win_v[1]) + _dot_nt(parts[2], win_v[2]) + _dot_nt(parts[3], win_v[3])
        dh, dgain = _rms_bwd(dhn, xn, r, gain_ref[...])
        dh_ref[...] = dh1 + dh
        dgain_ref[...] += dgain

        @pl.when(i == nt - 1)
        def _():
            _copy_all([(acc_in, dwin_hbm), (acc_out, dwout_hbm)], sems)

    row = lambda width: pl.BlockSpec((ts, width), lambda i: (nt - 1 - i, 0))
    halo = lambda col: pl.BlockSpec((hb, e), lambda i: (jnp.maximum((nt - 1 - i) * per - 1, 0), col))
    return _call(
        body, name=name, grid=(nt,),
        in_specs=[row(d), row(d), row(4 * e), halo(1), halo(2), _full((1, d)), _full((8, e)), ANY, ANY],
        out_specs=[row(d), _full((8, e)), _full((1, d)), ANY, ANY],
        out_shape=[_sds((s, d), F32), _sds((8, e), F32), _sds((1, d), F32), _sds((4, d, e), F32), _sds((e, d), F32)],
        scratch_shapes=[pltpu.VMEM((4, d, e), BF16), pltpu.VMEM((e, d), BF16), pltpu.VMEM((4, d, e), F32),
                        pltpu.VMEM((e, d), F32), pltpu.VMEM((HALO, e), F32), pltpu.SemaphoreType.DMA((2,))],
        operands=[dh1, h, proj, proj, proj, gain, conv_w, w_in, w_out], rider=rider)


def _bwd_mix_a_weights(dh1, h, proj, gain, conv_w, w_out, name, rider=None):
    s, d = dh1.shape
    e = MIX_WIDTH
    ts = min(TS_MIX, s)
    nt = s // ts
    hb = 16
    per = ts // hb

    def body(dh1_ref, h_ref, proj_ref, ch_ref, vh_ref, gain_ref, cw_ref, wout_hbm,
             dproj_ref, dcw_ref, dwin_hbm, dwout_hbm, wout_v, acc_in, acc_out, carry, sems):
        i = pl.program_id(0)

        @pl.when(i == 0)
        def _():
            _copy_all([(wout_hbm, wout_v)], sems)
            carry[...] = jnp.zeros_like(carry)
            dcw_ref[...] = jnp.zeros_like(dcw_ref)
            acc_in[...] = jnp.zeros_like(acc_in)
            acc_out[...] = jnp.zeros_like(acc_out)

        dh1b = dh1_ref[...].astype(BF16)
        parts, mb = _mix_a_tile_grads(proj_ref, ch_ref, vh_ref, cw_ref, dh1b, wout_v, carry, dcw_ref, nt - 1 - i, hb)
        acc_out[...] += _dot_tn(mb, dh1b)
        xn, _ = _rms(h_ref[...])
        hnb = (xn * gain_ref[...]).astype(BF16)
        for q in range(4):
            acc_in[q] += _dot_tn(hnb, parts[q])
            dproj_ref[:, q * e:(q + 1) * e] = parts[q]

        @pl.when(i == nt - 1)
        def _():
            _copy_all([(acc_in, dwin_hbm), (acc_out, dwout_hbm)], sems)

    row = lambda width: pl.BlockSpec((ts, width), lambda i: (nt - 1 - i, 0))
    halo = lambda col: pl.BlockSpec((hb, e), lambda i: (jnp.maximum((nt - 1 - i) * per - 1, 0), col))
    return _call(
        body, name=name, grid=(nt,),
        in_specs=[row(d), row(d), row(4 * e), halo(1), halo(2), _full((1, d)), _full((8, e)), ANY],
        out_specs=[row(4 * e), _full((8, e)), ANY, ANY],
        out_shape=[_sds((s, 4 * e), BF16), _sds((8, e), F32), _sds((4, d, e), F32), _sds((e, d), F32)],
        scratch_shapes=[pltpu.VMEM((e, d), BF16), pltpu.VMEM((4, d, e), F32), pltpu.VMEM((e, d), F32),
                        pltpu.VMEM((HALO, e), F32), pltpu.SemaphoreType.DMA((2,))],
        operands=[dh1, h, proj, proj, proj, gain, conv_w, w_out], rider=rider)


def _bwd_mix_a_input(dproj, h, dh1, gain, w_in, name, rider=None):
    s, d = dh1.shape
    e = MIX_WIDTH
    ts = min(TS_PLE, s)
    nt = s // ts

    def body(dproj_ref, h_ref, dh1_ref, gain_ref, win_hbm, dh_ref, dgain_ref, win_v, sems):
        @pl.when(pl.program_id(0) == 0)
        def _():
            _copy_all([(win_hbm, win_v)], sems)
            dgain_ref[...] = jnp.zeros_like(dgain_ref)

        dhn = _dot_nt(dproj_ref[:, 0:e], win_v[0])
        for q in range(1, 4):
            dhn = dhn + _dot_nt(dproj_ref[:, q * e:(q + 1) * e], win_v[q])
        xn, r = _rms(h_ref[...])
        dh, dgain = _rms_bwd(dhn, xn, r, gain_ref[...])
        dh_ref[...] = dh1_ref[...] + dh
        dgain_ref[...] += dgain

    row = lambda width: pl.BlockSpec((ts, width), lambda i: (i, 0))
    return _call(
        body, name=name, grid=(nt,),
        in_specs=[row(4 * e), row(d), row(d), _full((1, d)), ANY],
        out_specs=[row(d), _full((1, d))],
        out_shape=[_sds((s, d), F32), _sds((1, d), F32)],
        scratch_shapes=[pltpu.VMEM((4, d, e), BF16), pltpu.SemaphoreType.DMA((1,))],
        operands=[dproj, h, dh1, gain, w_in], rider=rider)


def _bwd_mix_b(dh1, h, z, mx, diff, gain, scale, w_in, w_grp, w_out, name, rider=None):
    s, d = dh1.shape
    e = MIX_WIDTH
    ts = min(TS_MIX, s)
    nt = s // ts
    half = e // 2

    def body(dh1_ref, h_ref, z_ref, mx_ref, dd_ref, gain_ref, scale_ref, win_hbm, wgrp_hbm, wout_hbm,
             dh_ref, dscale_ref, dgain_ref, dwin_hbm, dwgrp_hbm, dwout_hbm,
             win_v, wgrp_v, wout_v, acc_in, acc_grp, acc_out, carry, sems):
        i = pl.program_id(0)
        tile = nt - 1 - i

        @pl.when(i == 0)
        def _():
            _copy_all([(win_hbm, win_v), (wout_hbm, wout_v)] + _grp_pairs(wgrp_hbm, wgrp_v), sems)
            carry[...] = jnp.zeros_like(carry)
            dscale_ref[...] = jnp.zeros_like(dscale_ref)
            dgain_ref[...] = jnp.zeros_like(dgain_ref)
            acc_in[...] = jnp.zeros_like(acc_in)
            acc_grp[...] = jnp.zeros_like(acc_grp)
            acc_out[...] = jnp.zeros_like(acc_out)

        zf = z_ref[...].astype(F32)
        mxf = mx_ref[...].astype(F32)
        sig = _sigmoid(zf)
        sz = zf * sig
        mixed = mxf * scale_ref[...]
        dh1 = dh1_ref[...]
        dh1b = dh1.astype(BF16)
        acc_out[...] += _dot_tn((sz * mixed).astype(BF16), dh1b)
        dm = _dot_nt(dh1b, wout_v[...])
        dz = (dm * mixed) * (sig * (1.0 + zf * (1.0 - sig)))
        dmixed = dm * sz
        dscale_ref[...] += jnp.sum(dmixed * mxf, axis=0, keepdims=True)
        dmxb = (dmixed * scale_ref[...]).astype(BF16)
        diff = dd_ref[...]
        for g in range(N_GROUPS):
            cols = slice(g * GROUP_DIM, (g + 1) * GROUP_DIM)
            acc_grp[g] += _dot_tn(diff[:, cols], dmxb[:, cols])
        ddiff = jnp.concatenate(
            [_dot_nt(dmxb[:, g * GROUP_DIM:(g + 1) * GROUP_DIM], wgrp_v[g]) for g in range(N_GROUPS)], axis=1)
        dub = (_pool_bwd(ddiff, carry, tile, ts) - ddiff).astype(BF16)
        dzb = dz.astype(BF16)
        parts = [dub[:, 0:half], dub[:, half:e], dzb[:, 0:half], dzb[:, half:e]]
        xn, r = _rms(h_ref[...])
        hnb = (xn * gain_ref[...]).astype(BF16)
        for k in range(4):
            acc_in[k] += _dot_tn(hnb, parts[k])
        dhn = _dot_nt(parts[0], win_v[0]) + _dot_nt(parts[1], win_v[1]) + _dot_nt(parts[2], win_v[2]) + _dot_nt(parts[3], win_v[3])
        dh, dgain = _rms_bwd(dhn, xn, r, gain_ref[...])
        dh_ref[...] = dh1 + dh
        dgain_ref[...] += dgain

        @pl.when(i == nt - 1)
        def _():
            _copy_all([(acc_in, dwin_hbm), (acc_out, dwout_hbm)] + [(v, hb_) for hb_, v in _grp_pairs(dwgrp_hbm, acc_grp)], sems)

    row = lambda width: pl.BlockSpec((ts, width), lambda i: (nt - 1 - i, 0))
    return _call(
        body, name=name, grid=(nt,),
        in_specs=[row(d), row(d), row(e), row(e), row(e), _full((1, d)), _full((1, e)), ANY, ANY, ANY],
        out_specs=[row(d), _full((1, e)), _full((1, d)), ANY, ANY, ANY],
        out_shape=[_sds((s, d), F32), _sds((1, e), F32), _sds((1, d), F32), _sds((4, d, half), F32),
                   _sds((4, N_GROUPS, GROUP_DIM // 4, GROUP_DIM), F32), _sds((e, d), F32)],
        scratch_shapes=[pltpu.VMEM((4, d, half), BF16), pltpu.VMEM((N_GROUPS, GROUP_DIM, GROUP_DIM), BF16),
                        pltpu.VMEM((e, d), BF16), pltpu.VMEM((4, d, half), F32),
                        pltpu.VMEM((N_GROUPS, GROUP_DIM, GROUP_DIM), F32), pltpu.VMEM((e, d), F32),
                        pltpu.VMEM((4, HALO, e), F32), pltpu.SemaphoreType.DMA((18,))],
        operands=[dh1, h, z, mx, diff, gain, scale, w_in, w_grp, w_out], rider=rider)


def _first_gather(rider, small):
    shards = rider.inputs
    ni = len(shards)

    def body(*refs):
        rin, small_src = refs[:ni], refs[ni]
        rout, small_dst = refs[ni + 1:2 * ni + 1], refs[2 * ni + 1]
        send, recv, ssend, srecv = refs[2 * ni + 2:]
        x, y, c, chips = _place()
        me = 2 * x + y
        peers = [(cx, cy, c) for cx, cy in chips] + [(x, y, 1 - c)]
        vec = [_remote(small_src, small_dst.at[me], ssend, srecv, j, to) for j, to in enumerate(peers)]
        for cp in vec:
            cp.start()
        rider.start(rin, rout, send, recv)
        rider.middle(rin, rout, send, recv)
        rider.finish(rin, rout, send, recv)
        for j, (px, py, _) in enumerate(peers):
            _remote(small_src, small_dst.at[2 * px + py], ssend, srecv, j, peers[j]).wait_recv()
        for cp in vec:
            cp.wait_send()

    outs = pl.pallas_call(
        body, name="first_gather", in_specs=[ANY] * (ni + 1), out_specs=[ANY] * (ni + 1),
        out_shape=rider.out_shapes + [_sds((4,) + small.shape, small.dtype)],
        scratch_shapes=[pltpu.SemaphoreType.DMA((rider.n_sems,)), pltpu.SemaphoreType.DMA((rider.n_sems,)),
                        pltpu.SemaphoreType.DMA((4,)), pltpu.SemaphoreType.DMA((4,))],
    )(*shards, small)
    return list(outs[:ni]), outs[ni]


def _vector_rider(pack):
    flips = [(fx, fy, fc) for fx in (0, 1) for fy in (0, 1) for fc in (0, 1)][1:]

    def copies(rin, rout, send, recv, base):
        x, y, c, _ = _place()
        me = 4 * x + 2 * y + c
        peers = [(1 - x if fx else x, 1 - y if fy else y, 1 - c if fc else c) for fx, fy, fc in flips]
        own = pltpu.make_async_copy(rin[0], rout[0].at[me], send.at[base + 7])
        out = [_remote(rin[0], rout[0].at[me], send, recv, base + r, peer) for r, peer in enumerate(peers)]
        back = [_remote(rin[0], rout[0].at[4 * px + 2 * py + pc], send, recv, base + r, (px, py, pc))
                for r, (px, py, pc) in enumerate(peers)]
        return own, out, back

    def start(rin, rout, send, recv, base=0):
        own, out, _ = copies(rin, rout, send, recv, base)
        own.start()
        for cp in out:
            cp.start()

    def finish(rin, rout, send, recv, base=0):
        own, out, back = copies(rin, rout, send, recv, base)
        for cp in back:
            cp.wait_recv()
        for cp in out:
            cp.wait_send()
        own.wait()

    return _Rider([pack], [_sds((8,) + pack.shape, pack.dtype)], 8, start, finish)


def _vector_sum(landed, row_counts):
    _, rows, d = landed.shape

    def body(l_ref, *out_refs):
        total = l_ref[0]
        for dev in range(1, 8):
            total = total + l_ref[dev]
        for t, out_ref in enumerate(out_refs):
            out_ref[...] = total[t * PACK_GROUP:t * PACK_GROUP + row_counts[t]]

    vmem = pl.BlockSpec(memory_space=pltpu.VMEM)
    return pl.pallas_call(body, name="vector_sum", in_specs=[vmem], out_specs=[vmem] * len(row_counts),
                          out_shape=[_sds((n, d), F32) for n in row_counts])(landed)


def _job_rows(rows, cols):
    return min(rows, max(8, JOB_BLOCK_BYTES // (4 * cols)))


def _pair_sum_job(grad, sibling_rows):
    _, _, rh, cols = grad.shape
    tr = _job_rows(rh, cols)
    nr = rh // tr

    def chip_of(j, pos):
        return jnp.bitwise_xor(pos[0], jnp.where(j == 2, 3, 2 - j))

    return dict(
        ins=[(grad, (None, None, tr, cols), lambda l, pos: (chip_of(l // nr, pos), pos[1], l % nr, 0)),
             (sibling_rows, (None, tr, cols), lambda l, pos: (chip_of(l // nr, pos), l % nr, 0))],
        outs=[((3, rh, cols), BF16, (None, tr, cols), lambda l, pos: (l // nr, l % nr, 0))],
        steps=3 * nr, fn=lambda g, sb: [(g + sb).astype(BF16)], alias=None)


def _final_sum_job(grad, sibling_rows, landed, stack, slot, n_slots):
    _, _, rh, cols = grad.shape
    tr = _job_rows(rh, cols)

    def fn(g, sb, ld):
        total = g + sb
        for j in range(3):
            total = total + ld[j].astype(F32)
        return [total]

    return dict(
        ins=[(grad, (None, None, tr, cols), lambda l, pos: (pos[0], pos[1], l, 0)),
             (sibling_rows, (None, tr, cols), lambda l, pos: (pos[0], l, 0)),
             (landed, (3, tr, cols), lambda l, pos: (0, l, 0))],
        outs=[((n_slots, 2, rh, cols), F32, (None, None, tr, cols), lambda l, pos: (slot, pos[1], l, 0))],
        steps=rh // tr, fn=fn, alias=stack)


def _adamw_job(g, w, m, v, block_bytes):
    rows, cols = g.shape
    tr = min(rows, max(8, block_bytes // (4 * cols)))

    def fn(gg, ww, mm, vv):
        nm = ADAM_B1 * mm + (1.0 - ADAM_B1) * gg
        nv = ADAM_B2 * vv + (1.0 - ADAM_B2) * (gg * gg)
        m_hat = nm / (1.0 - ADAM_B1 ** ADAM_STEP)
        v_hat = nv / (1.0 - ADAM_B2 ** ADAM_STEP)
        return [-ADAM_LR * (m_hat / (jnp.sqrt(v_hat) + ADAM_EPS) + ADAM_WD * ww), nm, nv, gg]

    block = lambda l, pos: (l, 0)
    return dict(ins=[(a, (tr, cols), block) for a in (g, w, m, v)],
                outs=[((rows, cols), F32, (tr, cols), block)] * 4, steps=rows // tr, fn=fn, alias=None)


def _run_jobs(jobs, place, name):
    starts, total = [], 0
    for jb in jobs:
        starts.append(total)
        total += jb["steps"]

    def clamped(fn, start, steps):
        return lambda s, pos: fn(jnp.clip(s - start, 0, steps - 1), pos)

    in_specs, operands = [], [place]
    for jb, start in zip(jobs, starts):
        for arr, block, fn in jb["ins"]:
            in_specs.append(pl.BlockSpec(block, clamped(fn, start, jb["steps"])))
            operands.append(arr)
    n_ins = len(in_specs)
    first_out, n_outs = [], 0
    for jb in jobs:
        first_out.append(n_outs)
        n_outs += len(jb["outs"])
    aliases = {}
    for t, jb in enumerate(jobs):
        if jb["alias"] is not None:
            in_specs.append(ANY)
            operands.append(jb["alias"])
            aliases[len(operands) - 1] = first_out[t]
    out_specs = [pl.BlockSpec(block, clamped(fn, start, jb["steps"]))
                 for jb, start in zip(jobs, starts) for _, _, block, fn in jb["outs"]]

    def body(place_ref, *refs):
        in_refs, out_refs = refs[:n_ins], refs[len(in_specs):]
        s = pl.program_id(0)
        first = 0
        for t, (jb, start) in enumerate(zip(jobs, starts)):
            mine = in_refs[first:first + len(jb["ins"])]
            first += len(jb["ins"])

            @pl.when((s >= start) & (s < start + jb["steps"]))
            def _(mine=mine, t=t, jb=jb):
                values = jb["fn"](*[r[...] for r in mine])
                for n, value in enumerate(values):
                    out_refs[first_out[t] + n][...] = value

    grid_spec = pltpu.PrefetchScalarGridSpec(num_scalar_prefetch=1, grid=(total,), in_specs=in_specs, out_specs=out_specs)
    outs = pl.pallas_call(body, name=name, grid_spec=grid_spec,
                          out_shape=[_sds(shape, dtype) for jb in jobs for shape, dtype, _, _ in jb["outs"]],
                          input_output_aliases=aliases, compiler_params=_params(1))(*operands)
    return [list(outs[first_out[t]:first_out[t] + len(jb["outs"])]) for t, jb in enumerate(jobs)]


BIG = ["a_w_in", "a_w_out", "b_w_in", "b_w_grp", "b_w_out", "ple_w_gate", "ple_w_proj"]

GATHER_PLAN = {
    "first": [("a_w_in", 0), ("a_w_out", 0)],
    "mix0": [("ple_w_gate", 0), ("ple_w_proj", 0), ("b_w_in", 0), ("b_w_grp", 0), ("b_w_out", 0)],
    "ple0": [("ple_w_gate", 1), ("ple_w_proj", 1)],
    "mix1": [("a_w_in", 1), ("a_w_out", 1), ("ple_w_gate", 2), ("ple_w_proj", 2)],
    "mix2": [("b_w_in", 1), ("b_w_grp", 1), ("b_w_out", 1), ("ple_w_gate", 3), ("ple_w_proj", 3)],
}


def _as_2d(name, a):
    if name == "b_w_grp":
        return a.reshape(a.shape[0], N_GROUPS * (GROUP_DIM // 4), GROUP_DIM)
    return a


def kernel(x, p, norm_mix, a_w_in, a_w_conv, a_w_out, b_w_in, b_w_grp, b_scale, b_w_out, ple_norm, ple_w_gate, ple_w_proj, final_norm, loss_target, m_norm_mix, m_a_w_in, m_a_w_conv, m_a_w_out, m_b_w_in, m_b_w_grp, m_b_scale, m_b_w_out, m_ple_norm, m_ple_w_gate, m_ple_w_proj, m_final_norm, v_norm_mix, v_a_w_in, v_a_w_conv, v_a_w_out, v_b_w_in, v_b_w_grp, v_b_scale, v_b_w_out, v_ple_norm, v_ple_w_gate, v_ple_w_proj, v_final_norm):
    d, e = D_MODEL, MIX_WIDTH
    s = x.shape[1]
    cx, cy, cc = lax.axis_index("x"), lax.axis_index("y"), lax.axis_index("c")
    chip = 2 * cx + cy
    place = jnp.stack([chip, cc]).astype(jnp.int32)

    weights = dict(a_w_in=a_w_in, a_w_out=a_w_out, b_w_in=b_w_in, b_w_grp=b_w_grp, b_w_out=b_w_out,
                   ple_w_gate=ple_w_gate, ple_w_proj=ple_w_proj)
    moms = dict(a_w_in=m_a_w_in, a_w_out=m_a_w_out, b_w_in=m_b_w_in, b_w_grp=m_b_w_grp, b_w_out=m_b_w_out,
                ple_w_gate=m_ple_w_gate, ple_w_proj=m_ple_w_proj)
    vars_ = dict(a_w_in=v_a_w_in, a_w_out=v_a_w_out, b_w_in=v_b_w_in, b_w_grp=v_b_w_grp, b_w_out=v_b_w_out,
                 ple_w_gate=v_ple_w_gate, ple_w_proj=v_ple_w_proj)
    w2d = {nm: _as_2d(nm, weights[nm]) for nm in BIG}
    bf = {nm: w2d[nm].astype(BF16).reshape(w2d[nm].shape[0], 2, w2d[nm].shape[1] // 2, w2d[nm].shape[2]) for nm in BIG}
    gathered = {}

    def gather_rider(host):
        keys = GATHER_PLAN.get(host)
        return _gather_rider([bf[nm] for nm, _ in keys], [j for _, j in keys]) if keys else None

    def keep(host, landed):
        for k, a in zip(GATHER_PLAN.get(host, []), landed):
            gathered[k] = a

    def weight(nm, j):
        a = gathered[(nm, j)]
        shapes = {"a_w_in": (4, d, e), "a_w_out": (e, d), "b_w_in": (4, d, e // 2),
                  "b_w_grp": (4, N_GROUPS, GROUP_DIM // 4, GROUP_DIM), "b_w_out": (e, d), "ple_w_gate": (d, d),
                  "ple_w_proj": (4, PLE_DIM, d // 4)}
        return a.reshape(shapes[nm])

    pad = jnp.zeros((4, e // 4), F32)
    small = jnp.concatenate([a_w_conv[0], b_scale[0:1], pad, a_w_conv[1], b_scale[1:2], pad], axis=0)
    landed, small_full = _first_gather(gather_rider("first"), small)
    keep("first", landed)
    small_full = small_full.transpose(1, 0, 2).reshape(16, e)
    conv_w = [_Block(small_full.reshape(2, 8, e), j) for j in range(2)]
    scale_w = [_Block(small_full.reshape(16, 1, e), 8 * j + 3) for j in range(2)]

    p3 = p.reshape(DEPTH, s, PLE_DIM)
    mix_gain = [_Block(norm_mix.reshape(DEPTH, 1, d), i) for i in range(DEPTH)]
    ple_gain = [_Block(ple_norm.reshape(DEPTH, 1, d), i) for i in range(DEPTH)]

    h = x.reshape(s, d)
    saved = []
    for i in range(DEPTH):
        j = i // 2
        rider = gather_rider(f"mix{i}")
        ple = (p3, i, ple_gain[i], weight("ple_w_gate", i), weight("ple_w_proj", i)) if i > 0 else None
        if i % 2 == 0:
            outs, landed = _fwd_mix_a(h, mix_gain[i], conv_w[j], weight("a_w_in", j), weight("a_w_out", j),
                                      f"fwd_mix_a{j}", rider, ple)
            mix = dict(proj=outs[1])
        else:
            outs, landed = _fwd_mix_b(h, mix_gain[i], scale_w[j], weight("b_w_in", j), weight("b_w_grp", j),
                                      weight("b_w_out", j), f"fwd_mix_b{j}", rider, ple)
            mix = dict(z=outs[1], mx=outs[2], diff=outs[3])
        keep(f"mix{i}", landed)
        h1 = outs[0]
        if ple:
            h2, gate = outs[-2:]
        else:
            (h2, gate), landed = _fwd_ple(h1, p3, ple_gain[i], weight("ple_w_gate", i), weight("ple_w_proj", i), i,
                                          gather_rider(f"ple{i}"))
            keep(f"ple{i}", landed)
        saved.append(dict(h=h, h1=h1, gate=gate, **mix))
        h = h2

    n_slots = {nm: weights[nm].shape[0] for nm in BIG}
    stacks = {nm: None for nm in BIG}

    class Group:
        def __init__(self, keys, grads):
            self.keys, self.stage = keys, 0
            self.g32 = [g.reshape(4, 2, w2d[nm].shape[1] // 2, w2d[nm].shape[2]) for (nm, _), g in zip(keys, grads)]

        def rider(self):
            if self.stage == 0:
                return _pair_rider(self.g32)
            if self.stage == 1:
                return _ici_rider(self.pair_sums)
            return _final_rider([stacks[nm] for nm, _ in self.keys], [j for _, j in self.keys])

        def jobs_after(self, landed):
            if self.stage == 0:
                self.from_sibling = landed
                return [_pair_sum_job(g, sb) for g, sb in zip(self.g32, landed)]
            if self.stage == 1:
                return [_final_sum_job(g, sb, ld, stacks[nm], j, n_slots[nm])
                        for (nm, j), g, sb, ld in zip(self.keys, self.g32, self.from_sibling, landed)]
            return []

        def advance(self, landed, summed):
            if self.stage == 0:
                self.pair_sums = summed
            else:
                for (nm, _), a in zip(self.keys, summed if self.stage == 1 else landed):
                    stacks[nm] = a
            self.stage += 1

    active = []
    batches = [0]

    def riders_now():
        parts = [g.rider() for g in active]
        return parts, _merge(parts)

    def advance_all(parts, landed):
        groups = list(active)
        pieces = _split(landed, parts)
        jobs = [g.jobs_after(l) for g, l in zip(groups, pieces)]
        flat = sum(jobs, [])
        outs = [o[0] for o in _run_jobs(flat, place, f"reduce_sums{batches[0]}")] if flat else []
        batches[0] += 1
        for g, l, jb in zip(groups, pieces, jobs):
            g.advance(l, outs[:len(jb)])
            outs = outs[len(jb):]
            if g.stage == 3:
                active.remove(g)

    d_mix_gain, d_ple_gain = [None] * DEPTH, [None] * DEPTH
    d_conv, d_scale = [None] * 2, [None] * 2
    for i in reversed(range(DEPTH)):
        j = i // 2
        sv = saved[i]
        parts, rider = riders_now()
        if i == DEPTH - 1:
            (dh1, d_ple_gain[i], dwg, dwp, loss_part, d_final), landed = _bwd_ple(
                h, sv["h1"], sv["gate"], p3, ple_gain[i], weight("ple_w_gate", i), weight("ple_w_proj", i), i, rider,
                loss_head=(loss_target.reshape(s, d), final_norm.reshape(1, d)))
        else:
            (dh1, d_ple_gain[i], dwg, dwp), landed = _bwd_ple(
                dh, sv["h1"], sv["gate"], p3, ple_gain[i], weight("ple_w_gate", i), weight("ple_w_proj", i), i, rider)
        advance_all(parts, landed)
        active.append(Group([("ple_w_gate", i), ("ple_w_proj", i)], [dwg, dwp]))
        parts, rider = riders_now()
        if i == 0:
            (dproj, d_conv[0], dwin, dwout), landed = _bwd_mix_a_weights(
                dh1, sv["h"], sv["proj"], mix_gain[0], conv_w[0], weight("a_w_out", 0), "bwd_mix_a0_weights", rider)
            advance_all(parts, landed)
            active.append(Group([("a_w_in", 0), ("a_w_out", 0)], [dwin, dwout]))
            parts, rider = riders_now()
            advance_all(parts, _run_rider(rider, "pair_exchange0"))
            parts, rider = riders_now()
            (dh, d_mix_gain[0]), landed = _bwd_mix_a_input(dproj, sv["h"], dh1, mix_gain[0], weight("a_w_in", 0),
                                                          "bwd_mix_a0_input", rider)
            advance_all(parts, landed)
            continue
        if i % 2 == 0:
            (dh, d_conv[j], d_mix_gain[i], dwin, dwout), landed = _bwd_mix_a(
                dh1, sv["h"], sv["proj"], mix_gain[i], conv_w[j], weight("a_w_in", j), weight("a_w_out", j),
                f"bwd_mix_a{j}", rider)
            new = Group([("a_w_in", j), ("a_w_out", j)], [dwin, dwout])
        else:
            (dh, d_scale[j], d_mix_gain[i], dwin, dwgrp, dwout), landed = _bwd_mix_b(
                dh1, sv["h"], sv["z"], sv["mx"], sv["diff"], mix_gain[i], scale_w[j], weight("b_w_in", j),
                weight("b_w_grp", j), weight("b_w_out", j), f"bwd_mix_b{j}", rider)
            new = Group([("b_w_in", j), ("b_w_grp", j), ("b_w_out", j)], [dwin, dwgrp, dwout])
        advance_all(parts, landed)
        active.append(new)
    grad_x = dh.reshape(1, s, d)

    def padded(pieces):
        n = sum(a.shape[0] for a in pieces)
        return pieces + [jnp.zeros((PACK_GROUP - n, d), F32)]

    pack = jnp.concatenate(
        padded(d_mix_gain) + padded(d_ple_gain) + padded([d_final]) + padded([d_conv[0][0:3], d_conv[1][0:3]])
        + padded(d_scale) + padded([jnp.tile(loss_part[0:1], (1, d // 128))]), axis=0)
    vectors = _vector_rider(pack)
    tail = 0
    while active:
        parts, _ = riders_now()
        extra = [vectors] if tail == 0 else []
        landed = _run_rider(_merge(parts + extra), f"tail_exchange{tail}")
        if extra:
            g_mix, g_ple, g_final, g_conv, g_scale, loss_row = _vector_sum(
                _split(landed, parts + extra)[-1][0], [DEPTH, DEPTH, 1, 6, 2, 1])
        advance_all(parts, landed)
        tail += 1
    loss = loss_row[0, 0]

    mine = lambda a: lax.dynamic_slice_in_dim(a, chip * (e // 4), e // 4, axis=1)
    row = lambda a: a.reshape(1, d)
    taps = lambda a: a.reshape(6, e // 4)
    flat = {nm: (w2d[nm].shape[0] * w2d[nm].shape[1], w2d[nm].shape[2]) for nm in BIG}
    tensors = {nm: (stacks[nm].reshape(flat[nm]), w2d[nm].reshape(flat[nm]), _as_2d(nm, moms[nm]).reshape(flat[nm]),
                    _as_2d(nm, vars_[nm]).reshape(flat[nm])) for nm in BIG}
    tensors.update(
        norm_mix=(g_mix, norm_mix, m_norm_mix, v_norm_mix), ple_norm=(g_ple, ple_norm, m_ple_norm, v_ple_norm),
        final_norm=(g_final, row(final_norm), row(m_final_norm), row(v_final_norm)),
        a_w_conv=(mine(g_conv), taps(a_w_conv), taps(m_a_w_conv), taps(v_a_w_conv)),
        b_scale=(mine(g_scale), b_scale, m_b_scale, v_b_scale))
    order = ["norm_mix", "a_w_in", "a_w_conv", "a_w_out", "b_w_in", "b_w_grp", "b_scale", "b_w_out", "ple_norm",
             "ple_w_gate", "ple_w_proj", "final_norm"]
    shapes = dict(norm_mix=norm_mix.shape, ple_norm=ple_norm.shape, final_norm=final_norm.shape,
                  a_w_conv=a_w_conv.shape, b_scale=b_scale.shape, **{nm: weights[nm].shape for nm in BIG})
    updates = {nm: _run_jobs([_adamw_job(*tensors[nm], ADAMW_BIG_BLOCK_BYTES)], place, f"adamw_{nm}")[0]
               for nm in ADAMW_ALONE}
    rest = [nm for nm in order if nm not in ADAMW_ALONE]
    updates.update(zip(rest, _run_jobs([_adamw_job(*tensors[nm], ADAMW_BLOCK_BYTES) for nm in rest], place, "adamw_rest")))
    outs = [loss, grad_x]
    for which in (3, 0, 1, 2):
        outs += [updates[nm][which].reshape(shapes[nm]) for nm in order]
    return tuple(outs)
```

```python
import jax
import jax.numpy as jnp
from jax import lax
from jax.experimental import pallas as pl
from jax.experimental.pallas import tpu as pltpu

F32 = jnp.float32
BF16 = jnp.bfloat16
MESH = pl.DeviceIdType.MESH

D_MODEL = 1024
MIX_WIDTH = 1024
PLE_DIM = 256
N_GROUPS = 4
GROUP_DIM = 256
POOL_WINDOWS = (2, 4, 8, 16)
DEPTH = 4
EPS = 1e-6

ADAM_LR = 0.001
ADAM_B1 = 0.9
ADAM_B2 = 0.999
ADAM_EPS = 1e-08
ADAM_WD = 0.01
ADAM_STEP = 10

HALO = 8
TS_MIX = 256
TS_FWD = 512
TS_PLE = 512
VMEM_LIMIT = 56 * 1024 * 1024
PACK_GROUP = 8
JOB_BLOCK_BYTES = 1024 * 1024
ADAMW_BLOCK_BYTES = 512 * 1024
ADAMW_BIG_BLOCK_BYTES = 2 * 1024 * 1024
ADAMW_ALONE = ("a_w_in", "b_w_in", "ple_w_gate")
MIDDLE_STEPS_BEFORE_END = 1

ANY = pl.BlockSpec(memory_space=pl.ANY)


def _sds(shape, dtype):
    return jax.ShapeDtypeStruct(shape, dtype)


def _full(shape):
    nd = len(shape)
    return pl.BlockSpec(shape, lambda *_: (0,) * nd)


def _params(n_axes=1):
    return pltpu.CompilerParams(dimension_semantics=("arbitrary",) * n_axes, vmem_limit_bytes=VMEM_LIMIT)


def _dot(a, b):
    return jnp.dot(a, b, preferred_element_type=F32)


def _dot_nt(a, b):
    return lax.dot_general(a, b, (((1,), (1,)), ((), ())), preferred_element_type=F32)


def _dot_tn(a, b):
    return lax.dot_general(a, b, (((0,), (0,)), ((), ())), preferred_element_type=F32)


def _sigmoid(z):
    return 1.0 / (1.0 + jnp.exp(-z))


def _shift_down(x, k, tail):
    rolled = pltpu.roll(x, k, 0)
    rt = tail if k % HALO == 0 else pltpu.roll(tail, k % HALO, 0)
    row = lax.broadcasted_iota(jnp.int32, rt.shape, 0)
    head = jnp.where(row < k, rt, rolled[0:HALO])
    return jnp.concatenate([head, rolled[HALO:]], axis=0)


def _shift_up(x, k, head_next):
    n = x.shape[0]
    rolled = pltpu.roll(x, n - k, 0)
    rh = head_next if k % HALO == 0 else pltpu.roll(head_next, HALO - k % HALO, 0)
    row = lax.broadcasted_iota(jnp.int32, rh.shape, 0)
    tail = jnp.where(row >= HALO - k, rh, rolled[n - HALO:n])
    return jnp.concatenate([rolled[:n - HALO], tail], axis=0)


def _inv_counts(tile, ts):
    t = tile * ts + lax.broadcasted_iota(jnp.int32, (ts, 1), 0)
    return [1.0 / jnp.minimum(t + 1, w).astype(F32) for w in POOL_WINDOWS]


def _pool_fwd(u, carry, tile, ts):
    inv = _inv_counts(tile, ts)
    outs = []
    for g, w in enumerate(POOL_WINDOWS):
        cols = slice(g * GROUP_DIM, (g + 1) * GROUP_DIM)
        s = u[:, cols]
        level, k = 0, 1
        while k < w:
            tail = carry[level, :, cols]
            carry[level, :, cols] = s[ts - HALO:ts]
            s = s + _shift_down(s, k, tail)
            level, k = level + 1, k * 2
        outs.append(s * inv[g])
    return jnp.concatenate(outs, axis=1)


def _pool_bwd(dd, carry, tile, ts):
    inv = _inv_counts(tile, ts)
    outs = []
    for g, w in enumerate(POOL_WINDOWS):
        cols = slice(g * GROUP_DIM, (g + 1) * GROUP_DIM)
        q = dd[:, cols] * inv[g]
        level, k = 0, 1
        while k < w:
            head = carry[level, :, cols]
            carry[level, :, cols] = q[0:HALO]
            q = q + _shift_up(q, k, head)
            level, k = level + 1, k * 2
        outs.append(q)
    return jnp.concatenate(outs, axis=1)


def _copy_all(pairs, sems):
    copies = [pltpu.make_async_copy(src, dst, sems.at[n]) for n, (src, dst) in enumerate(pairs)]
    for cp in copies:
        cp.start()
    for cp in copies:
        cp.wait()


def _grp_pairs(wgrp_hbm, wgrp_v):
    rows = GROUP_DIM // 4
    return [(wgrp_hbm.at[k, g], wgrp_v.at[g, pl.ds(k * rows, rows), :]) for k in range(4) for g in range(N_GROUPS)]


def _rms(h):
    r = lax.rsqrt(jnp.mean(h * h, axis=-1, keepdims=True) + EPS)
    return h * r, r


def _rms_bwd(dhn, xn, r, gain):
    dgain = jnp.sum(dhn * xn, axis=0, keepdims=True)
    dxn = dhn * gain
    dh = r * (dxn - xn * jnp.mean(dxn * xn, axis=-1, keepdims=True))
    return dh, dgain


class _Rider:
    def __init__(self, inputs, out_shapes, n_sems, start, finish, middle=None, aliases=None):
        self.inputs, self.out_shapes, self.n_sems = list(inputs), list(out_shapes), n_sems
        self.start, self.middle, self.finish = start, middle, finish
        self.aliases = dict(aliases or {})


def _merge(riders):
    riders = [r for r in riders if r is not None]
    if not riders:
        return None
    if len(riders) == 1:
        return riders[0]

    def phase(which):
        def run(rin, rout, send, recv, base=0):
            i0 = o0 = s0 = 0
            for r in riders:
                fn = getattr(r, which)
                if fn is not None:
                    fn(rin[i0:i0 + len(r.inputs)], rout[o0:o0 + len(r.out_shapes)], send, recv, base + s0)
                i0, o0, s0 = i0 + len(r.inputs), o0 + len(r.out_shapes), s0 + r.n_sems
        return run

    aliases, i0, o0 = {}, 0, 0
    for r in riders:
        aliases.update({i0 + a: o0 + b for a, b in r.aliases.items()})
        i0, o0 = i0 + len(r.inputs), o0 + len(r.out_shapes)
    return _Rider(sum([r.inputs for r in riders], []), sum([r.out_shapes for r in riders], []),
                  sum(r.n_sems for r in riders), phase("start"), phase("finish"),
                  phase("middle") if any(r.middle for r in riders) else None, aliases)


def _split(landed, riders):
    out, o0 = [], 0
    for r in riders:
        if r is None:
            out.append(None)
        else:
            out.append(landed[o0:o0 + len(r.out_shapes)])
            o0 += len(r.out_shapes)
    return out


def _place():
    x, y, c = lax.axis_index("x"), lax.axis_index("y"), lax.axis_index("c")
    chips = [(1 - x, y), (x, 1 - y), (1 - x, 1 - y)]
    return x, y, c, chips


def _remote(src, dst, send_sems, recv_sems, sem, to):
    return pltpu.make_async_remote_copy(src_ref=src, dst_ref=dst, send_sem=send_sems.at[sem], recv_sem=recv_sems.at[sem],
                                        device_id=to, device_id_type=MESH)


def _gather_rider(stacked, slots):
    ni = len(stacked)

    def first_hops(rin, rout, send, recv, base, x, y, c, chips):
        me = 2 * x + y
        return [_remote(rin[t].at[slots[t], c], rout[t].at[me, c], send, recv, base + 7 * t + j, (cx, cy, c))
                for j, (cx, cy) in enumerate(chips) for t in range(ni)]

    def passes(rout, send, recv, base, x, y, c, chips):
        out = []
        for j, (cx, cy) in enumerate(chips):
            for t in range(ni):
                landed = rout[t].at[2 * cx + cy, c]
                out.append((_remote(landed, landed, send, recv, base + 7 * t + j, (x, y, 1 - c)),
                            _remote(landed, landed, send, recv, base + 7 * t + 3 + j, (x, y, 1 - c))))
        return out

    def own(rin, rout, send, recv, base, x, y, c):
        return [_remote(rin[t].at[slots[t]], rout[t].at[2 * x + y], send, recv, base + 7 * t + 6, (x, y, 1 - c))
                for t in range(ni)]

    def start(rin, rout, send, recv, base=0):
        x, y, c, chips = _place()
        for cp in first_hops(rin, rout, send, recv, base, x, y, c, chips) + own(rin, rout, send, recv, base, x, y, c):
            cp.start()

    def middle(rin, rout, send, recv, base=0):
        x, y, c, chips = _place()
        for arrival, onward in passes(rout, send, recv, base, x, y, c, chips):
            arrival.wait_recv()
            onward.start()

    def finish(rin, rout, send, recv, base=0):
        x, y, c, chips = _place()
        for j, (cx, cy) in enumerate(chips):
            for t in range(ni):
                other = rout[t].at[2 * cx + cy, 1 - c]
                _remote(other, other, send, recv, base + 7 * t + 3 + j, (x, y, 1 - c)).wait_recv()
        for cp in own(rin, rout, send, recv, base, x, y, c):
            cp.wait_recv()
            cp.wait_send()
        for cp in first_hops(rin, rout, send, recv, base, x, y, c, chips):
            cp.wait_send()
        for _, onward in passes(rout, send, recv, base, x, y, c, chips):
            onward.wait_send()

    return _Rider(stacked, [_sds((4,) + a.shape[1:], a.dtype) for a in stacked], 7 * ni, start, finish, middle)


def _pair_rider(grads):
    ni = len(grads)

    def copies(rin, rout, send, recv, base):
        x, y, c, _ = _place()
        return [_remote(rin[t].at[:, 1 - c], rout[t], send, recv, base + t, (x, y, 1 - c)) for t in range(ni)]

    def start(rin, rout, send, recv, base=0):
        for cp in copies(rin, rout, send, recv, base):
            cp.start()

    def finish(rin, rout, send, recv, base=0):
        for cp in copies(rin, rout, send, recv, base):
            cp.wait()

    return _Rider(grads, [_sds(g.shape[:1] + g.shape[2:], g.dtype) for g in grads], ni, start, finish)


def _ici_rider(pair_sums):
    ni = len(pair_sums)

    def copies(rin, rout, send, recv, base):
        x, y, c, chips = _place()
        return [_remote(rin[t].at[j], rout[t].at[j], send, recv, base + 3 * t + j, (cx, cy, c))
                for j, (cx, cy) in enumerate(chips) for t in range(ni)]

    def start(rin, rout, send, recv, base=0):
        for cp in copies(rin, rout, send, recv, base):
            cp.start()

    def finish(rin, rout, send, recv, base=0):
        for cp in copies(rin, rout, send, recv, base):
            cp.wait()

    return _Rider(pair_sums, [_sds((3,) + g.shape[1:], g.dtype) for g in pair_sums], 3 * ni, start, finish)


def _final_rider(summed, slots):
    ni = len(summed)

    def copies(rout, send, recv, base):
        x, y, c, _ = _place()
        return [(_remote(rout[t].at[slots[t], c], rout[t].at[slots[t], c], send, recv, base + t, (x, y, 1 - c)),
                 _remote(rout[t].at[slots[t], 1 - c], rout[t].at[slots[t], 1 - c], send, recv, base + t, (x, y, 1 - c)))
                for t in range(ni)]

    def start(rin, rout, send, recv, base=0):
        for mine, _ in copies(rout, send, recv, base):
            mine.start()

    def finish(rin, rout, send, recv, base=0):
        for mine, theirs in copies(rout, send, recv, base):
            mine.wait_send()
            theirs.wait_recv()

    return _Rider(summed, [_sds(a.shape, a.dtype) for a in summed], ni, start, finish,
                  aliases={t: t for t in range(ni)})


class _Block:
    def __init__(self, array, index):
        self.array, self.index = array, index

    def spec(self):
        index = self.index
        return pl.BlockSpec((None,) + self.array.shape[1:], lambda *_: (index, 0, 0))


def _call(body, *, name, grid, in_specs, out_specs, out_shape, scratch_shapes, operands, rider=None):
    operands, in_specs = list(operands), list(in_specs)
    for n, op in enumerate(operands):
        if isinstance(op, _Block):
            operands[n], in_specs[n] = op.array, op.spec()
    if rider is None:
        outs = pl.pallas_call(body, name=name, grid=grid, in_specs=in_specs, out_specs=out_specs, out_shape=out_shape,
                              scratch_shapes=scratch_shapes, compiler_params=_params(len(grid)))(*operands)
        return list(outs), []
    n_in, n_out, n_scr = len(in_specs), len(out_specs), len(scratch_shapes)
    r_in, r_out = len(rider.inputs), len(rider.out_shapes)
    steps = 1
    for g in grid:
        steps *= g
    mid = max(steps - 1 - MIDDLE_STEPS_BEFORE_END, 0)

    def full_body(*refs):
        own_in, rin = refs[:n_in], refs[n_in:n_in + r_in]
        own_out = refs[n_in + r_in:n_in + r_in + n_out]
        rout = refs[n_in + r_in + n_out:n_in + r_in + n_out + r_out]
        own_scr = refs[n_in + r_in + n_out + r_out:n_in + r_in + n_out + r_out + n_scr]
        send, recv = refs[-2], refs[-1]
        step = pl.program_id(0)
        for axis in range(1, len(grid)):
            step = step * grid[axis] + pl.program_id(axis)

        @pl.when(step == 0)
        def _():
            rider.start(rin, rout, send, recv)

        body(*own_in, *own_out, *own_scr)

        if rider.middle is not None:
            @pl.when(step == mid)
            def _():
                rider.middle(rin, rout, send, recv)

        @pl.when(step == steps - 1)
        def _():
            rider.finish(rin, rout, send, recv)

    outs = pl.pallas_call(
        full_body, name=name, grid=grid,
        in_specs=list(in_specs) + [ANY] * r_in, out_specs=list(out_specs) + [ANY] * r_out,
        out_shape=list(out_shape) + rider.out_shapes,
        scratch_shapes=list(scratch_shapes) + [pltpu.SemaphoreType.DMA((rider.n_sems,)), pltpu.SemaphoreType.DMA((rider.n_sems,))],
        input_output_aliases={n_in + a: n_out + b for a, b in rider.aliases.items()},
        compiler_params=_params(len(grid)),
    )(*operands, *rider.inputs)
    return list(outs[:n_out]), list(outs[n_out:])


def _run_rider(rider, name):
    r_in, r_out = len(rider.inputs), len(rider.out_shapes)

    def body(*refs):
        rin, rout, send, recv = refs[:r_in], refs[r_in:r_in + r_out], refs[-2], refs[-1]
        rider.start(rin, rout, send, recv)
        if rider.middle is not None:
            rider.middle(rin, rout, send, recv)
        rider.finish(rin, rout, send, recv)

    outs = pl.pallas_call(
        body, name=name, in_specs=[ANY] * r_in, out_specs=[ANY] * r_out, out_shape=rider.out_shapes,
        scratch_shapes=[pltpu.SemaphoreType.DMA((rider.n_sems,)), pltpu.SemaphoreType.DMA((rider.n_sems,))],
        input_output_aliases=rider.aliases,
    )(*rider.inputs)
    return list(outs)


def _ple_tile(h1, p_ref, gain_ref, wg_v, wp_v):
    xn, _ = _rms(h1)
    hpb = (xn * gain_ref[...]).astype(BF16)
    gate = _sigmoid(_dot(hpb, wg_v[...]))
    pb = p_ref[...].astype(BF16)
    pe = jnp.concatenate([_dot(pb, wp_v[k]) for k in range(4)], axis=1)
    return h1 + gate * pe, gate


def _ple_parts(ple, ts, d):
    p, layer, gain, w_gate, w_proj = ple
    s, pd = p.shape[1:]
    row = pl.BlockSpec((ts, d), lambda i: (i, 0))
    return dict(
        operands=[p, gain, w_gate, w_proj],
        in_specs=[pl.BlockSpec((None, ts, pd), lambda i: (layer, i, 0)), _full((1, d)), ANY, ANY],
        out_specs=[row, row], out_shape=[_sds((s, d), F32), _sds((s, d), BF16)],
        scratch=[pltpu.VMEM((d, d), BF16), pltpu.VMEM((4, pd, d // 4), BF16)])


def _fwd_mix_a(h, gain, conv_w, w_in, w_out, name, rider=None, ple=None):
    s, d = h.shape
    e = MIX_WIDTH
    ts = min(TS_FWD, s)
    nt = s // ts
    extra = _ple_parts(ple, ts, d) if ple else None

    def body(*refs):
        h_ref, gain_ref, cw_ref, win_hbm, wout_hbm = refs[:5]
        n_in = 9 if ple else 5
        h1_ref, proj_ref = refs[n_in:n_in + 2]
        win_v, wout_v, carry, sems = refs[n_in + (4 if ple else 2):][:4]
        i = pl.program_id(0)

        @pl.when(i == 0)
        def _():
            loads = [(win_hbm, win_v), (wout_hbm, wout_v)]
            if ple:
                loads += [(refs[7], refs[-2]), (refs[8], refs[-1])]
            _copy_all(loads, sems)
            carry[...] = jnp.zeros_like(carry)

        hh = h_ref[...]
        xn, _ = _rms(hh)
        hnb = (xn * gain_ref[...]).astype(BF16)
        b = _dot(hnb, win_v[0])
        c = _dot(hnb, win_v[1])
        v = _dot(hnb, win_v[2])
        z = _dot(hnb, win_v[3])
        proj_ref[:, 0 * e:1 * e] = b.astype(BF16)
        proj_ref[:, 1 * e:2 * e] = c.astype(BF16)
        proj_ref[:, 2 * e:3 * e] = v.astype(BF16)
        proj_ref[:, 3 * e:4 * e] = z.astype(BF16)
        cv = c * v
        tail = carry[...]
        carry[...] = cv[ts - HALO:ts]
        conv = cw_ref[0:1, :] * _shift_down(cv, 2, tail) + cw_ref[1:2, :] * _shift_down(cv, 1, tail) + cw_ref[2:3, :] * cv
        mb = ((z * _sigmoid(z)) * (b * conv)).astype(BF16)
        h1 = hh + _dot(mb, wout_v[...])
        h1_ref[...] = h1
        if ple:
            h2, gate = _ple_tile(h1, refs[5], refs[6], refs[-2], refs[-1])
            refs[n_in + 2][...] = h2
            refs[n_in + 3][...] = gate.astype(BF16)

    row = lambda width: pl.BlockSpec((ts, width), lambda i: (i, 0))
    return _call(
        body, name=name, grid=(nt,),
        in_specs=[row(d), _full((1, d)), _full((8, e)), ANY, ANY] + (extra["in_specs"] if ple else []),
        out_specs=[row(d), row(4 * e)] + (extra["out_specs"] if ple else []),
        out_shape=[_sds((s, d), F32), _sds((s, 4 * e), BF16)] + (extra["out_shape"] if ple else []),
        scratch_shapes=[pltpu.VMEM((4, d, e), BF16), pltpu.VMEM((e, d), BF16), pltpu.VMEM((HALO, e), F32),
                        pltpu.SemaphoreType.DMA((4,))] + (extra["scratch"] if ple else []),
        operands=[h, gain, conv_w, w_in, w_out] + (extra["operands"] if ple else []), rider=rider)


def _fwd_mix_b(h, gain, scale, w_in, w_grp, w_out, name, rider=None, ple=None):
    s, d = h.shape
    e = MIX_WIDTH
    ts = min(TS_FWD, s)
    nt = s // ts
    extra = _ple_parts(ple, ts, d) if ple else None

    def body(*refs):
        h_ref, gain_ref, scale_ref, win_hbm, wgrp_hbm, wout_hbm = refs[:6]
        n_in = 10 if ple else 6
        h1_ref, z_ref, mx_ref, dd_ref = refs[n_in:n_in + 4]
        win_v, wgrp_v, wout_v, carry, sems = refs[n_in + (6 if ple else 4):][:5]
        i = pl.program_id(0)

        @pl.when(i == 0)
        def _():
            loads = [(win_hbm, win_v), (wout_hbm, wout_v)] + _grp_pairs(wgrp_hbm, wgrp_v)
            if ple:
                loads += [(refs[8], refs[-2]), (refs[9], refs[-1])]
            _copy_all(loads, sems)
            carry[...] = jnp.zeros_like(carry)

        hh = h_ref[...]
        xn, _ = _rms(hh)
        hnb = (xn * gain_ref[...]).astype(BF16)
        u = jnp.concatenate([_dot(hnb, win_v[0]), _dot(hnb, win_v[1])], axis=1)
        z = jnp.concatenate([_dot(hnb, win_v[2]), _dot(hnb, win_v[3])], axis=1)
        z_ref[...] = z.astype(BF16)
        diff = (_pool_fwd(u, carry, i, ts) - u).astype(BF16)
        dd_ref[...] = diff
        mx = jnp.concatenate(
            [_dot(diff[:, g * GROUP_DIM:(g + 1) * GROUP_DIM], wgrp_v[g]) for g in range(N_GROUPS)], axis=1)
        mx_ref[...] = mx.astype(BF16)
        mb = ((z * _sigmoid(z)) * (mx * scale_ref[...])).astype(BF16)
        h1 = hh + _dot(mb, wout_v[...])
        h1_ref[...] = h1
        if ple:
            h2, gate = _ple_tile(h1, refs[6], refs[7], refs[-2], refs[-1])
            refs[n_in + 4][...] = h2
            refs[n_in + 5][...] = gate.astype(BF16)

    row = lambda width: pl.BlockSpec((ts, width), lambda i: (i, 0))
    return _call(
        body, name=name, grid=(nt,),
        in_specs=[row(d), _full((1, d)), _full((1, e)), ANY, ANY, ANY] + (extra["in_specs"] if ple else []),
        out_specs=[row(d), row(e), row(e), row(e)] + (extra["out_specs"] if ple else []),
        out_shape=[_sds((s, d), F32)] + [_sds((s, e), BF16)] * 3 + (extra["out_shape"] if ple else []),
        scratch_shapes=[pltpu.VMEM((4, d, e // 2), BF16), pltpu.VMEM((N_GROUPS, GROUP_DIM, GROUP_DIM), BF16),
                        pltpu.VMEM((e, d), BF16), pltpu.VMEM((4, HALO, e), F32), pltpu.SemaphoreType.DMA((20,))]
        + (extra["scratch"] if ple else []),
        operands=[h, gain, scale, w_in, w_grp, w_out] + (extra["operands"] if ple else []), rider=rider)


def _fwd_ple(h1, p, gain, w_gate, w_proj, layer, rider=None):
    s, d = h1.shape
    pd = p.shape[-1]
    ts = min(TS_PLE, s)
    nt = s // ts

    def body(h1_ref, p_ref, gain_ref, wg_hbm, wp_hbm, h2_ref, gate_ref, wg_v, wp_v, sems):
        @pl.when(pl.program_id(0) == 0)
        def _():
            _copy_all([(wg_hbm, wg_v), (wp_hbm, wp_v)], sems)

        hh = h1_ref[...]
        xn, _ = _rms(hh)
        hpb = (xn * gain_ref[...]).astype(BF16)
        gate = _sigmoid(_dot(hpb, wg_v[...]))
        pb = p_ref[...].astype(BF16)
        pe = jnp.concatenate([_dot(pb, wp_v[k]) for k in range(4)], axis=1)
        gate_ref[...] = gate.astype(BF16)
        h2_ref[...] = hh + gate * pe

    row = lambda width: pl.BlockSpec((ts, width), lambda i: (i, 0))
    return _call(
        body, name=f"fwd_ple{layer}", grid=(nt,),
        in_specs=[row(d), pl.BlockSpec((None, ts, pd), lambda i: (layer, i, 0)), _full((1, d)), ANY, ANY],
        out_specs=[row(d), row(d)],
        out_shape=[_sds((s, d), F32), _sds((s, d), BF16)],
        scratch_shapes=[pltpu.VMEM((d, d), BF16), pltpu.VMEM((4, pd, d // 4), BF16), pltpu.SemaphoreType.DMA((2,))],
        operands=[h1, p, gain, w_gate, w_proj], rider=rider)


def _bwd_ple(dh2, h1, gate, p, gain, w_gate, w_proj, layer, rider=None, loss_head=None):
    s, d = dh2.shape
    pd = p.shape[-1]
    ts = min(TS_PLE, s)
    nt = s // ts
    qd = d // 4
    n_head = 0 if loss_head is None else 2

    def body(*refs):
        dh2_ref = refs[0]
        h1_ref, gate_ref, p_ref, gain_ref, wg_hbm, wp_hbm, dh1_ref, dgain_ref, dwg_hbm, dwp_hbm = refs[1 + n_head:11 + n_head]
        wg_v, wp_v, acc_g, acc_p, sems = refs[-5:]
        i = pl.program_id(0)

        @pl.when(i == 0)
        def _():
            _copy_all([(wg_hbm, wg_v), (wp_hbm, wp_v)], sems)
            dgain_ref[...] = jnp.zeros_like(dgain_ref)
            acc_g[...] = jnp.zeros_like(acc_g)
            acc_p[...] = jnp.zeros_like(acc_p)

        if loss_head is None:
            g2 = dh2_ref[...]
        else:
            t_ref, fgain_ref, loss_ref, dfgain_ref = refs[1], refs[2], refs[11 + n_head], refs[12 + n_head]

            @pl.when(i == 0)
            def _():
                loss_ref[...] = jnp.zeros_like(loss_ref)
                dfgain_ref[...] = jnp.zeros_like(dfgain_ref)

            xf, rf = _rms(dh2_ref[...])
            err = xf * fgain_ref[...] - t_ref[...]
            part = 0.5 * jnp.sum(jnp.mean(err * err, axis=-1, keepdims=True), axis=0, keepdims=True)
            loss_ref[...] += jnp.broadcast_to(part, loss_ref.shape)
            g2, dfgain = _rms_bwd(err * (1.0 / d), xf, rf, fgain_ref[...])
            dfgain_ref[...] += dfgain
        gate_f = gate_ref[...].astype(F32)
        xn, r = _rms(h1_ref[...])
        hpb = (xn * gain_ref[...]).astype(BF16)
        pb = p_ref[...].astype(BF16)
        pe = jnp.concatenate([_dot(pb, wp_v[k]) for k in range(4)], axis=1)
        dpeb = (g2 * gate_f).astype(BF16)
        dab = ((g2 * pe) * (gate_f * (1.0 - gate_f))).astype(BF16)
        acc_g[...] += _dot_tn(hpb, dab)
        for k in range(4):
            acc_p[k] += _dot_tn(pb, dpeb[:, k * qd:(k + 1) * qd])
        dhp = _dot_nt(dab, wg_v[...])
        dh, dgain = _rms_bwd(dhp, xn, r, gain_ref[...])
        dh1_ref[...] = g2 + dh
        dgain_ref[...] += dgain

        @pl.when(i == nt - 1)
        def _():
            _copy_all([(acc_g, dwg_hbm), (acc_p, dwp_hbm)], sems)

    row = pl.BlockSpec((ts, d), lambda i: (i, 0))
    head = loss_head is not None
    return _call(
        body, name=f"bwd_ple{layer}", grid=(nt,),
        in_specs=[row] + ([row, _full((1, d))] if head else [])
        + [row, row, pl.BlockSpec((None, ts, pd), lambda i: (layer, i, 0)), _full((1, d)), ANY, ANY],
        out_specs=[row, _full((1, d)), ANY, ANY] + ([_full((8, 128)), _full((1, d))] if head else []),
        out_shape=[_sds((s, d), F32), _sds((1, d), F32), _sds((d, d), F32), _sds((4, pd, qd), F32)]
        + ([_sds((8, 128), F32), _sds((1, d), F32)] if head else []),
        scratch_shapes=[pltpu.VMEM((d, d), BF16), pltpu.VMEM((4, pd, qd), BF16), pltpu.VMEM((d, d), F32),
                        pltpu.VMEM((4, pd, qd), F32), pltpu.SemaphoreType.DMA((2,))],
        operands=[dh2] + (list(loss_head) if head else []) + [h1, gate, p, gain, w_gate, w_proj], rider=rider)


def _mix_a_tile_grads(proj_ref, ch_ref, vh_ref, cw_ref, dh1b, wout_v, carry, dcw_ref, tile, hb):
    e = MIX_WIDTH
    b = proj_ref[:, 0 * e:1 * e].astype(F32)
    c = proj_ref[:, 1 * e:2 * e].astype(F32)
    v = proj_ref[:, 2 * e:3 * e].astype(F32)
    z = proj_ref[:, 3 * e:4 * e].astype(F32)
    cv = c * v
    prev = (ch_ref[...].astype(F32) * vh_ref[...].astype(F32))[hb - HALO:hb]
    tail = jnp.where(tile > 0, prev, jnp.zeros_like(prev))
    cv1 = _shift_down(cv, 1, tail)
    cv2 = _shift_down(cv, 2, tail)
    conv = cw_ref[0:1, :] * cv2 + cw_ref[1:2, :] * cv1 + cw_ref[2:3, :] * cv
    sig = _sigmoid(z)
    sz = z * sig
    y = b * conv
    dm = _dot_nt(dh1b, wout_v[...])
    dz = (dm * y) * (sig * (1.0 + z * (1.0 - sig)))
    dy = dm * sz
    db = dy * conv
    dconv = dy * b
    head = carry[...]
    carry[...] = dconv[0:HALO]
    dcv = cw_ref[2:3, :] * dconv + cw_ref[1:2, :] * _shift_up(dconv, 1, head) + cw_ref[0:1, :] * _shift_up(dconv, 2, head)
    dcw_ref[0:1, :] += jnp.sum(dconv * cv2, axis=0, keepdims=True)
    dcw_ref[1:2, :] += jnp.sum(dconv * cv1, axis=0, keepdims=True)
    dcw_ref[2:3, :] += jnp.sum(dconv * cv, axis=0, keepdims=True)
    parts = [db.astype(BF16), (dcv * v).astype(BF16), (dcv * c).astype(BF16), dz.astype(BF16)]
    return parts, (sz * y).astype(BF16)


def _bwd_mix_a(dh1, h, proj, gain, conv_w, w_in, w_out, name, rider=None):
    s, d = dh1.shape
    e = MIX_WIDTH
    ts = min(TS_MIX, s)
    nt = s // ts
    hb = 16
    per = ts // hb

    def body(dh1_ref, h_ref, proj_ref, ch_ref, vh_ref, gain_ref, cw_ref, win_hbm, wout_hbm,
             dh_ref, dcw_ref, dgain_ref, dwin_hbm, dwout_hbm, win_v, wout_v, acc_in, acc_out, carry, sems):
        i = pl.program_id(0)

        @pl.when(i == 0)
        def _():
            _copy_all([(win_hbm, win_v), (wout_hbm, wout_v)], sems)
            carry[...] = jnp.zeros_like(carry)
            dcw_ref[...] = jnp.zeros_like(dcw_ref)
            dgain_ref[...] = jnp.zeros_like(dgain_ref)
            acc_in[...] = jnp.zeros_like(acc_in)
            acc_out[...] = jnp.zeros_like(acc_out)

        dh1 = dh1_ref[...]
        dh1b = dh1.astype(BF16)
        parts, mb = _mix_a_tile_grads(proj_ref, ch_ref, vh_ref, cw_ref, dh1b, wout_v, carry, dcw_ref, nt - 1 - i, hb)
        acc_out[...] += _dot_tn(mb, dh1b)
        xn, r = _rms(h_ref[...])
        hnb = (xn * gain_ref[...]).astype(BF16)
        for q in range(4):
            acc_in[q] += _dot_tn(hnb, parts[q])
        dhn = _dot_nt(parts[0], win_v[0]) + _dot_nt(parts[1], win_v[1]) + _dot_nt(parts[2], win_v[2]) + _dot_nt(parts[3], win_v[3])
        dh, dgain = _rms_bwd(dhn, xn, r, gain_ref[...])
        dh_ref[...] = dh1 + dh
        dgain_ref[...] += dgain

        @pl.when(i == nt - 1)
        def _():
            _copy_all([(acc_in, dwin_hbm), (acc_out, dwout_hbm)], sems)

    row = lambda width: pl.BlockSpec((ts, width), lambda i: (nt - 1 - i, 0))
    halo = lambda col: pl.BlockSpec((hb, e), lambda i: (jnp.maximum((nt - 1 - i) * per - 1, 0), col))
    return _call(
        body, name=name, grid=(nt,),
        in_specs=[row(d), row(d), row(4 * e), halo(1), halo(2), _full((1, d)), _full((8, e)), ANY, ANY],
        out_specs=[row(d), _full((8, e)), _full((1, d)), ANY, ANY],
        out_shape=[_sds((s, d), F32), _sds((8, e), F32), _sds((1, d), F32), _sds((4, d, e), F32), _sds((e, d), F32)],
        scratch_shapes=[pltpu.VMEM((4, d, e), BF16), pltpu.VMEM((e, d), BF16), pltpu.VMEM((4, d, e), F32),
                        pltpu.VMEM((e, d), F32), pltpu.VMEM((HALO, e), F32), pltpu.SemaphoreType.DMA((2,))],
        operands=[dh1, h, proj, proj, proj, gain, conv_w, w_in, w_out], rider=rider)


def _bwd_mix_a_weights(dh1, h, proj, gain, conv_w, w_out, name, rider=None):
    s, d = dh1.shape
    e = MIX_WIDTH
    ts = min(TS_MIX, s)
    nt = s // ts
    hb = 16
    per = ts // hb

    def body(dh1_ref, h_ref, proj_ref, ch_ref, vh_ref, gain_ref, cw_ref, wout_hbm,
             dproj_ref, dcw_ref, dwin_hbm, dwout_hbm, wout_v, acc_in, acc_out, carry, sems):
        i = pl.program_id(0)

        @pl.when(i == 0)
        def _():
            _copy_all([(wout_hbm, wout_v)], sems)
            carry[...] = jnp.zeros_like(carry)
            dcw_ref[...] = jnp.zeros_like(dcw_ref)
            acc_in[...] = jnp.zeros_like(acc_in)
            acc_out[...] = jnp.zeros_like(acc_out)

        dh1b = dh1_ref[...].astype(BF16)
        parts, mb = _mix_a_tile_grads(proj_ref, ch_ref, vh_ref, cw_ref, dh1b, wout_v, carry, dcw_ref, nt - 1 - i, hb)
        acc_out[...] += _dot_tn(mb, dh1b)
        xn, _ = _rms(h_ref[...])
        hnb = (xn * gain_ref[...]).astype(BF16)
        for q in range(4):
            acc_in[q] += _dot_tn(hnb, parts[q])
            dproj_ref[:, q * e:(q + 1) * e] = parts[q]

        @pl.when(i == nt - 1)
        def _():
            _copy_all([(acc_in, dwin_hbm), (acc_out, dwout_hbm)], sems)

    row = lambda width: pl.BlockSpec((ts, width), lambda i: (nt - 1 - i, 0))
    halo = lambda col: pl.BlockSpec((hb, e), lambda i: (jnp.maximum((nt - 1 - i) * per - 1, 0), col))
    return _call(
        body, name=name, grid=(nt,),
        in_specs=[row(d), row(d), row(4 * e), halo(1), halo(2), _full((1, d)), _full((8, e)), ANY],
        out_specs=[row(4 * e), _full((8, e)), ANY, ANY],
        out_shape=[_sds((s, 4 * e), BF16), _sds((8, e), F32), _sds((4, d, e), F32), _sds((e, d), F32)],
        scratch_shapes=[pltpu.VMEM((e, d), BF16), pltpu.VMEM((4, d, e), F32), pltpu.VMEM((e, d), F32),
                        pltpu.VMEM((HALO, e), F32), pltpu.SemaphoreType.DMA((2,))],
        operands=[dh1, h, proj, proj, proj, gain, conv_w, w_out], rider=rider)


def _bwd_mix_a_input(dproj, h, dh1, gain, w_in, name, rider=None):
    s, d = dh1.shape
    e = MIX_WIDTH
    ts = min(TS_PLE, s)
    nt = s // ts

    def body(dproj_ref, h_ref, dh1_ref, gain_ref, win_hbm, dh_ref, dgain_ref, win_v, sems):
        @pl.when(pl.program_id(0) == 0)
        def _():
            _copy_all([(win_hbm, win_v)], sems)
            dgain_ref[...] = jnp.zeros_like(dgain_ref)

        dhn = _dot_nt(dproj_ref[:, 0:e], win_v[0])
        for q in range(1, 4):
            dhn = dhn + _dot_nt(dproj_ref[:, q * e:(q + 1) * e], win_v[q])
        xn, r = _rms(h_ref[...])
        dh, dgain = _rms_bwd(dhn, xn, r, gain_ref[...])
        dh_ref[...] = dh1_ref[...] + dh
        dgain_ref[...] += dgain

    row = lambda width: pl.BlockSpec((ts, width), lambda i: (i, 0))
    return _call(
        body, name=name, grid=(nt,),
        in_specs=[row(4 * e), row(d), row(d), _full((1, d)), ANY],
        out_specs=[row(d), _full((1, d))],
        out_shape=[_sds((s, d), F32), _sds((1, d), F32)],
        scratch_shapes=[pltpu.VMEM((4, d, e), BF16), pltpu.SemaphoreType.DMA((1,))],
        operands=[dproj, h, dh1, gain, w_in], rider=rider)


def _bwd_mix_b(dh1, h, z, mx, diff, gain, scale, w_in, w_grp, w_out, name, rider=None):
    s, d = dh1.shape
    e = MIX_WIDTH
    ts = min(TS_MIX, s)
    nt = s // ts
    half = e // 2

    def body(dh1_ref, h_ref, z_ref, mx_ref, dd_ref, gain_ref, scale_ref, win_hbm, wgrp_hbm, wout_hbm,
             dh_ref, dscale_ref, dgain_ref, dwin_hbm, dwgrp_hbm, dwout_hbm,
             win_v, wgrp_v, wout_v, acc_in, acc_grp, acc_out, carry, sems):
        i = pl.program_id(0)
        tile = nt - 1 - i

        @pl.when(i == 0)
        def _():
            _copy_all([(win_hbm, win_v), (wout_hbm, wout_v)] + _grp_pairs(wgrp_hbm, wgrp_v), sems)
            carry[...] = jnp.zeros_like(carry)
            dscale_ref[...] = jnp.zeros_like(dscale_ref)
            dgain_ref[...] = jnp.zeros_like(dgain_ref)
            acc_in[...] = jnp.zeros_like(acc_in)
            acc_grp[...] = jnp.zeros_like(acc_grp)
            acc_out[...] = jnp.zeros_like(acc_out)

        zf = z_ref[...].astype(F32)
        mxf = mx_ref[...].astype(F32)
        sig = _sigmoid(zf)
        sz = zf * sig
        mixed = mxf * scale_ref[...]
        dh1 = dh1_ref[...]
        dh1b = dh1.astype(BF16)
        acc_out[...] += _dot_tn((sz * mixed).astype(BF16), dh1b)
        dm = _dot_nt(dh1b, wout_v[...])
        dz = (dm * mixed) * (sig * (1.0 + zf * (1.0 - sig)))
        dmixed = dm * sz
        dscale_ref[...] += jnp.sum(dmixed * mxf, axis=0, keepdims=True)
        dmxb = (dmixed * scale_ref[...]).astype(BF16)
        diff = dd_ref[...]
        for g in range(N_GROUPS):
            cols = slice(g * GROUP_DIM, (g + 1) * GROUP_DIM)
            acc_grp[g] += _dot_tn(diff[:, cols], dmxb[:, cols])
        ddiff = jnp.concatenate(
            [_dot_nt(dmxb[:, g * GROUP_DIM:(g + 1) * GROUP_DIM], wgrp_v[g]) for g in range(N_GROUPS)], axis=1)
        dub = (_pool_bwd(ddiff, carry, tile, ts) - ddiff).astype(BF16)
        dzb = dz.astype(BF16)
        parts = [dub[:, 0:half], dub[:, half:e], dzb[:, 0:half], dzb[:, half:e]]
        xn, r = _rms(h_ref[...])
        hnb = (xn * gain_ref[...]).astype(BF16)
        for k in range(4):
            acc_in[k] += _dot_tn(hnb, parts[k])
        dhn = _dot_nt(parts[0], win_v[0]) + _dot_nt(parts[1], win_v[1]) + _dot_nt(parts[2], win_v[2]) + _dot_nt(parts[3], win_v[3])
        dh, dgain = _rms_bwd(dhn, xn, r, gain_ref[...])
        dh_ref[...] = dh1 + dh
        dgain_ref[...] += dgain

        @pl.when(i == nt - 1)
        def _():
            _copy_all([(acc_in, dwin_hbm), (acc_out, dwout_hbm)] + [(v, hb_) for hb_, v in _grp_pairs(dwgrp_hbm, acc_grp)], sems)

    row = lambda width: pl.BlockSpec((ts, width), lambda i: (nt - 1 - i, 0))
    return _call(
        body, name=name, grid=(nt,),
        in_specs=[row(d), row(d), row(e), row(e), row(e), _full((1, d)), _full((1, e)), ANY, ANY, ANY],
        out_specs=[row(d), _full((1, e)), _full((1, d)), ANY, ANY, ANY],
        out_shape=[_sds((s, d), F32), _sds((1, e), F32), _sds((1, d), F32), _sds((4, d, half), F32),
                   _sds((4, N_GROUPS, GROUP_DIM // 4, GROUP_DIM), F32), _sds((e, d), F32)],
        scratch_shapes=[pltpu.VMEM((4, d, half), BF16), pltpu.VMEM((N_GROUPS, GROUP_DIM, GROUP_DIM), BF16),
                        pltpu.VMEM((e, d), BF16), pltpu.VMEM((4, d, half), F32),
                        pltpu.VMEM((N_GROUPS, GROUP_DIM, GROUP_DIM), F32), pltpu.VMEM((e, d), F32),
                        pltpu.VMEM((4, HALO, e), F32), pltpu.SemaphoreType.DMA((18,))],
        operands=[dh1, h, z, mx, diff, gain, scale, w_in, w_grp, w_out], rider=rider)


def _first_gather(rider, small):
    shards = rider.inputs
    ni = len(shards)

    def body(*refs):
        rin, small_src = refs[:ni], refs[ni]
        rout, small_dst = refs[ni + 1:2 * ni + 1], refs[2 * ni + 1]
        send, recv, ssend, srecv = refs[2 * ni + 2:]
        x, y, c, chips = _place()
        me = 2 * x + y
        peers = [(cx, cy, c) for cx, cy in chips] + [(x, y, 1 - c)]
        vec = [_remote(small_src, small_dst.at[me], ssend, srecv, j, to) for j, to in enumerate(peers)]
        for cp in vec:
            cp.start()
        rider.start(rin, rout, send, recv)
        rider.middle(rin, rout, send, recv)
        rider.finish(rin, rout, send, recv)
        for j, (px, py, _) in enumerate(peers):
            _remote(small_src, small_dst.at[2 * px + py], ssend, srecv, j, peers[j]).wait_recv()
        for cp in vec:
            cp.wait_send()

    outs = pl.pallas_call(
        body, name="first_gather", in_specs=[ANY] * (ni + 1), out_specs=[ANY] * (ni + 1),
        out_shape=rider.out_shapes + [_sds((4,) + small.shape, small.dtype)],
        scratch_shapes=[pltpu.SemaphoreType.DMA((rider.n_sems,)), pltpu.SemaphoreType.DMA((rider.n_sems,)),
                        pltpu.SemaphoreType.DMA((4,)), pltpu.SemaphoreType.DMA((4,))],
    )(*shards, small)
    return list(outs[:ni]), outs[ni]


def _vector_rider(pack):
    flips = [(fx, fy, fc) for fx in (0, 1) for fy in (0, 1) for fc in (0, 1)][1:]

    def copies(rin, rout, send, recv, base):
        x, y, c, _ = _place()
        me = 4 * x + 2 * y + c
        peers = [(1 - x if fx else x, 1 - y if fy else y, 1 - c if fc else c) for fx, fy, fc in flips]
        own = pltpu.make_async_copy(rin[0], rout[0].at[me], send.at[base + 7])
        out = [_remote(rin[0], rout[0].at[me], send, recv, base + r, peer) for r, peer in enumerate(peers)]
        back = [_remote(rin[0], rout[0].at[4 * px + 2 * py + pc], send, recv, base + r, (px, py, pc))
                for r, (px, py, pc) in enumerate(peers)]
        return own, out, back

    def start(rin, rout, send, recv, base=0):
        own, out, _ = copies(rin, rout, send, recv, base)
        own.start()
        for cp in out:
            cp.start()

    def finish(rin, rout, send, recv, base=0):
        own, out, back = copies(rin, rout, send, recv, base)
        for cp in back:
            cp.wait_recv()
        for cp in out:
            cp.wait_send()
        own.wait()

    return _Rider([pack], [_sds((8,) + pack.shape, pack.dtype)], 8, start, finish)


def _vector_sum(landed, row_counts):
    _, rows, d = landed.shape

    def body(l_ref, *out_refs):
        total = l_ref[0]
        for dev in range(1, 8):
            total = total + l_ref[dev]
        for t, out_ref in enumerate(out_refs):
            out_ref[...] = total[t * PACK_GROUP:t * PACK_GROUP + row_counts[t]]

    vmem = pl.BlockSpec(memory_space=pltpu.VMEM)
    return pl.pallas_call(body, name="vector_sum", in_specs=[vmem], out_specs=[vmem] * len(row_counts),
                          out_shape=[_sds((n, d), F32) for n in row_counts])(landed)


def _job_rows(rows, cols):
    return min(rows, max(8, JOB_BLOCK_BYTES // (4 * cols)))


def _pair_sum_job(grad, sibling_rows):
    _, _, rh, cols = grad.shape
    tr = _job_rows(rh, cols)
    nr = rh // tr

    def chip_of(j, pos):
        return jnp.bitwise_xor(pos[0], jnp.where(j == 2, 3, 2 - j))

    return dict(
        ins=[(grad, (None, None, tr, cols), lambda l, pos: (chip_of(l // nr, pos), pos[1], l % nr, 0)),
             (sibling_rows, (None, tr, cols), lambda l, pos: (chip_of(l // nr, pos), l % nr, 0))],
        outs=[((3, rh, cols), BF16, (None, tr, cols), lambda l, pos: (l // nr, l % nr, 0))],
        steps=3 * nr, fn=lambda g, sb: [(g + sb).astype(BF16)], alias=None)


def _final_sum_job(grad, sibling_rows, landed, stack, slot, n_slots):
    _, _, rh, cols = grad.shape
    tr = _job_rows(rh, cols)

    def fn(g, sb, ld):
        total = g + sb
        for j in range(3):
            total = total + ld[j].astype(F32)
        return [total]

    return dict(
        ins=[(grad, (None, None, tr, cols), lambda l, pos: (pos[0], pos[1], l, 0)),
             (sibling_rows, (None, tr, cols), lambda l, pos: (pos[0], l, 0)),
             (landed, (3, tr, cols), lambda l, pos: (0, l, 0))],
        outs=[((n_slots, 2, rh, cols), F32, (None, None, tr, cols), lambda l, pos: (slot, pos[1], l, 0))],
        steps=rh // tr, fn=fn, alias=stack)


def _adamw_job(g, w, m, v, block_bytes):
    rows, cols = g.shape
    tr = min(rows, max(8, block_bytes // (4 * cols)))

    def fn(gg, ww, mm, vv):
        nm = ADAM_B1 * mm + (1.0 - ADAM_B1) * gg
        nv = ADAM_B2 * vv + (1.0 - ADAM_B2) * (gg * gg)
        m_hat = nm / (1.0 - ADAM_B1 ** ADAM_STEP)
        v_hat = nv / (1.0 - ADAM_B2 ** ADAM_STEP)
        return [-ADAM_LR * (m_hat / (jnp.sqrt(v_hat) + ADAM_EPS) + ADAM_WD * ww), nm, nv, gg]

    block = lambda l, pos: (l, 0)
    return dict(ins=[(a, (tr, cols), block) for a in (g, w, m, v)],
                outs=[((rows, cols), F32, (tr, cols), block)] * 4, steps=rows // tr, fn=fn, alias=None)


def _run_jobs(jobs, place, name):
    starts, total = [], 0
    for jb in jobs:
        starts.append(total)
        total += jb["steps"]

    def clamped(fn, start, steps):
        return lambda s, pos: fn(jnp.clip(s - start, 0, steps - 1), pos)

    in_specs, operands = [], [place]
    for jb, start in zip(jobs, starts):
        for arr, block, fn in jb["ins"]:
            in_specs.append(pl.BlockSpec(block, clamped(fn, start, jb["steps"])))
            operands.append(arr)
    n_ins = len(in_specs)
    first_out, n_outs = [], 0
    for jb in jobs:
        first_out.append(n_outs)
        n_outs += len(jb["outs"])
    aliases = {}
    for t, jb in enumerate(jobs):
        if jb["alias"] is not None:
            in_specs.append(ANY)
            operands.append(jb["alias"])
            aliases[len(operands) - 1] = first_out[t]
    out_specs = [pl.BlockSpec(block, clamped(fn, start, jb["steps"]))
                 for jb, start in zip(jobs, starts) for _, _, block, fn in jb["outs"]]

    def body(place_ref, *refs):
        in_refs, out_refs = refs[:n_ins], refs[len(in_specs):]
        s = pl.program_id(0)
        first = 0
        for t, (jb, start) in enumerate(zip(jobs, starts)):
            mine = in_refs[first:first + len(jb["ins"])]
            first += len(jb["ins"])

            @pl.when((s >= start) & (s < start + jb["steps"]))
            def _(mine=mine, t=t, jb=jb):
                values = jb["fn"](*[r[...] for r in mine])
                for n, value in enumerate(values):
                    out_refs[first_out[t] + n][...] = value

    grid_spec = pltpu.PrefetchScalarGridSpec(num_scalar_prefetch=1, grid=(total,), in_specs=in_specs, out_specs=out_specs)
    outs = pl.pallas_call(body, name=name, grid_spec=grid_spec,
                          out_shape=[_sds(shape, dtype) for jb in jobs for shape, dtype, _, _ in jb["outs"]],
                          input_output_aliases=aliases, compiler_params=_params(1))(*operands)
    return [list(outs[first_out[t]:first_out[t] + len(jb["outs"])]) for t, jb in enumerate(jobs)]


BIG = ["a_w_in", "a_w_out", "b_w_in", "b_w_grp", "b_w_out", "ple_w_gate", "ple_w_proj"]

GATHER_PLAN = {
    "first": [("a_w_in", 0), ("a_w_out", 0)],
    "mix0": [("ple_w_gate", 0), ("ple_w_proj", 0), ("b_w_in", 0), ("b_w_grp", 0), ("b_w_out", 0)],
    "ple0": [("ple_w_gate", 1), ("ple_w_proj", 1)],
    "mix1": [("a_w_in", 1), ("a_w_out", 1), ("ple_w_gate", 2), ("ple_w_proj", 2)],
    "mix2": [("b_w_in", 1), ("b_w_grp", 1), ("b_w_out", 1), ("ple_w_gate", 3), ("ple_w_proj", 3)],
}


def _as_2d(name, a):
    if name == "b_w_grp":
        return a.reshape(a.shape[0], N_GROUPS * (GROUP_DIM // 4), GROUP_DIM)
    return a


def kernel(x, p, norm_mix, a_w_in, a_w_conv, a_w_out, b_w_in, b_w_grp, b_scale, b_w_out, ple_norm, ple_w_gate, ple_w_proj, final_norm, loss_target, m_norm_mix, m_a_w_in, m_a_w_conv, m_a_w_out, m_b_w_in, m_b_w_grp, m_b_scale, m_b_w_out, m_ple_norm, m_ple_w_gate, m_ple_w_proj, m_final_norm, v_norm_mix, v_a_w_in, v_a_w_conv, v_a_w_out, v_b_w_in, v_b_w_grp, v_b_scale, v_b_w_out, v_ple_norm, v_ple_w_gate, v_ple_w_proj, v_final_norm):
    d, e = D_MODEL, MIX_WIDTH
    s = x.shape[1]
    cx, cy, cc = lax.axis_index("x"), lax.axis_index("y"), lax.axis_index("c")
    chip = 2 * cx + cy
    place = jnp.stack([chip, cc]).astype(jnp.int32)

    weights = dict(a_w_in=a_w_in, a_w_out=a_w_out, b_w_in=b_w_in, b_w_grp=b_w_grp, b_w_out=b_w_out,
                   ple_w_gate=ple_w_gate, ple_w_proj=ple_w_proj)
    moms = dict(a_w_in=m_a_w_in, a_w_out=m_a_w_out, b_w_in=m_b_w_in, b_w_grp=m_b_w_grp, b_w_out=m_b_w_out,
                ple_w_gate=m_ple_w_gate, ple_w_proj=m_ple_w_proj)
    vars_ = dict(a_w_in=v_a_w_in, a_w_out=v_a_w_out, b_w_in=v_b_w_in, b_w_grp=v_b_w_grp, b_w_out=v_b_w_out,
                 ple_w_gate=v_ple_w_gate, ple_w_proj=v_ple_w_proj)
    w2d = {nm: _as_2d(nm, weights[nm]) for nm in BIG}
    bf = {nm: w2d[nm].astype(BF16).reshape(w2d[nm].shape[0], 2, w2d[nm].shape[1] // 2, w2d[nm].shape[2]) for nm in BIG}
    gathered = {}

    def gather_rider(host):
        keys = GATHER_PLAN.get(host)
        return _gather_rider([bf[nm] for nm, _ in keys], [j for _, j in keys]) if keys else None

    def keep(host, landed):
        for k, a in zip(GATHER_PLAN.get(host, []), landed):
            gathered[k] = a

    def weight(nm, j):
        a = gathered[(nm, j)]
        shapes = {"a_w_in": (4, d, e), "a_w_out": (e, d), "b_w_in": (4, d, e // 2),
                  "b_w_grp": (4, N_GROUPS, GROUP_DIM // 4, GROUP_DIM), "b_w_out": (e, d), "ple_w_gate": (d, d),
                  "ple_w_proj": (4, PLE_DIM, d // 4)}
        return a.reshape(shapes[nm])

    pad = jnp.zeros((4, e // 4), F32)
    small = jnp.concatenate([a_w_conv[0], b_scale[0:1], pad, a_w_conv[1], b_scale[1:2], pad], axis=0)
    landed, small_full = _first_gather(gather_rider("first"), small)
    keep("first", landed)
    small_full = small_full.transpose(1, 0, 2).reshape(16, e)
    conv_w = [_Block(small_full.reshape(2, 8, e), j) for j in range(2)]
    scale_w = [_Block(small_full.reshape(16, 1, e), 8 * j + 3) for j in range(2)]

    p3 = p.reshape(DEPTH, s, PLE_DIM)
    mix_gain = [_Block(norm_mix.reshape(DEPTH, 1, d), i) for i in range(DEPTH)]
    ple_gain = [_Block(ple_norm.reshape(DEPTH, 1, d), i) for i in range(DEPTH)]

    h = x.reshape(s, d)
    saved = []
    for i in range(DEPTH):
        j = i // 2
        rider = gather_rider(f"mix{i}")
        ple = (p3, i, ple_gain[i], weight("ple_w_gate", i), weight("ple_w_proj", i)) if i > 0 else None
        if i % 2 == 0:
            outs, landed = _fwd_mix_a(h, mix_gain[i], conv_w[j], weight("a_w_in", j), weight("a_w_out", j),
                                      f"fwd_mix_a{j}", rider, ple)
            mix = dict(proj=outs[1])
        else:
            outs, landed = _fwd_mix_b(h, mix_gain[i], scale_w[j], weight("b_w_in", j), weight("b_w_grp", j),
                                      weight("b_w_out", j), f"fwd_mix_b{j}", rider, ple)
            mix = dict(z=outs[1], mx=outs[2], diff=outs[3])
        keep(f"mix{i}", landed)
        h1 = outs[0]
        if ple:
            h2, gate = outs[-2:]
        else:
            (h2, gate), landed = _fwd_ple(h1, p3, ple_gain[i], weight("ple_w_gate", i), weight("ple_w_proj", i), i,
                                          gather_rider(f"ple{i}"))
            keep(f"ple{i}", landed)
        saved.append(dict(h=h, h1=h1, gate=gate, **mix))
        h = h2

    n_slots = {nm: weights[nm].shape[0] for nm in BIG}
    stacks = {nm: None for nm in BIG}

    class Group:
        def __init__(self, keys, grads):
            self.keys, self.stage = keys, 0
            self.g32 = [g.reshape(4, 2, w2d[nm].shape[1] // 2, w2d[nm].shape[2]) for (nm, _), g in zip(keys, grads)]

        def rider(self):
            if self.stage == 0:
                return _pair_rider(self.g32)
            if self.stage == 1:
                return _ici_rider(self.pair_sums)
            return _final_rider([stacks[nm] for nm, _ in self.keys], [j for _, j in self.keys])

        def jobs_after(self, landed):
            if self.stage == 0:
                self.from_sibling = landed
                return [_pair_sum_job(g, sb) for g, sb in zip(self.g32, landed)]
            if self.stage == 1:
                return [_final_sum_job(g, sb, ld, stacks[nm], j, n_slots[nm])
                        for (nm, j), g, sb, ld in zip(self.keys, self.g32, self.from_sibling, landed)]
            return []

        def advance(self, landed, summed):
            if self.stage == 0:
                self.pair_sums = summed
            else:
                for (nm, _), a in zip(self.keys, summed if self.stage == 1 else landed):
                    stacks[nm] = a
            self.stage += 1

    active = []
    batches = [0]

    def riders_now():
        parts = [g.rider() for g in active]
        return parts, _merge(parts)

    def advance_all(parts, landed):
        groups = list(active)
        pieces = _split(landed, parts)
        jobs = [g.jobs_after(l) for g, l in zip(groups, pieces)]
        flat = sum(jobs, [])
        outs = [o[0] for o in _run_jobs(flat, place, f"reduce_sums{batches[0]}")] if flat else []
        batches[0] += 1
        for g, l, jb in zip(groups, pieces, jobs):
            g.advance(l, outs[:len(jb)])
            outs = outs[len(jb):]
            if g.stage == 3:
                active.remove(g)

    d_mix_gain, d_ple_gain = [None] * DEPTH, [None] * DEPTH
    d_conv, d_scale = [None] * 2, [None] * 2
    for i in reversed(range(DEPTH)):
        j = i // 2
        sv = saved[i]
        parts, rider = riders_now()
        if i == DEPTH - 1:
            (dh1, d_ple_gain[i], dwg, dwp, loss_part, d_final), landed = _bwd_ple(
                h, sv["h1"], sv["gate"], p3, ple_gain[i], weight("ple_w_gate", i), weight("ple_w_proj", i), i, rider,
                loss_head=(loss_target.reshape(s, d), final_norm.reshape(1, d)))
        else:
            (dh1, d_ple_gain[i], dwg, dwp), landed = _bwd_ple(
                dh, sv["h1"], sv["gate"], p3, ple_gain[i], weight("ple_w_gate", i), weight("ple_w_proj", i), i, rider)
        advance_all(parts, landed)
        active.append(Group([("ple_w_gate", i), ("ple_w_proj", i)], [dwg, dwp]))
        parts, rider = riders_now()
        if i == 0:
            (dproj, d_conv[0], dwin, dwout), landed = _bwd_mix_a_weights(
                dh1, sv["h"], sv["proj"], mix_gain[0], conv_w[0], weight("a_w_out", 0), "bwd_mix_a0_weights", rider)
            advance_all(parts, landed)
            active.append(Group([("a_w_in", 0), ("a_w_out", 0)], [dwin, dwout]))
            parts, rider = riders_now()
            advance_all(parts, _run_rider(rider, "pair_exchange0"))
            parts, rider = riders_now()
            (dh, d_mix_gain[0]), landed = _bwd_mix_a_input(dproj, sv["h"], dh1, mix_gain[0], weight("a_w_in", 0),
                                                          "bwd_mix_a0_input", rider)
            advance_all(parts, landed)
            continue
        if i % 2 == 0:
            (dh, d_conv[j], d_mix_gain[i], dwin, dwout), landed = _bwd_mix_a(
                dh1, sv["h"], sv["proj"], mix_gain[i], conv_w[j], weight("a_w_in", j), weight("a_w_out", j),
                f"bwd_mix_a{j}", rider)
            new = Group([("a_w_in", j), ("a_w_out", j)], [dwin, dwout])
        else:
            (dh, d_scale[j], d_mix_gain[i], dwin, dwgrp, dwout), landed = _bwd_mix_b(
                dh1, sv["h"], sv["z"], sv["mx"], sv["diff"], mix_gain[i], scale_w[j], weight("b_w_in", j),
                weight("b_w_grp", j), weight("b_w_out", j), f"bwd_mix_b{j}", rider)
            new = Group([("b_w_in", j), ("b_w_grp", j), ("b_w_out", j)], [dwin, dwgrp, dwout])
        advance_all(parts, landed)
        active.append(new)
    grad_x = dh.reshape(1, s, d)

    def padded(pieces):
        n = sum(a.shape[0] for a in pieces)
        return pieces + [jnp.zeros((PACK_GROUP - n, d), F32)]

    pack = jnp.concatenate(
        padded(d_mix_gain) + padded(d_ple_gain) + padded([d_final]) + padded([d_conv[0][0:3], d_conv[1][0:3]])
        + padded(d_scale) + padded([jnp.tile(loss_part[0:1], (1, d // 128))]), axis=0)
    vectors = _vector_rider(pack)
    tail = 0
    while active:
        parts, _ = riders_now()
        extra = [vectors] if tail == 0 else []
        landed = _run_rider(_merge(parts + extra), f"tail_exchange{tail}")
        if extra:
            g_mix, g_ple, g_final, g_conv, g_scale, loss_row = _vector_sum(
                _split(landed, parts + extra)[-1][0], [DEPTH, DEPTH, 1, 6, 2, 1])
        advance_all(parts, landed)
        tail += 1
    loss = loss_row[0, 0]

    mine = lambda a: lax.dynamic_slice_in_dim(a, chip * (e // 4), e // 4, axis=1)
    row = lambda a: a.reshape(1, d)
    taps = lambda a: a.reshape(6, e // 4)
    flat = {nm: (w2d[nm].shape[0] * w2d[nm].shape[1], w2d[nm].shape[2]) for nm in BIG}
    tensors = {nm: (stacks[nm].reshape(flat[nm]), w2d[nm].reshape(flat[nm]), _as_2d(nm, moms[nm]).reshape(flat[nm]),
                    _as_2d(nm, vars_[nm]).reshape(flat[nm])) for nm in BIG}
    tensors.update(
        norm_mix=(g_mix, norm_mix, m_norm_mix, v_norm_mix), ple_norm=(g_ple, ple_norm, m_ple_norm, v_ple_norm),
        final_norm=(g_final, row(final_norm), row(m_final_norm), row(v_final_norm)),
        a_w_conv=(mine(g_conv), taps(a_w_conv), taps(m_a_w_conv), taps(v_a_w_conv)),
        b_scale=(mine(g_scale), b_scale, m_b_scale, v_b_scale))
    order = ["norm_mix", "a_w_in", "a_w_conv", "a_w_out", "b_w_in", "b_w_grp", "b_scale", "b_w_out", "ple_norm",
             "ple_w_gate", "ple_w_proj", "final_norm"]
    shapes = dict(norm_mix=norm_mix.shape, ple_norm=ple_norm.shape, final_norm=final_norm.shape,
                  a_w_conv=a_w_conv.shape, b_scale=b_scale.shape, **{nm: weights[nm].shape for nm in BIG})
    updates = {nm: _run_jobs([_adamw_job(*tensors[nm], ADAMW_BIG_BLOCK_BYTES)], place, f"adamw_{nm}")[0]
               for nm in ADAMW_ALONE}
    rest = [nm for nm in order if nm not in ADAMW_ALONE]
    updates.update(zip(rest, _run_jobs([_adamw_job(*tensors[nm], ADAMW_BLOCK_BYTES) for nm in rest], place, "adamw_rest")))
    outs = [loss, grad_x]
    for which in (3, 0, 1, 2):
        outs += [updates[nm][which].reshape(shapes[nm]) for nm in order]
    return tuple(outs)
```

```python
import jax
import jax.numpy as jnp
from jax import lax
from jax.experimental import pallas as pl
from jax.experimental.pallas import tpu as pltpu

F32 = jnp.float32
BF16 = jnp.bfloat16
MESH = pl.DeviceIdType.MESH

D_MODEL = 1024
MIX_WIDTH = 1024
PLE_DIM = 256
N_GROUPS = 4
GROUP_DIM = 256
POOL_WINDOWS = (2, 4, 8, 16)
DEPTH = 4
EPS = 1e-6

ADAM_LR = 0.001
ADAM_B1 = 0.9
ADAM_B2 = 0.999
ADAM_EPS = 1e-08
ADAM_WD = 0.01
ADAM_STEP = 10

HALO = 8
TS_MIX = 256
TS_FWD = 512
TS_PLE = 512
VMEM_LIMIT = 56 * 1024 * 1024
PACK_GROUP = 8
JOB_BLOCK_BYTES = 1024 * 1024
ADAMW_BLOCK_BYTES = 512 * 1024
ADAMW_BIG_BLOCK_BYTES = 2 * 1024 * 1024
ADAMW_ALONE = ("a_w_in", "b_w_in", "ple_w_gate")
MIDDLE_STEPS_BEFORE_END = 1

ANY = pl.BlockSpec(memory_space=pl.ANY)


def _sds(shape, dtype):
    return jax.ShapeDtypeStruct(shape, dtype)


def _full(shape):
    nd = len(shape)
    return pl.BlockSpec(shape, lambda *_: (0,) * nd)


def _params(n_axes=1):
    return pltpu.CompilerParams(dimension_semantics=("arbitrary",) * n_axes, vmem_limit_bytes=VMEM_LIMIT)


def _dot(a, b):
    return jnp.dot(a, b, preferred_element_type=F32)


def _dot_nt(a, b):
    return lax.dot_general(a, b, (((1,), (1,)), ((), ())), preferred_element_type=F32)


def _dot_tn(a, b):
    return lax.dot_general(a, b, (((0,), (0,)), ((), ())), preferred_element_type=F32)


def _sigmoid(z):
    return 1.0 / (1.0 + jnp.exp(-z))


def _shift_down(x, k, tail):
    rolled = pltpu.roll(x, k, 0)
    rt = tail if k % HALO == 0 else pltpu.roll(tail, k % HALO, 0)
    row = lax.broadcasted_iota(jnp.int32, rt.shape, 0)
    head = jnp.where(row < k, rt, rolled[0:HALO])
    return jnp.concatenate([head, rolled[HALO:]], axis=0)


def _shift_up(x, k, head_next):
    n = x.shape[0]
    rolled = pltpu.roll(x, n - k, 0)
    rh = head_next if k % HALO == 0 else pltpu.roll(head_next, HALO - k % HALO, 0)
    row = lax.broadcasted_iota(jnp.int32, rh.shape, 0)
    tail = jnp.where(row >= HALO - k, rh, rolled[n - HALO:n])
    return jnp.concatenate([rolled[:n - HALO], tail], axis=0)


def _inv_counts(tile, ts):
    t = tile * ts + lax.broadcasted_iota(jnp.int32, (ts, 1), 0)
    return [1.0 / jnp.minimum(t + 1, w).astype(F32) for w in POOL_WINDOWS]


def _pool_fwd(u, carry, tile, ts):
    inv = _inv_counts(tile, ts)
    outs = []
    for g, w in enumerate(POOL_WINDOWS):
        cols = slice(g * GROUP_DIM, (g + 1) * GROUP_DIM)
        s = u[:, cols]
        level, k = 0, 1
        while k < w:
            tail = carry[level, :, cols]
            carry[level, :, cols] = s[ts - HALO:ts]
            s = s + _shift_down(s, k, tail)
            level, k = level + 1, k * 2
        outs.append(s * inv[g])
    return jnp.concatenate(outs, axis=1)


def _pool_bwd(dd, carry, tile, ts):
    inv = _inv_counts(tile, ts)
    outs = []
    for g, w in enumerate(POOL_WINDOWS):
        cols = slice(g * GROUP_DIM, (g + 1) * GROUP_DIM)
        q = dd[:, cols] * inv[g]
        level, k = 0, 1
        while k < w:
            head = carry[level, :, cols]
            carry[level, :, cols] = q[0:HALO]
            q = q + _shift_up(q, k, head)
            level, k = level + 1, k * 2
        outs.append(q)
    return jnp.concatenate(outs, axis=1)


def _copy_all(pairs, sems):
    copies = [pltpu.make_async_copy(src, dst, sems.at[n]) for n, (src, dst) in enumerate(pairs)]
    for cp in copies:
        cp.start()
    for cp in copies:
        cp.wait()


def _grp_pairs(wgrp_hbm, wgrp_v):
    rows = GROUP_DIM // 4
    return [(wgrp_hbm.at[k, g], wgrp_v.at[g, pl.ds(k * rows, rows), :]) for k in range(4) for g in range(N_GROUPS)]


def _rms(h):
    r = lax.rsqrt(jnp.mean(h * h, axis=-1, keepdims=True) + EPS)
    return h * r, r


def _rms_bwd(dhn, xn, r, gain):
    dgain = jnp.sum(dhn * xn, axis=0, keepdims=True)
    dxn = dhn * gain
    dh = r * (dxn - xn * jnp.mean(dxn * xn, axis=-1, keepdims=True))
    return dh, dgain


class _Rider:
    def __init__(self, inputs, out_shapes, n_sems, start, finish, middle=None, aliases=None):
        self.inputs, self.out_shapes, self.n_sems = list(inputs), list(out_shapes), n_sems
        self.start, self.middle, self.finish = start, middle, finish
        self.aliases = dict(aliases or {})


def _merge(riders):
    riders = [r for r in riders if r is not None]
    if not riders:
        return None
    if len(riders) == 1:
        return riders[0]

    def phase(which):
        def run(rin, rout, send, recv, base=0):
            i0 = o0 = s0 = 0
            for r in riders:
                fn = getattr(r, which)
                if fn is not None:
                    fn(rin[i0:i0 + len(r.inputs)], rout[o0:o0 + len(r.out_shapes)], send, recv, base + s0)
                i0, o0, s0 = i0 + len(r.inputs), o0 + len(r.out_shapes), s0 + r.n_sems
        return run

    aliases, i0, o0 = {}, 0, 0
    for r in riders:
        aliases.update({i0 + a: o0 + b for a, b in r.aliases.items()})
        i0, o0 = i0 + len(r.inputs), o0 + len(r.out_shapes)
    return _Rider(sum([r.inputs for r in riders], []), sum([r.out_shapes for r in riders], []),
                  sum(r.n_sems for r in riders), phase("start"), phase("finish"),
                  phase("middle") if any(r.middle for r in riders) else None, aliases)


def _split(landed, riders):
    out, o0 = [], 0
    for r in riders:
        if r is None:
            out.append(None)
        else:
            out.append(landed[o0:o0 + len(r.out_shapes)])
            o0 += len(r.out_shapes)
    return out


def _place():
    x, y, c = lax.axis_index("x"), lax.axis_index("y"), lax.axis_index("c")
    chips = [(1 - x, y), (x, 1 - y), (1 - x, 1 - y)]
    return x, y, c, chips


def _remote(src, dst, send_sems, recv_sems, sem, to):
    return pltpu.make_async_remote_copy(src_ref=src, dst_ref=dst, send_sem=send_sems.at[sem], recv_sem=recv_sems.at[sem],
                                        device_id=to, device_id_type=MESH)


def _gather_rider(stacked, slots):
    ni = len(stacked)

    def first_hops(rin, rout, send, recv, base, x, y, c, chips):
        me = 2 * x + y
        return [_remote(rin[t].at[slots[t], c], rout[t].at[me, c], send, recv, base + 7 * t + j, (cx, cy, c))
                for j, (cx, cy) in enumerate(chips) for t in range(ni)]

    def passes(rout, send, recv, base, x, y, c, chips):
        out = []
        for j, (cx, cy) in enumerate(chips):
            for t in range(ni):
                landed = rout[t].at[2 * cx + cy, c]
                out.append((_remote(landed, landed, send, recv, base + 7 * t + j, (x, y, 1 - c)),
                            _remote(landed, landed, send, recv, base + 7 * t + 3 + j, (x, y, 1 - c))))
        return out

    def own(rin, rout, send, recv, base, x, y, c):
        return [_remote(rin[t].at[slots[t]], rout[t].at[2 * x + y], send, recv, base + 7 * t + 6, (x, y, 1 - c))
                for t in range(ni)]

    def start(rin, rout, send, recv, base=0):
        x, y, c, chips = _place()
        for cp in first_hops(rin, rout, send, recv, base, x, y, c, chips) + own(rin, rout, send, recv, base, x, y, c):
            cp.start()

    def middle(rin, rout, send, recv, base=0):
        x, y, c, chips = _place()
        for arrival, onward in passes(rout, send, recv, base, x, y, c, chips):
            arrival.wait_recv()
            onward.start()

    def finish(rin, rout, send, recv, base=0):
        x, y, c, chips = _place()
        for j, (cx, cy) in enumerate(chips):
            for t in range(ni):
                other = rout[t].at[2 * cx + cy, 1 - c]
                _remote(other, other, send, recv, base + 7 * t + 3 + j, (x, y, 1 - c)).wait_recv()
        for cp in own(rin, rout, send, recv, base, x, y, c):
            cp.wait_recv()
            cp.wait_send()
        for cp in first_hops(rin, rout, send, recv, base, x, y, c, chips):
            cp.wait_send()
        for _, onward in passes(rout, send, recv, base, x, y, c, chips):
            onward.wait_send()

    return _Rider(stacked, [_sds((4,) + a.shape[1:], a.dtype) for a in stacked], 7 * ni, start, finish, middle)


def _pair_rider(grads):
    ni = len(grads)

    def copies(rin, rout, send, recv, base):
        x, y, c, _ = _place()
        return [_remote(rin[t].at[:, 1 - c], rout[t], send, recv, base + t, (x, y, 1 - c)) for t in range(ni)]

    def start(rin, rout, send, recv, base=0):
        for cp in copies(rin, rout, send, recv, base):
            cp.start()

    def finish(rin, rout, send, recv, base=0):
        for cp in copies(rin, rout, send, recv, base):
            cp.wait()

    return _Rider(grads, [_sds(g.shape[:1] + g.shape[2:], g.dtype) for g in grads], ni, start, finish)


def _ici_rider(pair_sums):
    ni = len(pair_sums)

    def copies(rin, rout, send, recv, base):
        x, y, c, chips = _place()
        return [_remote(rin[t].at[j], rout[t].at[j], send, recv, base + 3 * t + j, (cx, cy, c))
                for j, (cx, cy) in enumerate(chips) for t in range(ni)]

    def start(rin, rout, send, recv, base=0):
        for cp in copies(rin, rout, send, recv, base):
            cp.start()

    def finish(rin, rout, send, recv, base=0):
        for cp in copies(rin, rout, send, recv, base):
            cp.wait()

    return _Rider(pair_sums, [_sds((3,) + g.shape[1:], g.dtype) for g in pair_sums], 3 * ni, start, finish)


def _final_rider(summed, slots):
    ni = len(summed)

    def copies(rout, send, recv, base):
        x, y, c, _ = _place()
        return [(_remote(rout[t].at[slots[t], c], rout[t].at[slots[t], c], send, recv, base + t, (x, y, 1 - c)),
                 _remote(rout[t].at[slots[t], 1 - c], rout[t].at[slots[t], 1 - c], send, recv, base + t, (x, y, 1 - c)))
                for t in range(ni)]

    def start(rin, rout, send, recv, base=0):
        for mine, _ in copies(rout, send, recv, base):
            mine.start()

    def finish(rin, rout, send, recv, base=0):
        for mine, theirs in copies(rout, send, recv, base):
            mine.wait_send()
            theirs.wait_recv()

    return _Rider(summed, [_sds(a.shape, a.dtype) for a in summed], ni, start, finish,
                  aliases={t: t for t in range(ni)})


class _Block:
    def __init__(self, array, index):
        self.array, self.index = array, index

    def spec(self):
        index = self.index
        return pl.BlockSpec((None,) + self.array.shape[1:], lambda *_: (index, 0, 0))


def _call(body, *, name, grid, in_specs, out_specs, out_shape, scratch_shapes, operands, rider=None):
    operands, in_specs = list(operands), list(in_specs)
    for n, op in enumerate(operands):
        if isinstance(op, _Block):
            operands[n], in_specs[n] = op.array, op.spec()
    if rider is None:
        outs = pl.pallas_call(body, name=name, grid=grid, in_specs=in_specs, out_specs=out_specs, out_shape=out_shape,
                              scratch_shapes=scratch_shapes, compiler_params=_params(len(grid)))(*operands)
        return list(outs), []
    n_in, n_out, n_scr = len(in_specs), len(out_specs), len(scratch_shapes)
    r_in, r_out = len(rider.inputs), len(rider.out_shapes)
    steps = 1
    for g in grid:
        steps *= g
    mid = max(steps - 1 - MIDDLE_STEPS_BEFORE_END, 0)

    def full_body(*refs):
        own_in, rin = refs[:n_in], refs[n_in:n_in + r_in]
        own_out = refs[n_in + r_in:n_in + r_in + n_out]
        rout = refs[n_in + r_in + n_out:n_in + r_in + n_out + r_out]
        own_scr = refs[n_in + r_in + n_out + r_out:n_in + r_in + n_out + r_out + n_scr]
        send, recv = refs[-2], refs[-1]
        step = pl.program_id(0)
        for axis in range(1, len(grid)):
            step = step * grid[axis] + pl.program_id(axis)

        @pl.when(step == 0)
        def _():
            rider.start(rin, rout, send, recv)

        body(*own_in, *own_out, *own_scr)

        if rider.middle is not None:
            @pl.when(step == mid)
            def _():
                rider.middle(rin, rout, send, recv)

        @pl.when(step == steps - 1)
        def _():
            rider.finish(rin, rout, send, recv)

    outs = pl.pallas_call(
        full_body, name=name, grid=grid,
        in_specs=list(in_specs) + [ANY] * r_in, out_specs=list(out_specs) + [ANY] * r_out,
        out_shape=list(out_shape) + rider.out_shapes,
        scratch_shapes=list(scratch_shapes) + [pltpu.SemaphoreType.DMA((rider.n_sems,)), pltpu.SemaphoreType.DMA((rider.n_sems,))],
        input_output_aliases={n_in + a: n_out + b for a, b in rider.aliases.items()},
        compiler_params=_params(len(grid)),
    )(*operands, *rider.inputs)
    return list(outs[:n_out]), list(outs[n_out:])


def _run_rider(rider, name):
    r_in, r_out = len(rider.inputs), len(rider.out_shapes)

    def body(*refs):
        rin, rout, send, recv = refs[:r_in], refs[r_in:r_in + r_out], refs[-2], refs[-1]
        rider.start(rin, rout, send, recv)
        if rider.middle is not None:
            rider.middle(rin, rout, send, recv)
        rider.finish(rin, rout, send, recv)

    outs = pl.pallas_call(
        body, name=name, in_specs=[ANY] * r_in, out_specs=[ANY] * r_out, out_shape=rider.out_shapes,
        scratch_shapes=[pltpu.SemaphoreType.DMA((rider.n_sems,)), pltpu.SemaphoreType.DMA((rider.n_sems,))],
        input_output_aliases=rider.aliases,
    )(*rider.inputs)
    return list(outs)


def _ple_tile(h1, p_ref, gain_ref, wg_v, wp_v):
    xn, _ = _rms(h1)
    hpb = (xn * gain_ref[...]).astype(BF16)
    gate = _sigmoid(_dot(hpb, wg_v[...]))
    pb = p_ref[...].astype(BF16)
    pe = jnp.concatenate([_dot(pb, wp_v[k]) for k in range(4)], axis=1)
    return h1 + gate * pe, gate


def _ple_parts(ple, ts, d):
    p, layer, gain, w_gate, w_proj = ple
    s, pd = p.shape[1:]
    row = pl.BlockSpec((ts, d), lambda i: (i, 0))
    return dict(
        operands=[p, gain, w_gate, w_proj],
        in_specs=[pl.BlockSpec((None, ts, pd), lambda i: (layer, i, 0)), _full((1, d)), ANY, ANY],
        out_specs=[row, row], out_shape=[_sds((s, d), F32), _sds((s, d), BF16)],
        scratch=[pltpu.VMEM((d, d), BF16), pltpu.VMEM((4, pd, d // 4), BF16)])


def _fwd_mix_a(h, gain, conv_w, w_in, w_out, name, rider=None, ple=None):
    s, d = h.shape
    e = MIX_WIDTH
    ts = min(TS_FWD, s)
    nt = s // ts
    extra = _ple_parts(ple, ts, d) if ple else None

    def body(*refs):
        h_ref, gain_ref, cw_ref, win_hbm, wout_hbm = refs[:5]
        n_in = 9 if ple else 5
        h1_ref, proj_ref = refs[n_in:n_in + 2]
        win_v, wout_v, carry, sems = refs[n_in + (4 if ple else 2):][:4]
        i = pl.program_id(0)

        @pl.when(i == 0)
        def _():
            loads = [(win_hbm, win_v), (wout_hbm, wout_v)]
            if ple:
                loads += [(refs[7], refs[-2]), (refs[8], refs[-1])]
            _copy_all(loads, sems)
            carry[...] = jnp.zeros_like(carry)

        hh = h_ref[...]
        xn, _ = _rms(hh)
        hnb = (xn * gain_ref[...]).astype(BF16)
        b = _dot(hnb, win_v[0])
        c = _dot(hnb, win_v[1])
        v = _dot(hnb, win_v[2])
        z = _dot(hnb, win_v[3])
        proj_ref[:, 0 * e:1 * e] = b.astype(BF16)
        proj_ref[:, 1 * e:2 * e] = c.astype(BF16)
        proj_ref[:, 2 * e:3 * e] = v.astype(BF16)
        proj_ref[:, 3 * e:4 * e] = z.astype(BF16)
        cv = c * v
        tail = carry[...]
        carry[...] = cv[ts - HALO:ts]
        conv = cw_ref[0:1, :] * _shift_down(cv, 2, tail) + cw_ref[1:2, :] * _shift_down(cv, 1, tail) + cw_ref[2:3, :] * cv
        mb = ((z * _sigmoid(z)) * (b * conv)).astype(BF16)
        h1 = hh + _dot(mb, wout_v[...])
        h1_ref[...] = h1
        if ple:
            h2, gate = _ple_tile(h1, refs[5], refs[6], refs[-2], refs[-1])
            refs[n_in + 2][...] = h2
            refs[n_in + 3][...] = gate.astype(BF16)

    row = lambda width: pl.BlockSpec((ts, width), lambda i: (i, 0))
    return _call(
        body, name=name, grid=(nt,),
        in_specs=[row(d), _full((1, d)), _full((8, e)), ANY, ANY] + (extra["in_specs"] if ple else []),
        out_specs=[row(d), row(4 * e)] + (extra["out_specs"] if ple else []),
        out_shape=[_sds((s, d), F32), _sds((s, 4 * e), BF16)] + (extra["out_shape"] if ple else []),
        scratch_shapes=[pltpu.VMEM((4, d, e), BF16), pltpu.VMEM((e, d), BF16), pltpu.VMEM((HALO, e), F32),
                        pltpu.SemaphoreType.DMA((4,))] + (extra["scratch"] if ple else []),
        operands=[h, gain, conv_w, w_in, w_out] + (extra["operands"] if ple else []), rider=rider)


def _fwd_mix_b(h, gain, scale, w_in, w_grp, w_out, name, rider=None, ple=None):
    s, d = h.shape
    e = MIX_WIDTH
    ts = min(TS_FWD, s)
    nt = s // ts
    extra = _ple_parts(ple, ts, d) if ple else None

    def body(*refs):
        h_ref, gain_ref, scale_ref, win_hbm, wgrp_hbm, wout_hbm = refs[:6]
        n_in = 10 if ple else 6
        h1_ref, z_ref, mx_ref, dd_ref = refs[n_in:n_in + 4]
        win_v, wgrp_v, wout_v, carry, sems = refs[n_in + (6 if ple else 4):][:5]
        i = pl.program_id(0)

        @pl.when(i == 0)
        def _():
            loads = [(win_hbm, win_v), (wout_hbm, wout_v)] + _grp_pairs(wgrp_hbm, wgrp_v)
            if ple:
                loads += [(refs[8], refs[-2]), (refs[9], refs[-1])]
            _copy_all(loads, sems)
            carry[...] = jnp.zeros_like(carry)

        hh = h_ref[...]
        xn, _ = _rms(hh)
        hnb = (xn * gain_ref[...]).astype(BF16)
        u = jnp.concatenate([_dot(hnb, win_v[0]), _dot(hnb, win_v[1])], axis=1)
        z = jnp.concatenate([_dot(hnb, win_v[2]), _dot(hnb, win_v[3])], axis=1)
        z_ref[...] = z.astype(BF16)
        diff = (_pool_fwd(u, carry, i, ts) - u).astype(BF16)
        dd_ref[...] = diff
        mx = jnp.concatenate(
            [_dot(diff[:, g * GROUP_DIM:(g + 1) * GROUP_DIM], wgrp_v[g]) for g in range(N_GROUPS)], axis=1)
        mx_ref[...] = mx.astype(BF16)
        mb = ((z * _sigmoid(z)) * (mx * scale_ref[...])).astype(BF16)
        h1 = hh + _dot(mb, wout_v[...])
        h1_ref[...] = h1
        if ple:
            h2, gate = _ple_tile(h1, refs[6], refs[7], refs[-2], refs[-1])
            refs[n_in + 4][...] = h2
            refs[n_in + 5][...] = gate.astype(BF16)

    row = lambda width: pl.BlockSpec((ts, width), lambda i: (i, 0))
    return _call(
        body, name=name, grid=(nt,),
        in_specs=[row(d), _full((1, d)), _full((1, e)), ANY, ANY, ANY] + (extra["in_specs"] if ple else []),
        out_specs=[row(d), row(e), row(e), row(e)] + (extra["out_specs"] if ple else []),
        out_shape=[_sds((s, d), F32)] + [_sds((s, e), BF16)] * 3 + (extra["out_shape"] if ple else []),
        scratch_shapes=[pltpu.VMEM((4, d, e // 2), BF16), pltpu.VMEM((N_GROUPS, GROUP_DIM, GROUP_DIM), BF16),
                        pltpu.VMEM((e, d), BF16), pltpu.VMEM((4, HALO, e), F32), pltpu.SemaphoreType.DMA((20,))]
        + (extra["scratch"] if ple else []),
        operands=[h, gain, scale, w_in, w_grp, w_out] + (extra["operands"] if ple else []), rider=rider)


def _fwd_ple(h1, p, gain, w_gate, w_proj, layer, rider=None):
    s, d = h1.shape
    pd = p.shape[-1]
    ts = min(TS_PLE, s)
    nt = s // ts

    def body(h1_ref, p_ref, gain_ref, wg_hbm, wp_hbm, h2_ref, gate_ref, wg_v, wp_v, sems):
        @pl.when(pl.program_id(0) == 0)
        def _():
            _copy_all([(wg_hbm, wg_v), (wp_hbm, wp_v)], sems)

        hh = h1_ref[...]
        xn, _ = _rms(hh)
        hpb = (xn * gain_ref[...]).astype(BF16)
        gate = _sigmoid(_dot(hpb, wg_v[...]))
        pb = p_ref[...].astype(BF16)
        pe = jnp.concatenate([_dot(pb, wp_v[k]) for k in range(4)], axis=1)
        gate_ref[...] = gate.astype(BF16)
        h2_ref[...] = hh + gate * pe

    row = lambda width: pl.BlockSpec((ts, width), lambda i: (i, 0))
    return _call(
        body, name=f"fwd_ple{layer}", grid=(nt,),
        in_specs=[row(d), pl.BlockSpec((None, ts, pd), lambda i: (layer, i, 0)), _full((1, d)), ANY, ANY],
        out_specs=[row(d), row(d)],
        out_shape=[_sds((s, d), F32), _sds((s, d), BF16)],
        scratch_shapes=[pltpu.VMEM((d, d), BF16), pltpu.VMEM((4, pd, d // 4), BF16), pltpu.SemaphoreType.DMA((2,))],
        operands=[h1, p, gain, w_gate, w_proj], rider=rider)


def _bwd_ple(dh2, h1, gate, p, gain, w_gate, w_proj, layer, rider=None, loss_head=None):
    s, d = dh2.shape
    pd = p.shape[-1]
    ts = min(TS_PLE, s)
    nt = s // ts
    qd = d // 4
    n_head = 0 if loss_head is None else 2

    def body(*refs):
        dh2_ref = refs[0]
        h1_ref, gate_ref, p_ref, gain_ref, wg_hbm, wp_hbm, dh1_ref, dgain_ref, dwg_hbm, dwp_hbm = refs[1 + n_head:11 + n_head]
        wg_v, wp_v, acc_g, acc_p, sems = refs[-5:]
        i = pl.program_id(0)

        @pl.when(i == 0)
        def _():
            _copy_all([(wg_hbm, wg_v), (wp_hbm, wp_v)], sems)
            dgain_ref[...] = jnp.zeros_like(dgain_ref)
            acc_g[...] = jnp.zeros_like(acc_g)
            acc_p[...] = jnp.zeros_like(acc_p)

        if loss_head is None:
            g2 = dh2_ref[...]
        else:
            t_ref, fgain_ref, loss_ref, dfgain_ref = refs[1], refs[2], refs[11 + n_head], refs[12 + n_head]

            @pl.when(i == 0)
            def _():
                loss_ref[...] = jnp.zeros_like(loss_ref)
                dfgain_ref[...] = jnp.zeros_like(dfgain_ref)

            xf, rf = _rms(dh2_ref[...])
            err = xf * fgain_ref[...] - t_ref[...]
            part = 0.5 * jnp.sum(jnp.mean(err * err, axis=-1, keepdims=True), axis=0, keepdims=True)
            loss_ref[...] += jnp.broadcast_to(part, loss_ref.shape)
            g2, dfgain = _rms_bwd(err * (1.0 / d), xf, rf, fgain_ref[...])
            dfgain_ref[...] += dfgain
        gate_f = gate_ref[...].astype(F32)
        xn, r = _rms(h1_ref[...])
        hpb = (xn * gain_ref[...]).astype(BF16)
        pb = p_ref[...].astype(BF16)
        pe = jnp.concatenate([_dot(pb, wp_v[k]) for k in range(4)], axis=1)
        dpeb = (g2 * gate_f).astype(BF16)
        dab = ((g2 * pe) * (gate_f * (1.0 - gate_f))).astype(BF16)
        acc_g[...] += _dot_tn(hpb, dab)
        for k in range(4):
            acc_p[k] += _dot_tn(pb, dpeb[:, k * qd:(k + 1) * qd])
        dhp = _dot_nt(dab, wg_v[...])
        dh, dgain = _rms_bwd(dhp, xn, r, gain_ref[...])
        dh1_ref[...] = g2 + dh
        dgain_ref[...] += dgain

        @pl.when(i == nt - 1)
        def _():
            _copy_all([(acc_g, dwg_hbm), (acc_p, dwp_hbm)], sems)

    row = pl.BlockSpec((ts, d), lambda i: (i, 0))
    head = loss_head is not None
    return _call(
        body, name=f"bwd_ple{layer}", grid=(nt,),
        in_specs=[row] + ([row, _full((1, d))] if head else [])
        + [row, row, pl.BlockSpec((None, ts, pd), lambda i: (layer, i, 0)), _full((1, d)), ANY, ANY],
        out_specs=[row, _full((1, d)), ANY, ANY] + ([_full((8, 128)), _full((1, d))] if head else []),
        out_shape=[_sds((s, d), F32), _sds((1, d), F32), _sds((d, d), F32), _sds((4, pd, qd), F32)]
        + ([_sds((8, 128), F32), _sds((1, d), F32)] if head else []),
        scratch_shapes=[pltpu.VMEM((d, d), BF16), pltpu.VMEM((4, pd, qd), BF16), pltpu.VMEM((d, d), F32),
                        pltpu.VMEM((4, pd, qd), F32), pltpu.SemaphoreType.DMA((2,))],
        operands=[dh2] + (list(loss_head) if head else []) + [h1, gate, p, gain, w_gate, w_proj], rider=rider)


def _mix_a_tile_grads(proj_ref, ch_ref, vh_ref, cw_ref, dh1b, wout_v, carry, dcw_ref, tile, hb):
    e = MIX_WIDTH
    b = proj_ref[:, 0 * e:1 * e].astype(F32)
    c = proj_ref[:, 1 * e:2 * e].astype(F32)
    v = proj_ref[:, 2 * e:3 * e].astype(F32)
    z = proj_ref[:, 3 * e:4 * e].astype(F32)
    cv = c * v
    prev = (ch_ref[...].astype(F32) * vh_ref[...].astype(F32))[hb - HALO:hb]
    tail = jnp.where(tile > 0, prev, jnp.zeros_like(prev))
    cv1 = _shift_down(cv, 1, tail)
    cv2 = _shift_down(cv, 2, tail)
    conv = cw_ref[0:1, :] * cv2 + cw_ref[1:2, :] * cv1 + cw_ref[2:3, :] * cv
    sig = _sigmoid(z)
    sz = z * sig
    y = b * conv
    dm = _dot_nt(dh1b, wout_v[...])
    dz = (dm * y) * (sig * (1.0 + z * (1.0 - sig)))
    dy = dm * sz
    db = dy * conv
    dconv = dy * b
    head = carry[...]
    carry[...] = dconv[0:HALO]
    dcv = cw_ref[2:3, :] * dconv + cw_ref[1:2, :] * _shift_up(dconv, 1, head) + cw_ref[0:1, :] * _shift_up(dconv, 2, head)
    dcw_ref[0:1, :] += jnp.sum(dconv * cv2, axis=0, keepdims=True)
    dcw_ref[1:2, :] += jnp.sum(dconv * cv1, axis=0, keepdims=True)
    dcw_ref[2:3, :] += jnp.sum(dconv * cv, axis=0, keepdims=True)
    parts = [db.astype(BF16), (dcv * v).astype(BF16), (dcv * c).astype(BF16), dz.astype(BF16)]
    return parts, (sz * y).astype(BF16)


def _bwd_mix_a(dh1, h, proj, gain, conv_w, w_in, w_out, name, rider=None):
    s, d = dh1.shape
    e = MIX_WIDTH
    ts = min(TS_MIX, s)
    nt = s // ts
    hb = 16
    per = ts // hb

    def body(dh1_ref, h_ref, proj_ref, ch_ref, vh_ref, gain_ref, cw_ref, win_hbm, wout_hbm,
             dh_ref, dcw_ref, dgain_ref, dwin_hbm, dwout_hbm, win_v, wout_v, acc_in, acc_out, carry, sems):
        i = pl.program_id(0)

        @pl.when(i == 0)
        def _():
            _copy_all([(win_hbm, win_v), (wout_hbm, wout_v)], sems)
            carry[...] = jnp.zeros_like(carry)
            dcw_ref[...] = jnp.zeros_like(dcw_ref)
            dgain_ref[...] = jnp.zeros_like(dgain_ref)
            acc_in[...] = jnp.zeros_like(acc_in)
            acc_out[...] = jnp.zeros_like(acc_out)

        dh1 = dh1_ref[...]
        dh1b = dh1.astype(BF16)
        parts, mb = _mix_a_tile_grads(proj_ref, ch_ref, vh_ref, cw_ref, dh1b, wout_v, carry, dcw_ref, nt - 1 - i, hb)
        acc_out[...] += _dot_tn(mb, dh1b)
        xn, r = _rms(h_ref[...])
        hnb = (xn * gain_ref[...]).astype(BF16)
        for q in range(4):
            acc_in[q] += _dot_tn(hnb, parts[q])
        dhn = _dot_nt(parts[0], win_v[0]) + _dot_nt(parts[1], win_v[1]) + _dot_nt(parts[2], win_v[2]) + _dot_nt(parts[3], win_v[3])
        dh, dgain = _rms_bwd(dhn, xn, r, gain_ref[...])
        dh_ref[...] = dh1 + dh
        dgain_ref[...] += dgain

        @pl.when(i == nt - 1)
        def _():
            _copy_all([(acc_in, dwin_hbm), (acc_out, dwout_hbm)], sems)

    row = lambda width: pl.BlockSpec((ts, width), lambda i: (nt - 1 - i, 0))
    halo = lambda col: pl.BlockSpec((hb, e), lambda i: (jnp.maximum((nt - 1 - i) * per - 1, 0), col))
    return _call(
        body, name=name, grid=(nt,),
        in_specs=[row(d), row(d), row(4 * e), halo(1), halo(2), _full((1, d)), _full((8, e)), ANY, ANY],
        out_specs=[row(d), _full((8, e)), _full((1, d)), ANY, ANY],
        out_shape=[_sds((s, d), F32), _sds((8, e), F32), _sds((1, d), F32), _sds((4, d, e), F32), _sds((e, d), F32)],
        scratch_shapes=[pltpu.VMEM((4, d, e), BF16), pltpu.VMEM((e, d), BF16), pltpu.VMEM((4, d, e), F32),
                        pltpu.VMEM((e, d), F32), pltpu.VMEM((HALO, e), F32), pltpu.SemaphoreType.DMA((2,))],
        operands=[dh1, h, proj, proj, proj, gain, conv_w, w_in, w_out], rider=rider)


def _bwd_mix_a_weights(dh1, h, proj, gain, conv_w, w_out, name, rider=None):
    s, d = dh1.shape
    e = MIX_WIDTH
    ts = min(TS_MIX, s)
    nt = s // ts
    hb = 16
    per = ts // hb

    def body(dh1_ref, h_ref, proj_ref, ch_ref, vh_ref, gain_ref, cw_ref, wout_hbm,
             dproj_ref, dcw_ref, dwin_hbm, dwout_hbm, wout_v, acc_in, acc_out, carry, sems):
        i = pl.program_id(0)

        @pl.when(i == 0)
        def _():
            _copy_all([(wout_hbm, wout_v)], sems)
            carry[...] = jnp.zeros_like(carry)
            dcw_ref[...] = jnp.zeros_like(dcw_ref)
            acc_in[...] = jnp.zeros_like(acc_in)
            acc_out[...] = jnp.zeros_like(acc_out)

        dh1b = dh1_ref[...].astype(BF16)
        parts, mb = _mix_a_tile_grads(proj_ref, ch_ref, vh_ref, cw_ref, dh1b, wout_v, carry, dcw_ref, nt - 1 - i, hb)
        acc_out[...] += _dot_tn(mb, dh1b)
        xn, _ = _rms(h_ref[...])
        hnb = (xn * gain_ref[...]).astype(BF16)
        for q in range(4):
            acc_in[q] += _dot_tn(hnb, parts[q])
            dproj_ref[:, q * e:(q + 1) * e] = parts[q]

        @pl.when(i == nt - 1)
        def _():
            _copy_all([(acc_in, dwin_hbm), (acc_out, dwout_hbm)], sems)

    row = lambda width: pl.BlockSpec((ts, width), lambda i: (nt - 1 - i, 0))
    halo = lambda col: pl.BlockSpec((hb, e), lambda i: (jnp.maximum((nt - 1 - i) * per - 1, 0), col))
    return _call(
        body, name=name, grid=(nt,),
        in_specs=[row(d), row(d), row(4 * e), halo(1), halo(2), _full((1, d)), _full((8, e)), ANY],
        out_specs=[row(4 * e), _full((8, e)), ANY, ANY],
        out_shape=[_sds((s, 4 * e), BF16), _sds((8, e), F32), _sds((4, d, e), F32), _sds((e, d), F32)],
        scratch_shapes=[pltpu.VMEM((e, d), BF16), pltpu.VMEM((4, d, e), F32), pltpu.VMEM((e, d), F32),
                        pltpu.VMEM((HALO, e), F32), pltpu.SemaphoreType.DMA((2,))],
        operands=[dh1, h, proj, proj, proj, gain, conv_w, w_out], rider=rider)


def _bwd_mix_a_input(dproj, h, dh1, gain, w_in, name, rider=None):
    s, d = dh1.shape
    e = MIX_WIDTH
    ts = min(TS_PLE, s)
    nt = s // ts

    def body(dproj_ref, h_ref, dh1_ref, gain_ref, win_hbm, dh_ref, dgain_ref, win_v, sems):
        @pl.when(pl.program_id(0) == 0)
        def _():
            _copy_all([(win_hbm, win_v)], sems)
            dgain_ref[...] = jnp.zeros_like(dgain_ref)

        dhn = _dot_nt(dproj_ref[:, 0:e], win_v[0])
        for q in range(1, 4):
            dhn = dhn + _dot_nt(dproj_ref[:, q * e:(q + 1) * e], win_v[q])
        xn, r = _rms(h_ref[...])
        dh, dgain = _rms_bwd(dhn, xn, r, gain_ref[...])
        dh_ref[...] = dh1_ref[...] + dh
        dgain_ref[...] += dgain

    row = lambda width: pl.BlockSpec((ts, width), lambda i: (i, 0))
    return _call(
        body, name=name, grid=(nt,),
        in_specs=[row(4 * e), row(d), row(d), _full((1, d)), ANY],
        out_specs=[row(d), _full((1, d))],
        out_shape=[_sds((s, d), F32), _sds((1, d), F32)],
        scratch_shapes=[pltpu.VMEM((4, d, e), BF16), pltpu.SemaphoreType.DMA((1,))],
        operands=[dproj, h, dh1, gain, w_in], rider=rider)


def _bwd_mix_b(dh1, h, z, mx, diff, gain, scale, w_in, w_grp, w_out, name, rider=None):
    s, d = dh1.shape
    e = MIX_WIDTH
    ts = min(TS_MIX, s)
    nt = s // ts
    half = e // 2

    def body(dh1_ref, h_ref, z_ref, mx_ref, dd_ref, gain_ref, scale_ref, win_hbm, wgrp_hbm, wout_hbm,
             dh_ref, dscale_ref, dgain_ref, dwin_hbm, dwgrp_hbm, dwout_hbm,
             win_v, wgrp_v, wout_v, acc_in, acc_grp, acc_out, carry, sems):
        i = pl.program_id(0)
        tile = nt - 1 - i

        @pl.when(i == 0)
        def _():
            _copy_all([(win_hbm, win_v), (wout_hbm, wout_v)] + _grp_pairs(wgrp_hbm, wgrp_v), sems)
            carry[...] = jnp.zeros_like(carry)
            dscale_ref[...] = jnp.zeros_like(dscale_ref)
            dgain_ref[...] = jnp.zeros_like(dgain_ref)
            acc_in[...] = jnp.zeros_like(acc_in)
            acc_grp[...] = jnp.zeros_like(acc_grp)
            acc_out[...] = jnp.zeros_like(acc_out)

        zf = z_ref[...].astype(F32)
        mxf = mx_ref[...].astype(F32)
        sig = _sigmoid(zf)
        sz = zf * sig
        mixed = mxf * scale_ref[...]
        dh1 = dh1_ref[...]
        dh1b = dh1.astype(BF16)
        acc_out[...] += _dot_tn((sz * mixed).astype(BF16), dh1b)
        dm = _dot_nt(dh1b, wout_v[...])
        dz = (dm * mixed) * (sig * (1.0 + zf * (1.0 - sig)))
        dmixed = dm * sz
        dscale_ref[...] += jnp.sum(dmixed * mxf, axis=0, keepdims=True)
        dmxb = (dmixed * scale_ref[...]).astype(BF16)
        diff = dd_ref[...]
        for g in range(N_GROUPS):
            cols = slice(g * GROUP_DIM, (g + 1) * GROUP_DIM)
            acc_grp[g] += _dot_tn(diff[:, cols], dmxb[:, cols])
        ddiff = jnp.concatenate(
            [_dot_nt(dmxb[:, g * GROUP_DIM:(g + 1) * GROUP_DIM], wgrp_v[g]) for g in range(N_GROUPS)], axis=1)
        dub = (_pool_bwd(ddiff, carry, tile, ts) - ddiff).astype(BF16)
        dzb = dz.astype(BF16)
        parts = [dub[:, 0:half], dub[:, half:e], dzb[:, 0:half], dzb[:, half:e]]
        xn, r = _rms(h_ref[...])
        hnb = (xn * gain_ref[...]).astype(BF16)
        for k in range(4):
            acc_in[k] += _dot_tn(hnb, parts[k])
        dhn = _dot_nt(parts[0], win_v[0]) + _dot_nt(parts[1], win_v[1]) + _dot_nt(parts[2], win_v[2]) + _dot_nt(parts[3], win_v[3])
        dh, dgain = _rms_bwd(dhn, xn, r, gain_ref[...])
        dh_ref[...] = dh1 + dh
        dgain_ref[...] += dgain

        @pl.when(i == nt - 1)
        def _():
            _copy_all([(acc_in, dwin_hbm), (acc_out, dwout_hbm)] + [(v, hb_) for hb_, v in _grp_pairs(dwgrp_hbm, acc_grp)], sems)

    row = lambda width: pl.BlockSpec((ts, width), lambda i: (nt - 1 - i, 0))
    return _call(
        body, name=name, grid=(nt,),
        in_specs=[row(d), row(d), row(e), row(e), row(e), _full((1, d)), _full((1, e)), ANY, ANY, ANY],
        out_specs=[row(d), _full((1, e)), _full((1, d)), ANY, ANY, ANY],
        out_shape=[_sds((s, d), F32), _sds((1, e), F32), _sds((1, d), F32), _sds((4, d, half), F32),
                   _sds((4, N_GROUPS, GROUP_DIM // 4, GROUP_DIM), F32), _sds((e, d), F32)],
        scratch_shapes=[pltpu.VMEM((4, d, half), BF16), pltpu.VMEM((N_GROUPS, GROUP_DIM, GROUP_DIM), BF16),
                        pltpu.VMEM((e, d), BF16), pltpu.VMEM((4, d, half), F32),
                        pltpu.VMEM((N_GROUPS, GROUP_DIM, GROUP_DIM), F32), pltpu.VMEM((e, d), F32),
                        pltpu.VMEM((4, HALO, e), F32), pltpu.SemaphoreType.DMA((18,))],
        operands=[dh1, h, z, mx, diff, gain, scale, w_in, w_grp, w_out], rider=rider)


def _first_gather(rider, small):
    shards = rider.inputs
    ni = len(shards)

    def body(*refs):
        rin, small_src = refs[:ni], refs[ni]
        rout, small_dst = refs[ni + 1:2 * ni + 1], refs[2 * ni + 1]
        send, recv, ssend, srecv = refs[2 * ni + 2:]
        x, y, c, chips = _place()
        me = 2 * x + y
        peers = [(cx, cy, c) for cx, cy in chips] + [(x, y, 1 - c)]
        vec = [_remote(small_src, small_dst.at[me], ssend, srecv, j, to) for j, to in enumerate(peers)]
        for cp in vec:
            cp.start()
        rider.start(rin, rout, send, recv)
        rider.middle(rin, rout, send, recv)
        rider.finish(rin, rout, send, recv)
        for j, (px, py, _) in enumerate(peers):
            _remote(small_src, small_dst.at[2 * px + py], ssend, srecv, j, peers[j]).wait_recv()
        for cp in vec:
            cp.wait_send()

    outs = pl.pallas_call(
        body, name="first_gather", in_specs=[ANY] * (ni + 1), out_specs=[ANY] * (ni + 1),
        out_shape=rider.out_shapes + [_sds((4,) + small.shape, small.dtype)],
        scratch_shapes=[pltpu.SemaphoreType.DMA((rider.n_sems,)), pltpu.SemaphoreType.DMA((rider.n_sems,)),
                        pltpu.SemaphoreType.DMA((4,)), pltpu.SemaphoreType.DMA((4,))],
    )(*shards, small)
    return list(outs[:ni]), outs[ni]


def _vector_rider(pack):
    flips = [(fx, fy, fc) for fx in (0, 1) for fy in (0, 1) for fc in (0, 1)][1:]

    def copies(rin, rout, send, recv, base):
        x, y, c, _ = _place()
        me = 4 * x + 2 * y + c
        peers = [(1 - x if fx else x, 1 - y if fy else y, 1 - c if fc else c) for fx, fy, fc in flips]
        own = pltpu.make_async_copy(rin[0], rout[0].at[me], send.at[base + 7])
        out = [_remote(rin[0], rout[0].at[me], send, recv, base + r, peer) for r, peer in enumerate(peers)]
        back = [_remote(rin[0], rout[0].at[4 * px + 2 * py + pc], send, recv, base + r, (px, py, pc))
                for r, (px, py, pc) in enumerate(peers)]
        return own, out, back

    def start(rin, rout, send, recv, base=0):
        own, out, _ = copies(rin, rout, send, recv, base)
        own.start()
        for cp in out:
            cp.start()

    def finish(rin, rout, send, recv, base=0):
        own, out, back = copies(rin, rout, send, recv, base)
        for cp in back:
            cp.wait_recv()
        for cp in out:
            cp.wait_send()
        own.wait()

    return _Rider([pack], [_sds((8,) + pack.shape, pack.dtype)], 8, start, finish)


def _vector_sum(landed, row_counts):
    _, rows, d = landed.shape

    def body(l_ref, *out_refs):
        first = 0
        for n, out_ref in zip(row_counts, out_refs):
            total = l_ref[0, first:first + n, :]
            for dev in range(1, 8):
                total = total + l_ref[dev, first:first + n, :]
            out_ref[...] = total
            first += n

    vmem = pl.BlockSpec(memory_space=pltpu.VMEM)
    return pl.pallas_call(body, name="vector_sum", in_specs=[vmem], out_specs=[vmem] * len(row_counts),
                          out_shape=[_sds((n, d), F32) for n in row_counts])(landed)


def _job_rows(rows, cols):
    return min(rows, max(8, JOB_BLOCK_BYTES // (4 * cols)))


def _pair_sum_job(grad, sibling_rows):
    _, _, rh, cols = grad.shape
    tr = _job_rows(rh, cols)
    nr = rh // tr

    def chip_of(j, pos):
        return jnp.bitwise_xor(pos[0], jnp.where(j == 2, 3, 2 - j))

    return dict(
        ins=[(grad, (None, None, tr, cols), lambda l, pos: (chip_of(l // nr, pos), pos[1], l % nr, 0)),
             (sibling_rows, (None, tr, cols), lambda l, pos: (chip_of(l // nr, pos), l % nr, 0))],
        outs=[((3, rh, cols), BF16, (None, tr, cols), lambda l, pos: (l // nr, l % nr, 0))],
        steps=3 * nr, fn=lambda g, sb: [(g + sb).astype(BF16)], alias=None)


def _final_sum_job(grad, sibling_rows, landed, stack, slot, n_slots):
    _, _, rh, cols = grad.shape
    tr = _job_rows(rh, cols)

    def fn(g, sb, ld):
        total = g + sb
        for j in range(3):
            total = total + ld[j].astype(F32)
        return [total]

    return dict(
        ins=[(grad, (None, None, tr, cols), lambda l, pos: (pos[0], pos[1], l, 0)),
             (sibling_rows, (None, tr, cols), lambda l, pos: (pos[0], l, 0)),
             (landed, (3, tr, cols), lambda l, pos: (0, l, 0))],
        outs=[((n_slots, 2, rh, cols), F32, (None, None, tr, cols), lambda l, pos: (slot, pos[1], l, 0))],
        steps=rh // tr, fn=fn, alias=stack)


def _adamw_job(g, w, m, v, block_bytes):
    rows, cols = g.shape
    tr = min(rows, max(8, block_bytes // (4 * cols)))

    def fn(gg, ww, mm, vv):
        nm = ADAM_B1 * mm + (1.0 - ADAM_B1) * gg
        nv = ADAM_B2 * vv + (1.0 - ADAM_B2) * (gg * gg)
        m_hat = nm / (1.0 - ADAM_B1 ** ADAM_STEP)
        v_hat = nv / (1.0 - ADAM_B2 ** ADAM_STEP)
        return [-ADAM_LR * (m_hat / (jnp.sqrt(v_hat) + ADAM_EPS) + ADAM_WD * ww), nm, nv, gg]

    block = lambda l, pos: (l, 0)
    return dict(ins=[(a, (tr, cols), block) for a in (g, w, m, v)],
                outs=[((rows, cols), F32, (tr, cols), block)] * 4, steps=rows // tr, fn=fn, alias=None)


def _run_jobs(jobs, place, name):
    starts, total = [], 0
    for jb in jobs:
        starts.append(total)
        total += jb["steps"]

    def clamped(fn, start, steps):
        return lambda s, pos: fn(jnp.clip(s - start, 0, steps - 1), pos)

    in_specs, operands = [], [place]
    for jb, start in zip(jobs, starts):
        for arr, block, fn in jb["ins"]:
            in_specs.append(pl.BlockSpec(block, clamped(fn, start, jb["steps"])))
            operands.append(arr)
    n_ins = len(in_specs)
    first_out, n_outs = [], 0
    for jb in jobs:
        first_out.append(n_outs)
        n_outs += len(jb["outs"])
    aliases = {}
    for t, jb in enumerate(jobs):
        if jb["alias"] is not None:
            in_specs.append(ANY)
            operands.append(jb["alias"])
            aliases[len(operands) - 1] = first_out[t]
    out_specs = [pl.BlockSpec(block, clamped(fn, start, jb["steps"]))
                 for jb, start in zip(jobs, starts) for _, _, block, fn in jb["outs"]]

    def body(place_ref, *refs):
        in_refs, out_refs = refs[:n_ins], refs[len(in_specs):]
        s = pl.program_id(0)
        first = 0
        for t, (jb, start) in enumerate(zip(jobs, starts)):
            mine = in_refs[first:first + len(jb["ins"])]
            first += len(jb["ins"])

            @pl.when((s >= start) & (s < start + jb["steps"]))
            def _(mine=mine, t=t, jb=jb):
                values = jb["fn"](*[r[...] for r in mine])
                for n, value in enumerate(values):
                    out_refs[first_out[t] + n][...] = value

    grid_spec = pltpu.PrefetchScalarGridSpec(num_scalar_prefetch=1, grid=(total,), in_specs=in_specs, out_specs=out_specs)
    outs = pl.pallas_call(body, name=name, grid_spec=grid_spec,
                          out_shape=[_sds(shape, dtype) for jb in jobs for shape, dtype, _, _ in jb["outs"]],
                          input_output_aliases=aliases, compiler_params=_params(1))(*operands)
    return [list(outs[first_out[t]:first_out[t] + len(jb["outs"])]) for t, jb in enumerate(jobs)]


BIG = ["a_w_in", "a_w_out", "b_w_in", "b_w_grp", "b_w_out", "ple_w_gate", "ple_w_proj"]

GATHER_PLAN = {
    "first": [("a_w_in", 0), ("a_w_out", 0)],
    "mix0": [("ple_w_gate", 0), ("ple_w_proj", 0), ("b_w_in", 0), ("b_w_grp", 0), ("b_w_out", 0)],
    "ple0": [("ple_w_gate", 1), ("ple_w_proj", 1), ("ple_w_proj", 2)],
    "mix1": [("a_w_in", 1), ("a_w_out", 1), ("ple_w_gate", 2)],
    "mix2": [("b_w_in", 1), ("b_w_grp", 1), ("b_w_out", 1), ("ple_w_gate", 3), ("ple_w_proj", 3)],
}


def _as_2d(name, a):
    if name == "b_w_grp":
        return a.reshape(a.shape[0], N_GROUPS * (GROUP_DIM // 4), GROUP_DIM)
    return a


def kernel(x, p, norm_mix, a_w_in, a_w_conv, a_w_out, b_w_in, b_w_grp, b_scale, b_w_out, ple_norm, ple_w_gate, ple_w_proj, final_norm, loss_target, m_norm_mix, m_a_w_in, m_a_w_conv, m_a_w_out, m_b_w_in, m_b_w_grp, m_b_scale, m_b_w_out, m_ple_norm, m_ple_w_gate, m_ple_w_proj, m_final_norm, v_norm_mix, v_a_w_in, v_a_w_conv, v_a_w_out, v_b_w_in, v_b_w_grp, v_b_scale, v_b_w_out, v_ple_norm, v_ple_w_gate, v_ple_w_proj, v_final_norm):
    d, e = D_MODEL, MIX_WIDTH
    s = x.shape[1]
    cx, cy, cc = lax.axis_index("x"), lax.axis_index("y"), lax.axis_index("c")
    chip = 2 * cx + cy
    place = jnp.stack([chip, cc]).astype(jnp.int32)

    weights = dict(a_w_in=a_w_in, a_w_out=a_w_out, b_w_in=b_w_in, b_w_grp=b_w_grp, b_w_out=b_w_out,
                   ple_w_gate=ple_w_gate, ple_w_proj=ple_w_proj)
    moms = dict(a_w_in=m_a_w_in, a_w_out=m_a_w_out, b_w_in=m_b_w_in, b_w_grp=m_b_w_grp, b_w_out=m_b_w_out,
                ple_w_gate=m_ple_w_gate, ple_w_proj=m_ple_w_proj)
    vars_ = dict(a_w_in=v_a_w_in, a_w_out=v_a_w_out, b_w_in=v_b_w_in, b_w_grp=v_b_w_grp, b_w_out=v_b_w_out,
                 ple_w_gate=v_ple_w_gate, ple_w_proj=v_ple_w_proj)
    w2d = {nm: _as_2d(nm, weights[nm]) for nm in BIG}
    bf = {nm: w2d[nm].astype(BF16).reshape(w2d[nm].shape[0], 2, w2d[nm].shape[1] // 2, w2d[nm].shape[2]) for nm in BIG}
    gathered = {}

    def gather_rider(host):
        keys = GATHER_PLAN.get(host)
        return _gather_rider([bf[nm] for nm, _ in keys], [j for _, j in keys]) if keys else None

    def keep(host, landed):
        for k, a in zip(GATHER_PLAN.get(host, []), landed):
            gathered[k] = a

    def weight(nm, j):
        a = gathered[(nm, j)]
        shapes = {"a_w_in": (4, d, e), "a_w_out": (e, d), "b_w_in": (4, d, e // 2),
                  "b_w_grp": (4, N_GROUPS, GROUP_DIM // 4, GROUP_DIM), "b_w_out": (e, d), "ple_w_gate": (d, d),
                  "ple_w_proj": (4, PLE_DIM, d // 4)}
        return a.reshape(shapes[nm])

    pad = jnp.zeros((4, e // 4), F32)
    small = jnp.concatenate([a_w_conv[0], b_scale[0:1], pad, a_w_conv[1], b_scale[1:2], pad], axis=0)
    landed, small_full = _first_gather(gather_rider("first"), small)
    keep("first", landed)
    small_full = small_full.transpose(1, 0, 2).reshape(16, e)
    conv_w = [_Block(small_full.reshape(2, 8, e), j) for j in range(2)]
    scale_w = [_Block(small_full.reshape(16, 1, e), 8 * j + 3) for j in range(2)]

    p3 = p.reshape(DEPTH, s, PLE_DIM)
    mix_gain = [_Block(norm_mix.reshape(DEPTH, 1, d), i) for i in range(DEPTH)]
    ple_gain = [_Block(ple_norm.reshape(DEPTH, 1, d), i) for i in range(DEPTH)]

    h = x.reshape(s, d)
    saved = []
    for i in range(DEPTH):
        j = i // 2
        rider = gather_rider(f"mix{i}")
        ple = (p3, i, ple_gain[i], weight("ple_w_gate", i), weight("ple_w_proj", i)) if i > 0 else None
        if i % 2 == 0:
            outs, landed = _fwd_mix_a(h, mix_gain[i], conv_w[j], weight("a_w_in", j), weight("a_w_out", j),
                                      f"fwd_mix_a{j}", rider, ple)
            mix = dict(proj=outs[1])
        else:
            outs, landed = _fwd_mix_b(h, mix_gain[i], scale_w[j], weight("b_w_in", j), weight("b_w_grp", j),
                                      weight("b_w_out", j), f"fwd_mix_b{j}", rider, ple)
            mix = dict(z=outs[1], mx=outs[2], diff=outs[3])
        keep(f"mix{i}", landed)
        h1 = outs[0]
        if ple:
            h2, gate = outs[-2:]
        else:
            (h2, gate), landed = _fwd_ple(h1, p3, ple_gain[i], weight("ple_w_gate", i), weight("ple_w_proj", i), i,
                                          gather_rider(f"ple{i}"))
            keep(f"ple{i}", landed)
        saved.append(dict(h=h, h1=h1, gate=gate, **mix))
        h = h2

    n_slots = {nm: weights[nm].shape[0] for nm in BIG}
    stacks = {nm: None for nm in BIG}

    class Group:
        def __init__(self, keys, grads):
            self.keys, self.stage = keys, 0
            self.g32 = [g.reshape(4, 2, w2d[nm].shape[1] // 2, w2d[nm].shape[2]) for (nm, _), g in zip(keys, grads)]

        def rider(self):
            if self.stage == 0:
                return _pair_rider(self.g32)
            if self.stage == 1:
                return _ici_rider(self.pair_sums)
            return _final_rider([stacks[nm] for nm, _ in self.keys], [j for _, j in self.keys])

        def jobs_after(self, landed):
            if self.stage == 0:
                self.from_sibling = landed
                return [_pair_sum_job(g, sb) for g, sb in zip(self.g32, landed)]
            if self.stage == 1:
                return [_final_sum_job(g, sb, ld, stacks[nm], j, n_slots[nm])
                        for (nm, j), g, sb, ld in zip(self.keys, self.g32, self.from_sibling, landed)]
            return []

        def advance(self, landed, summed):
            if self.stage == 0:
                self.pair_sums = summed
            else:
                for (nm, _), a in zip(self.keys, summed if self.stage == 1 else landed):
                    stacks[nm] = a
            self.stage += 1

    active = []
    batches = [0]

    def riders_now():
        parts = [g.rider() for g in active]
        return parts, _merge(parts)

    def advance_all(parts, landed):
        groups = list(active)
        pieces = _split(landed, parts)
        jobs = [g.jobs_after(l) for g, l in zip(groups, pieces)]
        flat = sum(jobs, [])
        outs = [o[0] for o in _run_jobs(flat, place, f"reduce_sums{batches[0]}")] if flat else []
        batches[0] += 1
        for g, l, jb in zip(groups, pieces, jobs):
            g.advance(l, outs[:len(jb)])
            outs = outs[len(jb):]
            if g.stage == 3:
                active.remove(g)

    d_mix_gain, d_ple_gain = [None] * DEPTH, [None] * DEPTH
    d_conv, d_scale = [None] * 2, [None] * 2
    for i in reversed(range(DEPTH)):
        j = i // 2
        sv = saved[i]
        parts, rider = riders_now()
        if i == DEPTH - 1:
            (dh1, d_ple_gain[i], dwg, dwp, loss_part, d_final), landed = _bwd_ple(
                h, sv["h1"], sv["gate"], p3, ple_gain[i], weight("ple_w_gate", i), weight("ple_w_proj", i), i, rider,
                loss_head=(loss_target.reshape(s, d), final_norm.reshape(1, d)))
        else:
            (dh1, d_ple_gain[i], dwg, dwp), landed = _bwd_ple(
                dh, sv["h1"], sv["gate"], p3, ple_gain[i], weight("ple_w_gate", i), weight("ple_w_proj", i), i, rider)
        advance_all(parts, landed)
        active.append(Group([("ple_w_gate", i), ("ple_w_proj", i)], [dwg, dwp]))
        parts, rider = riders_now()
        if i == 0:
            (dproj, d_conv[0], dwin, dwout), landed = _bwd_mix_a_weights(
                dh1, sv["h"], sv["proj"], mix_gain[0], conv_w[0], weight("a_w_out", 0), "bwd_mix_a0_weights", rider)
            advance_all(parts, landed)
            active.append(Group([("a_w_in", 0), ("a_w_out", 0)], [dwin, dwout]))
            parts, rider = riders_now()
            advance_all(parts, _run_rider(rider, "pair_exchange0"))
            parts, rider = riders_now()
            (dh, d_mix_gain[0]), landed = _bwd_mix_a_input(dproj, sv["h"], dh1, mix_gain[0], weight("a_w_in", 0),
                                                          "bwd_mix_a0_input", rider)
            advance_all(parts, landed)
            continue
        if i % 2 == 0:
            (dh, d_conv[j], d_mix_gain[i], dwin, dwout), landed = _bwd_mix_a(
                dh1, sv["h"], sv["proj"], mix_gain[i], conv_w[j], weight("a_w_in", j), weight("a_w_out", j),
                f"bwd_mix_a{j}", rider)
            new = Group([("a_w_in", j), ("a_w_out", j)], [dwin, dwout])
        else:
            (dh, d_scale[j], d_mix_gain[i], dwin, dwgrp, dwout), landed = _bwd_mix_b(
                dh1, sv["h"], sv["z"], sv["mx"], sv["diff"], mix_gain[i], scale_w[j], weight("b_w_in", j),
                weight("b_w_grp", j), weight("b_w_out", j), f"bwd_mix_b{j}", rider)
            new = Group([("b_w_in", j), ("b_w_grp", j), ("b_w_out", j)], [dwin, dwgrp, dwout])
        advance_all(parts, landed)
        active.append(new)
    grad_x = dh.reshape(1, s, d)

    pieces = (d_mix_gain + d_ple_gain + [d_final, d_conv[0][0:3], d_conv[1][0:3]] + d_scale
              + [jnp.tile(loss_part[0:1], (1, d // 128))])
    used = sum(a.shape[0] for a in pieces)
    pack = jnp.concatenate(pieces + [jnp.zeros((-used % PACK_GROUP, d), F32)], axis=0)
    vectors = _vector_rider(pack)
    tail = 0
    while active:
        parts, _ = riders_now()
        extra = [vectors] if tail == 0 else []
        landed = _run_rider(_merge(parts + extra), f"tail_exchange{tail}")
        if extra:
            g_mix, g_ple, g_final, g_conv, g_scale, loss_row = _vector_sum(
                _split(landed, parts + extra)[-1][0], [DEPTH, DEPTH, 1, 6, 2, 1])
        advance_all(parts, landed)
        tail += 1
    loss = loss_row[0, 0]

    mine = lambda a: lax.dynamic_slice_in_dim(a, chip * (e // 4), e // 4, axis=1)
    row = lambda a: a.reshape(1, d)
    taps = lambda a: a.reshape(6, e // 4)
    flat = {nm: (w2d[nm].shape[0] * w2d[nm].shape[1], w2d[nm].shape[2]) for nm in BIG}
    tensors = {nm: (stacks[nm].reshape(flat[nm]), w2d[nm].reshape(flat[nm]), _as_2d(nm, moms[nm]).reshape(flat[nm]),
                    _as_2d(nm, vars_[nm]).reshape(flat[nm])) for nm in BIG}
    tensors.update(
        norm_mix=(g_mix, norm_mix, m_norm_mix, v_norm_mix), ple_norm=(g_ple, ple_norm, m_ple_norm, v_ple_norm),
        final_norm=(g_final, row(final_norm), row(m_final_norm), row(v_final_norm)),
        a_w_conv=(mine(g_conv), taps(a_w_conv), taps(m_a_w_conv), taps(v_a_w_conv)),
        b_scale=(mine(g_scale), b_scale, m_b_scale, v_b_scale))
    order = ["norm_mix", "a_w_in", "a_w_conv", "a_w_out", "b_w_in", "b_w_grp", "b_scale", "b_w_out", "ple_norm",
             "ple_w_gate", "ple_w_proj", "final_norm"]
    shapes = dict(norm_mix=norm_mix.shape, ple_norm=ple_norm.shape, final_norm=final_norm.shape,
                  a_w_conv=a_w_conv.shape, b_scale=b_scale.shape, **{nm: weights[nm].shape for nm in BIG})
    updates = {nm: _run_jobs([_adamw_job(*tensors[nm], ADAMW_BIG_BLOCK_BYTES)], place, f"adamw_{nm}")[0]
               for nm in ADAMW_ALONE}
    rest = [nm for nm in order if nm not in ADAMW_ALONE]
    updates.update(zip(rest, _run_jobs([_adamw_job(*tensors[nm], ADAMW_BLOCK_BYTES) for nm in rest], place, "adamw_rest")))
    outs = [loss, grad_x]
    for which in (3, 0, 1, 2):
        outs += [updates[nm][which].reshape(shapes[nm]) for nm in order]
    return tuple(outs)
```

```python
import jax
import jax.numpy as jnp
from jax import lax
from jax.experimental import pallas as pl
from jax.experimental.pallas import tpu as pltpu

F32 = jnp.float32
BF16 = jnp.bfloat16
MESH = pl.DeviceIdType.MESH

D_MODEL = 1024
MIX_WIDTH = 1024
PLE_DIM = 256
N_GROUPS = 4
GROUP_DIM = 256
POOL_WINDOWS = (2, 4, 8, 16)
DEPTH = 4
EPS = 1e-6

ADAM_LR = 0.001
ADAM_B1 = 0.9
ADAM_B2 = 0.999
ADAM_EPS = 1e-08
ADAM_WD = 0.01
ADAM_STEP = 10

HALO = 8
TS_MIX = 256
TS_FWD = 512
TS_PLE = 512
VMEM_LIMIT = 56 * 1024 * 1024
PACK_GROUP = 8
JOB_BLOCK_BYTES = 2 * 1024 * 1024
ADAMW_BLOCK_BYTES = 512 * 1024
ADAMW_BIG_BLOCK_BYTES = 2 * 1024 * 1024
ADAMW_ALONE = ("a_w_in", "b_w_in", "ple_w_gate")
MIDDLE_STEPS_BEFORE_END = 1

ANY = pl.BlockSpec(memory_space=pl.ANY)


def _sds(shape, dtype):
    return jax.ShapeDtypeStruct(shape, dtype)


def _full(shape):
    nd = len(shape)
    return pl.BlockSpec(shape, lambda *_: (0,) * nd)


def _params(n_axes=1):
    return pltpu.CompilerParams(dimension_semantics=("arbitrary",) * n_axes, vmem_limit_bytes=VMEM_LIMIT)


def _dot(a, b):
    return jnp.dot(a, b, preferred_element_type=F32)


def _dot_nt(a, b):
    return lax.dot_general(a, b, (((1,), (1,)), ((), ())), preferred_element_type=F32)


def _dot_tn(a, b):
    return lax.dot_general(a, b, (((0,), (0,)), ((), ())), preferred_element_type=F32)


def _sigmoid(z):
    return 1.0 / (1.0 + jnp.exp(-z))


def _shift_down(x, k, tail):
    rolled = pltpu.roll(x, k, 0)
    rt = tail if k % HALO == 0 else pltpu.roll(tail, k % HALO, 0)
    row = lax.broadcasted_iota(jnp.int32, rt.shape, 0)
    head = jnp.where(row < k, rt, rolled[0:HALO])
    return jnp.concatenate([head, rolled[HALO:]], axis=0)


def _shift_up(x, k, head_next):
    n = x.shape[0]
    rolled = pltpu.roll(x, n - k, 0)
    rh = head_next if k % HALO == 0 else pltpu.roll(head_next, HALO - k % HALO, 0)
    row = lax.broadcasted_iota(jnp.int32, rh.shape, 0)
    tail = jnp.where(row >= HALO - k, rh, rolled[n - HALO:n])
    return jnp.concatenate([rolled[:n - HALO], tail], axis=0)


def _inv_counts(tile, ts):
    t = tile * ts + lax.broadcasted_iota(jnp.int32, (ts, 1), 0)
    return [1.0 / jnp.minimum(t + 1, w).astype(F32) for w in POOL_WINDOWS]


def _pool_fwd(u, carry, tile, ts):
    inv = _inv_counts(tile, ts)
    outs = []
    for g, w in enumerate(POOL_WINDOWS):
        cols = slice(g * GROUP_DIM, (g + 1) * GROUP_DIM)
        s = u[:, cols]
        level, k = 0, 1
        while k < w:
            tail = carry[level, :, cols]
            carry[level, :, cols] = s[ts - HALO:ts]
            s = s + _shift_down(s, k, tail)
            level, k = level + 1, k * 2
        outs.append(s * inv[g])
    return jnp.concatenate(outs, axis=1)


def _pool_bwd(dd, carry, tile, ts):
    inv = _inv_counts(tile, ts)
    outs = []
    for g, w in enumerate(POOL_WINDOWS):
        cols = slice(g * GROUP_DIM, (g + 1) * GROUP_DIM)
        q = dd[:, cols] * inv[g]
        level, k = 0, 1
        while k < w:
            head = carry[level, :, cols]
            carry[level, :, cols] = q[0:HALO]
            q = q + _shift_up(q, k, head)
            level, k = level + 1, k * 2
        outs.append(q)
    return jnp.concatenate(outs, axis=1)


def _copy_all(pairs, sems):
    copies = [pltpu.make_async_copy(src, dst, sems.at[n]) for n, (src, dst) in enumerate(pairs)]
    for cp in copies:
        cp.start()
    for cp in copies:
        cp.wait()


def _grp_pairs(wgrp_hbm, wgrp_v):
    rows = GROUP_DIM // 4
    return [(wgrp_hbm.at[k, g], wgrp_v.at[g, pl.ds(k * rows, rows), :]) for k in range(4) for g in range(N_GROUPS)]


def _rms(h):
    r = lax.rsqrt(jnp.mean(h * h, axis=-1, keepdims=True) + EPS)
    return h * r, r


def _rms_bwd(dhn, xn, r, gain):
    dgain = jnp.sum(dhn * xn, axis=0, keepdims=True)
    dxn = dhn * gain
    dh = r * (dxn - xn * jnp.mean(dxn * xn, axis=-1, keepdims=True))
    return dh, dgain


class _Rider:
    def __init__(self, inputs, out_shapes, n_sems, start, finish, middle=None, aliases=None):
        self.inputs, self.out_shapes, self.n_sems = list(inputs), list(out_shapes), n_sems
        self.start, self.middle, self.finish = start, middle, finish
        self.aliases = dict(aliases or {})


def _merge(riders):
    riders = [r for r in riders if r is not None]
    if not riders:
        return None
    if len(riders) == 1:
        return riders[0]

    def phase(which):
        def run(rin, rout, send, recv, base=0):
            i0 = o0 = s0 = 0
            for r in riders:
                fn = getattr(r, which)
                if fn is not None:
                    fn(rin[i0:i0 + len(r.inputs)], rout[o0:o0 + len(r.out_shapes)], send, recv, base + s0)
                i0, o0, s0 = i0 + len(r.inputs), o0 + len(r.out_shapes), s0 + r.n_sems
        return run

    aliases, i0, o0 = {}, 0, 0
    for r in riders:
        aliases.update({i0 + a: o0 + b for a, b in r.aliases.items()})
        i0, o0 = i0 + len(r.inputs), o0 + len(r.out_shapes)
    return _Rider(sum([r.inputs for r in riders], []), sum([r.out_shapes for r in riders], []),
                  sum(r.n_sems for r in riders), phase("start"), phase("finish"),
                  phase("middle") if any(r.middle for r in riders) else None, aliases)


def _split(landed, riders):
    out, o0 = [], 0
    for r in riders:
        if r is None:
            out.append(None)
        else:
            out.append(landed[o0:o0 + len(r.out_shapes)])
            o0 += len(r.out_shapes)
    return out


def _place():
    x, y, c = lax.axis_index("x"), lax.axis_index("y"), lax.axis_index("c")
    chips = [(1 - x, y), (x, 1 - y), (1 - x, 1 - y)]
    return x, y, c, chips


def _remote(src, dst, send_sems, recv_sems, sem, to):
    return pltpu.make_async_remote_copy(src_ref=src, dst_ref=dst, send_sem=send_sems.at[sem], recv_sem=recv_sems.at[sem],
                                        device_id=to, device_id_type=MESH)


def _gather_rider(stacked, slots):
    ni = len(stacked)

    def first_hops(rin, rout, send, recv, base, x, y, c, chips):
        me = 2 * x + y
        return [_remote(rin[t].at[slots[t], c], rout[t].at[me, c], send, recv, base + 7 * t + j, (cx, cy, c))
                for j, (cx, cy) in enumerate(chips) for t in range(ni)]

    def passes(rout, send, recv, base, x, y, c, chips):
        out = []
        for j, (cx, cy) in enumerate(chips):
            for t in range(ni):
                landed = rout[t].at[2 * cx + cy, c]
                out.append((_remote(landed, landed, send, recv, base + 7 * t + j, (x, y, 1 - c)),
                            _remote(landed, landed, send, recv, base + 7 * t + 3 + j, (x, y, 1 - c))))
        return out

    def own(rin, rout, send, recv, base, x, y, c):
        return [_remote(rin[t].at[slots[t]], rout[t].at[2 * x + y], send, recv, base + 7 * t + 6, (x, y, 1 - c))
                for t in range(ni)]

    def start(rin, rout, send, recv, base=0):
        x, y, c, chips = _place()
        for cp in first_hops(rin, rout, send, recv, base, x, y, c, chips) + own(rin, rout, send, recv, base, x, y, c):
            cp.start()

    def middle(rin, rout, send, recv, base=0):
        x, y, c, chips = _place()
        for arrival, onward in passes(rout, send, recv, base, x, y, c, chips):
            arrival.wait_recv()
            onward.start()

    def finish(rin, rout, send, recv, base=0):
        x, y, c, chips = _place()
        for j, (cx, cy) in enumerate(chips):
            for t in range(ni):
                other = rout[t].at[2 * cx + cy, 1 - c]
                _remote(other, other, send, recv, base + 7 * t + 3 + j, (x, y, 1 - c)).wait_recv()
        for cp in own(rin, rout, send, recv, base, x, y, c):
            cp.wait_recv()
            cp.wait_send()
        for cp in first_hops(rin, rout, send, recv, base, x, y, c, chips):
            cp.wait_send()
        for _, onward in passes(rout, send, recv, base, x, y, c, chips):
            onward.wait_send()

    return _Rider(stacked, [_sds((4,) + a.shape[1:], a.dtype) for a in stacked], 7 * ni, start, finish, middle)


def _pair_rider(grads):
    ni = len(grads)

    def copies(rin, rout, send, recv, base):
        x, y, c, _ = _place()
        return [_remote(rin[t].at[:, 1 - c], rout[t], send, recv, base + t, (x, y, 1 - c)) for t in range(ni)]

    def start(rin, rout, send, recv, base=0):
        for cp in copies(rin, rout, send, recv, base):
            cp.start()

    def finish(rin, rout, send, recv, base=0):
        for cp in copies(rin, rout, send, recv, base):
            cp.wait()

    return _Rider(grads, [_sds(g.shape[:1] + g.shape[2:], g.dtype) for g in grads], ni, start, finish)


def _ici_rider(pair_sums):
    ni = len(pair_sums)

    def copies(rin, rout, send, recv, base):
        x, y, c, chips = _place()
        return [_remote(rin[t].at[j], rout[t].at[j], send, recv, base + 3 * t + j, (cx, cy, c))
                for j, (cx, cy) in enumerate(chips) for t in range(ni)]

    def start(rin, rout, send, recv, base=0):
        for cp in copies(rin, rout, send, recv, base):
            cp.start()

    def finish(rin, rout, send, recv, base=0):
        for cp in copies(rin, rout, send, recv, base):
            cp.wait()

    return _Rider(pair_sums, [_sds((3,) + g.shape[1:], g.dtype) for g in pair_sums], 3 * ni, start, finish)


def _final_rider(summed, slots):
    ni = len(summed)

    def copies(rout, send, recv, base):
        x, y, c, _ = _place()
        return [(_remote(rout[t].at[slots[t], c], rout[t].at[slots[t], c], send, recv, base + t, (x, y, 1 - c)),
                 _remote(rout[t].at[slots[t], 1 - c], rout[t].at[slots[t], 1 - c], send, recv, base + t, (x, y, 1 - c)))
                for t in range(ni)]

    def start(rin, rout, send, recv, base=0):
        for mine, _ in copies(rout, send, recv, base):
            mine.start()

    def finish(rin, rout, send, recv, base=0):
        for mine, theirs in copies(rout, send, recv, base):
            mine.wait_send()
            theirs.wait_recv()

    return _Rider(summed, [_sds(a.shape, a.dtype) for a in summed], ni, start, finish,
                  aliases={t: t for t in range(ni)})


class _Block:
    def __init__(self, array, index):
        self.array, self.index = array, index

    def spec(self):
        index = self.index
        return pl.BlockSpec((None,) + self.array.shape[1:], lambda *_: (index, 0, 0))


def _call(body, *, name, grid, in_specs, out_specs, out_shape, scratch_shapes, operands, rider=None):
    operands, in_specs = list(operands), list(in_specs)
    for n, op in enumerate(operands):
        if isinstance(op, _Block):
            operands[n], in_specs[n] = op.array, op.spec()
    if rider is None:
        outs = pl.pallas_call(body, name=name, grid=grid, in_specs=in_specs, out_specs=out_specs, out_shape=out_shape,
                              scratch_shapes=scratch_shapes, compiler_params=_params(len(grid)))(*operands)
        return list(outs), []
    n_in, n_out, n_scr = len(in_specs), len(out_specs), len(scratch_shapes)
    r_in, r_out = len(rider.inputs), len(rider.out_shapes)
    steps = 1
    for g in grid:
        steps *= g
    mid = max(steps - 1 - MIDDLE_STEPS_BEFORE_END, 0)

    def full_body(*refs):
        own_in, rin = refs[:n_in], refs[n_in:n_in + r_in]
        own_out = refs[n_in + r_in:n_in + r_in + n_out]
        rout = refs[n_in + r_in + n_out:n_in + r_in + n_out + r_out]
        own_scr = refs[n_in + r_in + n_out + r_out:n_in + r_in + n_out + r_out + n_scr]
        send, recv = refs[-2], refs[-1]
        step = pl.program_id(0)
        for axis in range(1, len(grid)):
            step = step * grid[axis] + pl.program_id(axis)

        @pl.when(step == 0)
        def _():
            rider.start(rin, rout, send, recv)

        body(*own_in, *own_out, *own_scr)

        if rider.middle is not None:
            @pl.when(step == mid)
            def _():
                rider.middle(rin, rout, send, recv)

        @pl.when(step == steps - 1)
        def _():
            rider.finish(rin, rout, send, recv)

    outs = pl.pallas_call(
        full_body, name=name, grid=grid,
        in_specs=list(in_specs) + [ANY] * r_in, out_specs=list(out_specs) + [ANY] * r_out,
        out_shape=list(out_shape) + rider.out_shapes,
        scratch_shapes=list(scratch_shapes) + [pltpu.SemaphoreType.DMA((rider.n_sems,)), pltpu.SemaphoreType.DMA((rider.n_sems,))],
        input_output_aliases={n_in + a: n_out + b for a, b in rider.aliases.items()},
        compiler_params=_params(len(grid)),
    )(*operands, *rider.inputs)
    return list(outs[:n_out]), list(outs[n_out:])


def _run_rider(rider, name):
    r_in, r_out = len(rider.inputs), len(rider.out_shapes)

    def body(*refs):
        rin, rout, send, recv = refs[:r_in], refs[r_in:r_in + r_out], refs[-2], refs[-1]
        rider.start(rin, rout, send, recv)
        if rider.middle is not None:
            rider.middle(rin, rout, send, recv)
        rider.finish(rin, rout, send, recv)

    outs = pl.pallas_call(
        body, name=name, in_specs=[ANY] * r_in, out_specs=[ANY] * r_out, out_shape=rider.out_shapes,
        scratch_shapes=[pltpu.SemaphoreType.DMA((rider.n_sems,)), pltpu.SemaphoreType.DMA((rider.n_sems,))],
        input_output_aliases=rider.aliases,
    )(*rider.inputs)
    return list(outs)


def _ple_tile(h1, p_ref, gain_ref, wg_v, wp_v):
    xn, _ = _rms(h1)
    hpb = (xn * gain_ref[...]).astype(BF16)
    gate = _sigmoid(_dot(hpb, wg_v[...]))
    pb = p_ref[...].astype(BF16)
    pe = jnp.concatenate([_dot(pb, wp_v[k]) for k in range(4)], axis=1)
    return h1 + gate * pe, gate


def _ple_parts(ple, ts, d):
    p, layer, gain, w_gate, w_proj = ple
    s, pd = p.shape[1:]
    row = pl.BlockSpec((ts, d), lambda i: (i, 0))
    return dict(
        operands=[p, gain, w_gate, w_proj],
        in_specs=[pl.BlockSpec((None, ts, pd), lambda i: (layer, i, 0)), _full((1, d)), ANY, ANY],
        out_specs=[row, row], out_shape=[_sds((s, d), F32), _sds((s, d), BF16)],
        scratch=[pltpu.VMEM((d, d), BF16), pltpu.VMEM((4, pd, d // 4), BF16)])


def _fwd_mix_a(h, gain, conv_w, w_in, w_out, name, rider=None, ple=None):
    s, d = h.shape
    e = MIX_WIDTH
    ts = min(TS_FWD, s)
    nt = s // ts
    extra = _ple_parts(ple, ts, d) if ple else None

    def body(*refs):
        h_ref, gain_ref, cw_ref, win_hbm, wout_hbm = refs[:5]
        n_in = 9 if ple else 5
        h1_ref, proj_ref = refs[n_in:n_in + 2]
        win_v, wout_v, carry, sems = refs[n_in + (4 if ple else 2):][:4]
        i = pl.program_id(0)

        @pl.when(i == 0)
        def _():
            loads = [(win_hbm, win_v), (wout_hbm, wout_v)]
            if ple:
                loads += [(refs[7], refs[-2]), (refs[8], refs[-1])]
            _copy_all(loads, sems)
            carry[...] = jnp.zeros_like(carry)

        hh = h_ref[...]
        xn, _ = _rms(hh)
        hnb = (xn * gain_ref[...]).astype(BF16)
        b = _dot(hnb, win_v[0])
        c = _dot(hnb, win_v[1])
        v = _dot(hnb, win_v[2])
        z = _dot(hnb, win_v[3])
        proj_ref[:, 0 * e:1 * e] = b.astype(BF16)
        proj_ref[:, 1 * e:2 * e] = c.astype(BF16)
        proj_ref[:, 2 * e:3 * e] = v.astype(BF16)
        proj_ref[:, 3 * e:4 * e] = z.astype(BF16)
        cv = c * v
        tail = carry[...]
        carry[...] = cv[ts - HALO:ts]
        conv = cw_ref[0:1, :] * _shift_down(cv, 2, tail) + cw_ref[1:2, :] * _shift_down(cv, 1, tail) + cw_ref[2:3, :] * cv
        mb = ((z * _sigmoid(z)) * (b * conv)).astype(BF16)
        h1 = hh + _dot(mb, wout_v[...])
        h1_ref[...] = h1
        if ple:
            h2, gate = _ple_tile(h1, refs[5], refs[6], refs[-2], refs[-1])
            refs[n_in + 2][...] = h2
            refs[n_in + 3][...] = gate.astype(BF16)

    row = lambda width: pl.BlockSpec((ts, width), lambda i: (i, 0))
    return _call(
        body, name=name, grid=(nt,),
        in_specs=[row(d), _full((1, d)), _full((8, e)), ANY, ANY] + (extra["in_specs"] if ple else []),
        out_specs=[row(d), row(4 * e)] + (extra["out_specs"] if ple else []),
        out_shape=[_sds((s, d), F32), _sds((s, 4 * e), BF16)] + (extra["out_shape"] if ple else []),
        scratch_shapes=[pltpu.VMEM((4, d, e), BF16), pltpu.VMEM((e, d), BF16), pltpu.VMEM((HALO, e), F32),
                        pltpu.SemaphoreType.DMA((4,))] + (extra["scratch"] if ple else []),
        operands=[h, gain, conv_w, w_in, w_out] + (extra["operands"] if ple else []), rider=rider)


def _fwd_mix_b(h, gain, scale, w_in, w_grp, w_out, name, rider=None, ple=None):
    s, d = h.shape
    e = MIX_WIDTH
    ts = min(TS_FWD, s)
    nt = s // ts
    extra = _ple_parts(ple, ts, d) if ple else None

    def body(*refs):
        h_ref, gain_ref, scale_ref, win_hbm, wgrp_hbm, wout_hbm = refs[:6]
        n_in = 10 if ple else 6
        h1_ref, z_ref, mx_ref, dd_ref = refs[n_in:n_in + 4]
        win_v, wgrp_v, wout_v, carry, sems = refs[n_in + (6 if ple else 4):][:5]
        i = pl.program_id(0)

        @pl.when(i == 0)
        def _():
            loads = [(win_hbm, win_v), (wout_hbm, wout_v)] + _grp_pairs(wgrp_hbm, wgrp_v)
            if ple:
                loads += [(refs[8], refs[-2]), (refs[9], refs[-1])]
            _copy_all(loads, sems)
            carry[...] = jnp.zeros_like(carry)

        hh = h_ref[...]
        xn, _ = _rms(hh)
        hnb = (xn * gain_ref[...]).astype(BF16)
        u = jnp.concatenate([_dot(hnb, win_v[0]), _dot(hnb, win_v[1])], axis=1)
        z = jnp.concatenate([_dot(hnb, win_v[2]), _dot(hnb, win_v[3])], axis=1)
        z_ref[...] = z.astype(BF16)
        diff = (_pool_fwd(u, carry, i, ts) - u).astype(BF16)
        dd_ref[...] = diff
        mx = jnp.concatenate(
            [_dot(diff[:, g * GROUP_DIM:(g + 1) * GROUP_DIM], wgrp_v[g]) for g in range(N_GROUPS)], axis=1)
        mx_ref[...] = mx.astype(BF16)
        mb = ((z * _sigmoid(z)) * (mx * scale_ref[...])).astype(BF16)
        h1 = hh + _dot(mb, wout_v[...])
        h1_ref[...] = h1
        if ple:
            h2, gate = _ple_tile(h1, refs[6], refs[7], refs[-2], refs[-1])
            refs[n_in + 4][...] = h2
            refs[n_in + 5][...] = gate.astype(BF16)

    row = lambda width: pl.BlockSpec((ts, width), lambda i: (i, 0))
    return _call(
        body, name=name, grid=(nt,),
        in_specs=[row(d), _full((1, d)), _full((1, e)), ANY, ANY, ANY] + (extra["in_specs"] if ple else []),
        out_specs=[row(d), row(e), row(e), row(e)] + (extra["out_specs"] if ple else []),
        out_shape=[_sds((s, d), F32)] + [_sds((s, e), BF16)] * 3 + (extra["out_shape"] if ple else []),
        scratch_shapes=[pltpu.VMEM((4, d, e // 2), BF16), pltpu.VMEM((N_GROUPS, GROUP_DIM, GROUP_DIM), BF16),
                        pltpu.VMEM((e, d), BF16), pltpu.VMEM((4, HALO, e), F32), pltpu.SemaphoreType.DMA((20,))]
        + (extra["scratch"] if ple else []),
        operands=[h, gain, scale, w_in, w_grp, w_out] + (extra["operands"] if ple else []), rider=rider)


def _fwd_ple(h1, p, gain, w_gate, w_proj, layer, rider=None):
    s, d = h1.shape
    pd = p.shape[-1]
    ts = min(TS_PLE, s)
    nt = s // ts

    def body(h1_ref, p_ref, gain_ref, wg_hbm, wp_hbm, h2_ref, gate_ref, wg_v, wp_v, sems):
        @pl.when(pl.program_id(0) == 0)
        def _():
            _copy_all([(wg_hbm, wg_v), (wp_hbm, wp_v)], sems)

        hh = h1_ref[...]
        xn, _ = _rms(hh)
        hpb = (xn * gain_ref[...]).astype(BF16)
        gate = _sigmoid(_dot(hpb, wg_v[...]))
        pb = p_ref[...].astype(BF16)
        pe = jnp.concatenate([_dot(pb, wp_v[k]) for k in range(4)], axis=1)
        gate_ref[...] = gate.astype(BF16)
        h2_ref[...] = hh + gate * pe

    row = lambda width: pl.BlockSpec((ts, width), lambda i: (i, 0))
    return _call(
        body, name=f"fwd_ple{layer}", grid=(nt,),
        in_specs=[row(d), pl.BlockSpec((None, ts, pd), lambda i: (layer, i, 0)), _full((1, d)), ANY, ANY],
        out_specs=[row(d), row(d)],
        out_shape=[_sds((s, d), F32), _sds((s, d), BF16)],
        scratch_shapes=[pltpu.VMEM((d, d), BF16), pltpu.VMEM((4, pd, d // 4), BF16), pltpu.SemaphoreType.DMA((2,))],
        operands=[h1, p, gain, w_gate, w_proj], rider=rider)


def _bwd_ple(dh2, h1, gate, p, gain, w_gate, w_proj, layer, rider=None, loss_head=None):
    s, d = dh2.shape
    pd = p.shape[-1]
    ts = min(TS_PLE, s)
    nt = s // ts
    qd = d // 4
    n_head = 0 if loss_head is None else 2

    def body(*refs):
        dh2_ref = refs[0]
        h1_ref, gate_ref, p_ref, gain_ref, wg_hbm, wp_hbm, dh1_ref, dgain_ref, dwg_hbm, dwp_hbm = refs[1 + n_head:11 + n_head]
        wg_v, wp_v, acc_g, acc_p, sems = refs[-5:]
        i = pl.program_id(0)

        @pl.when(i == 0)
        def _():
            _copy_all([(wg_hbm, wg_v), (wp_hbm, wp_v)], sems)
            dgain_ref[...] = jnp.zeros_like(dgain_ref)
            acc_g[...] = jnp.zeros_like(acc_g)
            acc_p[...] = jnp.zeros_like(acc_p)

        if loss_head is None:
            g2 = dh2_ref[...]
        else:
            t_ref, fgain_ref, loss_ref, dfgain_ref = refs[1], refs[2], refs[11 + n_head], refs[12 + n_head]

            @pl.when(i == 0)
            def _():
                loss_ref[...] = jnp.zeros_like(loss_ref)
                dfgain_ref[...] = jnp.zeros_like(dfgain_ref)

            xf, rf = _rms(dh2_ref[...])
            err = xf * fgain_ref[...] - t_ref[...]
            part = 0.5 * jnp.sum(jnp.mean(err * err, axis=-1, keepdims=True), axis=0, keepdims=True)
            loss_ref[...] += jnp.broadcast_to(part, loss_ref.shape)
            g2, dfgain = _rms_bwd(err * (1.0 / d), xf, rf, fgain_ref[...])
            dfgain_ref[...] += dfgain
        gate_f = gate_ref[...].astype(F32)
        xn, r = _rms(h1_ref[...])
        hpb = (xn * gain_ref[...]).astype(BF16)
        pb = p_ref[...].astype(BF16)
        pe = jnp.concatenate([_dot(pb, wp_v[k]) for k in range(4)], axis=1)
        dpeb = (g2 * gate_f).astype(BF16)
        dab = ((g2 * pe) * (gate_f * (1.0 - gate_f))).astype(BF16)
        acc_g[...] += _dot_tn(hpb, dab)
        for k in range(4):
            acc_p[k] += _dot_tn(pb, dpeb[:, k * qd:(k + 1) * qd])
        dhp = _dot_nt(dab, wg_v[...])
        dh, dgain = _rms_bwd(dhp, xn, r, gain_ref[...])
        dh1_ref[...] = g2 + dh
        dgain_ref[...] += dgain

        @pl.when(i == nt - 1)
        def _():
            _copy_all([(acc_g, dwg_hbm), (acc_p, dwp_hbm)], sems)

    row = pl.BlockSpec((ts, d), lambda i: (i, 0))
    head = loss_head is not None
    return _call(
        body, name=f"bwd_ple{layer}", grid=(nt,),
        in_specs=[row] + ([row, _full((1, d))] if head else [])
        + [row, row, pl.BlockSpec((None, ts, pd), lambda i: (layer, i, 0)), _full((1, d)), ANY, ANY],
        out_specs=[row, _full((1, d)), ANY, ANY] + ([_full((8, 128)), _full((1, d))] if head else []),
        out_shape=[_sds((s, d), F32), _sds((1, d), F32), _sds((d, d), F32), _sds((4, pd, qd), F32)]
        + ([_sds((8, 128), F32), _sds((1, d), F32)] if head else []),
        scratch_shapes=[pltpu.VMEM((d, d), BF16), pltpu.VMEM((4, pd, qd), BF16), pltpu.VMEM((d, d), F32),
                        pltpu.VMEM((4, pd, qd), F32), pltpu.SemaphoreType.DMA((2,))],
        operands=[dh2] + (list(loss_head) if head else []) + [h1, gate, p, gain, w_gate, w_proj], rider=rider)


def _mix_a_tile_grads(proj_ref, ch_ref, vh_ref, cw_ref, dh1b, wout_v, carry, dcw_ref, tile, hb):
    e = MIX_WIDTH
    b = proj_ref[:, 0 * e:1 * e].astype(F32)
    c = proj_ref[:, 1 * e:2 * e].astype(F32)
    v = proj_ref[:, 2 * e:3 * e].astype(F32)
    z = proj_ref[:, 3 * e:4 * e].astype(F32)
    cv = c * v
    prev = (ch_ref[...].astype(F32) * vh_ref[...].astype(F32))[hb - HALO:hb]
    tail = jnp.where(tile > 0, prev, jnp.zeros_like(prev))
    cv1 = _shift_down(cv, 1, tail)
    cv2 = _shift_down(cv, 2, tail)
    conv = cw_ref[0:1, :] * cv2 + cw_ref[1:2, :] * cv1 + cw_ref[2:3, :] * cv
    sig = _sigmoid(z)
    sz = z * sig
    y = b * conv
    dm = _dot_nt(dh1b, wout_v[...])
    dz = (dm * y) * (sig * (1.0 + z * (1.0 - sig)))
    dy = dm * sz
    db = dy * conv
    dconv = dy * b
    head = carry[...]
    carry[...] = dconv[0:HALO]
    dcv = cw_ref[2:3, :] * dconv + cw_ref[1:2, :] * _shift_up(dconv, 1, head) + cw_ref[0:1, :] * _shift_up(dconv, 2, head)
    dcw_ref[0:1, :] += jnp.sum(dconv * cv2, axis=0, keepdims=True)
    dcw_ref[1:2, :] += jnp.sum(dconv * cv1, axis=0, keepdims=True)
    dcw_ref[2:3, :] += jnp.sum(dconv * cv, axis=0, keepdims=True)
    parts = [db.astype(BF16), (dcv * v).astype(BF16), (dcv * c).astype(BF16), dz.astype(BF16)]
    return parts, (sz * y).astype(BF16)


def _bwd_mix_a(dh1, h, proj, gain, conv_w, w_in, w_out, name, rider=None):
    s, d = dh1.shape
    e = MIX_WIDTH
    ts = min(TS_MIX, s)
    nt = s // ts
    hb = 16
    per = ts // hb

    def body(dh1_ref, h_ref, proj_ref, ch_ref, vh_ref, gain_ref, cw_ref, win_hbm, wout_hbm,
             dh_ref, dcw_ref, dgain_ref, dwin_hbm, dwout_hbm, win_v, wout_v, acc_in, acc_out, carry, sems):
        i = pl.program_id(0)

        @pl.when(i == 0)
        def _():
            _copy_all([(win_hbm, win_v), (wout_hbm, wout_v)], sems)
            carry[...] = jnp.zeros_like(carry)
            dcw_ref[...] = jnp.zeros_like(dcw_ref)
            dgain_ref[...] = jnp.zeros_like(dgain_ref)
            acc_in[...] = jnp.zeros_like(acc_in)
            acc_out[...] = jnp.zeros_like(acc_out)

        dh1 = dh1_ref[...]
        dh1b = dh1.astype(BF16)
        parts, mb = _mix_a_tile_grads(proj_ref, ch_ref, vh_ref, cw_ref, dh1b, wout_v, carry, dcw_ref, nt - 1 - i, hb)
        acc_out[...] += _dot_tn(mb, dh1b)
        xn, r = _rms(h_ref[...])
        hnb = (xn * gain_ref[...]).astype(BF16)
        for q in range(4):
            acc_in[q] += _dot_tn(hnb, parts[q])
        dhn = _dot_nt(parts[0], win_v[0]) + _dot_nt(parts[1], win_v[1]) + _dot_nt(parts[2], win_v[2]) + _dot_nt(parts[3], win_v[3])
        dh, dgain = _rms_bwd(dhn, xn, r, gain_ref[...])
        dh_ref[...] = dh1 + dh
        dgain_ref[...] += dgain

        @pl.when(i == nt - 1)
        def _():
            _copy_all([(acc_in, dwin_hbm), (acc_out, dwout_hbm)], sems)

    row = lambda width: pl.BlockSpec((ts, width), lambda i: (nt - 1 - i, 0))
    halo = lambda col: pl.BlockSpec((hb, e), lambda i: (jnp.maximum((nt - 1 - i) * per - 1, 0), col))
    return _call(
        body, name=name, grid=(nt,),
        in_specs=[row(d), row(d), row(4 * e), halo(1), halo(2), _full((1, d)), _full((8, e)), ANY, ANY],
        out_specs=[row(d), _full((8, e)), _full((1, d)), ANY, ANY],
        out_shape=[_sds((s, d), F32), _sds((8, e), F32), _sds((1, d), F32), _sds((4, d, e), F32), _sds((e, d), F32)],
        scratch_shapes=[pltpu.VMEM((4, d, e), BF16), pltpu.VMEM((e, d), BF16), pltpu.VMEM((4, d, e), F32),
                        pltpu.VMEM((e, d), F32), pltpu.VMEM((HALO, e), F32), pltpu.SemaphoreType.DMA((2,))],
        operands=[dh1, h, proj, proj, proj, gain, conv_w, w_in, w_out], rider=rider)


def _bwd_mix_a_weights(dh1, h, proj, gain, conv_w, w_out, name, rider=None):
    s, d = dh1.shape
    e = MIX_WIDTH
    ts = min(TS_MIX, s)
    nt = s // ts
    hb = 16
    per = ts // hb

    def body(dh1_ref, h_ref, proj_ref, ch_ref, vh_ref, gain_ref, cw_ref, wout_hbm,
             dproj_ref, dcw_ref, dwin_hbm, dwout_hbm, wout_v, acc_in, acc_out, carry, sems):
        i = pl.program_id(0)

        @pl.when(i == 0)
        def _():
            _copy_all([(wout_hbm, wout_v)], sems)
            carry[...] = jnp.zeros_like(carry)
            dcw_ref[...] = jnp.zeros_like(dcw_ref)
            acc_in[...] = jnp.zeros_like(acc_in)
            acc_out[...] = jnp.zeros_like(acc_out)

        dh1b = dh1_ref[...].astype(BF16)
        parts, mb = _mix_a_tile_grads(proj_ref, ch_ref, vh_ref, cw_ref, dh1b, wout_v, carry, dcw_ref, nt - 1 - i, hb)
        acc_out[...] += _dot_tn(mb, dh1b)
        xn, _ = _rms(h_ref[...])
        hnb = (xn * gain_ref[...]).astype(BF16)
        for q in range(4):
            acc_in[q] += _dot_tn(hnb, parts[q])
            dproj_ref[:, q * e:(q + 1) * e] = parts[q]

        @pl.when(i == nt - 1)
        def _():
            _copy_all([(acc_in, dwin_hbm), (acc_out, dwout_hbm)], sems)

    row = lambda width: pl.BlockSpec((ts, width), lambda i: (nt - 1 - i, 0))
    halo = lambda col: pl.BlockSpec((hb, e), lambda i: (jnp.maximum((nt - 1 - i) * per - 1, 0), col))
    return _call(
        body, name=name, grid=(nt,),
        in_specs=[row(d), row(d), row(4 * e), halo(1), halo(2), _full((1, d)), _full((8, e)), ANY],
        out_specs=[row(4 * e), _full((8, e)), ANY, ANY],
        out_shape=[_sds((s, 4 * e), BF16), _sds((8, e), F32), _sds((4, d, e), F32), _sds((e, d), F32)],
        scratch_shapes=[pltpu.VMEM((e, d), BF16), pltpu.VMEM((4, d, e), F32), pltpu.VMEM((e, d), F32),
                        pltpu.VMEM((HALO, e), F32), pltpu.SemaphoreType.DMA((2,))],
        operands=[dh1, h, proj, proj, proj, gain, conv_w, w_out], rider=rider)


def _bwd_mix_a_input(dproj, h, dh1, gain, w_in, name, rider=None):
    s, d = dh1.shape
    e = MIX_WIDTH
    ts = min(TS_PLE, s)
    nt = s // ts

    def body(dproj_ref, h_ref, dh1_ref, gain_ref, win_hbm, dh_ref, dgain_ref, win_v, sems):
        @pl.when(pl.program_id(0) == 0)
        def _():
            _copy_all([(win_hbm, win_v)], sems)
            dgain_ref[...] = jnp.zeros_like(dgain_ref)

        dhn = _dot_nt(dproj_ref[:, 0:e], win_v[0])
        for q in range(1, 4):
            dhn = dhn + _dot_nt(dproj_ref[:, q * e:(q + 1) * e], win_v[q])
        xn, r = _rms(h_ref[...])
        dh, dgain = _rms_bwd(dhn, xn, r, gain_ref[...])
        dh_ref[...] = dh1_ref[...] + dh
        dgain_ref[...] += dgain

    row = lambda width: pl.BlockSpec((ts, width), lambda i: (i, 0))
    return _call(
        body, name=name, grid=(nt,),
        in_specs=[row(4 * e), row(d), row(d), _full((1, d)), ANY],
        out_specs=[row(d), _full((1, d))],
        out_shape=[_sds((s, d), F32), _sds((1, d), F32)],
        scratch_shapes=[pltpu.VMEM((4, d, e), BF16), pltpu.SemaphoreType.DMA((1,))],
        operands=[dproj, h, dh1, gain, w_in], rider=rider)


def _bwd_mix_b(dh1, h, z, mx, diff, gain, scale, w_in, w_grp, w_out, name, rider=None):
    s, d = dh1.shape
    e = MIX_WIDTH
    ts = min(TS_MIX, s)
    nt = s // ts
    half = e // 2

    def body(dh1_ref, h_ref, z_ref, mx_ref, dd_ref, gain_ref, scale_ref, win_hbm, wgrp_hbm, wout_hbm,
             dh_ref, dscale_ref, dgain_ref, dwin_hbm, dwgrp_hbm, dwout_hbm,
             win_v, wgrp_v, wout_v, acc_in, acc_grp, acc_out, carry, sems):
        i = pl.program_id(0)
        tile = nt - 1 - i

        @pl.when(i == 0)
        def _():
            _copy_all([(win_hbm, win_v), (wout_hbm, wout_v)] + _grp_pairs(wgrp_hbm, wgrp_v), sems)
            carry[...] = jnp.zeros_like(carry)
            dscale_ref[...] = jnp.zeros_like(dscale_ref)
            dgain_ref[...] = jnp.zeros_like(dgain_ref)
            acc_in[...] = jnp.zeros_like(acc_in)
            acc_grp[...] = jnp.zeros_like(acc_grp)
            acc_out[...] = jnp.zeros_like(acc_out)

        zf = z_ref[...].astype(F32)
        mxf = mx_ref[...].astype(F32)
        sig = _sigmoid(zf)
        sz = zf * sig
        mixed = mxf * scale_ref[...]
        dh1 = dh1_ref[...]
        dh1b = dh1.astype(BF16)
        acc_out[...] += _dot_tn((sz * mixed).astype(BF16), dh1b)
        dm = _dot_nt(dh1b, wout_v[...])
        dz = (dm * mixed) * (sig * (1.0 + zf * (1.0 - sig)))
        dmixed = dm * sz
        dscale_ref[...] += jnp.sum(dmixed * mxf, axis=0, keepdims=True)
        dmxb = (dmixed * scale_ref[...]).astype(BF16)
        diff = dd_ref[...]
        for g in range(N_GROUPS):
            cols = slice(g * GROUP_DIM, (g + 1) * GROUP_DIM)
            acc_grp[g] += _dot_tn(diff[:, cols], dmxb[:, cols])
        ddiff = jnp.concatenate(
            [_dot_nt(dmxb[:, g * GROUP_DIM:(g + 1) * GROUP_DIM], wgrp_v[g]) for g in range(N_GROUPS)], axis=1)
        dub = (_pool_bwd(ddiff, carry, tile, ts) - ddiff).astype(BF16)
        dzb = dz.astype(BF16)
        parts = [dub[:, 0:half], dub[:, half:e], dzb[:, 0:half], dzb[:, half:e]]
        xn, r = _rms(h_ref[...])
        hnb = (xn * gain_ref[...]).astype(BF16)
        for k in range(4):
            acc_in[k] += _dot_tn(hnb, parts[k])
        dhn = _dot_nt(parts[0], win_v[0]) + _dot_nt(parts[1], win_v[1]) + _dot_nt(parts[2], win_v[2]) + _dot_nt(parts[3], win_v[3])
        dh, dgain = _rms_bwd(dhn, xn, r, gain_ref[...])
        dh_ref[...] = dh1 + dh
        dgain_ref[...] += dgain

        @pl.when(i == nt - 1)
        def _():
            _copy_all([(acc_in, dwin_hbm), (acc_out, dwout_hbm)] + [(v, hb_) for hb_, v in _grp_pairs(dwgrp_hbm, acc_grp)], sems)

    row = lambda width: pl.BlockSpec((ts, width), lambda i: (nt - 1 - i, 0))
    return _call(
        body, name=name, grid=(nt,),
        in_specs=[row(d), row(d), row(e), row(e), row(e), _full((1, d)), _full((1, e)), ANY, ANY, ANY],
        out_specs=[row(d), _full((1, e)), _full((1, d)), ANY, ANY, ANY],
        out_shape=[_sds((s, d), F32), _sds((1, e), F32), _sds((1, d), F32), _sds((4, d, half), F32),
                   _sds((4, N_GROUPS, GROUP_DIM // 4, GROUP_DIM), F32), _sds((e, d), F32)],
        scratch_shapes=[pltpu.VMEM((4, d, half), BF16), pltpu.VMEM((N_GROUPS, GROUP_DIM, GROUP_DIM), BF16),
                        pltpu.VMEM((e, d), BF16), pltpu.VMEM((4, d, half), F32),
                        pltpu.VMEM((N_GROUPS, GROUP_DIM, GROUP_DIM), F32), pltpu.VMEM((e, d), F32),
                        pltpu.VMEM((4, HALO, e), F32), pltpu.SemaphoreType.DMA((18,))],
        operands=[dh1, h, z, mx, diff, gain, scale, w_in, w_grp, w_out], rider=rider)


def _first_gather(rider, small):
    shards = rider.inputs
    ni = len(shards)

    def body(*refs):
        rin, small_src = refs[:ni], refs[ni]
        rout, small_dst = refs[ni + 1:2 * ni + 1], refs[2 * ni + 1]
        send, recv, ssend, srecv = refs[2 * ni + 2:]
        x, y, c, chips = _place()
        me = 2 * x + y
        peers = [(cx, cy, c) for cx, cy in chips] + [(x, y, 1 - c)]
        vec = [_remote(small_src, small_dst.at[me], ssend, srecv, j, to) for j, to in enumerate(peers)]
        for cp in vec:
            cp.start()
        rider.start(rin, rout, send, recv)
        rider.middle(rin, rout, send, recv)
        rider.finish(rin, rout, send, recv)
        for j, (px, py, _) in enumerate(peers):
            _remote(small_src, small_dst.at[2 * px + py], ssend, srecv, j, peers[j]).wait_recv()
        for cp in vec:
            cp.wait_send()

    outs = pl.pallas_call(
        body, name="first_gather", in_specs=[ANY] * (ni + 1), out_specs=[ANY] * (ni + 1),
        out_shape=rider.out_shapes + [_sds((4,) + small.shape, small.dtype)],
        scratch_shapes=[pltpu.SemaphoreType.DMA((rider.n_sems,)), pltpu.SemaphoreType.DMA((rider.n_sems,)),
                        pltpu.SemaphoreType.DMA((4,)), pltpu.SemaphoreType.DMA((4,))],
    )(*shards, small)
    return list(outs[:ni]), outs[ni]


def _vector_rider(pack):
    flips = [(fx, fy, fc) for fx in (0, 1) for fy in (0, 1) for fc in (0, 1)][1:]

    def copies(rin, rout, send, recv, base):
        x, y, c, _ = _place()
        me = 4 * x + 2 * y + c
        peers = [(1 - x if fx else x, 1 - y if fy else y, 1 - c if fc else c) for fx, fy, fc in flips]
        own = pltpu.make_async_copy(rin[0], rout[0].at[me], send.at[base + 7])
        out = [_remote(rin[0], rout[0].at[me], send, recv, base + r, peer) for r, peer in enumerate(peers)]
        back = [_remote(rin[0], rout[0].at[4 * px + 2 * py + pc], send, recv, base + r, (px, py, pc))
                for r, (px, py, pc) in enumerate(peers)]
        return own, out, back

    def start(rin, rout, send, recv, base=0):
        own, out, _ = copies(rin, rout, send, recv, base)
        own.start()
        for cp in out:
            cp.start()

    def finish(rin, rout, send, recv, base=0):
        own, out, back = copies(rin, rout, send, recv, base)
        for cp in back:
            cp.wait_recv()
        for cp in out:
            cp.wait_send()
        own.wait()

    return _Rider([pack], [_sds((8,) + pack.shape, pack.dtype)], 8, start, finish)


def _vector_sum(landed, row_counts):
    _, rows, d = landed.shape

    def body(l_ref, *out_refs):
        first = 0
        for n, out_ref in zip(row_counts, out_refs):
            total = l_ref[0, first:first + n, :]
            for dev in range(1, 8):
                total = total + l_ref[dev, first:first + n, :]
            out_ref[...] = total
            first += n

    vmem = pl.BlockSpec(memory_space=pltpu.VMEM)
    return pl.pallas_call(body, name="vector_sum", in_specs=[vmem], out_specs=[vmem] * len(row_counts),
                          out_shape=[_sds((n, d), F32) for n in row_counts])(landed)


def _job_rows(rows, cols):
    return min(rows, max(8, JOB_BLOCK_BYTES // (4 * cols)))


def _pair_sum_job(grad, sibling_rows):
    _, _, rh, cols = grad.shape
    tr = _job_rows(rh, cols)
    nr = rh // tr

    def chip_of(j, pos):
        return jnp.bitwise_xor(pos[0], jnp.where(j == 2, 3, 2 - j))

    return dict(
        ins=[(grad, (None, None, tr, cols), lambda l, pos: (chip_of(l // nr, pos), pos[1], l % nr, 0)),
             (sibling_rows, (None, tr, cols), lambda l, pos: (chip_of(l // nr, pos), l % nr, 0))],
        outs=[((3, rh, cols), BF16, (None, tr, cols), lambda l, pos: (l // nr, l % nr, 0))],
        steps=3 * nr, fn=lambda g, sb: [(g + sb).astype(BF16)], alias=None)


def _final_sum_job(grad, sibling_rows, landed, stack, slot, n_slots):
    _, _, rh, cols = grad.shape
    tr = _job_rows(rh, cols)

    def fn(g, sb, ld):
        total = g + sb
        for j in range(3):
            total = total + ld[j].astype(F32)
        return [total]

    return dict(
        ins=[(grad, (None, None, tr, cols), lambda l, pos: (pos[0], pos[1], l, 0)),
             (sibling_rows, (None, tr, cols), lambda l, pos: (pos[0], l, 0)),
             (landed, (3, tr, cols), lambda l, pos: (0, l, 0))],
        outs=[((n_slots, 2, rh, cols), F32, (None, None, tr, cols), lambda l, pos: (slot, pos[1], l, 0))],
        steps=rh // tr, fn=fn, alias=stack)


def _adamw_job(g, w, m, v, block_bytes):
    rows, cols = g.shape
    tr = min(rows, max(8, block_bytes // (4 * cols)))

    def fn(gg, ww, mm, vv):
        nm = ADAM_B1 * mm + (1.0 - ADAM_B1) * gg
        nv = ADAM_B2 * vv + (1.0 - ADAM_B2) * (gg * gg)
        m_hat = nm / (1.0 - ADAM_B1 ** ADAM_STEP)
        v_hat = nv / (1.0 - ADAM_B2 ** ADAM_STEP)
        return [-ADAM_LR * (m_hat / (jnp.sqrt(v_hat) + ADAM_EPS) + ADAM_WD * ww), nm, nv, gg]

    block = lambda l, pos: (l, 0)
    return dict(ins=[(a, (tr, cols), block) for a in (g, w, m, v)],
                outs=[((rows, cols), F32, (tr, cols), block)] * 4, steps=rows // tr, fn=fn, alias=None)


def _run_jobs(jobs, place, name):
    starts, total = [], 0
    for jb in jobs:
        starts.append(total)
        total += jb["steps"]

    def clamped(fn, start, steps):
        return lambda s, pos: fn(jnp.clip(s - start, 0, steps - 1), pos)

    in_specs, operands = [], [place]
    for jb, start in zip(jobs, starts):
        for arr, block, fn in jb["ins"]:
            in_specs.append(pl.BlockSpec(block, clamped(fn, start, jb["steps"])))
            operands.append(arr)
    n_ins = len(in_specs)
    first_out, n_outs = [], 0
    for jb in jobs:
        first_out.append(n_outs)
        n_outs += len(jb["outs"])
    aliases = {}
    for t, jb in enumerate(jobs):
        if jb["alias"] is not None:
            in_specs.append(ANY)
            operands.append(jb["alias"])
            aliases[len(operands) - 1] = first_out[t]
    out_specs = [pl.BlockSpec(block, clamped(fn, start, jb["steps"]))
                 for jb, start in zip(jobs, starts) for _, _, block, fn in jb["outs"]]

    def body(place_ref, *refs):
        in_refs, out_refs = refs[:n_ins], refs[len(in_specs):]
        s = pl.program_id(0)
        first = 0
        for t, (jb, start) in enumerate(zip(jobs, starts)):
            mine = in_refs[first:first + len(jb["ins"])]
            first += len(jb["ins"])

            @pl.when((s >= start) & (s < start + jb["steps"]))
            def _(mine=mine, t=t, jb=jb):
                values = jb["fn"](*[r[...] for r in mine])
                for n, value in enumerate(values):
                    out_refs[first_out[t] + n][...] = value

    grid_spec = pltpu.PrefetchScalarGridSpec(num_scalar_prefetch=1, grid=(total,), in_specs=in_specs, out_specs=out_specs)
    outs = pl.pallas_call(body, name=name, grid_spec=grid_spec,
                          out_shape=[_sds(shape, dtype) for jb in jobs for shape, dtype, _, _ in jb["outs"]],
                          input_output_aliases=aliases, compiler_params=_params(1))(*operands)
    return [list(outs[first_out[t]:first_out[t] + len(jb["outs"])]) for t, jb in enumerate(jobs)]


BIG = ["a_w_in", "a_w_out", "b_w_in", "b_w_grp", "b_w_out", "ple_w_gate", "ple_w_proj"]

GATHER_PLAN = {
    "first": [("a_w_in", 0), ("a_w_out", 0)],
    "mix0": [("ple_w_gate", 0), ("ple_w_proj", 0), ("b_w_in", 0), ("b_w_grp", 0), ("b_w_out", 0)],
    "ple0": [("ple_w_gate", 1), ("ple_w_proj", 1), ("ple_w_proj", 2)],
    "mix1": [("a_w_in", 1), ("a_w_out", 1), ("ple_w_gate", 2)],
    "mix2": [("b_w_in", 1), ("b_w_grp", 1), ("b_w_out", 1), ("ple_w_gate", 3), ("ple_w_proj", 3)],
}


def _as_2d(name, a):
    if name == "b_w_grp":
        return a.reshape(a.shape[0], N_GROUPS * (GROUP_DIM // 4), GROUP_DIM)
    return a


def kernel(x, p, norm_mix, a_w_in, a_w_conv, a_w_out, b_w_in, b_w_grp, b_scale, b_w_out, ple_norm, ple_w_gate, ple_w_proj, final_norm, loss_target, m_norm_mix, m_a_w_in, m_a_w_conv, m_a_w_out, m_b_w_in, m_b_w_grp, m_b_scale, m_b_w_out, m_ple_norm, m_ple_w_gate, m_ple_w_proj, m_final_norm, v_norm_mix, v_a_w_in, v_a_w_conv, v_a_w_out, v_b_w_in, v_b_w_grp, v_b_scale, v_b_w_out, v_ple_norm, v_ple_w_gate, v_ple_w_proj, v_final_norm):
    d, e = D_MODEL, MIX_WIDTH
    s = x.shape[1]
    cx, cy, cc = lax.axis_index("x"), lax.axis_index("y"), lax.axis_index("c")
    chip = 2 * cx + cy
    place = jnp.stack([chip, cc]).astype(jnp.int32)

    weights = dict(a_w_in=a_w_in, a_w_out=a_w_out, b_w_in=b_w_in, b_w_grp=b_w_grp, b_w_out=b_w_out,
                   ple_w_gate=ple_w_gate, ple_w_proj=ple_w_proj)
    moms = dict(a_w_in=m_a_w_in, a_w_out=m_a_w_out, b_w_in=m_b_w_in, b_w_grp=m_b_w_grp, b_w_out=m_b_w_out,
                ple_w_gate=m_ple_w_gate, ple_w_proj=m_ple_w_proj)
    vars_ = dict(a_w_in=v_a_w_in, a_w_out=v_a_w_out, b_w_in=v_b_w_in, b_w_grp=v_b_w_grp, b_w_out=v_b_w_out,
                 ple_w_gate=v_ple_w_gate, ple_w_proj=v_ple_w_proj)
    w2d = {nm: _as_2d(nm, weights[nm]) for nm in BIG}
    bf = {nm: w2d[nm].astype(BF16).reshape(w2d[nm].shape[0], 2, w2d[nm].shape[1] // 2, w2d[nm].shape[2]) for nm in BIG}
    gathered = {}

    def gather_rider(host):
        keys = GATHER_PLAN.get(host)
        return _gather_rider([bf[nm] for nm, _ in keys], [j for _, j in keys]) if keys else None

    def keep(host, landed):
        for k, a in zip(GATHER_PLAN.get(host, []), landed):
            gathered[k] = a

    def weight(nm, j):
        a = gathered[(nm, j)]
        shapes = {"a_w_in": (4, d, e), "a_w_out": (e, d), "b_w_in": (4, d, e // 2),
                  "b_w_grp": (4, N_GROUPS, GROUP_DIM // 4, GROUP_DIM), "b_w_out": (e, d), "ple_w_gate": (d, d),
                  "ple_w_proj": (4, PLE_DIM, d // 4)}
        return a.reshape(shapes[nm])

    pad = jnp.zeros((4, e // 4), F32)
    small = jnp.concatenate([a_w_conv[0], b_scale[0:1], pad, a_w_conv[1], b_scale[1:2], pad], axis=0)
    landed, small_full = _first_gather(gather_rider("first"), small)
    keep("first", landed)
    small_full = small_full.transpose(1, 0, 2).reshape(16, e)
    conv_w = [_Block(small_full.reshape(2, 8, e), j) for j in range(2)]
    scale_w = [_Block(small_full.reshape(16, 1, e), 8 * j + 3) for j in range(2)]

    p3 = p.reshape(DEPTH, s, PLE_DIM)
    mix_gain = [_Block(norm_mix.reshape(DEPTH, 1, d), i) for i in range(DEPTH)]
    ple_gain = [_Block(ple_norm.reshape(DEPTH, 1, d), i) for i in range(DEPTH)]

    h = x.reshape(s, d)
    saved = []
    for i in range(DEPTH):
        j = i // 2
        rider = gather_rider(f"mix{i}")
        ple = (p3, i, ple_gain[i], weight("ple_w_gate", i), weight("ple_w_proj", i)) if i > 0 else None
        if i % 2 == 0:
            outs, landed = _fwd_mix_a(h, mix_gain[i], conv_w[j], weight("a_w_in", j), weight("a_w_out", j),
                                      f"fwd_mix_a{j}", rider, ple)
            mix = dict(proj=outs[1])
        else:
            outs, landed = _fwd_mix_b(h, mix_gain[i], scale_w[j], weight("b_w_in", j), weight("b_w_grp", j),
                                      weight("b_w_out", j), f"fwd_mix_b{j}", rider, ple)
            mix = dict(z=outs[1], mx=outs[2], diff=outs[3])
        keep(f"mix{i}", landed)
        h1 = outs[0]
        if ple:
            h2, gate = outs[-2:]
        else:
            (h2, gate), landed = _fwd_ple(h1, p3, ple_gain[i], weight("ple_w_gate", i), weight("ple_w_proj", i), i,
                                          gather_rider(f"ple{i}"))
            keep(f"ple{i}", landed)
        saved.append(dict(h=h, h1=h1, gate=gate, **mix))
        h = h2

    n_slots = {nm: weights[nm].shape[0] for nm in BIG}
    stacks = {nm: None for nm in BIG}

    class Group:
        def __init__(self, keys, grads):
            self.keys, self.stage = keys, 0
            self.g32 = [g.reshape(4, 2, w2d[nm].shape[1] // 2, w2d[nm].shape[2]) for (nm, _), g in zip(keys, grads)]

        def rider(self):
            if self.stage == 0:
                return _pair_rider(self.g32)
            if self.stage == 1:
                return _ici_rider(self.pair_sums)
            return _final_rider([stacks[nm] for nm, _ in self.keys], [j for _, j in self.keys])

        def jobs_after(self, landed):
            if self.stage == 0:
                self.from_sibling = landed
                return [_pair_sum_job(g, sb) for g, sb in zip(self.g32, landed)]
            if self.stage == 1:
                return [_final_sum_job(g, sb, ld, stacks[nm], j, n_slots[nm])
                        for (nm, j), g, sb, ld in zip(self.keys, self.g32, self.from_sibling, landed)]
            return []

        def advance(self, landed, summed):
            if self.stage == 0:
                self.pair_sums = summed
            else:
                for (nm, _), a in zip(self.keys, summed if self.stage == 1 else landed):
                    stacks[nm] = a
            self.stage += 1

    active = []
    batches = [0]

    def riders_now():
        parts = [g.rider() for g in active]
        return parts, _merge(parts)

    def advance_all(parts, landed):
        groups = list(active)
        pieces = _split(landed, parts)
        jobs = [g.jobs_after(l) for g, l in zip(groups, pieces)]
        flat = sum(jobs, [])
        outs = [o[0] for o in _run_jobs(flat, place, f"reduce_sums{batches[0]}")] if flat else []
        batches[0] += 1
        for g, l, jb in zip(groups, pieces, jobs):
            g.advance(l, outs[:len(jb)])
            outs = outs[len(jb):]
            if g.stage == 3:
                active.remove(g)

    d_mix_gain, d_ple_gain = [None] * DEPTH, [None] * DEPTH
    d_conv, d_scale = [None] * 2, [None] * 2
    for i in reversed(range(DEPTH)):
        j = i // 2
        sv = saved[i]
        parts, rider = riders_now()
        if i == DEPTH - 1:
            (dh1, d_ple_gain[i], dwg, dwp, loss_part, d_final), landed = _bwd_ple(
                h, sv["h1"], sv["gate"], p3, ple_gain[i], weight("ple_w_gate", i), weight("ple_w_proj", i), i, rider,
                loss_head=(loss_target.reshape(s, d), final_norm.reshape(1, d)))
        else:
            (dh1, d_ple_gain[i], dwg, dwp), landed = _bwd_ple(
                dh, sv["h1"], sv["gate"], p3, ple_gain[i], weight("ple_w_gate", i), weight("ple_w_proj", i), i, rider)
        advance_all(parts, landed)
        active.append(Group([("ple_w_gate", i), ("ple_w_proj", i)], [dwg, dwp]))
        parts, rider = riders_now()
        if i == 0:
            (dproj, d_conv[0], dwin, dwout), landed = _bwd_mix_a_weights(
                dh1, sv["h"], sv["proj"], mix_gain[0], conv_w[0], weight("a_w_out", 0), "bwd_mix_a0_weights", rider)
            advance_all(parts, landed)
            active.append(Group([("a_w_in", 0), ("a_w_out", 0)], [dwin, dwout]))
            parts, rider = riders_now()
            advance_all(parts, _run_rider(rider, "pair_exchange0"))
            parts, rider = riders_now()
            (dh, d_mix_gain[0]), landed = _bwd_mix_a_input(dproj, sv["h"], dh1, mix_gain[0], weight("a_w_in", 0),
                                                          "bwd_mix_a0_input", rider)
            advance_all(parts, landed)
            continue
        if i % 2 == 0:
            (dh, d_conv[j], d_mix_gain[i], dwin, dwout), landed = _bwd_mix_a(
                dh1, sv["h"], sv["proj"], mix_gain[i], conv_w[j], weight("a_w_in", j), weight("a_w_out", j),
                f"bwd_mix_a{j}", rider)
            new = Group([("a_w_in", j), ("a_w_out", j)], [dwin, dwout])
        else:
            (dh, d_scale[j], d_mix_gain[i], dwin, dwgrp, dwout), landed = _bwd_mix_b(
                dh1, sv["h"], sv["z"], sv["mx"], sv["diff"], mix_gain[i], scale_w[j], weight("b_w_in", j),
                weight("b_w_grp", j), weight("b_w_out", j), f"bwd_mix_b{j}", rider)
            new = Group([("b_w_in", j), ("b_w_grp", j), ("b_w_out", j)], [dwin, dwgrp, dwout])
        advance_all(parts, landed)
        active.append(new)
    grad_x = dh.reshape(1, s, d)

    pieces = (d_mix_gain + d_ple_gain + [d_final, d_conv[0][0:3], d_conv[1][0:3]] + d_scale
              + [jnp.tile(loss_part[0:1], (1, d // 128))])
    used = sum(a.shape[0] for a in pieces)
    pack = jnp.concatenate(pieces + [jnp.zeros((-used % PACK_GROUP, d), F32)], axis=0)
    vectors = _vector_rider(pack)
    tail = 0
    while active:
        parts, _ = riders_now()
        extra = [vectors] if tail == 0 else []
        landed = _run_rider(_merge(parts + extra), f"tail_exchange{tail}")
        if extra:
            g_mix, g_ple, g_final, g_conv, g_scale, loss_row = _vector_sum(
                _split(landed, parts + extra)[-1][0], [DEPTH, DEPTH, 1, 6, 2, 1])
        advance_all(parts, landed)
        tail += 1
    loss = loss_row[0, 0]

    mine = lambda a: lax.dynamic_slice_in_dim(a, chip * (e // 4), e // 4, axis=1)
    row = lambda a: a.reshape(1, d)
    taps = lambda a: a.reshape(6, e // 4)
    flat = {nm: (w2d[nm].shape[0] * w2d[nm].shape[1], w2d[nm].shape[2]) for nm in BIG}
    tensors = {nm: (stacks[nm].reshape(flat[nm]), w2d[nm].reshape(flat[nm]), _as_2d(nm, moms[nm]).reshape(flat[nm]),
                    _as_2d(nm, vars_[nm]).reshape(flat[nm])) for nm in BIG}
    tensors.update(
        norm_mix=(g_mix, norm_mix, m_norm_mix, v_norm_mix), ple_norm=(g_ple, ple_norm, m_ple_norm, v_ple_norm),
        final_norm=(g_final, row(final_norm), row(m_final_norm), row(v_final_norm)),
        a_w_conv=(mine(g_conv), taps(a_w_conv), taps(m_a_w_conv), taps(v_a_w_conv)),
        b_scale=(mine(g_scale), b_scale, m_b_scale, v_b_scale))
    order = ["norm_mix", "a_w_in", "a_w_conv", "a_w_out", "b_w_in", "b_w_grp", "b_scale", "b_w_out", "ple_norm",
             "ple_w_gate", "ple_w_proj", "final_norm"]
    shapes = dict(norm_mix=norm_mix.shape, ple_norm=ple_norm.shape, final_norm=final_norm.shape,
                  a_w_conv=a_w_conv.shape, b_scale=b_scale.shape, **{nm: weights[nm].shape for nm in BIG})
    updates = {nm: _run_jobs([_adamw_job(*tensors[nm], ADAMW_BIG_BLOCK_BYTES)], place, f"adamw_{nm}")[0]
               for nm in ADAMW_ALONE}
    rest = [nm for nm in order if nm not in ADAMW_ALONE]
    updates.update(zip(rest, _run_jobs([_adamw_job(*tensors[nm], ADAMW_BLOCK_BYTES) for nm in rest], place, "adamw_rest")))
    outs = [loss, grad_x]
    for which in (3, 0, 1, 2):
        outs += [updates[nm][which].reshape(shapes[nm]) for nm in order]
    return tuple(outs)
```

```python
import jax
import jax.numpy as jnp
from jax import lax
from jax.experimental import pallas as pl
from jax.experimental.pallas import tpu as pltpu

F32 = jnp.float32
BF16 = jnp.bfloat16
MESH = pl.DeviceIdType.MESH

D_MODEL = 1024
MIX_WIDTH = 1024
PLE_DIM = 256
N_GROUPS = 4
GROUP_DIM = 256
POOL_WINDOWS = (2, 4, 8, 16)
DEPTH = 4
EPS = 1e-6

ADAM_LR = 0.001
ADAM_B1 = 0.9
ADAM_B2 = 0.999
ADAM_EPS = 1e-08
ADAM_WD = 0.01
ADAM_STEP = 10

HALO = 8
TS_MIX = 256
TS_FWD = 512
TS_PLE = 512
VMEM_LIMIT = 56 * 1024 * 1024
PACK_GROUP = 8
JOB_BLOCK_BYTES = 2 * 1024 * 1024
ADAMW_BLOCK_BYTES = 512 * 1024
ADAMW_BIG_BLOCK_BYTES = 2 * 1024 * 1024
ADAMW_ALONE = ("a_w_in", "b_w_in", "ple_w_gate")
MIDDLE_STEPS_BEFORE_END = 1

ANY = pl.BlockSpec(memory_space=pl.ANY)


def _sds(shape, dtype):
    return jax.ShapeDtypeStruct(shape, dtype)


def _full(shape):
    nd = len(shape)
    return pl.BlockSpec(shape, lambda *_: (0,) * nd)


def _params(n_axes=1):
    return pltpu.CompilerParams(dimension_semantics=("arbitrary",) * n_axes, vmem_limit_bytes=VMEM_LIMIT)


def _dot(a, b):
    return jnp.dot(a, b, preferred_element_type=F32)


def _dot_nt(a, b):
    return lax.dot_general(a, b, (((1,), (1,)), ((), ())), preferred_element_type=F32)


def _dot_tn(a, b):
    return lax.dot_general(a, b, (((0,), (0,)), ((), ())), preferred_element_type=F32)


def _sigmoid(z):
    return 1.0 / (1.0 + jnp.exp(-z))


def _shift_down(x, k, tail):
    rolled = pltpu.roll(x, k, 0)
    rt = tail if k % HALO == 0 else pltpu.roll(tail, k % HALO, 0)
    row = lax.broadcasted_iota(jnp.int32, rt.shape, 0)
    head = jnp.where(row < k, rt, rolled[0:HALO])
    return jnp.concatenate([head, rolled[HALO:]], axis=0)


def _shift_up(x, k, head_next):
    n = x.shape[0]
    rolled = pltpu.roll(x, n - k, 0)
    rh = head_next if k % HALO == 0 else pltpu.roll(head_next, HALO - k % HALO, 0)
    row = lax.broadcasted_iota(jnp.int32, rh.shape, 0)
    tail = jnp.where(row >= HALO - k, rh, rolled[n - HALO:n])
    return jnp.concatenate([rolled[:n - HALO], tail], axis=0)


def _inv_counts(tile, ts):
    t = tile * ts + lax.broadcasted_iota(jnp.int32, (ts, 1), 0)
    return [1.0 / jnp.minimum(t + 1, w).astype(F32) for w in POOL_WINDOWS]


def _pool_fwd(u, carry, tile, ts):
    inv = _inv_counts(tile, ts)
    outs = []
    for g, w in enumerate(POOL_WINDOWS):
        cols = slice(g * GROUP_DIM, (g + 1) * GROUP_DIM)
        s = u[:, cols]
        level, k = 0, 1
        while k < w:
            tail = carry[level, :, cols]
            carry[level, :, cols] = s[ts - HALO:ts]
            s = s + _shift_down(s, k, tail)
            level, k = level + 1, k * 2
        outs.append(s * inv[g])
    return jnp.concatenate(outs, axis=1)


def _pool_bwd(dd, carry, tile, ts):
    inv = _inv_counts(tile, ts)
    outs = []
    for g, w in enumerate(POOL_WINDOWS):
        cols = slice(g * GROUP_DIM, (g + 1) * GROUP_DIM)
        q = dd[:, cols] * inv[g]
        level, k = 0, 1
        while k < w:
            head = carry[level, :, cols]
            carry[level, :, cols] = q[0:HALO]
            q = q + _shift_up(q, k, head)
            level, k = level + 1, k * 2
        outs.append(q)
    return jnp.concatenate(outs, axis=1)


def _copy_all(pairs, sems):
    copies = [pltpu.make_async_copy(src, dst, sems.at[n]) for n, (src, dst) in enumerate(pairs)]
    for cp in copies:
        cp.start()
    for cp in copies:
        cp.wait()


def _grp_pairs(wgrp_hbm, wgrp_v):
    rows = GROUP_DIM // 4
    return [(wgrp_hbm.at[k, g], wgrp_v.at[g, pl.ds(k * rows, rows), :]) for k in range(4) for g in range(N_GROUPS)]


def _rms(h):
    r = lax.rsqrt(jnp.mean(h * h, axis=-1, keepdims=True) + EPS)
    return h * r, r


def _rms_bwd(dhn, xn, r, gain):
    dgain = jnp.sum(dhn * xn, axis=0, keepdims=True)
    dxn = dhn * gain
    dh = r * (dxn - xn * jnp.mean(dxn * xn, axis=-1, keepdims=True))
    return dh, dgain


class _Rider:
    def __init__(self, inputs, out_shapes, n_sems, start, finish, middle=None, aliases=None):
        self.inputs, self.out_shapes, self.n_sems = list(inputs), list(out_shapes), n_sems
        self.start, self.middle, self.finish = start, middle, finish
        self.aliases = dict(aliases or {})
        self.middle_at_end = False


def _merge(riders):
    riders = [r for r in riders if r is not None]
    if not riders:
        return None
    if len(riders) == 1:
        return riders[0]

    def phase(which):
        def run(rin, rout, send, recv, base=0):
            i0 = o0 = s0 = 0
            for r in riders:
                fn = getattr(r, which)
                if fn is not None:
                    fn(rin[i0:i0 + len(r.inputs)], rout[o0:o0 + len(r.out_shapes)], send, recv, base + s0)
                i0, o0, s0 = i0 + len(r.inputs), o0 + len(r.out_shapes), s0 + r.n_sems
        return run

    aliases, i0, o0 = {}, 0, 0
    for r in riders:
        aliases.update({i0 + a: o0 + b for a, b in r.aliases.items()})
        i0, o0 = i0 + len(r.inputs), o0 + len(r.out_shapes)
    return _Rider(sum([r.inputs for r in riders], []), sum([r.out_shapes for r in riders], []),
                  sum(r.n_sems for r in riders), phase("start"), phase("finish"),
                  phase("middle") if any(r.middle for r in riders) else None, aliases)


def _split(landed, riders):
    out, o0 = [], 0
    for r in riders:
        if r is None:
            out.append(None)
        else:
            out.append(landed[o0:o0 + len(r.out_shapes)])
            o0 += len(r.out_shapes)
    return out


def _place():
    x, y, c = lax.axis_index("x"), lax.axis_index("y"), lax.axis_index("c")
    chips = [(1 - x, y), (x, 1 - y), (1 - x, 1 - y)]
    return x, y, c, chips


def _remote(src, dst, send_sems, recv_sems, sem, to):
    return pltpu.make_async_remote_copy(src_ref=src, dst_ref=dst, send_sem=send_sems.at[sem], recv_sem=recv_sems.at[sem],
                                        device_id=to, device_id_type=MESH)


def _gather_rider(stacked, slots):
    ni = len(stacked)

    def first_hops(rin, rout, send, recv, base, x, y, c, chips):
        me = 2 * x + y
        return [_remote(rin[t].at[slots[t], c], rout[t].at[me, c], send, recv, base + 7 * t + j, (cx, cy, c))
                for j, (cx, cy) in enumerate(chips) for t in range(ni)]

    def passes(rout, send, recv, base, x, y, c, chips):
        out = []
        for j, (cx, cy) in enumerate(chips):
            for t in range(ni):
                landed = rout[t].at[2 * cx + cy, c]
                out.append((_remote(landed, landed, send, recv, base + 7 * t + j, (x, y, 1 - c)),
                            _remote(landed, landed, send, recv, base + 7 * t + 3 + j, (x, y, 1 - c))))
        return out

    def own(rin, rout, send, recv, base, x, y, c):
        return [_remote(rin[t].at[slots[t]], rout[t].at[2 * x + y], send, recv, base + 7 * t + 6, (x, y, 1 - c))
                for t in range(ni)]

    def start(rin, rout, send, recv, base=0):
        x, y, c, chips = _place()
        for cp in first_hops(rin, rout, send, recv, base, x, y, c, chips) + own(rin, rout, send, recv, base, x, y, c):
            cp.start()

    def middle(rin, rout, send, recv, base=0):
        x, y, c, chips = _place()
        for arrival, onward in passes(rout, send, recv, base, x, y, c, chips):
            arrival.wait_recv()
            onward.start()

    def finish(rin, rout, send, recv, base=0):
        x, y, c, chips = _place()
        for j, (cx, cy) in enumerate(chips):
            for t in range(ni):
                other = rout[t].at[2 * cx + cy, 1 - c]
                _remote(other, other, send, recv, base + 7 * t + 3 + j, (x, y, 1 - c)).wait_recv()
        for cp in own(rin, rout, send, recv, base, x, y, c):
            cp.wait_recv()
            cp.wait_send()
        for cp in first_hops(rin, rout, send, recv, base, x, y, c, chips):
            cp.wait_send()
        for _, onward in passes(rout, send, recv, base, x, y, c, chips):
            onward.wait_send()

    return _Rider(stacked, [_sds((4,) + a.shape[1:], a.dtype) for a in stacked], 7 * ni, start, finish, middle)


def _pair_rider(grads):
    ni = len(grads)

    def copies(rin, rout, send, recv, base):
        x, y, c, _ = _place()
        return [_remote(rin[t].at[:, 1 - c], rout[t], send, recv, base + t, (x, y, 1 - c)) for t in range(ni)]

    def start(rin, rout, send, recv, base=0):
        for cp in copies(rin, rout, send, recv, base):
            cp.start()

    def finish(rin, rout, send, recv, base=0):
        for cp in copies(rin, rout, send, recv, base):
            cp.wait()

    return _Rider(grads, [_sds(g.shape[:1] + g.shape[2:], g.dtype) for g in grads], ni, start, finish)


def _ici_rider(pair_sums):
    ni = len(pair_sums)

    def copies(rin, rout, send, recv, base):
        x, y, c, chips = _place()
        return [_remote(rin[t].at[j], rout[t].at[j], send, recv, base + 3 * t + j, (cx, cy, c))
                for j, (cx, cy) in enumerate(chips) for t in range(ni)]

    def start(rin, rout, send, recv, base=0):
        for cp in copies(rin, rout, send, recv, base):
            cp.start()

    def finish(rin, rout, send, recv, base=0):
        for cp in copies(rin, rout, send, recv, base):
            cp.wait()

    return _Rider(pair_sums, [_sds((3,) + g.shape[1:], g.dtype) for g in pair_sums], 3 * ni, start, finish)


def _final_rider(summed, slots):
    ni = len(summed)

    def copies(rout, send, recv, base):
        x, y, c, _ = _place()
        return [(_remote(rout[t].at[slots[t], c], rout[t].at[slots[t], c], send, recv, base + t, (x, y, 1 - c)),
                 _remote(rout[t].at[slots[t], 1 - c], rout[t].at[slots[t], 1 - c], send, recv, base + t, (x, y, 1 - c)))
                for t in range(ni)]

    def start(rin, rout, send, recv, base=0):
        for mine, _ in copies(rout, send, recv, base):
            mine.start()

    def finish(rin, rout, send, recv, base=0):
        for mine, theirs in copies(rout, send, recv, base):
            mine.wait_send()
            theirs.wait_recv()

    return _Rider(summed, [_sds(a.shape, a.dtype) for a in summed], ni, start, finish,
                  aliases={t: t for t in range(ni)})


class _Block:
    def __init__(self, array, index):
        self.array, self.index = array, index

    def spec(self):
        index = self.index
        return pl.BlockSpec((None,) + self.array.shape[1:], lambda *_: (index, 0, 0))


def _call(body, *, name, grid, in_specs, out_specs, out_shape, scratch_shapes, operands, rider=None):
    operands, in_specs = list(operands), list(in_specs)
    for n, op in enumerate(operands):
        if isinstance(op, _Block):
            operands[n], in_specs[n] = op.array, op.spec()
    if rider is None:
        outs = pl.pallas_call(body, name=name, grid=grid, in_specs=in_specs, out_specs=out_specs, out_shape=out_shape,
                              scratch_shapes=scratch_shapes, compiler_params=_params(len(grid)))(*operands)
        return list(outs), []
    n_in, n_out, n_scr = len(in_specs), len(out_specs), len(scratch_shapes)
    r_in, r_out = len(rider.inputs), len(rider.out_shapes)
    steps = 1
    for g in grid:
        steps *= g
    mid = steps - 1 if rider.middle_at_end else max(steps - 1 - MIDDLE_STEPS_BEFORE_END, 0)

    def full_body(*refs):
        own_in, rin = refs[:n_in], refs[n_in:n_in + r_in]
        own_out = refs[n_in + r_in:n_in + r_in + n_out]
        rout = refs[n_in + r_in + n_out:n_in + r_in + n_out + r_out]
        own_scr = refs[n_in + r_in + n_out + r_out:n_in + r_in + n_out + r_out + n_scr]
        send, recv = refs[-2], refs[-1]
        step = pl.program_id(0)
        for axis in range(1, len(grid)):
            step = step * grid[axis] + pl.program_id(axis)

        @pl.when(step == 0)
        def _():
            rider.start(rin, rout, send, recv)

        body(*own_in, *own_out, *own_scr)

        if rider.middle is not None:
            @pl.when(step == mid)
            def _():
                rider.middle(rin, rout, send, recv)

        @pl.when(step == steps - 1)
        def _():
            rider.finish(rin, rout, send, recv)

    outs = pl.pallas_call(
        full_body, name=name, grid=grid,
        in_specs=list(in_specs) + [ANY] * r_in, out_specs=list(out_specs) + [ANY] * r_out,
        out_shape=list(out_shape) + rider.out_shapes,
        scratch_shapes=list(scratch_shapes) + [pltpu.SemaphoreType.DMA((rider.n_sems,)), pltpu.SemaphoreType.DMA((rider.n_sems,))],
        input_output_aliases={n_in + a: n_out + b for a, b in rider.aliases.items()},
        compiler_params=_params(len(grid)),
    )(*operands, *rider.inputs)
    return list(outs[:n_out]), list(outs[n_out:])


def _run_rider(rider, name):
    r_in, r_out = len(rider.inputs), len(rider.out_shapes)

    def body(*refs):
        rin, rout, send, recv = refs[:r_in], refs[r_in:r_in + r_out], refs[-2], refs[-1]
        rider.start(rin, rout, send, recv)
        if rider.middle is not None:
            rider.middle(rin, rout, send, recv)
        rider.finish(rin, rout, send, recv)

    outs = pl.pallas_call(
        body, name=name, in_specs=[ANY] * r_in, out_specs=[ANY] * r_out, out_shape=rider.out_shapes,
        scratch_shapes=[pltpu.SemaphoreType.DMA((rider.n_sems,)), pltpu.SemaphoreType.DMA((rider.n_sems,))],
        input_output_aliases=rider.aliases,
    )(*rider.inputs)
    return list(outs)


def _ple_tile(h1, p_ref, gain_ref, wg_v, wp_v):
    xn, _ = _rms(h1)
    hpb = (xn * gain_ref[...]).astype(BF16)
    gate = _sigmoid(_dot(hpb, wg_v[...]))
    pb = p_ref[...].astype(BF16)
    pe = jnp.concatenate([_dot(pb, wp_v[k]) for k in range(4)], axis=1)
    return h1 + gate * pe, gate


def _ple_parts(ple, ts, d):
    p, layer, gain, w_gate, w_proj = ple
    s, pd = p.shape[1:]
    row = pl.BlockSpec((ts, d), lambda i: (i, 0))
    return dict(
        operands=[p, gain, w_gate, w_proj],
        in_specs=[pl.BlockSpec((None, ts, pd), lambda i: (layer, i, 0)), _full((1, d)), ANY, ANY],
        out_specs=[row, row], out_shape=[_sds((s, d), F32), _sds((s, d), BF16)],
        scratch=[pltpu.VMEM((d, d), BF16), pltpu.VMEM((4, pd, d // 4), BF16)])


def _fwd_mix_a(h, gain, conv_w, w_in, w_out, name, rider=None, ple=None):
    s, d = h.shape
    e = MIX_WIDTH
    ts = min(TS_FWD, s)
    nt = s // ts
    extra = _ple_parts(ple, ts, d) if ple else None

    def body(*refs):
        h_ref, gain_ref, cw_ref, win_hbm, wout_hbm = refs[:5]
        n_in = 9 if ple else 5
        h1_ref, proj_ref = refs[n_in:n_in + 2]
        win_v, wout_v, carry, sems = refs[n_in + (4 if ple else 2):][:4]
        i = pl.program_id(0)

        @pl.when(i == 0)
        def _():
            loads = [(win_hbm, win_v), (wout_hbm, wout_v)]
            if ple:
                loads += [(refs[7], refs[-2]), (refs[8], refs[-1])]
            _copy_all(loads, sems)
            carry[...] = jnp.zeros_like(carry)

        hh = h_ref[...]
        xn, _ = _rms(hh)
        hnb = (xn * gain_ref[...]).astype(BF16)
        b = _dot(hnb, win_v[0])
        c = _dot(hnb, win_v[1])
        v = _dot(hnb, win_v[2])
        z = _dot(hnb, win_v[3])
        proj_ref[:, 0 * e:1 * e] = b.astype(BF16)
        proj_ref[:, 1 * e:2 * e] = c.astype(BF16)
        proj_ref[:, 2 * e:3 * e] = v.astype(BF16)
        proj_ref[:, 3 * e:4 * e] = z.astype(BF16)
        cv = c * v
        tail = carry[...]
        carry[...] = cv[ts - HALO:ts]
        conv = cw_ref[0:1, :] * _shift_down(cv, 2, tail) + cw_ref[1:2, :] * _shift_down(cv, 1, tail) + cw_ref[2:3, :] * cv
        mb = ((z * _sigmoid(z)) * (b * conv)).astype(BF16)
        h1 = hh + _dot(mb, wout_v[...])
        h1_ref[...] = h1
        if ple:
            h2, gate = _ple_tile(h1, refs[5], refs[6], refs[-2], refs[-1])
            refs[n_in + 2][...] = h2
            refs[n_in + 3][...] = gate.astype(BF16)

    row = lambda width: pl.BlockSpec((ts, width), lambda i: (i, 0))
    return _call(
        body, name=name, grid=(nt,),
        in_specs=[row(d), _full((1, d)), _full((8, e)), ANY, ANY] + (extra["in_specs"] if ple else []),
        out_specs=[row(d), row(4 * e)] + (extra["out_specs"] if ple else []),
        out_shape=[_sds((s, d), F32), _sds((s, 4 * e), BF16)] + (extra["out_shape"] if ple else []),
        scratch_shapes=[pltpu.VMEM((4, d, e), BF16), pltpu.VMEM((e, d), BF16), pltpu.VMEM((HALO, e), F32),
                        pltpu.SemaphoreType.DMA((4,))] + (extra["scratch"] if ple else []),
        operands=[h, gain, conv_w, w_in, w_out] + (extra["operands"] if ple else []), rider=rider)


def _fwd_mix_b(h, gain, scale, w_in, w_grp, w_out, name, rider=None, ple=None):
    s, d = h.shape
    e = MIX_WIDTH
    ts = min(TS_FWD, s)
    nt = s // ts
    extra = _ple_parts(ple, ts, d) if ple else None

    def body(*refs):
        h_ref, gain_ref, scale_ref, win_hbm, wgrp_hbm, wout_hbm = refs[:6]
        n_in = 10 if ple else 6
        h1_ref, z_ref, mx_ref, dd_ref = refs[n_in:n_in + 4]
        win_v, wgrp_v, wout_v, carry, sems = refs[n_in + (6 if ple else 4):][:5]
        i = pl.program_id(0)

        @pl.when(i == 0)
        def _():
            loads = [(win_hbm, win_v), (wout_hbm, wout_v)] + _grp_pairs(wgrp_hbm, wgrp_v)
            if ple:
                loads += [(refs[8], refs[-2]), (refs[9], refs[-1])]
            _copy_all(loads, sems)
            carry[...] = jnp.zeros_like(carry)

        hh = h_ref[...]
        xn, _ = _rms(hh)
        hnb = (xn * gain_ref[...]).astype(BF16)
        u = jnp.concatenate([_dot(hnb, win_v[0]), _dot(hnb, win_v[1])], axis=1)
        z = jnp.concatenate([_dot(hnb, win_v[2]), _dot(hnb, win_v[3])], axis=1)
        z_ref[...] = z.astype(BF16)
        diff = (_pool_fwd(u, carry, i, ts) - u).astype(BF16)
        dd_ref[...] = diff
        mx = jnp.concatenate(
            [_dot(diff[:, g * GROUP_DIM:(g + 1) * GROUP_DIM], wgrp_v[g]) for g in range(N_GROUPS)], axis=1)
        mx_ref[...] = mx.astype(BF16)
        mb = ((z * _sigmoid(z)) * (mx * scale_ref[...])).astype(BF16)
        h1 = hh + _dot(mb, wout_v[...])
        h1_ref[...] = h1
        if ple:
            h2, gate = _ple_tile(h1, refs[6], refs[7], refs[-2], refs[-1])
            refs[n_in + 4][...] = h2
            refs[n_in + 5][...] = gate.astype(BF16)

    row = lambda width: pl.BlockSpec((ts, width), lambda i: (i, 0))
    return _call(
        body, name=name, grid=(nt,),
        in_specs=[row(d), _full((1, d)), _full((1, e)), ANY, ANY, ANY] + (extra["in_specs"] if ple else []),
        out_specs=[row(d), row(e), row(e), row(e)] + (extra["out_specs"] if ple else []),
        out_shape=[_sds((s, d), F32)] + [_sds((s, e), BF16)] * 3 + (extra["out_shape"] if ple else []),
        scratch_shapes=[pltpu.VMEM((4, d, e // 2), BF16), pltpu.VMEM((N_GROUPS, GROUP_DIM, GROUP_DIM), BF16),
                        pltpu.VMEM((e, d), BF16), pltpu.VMEM((4, HALO, e), F32), pltpu.SemaphoreType.DMA((20,))]
        + (extra["scratch"] if ple else []),
        operands=[h, gain, scale, w_in, w_grp, w_out] + (extra["operands"] if ple else []), rider=rider)


def _fwd_ple(h1, p, gain, w_gate, w_proj, layer, rider=None):
    s, d = h1.shape
    pd = p.shape[-1]
    ts = min(TS_PLE, s)
    nt = s // ts

    def body(h1_ref, p_ref, gain_ref, wg_hbm, wp_hbm, h2_ref, gate_ref, wg_v, wp_v, sems):
        @pl.when(pl.program_id(0) == 0)
        def _():
            _copy_all([(wg_hbm, wg_v), (wp_hbm, wp_v)], sems)

        hh = h1_ref[...]
        xn, _ = _rms(hh)
        hpb = (xn * gain_ref[...]).astype(BF16)
        gate = _sigmoid(_dot(hpb, wg_v[...]))
        pb = p_ref[...].astype(BF16)
        pe = jnp.concatenate([_dot(pb, wp_v[k]) for k in range(4)], axis=1)
        gate_ref[...] = gate.astype(BF16)
        h2_ref[...] = hh + gate * pe

    row = lambda width: pl.BlockSpec((ts, width), lambda i: (i, 0))
    return _call(
        body, name=f"fwd_ple{layer}", grid=(nt,),
        in_specs=[row(d), pl.BlockSpec((None, ts, pd), lambda i: (layer, i, 0)), _full((1, d)), ANY, ANY],
        out_specs=[row(d), row(d)],
        out_shape=[_sds((s, d), F32), _sds((s, d), BF16)],
        scratch_shapes=[pltpu.VMEM((d, d), BF16), pltpu.VMEM((4, pd, d // 4), BF16), pltpu.SemaphoreType.DMA((2,))],
        operands=[h1, p, gain, w_gate, w_proj], rider=rider)


def _bwd_ple(dh2, h1, gate, p, gain, w_gate, w_proj, layer, rider=None, loss_head=None):
    s, d = dh2.shape
    pd = p.shape[-1]
    ts = min(TS_PLE, s)
    nt = s // ts
    qd = d // 4
    n_head = 0 if loss_head is None else 2

    def body(*refs):
        dh2_ref = refs[0]
        h1_ref, gate_ref, p_ref, gain_ref, wg_hbm, wp_hbm, dh1_ref, dgain_ref, dwg_hbm, dwp_hbm = refs[1 + n_head:11 + n_head]
        wg_v, wp_v, acc_g, acc_p, sems = refs[-5:]
        i = pl.program_id(0)

        @pl.when(i == 0)
        def _():
            _copy_all([(wg_hbm, wg_v), (wp_hbm, wp_v)], sems)
            dgain_ref[...] = jnp.zeros_like(dgain_ref)
            acc_g[...] = jnp.zeros_like(acc_g)
            acc_p[...] = jnp.zeros_like(acc_p)

        if loss_head is None:
            g2 = dh2_ref[...]
        else:
            t_ref, fgain_ref, loss_ref, dfgain_ref = refs[1], refs[2], refs[11 + n_head], refs[12 + n_head]

            @pl.when(i == 0)
            def _():
                loss_ref[...] = jnp.zeros_like(loss_ref)
                dfgain_ref[...] = jnp.zeros_like(dfgain_ref)

            xf, rf = _rms(dh2_ref[...])
            err = xf * fgain_ref[...] - t_ref[...]
            part = 0.5 * jnp.sum(jnp.mean(err * err, axis=-1, keepdims=True), axis=0, keepdims=True)
            loss_ref[...] += jnp.broadcast_to(part, loss_ref.shape)
            g2, dfgain = _rms_bwd(err * (1.0 / d), xf, rf, fgain_ref[...])
            dfgain_ref[...] += dfgain
        gate_f = gate_ref[...].astype(F32)
        xn, r = _rms(h1_ref[...])
        hpb = (xn * gain_ref[...]).astype(BF16)
        pb = p_ref[...].astype(BF16)
        pe = jnp.concatenate([_dot(pb, wp_v[k]) for k in range(4)], axis=1)
        dpeb = (g2 * gate_f).astype(BF16)
        dab = ((g2 * pe) * (gate_f * (1.0 - gate_f))).astype(BF16)
        acc_g[...] += _dot_tn(hpb, dab)
        for k in range(4):
            acc_p[k] += _dot_tn(pb, dpeb[:, k * qd:(k + 1) * qd])
        dhp = _dot_nt(dab, wg_v[...])
        dh, dgain = _rms_bwd(dhp, xn, r, gain_ref[...])
        dh1_ref[...] = g2 + dh
        dgain_ref[...] += dgain

        @pl.when(i == nt - 1)
        def _():
            _copy_all([(acc_g, dwg_hbm), (acc_p, dwp_hbm)], sems)

    row = pl.BlockSpec((ts, d), lambda i: (i, 0))
    head = loss_head is not None
    return _call(
        body, name=f"bwd_ple{layer}", grid=(nt,),
        in_specs=[row] + ([row, _full((1, d))] if head else [])
        + [row, row, pl.BlockSpec((None, ts, pd), lambda i: (layer, i, 0)), _full((1, d)), ANY, ANY],
        out_specs=[row, _full((1, d)), ANY, ANY] + ([_full((8, 128)), _full((1, d))] if head else []),
        out_shape=[_sds((s, d), F32), _sds((1, d), F32), _sds((d, d), F32), _sds((4, pd, qd), F32)]
        + ([_sds((8, 128), F32), _sds((1, d), F32)] if head else []),
        scratch_shapes=[pltpu.VMEM((d, d), BF16), pltpu.VMEM((4, pd, qd), BF16), pltpu.VMEM((d, d), F32),
                        pltpu.VMEM((4, pd, qd), F32), pltpu.SemaphoreType.DMA((2,))],
        operands=[dh2] + (list(loss_head) if head else []) + [h1, gate, p, gain, w_gate, w_proj], rider=rider)


def _mix_a_tile_grads(proj_ref, ch_ref, vh_ref, cw_ref, dh1b, wout_v, carry, dcw_ref, tile, hb):
    e = MIX_WIDTH
    b = proj_ref[:, 0 * e:1 * e].astype(F32)
    c = proj_ref[:, 1 * e:2 * e].astype(F32)
    v = proj_ref[:, 2 * e:3 * e].astype(F32)
    z = proj_ref[:, 3 * e:4 * e].astype(F32)
    cv = c * v
    prev = (ch_ref[...].astype(F32) * vh_ref[...].astype(F32))[hb - HALO:hb]
    tail = jnp.where(tile > 0, prev, jnp.zeros_like(prev))
    cv1 = _shift_down(cv, 1, tail)
    cv2 = _shift_down(cv, 2, tail)
    conv = cw_ref[0:1, :] * cv2 + cw_ref[1:2, :] * cv1 + cw_ref[2:3, :] * cv
    sig = _sigmoid(z)
    sz = z * sig
    y = b * conv
    dm = _dot_nt(dh1b, wout_v[...])
    dz = (dm * y) * (sig * (1.0 + z * (1.0 - sig)))
    dy = dm * sz
    db = dy * conv
    dconv = dy * b
    head = carry[...]
    carry[...] = dconv[0:HALO]
    dcv = cw_ref[2:3, :] * dconv + cw_ref[1:2, :] * _shift_up(dconv, 1, head) + cw_ref[0:1, :] * _shift_up(dconv, 2, head)
    dcw_ref[0:1, :] += jnp.sum(dconv * cv2, axis=0, keepdims=True)
    dcw_ref[1:2, :] += jnp.sum(dconv * cv1, axis=0, keepdims=True)
    dcw_ref[2:3, :] += jnp.sum(dconv * cv, axis=0, keepdims=True)
    parts = [db.astype(BF16), (dcv * v).astype(BF16), (dcv * c).astype(BF16), dz.astype(BF16)]
    return parts, (sz * y).astype(BF16)


def _bwd_mix_a(dh1, h, proj, gain, conv_w, w_in, w_out, name, rider=None):
    s, d = dh1.shape
    e = MIX_WIDTH
    ts = min(TS_MIX, s)
    nt = s // ts
    hb = 16
    per = ts // hb

    def body(dh1_ref, h_ref, proj_ref, ch_ref, vh_ref, gain_ref, cw_ref, win_hbm, wout_hbm,
             dh_ref, dcw_ref, dgain_ref, dwin_hbm, dwout_hbm, win_v, wout_v, acc_in, acc_out, carry, sems):
        i = pl.program_id(0)

        @pl.when(i == 0)
        def _():
            _copy_all([(win_hbm, win_v), (wout_hbm, wout_v)], sems)
            carry[...] = jnp.zeros_like(carry)
            dcw_ref[...] = jnp.zeros_like(dcw_ref)
            dgain_ref[...] = jnp.zeros_like(dgain_ref)
            acc_in[...] = jnp.zeros_like(acc_in)
            acc_out[...] = jnp.zeros_like(acc_out)

        dh1 = dh1_ref[...]
        dh1b = dh1.astype(BF16)
        parts, mb = _mix_a_tile_grads(proj_ref, ch_ref, vh_ref, cw_ref, dh1b, wout_v, carry, dcw_ref, nt - 1 - i, hb)
        acc_out[...] += _dot_tn(mb, dh1b)
        xn, r = _rms(h_ref[...])
        hnb = (xn * gain_ref[...]).astype(BF16)
        for q in range(4):
            acc_in[q] += _dot_tn(hnb, parts[q])
        dhn = _dot_nt(parts[0], win_v[0]) + _dot_nt(parts[1], win_v[1]) + _dot_nt(parts[2], win_v[2]) + _dot_nt(parts[3], win_v[3])
        dh, dgain = _rms_bwd(dhn, xn, r, gain_ref[...])
        dh_ref[...] = dh1 + dh
        dgain_ref[...] += dgain

        @pl.when(i == nt - 1)
        def _():
            _copy_all([(acc_in, dwin_hbm), (acc_out, dwout_hbm)], sems)

    row = lambda width: pl.BlockSpec((ts, width), lambda i: (nt - 1 - i, 0))
    halo = lambda col: pl.BlockSpec((hb, e), lambda i: (jnp.maximum((nt - 1 - i) * per - 1, 0), col))
    return _call(
        body, name=name, grid=(nt,),
        in_specs=[row(d), row(d), row(4 * e), halo(1), halo(2), _full((1, d)), _full((8, e)), ANY, ANY],
        out_specs=[row(d), _full((8, e)), _full((1, d)), ANY, ANY],
        out_shape=[_sds((s, d), F32), _sds((8, e), F32), _sds((1, d), F32), _sds((4, d, e), F32), _sds((e, d), F32)],
        scratch_shapes=[pltpu.VMEM((4, d, e), BF16), pltpu.VMEM((e, d), BF16), pltpu.VMEM((4, d, e), F32),
                        pltpu.VMEM((e, d), F32), pltpu.VMEM((HALO, e), F32), pltpu.SemaphoreType.DMA((2,))],
        operands=[dh1, h, proj, proj, proj, gain, conv_w, w_in, w_out], rider=rider)


def _bwd_mix_a_weights(dh1, h, proj, gain, conv_w, w_out, name, rider=None):
    s, d = dh1.shape
    e = MIX_WIDTH
    ts = min(TS_MIX, s)
    nt = s // ts
    hb = 16
    per = ts // hb

    def body(dh1_ref, h_ref, proj_ref, ch_ref, vh_ref, gain_ref, cw_ref, wout_hbm,
             dproj_ref, dcw_ref, dwin_hbm, dwout_hbm, wout_v, acc_in, acc_out, carry, sems):
        i = pl.program_id(0)

        @pl.when(i == 0)
        def _():
            _copy_all([(wout_hbm, wout_v)], sems)
            carry[...] = jnp.zeros_like(carry)
            dcw_ref[...] = jnp.zeros_like(dcw_ref)
            acc_in[...] = jnp.zeros_like(acc_in)
            acc_out[...] = jnp.zeros_like(acc_out)

        dh1b = dh1_ref[...].astype(BF16)
        parts, mb = _mix_a_tile_grads(proj_ref, ch_ref, vh_ref, cw_ref, dh1b, wout_v, carry, dcw_ref, nt - 1 - i, hb)
        acc_out[...] += _dot_tn(mb, dh1b)
        xn, _ = _rms(h_ref[...])
        hnb = (xn * gain_ref[...]).astype(BF16)
        for q in range(4):
            acc_in[q] += _dot_tn(hnb, parts[q])
            dproj_ref[:, q * e:(q + 1) * e] = parts[q]

        @pl.when(i == nt - 1)
        def _():
            _copy_all([(acc_in, dwin_hbm), (acc_out, dwout_hbm)], sems)

    row = lambda width: pl.BlockSpec((ts, width), lambda i: (nt - 1 - i, 0))
    halo = lambda col: pl.BlockSpec((hb, e), lambda i: (jnp.maximum((nt - 1 - i) * per - 1, 0), col))
    return _call(
        body, name=name, grid=(nt,),
        in_specs=[row(d), row(d), row(4 * e), halo(1), halo(2), _full((1, d)), _full((8, e)), ANY],
        out_specs=[row(4 * e), _full((8, e)), ANY, ANY],
        out_shape=[_sds((s, 4 * e), BF16), _sds((8, e), F32), _sds((4, d, e), F32), _sds((e, d), F32)],
        scratch_shapes=[pltpu.VMEM((e, d), BF16), pltpu.VMEM((4, d, e), F32), pltpu.VMEM((e, d), F32),
                        pltpu.VMEM((HALO, e), F32), pltpu.SemaphoreType.DMA((2,))],
        operands=[dh1, h, proj, proj, proj, gain, conv_w, w_out], rider=rider)


def _bwd_mix_a_input(dproj, h, dh1, gain, w_in, name, rider=None):
    s, d = dh1.shape
    e = MIX_WIDTH
    ts = min(TS_PLE, s)
    nt = s // ts

    def body(dproj_ref, h_ref, dh1_ref, gain_ref, win_hbm, dh_ref, dgain_ref, win_v, sems):
        @pl.when(pl.program_id(0) == 0)
        def _():
            _copy_all([(win_hbm, win_v)], sems)
            dgain_ref[...] = jnp.zeros_like(dgain_ref)

        dhn = _dot_nt(dproj_ref[:, 0:e], win_v[0])
        for q in range(1, 4):
            dhn = dhn + _dot_nt(dproj_ref[:, q * e:(q + 1) * e], win_v[q])
        xn, r = _rms(h_ref[...])
        dh, dgain = _rms_bwd(dhn, xn, r, gain_ref[...])
        dh_ref[...] = dh1_ref[...] + dh
        dgain_ref[...] += dgain

    row = lambda width: pl.BlockSpec((ts, width), lambda i: (i, 0))
    return _call(
        body, name=name, grid=(nt,),
        in_specs=[row(4 * e), row(d), row(d), _full((1, d)), ANY],
        out_specs=[row(d), _full((1, d))],
        out_shape=[_sds((s, d), F32), _sds((1, d), F32)],
        scratch_shapes=[pltpu.VMEM((4, d, e), BF16), pltpu.SemaphoreType.DMA((1,))],
        operands=[dproj, h, dh1, gain, w_in], rider=rider)


def _bwd_mix_b(dh1, h, z, mx, diff, gain, scale, w_in, w_grp, w_out, name, rider=None):
    s, d = dh1.shape
    e = MIX_WIDTH
    ts = min(TS_MIX, s)
    nt = s // ts
    half = e // 2

    def body(dh1_ref, h_ref, z_ref, mx_ref, dd_ref, gain_ref, scale_ref, win_hbm, wgrp_hbm, wout_hbm,
             dh_ref, dscale_ref, dgain_ref, dwin_hbm, dwgrp_hbm, dwout_hbm,
             win_v, wgrp_v, wout_v, acc_in, acc_grp, acc_out, carry, sems):
        i = pl.program_id(0)
        tile = nt - 1 - i

        @pl.when(i == 0)
        def _():
            _copy_all([(win_hbm, win_v), (wout_hbm, wout_v)] + _grp_pairs(wgrp_hbm, wgrp_v), sems)
            carry[...] = jnp.zeros_like(carry)
            dscale_ref[...] = jnp.zeros_like(dscale_ref)
            dgain_ref[...] = jnp.zeros_like(dgain_ref)
            acc_in[...] = jnp.zeros_like(acc_in)
            acc_grp[...] = jnp.zeros_like(acc_grp)
            acc_out[...] = jnp.zeros_like(acc_out)

        zf = z_ref[...].astype(F32)
        mxf = mx_ref[...].astype(F32)
        sig = _sigmoid(zf)
        sz = zf * sig
        mixed = mxf * scale_ref[...]
        dh1 = dh1_ref[...]
        dh1b = dh1.astype(BF16)
        acc_out[...] += _dot_tn((sz * mixed).astype(BF16), dh1b)
        dm = _dot_nt(dh1b, wout_v[...])
        dz = (dm * mixed) * (sig * (1.0 + zf * (1.0 - sig)))
        dmixed = dm * sz
        dscale_ref[...] += jnp.sum(dmixed * mxf, axis=0, keepdims=True)
        dmxb = (dmixed * scale_ref[...]).astype(BF16)
        diff = dd_ref[...]
        for g in range(N_GROUPS):
            cols = slice(g * GROUP_DIM, (g + 1) * GROUP_DIM)
            acc_grp[g] += _dot_tn(diff[:, cols], dmxb[:, cols])
        ddiff = jnp.concatenate(
            [_dot_nt(dmxb[:, g * GROUP_DIM:(g + 1) * GROUP_DIM], wgrp_v[g]) for g in range(N_GROUPS)], axis=1)
        dub = (_pool_bwd(ddiff, carry, tile, ts) - ddiff).astype(BF16)
        dzb = dz.astype(BF16)
        parts = [dub[:, 0:half], dub[:, half:e], dzb[:, 0:half], dzb[:, half:e]]
        xn, r = _rms(h_ref[...])
        hnb = (xn * gain_ref[...]).astype(BF16)
        for k in range(4):
            acc_in[k] += _dot_tn(hnb, parts[k])
        dhn = _dot_nt(parts[0], win_v[0]) + _dot_nt(parts[1], win_v[1]) + _dot_nt(parts[2], win_v[2]) + _dot_nt(parts[3], win_v[3])
        dh, dgain = _rms_bwd(dhn, xn, r, gain_ref[...])
        dh_ref[...] = dh1 + dh
        dgain_ref[...] += dgain

        @pl.when(i == nt - 1)
        def _():
            _copy_all([(acc_in, dwin_hbm), (acc_out, dwout_hbm)] + [(v, hb_) for hb_, v in _grp_pairs(dwgrp_hbm, acc_grp)], sems)

    row = lambda width: pl.BlockSpec((ts, width), lambda i: (nt - 1 - i, 0))
    return _call(
        body, name=name, grid=(nt,),
        in_specs=[row(d), row(d), row(e), row(e), row(e), _full((1, d)), _full((1, e)), ANY, ANY, ANY],
        out_specs=[row(d), _full((1, e)), _full((1, d)), ANY, ANY, ANY],
        out_shape=[_sds((s, d), F32), _sds((1, e), F32), _sds((1, d), F32), _sds((4, d, half), F32),
                   _sds((4, N_GROUPS, GROUP_DIM // 4, GROUP_DIM), F32), _sds((e, d), F32)],
        scratch_shapes=[pltpu.VMEM((4, d, half), BF16), pltpu.VMEM((N_GROUPS, GROUP_DIM, GROUP_DIM), BF16),
                        pltpu.VMEM((e, d), BF16), pltpu.VMEM((4, d, half), F32),
                        pltpu.VMEM((N_GROUPS, GROUP_DIM, GROUP_DIM), F32), pltpu.VMEM((e, d), F32),
                        pltpu.VMEM((4, HALO, e), F32), pltpu.SemaphoreType.DMA((18,))],
        operands=[dh1, h, z, mx, diff, gain, scale, w_in, w_grp, w_out], rider=rider)


def _first_gather(rider, small):
    shards = rider.inputs
    ni = len(shards)

    def body(*refs):
        rin, small_src = refs[:ni], refs[ni]
        rout, small_dst = refs[ni + 1:2 * ni + 1], refs[2 * ni + 1]
        send, recv, ssend, srecv = refs[2 * ni + 2:]
        x, y, c, chips = _place()
        me = 2 * x + y
        peers = [(cx, cy, c) for cx, cy in chips] + [(x, y, 1 - c)]
        vec = [_remote(small_src, small_dst.at[me], ssend, srecv, j, to) for j, to in enumerate(peers)]
        for cp in vec:
            cp.start()
        rider.start(rin, rout, send, recv)
        rider.middle(rin, rout, send, recv)
        rider.finish(rin, rout, send, recv)
        for j, (px, py, _) in enumerate(peers):
            _remote(small_src, small_dst.at[2 * px + py], ssend, srecv, j, peers[j]).wait_recv()
        for cp in vec:
            cp.wait_send()

    outs = pl.pallas_call(
        body, name="first_gather", in_specs=[ANY] * (ni + 1), out_specs=[ANY] * (ni + 1),
        out_shape=rider.out_shapes + [_sds((4,) + small.shape, small.dtype)],
        scratch_shapes=[pltpu.SemaphoreType.DMA((rider.n_sems,)), pltpu.SemaphoreType.DMA((rider.n_sems,)),
                        pltpu.SemaphoreType.DMA((4,)), pltpu.SemaphoreType.DMA((4,))],
    )(*shards, small)
    return list(outs[:ni]), outs[ni]


def _vector_rider(pack):
    flips = [(fx, fy, fc) for fx in (0, 1) for fy in (0, 1) for fc in (0, 1)][1:]

    def copies(rin, rout, send, recv, base):
        x, y, c, _ = _place()
        me = 4 * x + 2 * y + c
        peers = [(1 - x if fx else x, 1 - y if fy else y, 1 - c if fc else c) for fx, fy, fc in flips]
        own = pltpu.make_async_copy(rin[0], rout[0].at[me], send.at[base + 7])
        out = [_remote(rin[0], rout[0].at[me], send, recv, base + r, peer) for r, peer in enumerate(peers)]
        back = [_remote(rin[0], rout[0].at[4 * px + 2 * py + pc], send, recv, base + r, (px, py, pc))
                for r, (px, py, pc) in enumerate(peers)]
        return own, out, back

    def start(rin, rout, send, recv, base=0):
        own, out, _ = copies(rin, rout, send, recv, base)
        own.start()
        for cp in out:
            cp.start()

    def finish(rin, rout, send, recv, base=0):
        own, out, back = copies(rin, rout, send, recv, base)
        for cp in back:
            cp.wait_recv()
        for cp in out:
            cp.wait_send()
        own.wait()

    return _Rider([pack], [_sds((8,) + pack.shape, pack.dtype)], 8, start, finish)


def _vector_sum(landed, row_counts):
    _, rows, d = landed.shape

    def body(l_ref, *out_refs):
        first = 0
        for n, out_ref in zip(row_counts, out_refs):
            total = l_ref[0, first:first + n, :]
            for dev in range(1, 8):
                total = total + l_ref[dev, first:first + n, :]
            out_ref[...] = total
            first += n

    vmem = pl.BlockSpec(memory_space=pltpu.VMEM)
    return pl.pallas_call(body, name="vector_sum", in_specs=[vmem], out_specs=[vmem] * len(row_counts),
                          out_shape=[_sds((n, d), F32) for n in row_counts])(landed)


def _job_rows(rows, cols):
    return min(rows, max(8, JOB_BLOCK_BYTES // (4 * cols)))


def _pair_sum_job(grad, sibling_rows):
    _, _, rh, cols = grad.shape
    tr = _job_rows(rh, cols)
    nr = rh // tr

    def chip_of(j, pos):
        return jnp.bitwise_xor(pos[0], jnp.where(j == 2, 3, 2 - j))

    return dict(
        ins=[(grad, (None, None, tr, cols), lambda l, pos: (chip_of(l // nr, pos), pos[1], l % nr, 0)),
             (sibling_rows, (None, tr, cols), lambda l, pos: (chip_of(l // nr, pos), l % nr, 0))],
        outs=[((3, rh, cols), BF16, (None, tr, cols), lambda l, pos: (l // nr, l % nr, 0))],
        steps=3 * nr, fn=lambda g, sb: [(g + sb).astype(BF16)], alias=None)


def _final_sum_job(grad, sibling_rows, landed, stack, slot, n_slots):
    _, _, rh, cols = grad.shape
    tr = _job_rows(rh, cols)

    def fn(g, sb, ld):
        total = g + sb
        for j in range(3):
            total = total + ld[j].astype(F32)
        return [total]

    return dict(
        ins=[(grad, (None, None, tr, cols), lambda l, pos: (pos[0], pos[1], l, 0)),
             (sibling_rows, (None, tr, cols), lambda l, pos: (pos[0], l, 0)),
             (landed, (3, tr, cols), lambda l, pos: (0, l, 0))],
        outs=[((n_slots, 2, rh, cols), F32, (None, None, tr, cols), lambda l, pos: (slot, pos[1], l, 0))],
        steps=rh // tr, fn=fn, alias=stack)


def _adamw_job(g, w, m, v, block_bytes):
    rows, cols = g.shape
    tr = min(rows, max(8, block_bytes // (4 * cols)))

    def fn(gg, ww, mm, vv):
        nm = ADAM_B1 * mm + (1.0 - ADAM_B1) * gg
        nv = ADAM_B2 * vv + (1.0 - ADAM_B2) * (gg * gg)
        m_hat = nm / (1.0 - ADAM_B1 ** ADAM_STEP)
        v_hat = nv / (1.0 - ADAM_B2 ** ADAM_STEP)
        return [-ADAM_LR * (m_hat / (jnp.sqrt(v_hat) + ADAM_EPS) + ADAM_WD * ww), nm, nv, gg]

    block = lambda l, pos: (l, 0)
    return dict(ins=[(a, (tr, cols), block) for a in (g, w, m, v)],
                outs=[((rows, cols), F32, (tr, cols), block)] * 4, steps=rows // tr, fn=fn, alias=None)


def _run_jobs(jobs, place, name):
    starts, total = [], 0
    for jb in jobs:
        starts.append(total)
        total += jb["steps"]

    def clamped(fn, start, steps):
        return lambda s, pos: fn(jnp.clip(s - start, 0, steps - 1), pos)

    in_specs, operands = [], [place]
    for jb, start in zip(jobs, starts):
        for arr, block, fn in jb["ins"]:
            in_specs.append(pl.BlockSpec(block, clamped(fn, start, jb["steps"])))
            operands.append(arr)
    n_ins = len(in_specs)
    first_out, n_outs = [], 0
    for jb in jobs:
        first_out.append(n_outs)
        n_outs += len(jb["outs"])
    aliases = {}
    for t, jb in enumerate(jobs):
        if jb["alias"] is not None:
            in_specs.append(ANY)
            operands.append(jb["alias"])
            aliases[len(operands) - 1] = first_out[t]
    out_specs = [pl.BlockSpec(block, clamped(fn, start, jb["steps"]))
                 for jb, start in zip(jobs, starts) for _, _, block, fn in jb["outs"]]

    def body(place_ref, *refs):
        in_refs, out_refs = refs[:n_ins], refs[len(in_specs):]
        s = pl.program_id(0)
        first = 0
        for t, (jb, start) in enumerate(zip(jobs, starts)):
            mine = in_refs[first:first + len(jb["ins"])]
            first += len(jb["ins"])

            @pl.when((s >= start) & (s < start + jb["steps"]))
            def _(mine=mine, t=t, jb=jb):
                values = jb["fn"](*[r[...] for r in mine])
                for n, value in enumerate(values):
                    out_refs[first_out[t] + n][...] = value

    grid_spec = pltpu.PrefetchScalarGridSpec(num_scalar_prefetch=1, grid=(total,), in_specs=in_specs, out_specs=out_specs)
    outs = pl.pallas_call(body, name=name, grid_spec=grid_spec,
                          out_shape=[_sds(shape, dtype) for jb in jobs for shape, dtype, _, _ in jb["outs"]],
                          input_output_aliases=aliases, compiler_params=_params(1))(*operands)
    return [list(outs[first_out[t]:first_out[t] + len(jb["outs"])]) for t, jb in enumerate(jobs)]


BIG = ["a_w_in", "a_w_out", "b_w_in", "b_w_grp", "b_w_out", "ple_w_gate", "ple_w_proj"]

GATHER_PLAN = {
    "first": [("a_w_in", 0), ("a_w_out", 0)],
    "mix0": [("ple_w_gate", 0), ("ple_w_proj", 0), ("b_w_in", 0), ("b_w_grp", 0), ("b_w_out", 0)],
    "ple0": [("ple_w_gate", 1), ("ple_w_proj", 1), ("ple_w_proj", 2)],
    "mix1": [("a_w_in", 1), ("a_w_out", 1), ("ple_w_gate", 2)],
    "mix2": [("b_w_in", 1), ("b_w_grp", 1), ("b_w_out", 1), ("ple_w_gate", 3), ("ple_w_proj", 3)],
}
GATHER_LONGER_THAN_HOST = ("mix1",)


def _as_2d(name, a):
    if name == "b_w_grp":
        return a.reshape(a.shape[0], N_GROUPS * (GROUP_DIM // 4), GROUP_DIM)
    return a


def kernel(x, p, norm_mix, a_w_in, a_w_conv, a_w_out, b_w_in, b_w_grp, b_scale, b_w_out, ple_norm, ple_w_gate, ple_w_proj, final_norm, loss_target, m_norm_mix, m_a_w_in, m_a_w_conv, m_a_w_out, m_b_w_in, m_b_w_grp, m_b_scale, m_b_w_out, m_ple_norm, m_ple_w_gate, m_ple_w_proj, m_final_norm, v_norm_mix, v_a_w_in, v_a_w_conv, v_a_w_out, v_b_w_in, v_b_w_grp, v_b_scale, v_b_w_out, v_ple_norm, v_ple_w_gate, v_ple_w_proj, v_final_norm):
    d, e = D_MODEL, MIX_WIDTH
    s = x.shape[1]
    cx, cy, cc = lax.axis_index("x"), lax.axis_index("y"), lax.axis_index("c")
    chip = 2 * cx + cy
    place = jnp.stack([chip, cc]).astype(jnp.int32)

    weights = dict(a_w_in=a_w_in, a_w_out=a_w_out, b_w_in=b_w_in, b_w_grp=b_w_grp, b_w_out=b_w_out,
                   ple_w_gate=ple_w_gate, ple_w_proj=ple_w_proj)
    moms = dict(a_w_in=m_a_w_in, a_w_out=m_a_w_out, b_w_in=m_b_w_in, b_w_grp=m_b_w_grp, b_w_out=m_b_w_out,
                ple_w_gate=m_ple_w_gate, ple_w_proj=m_ple_w_proj)
    vars_ = dict(a_w_in=v_a_w_in, a_w_out=v_a_w_out, b_w_in=v_b_w_in, b_w_grp=v_b_w_grp, b_w_out=v_b_w_out,
                 ple_w_gate=v_ple_w_gate, ple_w_proj=v_ple_w_proj)
    w2d = {nm: _as_2d(nm, weights[nm]) for nm in BIG}
    bf = {nm: w2d[nm].astype(BF16).reshape(w2d[nm].shape[0], 2, w2d[nm].shape[1] // 2, w2d[nm].shape[2]) for nm in BIG}
    gathered = {}

    def gather_rider(host):
        keys = GATHER_PLAN.get(host)
        if not keys:
            return None
        rider = _gather_rider([bf[nm] for nm, _ in keys], [j for _, j in keys])
        rider.middle_at_end = host in GATHER_LONGER_THAN_HOST
        return rider

    def keep(host, landed):
        for k, a in zip(GATHER_PLAN.get(host, []), landed):
            gathered[k] = a

    def weight(nm, j):
        a = gathered[(nm, j)]
        shapes = {"a_w_in": (4, d, e), "a_w_out": (e, d), "b_w_in": (4, d, e // 2),
                  "b_w_grp": (4, N_GROUPS, GROUP_DIM // 4, GROUP_DIM), "b_w_out": (e, d), "ple_w_gate": (d, d),
                  "ple_w_proj": (4, PLE_DIM, d // 4)}
        return a.reshape(shapes[nm])

    pad = jnp.zeros((4, e // 4), F32)
    small = jnp.concatenate([a_w_conv[0], b_scale[0:1], pad, a_w_conv[1], b_scale[1:2], pad], axis=0)
    landed, small_full = _first_gather(gather_rider("first"), small)
    keep("first", landed)
    small_full = small_full.transpose(1, 0, 2).reshape(16, e)
    conv_w = [_Block(small_full.reshape(2, 8, e), j) for j in range(2)]
    scale_w = [_Block(small_full.reshape(16, 1, e), 8 * j + 3) for j in range(2)]

    p3 = p.reshape(DEPTH, s, PLE_DIM)
    mix_gain = [_Block(norm_mix.reshape(DEPTH, 1, d), i) for i in range(DEPTH)]
    ple_gain = [_Block(ple_norm.reshape(DEPTH, 1, d), i) for i in range(DEPTH)]

    h = x.reshape(s, d)
    saved = []
    for i in range(DEPTH):
        j = i // 2
        rider = gather_rider(f"mix{i}")
        ple = (p3, i, ple_gain[i], weight("ple_w_gate", i), weight("ple_w_proj", i)) if i > 0 else None
        if i % 2 == 0:
            outs, landed = _fwd_mix_a(h, mix_gain[i], conv_w[j], weight("a_w_in", j), weight("a_w_out", j),
                                      f"fwd_mix_a{j}", rider, ple)
            mix = dict(proj=outs[1])
        else:
            outs, landed = _fwd_mix_b(h, mix_gain[i], scale_w[j], weight("b_w_in", j), weight("b_w_grp", j),
                                      weight("b_w_out", j), f"fwd_mix_b{j}", rider, ple)
            mix = dict(z=outs[1], mx=outs[2], diff=outs[3])
        keep(f"mix{i}", landed)
        h1 = outs[0]
        if ple:
            h2, gate = outs[-2:]
        else:
            (h2, gate), landed = _fwd_ple(h1, p3, ple_gain[i], weight("ple_w_gate", i), weight("ple_w_proj", i), i,
                                          gather_rider(f"ple{i}"))
            keep(f"ple{i}", landed)
        saved.append(dict(h=h, h1=h1, gate=gate, **mix))
        h = h2

    n_slots = {nm: weights[nm].shape[0] for nm in BIG}
    stacks = {nm: None for nm in BIG}

    class Group:
        def __init__(self, keys, grads):
            self.keys, self.stage = keys, 0
            self.g32 = [g.reshape(4, 2, w2d[nm].shape[1] // 2, w2d[nm].shape[2]) for (nm, _), g in zip(keys, grads)]

        def rider(self):
            if self.stage == 0:
                return _pair_rider(self.g32)
            if self.stage == 1:
                return _ici_rider(self.pair_sums)
            return _final_rider([stacks[nm] for nm, _ in self.keys], [j for _, j in self.keys])

        def jobs_after(self, landed):
            if self.stage == 0:
                self.from_sibling = landed
                return [_pair_sum_job(g, sb) for g, sb in zip(self.g32, landed)]
            if self.stage == 1:
                return [_final_sum_job(g, sb, ld, stacks[nm], j, n_slots[nm])
                        for (nm, j), g, sb, ld in zip(self.keys, self.g32, self.from_sibling, landed)]
            return []

        def advance(self, landed, summed):
            if self.stage == 0:
                self.pair_sums = summed
            else:
                for (nm, _), a in zip(self.keys, summed if self.stage == 1 else landed):
                    stacks[nm] = a
            self.stage += 1

    active = []
    batches = [0]

    def riders_now():
        parts = [g.rider() for g in active]
        return parts, _merge(parts)

    def advance_all(parts, landed):
        groups = list(active)
        pieces = _split(landed, parts)
        jobs = [g.jobs_after(l) for g, l in zip(groups, pieces)]
        flat = sum(jobs, [])
        outs = [o[0] for o in _run_jobs(flat, place, f"reduce_sums{batches[0]}")] if flat else []
        batches[0] += 1
        for g, l, jb in zip(groups, pieces, jobs):
            g.advance(l, outs[:len(jb)])
            outs = outs[len(jb):]
            if g.stage == 3:
                active.remove(g)

    d_mix_gain, d_ple_gain = [None] * DEPTH, [None] * DEPTH
    d_conv, d_scale = [None] * 2, [None] * 2
    for i in reversed(range(DEPTH)):
        j = i // 2
        sv = saved[i]
        parts, rider = riders_now()
        if i == DEPTH - 1:
            (dh1, d_ple_gain[i], dwg, dwp, loss_part, d_final), landed = _bwd_ple(
                h, sv["h1"], sv["gate"], p3, ple_gain[i], weight("ple_w_gate", i), weight("ple_w_proj", i), i, rider,
                loss_head=(loss_target.reshape(s, d), final_norm.reshape(1, d)))
        else:
            (dh1, d_ple_gain[i], dwg, dwp), landed = _bwd_ple(
                dh, sv["h1"], sv["gate"], p3, ple_gain[i], weight("ple_w_gate", i), weight("ple_w_proj", i), i, rider)
        advance_all(parts, landed)
        active.append(Group([("ple_w_gate", i), ("ple_w_proj", i)], [dwg, dwp]))
        parts, rider = riders_now()
        if i == 0:
            (dproj, d_conv[0], dwin, dwout), landed = _bwd_mix_a_weights(
                dh1, sv["h"], sv["proj"], mix_gain[0], conv_w[0], weight("a_w_out", 0), "bwd_mix_a0_weights", rider)
            advance_all(parts, landed)
            active.append(Group([("a_w_in", 0), ("a_w_out", 0)], [dwin, dwout]))
            parts, rider = riders_now()
            advance_all(parts, _run_rider(rider, "pair_exchange0"))
            parts, rider = riders_now()
            (dh, d_mix_gain[0]), landed = _bwd_mix_a_input(dproj, sv["h"], dh1, mix_gain[0], weight("a_w_in", 0),
                                                          "bwd_mix_a0_input", rider)
            advance_all(parts, landed)
            continue
        if i % 2 == 0:
            (dh, d_conv[j], d_mix_gain[i], dwin, dwout), landed = _bwd_mix_a(
                dh1, sv["h"], sv["proj"], mix_gain[i], conv_w[j], weight("a_w_in", j), weight("a_w_out", j),
                f"bwd_mix_a{j}", rider)
            new = Group([("a_w_in", j), ("a_w_out", j)], [dwin, dwout])
        else:
            (dh, d_scale[j], d_mix_gain[i], dwin, dwgrp, dwout), landed = _bwd_mix_b(
                dh1, sv["h"], sv["z"], sv["mx"], sv["diff"], mix_gain[i], scale_w[j], weight("b_w_in", j),
                weight("b_w_grp", j), weight("b_w_out", j), f"bwd_mix_b{j}", rider)
            new = Group([("b_w_in", j), ("b_w_grp", j), ("b_w_out", j)], [dwin, dwgrp, dwout])
        advance_all(parts, landed)
        active.append(new)
    grad_x = dh.reshape(1, s, d)

    pieces = (d_mix_gain + d_ple_gain + [d_final, d_conv[0][0:3], d_conv[1][0:3]] + d_scale
              + [jnp.tile(loss_part[0:1], (1, d // 128))])
    used = sum(a.shape[0] for a in pieces)
    pack = jnp.concatenate(pieces + [jnp.zeros((-used % PACK_GROUP, d), F32)], axis=0)
    vectors = _vector_rider(pack)
    tail = 0
    while active:
        parts, _ = riders_now()
        extra = [vectors] if tail == 0 else []
        landed = _run_rider(_merge(parts + extra), f"tail_exchange{tail}")
        if extra:
            g_mix, g_ple, g_final, g_conv, g_scale, loss_row = _vector_sum(
                _split(landed, parts + extra)[-1][0], [DEPTH, DEPTH, 1, 6, 2, 1])
        advance_all(parts, landed)
        tail += 1
    loss = loss_row[0, 0]

    mine = lambda a: lax.dynamic_slice_in_dim(a, chip * (e // 4), e // 4, axis=1)
    row = lambda a: a.reshape(1, d)
    taps = lambda a: a.reshape(6, e // 4)
    flat = {nm: (w2d[nm].shape[0] * w2d[nm].shape[1], w2d[nm].shape[2]) for nm in BIG}
    tensors = {nm: (stacks[nm].reshape(flat[nm]), w2d[nm].reshape(flat[nm]), _as_2d(nm, moms[nm]).reshape(flat[nm]),
                    _as_2d(nm, vars_[nm]).reshape(flat[nm])) for nm in BIG}
    tensors.update(
        norm_mix=(g_mix, norm_mix, m_norm_mix, v_norm_mix), ple_norm=(g_ple, ple_norm, m_ple_norm, v_ple_norm),
        final_norm=(g_final, row(final_norm), row(m_final_norm), row(v_final_norm)),
        a_w_conv=(mine(g_conv), taps(a_w_conv), taps(m_a_w_conv), taps(v_a_w_conv)),
        b_scale=(mine(g_scale), b_scale, m_b_scale, v_b_scale))
    order = ["norm_mix", "a_w_in", "a_w_conv", "a_w_out", "b_w_in", "b_w_grp", "b_scale", "b_w_out", "ple_norm",
             "ple_w_gate", "ple_w_proj", "final_norm"]
    shapes = dict(norm_mix=norm_mix.shape, ple_norm=ple_norm.shape, final_norm=final_norm.shape,
                  a_w_conv=a_w_conv.shape, b_scale=b_scale.shape, **{nm: weights[nm].shape for nm in BIG})
    updates = {nm: _run_jobs([_adamw_job(*tensors[nm], ADAMW_BIG_BLOCK_BYTES)], place, f"adamw_{nm}")[0]
               for nm in ADAMW_ALONE}
    rest = [nm for nm in order if nm not in ADAMW_ALONE]
    updates.update(zip(rest, _run_jobs([_adamw_job(*tensors[nm], ADAMW_BLOCK_BYTES) for nm in rest], place, "adamw_rest")))
    outs = [loss, grad_x]
    for which in (3, 0, 1, 2):
        outs += [updates[nm][which].reshape(shapes[nm]) for nm in order]
    return tuple(outs)
```

```python
import jax
import jax.numpy as jnp
from jax import lax
from jax.experimental import pallas as pl
from jax.experimental.pallas import tpu as pltpu

F32 = jnp.float32
BF16 = jnp.bfloat16
MESH = pl.DeviceIdType.MESH

D_MODEL = 1024
MIX_WIDTH = 1024
PLE_DIM = 256
N_GROUPS = 4
GROUP_DIM = 256
POOL_WINDOWS = (2, 4, 8, 16)
DEPTH = 4
EPS = 1e-6

ADAM_LR = 0.001
ADAM_B1 = 0.9
ADAM_B2 = 0.999
ADAM_EPS = 1e-08
ADAM_WD = 0.01
ADAM_STEP = 10

HALO = 8
TS_MIX = 256
TS_FWD = 512
TS_PLE = 512
VMEM_LIMIT = 56 * 1024 * 1024
PACK_GROUP = 8
JOB_BLOCK_BYTES = 2 * 1024 * 1024
ADAMW_BLOCK_BYTES = 512 * 1024
ADAMW_BIG_BLOCK_BYTES = 2 * 1024 * 1024
ADAMW_ALONE = ("a_w_in", "b_w_in", "ple_w_gate")
MIDDLE_STEPS_BEFORE_END = 1

ANY = pl.BlockSpec(memory_space=pl.ANY)


def _sds(shape, dtype):
    return jax.ShapeDtypeStruct(shape, dtype)


def _full(shape):
    nd = len(shape)
    return pl.BlockSpec(shape, lambda *_: (0,) * nd)


def _params(n_axes=1):
    return pltpu.CompilerParams(dimension_semantics=("arbitrary",) * n_axes, vmem_limit_bytes=VMEM_LIMIT)


def _dot(a, b):
    return jnp.dot(a, b, preferred_element_type=F32)


def _dot_nt(a, b):
    return lax.dot_general(a, b, (((1,), (1,)), ((), ())), preferred_element_type=F32)


def _dot_tn(a, b):
    return lax.dot_general(a, b, (((0,), (0,)), ((), ())), preferred_element_type=F32)


def _sigmoid(z):
    return 1.0 / (1.0 + jnp.exp(-z))


def _shift_down(x, k, tail):
    rolled = pltpu.roll(x, k, 0)
    rt = tail if k % HALO == 0 else pltpu.roll(tail, k % HALO, 0)
    row = lax.broadcasted_iota(jnp.int32, rt.shape, 0)
    head = jnp.where(row < k, rt, rolled[0:HALO])
    return jnp.concatenate([head, rolled[HALO:]], axis=0)


def _shift_up(x, k, head_next):
    n = x.shape[0]
    rolled = pltpu.roll(x, n - k, 0)
    rh = head_next if k % HALO == 0 else pltpu.roll(head_next, HALO - k % HALO, 0)
    row = lax.broadcasted_iota(jnp.int32, rh.shape, 0)
    tail = jnp.where(row >= HALO - k, rh, rolled[n - HALO:n])
    return jnp.concatenate([rolled[:n - HALO], tail], axis=0)


def _inv_counts(tile, ts):
    t = tile * ts + lax.broadcasted_iota(jnp.int32, (ts, 1), 0)
    return [1.0 / jnp.minimum(t + 1, w).astype(F32) for w in POOL_WINDOWS]


def _pool_fwd(u, carry, tile, ts):
    inv = _inv_counts(tile, ts)
    outs = []
    for g, w in enumerate(POOL_WINDOWS):
        cols = slice(g * GROUP_DIM, (g + 1) * GROUP_DIM)
        s = u[:, cols]
        level, k = 0, 1
        while k < w:
            tail = carry[level, :, cols]
            carry[level, :, cols] = s[ts - HALO:ts]
            s = s + _shift_down(s, k, tail)
            level, k = level + 1, k * 2
        outs.append(s * inv[g])
    return jnp.concatenate(outs, axis=1)


def _pool_bwd(dd, carry, tile, ts):
    inv = _inv_counts(tile, ts)
    outs = []
    for g, w in enumerate(POOL_WINDOWS):
        cols = slice(g * GROUP_DIM, (g + 1) * GROUP_DIM)
        q = dd[:, cols] * inv[g]
        level, k = 0, 1
        while k < w:
            head = carry[level, :, cols]
            carry[level, :, cols] = q[0:HALO]
            q = q + _shift_up(q, k, head)
            level, k = level + 1, k * 2
        outs.append(q)
    return jnp.concatenate(outs, axis=1)


def _copy_all(pairs, sems):
    copies = [pltpu.make_async_copy(src, dst, sems.at[n]) for n, (src, dst) in enumerate(pairs)]
    for cp in copies:
        cp.start()
    for cp in copies:
        cp.wait()


def _grp_pairs(wgrp_hbm, wgrp_v):
    rows = GROUP_DIM // 4
    return [(wgrp_hbm.at[k, g], wgrp_v.at[g, pl.ds(k * rows, rows), :]) for k in range(4) for g in range(N_GROUPS)]


def _rms(h):
    r = lax.rsqrt(jnp.mean(h * h, axis=-1, keepdims=True) + EPS)
    return h * r, r


def _rms_bwd(dhn, xn, r, gain):
    dgain = jnp.sum(dhn * xn, axis=0, keepdims=True)
    dxn = dhn * gain
    dh = r * (dxn - xn * jnp.mean(dxn * xn, axis=-1, keepdims=True))
    return dh, dgain


class _Rider:
    def __init__(self, inputs, out_shapes, n_sems, start, finish, middle=None, aliases=None):
        self.inputs, self.out_shapes, self.n_sems = list(inputs), list(out_shapes), n_sems
        self.start, self.middle, self.finish = start, middle, finish
        self.aliases = dict(aliases or {})
        self.middle_at_end = False


def _merge(riders):
    riders = [r for r in riders if r is not None]
    if not riders:
        return None
    if len(riders) == 1:
        return riders[0]

    def phase(which):
        def run(rin, rout, send, recv, base=0):
            i0 = o0 = s0 = 0
            for r in riders:
                fn = getattr(r, which)
                if fn is not None:
                    fn(rin[i0:i0 + len(r.inputs)], rout[o0:o0 + len(r.out_shapes)], send, recv, base + s0)
                i0, o0, s0 = i0 + len(r.inputs), o0 + len(r.out_shapes), s0 + r.n_sems
        return run

    aliases, i0, o0 = {}, 0, 0
    for r in riders:
        aliases.update({i0 + a: o0 + b for a, b in r.aliases.items()})
        i0, o0 = i0 + len(r.inputs), o0 + len(r.out_shapes)
    return _Rider(sum([r.inputs for r in riders], []), sum([r.out_shapes for r in riders], []),
                  sum(r.n_sems for r in riders), phase("start"), phase("finish"),
                  phase("middle") if any(r.middle for r in riders) else None, aliases)


def _split(landed, riders):
    out, o0 = [], 0
    for r in riders:
        if r is None:
            out.append(None)
        else:
            out.append(landed[o0:o0 + len(r.out_shapes)])
            o0 += len(r.out_shapes)
    return out


def _place():
    x, y, c = lax.axis_index("x"), lax.axis_index("y"), lax.axis_index("c")
    chips = [(1 - x, y), (x, 1 - y), (1 - x, 1 - y)]
    return x, y, c, chips


def _remote(src, dst, send_sems, recv_sems, sem, to):
    return pltpu.make_async_remote_copy(src_ref=src, dst_ref=dst, send_sem=send_sems.at[sem], recv_sem=recv_sems.at[sem],
                                        device_id=to, device_id_type=MESH)


def _gather_rider(stacked, slots):
    ni = len(stacked)

    def first_hops(rin, rout, send, recv, base, x, y, c, chips):
        me = 2 * x + y
        return [_remote(rin[t].at[slots[t], c], rout[t].at[me, c], send, recv, base + 7 * t + j, (cx, cy, c))
                for j, (cx, cy) in enumerate(chips) for t in range(ni)]

    def passes(rout, send, recv, base, x, y, c, chips):
        out = []
        for j, (cx, cy) in enumerate(chips):
            for t in range(ni):
                landed = rout[t].at[2 * cx + cy, c]
                out.append((_remote(landed, landed, send, recv, base + 7 * t + j, (x, y, 1 - c)),
                            _remote(landed, landed, send, recv, base + 7 * t + 3 + j, (x, y, 1 - c))))
        return out

    def own(rin, rout, send, recv, base, x, y, c):
        return [_remote(rin[t].at[slots[t]], rout[t].at[2 * x + y], send, recv, base + 7 * t + 6, (x, y, 1 - c))
                for t in range(ni)]

    def start(rin, rout, send, recv, base=0):
        x, y, c, chips = _place()
        for cp in first_hops(rin, rout, send, recv, base, x, y, c, chips) + own(rin, rout, send, recv, base, x, y, c):
            cp.start()

    def middle(rin, rout, send, recv, base=0):
        x, y, c, chips = _place()
        for arrival, onward in passes(rout, send, recv, base, x, y, c, chips):
            arrival.wait_recv()
            onward.start()

    def finish(rin, rout, send, recv, base=0):
        x, y, c, chips = _place()
        for j, (cx, cy) in enumerate(chips):
            for t in range(ni):
                other = rout[t].at[2 * cx + cy, 1 - c]
                _remote(other, other, send, recv, base + 7 * t + 3 + j, (x, y, 1 - c)).wait_recv()
        for cp in own(rin, rout, send, recv, base, x, y, c):
            cp.wait_recv()
            cp.wait_send()
        for cp in first_hops(rin, rout, send, recv, base, x, y, c, chips):
            cp.wait_send()
        for _, onward in passes(rout, send, recv, base, x, y, c, chips):
            onward.wait_send()

    return _Rider(stacked, [_sds((4,) + a.shape[1:], a.dtype) for a in stacked], 7 * ni, start, finish, middle)


def _pair_rider(grads):
    ni = len(grads)

    def copies(rin, rout, send, recv, base):
        x, y, c, _ = _place()
        return [_remote(rin[t].at[:, 1 - c], rout[t], send, recv, base + t, (x, y, 1 - c)) for t in range(ni)]

    def start(rin, rout, send, recv, base=0):
        for cp in copies(rin, rout, send, recv, base):
            cp.start()

    def finish(rin, rout, send, recv, base=0):
        for cp in copies(rin, rout, send, recv, base):
            cp.wait()

    return _Rider(grads, [_sds(g.shape[:1] + g.shape[2:], g.dtype) for g in grads], ni, start, finish)


def _ici_rider(pair_sums):
    ni = len(pair_sums)

    def copies(rin, rout, send, recv, base):
        x, y, c, chips = _place()
        return [_remote(rin[t].at[j], rout[t].at[j], send, recv, base + 3 * t + j, (cx, cy, c))
                for j, (cx, cy) in enumerate(chips) for t in range(ni)]

    def start(rin, rout, send, recv, base=0):
        for cp in copies(rin, rout, send, recv, base):
            cp.start()

    def finish(rin, rout, send, recv, base=0):
        for cp in copies(rin, rout, send, recv, base):
            cp.wait()

    return _Rider(pair_sums, [_sds((3,) + g.shape[1:], g.dtype) for g in pair_sums], 3 * ni, start, finish)


def _final_rider(summed, slots):
    ni = len(summed)

    def copies(rout, send, recv, base):
        x, y, c, _ = _place()
        return [(_remote(rout[t].at[slots[t], c], rout[t].at[slots[t], c], send, recv, base + t, (x, y, 1 - c)),
                 _remote(rout[t].at[slots[t], 1 - c], rout[t].at[slots[t], 1 - c], send, recv, base + t, (x, y, 1 - c)))
                for t in range(ni)]

    def start(rin, rout, send, recv, base=0):
        for mine, _ in copies(rout, send, recv, base):
            mine.start()

    def finish(rin, rout, send, recv, base=0):
        for mine, theirs in copies(rout, send, recv, base):
            mine.wait_send()
            theirs.wait_recv()

    return _Rider(summed, [_sds(a.shape, a.dtype) for a in summed], ni, start, finish,
                  aliases={t: t for t in range(ni)})


class _Block:
    def __init__(self, array, index):
        self.array, self.index = array, index

    def spec(self):
        index = self.index
        return pl.BlockSpec((None,) + self.array.shape[1:], lambda *_: (index, 0, 0))


def _call(body, *, name, grid, in_specs, out_specs, out_shape, scratch_shapes, operands, rider=None):
    operands, in_specs = list(operands), list(in_specs)
    for n, op in enumerate(operands):
        if isinstance(op, _Block):
            operands[n], in_specs[n] = op.array, op.spec()
    if rider is None:
        outs = pl.pallas_call(body, name=name, grid=grid, in_specs=in_specs, out_specs=out_specs, out_shape=out_shape,
                              scratch_shapes=scratch_shapes, compiler_params=_params(len(grid)))(*operands)
        return list(outs), []
    n_in, n_out, n_scr = len(in_specs), len(out_specs), len(scratch_shapes)
    r_in, r_out = len(rider.inputs), len(rider.out_shapes)
    steps = 1
    for g in grid:
        steps *= g
    mid = steps - 1 if rider.middle_at_end else max(steps - 1 - MIDDLE_STEPS_BEFORE_END, 0)

    def full_body(*refs):
        own_in, rin = refs[:n_in], refs[n_in:n_in + r_in]
        own_out = refs[n_in + r_in:n_in + r_in + n_out]
        rout = refs[n_in + r_in + n_out:n_in + r_in + n_out + r_out]
        own_scr = refs[n_in + r_in + n_out + r_out:n_in + r_in + n_out + r_out + n_scr]
        send, recv = refs[-2], refs[-1]
        step = pl.program_id(0)
        for axis in range(1, len(grid)):
            step = step * grid[axis] + pl.program_id(axis)

        @pl.when(step == 0)
        def _():
            rider.start(rin, rout, send, recv)

        body(*own_in, *own_out, *own_scr)

        if rider.middle is not None:
            @pl.when(step == mid)
            def _():
                rider.middle(rin, rout, send, recv)

        @pl.when(step == steps - 1)
        def _():
            rider.finish(rin, rout, send, recv)

    outs = pl.pallas_call(
        full_body, name=name, grid=grid,
        in_specs=list(in_specs) + [ANY] * r_in, out_specs=list(out_specs) + [ANY] * r_out,
        out_shape=list(out_shape) + rider.out_shapes,
        scratch_shapes=list(scratch_shapes) + [pltpu.SemaphoreType.DMA((rider.n_sems,)), pltpu.SemaphoreType.DMA((rider.n_sems,))],
        input_output_aliases={n_in + a: n_out + b for a, b in rider.aliases.items()},
        compiler_params=_params(len(grid)),
    )(*operands, *rider.inputs)
    return list(outs[:n_out]), list(outs[n_out:])


def _run_rider(rider, name):
    r_in, r_out = len(rider.inputs), len(rider.out_shapes)

    def body(*refs):
        rin, rout, send, recv = refs[:r_in], refs[r_in:r_in + r_out], refs[-2], refs[-1]
        rider.start(rin, rout, send, recv)
        if rider.middle is not None:
            rider.middle(rin, rout, send, recv)
        rider.finish(rin, rout, send, recv)

    outs = pl.pallas_call(
        body, name=name, in_specs=[ANY] * r_in, out_specs=[ANY] * r_out, out_shape=rider.out_shapes,
        scratch_shapes=[pltpu.SemaphoreType.DMA((rider.n_sems,)), pltpu.SemaphoreType.DMA((rider.n_sems,))],
        input_output_aliases=rider.aliases,
    )(*rider.inputs)
    return list(outs)


def _ple_tile(h1, p_ref, gain_ref, wg_v, wp_v):
    xn, _ = _rms(h1)
    hpb = (xn * gain_ref[...]).astype(BF16)
    gate = _sigmoid(_dot(hpb, wg_v[...]))
    pb = p_ref[...].astype(BF16)
    pe = jnp.concatenate([_dot(pb, wp_v[k]) for k in range(4)], axis=1)
    return h1 + gate * pe, gate


def _ple_parts(ple, ts, d):
    p, layer, gain, w_gate, w_proj = ple
    s, pd = p.shape[1:]
    row = pl.BlockSpec((ts, d), lambda i: (i, 0))
    return dict(
        operands=[p, gain, w_gate, w_proj],
        in_specs=[pl.BlockSpec((None, ts, pd), lambda i: (layer, i, 0)), _full((1, d)), ANY, ANY],
        out_specs=[row, row], out_shape=[_sds((s, d), F32), _sds((s, d), BF16)],
        scratch=[pltpu.VMEM((d, d), BF16), pltpu.VMEM((4, pd, d // 4), BF16)])


def _fwd_mix_a(h, gain, conv_w, w_in, w_out, name, rider=None, ple=None):
    s, d = h.shape
    e = MIX_WIDTH
    ts = min(TS_FWD, s)
    nt = s // ts
    extra = _ple_parts(ple, ts, d) if ple else None

    def body(*refs):
        h_ref, gain_ref, cw_ref, win_hbm, wout_hbm = refs[:5]
        n_in = 9 if ple else 5
        h1_ref, proj_ref = refs[n_in:n_in + 2]
        win_v, wout_v, carry, sems = refs[n_in + (4 if ple else 2):][:4]
        i = pl.program_id(0)

        @pl.when(i == 0)
        def _():
            loads = [(win_hbm, win_v), (wout_hbm, wout_v)]
            if ple:
                loads += [(refs[7], refs[-2]), (refs[8], refs[-1])]
            _copy_all(loads, sems)
            carry[...] = jnp.zeros_like(carry)

        hh = h_ref[...]
        xn, _ = _rms(hh)
        hnb = (xn * gain_ref[...]).astype(BF16)
        b = _dot(hnb, win_v[0])
        c = _dot(hnb, win_v[1])
        v = _dot(hnb, win_v[2])
        z = _dot(hnb, win_v[3])
        proj_ref[:, 0 * e:1 * e] = b.astype(BF16)
        proj_ref[:, 1 * e:2 * e] = c.astype(BF16)
        proj_ref[:, 2 * e:3 * e] = v.astype(BF16)
        proj_ref[:, 3 * e:4 * e] = z.astype(BF16)
        cv = c * v
        tail = carry[...]
        carry[...] = cv[ts - HALO:ts]
        conv = cw_ref[0:1, :] * _shift_down(cv, 2, tail) + cw_ref[1:2, :] * _shift_down(cv, 1, tail) + cw_ref[2:3, :] * cv
        mb = ((z * _sigmoid(z)) * (b * conv)).astype(BF16)
        h1 = hh + _dot(mb, wout_v[...])
        h1_ref[...] = h1
        if ple:
            h2, gate = _ple_tile(h1, refs[5], refs[6], refs[-2], refs[-1])
            refs[n_in + 2][...] = h2
            refs[n_in + 3][...] = gate.astype(BF16)

    row = lambda width: pl.BlockSpec((ts, width), lambda i: (i, 0))
    return _call(
        body, name=name, grid=(nt,),
        in_specs=[row(d), _full((1, d)), _full((8, e)), ANY, ANY] + (extra["in_specs"] if ple else []),
        out_specs=[row(d), row(4 * e)] + (extra["out_specs"] if ple else []),
        out_shape=[_sds((s, d), F32), _sds((s, 4 * e), BF16)] + (extra["out_shape"] if ple else []),
        scratch_shapes=[pltpu.VMEM((4, d, e), BF16), pltpu.VMEM((e, d), BF16), pltpu.VMEM((HALO, e), F32),
                        pltpu.SemaphoreType.DMA((4,))] + (extra["scratch"] if ple else []),
        operands=[h, gain, conv_w, w_in, w_out] + (extra["operands"] if ple else []), rider=rider)


def _fwd_mix_b(h, gain, scale, w_in, w_grp, w_out, name, rider=None, ple=None):
    s, d = h.shape
    e = MIX_WIDTH
    ts = min(TS_FWD, s)
    nt = s // ts
    extra = _ple_parts(ple, ts, d) if ple else None

    def body(*refs):
        h_ref, gain_ref, scale_ref, win_hbm, wgrp_hbm, wout_hbm = refs[:6]
        n_in = 10 if ple else 6
        h1_ref, z_ref, mx_ref, dd_ref = refs[n_in:n_in + 4]
        win_v, wgrp_v, wout_v, carry, sems = refs[n_in + (6 if ple else 4):][:5]
        i = pl.program_id(0)

        @pl.when(i == 0)
        def _():
            loads = [(win_hbm, win_v), (wout_hbm, wout_v)] + _grp_pairs(wgrp_hbm, wgrp_v)
            if ple:
                loads += [(refs[8], refs[-2]), (refs[9], refs[-1])]
            _copy_all(loads, sems)
            carry[...] = jnp.zeros_like(carry)

        hh = h_ref[...]
        xn, _ = _rms(hh)
        hnb = (xn * gain_ref[...]).astype(BF16)
        u = jnp.concatenate([_dot(hnb, win_v[0]), _dot(hnb, win_v[1])], axis=1)
        z = jnp.concatenate([_dot(hnb, win_v[2]), _dot(hnb, win_v[3])], axis=1)
        z_ref[...] = z.astype(BF16)
        diff = (_pool_fwd(u, carry, i, ts) - u).astype(BF16)
        dd_ref[...] = diff
        mx = jnp.concatenate(
            [_dot(diff[:, g * GROUP_DIM:(g + 1) * GROUP_DIM], wgrp_v[g]) for g in range(N_GROUPS)], axis=1)
        mx_ref[...] = mx.astype(BF16)
        mb = ((z * _sigmoid(z)) * (mx * scale_ref[...])).astype(BF16)
        h1 = hh + _dot(mb, wout_v[...])
        h1_ref[...] = h1
        if ple:
            h2, gate = _ple_tile(h1, refs[6], refs[7], refs[-2], refs[-1])
            refs[n_in + 4][...] = h2
            refs[n_in + 5][...] = gate.astype(BF16)

    row = lambda width: pl.BlockSpec((ts, width), lambda i: (i, 0))
    return _call(
        body, name=name, grid=(nt,),
        in_specs=[row(d), _full((1, d)), _full((1, e)), ANY, ANY, ANY] + (extra["in_specs"] if ple else []),
        out_specs=[row(d), row(e), row(e), row(e)] + (extra["out_specs"] if ple else []),
        out_shape=[_sds((s, d), F32)] + [_sds((s, e), BF16)] * 3 + (extra["out_shape"] if ple else []),
        scratch_shapes=[pltpu.VMEM((4, d, e // 2), BF16), pltpu.VMEM((N_GROUPS, GROUP_DIM, GROUP_DIM), BF16),
                        pltpu.VMEM((e, d), BF16), pltpu.VMEM((4, HALO, e), F32), pltpu.SemaphoreType.DMA((20,))]
        + (extra["scratch"] if ple else []),
        operands=[h, gain, scale, w_in, w_grp, w_out] + (extra["operands"] if ple else []), rider=rider)


def _fwd_ple(h1, p, gain, w_gate, w_proj, layer, rider=None):
    s, d = h1.shape
    pd = p.shape[-1]
    ts = min(TS_PLE, s)
    nt = s // ts

    def body(h1_ref, p_ref, gain_ref, wg_hbm, wp_hbm, h2_ref, gate_ref, wg_v, wp_v, sems):
        @pl.when(pl.program_id(0) == 0)
        def _():
            _copy_all([(wg_hbm, wg_v), (wp_hbm, wp_v)], sems)

        hh = h1_ref[...]
        xn, _ = _rms(hh)
        hpb = (xn * gain_ref[...]).astype(BF16)
        gate = _sigmoid(_dot(hpb, wg_v[...]))
        pb = p_ref[...].astype(BF16)
        pe = jnp.concatenate([_dot(pb, wp_v[k]) for k in range(4)], axis=1)
        gate_ref[...] = gate.astype(BF16)
        h2_ref[...] = hh + gate * pe

    row = lambda width: pl.BlockSpec((ts, width), lambda i: (i, 0))
    return _call(
        body, name=f"fwd_ple{layer}", grid=(nt,),
        in_specs=[row(d), pl.BlockSpec((None, ts, pd), lambda i: (layer, i, 0)), _full((1, d)), ANY, ANY],
        out_specs=[row(d), row(d)],
        out_shape=[_sds((s, d), F32), _sds((s, d), BF16)],
        scratch_shapes=[pltpu.VMEM((d, d), BF16), pltpu.VMEM((4, pd, d // 4), BF16), pltpu.SemaphoreType.DMA((2,))],
        operands=[h1, p, gain, w_gate, w_proj], rider=rider)


def _bwd_ple(dh2, h1, gate, p, gain, w_gate, w_proj, layer, rider=None, loss_head=None):
    s, d = dh2.shape
    pd = p.shape[-1]
    ts = min(TS_PLE, s)
    nt = s // ts
    qd = d // 4
    n_head = 0 if loss_head is None else 2

    def body(*refs):
        dh2_ref = refs[0]
        h1_ref, gate_ref, p_ref, gain_ref, wg_hbm, wp_hbm, dh1_ref, dgain_ref, dwg_hbm, dwp_hbm = refs[1 + n_head:11 + n_head]
        wg_v, wp_v, acc_g, acc_p, sems = refs[-5:]
        i = pl.program_id(0)

        @pl.when(i == 0)
        def _():
            _copy_all([(wg_hbm, wg_v), (wp_hbm, wp_v)], sems)
            dgain_ref[...] = jnp.zeros_like(dgain_ref)
            acc_g[...] = jnp.zeros_like(acc_g)
            acc_p[...] = jnp.zeros_like(acc_p)

        if loss_head is None:
            g2 = dh2_ref[...]
        else:
            t_ref, fgain_ref, loss_ref, dfgain_ref = refs[1], refs[2], refs[11 + n_head], refs[12 + n_head]

            @pl.when(i == 0)
            def _():
                loss_ref[...] = jnp.zeros_like(loss_ref)
                dfgain_ref[...] = jnp.zeros_like(dfgain_ref)

            xf, rf = _rms(dh2_ref[...])
            err = xf * fgain_ref[...] - t_ref[...]
            part = 0.5 * jnp.sum(jnp.mean(err * err, axis=-1, keepdims=True), axis=0, keepdims=True)
            loss_ref[...] += jnp.broadcast_to(part, loss_ref.shape)
            g2, dfgain = _rms_bwd(err * (1.0 / d), xf, rf, fgain_ref[...])
            dfgain_ref[...] += dfgain
        gate_f = gate_ref[...].astype(F32)
        xn, r = _rms(h1_ref[...])
        hpb = (xn * gain_ref[...]).astype(BF16)
        pb = p_ref[...].astype(BF16)
        pe = jnp.concatenate([_dot(pb, wp_v[k]) for k in range(4)], axis=1)
        dpeb = (g2 * gate_f).astype(BF16)
        dab = ((g2 * pe) * (gate_f * (1.0 - gate_f))).astype(BF16)
        acc_g[...] += _dot_tn(hpb, dab)
        for k in range(4):
            acc_p[k] += _dot_tn(pb, dpeb[:, k * qd:(k + 1) * qd])
        dhp = _dot_nt(dab, wg_v[...])
        dh, dgain = _rms_bwd(dhp, xn, r, gain_ref[...])
        dh1_ref[...] = g2 + dh
        dgain_ref[...] += dgain

        @pl.when(i == nt - 1)
        def _():
            _copy_all([(acc_g, dwg_hbm), (acc_p, dwp_hbm)], sems)

    row = pl.BlockSpec((ts, d), lambda i: (i, 0))
    head = loss_head is not None
    return _call(
        body, name=f"bwd_ple{layer}", grid=(nt,),
        in_specs=[row] + ([row, _full((1, d))] if head else [])
        + [row, row, pl.BlockSpec((None, ts, pd), lambda i: (layer, i, 0)), _full((1, d)), ANY, ANY],
        out_specs=[row, _full((1, d)), ANY, ANY] + ([_full((8, 128)), _full((1, d))] if head else []),
        out_shape=[_sds((s, d), F32), _sds((1, d), F32), _sds((d, d), F32), _sds((4, pd, qd), F32)]
        + ([_sds((8, 128), F32), _sds((1, d), F32)] if head else []),
        scratch_shapes=[pltpu.VMEM((d, d), BF16), pltpu.VMEM((4, pd, qd), BF16), pltpu.VMEM((d, d), F32),
                        pltpu.VMEM((4, pd, qd), F32), pltpu.SemaphoreType.DMA((2,))],
        operands=[dh2] + (list(loss_head) if head else []) + [h1, gate, p, gain, w_gate, w_proj], rider=rider)


def _mix_a_tile_grads(proj_ref, ch_ref, vh_ref, cw_ref, dh1b, wout_v, carry, dcw_ref, tile, hb):
    e = MIX_WIDTH
    b = proj_ref[:, 0 * e:1 * e].astype(F32)
    c = proj_ref[:, 1 * e:2 * e].astype(F32)
    v = proj_ref[:, 2 * e:3 * e].astype(F32)
    z = proj_ref[:, 3 * e:4 * e].astype(F32)
    cv = c * v
    prev = (ch_ref[...].astype(F32) * vh_ref[...].astype(F32))[hb - HALO:hb]
    tail = jnp.where(tile > 0, prev, jnp.zeros_like(prev))
    cv1 = _shift_down(cv, 1, tail)
    cv2 = _shift_down(cv, 2, tail)
    conv = cw_ref[0:1, :] * cv2 + cw_ref[1:2, :] * cv1 + cw_ref[2:3, :] * cv
    sig = _sigmoid(z)
    sz = z * sig
    y = b * conv
    dm = _dot_nt(dh1b, wout_v[...])
    dz = (dm * y) * (sig * (1.0 + z * (1.0 - sig)))
    dy = dm * sz
    db = dy * conv
    dconv = dy * b
    head = carry[...]
    carry[...] = dconv[0:HALO]
    dcv = cw_ref[2:3, :] * dconv + cw_ref[1:2, :] * _shift_up(dconv, 1, head) + cw_ref[0:1, :] * _shift_up(dconv, 2, head)
    dcw_ref[0:1, :] += jnp.sum(dconv * cv2, axis=0, keepdims=True)
    dcw_ref[1:2, :] += jnp.sum(dconv * cv1, axis=0, keepdims=True)
    dcw_ref[2:3, :] += jnp.sum(dconv * cv, axis=0, keepdims=True)
    parts = [db.astype(BF16), (dcv * v).astype(BF16), (dcv * c).astype(BF16), dz.astype(BF16)]
    return parts, (sz * y).astype(BF16)


def _bwd_mix_a(dh1, h, proj, gain, conv_w, w_in, w_out, name, rider=None):
    s, d = dh1.shape
    e = MIX_WIDTH
    ts = min(TS_MIX, s)
    nt = s // ts
    hb = 16
    per = ts // hb

    def body(dh1_ref, h_ref, proj_ref, ch_ref, vh_ref, gain_ref, cw_ref, win_hbm, wout_hbm,
             dh_ref, dcw_ref, dgain_ref, dwin_hbm, dwout_hbm, win_v, wout_v, acc_in, acc_out, carry, sems):
        i = pl.program_id(0)

        @pl.when(i == 0)
        def _():
            _copy_all([(win_hbm, win_v), (wout_hbm, wout_v)], sems)
            carry[...] = jnp.zeros_like(carry)
            dcw_ref[...] = jnp.zeros_like(dcw_ref)
            dgain_ref[...] = jnp.zeros_like(dgain_ref)
            acc_in[...] = jnp.zeros_like(acc_in)
            acc_out[...] = jnp.zeros_like(acc_out)

        dh1 = dh1_ref[...]
        dh1b = dh1.astype(BF16)
        parts, mb = _mix_a_tile_grads(proj_ref, ch_ref, vh_ref, cw_ref, dh1b, wout_v, carry, dcw_ref, nt - 1 - i, hb)
        acc_out[...] += _dot_tn(mb, dh1b)
        xn, r = _rms(h_ref[...])
        hnb = (xn * gain_ref[...]).astype(BF16)
        for q in range(4):
            acc_in[q] += _dot_tn(hnb, parts[q])
        dhn = _dot_nt(parts[0], win_v[0]) + _dot_nt(parts[1], win_v[1]) + _dot_nt(parts[2], win_v[2]) + _dot_nt(parts[3], win_v[3])
        dh, dgain = _rms_bwd(dhn, xn, r, gain_ref[...])
        dh_ref[...] = dh1 + dh
        dgain_ref[...] += dgain

        @pl.when(i == nt - 1)
        def _():
            _copy_all([(acc_in, dwin_hbm), (acc_out, dwout_hbm)], sems)

    row = lambda width: pl.BlockSpec((ts, width), lambda i: (nt - 1 - i, 0))
    halo = lambda col: pl.BlockSpec((hb, e), lambda i: (jnp.maximum((nt - 1 - i) * per - 1, 0), col))
    return _call(
        body, name=name, grid=(nt,),
        in_specs=[row(d), row(d), row(4 * e), halo(1), halo(2), _full((1, d)), _full((8, e)), ANY, ANY],
        out_specs=[row(d), _full((8, e)), _full((1, d)), ANY, ANY],
        out_shape=[_sds((s, d), F32), _sds((8, e), F32), _sds((1, d), F32), _sds((4, d, e), F32), _sds((e, d), F32)],
        scratch_shapes=[pltpu.VMEM((4, d, e), BF16), pltpu.VMEM((e, d), BF16), pltpu.VMEM((4, d, e), F32),
                        pltpu.VMEM((e, d), F32), pltpu.VMEM((HALO, e), F32), pltpu.SemaphoreType.DMA((2,))],
        operands=[dh1, h, proj, proj, proj, gain, conv_w, w_in, w_out], rider=rider)


def _bwd_mix_a_weights(dh1, h, proj, gain, conv_w, w_out, name, rider=None):
    s, d = dh1.shape
    e = MIX_WIDTH
    ts = min(TS_MIX, s)
    nt = s // ts
    hb = 16
    per = ts // hb

    def body(dh1_ref, h_ref, proj_ref, ch_ref, vh_ref, gain_ref, cw_ref, wout_hbm,
             dproj_ref, dcw_ref, dwin_hbm, dwout_hbm, wout_v, acc_in, acc_out, carry, sems):
        i = pl.program_id(0)

        @pl.when(i == 0)
        def _():
            _copy_all([(wout_hbm, wout_v)], sems)
            carry[...] = jnp.zeros_like(carry)
            dcw_ref[...] = jnp.zeros_like(dcw_ref)
            acc_in[...] = jnp.zeros_like(acc_in)
            acc_out[...] = jnp.zeros_like(acc_out)

        dh1b = dh1_ref[...].astype(BF16)
        parts, mb = _mix_a_tile_grads(proj_ref, ch_ref, vh_ref, cw_ref, dh1b, wout_v, carry, dcw_ref, nt - 1 - i, hb)
        acc_out[...] += _dot_tn(mb, dh1b)
        xn, _ = _rms(h_ref[...])
        hnb = (xn * gain_ref[...]).astype(BF16)
        for q in range(4):
            acc_in[q] += _dot_tn(hnb, parts[q])
            dproj_ref[:, q * e:(q + 1) * e] = parts[q]

        @pl.when(i == nt - 1)
        def _():
            _copy_all([(acc_in, dwin_hbm), (acc_out, dwout_hbm)], sems)

    row = lambda width: pl.BlockSpec((ts, width), lambda i: (nt - 1 - i, 0))
    halo = lambda col: pl.BlockSpec((hb, e), lambda i: (jnp.maximum((nt - 1 - i) * per - 1, 0), col))
    return _call(
        body, name=name, grid=(nt,),
        in_specs=[row(d), row(d), row(4 * e), halo(1), halo(2), _full((1, d)), _full((8, e)), ANY],
        out_specs=[row(4 * e), _full((8, e)), ANY, ANY],
        out_shape=[_sds((s, 4 * e), BF16), _sds((8, e), F32), _sds((4, d, e), F32), _sds((e, d), F32)],
        scratch_shapes=[pltpu.VMEM((e, d), BF16), pltpu.VMEM((4, d, e), F32), pltpu.VMEM((e, d), F32),
                        pltpu.VMEM((HALO, e), F32), pltpu.SemaphoreType.DMA((2,))],
        operands=[dh1, h, proj, proj, proj, gain, conv_w, w_out], rider=rider)


def _bwd_mix_a_input(dproj, h, dh1, gain, w_in, name, rider=None):
    s, d = dh1.shape
    e = MIX_WIDTH
    ts = min(TS_PLE, s)
    nt = s // ts

    def body(dproj_ref, h_ref, dh1_ref, gain_ref, win_hbm, dh_ref, dgain_ref, win_v, sems):
        @pl.when(pl.program_id(0) == 0)
        def _():
            _copy_all([(win_hbm, win_v)], sems)
            dgain_ref[...] = jnp.zeros_like(dgain_ref)

        dhn = _dot_nt(dproj_ref[:, 0:e], win_v[0])
        for q in range(1, 4):
            dhn = dhn + _dot_nt(dproj_ref[:, q * e:(q + 1) * e], win_v[q])
        xn, r = _rms(h_ref[...])
        dh, dgain = _rms_bwd(dhn, xn, r, gain_ref[...])
        dh_ref[...] = dh1_ref[...] + dh
        dgain_ref[...] += dgain

    row = lambda width: pl.BlockSpec((ts, width), lambda i: (i, 0))
    return _call(
        body, name=name, grid=(nt,),
        in_specs=[row(4 * e), row(d), row(d), _full((1, d)), ANY],
        out_specs=[row(d), _full((1, d))],
        out_shape=[_sds((s, d), F32), _sds((1, d), F32)],
        scratch_shapes=[pltpu.VMEM((4, d, e), BF16), pltpu.SemaphoreType.DMA((1,))],
        operands=[dproj, h, dh1, gain, w_in], rider=rider)


def _bwd_mix_b(dh1, h, z, mx, diff, gain, scale, w_in, w_grp, w_out, name, rider=None):
    s, d = dh1.shape
    e = MIX_WIDTH
    ts = min(TS_MIX, s)
    nt = s // ts
    half = e // 2

    def body(dh1_ref, h_ref, z_ref, mx_ref, dd_ref, gain_ref, scale_ref, win_hbm, wgrp_hbm, wout_hbm,
             dh_ref, dscale_ref, dgain_ref, dwin_hbm, dwgrp_hbm, dwout_hbm,
             win_v, wgrp_v, wout_v, acc_in, acc_grp, acc_out, carry, sems):
        i = pl.program_id(0)
        tile = nt - 1 - i

        @pl.when(i == 0)
        def _():
            _copy_all([(win_hbm, win_v), (wout_hbm, wout_v)] + _grp_pairs(wgrp_hbm, wgrp_v), sems)
            carry[...] = jnp.zeros_like(carry)
            dscale_ref[...] = jnp.zeros_like(dscale_ref)
            dgain_ref[...] = jnp.zeros_like(dgain_ref)
            acc_in[...] = jnp.zeros_like(acc_in)
            acc_grp[...] = jnp.zeros_like(acc_grp)
            acc_out[...] = jnp.zeros_like(acc_out)

        zf = z_ref[...].astype(F32)
        mxf = mx_ref[...].astype(F32)
        sig = _sigmoid(zf)
        sz = zf * sig
        mixed = mxf * scale_ref[...]
        dh1 = dh1_ref[...]
        dh1b = dh1.astype(BF16)
        acc_out[...] += _dot_tn((sz * mixed).astype(BF16), dh1b)
        dm = _dot_nt(dh1b, wout_v[...])
        dz = (dm * mixed) * (sig * (1.0 + zf * (1.0 - sig)))
        dmixed = dm * sz
        dscale_ref[...] += jnp.sum(dmixed * mxf, axis=0, keepdims=True)
        dmxb = (dmixed * scale_ref[...]).astype(BF16)
        diff = dd_ref[...]
        for g in range(N_GROUPS):
            cols = slice(g * GROUP_DIM, (g + 1) * GROUP_DIM)
            acc_grp[g] += _dot_tn(diff[:, cols], dmxb[:, cols])
        ddiff = jnp.concatenate(
            [_dot_nt(dmxb[:, g * GROUP_DIM:(g + 1) * GROUP_DIM], wgrp_v[g]) for g in range(N_GROUPS)], axis=1)
        dub = (_pool_bwd(ddiff, carry, tile, ts) - ddiff).astype(BF16)
        dzb = dz.astype(BF16)
        parts = [dub[:, 0:half], dub[:, half:e], dzb[:, 0:half], dzb[:, half:e]]
        xn, r = _rms(h_ref[...])
        hnb = (xn * gain_ref[...]).astype(BF16)
        for k in range(4):
            acc_in[k] += _dot_tn(hnb, parts[k])
        dhn = _dot_nt(parts[0], win_v[0]) + _dot_nt(parts[1], win_v[1]) + _dot_nt(parts[2], win_v[2]) + _dot_nt(parts[3], win_v[3])
        dh, dgain = _rms_bwd(dhn, xn, r, gain_ref[...])
        dh_ref[...] = dh1 + dh
        dgain_ref[...] += dgain

        @pl.when(i == nt - 1)
        def _():
            _copy_all([(acc_in, dwin_hbm), (acc_out, dwout_hbm)] + [(v, hb_) for hb_, v in _grp_pairs(dwgrp_hbm, acc_grp)], sems)

    row = lambda width: pl.BlockSpec((ts, width), lambda i: (nt - 1 - i, 0))
    return _call(
        body, name=name, grid=(nt,),
        in_specs=[row(d), row(d), row(e), row(e), row(e), _full((1, d)), _full((1, e)), ANY, ANY, ANY],
        out_specs=[row(d), _full((1, e)), _full((1, d)), ANY, ANY, ANY],
        out_shape=[_sds((s, d), F32), _sds((1, e), F32), _sds((1, d), F32), _sds((4, d, half), F32),
                   _sds((4, N_GROUPS, GROUP_DIM // 4, GROUP_DIM), F32), _sds((e, d), F32)],
        scratch_shapes=[pltpu.VMEM((4, d, half), BF16), pltpu.VMEM((N_GROUPS, GROUP_DIM, GROUP_DIM), BF16),
                        pltpu.VMEM((e, d), BF16), pltpu.VMEM((4, d, half), F32),
                        pltpu.VMEM((N_GROUPS, GROUP_DIM, GROUP_DIM), F32), pltpu.VMEM((e, d), F32),
                        pltpu.VMEM((4, HALO, e), F32), pltpu.SemaphoreType.DMA((18,))],
        operands=[dh1, h, z, mx, diff, gain, scale, w_in, w_grp, w_out], rider=rider)


def _first_gather(rider, small):
    shards = rider.inputs
    ni = len(shards)

    def body(*refs):
        rin, small_src = refs[:ni], refs[ni]
        rout, small_dst = refs[ni + 1:2 * ni + 1], refs[2 * ni + 1]
        send, recv, ssend, srecv = refs[2 * ni + 2:]
        x, y, c, chips = _place()
        me = 2 * x + y
        peers = [(cx, cy, c) for cx, cy in chips] + [(x, y, 1 - c)]
        vec = [_remote(small_src, small_dst.at[me], ssend, srecv, j, to) for j, to in enumerate(peers)]
        for cp in vec:
            cp.start()
        rider.start(rin, rout, send, recv)
        rider.middle(rin, rout, send, recv)
        rider.finish(rin, rout, send, recv)
        for j, (px, py, _) in enumerate(peers):
            _remote(small_src, small_dst.at[2 * px + py], ssend, srecv, j, peers[j]).wait_recv()
        for cp in vec:
            cp.wait_send()

    outs = pl.pallas_call(
        body, name="first_gather", in_specs=[ANY] * (ni + 1), out_specs=[ANY] * (ni + 1),
        out_shape=rider.out_shapes + [_sds((4,) + small.shape, small.dtype)],
        scratch_shapes=[pltpu.SemaphoreType.DMA((rider.n_sems,)), pltpu.SemaphoreType.DMA((rider.n_sems,)),
                        pltpu.SemaphoreType.DMA((4,)), pltpu.SemaphoreType.DMA((4,))],
    )(*shards, small)
    return list(outs[:ni]), outs[ni]


def _vector_rider(pack):
    flips = [(fx, fy, fc) for fx in (0, 1) for fy in (0, 1) for fc in (0, 1)][1:]

    def copies(rin, rout, send, recv, base):
        x, y, c, _ = _place()
        me = 4 * x + 2 * y + c
        peers = [(1 - x if fx else x, 1 - y if fy else y, 1 - c if fc else c) for fx, fy, fc in flips]
        own = pltpu.make_async_copy(rin[0], rout[0].at[me], send.at[base + 7])
        out = [_remote(rin[0], rout[0].at[me], send, recv, base + r, peer) for r, peer in enumerate(peers)]
        back = [_remote(rin[0], rout[0].at[4 * px + 2 * py + pc], send, recv, base + r, (px, py, pc))
                for r, (px, py, pc) in enumerate(peers)]
        return own, out, back

    def start(rin, rout, send, recv, base=0):
        own, out, _ = copies(rin, rout, send, recv, base)
        own.start()
        for cp in out:
            cp.start()

    def finish(rin, rout, send, recv, base=0):
        own, out, back = copies(rin, rout, send, recv, base)
        for cp in back:
            cp.wait_recv()
        for cp in out:
            cp.wait_send()
        own.wait()

    return _Rider([pack], [_sds((8,) + pack.shape, pack.dtype)], 8, start, finish)


def _vector_sum(landed, row_counts):
    _, rows, d = landed.shape

    def body(l_ref, *out_refs):
        first = 0
        for n, out_ref in zip(row_counts, out_refs):
            total = l_ref[0, first:first + n, :]
            for dev in range(1, 8):
                total = total + l_ref[dev, first:first + n, :]
            out_ref[...] = total
            first += n

    vmem = pl.BlockSpec(memory_space=pltpu.VMEM)
    return pl.pallas_call(body, name="vector_sum", in_specs=[vmem], out_specs=[vmem] * len(row_counts),
                          out_shape=[_sds((n, d), F32) for n in row_counts])(landed)


def _job_rows(rows, cols):
    return min(rows, max(8, JOB_BLOCK_BYTES // (4 * cols)))


def _pair_sum_job(grad, sibling_rows):
    _, _, rh, cols = grad.shape
    tr = _job_rows(rh, cols)
    nr = rh // tr

    def chip_of(j, pos):
        return jnp.bitwise_xor(pos[0], jnp.where(j == 2, 3, 2 - j))

    return dict(
        ins=[(grad, (None, None, tr, cols), lambda l, pos: (chip_of(l // nr, pos), pos[1], l % nr, 0)),
             (sibling_rows, (None, tr, cols), lambda l, pos: (chip_of(l // nr, pos), l % nr, 0))],
        outs=[((3, rh, cols), BF16, (None, tr, cols), lambda l, pos: (l // nr, l % nr, 0))],
        steps=3 * nr, fn=lambda g, sb: [(g + sb).astype(BF16)], alias=None)


def _final_sum_job(grad, sibling_rows, landed, stack, slot, n_slots):
    _, _, rh, cols = grad.shape
    tr = _job_rows(rh, cols)

    def fn(g, sb, ld):
        total = g + sb
        for j in range(3):
            total = total + ld[j].astype(F32)
        return [total]

    return dict(
        ins=[(grad, (None, None, tr, cols), lambda l, pos: (pos[0], pos[1], l, 0)),
             (sibling_rows, (None, tr, cols), lambda l, pos: (pos[0], l, 0)),
             (landed, (3, tr, cols), lambda l, pos: (0, l, 0))],
        outs=[((n_slots, 2, rh, cols), F32, (None, None, tr, cols), lambda l, pos: (slot, pos[1], l, 0))],
        steps=rh // tr, fn=fn, alias=stack)


def _adamw_job(g, w, m, v, block_bytes):
    rows, cols = g.shape
    tr = min(rows, max(8, block_bytes // (4 * cols)))

    def fn(gg, ww, mm, vv):
        nm = ADAM_B1 * mm + (1.0 - ADAM_B1) * gg
        nv = ADAM_B2 * vv + (1.0 - ADAM_B2) * (gg * gg)
        m_hat = nm / (1.0 - ADAM_B1 ** ADAM_STEP)
        v_hat = nv / (1.0 - ADAM_B2 ** ADAM_STEP)
        return [-ADAM_LR * (m_hat / (jnp.sqrt(v_hat) + ADAM_EPS) + ADAM_WD * ww), nm, nv, gg]

    block = lambda l, pos: (l, 0)
    return dict(ins=[(a, (tr, cols), block) for a in (g, w, m, v)],
                outs=[((rows, cols), F32, (tr, cols), block)] * 4, steps=rows // tr, fn=fn, alias=None)


def _run_jobs(jobs, place, name):
    starts, total = [], 0
    for jb in jobs:
        starts.append(total)
        total += jb["steps"]

    def clamped(fn, start, steps):
        return lambda s, pos: fn(jnp.clip(s - start, 0, steps - 1), pos)

    in_specs, operands = [], [place]
    for jb, start in zip(jobs, starts):
        for arr, block, fn in jb["ins"]:
            in_specs.append(pl.BlockSpec(block, clamped(fn, start, jb["steps"])))
            operands.append(arr)
    n_ins = len(in_specs)
    first_out, n_outs = [], 0
    for jb in jobs:
        first_out.append(n_outs)
        n_outs += len(jb["outs"])
    aliases = {}
    for t, jb in enumerate(jobs):
        if jb["alias"] is not None:
            in_specs.append(ANY)
            operands.append(jb["alias"])
            aliases[len(operands) - 1] = first_out[t]
    out_specs = [pl.BlockSpec(block, clamped(fn, start, jb["steps"]))
                 for jb, start in zip(jobs, starts) for _, _, block, fn in jb["outs"]]

    def body(place_ref, *refs):
        in_refs, out_refs = refs[:n_ins], refs[len(in_specs):]
        s = pl.program_id(0)
        first = 0
        for t, (jb, start) in enumerate(zip(jobs, starts)):
            mine = in_refs[first:first + len(jb["ins"])]
            first += len(jb["ins"])

            @pl.when((s >= start) & (s < start + jb["steps"]))
            def _(mine=mine, t=t, jb=jb):
                values = jb["fn"](*[r[...] for r in mine])
                for n, value in enumerate(values):
                    out_refs[first_out[t] + n][...] = value

    grid_spec = pltpu.PrefetchScalarGridSpec(num_scalar_prefetch=1, grid=(total,), in_specs=in_specs, out_specs=out_specs)
    outs = pl.pallas_call(body, name=name, grid_spec=grid_spec,
                          out_shape=[_sds(shape, dtype) for jb in jobs for shape, dtype, _, _ in jb["outs"]],
                          input_output_aliases=aliases, compiler_params=_params(1))(*operands)
    return [list(outs[first_out[t]:first_out[t] + len(jb["outs"])]) for t, jb in enumerate(jobs)]


BIG = ["a_w_in", "a_w_out", "b_w_in", "b_w_grp", "b_w_out", "ple_w_gate", "ple_w_proj"]

GATHER_PLAN = {
    "first": [("a_w_in", 0), ("a_w_out", 0)],
    "mix0": [("ple_w_gate", 0), ("ple_w_proj", 0), ("b_w_in", 0), ("b_w_grp", 0), ("b_w_out", 0)],
    "ple0": [("ple_w_gate", 1), ("ple_w_proj", 1), ("ple_w_proj", 2)],
    "mix1": [("a_w_in", 1), ("a_w_out", 1), ("ple_w_gate", 2)],
    "mix2": [("b_w_in", 1), ("b_w_grp", 1), ("b_w_out", 1), ("ple_w_gate", 3), ("ple_w_proj", 3)],
}
GATHER_LONGER_THAN_HOST = ("mix0", "mix1")


def _as_2d(name, a):
    if name == "b_w_grp":
        return a.reshape(a.shape[0], N_GROUPS * (GROUP_DIM // 4), GROUP_DIM)
    return a


def kernel(x, p, norm_mix, a_w_in, a_w_conv, a_w_out, b_w_in, b_w_grp, b_scale, b_w_out, ple_norm, ple_w_gate, ple_w_proj, final_norm, loss_target, m_norm_mix, m_a_w_in, m_a_w_conv, m_a_w_out, m_b_w_in, m_b_w_grp, m_b_scale, m_b_w_out, m_ple_norm, m_ple_w_gate, m_ple_w_proj, m_final_norm, v_norm_mix, v_a_w_in, v_a_w_conv, v_a_w_out, v_b_w_in, v_b_w_grp, v_b_scale, v_b_w_out, v_ple_norm, v_ple_w_gate, v_ple_w_proj, v_final_norm):
    d, e = D_MODEL, MIX_WIDTH
    s = x.shape[1]
    cx, cy, cc = lax.axis_index("x"), lax.axis_index("y"), lax.axis_index("c")
    chip = 2 * cx + cy
    place = jnp.stack([chip, cc]).astype(jnp.int32)

    weights = dict(a_w_in=a_w_in, a_w_out=a_w_out, b_w_in=b_w_in, b_w_grp=b_w_grp, b_w_out=b_w_out,
                   ple_w_gate=ple_w_gate, ple_w_proj=ple_w_proj)
    moms = dict(a_w_in=m_a_w_in, a_w_out=m_a_w_out, b_w_in=m_b_w_in, b_w_grp=m_b_w_grp, b_w_out=m_b_w_out,
                ple_w_gate=m_ple_w_gate, ple_w_proj=m_ple_w_proj)
    vars_ = dict(a_w_in=v_a_w_in, a_w_out=v_a_w_out, b_w_in=v_b_w_in, b_w_grp=v_b_w_grp, b_w_out=v_b_w_out,
                 ple_w_gate=v_ple_w_gate, ple_w_proj=v_ple_w_proj)
    w2d = {nm: _as_2d(nm, weights[nm]) for nm in BIG}
    bf = {nm: w2d[nm].astype(BF16).reshape(w2d[nm].shape[0], 2, w2d[nm].shape[1] // 2, w2d[nm].shape[2]) for nm in BIG}
    gathered = {}

    def gather_rider(host):
        keys = GATHER_PLAN.get(host)
        if not keys:
            return None
        rider = _gather_rider([bf[nm] for nm, _ in keys], [j for _, j in keys])
        rider.middle_at_end = host in GATHER_LONGER_THAN_HOST
        return rider

    def keep(host, landed):
        for k, a in zip(GATHER_PLAN.get(host, []), landed):
            gathered[k] = a

    def weight(nm, j):
        a = gathered[(nm, j)]
        shapes = {"a_w_in": (4, d, e), "a_w_out": (e, d), "b_w_in": (4, d, e // 2),
                  "b_w_grp": (4, N_GROUPS, GROUP_DIM // 4, GROUP_DIM), "b_w_out": (e, d), "ple_w_gate": (d, d),
                  "ple_w_proj": (4, PLE_DIM, d // 4)}
        return a.reshape(shapes[nm])

    pad = jnp.zeros((4, e // 4), F32)
    small = jnp.concatenate([a_w_conv[0], b_scale[0:1], pad, a_w_conv[1], b_scale[1:2], pad], axis=0)
    landed, small_full = _first_gather(gather_rider("first"), small)
    keep("first", landed)
    small_full = small_full.transpose(1, 0, 2).reshape(16, e)
    conv_w = [_Block(small_full.reshape(2, 8, e), j) for j in range(2)]
    scale_w = [_Block(small_full.reshape(16, 1, e), 8 * j + 3) for j in range(2)]

    p3 = p.reshape(DEPTH, s, PLE_DIM)
    mix_gain = [_Block(norm_mix.reshape(DEPTH, 1, d), i) for i in range(DEPTH)]
    ple_gain = [_Block(ple_norm.reshape(DEPTH, 1, d), i) for i in range(DEPTH)]

    h = x.reshape(s, d)
    saved = []
    for i in range(DEPTH):
        j = i // 2
        rider = gather_rider(f"mix{i}")
        ple = (p3, i, ple_gain[i], weight("ple_w_gate", i), weight("ple_w_proj", i)) if i > 0 else None
        if i % 2 == 0:
            outs, landed = _fwd_mix_a(h, mix_gain[i], conv_w[j], weight("a_w_in", j), weight("a_w_out", j),
                                      f"fwd_mix_a{j}", rider, ple)
            mix = dict(proj=outs[1])
        else:
            outs, landed = _fwd_mix_b(h, mix_gain[i], scale_w[j], weight("b_w_in", j), weight("b_w_grp", j),
                                      weight("b_w_out", j), f"fwd_mix_b{j}", rider, ple)
            mix = dict(z=outs[1], mx=outs[2], diff=outs[3])
        keep(f"mix{i}", landed)
        h1 = outs[0]
        if ple:
            h2, gate = outs[-2:]
        else:
            (h2, gate), landed = _fwd_ple(h1, p3, ple_gain[i], weight("ple_w_gate", i), weight("ple_w_proj", i), i,
                                          gather_rider(f"ple{i}"))
            keep(f"ple{i}", landed)
        saved.append(dict(h=h, h1=h1, gate=gate, **mix))
        h = h2

    n_slots = {nm: weights[nm].shape[0] for nm in BIG}
    stacks = {nm: None for nm in BIG}

    class Group:
        def __init__(self, keys, grads):
            self.keys, self.stage = keys, 0
            self.g32 = [g.reshape(4, 2, w2d[nm].shape[1] // 2, w2d[nm].shape[2]) for (nm, _), g in zip(keys, grads)]

        def rider(self):
            if self.stage == 0:
                return _pair_rider(self.g32)
            if self.stage == 1:
                return _ici_rider(self.pair_sums)
            return _final_rider([stacks[nm] for nm, _ in self.keys], [j for _, j in self.keys])

        def jobs_after(self, landed):
            if self.stage == 0:
                self.from_sibling = landed
                return [_pair_sum_job(g, sb) for g, sb in zip(self.g32, landed)]
            if self.stage == 1:
                return [_final_sum_job(g, sb, ld, stacks[nm], j, n_slots[nm])
                        for (nm, j), g, sb, ld in zip(self.keys, self.g32, self.from_sibling, landed)]
            return []

        def advance(self, landed, summed):
            if self.stage == 0:
                self.pair_sums = summed
            else:
                for (nm, _), a in zip(self.keys, summed if self.stage == 1 else landed):
                    stacks[nm] = a
            self.stage += 1

    active = []
    batches = [0]

    def riders_now():
        parts = [g.rider() for g in active]
        return parts, _merge(parts)

    def advance_all(parts, landed):
        groups = list(active)
        pieces = _split(landed, parts)
        jobs = [g.jobs_after(l) for g, l in zip(groups, pieces)]
        flat = sum(jobs, [])
        outs = [o[0] for o in _run_jobs(flat, place, f"reduce_sums{batches[0]}")] if flat else []
        batches[0] += 1
        for g, l, jb in zip(groups, pieces, jobs):
            g.advance(l, outs[:len(jb)])
            outs = outs[len(jb):]
            if g.stage == 3:
                active.remove(g)

    d_mix_gain, d_ple_gain = [None] * DEPTH, [None] * DEPTH
    d_conv, d_scale = [None] * 2, [None] * 2
    for i in reversed(range(DEPTH)):
        j = i // 2
        sv = saved[i]
        parts, rider = riders_now()
        if i == DEPTH - 1:
            (dh1, d_ple_gain[i], dwg, dwp, loss_part, d_final), landed = _bwd_ple(
                h, sv["h1"], sv["gate"], p3, ple_gain[i], weight("ple_w_gate", i), weight("ple_w_proj", i), i, rider,
                loss_head=(loss_target.reshape(s, d), final_norm.reshape(1, d)))
        else:
            (dh1, d_ple_gain[i], dwg, dwp), landed = _bwd_ple(
                dh, sv["h1"], sv["gate"], p3, ple_gain[i], weight("ple_w_gate", i), weight("ple_w_proj", i), i, rider)
        advance_all(parts, landed)
        active.append(Group([("ple_w_gate", i), ("ple_w_proj", i)], [dwg, dwp]))
        parts, rider = riders_now()
        if i == 0:
            (dproj, d_conv[0], dwin, dwout), landed = _bwd_mix_a_weights(
                dh1, sv["h"], sv["proj"], mix_gain[0], conv_w[0], weight("a_w_out", 0), "bwd_mix_a0_weights", rider)
            advance_all(parts, landed)
            active.append(Group([("a_w_in", 0), ("a_w_out", 0)], [dwin, dwout]))
            parts, rider = riders_now()
            advance_all(parts, _run_rider(rider, "pair_exchange0"))
            parts, rider = riders_now()
            (dh, d_mix_gain[0]), landed = _bwd_mix_a_input(dproj, sv["h"], dh1, mix_gain[0], weight("a_w_in", 0),
                                                          "bwd_mix_a0_input", rider)
            advance_all(parts, landed)
            continue
        if i % 2 == 0:
            (dh, d_conv[j], d_mix_gain[i], dwin, dwout), landed = _bwd_mix_a(
                dh1, sv["h"], sv["proj"], mix_gain[i], conv_w[j], weight("a_w_in", j), weight("a_w_out", j),
                f"bwd_mix_a{j}", rider)
            new = Group([("a_w_in", j), ("a_w_out", j)], [dwin, dwout])
        else:
            (dh, d_scale[j], d_mix_gain[i], dwin, dwgrp, dwout), landed = _bwd_mix_b(
                dh1, sv["h"], sv["z"], sv["mx"], sv["diff"], mix_gain[i], scale_w[j], weight("b_w_in", j),
                weight("b_w_grp", j), weight("b_w_out", j), f"bwd_mix_b{j}", rider)
            new = Group([("b_w_in", j), ("b_w_grp", j), ("b_w_out", j)], [dwin, dwgrp, dwout])
        advance_all(parts, landed)
        active.append(new)
    grad_x = dh.reshape(1, s, d)

    pieces = (d_mix_gain + d_ple_gain + [d_final, d_conv[0][0:3], d_conv[1][0:3]] + d_scale
              + [jnp.tile(loss_part[0:1], (1, d // 128))])
    used = sum(a.shape[0] for a in pieces)
    pack = jnp.concatenate(pieces + [jnp.zeros((-used % PACK_GROUP, d), F32)], axis=0)
    vectors = _vector_rider(pack)
    tail = 0
    while active:
        parts, _ = riders_now()
        extra = [vectors] if tail == 0 else []
        landed = _run_rider(_merge(parts + extra), f"tail_exchange{tail}")
        if extra:
            g_mix, g_ple, g_final, g_conv, g_scale, loss_row = _vector_sum(
                _split(landed, parts + extra)[-1][0], [DEPTH, DEPTH, 1, 6, 2, 1])
        advance_all(parts, landed)
        tail += 1
    loss = loss_row[0, 0]

    mine = lambda a: lax.dynamic_slice_in_dim(a, chip * (e // 4), e // 4, axis=1)
    row = lambda a: a.reshape(1, d)
    taps = lambda a: a.reshape(6, e // 4)
    flat = {nm: (w2d[nm].shape[0] * w2d[nm].shape[1], w2d[nm].shape[2]) for nm in BIG}
    tensors = {nm: (stacks[nm].reshape(flat[nm]), w2d[nm].reshape(flat[nm]), _as_2d(nm, moms[nm]).reshape(flat[nm]),
                    _as_2d(nm, vars_[nm]).reshape(flat[nm])) for nm in BIG}
    tensors.update(
        norm_mix=(g_mix, norm_mix, m_norm_mix, v_norm_mix), ple_norm=(g_ple, ple_norm, m_ple_norm, v_ple_norm),
        final_norm=(g_final, row(final_norm), row(m_final_norm), row(v_final_norm)),
        a_w_conv=(mine(g_conv), taps(a_w_conv), taps(m_a_w_conv), taps(v_a_w_conv)),
        b_scale=(mine(g_scale), b_scale, m_b_scale, v_b_scale))
    order = ["norm_mix", "a_w_in", "a_w_conv", "a_w_out", "b_w_in", "b_w_grp", "b_scale", "b_w_out", "ple_norm",
             "ple_w_gate", "ple_w_proj", "final_norm"]
    shapes = dict(norm_mix=norm_mix.shape, ple_norm=ple_norm.shape, final_norm=final_norm.shape,
                  a_w_conv=a_w_conv.shape, b_scale=b_scale.shape, **{nm: weights[nm].shape for nm in BIG})
    updates = {nm: _run_jobs([_adamw_job(*tensors[nm], ADAMW_BIG_BLOCK_BYTES)], place, f"adamw_{nm}")[0]
               for nm in ADAMW_ALONE}
    rest = [nm for nm in order if nm not in ADAMW_ALONE]
    updates.update(zip(rest, _run_jobs([_adamw_job(*tensors[nm], ADAMW_BLOCK_BYTES) for nm in rest], place, "adamw_rest")))
    outs = [loss, grad_x]
    for which in (3, 0, 1, 2):
        outs += [updates[nm][which].reshape(shapes[nm]) for nm in order]
    return tuple(outs)
```

```python
import jax
import jax.numpy as jnp
from jax import lax
from jax.experimental import pallas as pl
from jax.experimental.pallas import tpu as pltpu

F32 = jnp.float32
BF16 = jnp.bfloat16
MESH = pl.DeviceIdType.MESH

D_MODEL = 1024
MIX_WIDTH = 1024
PLE_DIM = 256
N_GROUPS = 4
GROUP_DIM = 256
POOL_WINDOWS = (2, 4, 8, 16)
DEPTH = 4
EPS = 1e-6

ADAM_LR = 0.001
ADAM_B1 = 0.9
ADAM_B2 = 0.999
ADAM_EPS = 1e-08
ADAM_WD = 0.01
ADAM_STEP = 10

HALO = 8
TS_MIX = 256
TS_BWD_POOL = 512
TS_FWD = 512
TS_PLE = 512
VMEM_LIMIT = 56 * 1024 * 1024
PACK_GROUP = 8
JOB_BLOCK_BYTES = 2 * 1024 * 1024
ADAMW_BLOCK_BYTES = 512 * 1024
ADAMW_BIG_BLOCK_BYTES = 2 * 1024 * 1024
ADAMW_ALONE = ("a_w_in", "b_w_in", "ple_w_gate")
MIDDLE_STEPS_BEFORE_END = 1

ANY = pl.BlockSpec(memory_space=pl.ANY)


def _sds(shape, dtype):
    return jax.ShapeDtypeStruct(shape, dtype)


def _full(shape):
    nd = len(shape)
    return pl.BlockSpec(shape, lambda *_: (0,) * nd)


def _params(n_axes=1):
    return pltpu.CompilerParams(dimension_semantics=("arbitrary",) * n_axes, vmem_limit_bytes=VMEM_LIMIT)


def _dot(a, b):
    return jnp.dot(a, b, preferred_element_type=F32)


def _dot_nt(a, b):
    return lax.dot_general(a, b, (((1,), (1,)), ((), ())), preferred_element_type=F32)


def _dot_tn(a, b):
    return lax.dot_general(a, b, (((0,), (0,)), ((), ())), preferred_element_type=F32)


def _sigmoid(z):
    return 1.0 / (1.0 + jnp.exp(-z))


def _shift_down(x, k, tail):
    rolled = pltpu.roll(x, k, 0)
    rt = tail if k % HALO == 0 else pltpu.roll(tail, k % HALO, 0)
    row = lax.broadcasted_iota(jnp.int32, rt.shape, 0)
    head = jnp.where(row < k, rt, rolled[0:HALO])
    return jnp.concatenate([head, rolled[HALO:]], axis=0)


def _shift_up(x, k, head_next):
    n = x.shape[0]
    rolled = pltpu.roll(x, n - k, 0)
    rh = head_next if k % HALO == 0 else pltpu.roll(head_next, HALO - k % HALO, 0)
    row = lax.broadcasted_iota(jnp.int32, rh.shape, 0)
    tail = jnp.where(row >= HALO - k, rh, rolled[n - HALO:n])
    return jnp.concatenate([rolled[:n - HALO], tail], axis=0)


def _inv_counts(tile, ts):
    t = tile * ts + lax.broadcasted_iota(jnp.int32, (ts, 1), 0)
    return [1.0 / jnp.minimum(t + 1, w).astype(F32) for w in POOL_WINDOWS]


def _pool_fwd(u, carry, tile, ts):
    inv = _inv_counts(tile, ts)
    outs = []
    for g, w in enumerate(POOL_WINDOWS):
        cols = slice(g * GROUP_DIM, (g + 1) * GROUP_DIM)
        s = u[:, cols]
        level, k = 0, 1
        while k < w:
            tail = carry[level, :, cols]
            carry[level, :, cols] = s[ts - HALO:ts]
            s = s + _shift_down(s, k, tail)
            level, k = level + 1, k * 2
        outs.append(s * inv[g])
    return jnp.concatenate(outs, axis=1)


def _pool_bwd(dd, carry, tile, ts):
    inv = _inv_counts(tile, ts)
    outs = []
    for g, w in enumerate(POOL_WINDOWS):
        cols = slice(g * GROUP_DIM, (g + 1) * GROUP_DIM)
        q = dd[:, cols] * inv[g]
        level, k = 0, 1
        while k < w:
            head = carry[level, :, cols]
            carry[level, :, cols] = q[0:HALO]
            q = q + _shift_up(q, k, head)
            level, k = level + 1, k * 2
        outs.append(q)
    return jnp.concatenate(outs, axis=1)


def _copy_all(pairs, sems):
    copies = [pltpu.make_async_copy(src, dst, sems.at[n]) for n, (src, dst) in enumerate(pairs)]
    for cp in copies:
        cp.start()
    for cp in copies:
        cp.wait()


def _grp_pairs(wgrp_hbm, wgrp_v):
    rows = GROUP_DIM // 4
    return [(wgrp_hbm.at[k, g], wgrp_v.at[g, pl.ds(k * rows, rows), :]) for k in range(4) for g in range(N_GROUPS)]


def _rms(h):
    r = lax.rsqrt(jnp.mean(h * h, axis=-1, keepdims=True) + EPS)
    return h * r, r


def _rms_bwd(dhn, xn, r, gain):
    dgain = jnp.sum(dhn * xn, axis=0, keepdims=True)
    dxn = dhn * gain
    dh = r * (dxn - xn * jnp.mean(dxn * xn, axis=-1, keepdims=True))
    return dh, dgain


class _Rider:
    def __init__(self, inputs, out_shapes, n_sems, start, finish, middle=None, aliases=None):
        self.inputs, self.out_shapes, self.n_sems = list(inputs), list(out_shapes), n_sems
        self.start, self.middle, self.finish = start, middle, finish
        self.aliases = dict(aliases or {})
        self.middle_at_end = False


def _merge(riders):
    riders = [r for r in riders if r is not None]
    if not riders:
        return None
    if len(riders) == 1:
        return riders[0]

    def phase(which):
        def run(rin, rout, send, recv, base=0):
            i0 = o0 = s0 = 0
            for r in riders:
                fn = getattr(r, which)
                if fn is not None:
                    fn(rin[i0:i0 + len(r.inputs)], rout[o0:o0 + len(r.out_shapes)], send, recv, base + s0)
                i0, o0, s0 = i0 + len(r.inputs), o0 + len(r.out_shapes), s0 + r.n_sems
        return run

    aliases, i0, o0 = {}, 0, 0
    for r in riders:
        aliases.update({i0 + a: o0 + b for a, b in r.aliases.items()})
        i0, o0 = i0 + len(r.inputs), o0 + len(r.out_shapes)
    return _Rider(sum([r.inputs for r in riders], []), sum([r.out_shapes for r in riders], []),
                  sum(r.n_sems for r in riders), phase("start"), phase("finish"),
                  phase("middle") if any(r.middle for r in riders) else None, aliases)


def _split(landed, riders):
    out, o0 = [], 0
    for r in riders:
        if r is None:
            out.append(None)
        else:
            out.append(landed[o0:o0 + len(r.out_shapes)])
            o0 += len(r.out_shapes)
    return out


def _place():
    x, y, c = lax.axis_index("x"), lax.axis_index("y"), lax.axis_index("c")
    chips = [(1 - x, y), (x, 1 - y), (1 - x, 1 - y)]
    return x, y, c, chips


def _remote(src, dst, send_sems, recv_sems, sem, to):
    return pltpu.make_async_remote_copy(src_ref=src, dst_ref=dst, send_sem=send_sems.at[sem], recv_sem=recv_sems.at[sem],
                                        device_id=to, device_id_type=MESH)


def _gather_rider(stacked, slots):
    ni = len(stacked)

    def first_hops(rin, rout, send, recv, base, x, y, c, chips):
        me = 2 * x + y
        return [_remote(rin[t].at[slots[t], c], rout[t].at[me, c], send, recv, base + 7 * t + j, (cx, cy, c))
                for j, (cx, cy) in enumerate(chips) for t in range(ni)]

    def passes(rout, send, recv, base, x, y, c, chips):
        out = []
        for j, (cx, cy) in enumerate(chips):
            for t in range(ni):
                landed = rout[t].at[2 * cx + cy, c]
                out.append((_remote(landed, landed, send, recv, base + 7 * t + j, (x, y, 1 - c)),
                            _remote(landed, landed, send, recv, base + 7 * t + 3 + j, (x, y, 1 - c))))
        return out

    def own(rin, rout, send, recv, base, x, y, c):
        return [_remote(rin[t].at[slots[t]], rout[t].at[2 * x + y], send, recv, base + 7 * t + 6, (x, y, 1 - c))
                for t in range(ni)]

    def start(rin, rout, send, recv, base=0):
        x, y, c, chips = _place()
        for cp in first_hops(rin, rout, send, recv, base, x, y, c, chips) + own(rin, rout, send, recv, base, x, y, c):
            cp.start()

    def middle(rin, rout, send, recv, base=0):
        x, y, c, chips = _place()
        for arrival, onward in passes(rout, send, recv, base, x, y, c, chips):
            arrival.wait_recv()
            onward.start()

    def finish(rin, rout, send, recv, base=0):
        x, y, c, chips = _place()
        for j, (cx, cy) in enumerate(chips):
            for t in range(ni):
                other = rout[t].at[2 * cx + cy, 1 - c]
                _remote(other, other, send, recv, base + 7 * t + 3 + j, (x, y, 1 - c)).wait_recv()
        for cp in own(rin, rout, send, recv, base, x, y, c):
            cp.wait_recv()
            cp.wait_send()
        for cp in first_hops(rin, rout, send, recv, base, x, y, c, chips):
            cp.wait_send()
        for _, onward in passes(rout, send, recv, base, x, y, c, chips):
            onward.wait_send()

    return _Rider(stacked, [_sds((4,) + a.shape[1:], a.dtype) for a in stacked], 7 * ni, start, finish, middle)


def _pair_rider(grads):
    ni = len(grads)

    def copies(rin, rout, send, recv, base):
        x, y, c, _ = _place()
        return [_remote(rin[t].at[:, 1 - c], rout[t], send, recv, base + t, (x, y, 1 - c)) for t in range(ni)]

    def start(rin, rout, send, recv, base=0):
        for cp in copies(rin, rout, send, recv, base):
            cp.start()

    def finish(rin, rout, send, recv, base=0):
        for cp in copies(rin, rout, send, recv, base):
            cp.wait()

    return _Rider(grads, [_sds(g.shape[:1] + g.shape[2:], g.dtype) for g in grads], ni, start, finish)


def _ici_rider(pair_sums):
    ni = len(pair_sums)

    def copies(rin, rout, send, recv, base):
        x, y, c, chips = _place()
        return [_remote(rin[t].at[j], rout[t].at[j], send, recv, base + 3 * t + j, (cx, cy, c))
                for j, (cx, cy) in enumerate(chips) for t in range(ni)]

    def start(rin, rout, send, recv, base=0):
        for cp in copies(rin, rout, send, recv, base):
            cp.start()

    def finish(rin, rout, send, recv, base=0):
        for cp in copies(rin, rout, send, recv, base):
            cp.wait()

    return _Rider(pair_sums, [_sds((3,) + g.shape[1:], g.dtype) for g in pair_sums], 3 * ni, start, finish)


def _final_rider(summed, slots):
    ni = len(summed)

    def copies(rout, send, recv, base):
        x, y, c, _ = _place()
        return [(_remote(rout[t].at[slots[t], c], rout[t].at[slots[t], c], send, recv, base + t, (x, y, 1 - c)),
                 _remote(rout[t].at[slots[t], 1 - c], rout[t].at[slots[t], 1 - c], send, recv, base + t, (x, y, 1 - c)))
                for t in range(ni)]

    def start(rin, rout, send, recv, base=0):
        for mine, _ in copies(rout, send, recv, base):
            mine.start()

    def finish(rin, rout, send, recv, base=0):
        for mine, theirs in copies(rout, send, recv, base):
            mine.wait_send()
            theirs.wait_recv()

    return _Rider(summed, [_sds(a.shape, a.dtype) for a in summed], ni, start, finish,
                  aliases={t: t for t in range(ni)})


class _Block:
    def __init__(self, array, index):
        self.array, self.index = array, index

    def spec(self):
        index = self.index
        return pl.BlockSpec((None,) + self.array.shape[1:], lambda *_: (index, 0, 0))


def _call(body, *, name, grid, in_specs, out_specs, out_shape, scratch_shapes, operands, rider=None):
    operands, in_specs = list(operands), list(in_specs)
    for n, op in enumerate(operands):
        if isinstance(op, _Block):
            operands[n], in_specs[n] = op.array, op.spec()
    if rider is None:
        outs = pl.pallas_call(body, name=name, grid=grid, in_specs=in_specs, out_specs=out_specs, out_shape=out_shape,
                              scratch_shapes=scratch_shapes, compiler_params=_params(len(grid)))(*operands)
        return list(outs), []
    n_in, n_out, n_scr = len(in_specs), len(out_specs), len(scratch_shapes)
    r_in, r_out = len(rider.inputs), len(rider.out_shapes)
    steps = 1
    for g in grid:
        steps *= g
    mid = steps - 1 if rider.middle_at_end else max(steps - 1 - MIDDLE_STEPS_BEFORE_END, 0)

    def full_body(*refs):
        own_in, rin = refs[:n_in], refs[n_in:n_in + r_in]
        own_out = refs[n_in + r_in:n_in + r_in + n_out]
        rout = refs[n_in + r_in + n_out:n_in + r_in + n_out + r_out]
        own_scr = refs[n_in + r_in + n_out + r_out:n_in + r_in + n_out + r_out + n_scr]
        send, recv = refs[-2], refs[-1]
        step = pl.program_id(0)
        for axis in range(1, len(grid)):
            step = step * grid[axis] + pl.program_id(axis)

        @pl.when(step == 0)
        def _():
            rider.start(rin, rout, send, recv)

        body(*own_in, *own_out, *own_scr)

        if rider.middle is not None:
            @pl.when(step == mid)
            def _():
                rider.middle(rin, rout, send, recv)

        @pl.when(step == steps - 1)
        def _():
            rider.finish(rin, rout, send, recv)

    outs = pl.pallas_call(
        full_body, name=name, grid=grid,
        in_specs=list(in_specs) + [ANY] * r_in, out_specs=list(out_specs) + [ANY] * r_out,
        out_shape=list(out_shape) + rider.out_shapes,
        scratch_shapes=list(scratch_shapes) + [pltpu.SemaphoreType.DMA((rider.n_sems,)), pltpu.SemaphoreType.DMA((rider.n_sems,))],
        input_output_aliases={n_in + a: n_out + b for a, b in rider.aliases.items()},
        compiler_params=_params(len(grid)),
    )(*operands, *rider.inputs)
    return list(outs[:n_out]), list(outs[n_out:])


def _run_rider(rider, name):
    r_in, r_out = len(rider.inputs), len(rider.out_shapes)

    def body(*refs):
        rin, rout, send, recv = refs[:r_in], refs[r_in:r_in + r_out], refs[-2], refs[-1]
        rider.start(rin, rout, send, recv)
        if rider.middle is not None:
            rider.middle(rin, rout, send, recv)
        rider.finish(rin, rout, send, recv)

    outs = pl.pallas_call(
        body, name=name, in_specs=[ANY] * r_in, out_specs=[ANY] * r_out, out_shape=rider.out_shapes,
        scratch_shapes=[pltpu.SemaphoreType.DMA((rider.n_sems,)), pltpu.SemaphoreType.DMA((rider.n_sems,))],
        input_output_aliases=rider.aliases,
    )(*rider.inputs)
    return list(outs)


def _ple_tile(h1, p_ref, gain_ref, wg_v, wp_v):
    xn, _ = _rms(h1)
    hpb = (xn * gain_ref[...]).astype(BF16)
    gate = _sigmoid(_dot(hpb, wg_v[...]))
    pb = p_ref[...].astype(BF16)
    pe = jnp.concatenate([_dot(pb, wp_v[k]) for k in range(4)], axis=1)
    return h1 + gate * pe, gate


def _ple_parts(ple, ts, d):
    p, layer, gain, w_gate, w_proj = ple
    s, pd = p.shape[1:]
    row = pl.BlockSpec((ts, d), lambda i: (i, 0))
    return dict(
        operands=[p, gain, w_gate, w_proj],
        in_specs=[pl.BlockSpec((None, ts, pd), lambda i: (layer, i, 0)), _full((1, d)), ANY, ANY],
        out_specs=[row, row], out_shape=[_sds((s, d), F32), _sds((s, d), BF16)],
        scratch=[pltpu.VMEM((d, d), BF16), pltpu.VMEM((4, pd, d // 4), BF16)])


def _fwd_mix_a(h, gain, conv_w, w_in, w_out, name, rider=None, ple=None):
    s, d = h.shape
    e = MIX_WIDTH
    ts = min(TS_FWD, s)
    nt = s // ts
    extra = _ple_parts(ple, ts, d) if ple else None

    def body(*refs):
        h_ref, gain_ref, cw_ref, win_hbm, wout_hbm = refs[:5]
        n_in = 9 if ple else 5
        h1_ref, proj_ref = refs[n_in:n_in + 2]
        win_v, wout_v, carry, sems = refs[n_in + (4 if ple else 2):][:4]
        i = pl.program_id(0)

        @pl.when(i == 0)
        def _():
            loads = [(win_hbm, win_v), (wout_hbm, wout_v)]
            if ple:
                loads += [(refs[7], refs[-2]), (refs[8], refs[-1])]
            _copy_all(loads, sems)
            carry[...] = jnp.zeros_like(carry)

        hh = h_ref[...]
        xn, _ = _rms(hh)
        hnb = (xn * gain_ref[...]).astype(BF16)
        b = _dot(hnb, win_v[0])
        c = _dot(hnb, win_v[1])
        v = _dot(hnb, win_v[2])
        z = _dot(hnb, win_v[3])
        proj_ref[:, 0 * e:1 * e] = b.astype(BF16)
        proj_ref[:, 1 * e:2 * e] = c.astype(BF16)
        proj_ref[:, 2 * e:3 * e] = v.astype(BF16)
        proj_ref[:, 3 * e:4 * e] = z.astype(BF16)
        cv = c * v
        tail = carry[...]
        carry[...] = cv[ts - HALO:ts]
        conv = cw_ref[0:1, :] * _shift_down(cv, 2, tail) + cw_ref[1:2, :] * _shift_down(cv, 1, tail) + cw_ref[2:3, :] * cv
        mb = ((z * _sigmoid(z)) * (b * conv)).astype(BF16)
        h1 = hh + _dot(mb, wout_v[...])
        h1_ref[...] = h1
        if ple:
            h2, gate = _ple_tile(h1, refs[5], refs[6], refs[-2], refs[-1])
            refs[n_in + 2][...] = h2
            refs[n_in + 3][...] = gate.astype(BF16)

    row = lambda width: pl.BlockSpec((ts, width), lambda i: (i, 0))
    return _call(
        body, name=name, grid=(nt,),
        in_specs=[row(d), _full((1, d)), _full((8, e)), ANY, ANY] + (extra["in_specs"] if ple else []),
        out_specs=[row(d), row(4 * e)] + (extra["out_specs"] if ple else []),
        out_shape=[_sds((s, d), F32), _sds((s, 4 * e), BF16)] + (extra["out_shape"] if ple else []),
        scratch_shapes=[pltpu.VMEM((4, d, e), BF16), pltpu.VMEM((e, d), BF16), pltpu.VMEM((HALO, e), F32),
                        pltpu.SemaphoreType.DMA((4,))] + (extra["scratch"] if ple else []),
        operands=[h, gain, conv_w, w_in, w_out] + (extra["operands"] if ple else []), rider=rider)


def _fwd_mix_b(h, gain, scale, w_in, w_grp, w_out, name, rider=None, ple=None):
    s, d = h.shape
    e = MIX_WIDTH
    ts = min(TS_FWD, s)
    nt = s // ts
    extra = _ple_parts(ple, ts, d) if ple else None

    def body(*refs):
        h_ref, gain_ref, scale_ref, win_hbm, wgrp_hbm, wout_hbm = refs[:6]
        n_in = 10 if ple else 6
        h1_ref, z_ref, mx_ref, dd_ref = refs[n_in:n_in + 4]
        win_v, wgrp_v, wout_v, carry, sems = refs[n_in + (6 if ple else 4):][:5]
        i = pl.program_id(0)

        @pl.when(i == 0)
        def _():
            loads = [(win_hbm, win_v), (wout_hbm, wout_v)] + _grp_pairs(wgrp_hbm, wgrp_v)
            if ple:
                loads += [(refs[8], refs[-2]), (refs[9], refs[-1])]
            _copy_all(loads, sems)
            carry[...] = jnp.zeros_like(carry)

        hh = h_ref[...]
        xn, _ = _rms(hh)
        hnb = (xn * gain_ref[...]).astype(BF16)
        u = jnp.concatenate([_dot(hnb, win_v[0]), _dot(hnb, win_v[1])], axis=1)
        z = jnp.concatenate([_dot(hnb, win_v[2]), _dot(hnb, win_v[3])], axis=1)
        z_ref[...] = z.astype(BF16)
        diff = (_pool_fwd(u, carry, i, ts) - u).astype(BF16)
        dd_ref[...] = diff
        mx = jnp.concatenate(
            [_dot(diff[:, g * GROUP_DIM:(g + 1) * GROUP_DIM], wgrp_v[g]) for g in range(N_GROUPS)], axis=1)
        mx_ref[...] = mx.astype(BF16)
        mb = ((z * _sigmoid(z)) * (mx * scale_ref[...])).astype(BF16)
        h1 = hh + _dot(mb, wout_v[...])
        h1_ref[...] = h1
        if ple:
            h2, gate = _ple_tile(h1, refs[6], refs[7], refs[-2], refs[-1])
            refs[n_in + 4][...] = h2
            refs[n_in + 5][...] = gate.astype(BF16)

    row = lambda width: pl.BlockSpec((ts, width), lambda i: (i, 0))
    return _call(
        body, name=name, grid=(nt,),
        in_specs=[row(d), _full((1, d)), _full((1, e)), ANY, ANY, ANY] + (extra["in_specs"] if ple else []),
        out_specs=[row(d), row(e), row(e), row(e)] + (extra["out_specs"] if ple else []),
        out_shape=[_sds((s, d), F32)] + [_sds((s, e), BF16)] * 3 + (extra["out_shape"] if ple else []),
        scratch_shapes=[pltpu.VMEM((4, d, e // 2), BF16), pltpu.VMEM((N_GROUPS, GROUP_DIM, GROUP_DIM), BF16),
                        pltpu.VMEM((e, d), BF16), pltpu.VMEM((4, HALO, e), F32), pltpu.SemaphoreType.DMA((20,))]
        + (extra["scratch"] if ple else []),
        operands=[h, gain, scale, w_in, w_grp, w_out] + (extra["operands"] if ple else []), rider=rider)


def _fwd_ple(h1, p, gain, w_gate, w_proj, layer, rider=None):
    s, d = h1.shape
    pd = p.shape[-1]
    ts = min(TS_PLE, s)
    nt = s // ts

    def body(h1_ref, p_ref, gain_ref, wg_hbm, wp_hbm, h2_ref, gate_ref, wg_v, wp_v, sems):
        @pl.when(pl.program_id(0) == 0)
        def _():
            _copy_all([(wg_hbm, wg_v), (wp_hbm, wp_v)], sems)

        hh = h1_ref[...]
        xn, _ = _rms(hh)
        hpb = (xn * gain_ref[...]).astype(BF16)
        gate = _sigmoid(_dot(hpb, wg_v[...]))
        pb = p_ref[...].astype(BF16)
        pe = jnp.concatenate([_dot(pb, wp_v[k]) for k in range(4)], axis=1)
        gate_ref[...] = gate.astype(BF16)
        h2_ref[...] = hh + gate * pe

    row = lambda width: pl.BlockSpec((ts, width), lambda i: (i, 0))
    return _call(
        body, name=f"fwd_ple{layer}", grid=(nt,),
        in_specs=[row(d), pl.BlockSpec((None, ts, pd), lambda i: (layer, i, 0)), _full((1, d)), ANY, ANY],
        out_specs=[row(d), row(d)],
        out_shape=[_sds((s, d), F32), _sds((s, d), BF16)],
        scratch_shapes=[pltpu.VMEM((d, d), BF16), pltpu.VMEM((4, pd, d // 4), BF16), pltpu.SemaphoreType.DMA((2,))],
        operands=[h1, p, gain, w_gate, w_proj], rider=rider)


def _bwd_ple(dh2, h1, gate, p, gain, w_gate, w_proj, layer, rider=None, loss_head=None):
    s, d = dh2.shape
    pd = p.shape[-1]
    ts = min(TS_PLE, s)
    nt = s // ts
    qd = d // 4
    n_head = 0 if loss_head is None else 2

    def body(*refs):
        dh2_ref = refs[0]
        h1_ref, gate_ref, p_ref, gain_ref, wg_hbm, wp_hbm, dh1_ref, dgain_ref, dwg_hbm, dwp_hbm = refs[1 + n_head:11 + n_head]
        wg_v, wp_v, acc_g, acc_p, sems = refs[-5:]
        i = pl.program_id(0)

        @pl.when(i == 0)
        def _():
            _copy_all([(wg_hbm, wg_v), (wp_hbm, wp_v)], sems)
            dgain_ref[...] = jnp.zeros_like(dgain_ref)
            acc_g[...] = jnp.zeros_like(acc_g)
            acc_p[...] = jnp.zeros_like(acc_p)

        if loss_head is None:
            g2 = dh2_ref[...]
        else:
            t_ref, fgain_ref, loss_ref, dfgain_ref = refs[1], refs[2], refs[11 + n_head], refs[12 + n_head]

            @pl.when(i == 0)
            def _():
                loss_ref[...] = jnp.zeros_like(loss_ref)
                dfgain_ref[...] = jnp.zeros_like(dfgain_ref)

            xf, rf = _rms(dh2_ref[...])
            err = xf * fgain_ref[...] - t_ref[...]
            part = 0.5 * jnp.sum(jnp.mean(err * err, axis=-1, keepdims=True), axis=0, keepdims=True)
            loss_ref[...] += jnp.broadcast_to(part, loss_ref.shape)
            g2, dfgain = _rms_bwd(err * (1.0 / d), xf, rf, fgain_ref[...])
            dfgain_ref[...] += dfgain
        gate_f = gate_ref[...].astype(F32)
        xn, r = _rms(h1_ref[...])
        hpb = (xn * gain_ref[...]).astype(BF16)
        pb = p_ref[...].astype(BF16)
        pe = jnp.concatenate([_dot(pb, wp_v[k]) for k in range(4)], axis=1)
        dpeb = (g2 * gate_f).astype(BF16)
        dab = ((g2 * pe) * (gate_f * (1.0 - gate_f))).astype(BF16)
        acc_g[...] += _dot_tn(hpb, dab)
        for k in range(4):
            acc_p[k] += _dot_tn(pb, dpeb[:, k * qd:(k + 1) * qd])
        dhp = _dot_nt(dab, wg_v[...])
        dh, dgain = _rms_bwd(dhp, xn, r, gain_ref[...])
        dh1_ref[...] = g2 + dh
        dgain_ref[...] += dgain

        @pl.when(i == nt - 1)
        def _():
            _copy_all([(acc_g, dwg_hbm), (acc_p, dwp_hbm)], sems)

    row = pl.BlockSpec((ts, d), lambda i: (i, 0))
    head = loss_head is not None
    return _call(
        body, name=f"bwd_ple{layer}", grid=(nt,),
        in_specs=[row] + ([row, _full((1, d))] if head else [])
        + [row, row, pl.BlockSpec((None, ts, pd), lambda i: (layer, i, 0)), _full((1, d)), ANY, ANY],
        out_specs=[row, _full((1, d)), ANY, ANY] + ([_full((8, 128)), _full((1, d))] if head else []),
        out_shape=[_sds((s, d), F32), _sds((1, d), F32), _sds((d, d), F32), _sds((4, pd, qd), F32)]
        + ([_sds((8, 128), F32), _sds((1, d), F32)] if head else []),
        scratch_shapes=[pltpu.VMEM((d, d), BF16), pltpu.VMEM((4, pd, qd), BF16), pltpu.VMEM((d, d), F32),
                        pltpu.VMEM((4, pd, qd), F32), pltpu.SemaphoreType.DMA((2,))],
        operands=[dh2] + (list(loss_head) if head else []) + [h1, gate, p, gain, w_gate, w_proj], rider=rider)


def _mix_a_tile_grads(proj_ref, ch_ref, vh_ref, cw_ref, dh1b, wout_v, carry, dcw_ref, tile, hb):
    e = MIX_WIDTH
    b = proj_ref[:, 0 * e:1 * e].astype(F32)
    c = proj_ref[:, 1 * e:2 * e].astype(F32)
    v = proj_ref[:, 2 * e:3 * e].astype(F32)
    z = proj_ref[:, 3 * e:4 * e].astype(F32)
    cv = c * v
    prev = (ch_ref[...].astype(F32) * vh_ref[...].astype(F32))[hb - HALO:hb]
    tail = jnp.where(tile > 0, prev, jnp.zeros_like(prev))
    cv1 = _shift_down(cv, 1, tail)
    cv2 = _shift_down(cv, 2, tail)
    conv = cw_ref[0:1, :] * cv2 + cw_ref[1:2, :] * cv1 + cw_ref[2:3, :] * cv
    sig = _sigmoid(z)
    sz = z * sig
    y = b * conv
    dm = _dot_nt(dh1b, wout_v[...])
    dz = (dm * y) * (sig * (1.0 + z * (1.0 - sig)))
    dy = dm * sz
    db = dy * conv
    dconv = dy * b
    head = carry[...]
    carry[...] = dconv[0:HALO]
    dcv = cw_ref[2:3, :] * dconv + cw_ref[1:2, :] * _shift_up(dconv, 1, head) + cw_ref[0:1, :] * _shift_up(dconv, 2, head)
    dcw_ref[0:1, :] += jnp.sum(dconv * cv2, axis=0, keepdims=True)
    dcw_ref[1:2, :] += jnp.sum(dconv * cv1, axis=0, keepdims=True)
    dcw_ref[2:3, :] += jnp.sum(dconv * cv, axis=0, keepdims=True)
    parts = [db.astype(BF16), (dcv * v).astype(BF16), (dcv * c).astype(BF16), dz.astype(BF16)]
    return parts, (sz * y).astype(BF16)


def _bwd_mix_a(dh1, h, proj, gain, conv_w, w_in, w_out, name, rider=None):
    s, d = dh1.shape
    e = MIX_WIDTH
    ts = min(TS_MIX, s)
    nt = s // ts
    hb = 16
    per = ts // hb

    def body(dh1_ref, h_ref, proj_ref, ch_ref, vh_ref, gain_ref, cw_ref, win_hbm, wout_hbm,
             dh_ref, dcw_ref, dgain_ref, dwin_hbm, dwout_hbm, win_v, wout_v, acc_in, acc_out, carry, sems):
        i = pl.program_id(0)

        @pl.when(i == 0)
        def _():
            _copy_all([(win_hbm, win_v), (wout_hbm, wout_v)], sems)
            carry[...] = jnp.zeros_like(carry)
            dcw_ref[...] = jnp.zeros_like(dcw_ref)
            dgain_ref[...] = jnp.zeros_like(dgain_ref)
            acc_in[...] = jnp.zeros_like(acc_in)
            acc_out[...] = jnp.zeros_like(acc_out)

        dh1 = dh1_ref[...]
        dh1b = dh1.astype(BF16)
        parts, mb = _mix_a_tile_grads(proj_ref, ch_ref, vh_ref, cw_ref, dh1b, wout_v, carry, dcw_ref, nt - 1 - i, hb)
        acc_out[...] += _dot_tn(mb, dh1b)
        xn, r = _rms(h_ref[...])
        hnb = (xn * gain_ref[...]).astype(BF16)
        for q in range(4):
            acc_in[q] += _dot_tn(hnb, parts[q])
        dhn = _dot_nt(parts[0], win_v[0]) + _dot_nt(parts[1], win_v[1]) + _dot_nt(parts[2], win_v[2]) + _dot_nt(parts[3], win_v[3])
        dh, dgain = _rms_bwd(dhn, xn, r, gain_ref[...])
        dh_ref[...] = dh1 + dh
        dgain_ref[...] += dgain

        @pl.when(i == nt - 1)
        def _():
            _copy_all([(acc_in, dwin_hbm), (acc_out, dwout_hbm)], sems)

    row = lambda width: pl.BlockSpec((ts, width), lambda i: (nt - 1 - i, 0))
    halo = lambda col: pl.BlockSpec((hb, e), lambda i: (jnp.maximum((nt - 1 - i) * per - 1, 0), col))
    return _call(
        body, name=name, grid=(nt,),
        in_specs=[row(d), row(d), row(4 * e), halo(1), halo(2), _full((1, d)), _full((8, e)), ANY, ANY],
        out_specs=[row(d), _full((8, e)), _full((1, d)), ANY, ANY],
        out_shape=[_sds((s, d), F32), _sds((8, e), F32), _sds((1, d), F32), _sds((4, d, e), F32), _sds((e, d), F32)],
        scratch_shapes=[pltpu.VMEM((4, d, e), BF16), pltpu.VMEM((e, d), BF16), pltpu.VMEM((4, d, e), F32),
                        pltpu.VMEM((e, d), F32), pltpu.VMEM((HALO, e), F32), pltpu.SemaphoreType.DMA((2,))],
        operands=[dh1, h, proj, proj, proj, gain, conv_w, w_in, w_out], rider=rider)


def _bwd_mix_a_weights(dh1, h, proj, gain, conv_w, w_out, name, rider=None):
    s, d = dh1.shape
    e = MIX_WIDTH
    ts = min(TS_MIX, s)
    nt = s // ts
    hb = 16
    per = ts // hb

    def body(dh1_ref, h_ref, proj_ref, ch_ref, vh_ref, gain_ref, cw_ref, wout_hbm,
             dproj_ref, dcw_ref, dwin_hbm, dwout_hbm, wout_v, acc_in, acc_out, carry, sems):
        i = pl.program_id(0)

        @pl.when(i == 0)
        def _():
            _copy_all([(wout_hbm, wout_v)], sems)
            carry[...] = jnp.zeros_like(carry)
            dcw_ref[...] = jnp.zeros_like(dcw_ref)
            acc_in[...] = jnp.zeros_like(acc_in)
            acc_out[...] = jnp.zeros_like(acc_out)

        dh1b = dh1_ref[...].astype(BF16)
        parts, mb = _mix_a_tile_grads(proj_ref, ch_ref, vh_ref, cw_ref, dh1b, wout_v, carry, dcw_ref, nt - 1 - i, hb)
        acc_out[...] += _dot_tn(mb, dh1b)
        xn, _ = _rms(h_ref[...])
        hnb = (xn * gain_ref[...]).astype(BF16)
        for q in range(4):
            acc_in[q] += _dot_tn(hnb, parts[q])
            dproj_ref[:, q * e:(q + 1) * e] = parts[q]

        @pl.when(i == nt - 1)
        def _():
            _copy_all([(acc_in, dwin_hbm), (acc_out, dwout_hbm)], sems)

    row = lambda width: pl.BlockSpec((ts, width), lambda i: (nt - 1 - i, 0))
    halo = lambda col: pl.BlockSpec((hb, e), lambda i: (jnp.maximum((nt - 1 - i) * per - 1, 0), col))
    return _call(
        body, name=name, grid=(nt,),
        in_specs=[row(d), row(d), row(4 * e), halo(1), halo(2), _full((1, d)), _full((8, e)), ANY],
        out_specs=[row(4 * e), _full((8, e)), ANY, ANY],
        out_shape=[_sds((s, 4 * e), BF16), _sds((8, e), F32), _sds((4, d, e), F32), _sds((e, d), F32)],
        scratch_shapes=[pltpu.VMEM((e, d), BF16), pltpu.VMEM((4, d, e), F32), pltpu.VMEM((e, d), F32),
                        pltpu.VMEM((HALO, e), F32), pltpu.SemaphoreType.DMA((2,))],
        operands=[dh1, h, proj, proj, proj, gain, conv_w, w_out], rider=rider)


def _bwd_mix_a_input(dproj, h, dh1, gain, w_in, name, rider=None):
    s, d = dh1.shape
    e = MIX_WIDTH
    ts = min(TS_PLE, s)
    nt = s // ts

    def body(dproj_ref, h_ref, dh1_ref, gain_ref, win_hbm, dh_ref, dgain_ref, win_v, sems):
        @pl.when(pl.program_id(0) == 0)
        def _():
            _copy_all([(win_hbm, win_v)], sems)
            dgain_ref[...] = jnp.zeros_like(dgain_ref)

        dhn = _dot_nt(dproj_ref[:, 0:e], win_v[0])
        for q in range(1, 4):
            dhn = dhn + _dot_nt(dproj_ref[:, q * e:(q + 1) * e], win_v[q])
        xn, r = _rms(h_ref[...])
        dh, dgain = _rms_bwd(dhn, xn, r, gain_ref[...])
        dh_ref[...] = dh1_ref[...] + dh
        dgain_ref[...] += dgain

    row = lambda width: pl.BlockSpec((ts, width), lambda i: (i, 0))
    return _call(
        body, name=name, grid=(nt,),
        in_specs=[row(4 * e), row(d), row(d), _full((1, d)), ANY],
        out_specs=[row(d), _full((1, d))],
        out_shape=[_sds((s, d), F32), _sds((1, d), F32)],
        scratch_shapes=[pltpu.VMEM((4, d, e), BF16), pltpu.SemaphoreType.DMA((1,))],
        operands=[dproj, h, dh1, gain, w_in], rider=rider)


def _bwd_mix_b(dh1, h, z, mx, diff, gain, scale, w_in, w_grp, w_out, name, rider=None):
    s, d = dh1.shape
    e = MIX_WIDTH
    ts = min(TS_BWD_POOL, s)
    nt = s // ts
    half = e // 2

    def body(dh1_ref, h_ref, z_ref, mx_ref, dd_ref, gain_ref, scale_ref, win_hbm, wgrp_hbm, wout_hbm,
             dh_ref, dscale_ref, dgain_ref, dwin_hbm, dwgrp_hbm, dwout_hbm,
             win_v, wgrp_v, wout_v, acc_in, acc_grp, acc_out, carry, sems):
        i = pl.program_id(0)
        tile = nt - 1 - i

        @pl.when(i == 0)
        def _():
            _copy_all([(win_hbm, win_v), (wout_hbm, wout_v)] + _grp_pairs(wgrp_hbm, wgrp_v), sems)
            carry[...] = jnp.zeros_like(carry)
            dscale_ref[...] = jnp.zeros_like(dscale_ref)
            dgain_ref[...] = jnp.zeros_like(dgain_ref)
            acc_in[...] = jnp.zeros_like(acc_in)
            acc_grp[...] = jnp.zeros_like(acc_grp)
            acc_out[...] = jnp.zeros_like(acc_out)

        zf = z_ref[...].astype(F32)
        mxf = mx_ref[...].astype(F32)
        sig = _sigmoid(zf)
        sz = zf * sig
        mixed = mxf * scale_ref[...]
        dh1 = dh1_ref[...]
        dh1b = dh1.astype(BF16)
        acc_out[...] += _dot_tn((sz * mixed).astype(BF16), dh1b)
        dm = _dot_nt(dh1b, wout_v[...])
        dz = (dm * mixed) * (sig * (1.0 + zf * (1.0 - sig)))
        dmixed = dm * sz
        dscale_ref[...] += jnp.sum(dmixed * mxf, axis=0, keepdims=True)
        dmxb = (dmixed * scale_ref[...]).astype(BF16)
        diff = dd_ref[...]
        for g in range(N_GROUPS):
            cols = slice(g * GROUP_DIM, (g + 1) * GROUP_DIM)
            acc_grp[g] += _dot_tn(diff[:, cols], dmxb[:, cols])
        ddiff = jnp.concatenate(
            [_dot_nt(dmxb[:, g * GROUP_DIM:(g + 1) * GROUP_DIM], wgrp_v[g]) for g in range(N_GROUPS)], axis=1)
        dub = (_pool_bwd(ddiff, carry, tile, ts) - ddiff).astype(BF16)
        dzb = dz.astype(BF16)
        parts = [dub[:, 0:half], dub[:, half:e], dzb[:, 0:half], dzb[:, half:e]]
        xn, r = _rms(h_ref[...])
        hnb = (xn * gain_ref[...]).astype(BF16)
        for k in range(4):
            acc_in[k] += _dot_tn(hnb, parts[k])
        dhn = _dot_nt(parts[0], win_v[0]) + _dot_nt(parts[1], win_v[1]) + _dot_nt(parts[2], win_v[2]) + _dot_nt(parts[3], win_v[3])
        dh, dgain = _rms_bwd(dhn, xn, r, gain_ref[...])
        dh_ref[...] = dh1 + dh
        dgain_ref[...] += dgain

        @pl.when(i == nt - 1)
        def _():
            _copy_all([(acc_in, dwin_hbm), (acc_out, dwout_hbm)] + [(v, hb_) for hb_, v in _grp_pairs(dwgrp_hbm, acc_grp)], sems)

    row = lambda width: pl.BlockSpec((ts, width), lambda i: (nt - 1 - i, 0))
    return _call(
        body, name=name, grid=(nt,),
        in_specs=[row(d), row(d), row(e), row(e), row(e), _full((1, d)), _full((1, e)), ANY, ANY, ANY],
        out_specs=[row(d), _full((1, e)), _full((1, d)), ANY, ANY, ANY],
        out_shape=[_sds((s, d), F32), _sds((1, e), F32), _sds((1, d), F32), _sds((4, d, half), F32),
                   _sds((4, N_GROUPS, GROUP_DIM // 4, GROUP_DIM), F32), _sds((e, d), F32)],
        scratch_shapes=[pltpu.VMEM((4, d, half), BF16), pltpu.VMEM((N_GROUPS, GROUP_DIM, GROUP_DIM), BF16),
                        pltpu.VMEM((e, d), BF16), pltpu.VMEM((4, d, half), F32),
                        pltpu.VMEM((N_GROUPS, GROUP_DIM, GROUP_DIM), F32), pltpu.VMEM((e, d), F32),
                        pltpu.VMEM((4, HALO, e), F32), pltpu.SemaphoreType.DMA((18,))],
        operands=[dh1, h, z, mx, diff, gain, scale, w_in, w_grp, w_out], rider=rider)


def _first_gather(rider, small):
    shards = rider.inputs
    ni = len(shards)

    def body(*refs):
        rin, small_src = refs[:ni], refs[ni]
        rout, small_dst = refs[ni + 1:2 * ni + 1], refs[2 * ni + 1]
        send, recv, ssend, srecv = refs[2 * ni + 2:]
        x, y, c, chips = _place()
        me = 2 * x + y
        peers = [(cx, cy, c) for cx, cy in chips] + [(x, y, 1 - c)]
        vec = [_remote(small_src, small_dst.at[me], ssend, srecv, j, to) for j, to in enumerate(peers)]
        for cp in vec:
            cp.start()
        rider.start(rin, rout, send, recv)
        rider.middle(rin, rout, send, recv)
        rider.finish(rin, rout, send, recv)
        for j, (px, py, _) in enumerate(peers):
            _remote(small_src, small_dst.at[2 * px + py], ssend, srecv, j, peers[j]).wait_recv()
        for cp in vec:
            cp.wait_send()

    outs = pl.pallas_call(
        body, name="first_gather", in_specs=[ANY] * (ni + 1), out_specs=[ANY] * (ni + 1),
        out_shape=rider.out_shapes + [_sds((4,) + small.shape, small.dtype)],
        scratch_shapes=[pltpu.SemaphoreType.DMA((rider.n_sems,)), pltpu.SemaphoreType.DMA((rider.n_sems,)),
                        pltpu.SemaphoreType.DMA((4,)), pltpu.SemaphoreType.DMA((4,))],
    )(*shards, small)
    return list(outs[:ni]), outs[ni]


def _vector_rider(pack):
    flips = [(fx, fy, fc) for fx in (0, 1) for fy in (0, 1) for fc in (0, 1)][1:]

    def copies(rin, rout, send, recv, base):
        x, y, c, _ = _place()
        me = 4 * x + 2 * y + c
        peers = [(1 - x if fx else x, 1 - y if fy else y, 1 - c if fc else c) for fx, fy, fc in flips]
        own = pltpu.make_async_copy(rin[0], rout[0].at[me], send.at[base + 7])
        out = [_remote(rin[0], rout[0].at[me], send, recv, base + r, peer) for r, peer in enumerate(peers)]
        back = [_remote(rin[0], rout[0].at[4 * px + 2 * py + pc], send, recv, base + r, (px, py, pc))
                for r, (px, py, pc) in enumerate(peers)]
        return own, out, back

    def start(rin, rout, send, recv, base=0):
        own, out, _ = copies(rin, rout, send, recv, base)
        own.start()
        for cp in out:
            cp.start()

    def finish(rin, rout, send, recv, base=0):
        own, out, back = copies(rin, rout, send, recv, base)
        for cp in back:
            cp.wait_recv()
        for cp in out:
            cp.wait_send()
        own.wait()

    return _Rider([pack], [_sds((8,) + pack.shape, pack.dtype)], 8, start, finish)


def _vector_sum(landed, row_counts):
    _, rows, d = landed.shape

    def body(l_ref, *out_refs):
        first = 0
        for n, out_ref in zip(row_counts, out_refs):
            total = l_ref[0, first:first + n, :]
            for dev in range(1, 8):
                total = total + l_ref[dev, first:first + n, :]
            out_ref[...] = total
            first += n

    vmem = pl.BlockSpec(memory_space=pltpu.VMEM)
    return pl.pallas_call(body, name="vector_sum", in_specs=[vmem], out_specs=[vmem] * len(row_counts),
                          out_shape=[_sds((n, d), F32) for n in row_counts])(landed)


def _job_rows(rows, cols):
    return min(rows, max(8, JOB_BLOCK_BYTES // (4 * cols)))


def _pair_sum_job(grad, sibling_rows):
    _, _, rh, cols = grad.shape
    tr = _job_rows(rh, cols)
    nr = rh // tr

    def chip_of(j, pos):
        return jnp.bitwise_xor(pos[0], jnp.where(j == 2, 3, 2 - j))

    return dict(
        ins=[(grad, (None, None, tr, cols), lambda l, pos: (chip_of(l // nr, pos), pos[1], l % nr, 0)),
             (sibling_rows, (None, tr, cols), lambda l, pos: (chip_of(l // nr, pos), l % nr, 0))],
        outs=[((3, rh, cols), BF16, (None, tr, cols), lambda l, pos: (l // nr, l % nr, 0))],
        steps=3 * nr, fn=lambda g, sb: [(g + sb).astype(BF16)], alias=None)


def _final_sum_job(grad, sibling_rows, landed, stack, slot, n_slots):
    _, _, rh, cols = grad.shape
    tr = _job_rows(rh, cols)

    def fn(g, sb, ld):
        total = g + sb
        for j in range(3):
            total = total + ld[j].astype(F32)
        return [total]

    return dict(
        ins=[(grad, (None, None, tr, cols), lambda l, pos: (pos[0], pos[1], l, 0)),
             (sibling_rows, (None, tr, cols), lambda l, pos: (pos[0], l, 0)),
             (landed, (3, tr, cols), lambda l, pos: (0, l, 0))],
        outs=[((n_slots, 2, rh, cols), F32, (None, None, tr, cols), lambda l, pos: (slot, pos[1], l, 0))],
        steps=rh // tr, fn=fn, alias=stack)


def _adamw_job(g, w, m, v, block_bytes):
    rows, cols = g.shape
    tr = min(rows, max(8, block_bytes // (4 * cols)))

    def fn(gg, ww, mm, vv):
        nm = ADAM_B1 * mm + (1.0 - ADAM_B1) * gg
        nv = ADAM_B2 * vv + (1.0 - ADAM_B2) * (gg * gg)
        m_hat = nm / (1.0 - ADAM_B1 ** ADAM_STEP)
        v_hat = nv / (1.0 - ADAM_B2 ** ADAM_STEP)
        return [-ADAM_LR * (m_hat / (jnp.sqrt(v_hat) + ADAM_EPS) + ADAM_WD * ww), nm, nv, gg]

    block = lambda l, pos: (l, 0)
    return dict(ins=[(a, (tr, cols), block) for a in (g, w, m, v)],
                outs=[((rows, cols), F32, (tr, cols), block)] * 4, steps=rows // tr, fn=fn, alias=None)


def _run_jobs(jobs, place, name):
    starts, total = [], 0
    for jb in jobs:
        starts.append(total)
        total += jb["steps"]

    def clamped(fn, start, steps):
        return lambda s, pos: fn(jnp.clip(s - start, 0, steps - 1), pos)

    in_specs, operands = [], [place]
    for jb, start in zip(jobs, starts):
        for arr, block, fn in jb["ins"]:
            in_specs.append(pl.BlockSpec(block, clamped(fn, start, jb["steps"])))
            operands.append(arr)
    n_ins = len(in_specs)
    first_out, n_outs = [], 0
    for jb in jobs:
        first_out.append(n_outs)
        n_outs += len(jb["outs"])
    aliases = {}
    for t, jb in enumerate(jobs):
        if jb["alias"] is not None:
            in_specs.append(ANY)
            operands.append(jb["alias"])
            aliases[len(operands) - 1] = first_out[t]
    out_specs = [pl.BlockSpec(block, clamped(fn, start, jb["steps"]))
                 for jb, start in zip(jobs, starts) for _, _, block, fn in jb["outs"]]

    def body(place_ref, *refs):
        in_refs, out_refs = refs[:n_ins], refs[len(in_specs):]
        s = pl.program_id(0)
        first = 0
        for t, (jb, start) in enumerate(zip(jobs, starts)):
            mine = in_refs[first:first + len(jb["ins"])]
            first += len(jb["ins"])

            @pl.when((s >= start) & (s < start + jb["steps"]))
            def _(mine=mine, t=t, jb=jb):
                values = jb["fn"](*[r[...] for r in mine])
                for n, value in enumerate(values):
                    out_refs[first_out[t] + n][...] = value

    grid_spec = pltpu.PrefetchScalarGridSpec(num_scalar_prefetch=1, grid=(total,), in_specs=in_specs, out_specs=out_specs)
    outs = pl.pallas_call(body, name=name, grid_spec=grid_spec,
                          out_shape=[_sds(shape, dtype) for jb in jobs for shape, dtype, _, _ in jb["outs"]],
                          input_output_aliases=aliases, compiler_params=_params(1))(*operands)
    return [list(outs[first_out[t]:first_out[t] + len(jb["outs"])]) for t, jb in enumerate(jobs)]


BIG = ["a_w_in", "a_w_out", "b_w_in", "b_w_grp", "b_w_out", "ple_w_gate", "ple_w_proj"]

GATHER_PLAN = {
    "first": [("a_w_in", 0), ("a_w_out", 0)],
    "mix0": [("ple_w_gate", 0), ("ple_w_proj", 0), ("b_w_in", 0), ("b_w_grp", 0), ("b_w_out", 0)],
    "ple0": [("ple_w_gate", 1), ("ple_w_proj", 1), ("ple_w_proj", 2)],
    "mix1": [("a_w_in", 1), ("a_w_out", 1), ("ple_w_gate", 2)],
    "mix2": [("b_w_in", 1), ("b_w_grp", 1), ("b_w_out", 1), ("ple_w_gate", 3), ("ple_w_proj", 3)],
}
GATHER_LONGER_THAN_HOST = ("mix1",)


def _as_2d(name, a):
    if name == "b_w_grp":
        return a.reshape(a.shape[0], N_GROUPS * (GROUP_DIM // 4), GROUP_DIM)
    return a


def kernel(x, p, norm_mix, a_w_in, a_w_conv, a_w_out, b_w_in, b_w_grp, b_scale, b_w_out, ple_norm, ple_w_gate, ple_w_proj, final_norm, loss_target, m_norm_mix, m_a_w_in, m_a_w_conv, m_a_w_out, m_b_w_in, m_b_w_grp, m_b_scale, m_b_w_out, m_ple_norm, m_ple_w_gate, m_ple_w_proj, m_final_norm, v_norm_mix, v_a_w_in, v_a_w_conv, v_a_w_out, v_b_w_in, v_b_w_grp, v_b_scale, v_b_w_out, v_ple_norm, v_ple_w_gate, v_ple_w_proj, v_final_norm):
    d, e = D_MODEL, MIX_WIDTH
    s = x.shape[1]
    cx, cy, cc = lax.axis_index("x"), lax.axis_index("y"), lax.axis_index("c")
    chip = 2 * cx + cy
    place = jnp.stack([chip, cc]).astype(jnp.int32)

    weights = dict(a_w_in=a_w_in, a_w_out=a_w_out, b_w_in=b_w_in, b_w_grp=b_w_grp, b_w_out=b_w_out,
                   ple_w_gate=ple_w_gate, ple_w_proj=ple_w_proj)
    moms = dict(a_w_in=m_a_w_in, a_w_out=m_a_w_out, b_w_in=m_b_w_in, b_w_grp=m_b_w_grp, b_w_out=m_b_w_out,
                ple_w_gate=m_ple_w_gate, ple_w_proj=m_ple_w_proj)
    vars_ = dict(a_w_in=v_a_w_in, a_w_out=v_a_w_out, b_w_in=v_b_w_in, b_w_grp=v_b_w_grp, b_w_out=v_b_w_out,
                 ple_w_gate=v_ple_w_gate, ple_w_proj=v_ple_w_proj)
    w2d = {nm: _as_2d(nm, weights[nm]) for nm in BIG}
    bf = {nm: w2d[nm].astype(BF16).reshape(w2d[nm].shape[0], 2, w2d[nm].shape[1] // 2, w2d[nm].shape[2]) for nm in BIG}
    gathered = {}

    def gather_rider(host):
        keys = GATHER_PLAN.get(host)
        if not keys:
            return None
        rider = _gather_rider([bf[nm] for nm, _ in keys], [j for _, j in keys])
        rider.middle_at_end = host in GATHER_LONGER_THAN_HOST
        return rider

    def keep(host, landed):
        for k, a in zip(GATHER_PLAN.get(host, []), landed):
            gathered[k] = a

    def weight(nm, j):
        a = gathered[(nm, j)]
        shapes = {"a_w_in": (4, d, e), "a_w_out": (e, d), "b_w_in": (4, d, e // 2),
                  "b_w_grp": (4, N_GROUPS, GROUP_DIM // 4, GROUP_DIM), "b_w_out": (e, d), "ple_w_gate": (d, d),
                  "ple_w_proj": (4, PLE_DIM, d // 4)}
        return a.reshape(shapes[nm])

    pad = jnp.zeros((4, e // 4), F32)
    small = jnp.concatenate([a_w_conv[0], b_scale[0:1], pad, a_w_conv[1], b_scale[1:2], pad], axis=0)
    landed, small_full = _first_gather(gather_rider("first"), small)
    keep("first", landed)
    small_full = small_full.transpose(1, 0, 2).reshape(16, e)
    conv_w = [_Block(small_full.reshape(2, 8, e), j) for j in range(2)]
    scale_w = [_Block(small_full.reshape(16, 1, e), 8 * j + 3) for j in range(2)]

    p3 = p.reshape(DEPTH, s, PLE_DIM)
    mix_gain = [_Block(norm_mix.reshape(DEPTH, 1, d), i) for i in range(DEPTH)]
    ple_gain = [_Block(ple_norm.reshape(DEPTH, 1, d), i) for i in range(DEPTH)]

    h = x.reshape(s, d)
    saved = []
    for i in range(DEPTH):
        j = i // 2
        rider = gather_rider(f"mix{i}")
        ple = (p3, i, ple_gain[i], weight("ple_w_gate", i), weight("ple_w_proj", i)) if i > 0 else None
        if i % 2 == 0:
            outs, landed = _fwd_mix_a(h, mix_gain[i], conv_w[j], weight("a_w_in", j), weight("a_w_out", j),
                                      f"fwd_mix_a{j}", rider, ple)
            mix = dict(proj=outs[1])
        else:
            outs, landed = _fwd_mix_b(h, mix_gain[i], scale_w[j], weight("b_w_in", j), weight("b_w_grp", j),
                                      weight("b_w_out", j), f"fwd_mix_b{j}", rider, ple)
            mix = dict(z=outs[1], mx=outs[2], diff=outs[3])
        keep(f"mix{i}", landed)
        h1 = outs[0]
        if ple:
            h2, gate = outs[-2:]
        else:
            (h2, gate), landed = _fwd_ple(h1, p3, ple_gain[i], weight("ple_w_gate", i), weight("ple_w_proj", i), i,
                                          gather_rider(f"ple{i}"))
            keep(f"ple{i}", landed)
        saved.append(dict(h=h, h1=h1, gate=gate, **mix))
        h = h2

    n_slots = {nm: weights[nm].shape[0] for nm in BIG}
    stacks = {nm: None for nm in BIG}

    class Group:
        def __init__(self, keys, grads):
            self.keys, self.stage = keys, 0
            self.g32 = [g.reshape(4, 2, w2d[nm].shape[1] // 2, w2d[nm].shape[2]) for (nm, _), g in zip(keys, grads)]

        def rider(self):
            if self.stage == 0:
                return _pair_rider(self.g32)
            if self.stage == 1:
                return _ici_rider(self.pair_sums)
            return _final_rider([stacks[nm] for nm, _ in self.keys], [j for _, j in self.keys])

        def jobs_after(self, landed):
            if self.stage == 0:
                self.from_sibling = landed
                return [_pair_sum_job(g, sb) for g, sb in zip(self.g32, landed)]
            if self.stage == 1:
                return [_final_sum_job(g, sb, ld, stacks[nm], j, n_slots[nm])
                        for (nm, j), g, sb, ld in zip(self.keys, self.g32, self.from_sibling, landed)]
            return []

        def advance(self, landed, summed):
            if self.stage == 0:
                self.pair_sums = summed
            else:
                for (nm, _), a in zip(self.keys, summed if self.stage == 1 else landed):
                    stacks[nm] = a
            self.stage += 1

    active = []
    batches = [0]

    def riders_now():
        parts = [g.rider() for g in active]
        return parts, _merge(parts)

    def advance_all(parts, landed):
        groups = list(active)
        pieces = _split(landed, parts)
        jobs = [g.jobs_after(l) for g, l in zip(groups, pieces)]
        flat = sum(jobs, [])
        outs = [o[0] for o in _run_jobs(flat, place, f"reduce_sums{batches[0]}")] if flat else []
        batches[0] += 1
        for g, l, jb in zip(groups, pieces, jobs):
            g.advance(l, outs[:len(jb)])
            outs = outs[len(jb):]
            if g.stage == 3:
                active.remove(g)

    d_mix_gain, d_ple_gain = [None] * DEPTH, [None] * DEPTH
    d_conv, d_scale = [None] * 2, [None] * 2
    for i in reversed(range(DEPTH)):
        j = i // 2
        sv = saved[i]
        parts, rider = riders_now()
        if i == DEPTH - 1:
            (dh1, d_ple_gain[i], dwg, dwp, loss_part, d_final), landed = _bwd_ple(
                h, sv["h1"], sv["gate"], p3, ple_gain[i], weight("ple_w_gate", i), weight("ple_w_proj", i), i, rider,
                loss_head=(loss_target.reshape(s, d), final_norm.reshape(1, d)))
        else:
            (dh1, d_ple_gain[i], dwg, dwp), landed = _bwd_ple(
                dh, sv["h1"], sv["gate"], p3, ple_gain[i], weight("ple_w_gate", i), weight("ple_w_proj", i), i, rider)
        advance_all(parts, landed)
        active.append(Group([("ple_w_gate", i), ("ple_w_proj", i)], [dwg, dwp]))
        parts, rider = riders_now()
        if i == 0:
            (dproj, d_conv[0], dwin, dwout), landed = _bwd_mix_a_weights(
                dh1, sv["h"], sv["proj"], mix_gain[0], conv_w[0], weight("a_w_out", 0), "bwd_mix_a0_weights", rider)
            advance_all(parts, landed)
            active.append(Group([("a_w_in", 0), ("a_w_out", 0)], [dwin, dwout]))
            parts, rider = riders_now()
            advance_all(parts, _run_rider(rider, "pair_exchange0"))
            parts, rider = riders_now()
            (dh, d_mix_gain[0]), landed = _bwd_mix_a_input(dproj, sv["h"], dh1, mix_gain[0], weight("a_w_in", 0),
                                                          "bwd_mix_a0_input", rider)
            advance_all(parts, landed)
            continue
        if i % 2 == 0:
            (dh, d_conv[j], d_mix_gain[i], dwin, dwout), landed = _bwd_mix_a(
                dh1, sv["h"], sv["proj"], mix_gain[i], conv_w[j], weight("a_w_in", j), weight("a_w_out", j),
                f"bwd_mix_a{j}", rider)
            new = Group([("a_w_in", j), ("a_w_out", j)], [dwin, dwout])
        else:
            (dh, d_scale[j], d_mix_gain[i], dwin, dwgrp, dwout), landed = _bwd_mix_b(
                dh1, sv["h"], sv["z"], sv["mx"], sv["diff"], mix_gain[i], scale_w[j], weight("b_w_in", j),
                weight("b_w_grp", j), weight("b_w_out", j), f"bwd_mix_b{j}", rider)
            new = Group([("b_w_in", j), ("b_w_grp", j), ("b_w_out", j)], [dwin, dwgrp, dwout])
        advance_all(parts, landed)
        active.append(new)
    grad_x = dh.reshape(1, s, d)

    pieces = (d_mix_gain + d_ple_gain + [d_final, d_conv[0][0:3], d_conv[1][0:3]] + d_scale
              + [jnp.tile(loss_part[0:1], (1, d // 128))])
    used = sum(a.shape[0] for a in pieces)
    pack = jnp.concatenate(pieces + [jnp.zeros((-used % PACK_GROUP, d), F32)], axis=0)
    vectors = _vector_rider(pack)
    tail = 0
    while active:
        parts, _ = riders_now()
        extra = [vectors] if tail == 0 else []
        landed = _run_rider(_merge(parts + extra), f"tail_exchange{tail}")
        if extra:
            g_mix, g_ple, g_final, g_conv, g_scale, loss_row = _vector_sum(
                _split(landed, parts + extra)[-1][0], [DEPTH, DEPTH, 1, 6, 2, 1])
        advance_all(parts, landed)
        tail += 1
    loss = loss_row[0, 0]

    mine = lambda a: lax.dynamic_slice_in_dim(a, chip * (e // 4), e // 4, axis=1)
    row = lambda a: a.reshape(1, d)
    taps = lambda a: a.reshape(6, e // 4)
    flat = {nm: (w2d[nm].shape[0] * w2d[nm].shape[1], w2d[nm].shape[2]) for nm in BIG}
    tensors = {nm: (stacks[nm].reshape(flat[nm]), w2d[nm].reshape(flat[nm]), _as_2d(nm, moms[nm]).reshape(flat[nm]),
                    _as_2d(nm, vars_[nm]).reshape(flat[nm])) for nm in BIG}
    tensors.update(
        norm_mix=(g_mix, norm_mix, m_norm_mix, v_norm_mix), ple_norm=(g_ple, ple_norm, m_ple_norm, v_ple_norm),
        final_norm=(g_final, row(final_norm), row(m_final_norm), row(v_final_norm)),
        a_w_conv=(mine(g_conv), taps(a_w_conv), taps(m_a_w_conv), taps(v_a_w_conv)),
        b_scale=(mine(g_scale), b_scale, m_b_scale, v_b_scale))
    order = ["norm_mix", "a_w_in", "a_w_conv", "a_w_out", "b_w_in", "b_w_grp", "b_scale", "b_w_out", "ple_norm",
             "ple_w_gate", "ple_w_proj", "final_norm"]
    shapes = dict(norm_mix=norm_mix.shape, ple_norm=ple_norm.shape, final_norm=final_norm.shape,
                  a_w_conv=a_w_conv.shape, b_scale=b_scale.shape, **{nm: weights[nm].shape for nm in BIG})
    updates = {nm: _run_jobs([_adamw_job(*tensors[nm], ADAMW_BIG_BLOCK_BYTES)], place, f"adamw_{nm}")[0]
               for nm in ADAMW_ALONE}
    rest = [nm for nm in order if nm not in ADAMW_ALONE]
    updates.update(zip(rest, _run_jobs([_adamw_job(*tensors[nm], ADAMW_BLOCK_BYTES) for nm in rest], place, "adamw_rest")))
    outs = [loss, grad_x]
    for which in (3, 0, 1, 2):
        outs += [updates[nm][which].reshape(shapes[nm]) for nm in order]
    return tuple(outs)
```

```python
import jax
import jax.numpy as jnp
from jax import lax
from jax.experimental import pallas as pl
from jax.experimental.pallas import tpu as pltpu

F32 = jnp.float32
BF16 = jnp.bfloat16
MESH = pl.DeviceIdType.MESH

D_MODEL = 1024
MIX_WIDTH = 1024
PLE_DIM = 256
N_GROUPS = 4
GROUP_DIM = 256
POOL_WINDOWS = (2, 4, 8, 16)
DEPTH = 4
EPS = 1e-6

ADAM_LR = 0.001
ADAM_B1 = 0.9
ADAM_B2 = 0.999
ADAM_EPS = 1e-08
ADAM_WD = 0.01
ADAM_STEP = 10

HALO = 8
TS_MIX = 256
TS_BWD_POOL = 512
TS_FWD = 512
TS_PLE = 512
VMEM_LIMIT = 56 * 1024 * 1024
PACK_GROUP = 8
JOB_BLOCK_BYTES = 2 * 1024 * 1024
ADAMW_BLOCK_BYTES = 512 * 1024
ADAMW_BIG_BLOCK_BYTES = 2 * 1024 * 1024
ADAMW_ALONE = ("a_w_in", "b_w_in", "ple_w_gate")
MIDDLE_STEPS_BEFORE_END = 1

ANY = pl.BlockSpec(memory_space=pl.ANY)


def _sds(shape, dtype):
    return jax.ShapeDtypeStruct(shape, dtype)


def _full(shape):
    nd = len(shape)
    return pl.BlockSpec(shape, lambda *_: (0,) * nd)


def _params(n_axes=1):
    return pltpu.CompilerParams(dimension_semantics=("arbitrary",) * n_axes, vmem_limit_bytes=VMEM_LIMIT)


def _dot(a, b):
    return jnp.dot(a, b, preferred_element_type=F32)


def _dot_nt(a, b):
    return lax.dot_general(a, b, (((1,), (1,)), ((), ())), preferred_element_type=F32)


def _dot_tn(a, b):
    return lax.dot_general(a, b, (((0,), (0,)), ((), ())), preferred_element_type=F32)


def _sigmoid(z):
    return 1.0 / (1.0 + jnp.exp(-z))


def _shift_down(x, k, tail):
    rolled = pltpu.roll(x, k, 0)
    rt = tail if k % HALO == 0 else pltpu.roll(tail, k % HALO, 0)
    row = lax.broadcasted_iota(jnp.int32, rt.shape, 0)
    head = jnp.where(row < k, rt, rolled[0:HALO])
    return jnp.concatenate([head, rolled[HALO:]], axis=0)


def _shift_up(x, k, head_next):
    n = x.shape[0]
    rolled = pltpu.roll(x, n - k, 0)
    rh = head_next if k % HALO == 0 else pltpu.roll(head_next, HALO - k % HALO, 0)
    row = lax.broadcasted_iota(jnp.int32, rh.shape, 0)
    tail = jnp.where(row >= HALO - k, rh, rolled[n - HALO:n])
    return jnp.concatenate([rolled[:n - HALO], tail], axis=0)


def _inv_counts(tile, ts):
    t = tile * ts + lax.broadcasted_iota(jnp.int32, (ts, 1), 0)
    return [1.0 / jnp.minimum(t + 1, w).astype(F32) for w in POOL_WINDOWS]


def _pool_fwd(u, carry, tile, ts):
    inv = _inv_counts(tile, ts)
    outs = []
    for g, w in enumerate(POOL_WINDOWS):
        cols = slice(g * GROUP_DIM, (g + 1) * GROUP_DIM)
        s = u[:, cols]
        level, k = 0, 1
        while k < w:
            tail = carry[level, :, cols]
            carry[level, :, cols] = s[ts - HALO:ts]
            s = s + _shift_down(s, k, tail)
            level, k = level + 1, k * 2
        outs.append(s * inv[g])
    return jnp.concatenate(outs, axis=1)


def _pool_bwd(dd, carry, tile, ts):
    inv = _inv_counts(tile, ts)
    outs = []
    for g, w in enumerate(POOL_WINDOWS):
        cols = slice(g * GROUP_DIM, (g + 1) * GROUP_DIM)
        q = dd[:, cols] * inv[g]
        level, k = 0, 1
        while k < w:
            head = carry[level, :, cols]
            carry[level, :, cols] = q[0:HALO]
            q = q + _shift_up(q, k, head)
            level, k = level + 1, k * 2
        outs.append(q)
    return jnp.concatenate(outs, axis=1)


def _copy_all(pairs, sems):
    copies = [pltpu.make_async_copy(src, dst, sems.at[n]) for n, (src, dst) in enumerate(pairs)]
    for cp in copies:
        cp.start()
    for cp in copies:
        cp.wait()


def _grp_pairs(wgrp_hbm, wgrp_v):
    rows = GROUP_DIM // 4
    return [(wgrp_hbm.at[k, g], wgrp_v.at[g, pl.ds(k * rows, rows), :]) for k in range(4) for g in range(N_GROUPS)]


def _rms(h):
    r = lax.rsqrt(jnp.mean(h * h, axis=-1, keepdims=True) + EPS)
    return h * r, r


def _rms_bwd(dhn, xn, r, gain):
    dgain = jnp.sum(dhn * xn, axis=0, keepdims=True)
    dxn = dhn * gain
    dh = r * (dxn - xn * jnp.mean(dxn * xn, axis=-1, keepdims=True))
    return dh, dgain


class _Rider:
    def __init__(self, inputs, out_shapes, n_sems, start, finish, middle=None, aliases=None):
        self.inputs, self.out_shapes, self.n_sems = list(inputs), list(out_shapes), n_sems
        self.start, self.middle, self.finish = start, middle, finish
        self.aliases = dict(aliases or {})
        self.middle_at_end = False


def _merge(riders):
    riders = [r for r in riders if r is not None]
    if not riders:
        return None
    if len(riders) == 1:
        return riders[0]

    def phase(which):
        def run(rin, rout, send, recv, base=0):
            i0 = o0 = s0 = 0
            for r in riders:
                fn = getattr(r, which)
                if fn is not None:
                    fn(rin[i0:i0 + len(r.inputs)], rout[o0:o0 + len(r.out_shapes)], send, recv, base + s0)
                i0, o0, s0 = i0 + len(r.inputs), o0 + len(r.out_shapes), s0 + r.n_sems
        return run

    aliases, i0, o0 = {}, 0, 0
    for r in riders:
        aliases.update({i0 + a: o0 + b for a, b in r.aliases.items()})
        i0, o0 = i0 + len(r.inputs), o0 + len(r.out_shapes)
    return _Rider(sum([r.inputs for r in riders], []), sum([r.out_shapes for r in riders], []),
                  sum(r.n_sems for r in riders), phase("start"), phase("finish"),
                  phase("middle") if any(r.middle for r in riders) else None, aliases)


def _split(landed, riders):
    out, o0 = [], 0
    for r in riders:
        if r is None:
            out.append(None)
        else:
            out.append(landed[o0:o0 + len(r.out_shapes)])
            o0 += len(r.out_shapes)
    return out


def _place():
    x, y, c = lax.axis_index("x"), lax.axis_index("y"), lax.axis_index("c")
    chips = [(1 - x, y), (x, 1 - y), (1 - x, 1 - y)]
    return x, y, c, chips


def _remote(src, dst, send_sems, recv_sems, sem, to):
    return pltpu.make_async_remote_copy(src_ref=src, dst_ref=dst, send_sem=send_sems.at[sem], recv_sem=recv_sems.at[sem],
                                        device_id=to, device_id_type=MESH)


def _gather_rider(stacked, slots):
    ni = len(stacked)

    def first_hops(rin, rout, send, recv, base, x, y, c, chips):
        me = 2 * x + y
        return [_remote(rin[t].at[slots[t], c], rout[t].at[me, c], send, recv, base + 7 * t + j, (cx, cy, c))
                for j, (cx, cy) in enumerate(chips) for t in range(ni)]

    def passes(rout, send, recv, base, x, y, c, chips):
        out = []
        for j, (cx, cy) in enumerate(chips):
            for t in range(ni):
                landed = rout[t].at[2 * cx + cy, c]
                out.append((_remote(landed, landed, send, recv, base + 7 * t + j, (x, y, 1 - c)),
                            _remote(landed, landed, send, recv, base + 7 * t + 3 + j, (x, y, 1 - c))))
        return out

    def own(rin, rout, send, recv, base, x, y, c):
        return [_remote(rin[t].at[slots[t]], rout[t].at[2 * x + y], send, recv, base + 7 * t + 6, (x, y, 1 - c))
                for t in range(ni)]

    def start(rin, rout, send, recv, base=0):
        x, y, c, chips = _place()
        for cp in first_hops(rin, rout, send, recv, base, x, y, c, chips) + own(rin, rout, send, recv, base, x, y, c):
            cp.start()

    def middle(rin, rout, send, recv, base=0):
        x, y, c, chips = _place()
        for arrival, onward in passes(rout, send, recv, base, x, y, c, chips):
            arrival.wait_recv()
            onward.start()

    def finish(rin, rout, send, recv, base=0):
        x, y, c, chips = _place()
        for j, (cx, cy) in enumerate(chips):
            for t in range(ni):
                other = rout[t].at[2 * cx + cy, 1 - c]
                _remote(other, other, send, recv, base + 7 * t + 3 + j, (x, y, 1 - c)).wait_recv()
        for cp in own(rin, rout, send, recv, base, x, y, c):
            cp.wait_recv()
            cp.wait_send()
        for cp in first_hops(rin, rout, send, recv, base, x, y, c, chips):
            cp.wait_send()
        for _, onward in passes(rout, send, recv, base, x, y, c, chips):
            onward.wait_send()

    return _Rider(stacked, [_sds((4,) + a.shape[1:], a.dtype) for a in stacked], 7 * ni, start, finish, middle)


def _pair_rider(grads):
    ni = len(grads)

    def copies(rin, rout, send, recv, base):
        x, y, c, _ = _place()
        return [_remote(rin[t].at[:, 1 - c], rout[t], send, recv, base + t, (x, y, 1 - c)) for t in range(ni)]

    def start(rin, rout, send, recv, base=0):
        for cp in copies(rin, rout, send, recv, base):
            cp.start()

    def finish(rin, rout, send, recv, base=0):
        for cp in copies(rin, rout, send, recv, base):
            cp.wait()

    return _Rider(grads, [_sds(g.shape[:1] + g.shape[2:], g.dtype) for g in grads], ni, start, finish)


def _ici_rider(pair_sums):
    ni = len(pair_sums)

    def copies(rin, rout, send, recv, base):
        x, y, c, chips = _place()
        return [_remote(rin[t].at[j], rout[t].at[j], send, recv, base + 3 * t + j, (cx, cy, c))
                for j, (cx, cy) in enumerate(chips) for t in range(ni)]

    def start(rin, rout, send, recv, base=0):
        for cp in copies(rin, rout, send, recv, base):
            cp.start()

    def finish(rin, rout, send, recv, base=0):
        for cp in copies(rin, rout, send, recv, base):
            cp.wait()

    return _Rider(pair_sums, [_sds((3,) + g.shape[1:], g.dtype) for g in pair_sums], 3 * ni, start, finish)


def _final_rider(summed, slots):
    ni = len(summed)

    def copies(rout, send, recv, base):
        x, y, c, _ = _place()
        return [(_remote(rout[t].at[slots[t], c], rout[t].at[slots[t], c], send, recv, base + t, (x, y, 1 - c)),
                 _remote(rout[t].at[slots[t], 1 - c], rout[t].at[slots[t], 1 - c], send, recv, base + t, (x, y, 1 - c)))
                for t in range(ni)]

    def start(rin, rout, send, recv, base=0):
        for mine, _ in copies(rout, send, recv, base):
            mine.start()

    def finish(rin, rout, send, recv, base=0):
        for mine, theirs in copies(rout, send, recv, base):
            mine.wait_send()
            theirs.wait_recv()

    return _Rider(summed, [_sds(a.shape, a.dtype) for a in summed], ni, start, finish,
                  aliases={t: t for t in range(ni)})


class _Block:
    def __init__(self, array, index):
        self.array, self.index = array, index

    def spec(self):
        index = self.index
        return pl.BlockSpec((None,) + self.array.shape[1:], lambda *_: (index, 0, 0))


def _call(body, *, name, grid, in_specs, out_specs, out_shape, scratch_shapes, operands, rider=None):
    operands, in_specs = list(operands), list(in_specs)
    for n, op in enumerate(operands):
        if isinstance(op, _Block):
            operands[n], in_specs[n] = op.array, op.spec()
    if rider is None:
        outs = pl.pallas_call(body, name=name, grid=grid, in_specs=in_specs, out_specs=out_specs, out_shape=out_shape,
                              scratch_shapes=scratch_shapes, compiler_params=_params(len(grid)))(*operands)
        return list(outs), []
    n_in, n_out, n_scr = len(in_specs), len(out_specs), len(scratch_shapes)
    r_in, r_out = len(rider.inputs), len(rider.out_shapes)
    steps = 1
    for g in grid:
        steps *= g
    mid = steps - 1 if rider.middle_at_end else max(steps - 1 - MIDDLE_STEPS_BEFORE_END, 0)

    def full_body(*refs):
        own_in, rin = refs[:n_in], refs[n_in:n_in + r_in]
        own_out = refs[n_in + r_in:n_in + r_in + n_out]
        rout = refs[n_in + r_in + n_out:n_in + r_in + n_out + r_out]
        own_scr = refs[n_in + r_in + n_out + r_out:n_in + r_in + n_out + r_out + n_scr]
        send, recv = refs[-2], refs[-1]
        step = pl.program_id(0)
        for axis in range(1, len(grid)):
            step = step * grid[axis] + pl.program_id(axis)

        @pl.when(step == 0)
        def _():
            rider.start(rin, rout, send, recv)

        body(*own_in, *own_out, *own_scr)

        if rider.middle is not None:
            @pl.when(step == mid)
            def _():
                rider.middle(rin, rout, send, recv)

        @pl.when(step == steps - 1)
        def _():
            rider.finish(rin, rout, send, recv)

    outs = pl.pallas_call(
        full_body, name=name, grid=grid,
        in_specs=list(in_specs) + [ANY] * r_in, out_specs=list(out_specs) + [ANY] * r_out,
        out_shape=list(out_shape) + rider.out_shapes,
        scratch_shapes=list(scratch_shapes) + [pltpu.SemaphoreType.DMA((rider.n_sems,)), pltpu.SemaphoreType.DMA((rider.n_sems,))],
        input_output_aliases={n_in + a: n_out + b for a, b in rider.aliases.items()},
        compiler_params=_params(len(grid)),
    )(*operands, *rider.inputs)
    return list(outs[:n_out]), list(outs[n_out:])


def _run_rider(rider, name):
    r_in, r_out = len(rider.inputs), len(rider.out_shapes)

    def body(*refs):
        rin, rout, send, recv = refs[:r_in], refs[r_in:r_in + r_out], refs[-2], refs[-1]
        rider.start(rin, rout, send, recv)
        if rider.middle is not None:
            rider.middle(rin, rout, send, recv)
        rider.finish(rin, rout, send, recv)

    outs = pl.pallas_call(
        body, name=name, in_specs=[ANY] * r_in, out_specs=[ANY] * r_out, out_shape=rider.out_shapes,
        scratch_shapes=[pltpu.SemaphoreType.DMA((rider.n_sems,)), pltpu.SemaphoreType.DMA((rider.n_sems,))],
        input_output_aliases=rider.aliases,
    )(*rider.inputs)
    return list(outs)


def _ple_tile(h1, p_ref, gain_ref, wg_v, wp_v):
    xn, _ = _rms(h1)
    hpb = (xn * gain_ref[...]).astype(BF16)
    gate = _sigmoid(_dot(hpb, wg_v[...]))
    pb = p_ref[...].astype(BF16)
    pe = jnp.concatenate([_dot(pb, wp_v[k]) for k in range(4)], axis=1)
    return h1 + gate * pe, gate


def _ple_parts(ple, ts, d):
    p, layer, gain, w_gate, w_proj = ple
    s, pd = p.shape[1:]
    row = pl.BlockSpec((ts, d), lambda i: (i, 0))
    return dict(
        operands=[p, gain, w_gate, w_proj],
        in_specs=[pl.BlockSpec((None, ts, pd), lambda i: (layer, i, 0)), _full((1, d)), ANY, ANY],
        out_specs=[row, row], out_shape=[_sds((s, d), F32), _sds((s, d), BF16)],
        scratch=[pltpu.VMEM((d, d), BF16), pltpu.VMEM((4, pd, d // 4), BF16)])


def _fwd_mix_a(h, gain, conv_w, w_in, w_out, name, rider=None, ple=None):
    s, d = h.shape
    e = MIX_WIDTH
    ts = min(TS_FWD, s)
    nt = s // ts
    extra = _ple_parts(ple, ts, d) if ple else None

    def body(*refs):
        h_ref, gain_ref, cw_ref, win_hbm, wout_hbm = refs[:5]
        n_in = 9 if ple else 5
        h1_ref, proj_ref = refs[n_in:n_in + 2]
        win_v, wout_v, carry, sems = refs[n_in + (4 if ple else 2):][:4]
        i = pl.program_id(0)

        @pl.when(i == 0)
        def _():
            loads = [(win_hbm, win_v), (wout_hbm, wout_v)]
            if ple:
                loads += [(refs[7], refs[-2]), (refs[8], refs[-1])]
            _copy_all(loads, sems)
            carry[...] = jnp.zeros_like(carry)

        hh = h_ref[...]
        xn, _ = _rms(hh)
        hnb = (xn * gain_ref[...]).astype(BF16)
        b = _dot(hnb, win_v[0])
        c = _dot(hnb, win_v[1])
        v = _dot(hnb, win_v[2])
        z = _dot(hnb, win_v[3])
        proj_ref[:, 0 * e:1 * e] = b.astype(BF16)
        proj_ref[:, 1 * e:2 * e] = c.astype(BF16)
        proj_ref[:, 2 * e:3 * e] = v.astype(BF16)
        proj_ref[:, 3 * e:4 * e] = z.astype(BF16)
        cv = c * v
        tail = carry[...]
        carry[...] = cv[ts - HALO:ts]
        conv = cw_ref[0:1, :] * _shift_down(cv, 2, tail) + cw_ref[1:2, :] * _shift_down(cv, 1, tail) + cw_ref[2:3, :] * cv
        mb = ((z * _sigmoid(z)) * (b * conv)).astype(BF16)
        h1 = hh + _dot(mb, wout_v[...])
        h1_ref[...] = h1
        if ple:
            h2, gate = _ple_tile(h1, refs[5], refs[6], refs[-2], refs[-1])
            refs[n_in + 2][...] = h2
            refs[n_in + 3][...] = gate.astype(BF16)

    row = lambda width: pl.BlockSpec((ts, width), lambda i: (i, 0))
    return _call(
        body, name=name, grid=(nt,),
        in_specs=[row(d), _full((1, d)), _full((8, e)), ANY, ANY] + (extra["in_specs"] if ple else []),
        out_specs=[row(d), row(4 * e)] + (extra["out_specs"] if ple else []),
        out_shape=[_sds((s, d), F32), _sds((s, 4 * e), BF16)] + (extra["out_shape"] if ple else []),
        scratch_shapes=[pltpu.VMEM((4, d, e), BF16), pltpu.VMEM((e, d), BF16), pltpu.VMEM((HALO, e), F32),
                        pltpu.SemaphoreType.DMA((4,))] + (extra["scratch"] if ple else []),
        operands=[h, gain, conv_w, w_in, w_out] + (extra["operands"] if ple else []), rider=rider)


def _fwd_mix_b(h, gain, scale, w_in, w_grp, w_out, name, rider=None, ple=None):
    s, d = h.shape
    e = MIX_WIDTH
    ts = min(TS_FWD, s)
    nt = s // ts
    extra = _ple_parts(ple, ts, d) if ple else None

    def body(*refs):
        h_ref, gain_ref, scale_ref, win_hbm, wgrp_hbm, wout_hbm = refs[:6]
        n_in = 10 if ple else 6
        h1_ref, z_ref, mx_ref, dd_ref = refs[n_in:n_in + 4]
        win_v, wgrp_v, wout_v, carry, sems = refs[n_in + (6 if ple else 4):][:5]
        i = pl.program_id(0)

        @pl.when(i == 0)
        def _():
            loads = [(win_hbm, win_v), (wout_hbm, wout_v)] + _grp_pairs(wgrp_hbm, wgrp_v)
            if ple:
                loads += [(refs[8], refs[-2]), (refs[9], refs[-1])]
            _copy_all(loads, sems)
            carry[...] = jnp.zeros_like(carry)

        hh = h_ref[...]
        xn, _ = _rms(hh)
        hnb = (xn * gain_ref[...]).astype(BF16)
        u = jnp.concatenate([_dot(hnb, win_v[0]), _dot(hnb, win_v[1])], axis=1)
        z = jnp.concatenate([_dot(hnb, win_v[2]), _dot(hnb, win_v[3])], axis=1)
        z_ref[...] = z.astype(BF16)
        diff = (_pool_fwd(u, carry, i, ts) - u).astype(BF16)
        dd_ref[...] = diff
        mx = jnp.concatenate(
            [_dot(diff[:, g * GROUP_DIM:(g + 1) * GROUP_DIM], wgrp_v[g]) for g in range(N_GROUPS)], axis=1)
        mx_ref[...] = mx.astype(BF16)
        mb = ((z * _sigmoid(z)) * (mx * scale_ref[...])).astype(BF16)
        h1 = hh + _dot(mb, wout_v[...])
        h1_ref[...] = h1
        if ple:
            h2, gate = _ple_tile(h1, refs[6], refs[7], refs[-2], refs[-1])
            refs[n_in + 4][...] = h2
            refs[n_in + 5][...] = gate.astype(BF16)

    row = lambda width: pl.BlockSpec((ts, width), lambda i: (i, 0))
    return _call(
        body, name=name, grid=(nt,),
        in_specs=[row(d), _full((1, d)), _full((1, e)), ANY, ANY, ANY] + (extra["in_specs"] if ple else []),
        out_specs=[row(d), row(e), row(e), row(e)] + (extra["out_specs"] if ple else []),
        out_shape=[_sds((s, d), F32)] + [_sds((s, e), BF16)] * 3 + (extra["out_shape"] if ple else []),
        scratch_shapes=[pltpu.VMEM((4, d, e // 2), BF16), pltpu.VMEM((N_GROUPS, GROUP_DIM, GROUP_DIM), BF16),
                        pltpu.VMEM((e, d), BF16), pltpu.VMEM((4, HALO, e), F32), pltpu.SemaphoreType.DMA((20,))]
        + (extra["scratch"] if ple else []),
        operands=[h, gain, scale, w_in, w_grp, w_out] + (extra["operands"] if ple else []), rider=rider)


def _fwd_ple(h1, p, gain, w_gate, w_proj, layer, rider=None):
    s, d = h1.shape
    pd = p.shape[-1]
    ts = min(TS_PLE, s)
    nt = s // ts

    def body(h1_ref, p_ref, gain_ref, wg_hbm, wp_hbm, h2_ref, gate_ref, wg_v, wp_v, sems):
        @pl.when(pl.program_id(0) == 0)
        def _():
            _copy_all([(wg_hbm, wg_v), (wp_hbm, wp_v)], sems)

        hh = h1_ref[...]
        xn, _ = _rms(hh)
        hpb = (xn * gain_ref[...]).astype(BF16)
        gate = _sigmoid(_dot(hpb, wg_v[...]))
        pb = p_ref[...].astype(BF16)
        pe = jnp.concatenate([_dot(pb, wp_v[k]) for k in range(4)], axis=1)
        gate_ref[...] = gate.astype(BF16)
        h2_ref[...] = hh + gate * pe

    row = lambda width: pl.BlockSpec((ts, width), lambda i: (i, 0))
    return _call(
        body, name=f"fwd_ple{layer}", grid=(nt,),
        in_specs=[row(d), pl.BlockSpec((None, ts, pd), lambda i: (layer, i, 0)), _full((1, d)), ANY, ANY],
        out_specs=[row(d), row(d)],
        out_shape=[_sds((s, d), F32), _sds((s, d), BF16)],
        scratch_shapes=[pltpu.VMEM((d, d), BF16), pltpu.VMEM((4, pd, d // 4), BF16), pltpu.SemaphoreType.DMA((2,))],
        operands=[h1, p, gain, w_gate, w_proj], rider=rider)


def _bwd_ple(dh2, h1, gate, p, gain, w_gate, w_proj, layer, rider=None, loss_head=None):
    s, d = dh2.shape
    pd = p.shape[-1]
    ts = min(TS_PLE, s)
    nt = s // ts
    qd = d // 4
    n_head = 0 if loss_head is None else 2

    def body(*refs):
        dh2_ref = refs[0]
        h1_ref, gate_ref, p_ref, gain_ref, wg_hbm, wp_hbm, dh1_ref, dgain_ref, dwg_hbm, dwp_hbm = refs[1 + n_head:11 + n_head]
        wg_v, wp_v, acc_g, acc_p, sems = refs[-5:]
        i = pl.program_id(0)

        @pl.when(i == 0)
        def _():
            _copy_all([(wg_hbm, wg_v), (wp_hbm, wp_v)], sems)
            dgain_ref[...] = jnp.zeros_like(dgain_ref)
            acc_g[...] = jnp.zeros_like(acc_g)
            acc_p[...] = jnp.zeros_like(acc_p)

        if loss_head is None:
            g2 = dh2_ref[...]
        else:
            t_ref, fgain_ref, loss_ref, dfgain_ref = refs[1], refs[2], refs[11 + n_head], refs[12 + n_head]

            @pl.when(i == 0)
            def _():
                loss_ref[...] = jnp.zeros_like(loss_ref)
                dfgain_ref[...] = jnp.zeros_like(dfgain_ref)

            xf, rf = _rms(dh2_ref[...])
            err = xf * fgain_ref[...] - t_ref[...]
            part = 0.5 * jnp.sum(jnp.mean(err * err, axis=-1, keepdims=True), axis=0, keepdims=True)
            loss_ref[...] += jnp.broadcast_to(part, loss_ref.shape)
            g2, dfgain = _rms_bwd(err * (1.0 / d), xf, rf, fgain_ref[...])
            dfgain_ref[...] += dfgain
        gate_f = gate_ref[...].astype(F32)
        xn, r = _rms(h1_ref[...])
        hpb = (xn * gain_ref[...]).astype(BF16)
        pb = p_ref[...].astype(BF16)
        pe = jnp.concatenate([_dot(pb, wp_v[k]) for k in range(4)], axis=1)
        dpeb = (g2 * gate_f).astype(BF16)
        dab = ((g2 * pe) * (gate_f * (1.0 - gate_f))).astype(BF16)
        acc_g[...] += _dot_tn(hpb, dab)
        for k in range(4):
            acc_p[k] += _dot_tn(pb, dpeb[:, k * qd:(k + 1) * qd])
        dhp = _dot_nt(dab, wg_v[...])
        dh, dgain = _rms_bwd(dhp, xn, r, gain_ref[...])
        dh1_ref[...] = g2 + dh
        dgain_ref[...] += dgain

        @pl.when(i == nt - 1)
        def _():
            _copy_all([(acc_g, dwg_hbm), (acc_p, dwp_hbm)], sems)

    row = pl.BlockSpec((ts, d), lambda i: (i, 0))
    head = loss_head is not None
    return _call(
        body, name=f"bwd_ple{layer}", grid=(nt,),
        in_specs=[row] + ([row, _full((1, d))] if head else [])
        + [row, row, pl.BlockSpec((None, ts, pd), lambda i: (layer, i, 0)), _full((1, d)), ANY, ANY],
        out_specs=[row, _full((1, d)), ANY, ANY] + ([_full((8, 128)), _full((1, d))] if head else []),
        out_shape=[_sds((s, d), F32), _sds((1, d), F32), _sds((d, d), F32), _sds((4, pd, qd), F32)]
        + ([_sds((8, 128), F32), _sds((1, d), F32)] if head else []),
        scratch_shapes=[pltpu.VMEM((d, d), BF16), pltpu.VMEM((4, pd, qd), BF16), pltpu.VMEM((d, d), F32),
                        pltpu.VMEM((4, pd, qd), F32), pltpu.SemaphoreType.DMA((2,))],
        operands=[dh2] + (list(loss_head) if head else []) + [h1, gate, p, gain, w_gate, w_proj], rider=rider)


def _mix_a_tile_grads(proj_ref, ch_ref, vh_ref, cw_ref, dh1b, wout_v, carry, dcw_ref, tile, hb):
    e = MIX_WIDTH
    b = proj_ref[:, 0 * e:1 * e].astype(F32)
    c = proj_ref[:, 1 * e:2 * e].astype(F32)
    v = proj_ref[:, 2 * e:3 * e].astype(F32)
    z = proj_ref[:, 3 * e:4 * e].astype(F32)
    cv = c * v
    prev = (ch_ref[...].astype(F32) * vh_ref[...].astype(F32))[hb - HALO:hb]
    tail = jnp.where(tile > 0, prev, jnp.zeros_like(prev))
    cv1 = _shift_down(cv, 1, tail)
    cv2 = _shift_down(cv, 2, tail)
    conv = cw_ref[0:1, :] * cv2 + cw_ref[1:2, :] * cv1 + cw_ref[2:3, :] * cv
    sig = _sigmoid(z)
    sz = z * sig
    y = b * conv
    dm = _dot_nt(dh1b, wout_v[...])
    dz = (dm * y) * (sig * (1.0 + z * (1.0 - sig)))
    dy = dm * sz
    db = dy * conv
    dconv = dy * b
    head = carry[...]
    carry[...] = dconv[0:HALO]
    dcv = cw_ref[2:3, :] * dconv + cw_ref[1:2, :] * _shift_up(dconv, 1, head) + cw_ref[0:1, :] * _shift_up(dconv, 2, head)
    dcw_ref[0:1, :] += jnp.sum(dconv * cv2, axis=0, keepdims=True)
    dcw_ref[1:2, :] += jnp.sum(dconv * cv1, axis=0, keepdims=True)
    dcw_ref[2:3, :] += jnp.sum(dconv * cv, axis=0, keepdims=True)
    parts = [db.astype(BF16), (dcv * v).astype(BF16), (dcv * c).astype(BF16), dz.astype(BF16)]
    return parts, (sz * y).astype(BF16)


def _bwd_mix_a(dh1, h, proj, gain, conv_w, w_in, w_out, name, rider=None):
    s, d = dh1.shape
    e = MIX_WIDTH
    ts = min(TS_MIX, s)
    nt = s // ts
    hb = 16
    per = ts // hb

    def body(dh1_ref, h_ref, proj_ref, ch_ref, vh_ref, gain_ref, cw_ref, win_hbm, wout_hbm,
             dh_ref, dcw_ref, dgain_ref, dwin_hbm, dwout_hbm, win_v, wout_v, acc_in, acc_out, carry, sems):
        i = pl.program_id(0)

        @pl.when(i == 0)
        def _():
            _copy_all([(win_hbm, win_v), (wout_hbm, wout_v)], sems)
            carry[...] = jnp.zeros_like(carry)
            dcw_ref[...] = jnp.zeros_like(dcw_ref)
            dgain_ref[...] = jnp.zeros_like(dgain_ref)
            acc_in[...] = jnp.zeros_like(acc_in)
            acc_out[...] = jnp.zeros_like(acc_out)

        dh1 = dh1_ref[...]
        dh1b = dh1.astype(BF16)
        parts, mb = _mix_a_tile_grads(proj_ref, ch_ref, vh_ref, cw_ref, dh1b, wout_v, carry, dcw_ref, nt - 1 - i, hb)
        acc_out[...] += _dot_tn(mb, dh1b)
        xn, r = _rms(h_ref[...])
        hnb = (xn * gain_ref[...]).astype(BF16)
        for q in range(4):
            acc_in[q] += _dot_tn(hnb, parts[q])
        dhn = _dot_nt(parts[0], win_v[0]) + _dot_nt(parts[1], win_v[1]) + _dot_nt(parts[2], win_v[2]) + _dot_nt(parts[3], win_v[3])
        dh, dgain = _rms_bwd(dhn, xn, r, gain_ref[...])
        dh_ref[...] = dh1 + dh
        dgain_ref[...] += dgain

        @pl.when(i == nt - 1)
        def _():
            _copy_all([(acc_in, dwin_hbm), (acc_out, dwout_hbm)], sems)

    row = lambda width: pl.BlockSpec((ts, width), lambda i: (nt - 1 - i, 0))
    halo = lambda col: pl.BlockSpec((hb, e), lambda i: (jnp.maximum((nt - 1 - i) * per - 1, 0), col))
    return _call(
        body, name=name, grid=(nt,),
        in_specs=[row(d), row(d), row(4 * e), halo(1), halo(2), _full((1, d)), _full((8, e)), ANY, ANY],
        out_specs=[row(d), _full((8, e)), _full((1, d)), ANY, ANY],
        out_shape=[_sds((s, d), F32), _sds((8, e), F32), _sds((1, d), F32), _sds((4, d, e), F32), _sds((e, d), F32)],
        scratch_shapes=[pltpu.VMEM((4, d, e), BF16), pltpu.VMEM((e, d), BF16), pltpu.VMEM((4, d, e), F32),
                        pltpu.VMEM((e, d), F32), pltpu.VMEM((HALO, e), F32), pltpu.SemaphoreType.DMA((2,))],
        operands=[dh1, h, proj, proj, proj, gain, conv_w, w_in, w_out], rider=rider)


def _bwd_mix_a_weights(dh1, h, proj, gain, conv_w, w_out, name, rider=None):
    s, d = dh1.shape
    e = MIX_WIDTH
    ts = min(TS_MIX, s)
    nt = s // ts
    hb = 16
    per = ts // hb

    def body(dh1_ref, h_ref, proj_ref, ch_ref, vh_ref, gain_ref, cw_ref, wout_hbm,
             dproj_ref, dcw_ref, dwin_hbm, dwout_hbm, wout_v, acc_in, acc_out, carry, sems):
        i = pl.program_id(0)

        @pl.when(i == 0)
        def _():
            _copy_all([(wout_hbm, wout_v)], sems)
            carry[...] = jnp.zeros_like(carry)
            dcw_ref[...] = jnp.zeros_like(dcw_ref)
            acc_in[...] = jnp.zeros_like(acc_in)
            acc_out[...] = jnp.zeros_like(acc_out)

        dh1b = dh1_ref[...].astype(BF16)
        parts, mb = _mix_a_tile_grads(proj_ref, ch_ref, vh_ref, cw_ref, dh1b, wout_v, carry, dcw_ref, nt - 1 - i, hb)
        acc_out[...] += _dot_tn(mb, dh1b)
        xn, _ = _rms(h_ref[...])
        hnb = (xn * gain_ref[...]).astype(BF16)
        for q in range(4):
            acc_in[q] += _dot_tn(hnb, parts[q])
            dproj_ref[:, q * e:(q + 1) * e] = parts[q]

        @pl.when(i == nt - 1)
        def _():
            _copy_all([(acc_in, dwin_hbm), (acc_out, dwout_hbm)], sems)

    row = lambda width: pl.BlockSpec((ts, width), lambda i: (nt - 1 - i, 0))
    halo = lambda col: pl.BlockSpec((hb, e), lambda i: (jnp.maximum((nt - 1 - i) * per - 1, 0), col))
    return _call(
        body, name=name, grid=(nt,),
        in_specs=[row(d), row(d), row(4 * e), halo(1), halo(2), _full((1, d)), _full((8, e)), ANY],
        out_specs=[row(4 * e), _full((8, e)), ANY, ANY],
        out_shape=[_sds((s, 4 * e), BF16), _sds((8, e), F32), _sds((4, d, e), F32), _sds((e, d), F32)],
        scratch_shapes=[pltpu.VMEM((e, d), BF16), pltpu.VMEM((4, d, e), F32), pltpu.VMEM((e, d), F32),
                        pltpu.VMEM((HALO, e), F32), pltpu.SemaphoreType.DMA((2,))],
        operands=[dh1, h, proj, proj, proj, gain, conv_w, w_out], rider=rider)


def _bwd_mix_a_input(dproj, h, dh1, gain, w_in, name, rider=None):
    s, d = dh1.shape
    e = MIX_WIDTH
    ts = min(TS_PLE, s)
    nt = s // ts

    def body(dproj_ref, h_ref, dh1_ref, gain_ref, win_hbm, dh_ref, dgain_ref, win_v, sems):
        @pl.when(pl.program_id(0) == 0)
        def _():
            _copy_all([(win_hbm, win_v)], sems)
            dgain_ref[...] = jnp.zeros_like(dgain_ref)

        dhn = _dot_nt(dproj_ref[:, 0:e], win_v[0])
        for q in range(1, 4):
            dhn = dhn + _dot_nt(dproj_ref[:, q * e:(q + 1) * e], win_v[q])
        xn, r = _rms(h_ref[...])
        dh, dgain = _rms_bwd(dhn, xn, r, gain_ref[...])
        dh_ref[...] = dh1_ref[...] + dh
        dgain_ref[...] += dgain

    row = lambda width: pl.BlockSpec((ts, width), lambda i: (i, 0))
    return _call(
        body, name=name, grid=(nt,),
        in_specs=[row(4 * e), row(d), row(d), _full((1, d)), ANY],
        out_specs=[row(d), _full((1, d))],
        out_shape=[_sds((s, d), F32), _sds((1, d), F32)],
        scratch_shapes=[pltpu.VMEM((4, d, e), BF16), pltpu.SemaphoreType.DMA((1,))],
        operands=[dproj, h, dh1, gain, w_in], rider=rider)


def _bwd_mix_b(dh1, h, z, mx, diff, gain, scale, w_in, w_grp, w_out, name, rider=None):
    s, d = dh1.shape
    e = MIX_WIDTH
    ts = min(TS_BWD_POOL, s)
    nt = s // ts
    half = e // 2

    def body(dh1_ref, h_ref, z_ref, mx_ref, dd_ref, gain_ref, scale_ref, win_hbm, wgrp_hbm, wout_hbm,
             dh_ref, dscale_ref, dgain_ref, dwin_hbm, dwgrp_hbm, dwout_hbm,
             win_v, wgrp_v, wout_v, acc_in, acc_grp, acc_out, carry, sems):
        i = pl.program_id(0)
        tile = nt - 1 - i

        @pl.when(i == 0)
        def _():
            _copy_all([(win_hbm, win_v), (wout_hbm, wout_v)] + _grp_pairs(wgrp_hbm, wgrp_v), sems)
            carry[...] = jnp.zeros_like(carry)
            dscale_ref[...] = jnp.zeros_like(dscale_ref)
            dgain_ref[...] = jnp.zeros_like(dgain_ref)
            acc_in[...] = jnp.zeros_like(acc_in)
            acc_grp[...] = jnp.zeros_like(acc_grp)
            acc_out[...] = jnp.zeros_like(acc_out)

        zf = z_ref[...].astype(F32)
        mxf = mx_ref[...].astype(F32)
        sig = _sigmoid(zf)
        sz = zf * sig
        mixed = mxf * scale_ref[...]
        dh1 = dh1_ref[...]
        dh1b = dh1.astype(BF16)
        acc_out[...] += _dot_tn((sz * mixed).astype(BF16), dh1b)
        dm = _dot_nt(dh1b, wout_v[...])
        dz = (dm * mixed) * (sig * (1.0 + zf * (1.0 - sig)))
        dmixed = dm * sz
        dscale_ref[...] += jnp.sum(dmixed * mxf, axis=0, keepdims=True)
        dmxb = (dmixed * scale_ref[...]).astype(BF16)
        diff = dd_ref[...]
        for g in range(N_GROUPS):
            cols = slice(g * GROUP_DIM, (g + 1) * GROUP_DIM)
            acc_grp[g] += _dot_tn(diff[:, cols], dmxb[:, cols])
        ddiff = jnp.concatenate(
            [_dot_nt(dmxb[:, g * GROUP_DIM:(g + 1) * GROUP_DIM], wgrp_v[g]) for g in range(N_GROUPS)], axis=1)
        dub = (_pool_bwd(ddiff, carry, tile, ts) - ddiff).astype(BF16)
        dzb = dz.astype(BF16)
        parts = [dub[:, 0:half], dub[:, half:e], dzb[:, 0:half], dzb[:, half:e]]
        xn, r = _rms(h_ref[...])
        hnb = (xn * gain_ref[...]).astype(BF16)
        for k in range(4):
            acc_in[k] += _dot_tn(hnb, parts[k])
        dhn = _dot_nt(parts[0], win_v[0]) + _dot_nt(parts[1], win_v[1]) + _dot_nt(parts[2], win_v[2]) + _dot_nt(parts[3], win_v[3])
        dh, dgain = _rms_bwd(dhn, xn, r, gain_ref[...])
        dh_ref[...] = dh1 + dh
        dgain_ref[...] += dgain

        @pl.when(i == nt - 1)
        def _():
            _copy_all([(acc_in, dwin_hbm), (acc_out, dwout_hbm)] + [(v, hb_) for hb_, v in _grp_pairs(dwgrp_hbm, acc_grp)], sems)

    row = lambda width: pl.BlockSpec((ts, width), lambda i: (nt - 1 - i, 0))
    return _call(
        body, name=name, grid=(nt,),
        in_specs=[row(d), row(d), row(e), row(e), row(e), _full((1, d)), _full((1, e)), ANY, ANY, ANY],
        out_specs=[row(d), _full((1, e)), _full((1, d)), ANY, ANY, ANY],
        out_shape=[_sds((s, d), F32), _sds((1, e), F32), _sds((1, d), F32), _sds((4, d, half), F32),
                   _sds((4, N_GROUPS, GROUP_DIM // 4, GROUP_DIM), F32), _sds((e, d), F32)],
        scratch_shapes=[pltpu.VMEM((4, d, half), BF16), pltpu.VMEM((N_GROUPS, GROUP_DIM, GROUP_DIM), BF16),
                        pltpu.VMEM((e, d), BF16), pltpu.VMEM((4, d, half), F32),
                        pltpu.VMEM((N_GROUPS, GROUP_DIM, GROUP_DIM), F32), pltpu.VMEM((e, d), F32),
                        pltpu.VMEM((4, HALO, e), F32), pltpu.SemaphoreType.DMA((18,))],
        operands=[dh1, h, z, mx, diff, gain, scale, w_in, w_grp, w_out], rider=rider)


def _first_gather(rider, small):
    shards = rider.inputs
    ni = len(shards)

    def body(*refs):
        rin, small_src = refs[:ni], refs[ni]
        rout, small_dst = refs[ni + 1:2 * ni + 1], refs[2 * ni + 1]
        send, recv, ssend, srecv = refs[2 * ni + 2:]
        x, y, c, chips = _place()
        me = 2 * x + y
        peers = [(cx, cy, c) for cx, cy in chips] + [(x, y, 1 - c)]
        vec = [_remote(small_src, small_dst.at[me], ssend, srecv, j, to) for j, to in enumerate(peers)]
        for cp in vec:
            cp.start()
        rider.start(rin, rout, send, recv)
        rider.middle(rin, rout, send, recv)
        rider.finish(rin, rout, send, recv)
        for j, (px, py, _) in enumerate(peers):
            _remote(small_src, small_dst.at[2 * px + py], ssend, srecv, j, peers[j]).wait_recv()
        for cp in vec:
            cp.wait_send()

    outs = pl.pallas_call(
        body, name="first_gather", in_specs=[ANY] * (ni + 1), out_specs=[ANY] * (ni + 1),
        out_shape=rider.out_shapes + [_sds((4,) + small.shape, small.dtype)],
        scratch_shapes=[pltpu.SemaphoreType.DMA((rider.n_sems,)), pltpu.SemaphoreType.DMA((rider.n_sems,)),
                        pltpu.SemaphoreType.DMA((4,)), pltpu.SemaphoreType.DMA((4,))],
    )(*shards, small)
    return list(outs[:ni]), outs[ni]


def _vector_rider(pack):
    flips = [(fx, fy, fc) for fx in (0, 1) for fy in (0, 1) for fc in (0, 1)][1:]

    def copies(rin, rout, send, recv, base):
        x, y, c, _ = _place()
        me = 4 * x + 2 * y + c
        peers = [(1 - x if fx else x, 1 - y if fy else y, 1 - c if fc else c) for fx, fy, fc in flips]
        own = pltpu.make_async_copy(rin[0], rout[0].at[me], send.at[base + 7])
        out = [_remote(rin[0], rout[0].at[me], send, recv, base + r, peer) for r, peer in enumerate(peers)]
        back = [_remote(rin[0], rout[0].at[4 * px + 2 * py + pc], send, recv, base + r, (px, py, pc))
                for r, (px, py, pc) in enumerate(peers)]
        return own, out, back

    def start(rin, rout, send, recv, base=0):
        own, out, _ = copies(rin, rout, send, recv, base)
        own.start()
        for cp in out:
            cp.start()

    def finish(rin, rout, send, recv, base=0):
        own, out, back = copies(rin, rout, send, recv, base)
        for cp in back:
            cp.wait_recv()
        for cp in out:
            cp.wait_send()
        own.wait()

    return _Rider([pack], [_sds((8,) + pack.shape, pack.dtype)], 8, start, finish)


def _vector_sum(landed, row_counts):
    _, rows, d = landed.shape

    def body(l_ref, *out_refs):
        first = 0
        for n, out_ref in zip(row_counts, out_refs):
            total = l_ref[0, first:first + n, :]
            for dev in range(1, 8):
                total = total + l_ref[dev, first:first + n, :]
            out_ref[...] = total
            first += n

    vmem = pl.BlockSpec(memory_space=pltpu.VMEM)
    return pl.pallas_call(body, name="vector_sum", in_specs=[vmem], out_specs=[vmem] * len(row_counts),
                          out_shape=[_sds((n, d), F32) for n in row_counts])(landed)


def _job_rows(rows, cols):
    return min(rows, max(8, JOB_BLOCK_BYTES // (4 * cols)))


def _pair_sum_job(grad, sibling_rows):
    _, _, rh, cols = grad.shape
    tr = _job_rows(rh, cols)
    nr = rh // tr

    def chip_of(j, pos):
        return jnp.bitwise_xor(pos[0], jnp.where(j == 2, 3, 2 - j))

    return dict(
        ins=[(grad, (None, None, tr, cols), lambda l, pos: (chip_of(l // nr, pos), pos[1], l % nr, 0)),
             (sibling_rows, (None, tr, cols), lambda l, pos: (chip_of(l // nr, pos), l % nr, 0))],
        outs=[((3, rh, cols), BF16, (None, tr, cols), lambda l, pos: (l // nr, l % nr, 0))],
        steps=3 * nr, fn=lambda g, sb: [(g + sb).astype(BF16)], alias=None)


def _final_sum_job(grad, sibling_rows, landed, stack, slot, n_slots):
    _, _, rh, cols = grad.shape
    tr = _job_rows(rh, cols)

    def fn(g, sb, ld):
        total = g + sb
        for j in range(3):
            total = total + ld[j].astype(F32)
        return [total]

    return dict(
        ins=[(grad, (None, None, tr, cols), lambda l, pos: (pos[0], pos[1], l, 0)),
             (sibling_rows, (None, tr, cols), lambda l, pos: (pos[0], l, 0)),
             (landed, (3, tr, cols), lambda l, pos: (0, l, 0))],
        outs=[((n_slots, 2, rh, cols), F32, (None, None, tr, cols), lambda l, pos: (slot, pos[1], l, 0))],
        steps=rh // tr, fn=fn, alias=stack)


def _adamw_job(g, w, m, v, block_bytes):
    rows, cols = g.shape
    tr = min(rows, max(8, block_bytes // (4 * cols)))

    def fn(gg, ww, mm, vv):
        nm = ADAM_B1 * mm + (1.0 - ADAM_B1) * gg
        nv = ADAM_B2 * vv + (1.0 - ADAM_B2) * (gg * gg)
        m_hat = nm / (1.0 - ADAM_B1 ** ADAM_STEP)
        v_hat = nv / (1.0 - ADAM_B2 ** ADAM_STEP)
        return [-ADAM_LR * (m_hat / (jnp.sqrt(v_hat) + ADAM_EPS) + ADAM_WD * ww), nm, nv, gg]

    block = lambda l, pos: (l, 0)
    return dict(ins=[(a, (tr, cols), block) for a in (g, w, m, v)],
                outs=[((rows, cols), F32, (tr, cols), block)] * 4, steps=rows // tr, fn=fn, alias=None)


def _run_jobs(jobs, place, name):
    starts, total = [], 0
    for jb in jobs:
        starts.append(total)
        total += jb["steps"]

    def clamped(fn, start, steps):
        return lambda s, pos: fn(jnp.clip(s - start, 0, steps - 1), pos)

    in_specs, operands = [], [place]
    for jb, start in zip(jobs, starts):
        for arr, block, fn in jb["ins"]:
            in_specs.append(pl.BlockSpec(block, clamped(fn, start, jb["steps"])))
            operands.append(arr)
    n_ins = len(in_specs)
    first_out, n_outs = [], 0
    for jb in jobs:
        first_out.append(n_outs)
        n_outs += len(jb["outs"])
    aliases = {}
    for t, jb in enumerate(jobs):
        if jb["alias"] is not None:
            in_specs.append(ANY)
            operands.append(jb["alias"])
            aliases[len(operands) - 1] = first_out[t]
    out_specs = [pl.BlockSpec(block, clamped(fn, start, jb["steps"]))
                 for jb, start in zip(jobs, starts) for _, _, block, fn in jb["outs"]]

    def body(place_ref, *refs):
        in_refs, out_refs = refs[:n_ins], refs[len(in_specs):]
        s = pl.program_id(0)
        first = 0
        for t, (jb, start) in enumerate(zip(jobs, starts)):
            mine = in_refs[first:first + len(jb["ins"])]
            first += len(jb["ins"])

            @pl.when((s >= start) & (s < start + jb["steps"]))
            def _(mine=mine, t=t, jb=jb):
                values = jb["fn"](*[r[...] for r in mine])
                for n, value in enumerate(values):
                    out_refs[first_out[t] + n][...] = value

    grid_spec = pltpu.PrefetchScalarGridSpec(num_scalar_prefetch=1, grid=(total,), in_specs=in_specs, out_specs=out_specs)
    outs = pl.pallas_call(body, name=name, grid_spec=grid_spec,
                          out_shape=[_sds(shape, dtype) for jb in jobs for shape, dtype, _, _ in jb["outs"]],
                          input_output_aliases=aliases, compiler_params=_params(1))(*operands)
    return [list(outs[first_out[t]:first_out[t] + len(jb["outs"])]) for t, jb in enumerate(jobs)]


BIG = ["a_w_in", "a_w_out", "b_w_in", "b_w_grp", "b_w_out", "ple_w_gate", "ple_w_proj"]

GATHER_PLAN = {
    "first": [("a_w_in", 0), ("a_w_out", 0)],
    "mix0": [("ple_w_gate", 0), ("ple_w_proj", 0), ("b_w_in", 0), ("b_w_grp", 0), ("b_w_out", 0)],
    "ple0": [("ple_w_gate", 1), ("ple_w_proj", 1), ("ple_w_proj", 2), ("a_w_out", 1)],
    "mix1": [("a_w_in", 1), ("ple_w_gate", 2)],
    "mix2": [("b_w_in", 1), ("b_w_grp", 1), ("b_w_out", 1), ("ple_w_gate", 3), ("ple_w_proj", 3)],
}
GATHER_LONGER_THAN_HOST = ("mix1",)


def _as_2d(name, a):
    if name == "b_w_grp":
        return a.reshape(a.shape[0], N_GROUPS * (GROUP_DIM // 4), GROUP_DIM)
    return a


def kernel(x, p, norm_mix, a_w_in, a_w_conv, a_w_out, b_w_in, b_w_grp, b_scale, b_w_out, ple_norm, ple_w_gate, ple_w_proj, final_norm, loss_target, m_norm_mix, m_a_w_in, m_a_w_conv, m_a_w_out, m_b_w_in, m_b_w_grp, m_b_scale, m_b_w_out, m_ple_norm, m_ple_w_gate, m_ple_w_proj, m_final_norm, v_norm_mix, v_a_w_in, v_a_w_conv, v_a_w_out, v_b_w_in, v_b_w_grp, v_b_scale, v_b_w_out, v_ple_norm, v_ple_w_gate, v_ple_w_proj, v_final_norm):
    d, e = D_MODEL, MIX_WIDTH
    s = x.shape[1]
    cx, cy, cc = lax.axis_index("x"), lax.axis_index("y"), lax.axis_index("c")
    chip = 2 * cx + cy
    place = jnp.stack([chip, cc]).astype(jnp.int32)

    weights = dict(a_w_in=a_w_in, a_w_out=a_w_out, b_w_in=b_w_in, b_w_grp=b_w_grp, b_w_out=b_w_out,
                   ple_w_gate=ple_w_gate, ple_w_proj=ple_w_proj)
    moms = dict(a_w_in=m_a_w_in, a_w_out=m_a_w_out, b_w_in=m_b_w_in, b_w_grp=m_b_w_grp, b_w_out=m_b_w_out,
                ple_w_gate=m_ple_w_gate, ple_w_proj=m_ple_w_proj)
    vars_ = dict(a_w_in=v_a_w_in, a_w_out=v_a_w_out, b_w_in=v_b_w_in, b_w_grp=v_b_w_grp, b_w_out=v_b_w_out,
                 ple_w_gate=v_ple_w_gate, ple_w_proj=v_ple_w_proj)
    w2d = {nm: _as_2d(nm, weights[nm]) for nm in BIG}
    bf = {nm: w2d[nm].astype(BF16).reshape(w2d[nm].shape[0], 2, w2d[nm].shape[1] // 2, w2d[nm].shape[2]) for nm in BIG}
    gathered = {}

    def gather_rider(host):
        keys = GATHER_PLAN.get(host)
        if not keys:
            return None
        rider = _gather_rider([bf[nm] for nm, _ in keys], [j for _, j in keys])
        rider.middle_at_end = host in GATHER_LONGER_THAN_HOST
        return rider

    def keep(host, landed):
        for k, a in zip(GATHER_PLAN.get(host, []), landed):
            gathered[k] = a

    def weight(nm, j):
        a = gathered[(nm, j)]
        shapes = {"a_w_in": (4, d, e), "a_w_out": (e, d), "b_w_in": (4, d, e // 2),
                  "b_w_grp": (4, N_GROUPS, GROUP_DIM // 4, GROUP_DIM), "b_w_out": (e, d), "ple_w_gate": (d, d),
                  "ple_w_proj": (4, PLE_DIM, d // 4)}
        return a.reshape(shapes[nm])

    pad = jnp.zeros((4, e // 4), F32)
    small = jnp.concatenate([a_w_conv[0], b_scale[0:1], pad, a_w_conv[1], b_scale[1:2], pad], axis=0)
    landed, small_full = _first_gather(gather_rider("first"), small)
    keep("first", landed)
    small_full = small_full.transpose(1, 0, 2).reshape(16, e)
    conv_w = [_Block(small_full.reshape(2, 8, e), j) for j in range(2)]
    scale_w = [_Block(small_full.reshape(16, 1, e), 8 * j + 3) for j in range(2)]

    p3 = p.reshape(DEPTH, s, PLE_DIM)
    mix_gain = [_Block(norm_mix.reshape(DEPTH, 1, d), i) for i in range(DEPTH)]
    ple_gain = [_Block(ple_norm.reshape(DEPTH, 1, d), i) for i in range(DEPTH)]

    h = x.reshape(s, d)
    saved = []
    for i in range(DEPTH):
        j = i // 2
        rider = gather_rider(f"mix{i}")
        ple = (p3, i, ple_gain[i], weight("ple_w_gate", i), weight("ple_w_proj", i)) if i > 0 else None
        if i % 2 == 0:
            outs, landed = _fwd_mix_a(h, mix_gain[i], conv_w[j], weight("a_w_in", j), weight("a_w_out", j),
                                      f"fwd_mix_a{j}", rider, ple)
            mix = dict(proj=outs[1])
        else:
            outs, landed = _fwd_mix_b(h, mix_gain[i], scale_w[j], weight("b_w_in", j), weight("b_w_grp", j),
                                      weight("b_w_out", j), f"fwd_mix_b{j}", rider, ple)
            mix = dict(z=outs[1], mx=outs[2], diff=outs[3])
        keep(f"mix{i}", landed)
        h1 = outs[0]
        if ple:
            h2, gate = outs[-2:]
        else:
            (h2, gate), landed = _fwd_ple(h1, p3, ple_gain[i], weight("ple_w_gate", i), weight("ple_w_proj", i), i,
                                          gather_rider(f"ple{i}"))
            keep(f"ple{i}", landed)
        saved.append(dict(h=h, h1=h1, gate=gate, **mix))
        h = h2

    n_slots = {nm: weights[nm].shape[0] for nm in BIG}
    stacks = {nm: None for nm in BIG}

    class Group:
        def __init__(self, keys, grads):
            self.keys, self.stage = keys, 0
            self.g32 = [g.reshape(4, 2, w2d[nm].shape[1] // 2, w2d[nm].shape[2]) for (nm, _), g in zip(keys, grads)]

        def rider(self):
            if self.stage == 0:
                return _pair_rider(self.g32)
            if self.stage == 1:
                return _ici_rider(self.pair_sums)
            return _final_rider([stacks[nm] for nm, _ in self.keys], [j for _, j in self.keys])

        def jobs_after(self, landed):
            if self.stage == 0:
                self.from_sibling = landed
                return [_pair_sum_job(g, sb) for g, sb in zip(self.g32, landed)]
            if self.stage == 1:
                return [_final_sum_job(g, sb, ld, stacks[nm], j, n_slots[nm])
                        for (nm, j), g, sb, ld in zip(self.keys, self.g32, self.from_sibling, landed)]
            return []

        def advance(self, landed, summed):
            if self.stage == 0:
                self.pair_sums = summed
            else:
                for (nm, _), a in zip(self.keys, summed if self.stage == 1 else landed):
                    stacks[nm] = a
            self.stage += 1

    active = []
    batches = [0]

    def riders_now():
        parts = [g.rider() for g in active]
        return parts, _merge(parts)

    def advance_all(parts, landed):
        groups = list(active)
        pieces = _split(landed, parts)
        jobs = [g.jobs_after(l) for g, l in zip(groups, pieces)]
        flat = sum(jobs, [])
        outs = [o[0] for o in _run_jobs(flat, place, f"reduce_sums{batches[0]}")] if flat else []
        batches[0] += 1
        for g, l, jb in zip(groups, pieces, jobs):
            g.advance(l, outs[:len(jb)])
            outs = outs[len(jb):]
            if g.stage == 3:
                active.remove(g)

    d_mix_gain, d_ple_gain = [None] * DEPTH, [None] * DEPTH
    d_conv, d_scale = [None] * 2, [None] * 2
    for i in reversed(range(DEPTH)):
        j = i // 2
        sv = saved[i]
        parts, rider = riders_now()
        if i == DEPTH - 1:
            (dh1, d_ple_gain[i], dwg, dwp, loss_part, d_final), landed = _bwd_ple(
                h, sv["h1"], sv["gate"], p3, ple_gain[i], weight("ple_w_gate", i), weight("ple_w_proj", i), i, rider,
                loss_head=(loss_target.reshape(s, d), final_norm.reshape(1, d)))
        else:
            (dh1, d_ple_gain[i], dwg, dwp), landed = _bwd_ple(
                dh, sv["h1"], sv["gate"], p3, ple_gain[i], weight("ple_w_gate", i), weight("ple_w_proj", i), i, rider)
        advance_all(parts, landed)
        active.append(Group([("ple_w_gate", i), ("ple_w_proj", i)], [dwg, dwp]))
        parts, rider = riders_now()
        if i == 0:
            (dproj, d_conv[0], dwin, dwout), landed = _bwd_mix_a_weights(
                dh1, sv["h"], sv["proj"], mix_gain[0], conv_w[0], weight("a_w_out", 0), "bwd_mix_a0_weights", rider)
            advance_all(parts, landed)
            active.append(Group([("a_w_in", 0), ("a_w_out", 0)], [dwin, dwout]))
            parts, rider = riders_now()
            advance_all(parts, _run_rider(rider, "pair_exchange0"))
            parts, rider = riders_now()
            (dh, d_mix_gain[0]), landed = _bwd_mix_a_input(dproj, sv["h"], dh1, mix_gain[0], weight("a_w_in", 0),
                                                          "bwd_mix_a0_input", rider)
            advance_all(parts, landed)
            continue
        if i % 2 == 0:
            (dh, d_conv[j], d_mix_gain[i], dwin, dwout), landed = _bwd_mix_a(
                dh1, sv["h"], sv["proj"], mix_gain[i], conv_w[j], weight("a_w_in", j), weight("a_w_out", j),
                f"bwd_mix_a{j}", rider)
            new = Group([("a_w_in", j), ("a_w_out", j)], [dwin, dwout])
        else:
            (dh, d_scale[j], d_mix_gain[i], dwin, dwgrp, dwout), landed = _bwd_mix_b(
                dh1, sv["h"], sv["z"], sv["mx"], sv["diff"], mix_gain[i], scale_w[j], weight("b_w_in", j),
                weight("b_w_grp", j), weight("b_w_out", j), f"bwd_mix_b{j}", rider)
            new = Group([("b_w_in", j), ("b_w_grp", j), ("b_w_out", j)], [dwin, dwgrp, dwout])
        advance_all(parts, landed)
        active.append(new)
    grad_x = dh.reshape(1, s, d)

    pieces = (d_mix_gain + d_ple_gain + [d_final, d_conv[0][0:3], d_conv[1][0:3]] + d_scale
              + [jnp.tile(loss_part[0:1], (1, d // 128))])
    used = sum(a.shape[0] for a in pieces)
    pack = jnp.concatenate(pieces + [jnp.zeros((-used % PACK_GROUP, d), F32)], axis=0)
    vectors = _vector_rider(pack)
    tail = 0
    while active:
        parts, _ = riders_now()
        extra = [vectors] if tail == 0 else []
        landed = _run_rider(_merge(parts + extra), f"tail_exchange{tail}")
        if extra:
            g_mix, g_ple, g_final, g_conv, g_scale, loss_row = _vector_sum(
                _split(landed, parts + extra)[-1][0], [DEPTH, DEPTH, 1, 6, 2, 1])
        advance_all(parts, landed)
        tail += 1
    loss = loss_row[0, 0]

    mine = lambda a: lax.dynamic_slice_in_dim(a, chip * (e // 4), e // 4, axis=1)
    row = lambda a: a.reshape(1, d)
    taps = lambda a: a.reshape(6, e // 4)
    flat = {nm: (w2d[nm].shape[0] * w2d[nm].shape[1], w2d[nm].shape[2]) for nm in BIG}
    tensors = {nm: (stacks[nm].reshape(flat[nm]), w2d[nm].reshape(flat[nm]), _as_2d(nm, moms[nm]).reshape(flat[nm]),
                    _as_2d(nm, vars_[nm]).reshape(flat[nm])) for nm in BIG}
    tensors.update(
        norm_mix=(g_mix, norm_mix, m_norm_mix, v_norm_mix), ple_norm=(g_ple, ple_norm, m_ple_norm, v_ple_norm),
        final_norm=(g_final, row(final_norm), row(m_final_norm), row(v_final_norm)),
        a_w_conv=(mine(g_conv), taps(a_w_conv), taps(m_a_w_conv), taps(v_a_w_conv)),
        b_scale=(mine(g_scale), b_scale, m_b_scale, v_b_scale))
    order = ["norm_mix", "a_w_in", "a_w_conv", "a_w_out", "b_w_in", "b_w_grp", "b_scale", "b_w_out", "ple_norm",
             "ple_w_gate", "ple_w_proj", "final_norm"]
    shapes = dict(norm_mix=norm_mix.shape, ple_norm=ple_norm.shape, final_norm=final_norm.shape,
                  a_w_conv=a_w_conv.shape, b_scale=b_scale.shape, **{nm: weights[nm].shape for nm in BIG})
    updates = {nm: _run_jobs([_adamw_job(*tensors[nm], ADAMW_BIG_BLOCK_BYTES)], place, f"adamw_{nm}")[0]
               for nm in ADAMW_ALONE}
    rest = [nm for nm in order if nm not in ADAMW_ALONE]
    updates.update(zip(rest, _run_jobs([_adamw_job(*tensors[nm], ADAMW_BLOCK_BYTES) for nm in rest], place, "adamw_rest")))
    outs = [loss, grad_x]
    for which in (3, 0, 1, 2):
        outs += [updates[nm][which].reshape(shapes[nm]) for nm in order]
    return tuple(outs)
```

```python
import jax
import jax.numpy as jnp
from jax import lax
from jax.experimental import pallas as pl
from jax.experimental.pallas import tpu as pltpu

F32 = jnp.float32
BF16 = jnp.bfloat16
MESH = pl.DeviceIdType.MESH

D_MODEL = 1024
MIX_WIDTH = 1024
PLE_DIM = 256
N_GROUPS = 4
GROUP_DIM = 256
POOL_WINDOWS = (2, 4, 8, 16)
DEPTH = 4
EPS = 1e-6

ADAM_LR = 0.001
ADAM_B1 = 0.9
ADAM_B2 = 0.999
ADAM_EPS = 1e-08
ADAM_WD = 0.01
ADAM_STEP = 10

HALO = 8
CHANNEL_SPLIT = 2
TS_MIX = 256
TS_BWD_POOL = 512
TS_FWD = 512
TS_PLE = 512
VMEM_LIMIT = 56 * 1024 * 1024
PACK_GROUP = 8
JOB_BLOCK_BYTES = 2 * 1024 * 1024
ADAMW_BLOCK_BYTES = 512 * 1024
ADAMW_BIG_BLOCK_BYTES = 2 * 1024 * 1024
ADAMW_ALONE = ("a_w_in", "b_w_in", "ple_w_gate")
MIDDLE_STEPS_BEFORE_END = 1

ANY = pl.BlockSpec(memory_space=pl.ANY)


def _sds(shape, dtype):
    return jax.ShapeDtypeStruct(shape, dtype)


def _full(shape):
    nd = len(shape)
    return pl.BlockSpec(shape, lambda *_: (0,) * nd)


def _params(n_axes=1):
    return pltpu.CompilerParams(dimension_semantics=("arbitrary",) * n_axes, vmem_limit_bytes=VMEM_LIMIT)


def _dot(a, b):
    return jnp.dot(a, b, preferred_element_type=F32)


def _dot_nt(a, b):
    return lax.dot_general(a, b, (((1,), (1,)), ((), ())), preferred_element_type=F32)


def _dot_tn(a, b):
    return lax.dot_general(a, b, (((0,), (0,)), ((), ())), preferred_element_type=F32)


def _sigmoid(z):
    return 1.0 / (1.0 + jnp.exp(-z))


def _shift_down(x, k, tail):
    rolled = pltpu.roll(x, k, 0)
    rt = tail if k % HALO == 0 else pltpu.roll(tail, k % HALO, 0)
    row = lax.broadcasted_iota(jnp.int32, rt.shape, 0)
    head = jnp.where(row < k, rt, rolled[0:HALO])
    return jnp.concatenate([head, rolled[HALO:]], axis=0)


def _shift_up(x, k, head_next):
    n = x.shape[0]
    rolled = pltpu.roll(x, n - k, 0)
    rh = head_next if k % HALO == 0 else pltpu.roll(head_next, HALO - k % HALO, 0)
    row = lax.broadcasted_iota(jnp.int32, rh.shape, 0)
    tail = jnp.where(row >= HALO - k, rh, rolled[n - HALO:n])
    return jnp.concatenate([rolled[:n - HALO], tail], axis=0)


def _inv_counts(tile, ts):
    t = tile * ts + lax.broadcasted_iota(jnp.int32, (ts, 1), 0)
    return [1.0 / jnp.minimum(t + 1, w).astype(F32) for w in POOL_WINDOWS]


def _pool_fwd(u, carry, tile, ts):
    inv = _inv_counts(tile, ts)
    outs = []
    for g, w in enumerate(POOL_WINDOWS):
        cols = slice(g * GROUP_DIM, (g + 1) * GROUP_DIM)
        s = u[:, cols]
        level, k = 0, 1
        while k < w:
            tail = carry[level, :, cols]
            carry[level, :, cols] = s[ts - HALO:ts]
            s = s + _shift_down(s, k, tail)
            level, k = level + 1, k * 2
        outs.append(s * inv[g])
    return jnp.concatenate(outs, axis=1)


def _pool_bwd(dd, carry, tile, ts):
    inv = _inv_counts(tile, ts)
    outs = []
    for g, w in enumerate(POOL_WINDOWS):
        cols = slice(g * GROUP_DIM, (g + 1) * GROUP_DIM)
        q = dd[:, cols] * inv[g]
        level, k = 0, 1
        while k < w:
            head = carry[level, :, cols]
            carry[level, :, cols] = q[0:HALO]
            q = q + _shift_up(q, k, head)
            level, k = level + 1, k * 2
        outs.append(q)
    return jnp.concatenate(outs, axis=1)


def _copy_all(pairs, sems):
    copies = [pltpu.make_async_copy(src, dst, sems.at[n]) for n, (src, dst) in enumerate(pairs)]
    for cp in copies:
        cp.start()
    for cp in copies:
        cp.wait()


def _grp_pairs(wgrp_hbm, wgrp_v):
    rows = GROUP_DIM // 4
    return [(wgrp_hbm.at[k, g], wgrp_v.at[g, pl.ds(k * rows, rows), :]) for k in range(4) for g in range(N_GROUPS)]


def _rms(h):
    r = lax.rsqrt(jnp.mean(h * h, axis=-1, keepdims=True) + EPS)
    return h * r, r


def _rms_bwd(dhn, xn, r, gain):
    dgain = jnp.sum(dhn * xn, axis=0, keepdims=True)
    dxn = dhn * gain
    dh = r * (dxn - xn * jnp.mean(dxn * xn, axis=-1, keepdims=True))
    return dh, dgain


class _Rider:
    def __init__(self, inputs, out_shapes, n_sems, start, finish, middle=None, aliases=None):
        self.inputs, self.out_shapes, self.n_sems = list(inputs), list(out_shapes), n_sems
        self.start, self.middle, self.finish = start, middle, finish
        self.aliases = dict(aliases or {})
        self.middle_at_end = False


def _merge(riders):
    riders = [r for r in riders if r is not None]
    if not riders:
        return None
    if len(riders) == 1:
        return riders[0]

    def phase(which):
        def run(rin, rout, send, recv, base=0):
            i0 = o0 = s0 = 0
            for r in riders:
                fn = getattr(r, which)
                if fn is not None:
                    fn(rin[i0:i0 + len(r.inputs)], rout[o0:o0 + len(r.out_shapes)], send, recv, base + s0)
                i0, o0, s0 = i0 + len(r.inputs), o0 + len(r.out_shapes), s0 + r.n_sems
        return run

    aliases, i0, o0 = {}, 0, 0
    for r in riders:
        aliases.update({i0 + a: o0 + b for a, b in r.aliases.items()})
        i0, o0 = i0 + len(r.inputs), o0 + len(r.out_shapes)
    return _Rider(sum([r.inputs for r in riders], []), sum([r.out_shapes for r in riders], []),
                  sum(r.n_sems for r in riders), phase("start"), phase("finish"),
                  phase("middle") if any(r.middle for r in riders) else None, aliases)


def _split(landed, riders):
    out, o0 = [], 0
    for r in riders:
        if r is None:
            out.append(None)
        else:
            out.append(landed[o0:o0 + len(r.out_shapes)])
            o0 += len(r.out_shapes)
    return out


def _place():
    x, y, c = lax.axis_index("x"), lax.axis_index("y"), lax.axis_index("c")
    chips = [(1 - x, y), (x, 1 - y), (1 - x, 1 - y)]
    return x, y, c, chips


def _remote(src, dst, send_sems, recv_sems, sem, to):
    return pltpu.make_async_remote_copy(src_ref=src, dst_ref=dst, send_sem=send_sems.at[sem], recv_sem=recv_sems.at[sem],
                                        device_id=to, device_id_type=MESH)


def _gather_rider(stacked, slots):
    ni = len(stacked)

    def first_hops(rin, rout, send, recv, base, x, y, c, chips):
        me = 2 * x + y
        return [_remote(rin[t].at[slots[t], c], rout[t].at[me, c], send, recv, base + 7 * t + j, (cx, cy, c))
                for j, (cx, cy) in enumerate(chips) for t in range(ni)]

    def passes(rout, send, recv, base, x, y, c, chips):
        out = []
        for j, (cx, cy) in enumerate(chips):
            for t in range(ni):
                landed = rout[t].at[2 * cx + cy, c]
                out.append((_remote(landed, landed, send, recv, base + 7 * t + j, (x, y, 1 - c)),
                            _remote(landed, landed, send, recv, base + 7 * t + 3 + j, (x, y, 1 - c))))
        return out

    def own(rin, rout, send, recv, base, x, y, c):
        return [_remote(rin[t].at[slots[t]], rout[t].at[2 * x + y], send, recv, base + 7 * t + 6, (x, y, 1 - c))
                for t in range(ni)]

    def start(rin, rout, send, recv, base=0):
        x, y, c, chips = _place()
        for cp in first_hops(rin, rout, send, recv, base, x, y, c, chips) + own(rin, rout, send, recv, base, x, y, c):
            cp.start()

    def middle(rin, rout, send, recv, base=0):
        x, y, c, chips = _place()
        for arrival, onward in passes(rout, send, recv, base, x, y, c, chips):
            arrival.wait_recv()
            onward.start()

    def finish(rin, rout, send, recv, base=0):
        x, y, c, chips = _place()
        for j, (cx, cy) in enumerate(chips):
            for t in range(ni):
                other = rout[t].at[2 * cx + cy, 1 - c]
                _remote(other, other, send, recv, base + 7 * t + 3 + j, (x, y, 1 - c)).wait_recv()
        for cp in own(rin, rout, send, recv, base, x, y, c):
            cp.wait_recv()
            cp.wait_send()
        for cp in first_hops(rin, rout, send, recv, base, x, y, c, chips):
            cp.wait_send()
        for _, onward in passes(rout, send, recv, base, x, y, c, chips):
            onward.wait_send()

    return _Rider(stacked, [_sds((4,) + a.shape[1:], a.dtype) for a in stacked], 7 * ni, start, finish, middle)


def _pair_rider(grads):
    ni = len(grads)

    def copies(rin, rout, send, recv, base):
        x, y, c, _ = _place()
        return [_remote(rin[t].at[:, 1 - c], rout[t], send, recv, base + t, (x, y, 1 - c)) for t in range(ni)]

    def start(rin, rout, send, recv, base=0):
        for cp in copies(rin, rout, send, recv, base):
            cp.start()

    def finish(rin, rout, send, recv, base=0):
        for cp in copies(rin, rout, send, recv, base):
            cp.wait()

    return _Rider(grads, [_sds(g.shape[:1] + g.shape[2:], g.dtype) for g in grads], ni, start, finish)


def _ici_rider(pair_sums):
    ni = len(pair_sums)

    def copies(rin, rout, send, recv, base):
        x, y, c, chips = _place()
        return [_remote(rin[t].at[j], rout[t].at[j], send, recv, base + 3 * t + j, (cx, cy, c))
                for j, (cx, cy) in enumerate(chips) for t in range(ni)]

    def start(rin, rout, send, recv, base=0):
        for cp in copies(rin, rout, send, recv, base):
            cp.start()

    def finish(rin, rout, send, recv, base=0):
        for cp in copies(rin, rout, send, recv, base):
            cp.wait()

    return _Rider(pair_sums, [_sds((3,) + g.shape[1:], g.dtype) for g in pair_sums], 3 * ni, start, finish)


def _final_rider(summed, slots):
    ni = len(summed)

    def copies(rout, send, recv, base):
        x, y, c, _ = _place()
        return [(_remote(rout[t].at[slots[t], c], rout[t].at[slots[t], c], send, recv, base + t, (x, y, 1 - c)),
                 _remote(rout[t].at[slots[t], 1 - c], rout[t].at[slots[t], 1 - c], send, recv, base + t, (x, y, 1 - c)))
                for t in range(ni)]

    def start(rin, rout, send, recv, base=0):
        for mine, _ in copies(rout, send, recv, base):
            mine.start()

    def finish(rin, rout, send, recv, base=0):
        for mine, theirs in copies(rout, send, recv, base):
            mine.wait_send()
            theirs.wait_recv()

    return _Rider(summed, [_sds(a.shape, a.dtype) for a in summed], ni, start, finish,
                  aliases={t: t for t in range(ni)})


class _Block:
    def __init__(self, array, index):
        self.array, self.index = array, index

    def spec(self):
        index = self.index
        return pl.BlockSpec((None,) + self.array.shape[1:], lambda *_: (index, 0, 0))


def _call(body, *, name, grid, in_specs, out_specs, out_shape, scratch_shapes, operands, rider=None):
    operands, in_specs = list(operands), list(in_specs)
    for n, op in enumerate(operands):
        if isinstance(op, _Block):
            operands[n], in_specs[n] = op.array, op.spec()
    if rider is None:
        outs = pl.pallas_call(body, name=name, grid=grid, in_specs=in_specs, out_specs=out_specs, out_shape=out_shape,
                              scratch_shapes=scratch_shapes, compiler_params=_params(len(grid)))(*operands)
        return list(outs), []
    n_in, n_out, n_scr = len(in_specs), len(out_specs), len(scratch_shapes)
    r_in, r_out = len(rider.inputs), len(rider.out_shapes)
    steps = 1
    for g in grid:
        steps *= g
    mid = steps - 1 if rider.middle_at_end else max(steps - 1 - MIDDLE_STEPS_BEFORE_END, 0)

    def full_body(*refs):
        own_in, rin = refs[:n_in], refs[n_in:n_in + r_in]
        own_out = refs[n_in + r_in:n_in + r_in + n_out]
        rout = refs[n_in + r_in + n_out:n_in + r_in + n_out + r_out]
        own_scr = refs[n_in + r_in + n_out + r_out:n_in + r_in + n_out + r_out + n_scr]
        send, recv = refs[-2], refs[-1]
        step = pl.program_id(0)
        for axis in range(1, len(grid)):
            step = step * grid[axis] + pl.program_id(axis)

        @pl.when(step == 0)
        def _():
            rider.start(rin, rout, send, recv)

        body(*own_in, *own_out, *own_scr)

        if rider.middle is not None:
            @pl.when(step == mid)
            def _():
                rider.middle(rin, rout, send, recv)

        @pl.when(step == steps - 1)
        def _():
            rider.finish(rin, rout, send, recv)

    outs = pl.pallas_call(
        full_body, name=name, grid=grid,
        in_specs=list(in_specs) + [ANY] * r_in, out_specs=list(out_specs) + [ANY] * r_out,
        out_shape=list(out_shape) + rider.out_shapes,
        scratch_shapes=list(scratch_shapes) + [pltpu.SemaphoreType.DMA((rider.n_sems,)), pltpu.SemaphoreType.DMA((rider.n_sems,))],
        input_output_aliases={n_in + a: n_out + b for a, b in rider.aliases.items()},
        compiler_params=_params(len(grid)),
    )(*operands, *rider.inputs)
    return list(outs[:n_out]), list(outs[n_out:])


def _run_rider(rider, name):
    r_in, r_out = len(rider.inputs), len(rider.out_shapes)

    def body(*refs):
        rin, rout, send, recv = refs[:r_in], refs[r_in:r_in + r_out], refs[-2], refs[-1]
        rider.start(rin, rout, send, recv)
        if rider.middle is not None:
            rider.middle(rin, rout, send, recv)
        rider.finish(rin, rout, send, recv)

    outs = pl.pallas_call(
        body, name=name, in_specs=[ANY] * r_in, out_specs=[ANY] * r_out, out_shape=rider.out_shapes,
        scratch_shapes=[pltpu.SemaphoreType.DMA((rider.n_sems,)), pltpu.SemaphoreType.DMA((rider.n_sems,))],
        input_output_aliases=rider.aliases,
    )(*rider.inputs)
    return list(outs)


def _ple_tile(h1, p_ref, gain_ref, wg_v, wp_v):
    xn, _ = _rms(h1)
    hpb = (xn * gain_ref[...]).astype(BF16)
    gate = _sigmoid(_dot(hpb, wg_v[...]))
    pb = p_ref[...].astype(BF16)
    pe = jnp.concatenate([_dot(pb, wp_v[k]) for k in range(4)], axis=1)
    return h1 + gate * pe, gate


def _ple_parts(ple, ts, d):
    p, layer, gain, w_gate, w_proj = ple
    s, pd = p.shape[1:]
    row = pl.BlockSpec((ts, d), lambda i: (i, 0))
    return dict(
        operands=[p, gain, w_gate, w_proj],
        in_specs=[pl.BlockSpec((None, ts, pd), lambda i: (layer, i, 0)), _full((1, d)), ANY, ANY],
        out_specs=[row, row], out_shape=[_sds((s, d), F32), _sds((s, d), BF16)],
        scratch=[pltpu.VMEM((d, d), BF16), pltpu.VMEM((4, pd, d // 4), BF16)])


def _fwd_mix_a(h, gain, conv_w, w_in, w_out, name, rider=None, ple=None):
    s, d = h.shape
    e = MIX_WIDTH
    ts = min(TS_FWD, s)
    nt = s // ts
    extra = _ple_parts(ple, ts, d) if ple else None

    def body(*refs):
        h_ref, gain_ref, cw_ref, win_hbm, wout_hbm = refs[:5]
        n_in = 9 if ple else 5
        h1_ref, proj_ref = refs[n_in:n_in + 2]
        win_v, wout_v, carry, sems = refs[n_in + (4 if ple else 2):][:4]
        i = pl.program_id(0)

        @pl.when(i == 0)
        def _():
            loads = [(win_hbm, win_v), (wout_hbm, wout_v)]
            if ple:
                loads += [(refs[7], refs[-2]), (refs[8], refs[-1])]
            _copy_all(loads, sems)
            carry[...] = jnp.zeros_like(carry)

        hh = h_ref[...]
        xn, _ = _rms(hh)
        hnb = (xn * gain_ref[...]).astype(BF16)
        b = _dot(hnb, win_v[0])
        c = _dot(hnb, win_v[1])
        v = _dot(hnb, win_v[2])
        z = _dot(hnb, win_v[3])
        proj_ref[:, 0 * e:1 * e] = b.astype(BF16)
        proj_ref[:, 1 * e:2 * e] = c.astype(BF16)
        proj_ref[:, 2 * e:3 * e] = v.astype(BF16)
        proj_ref[:, 3 * e:4 * e] = z.astype(BF16)
        cv = c * v
        tail = carry[...]
        carry[...] = cv[ts - HALO:ts]
        conv = cw_ref[0:1, :] * _shift_down(cv, 2, tail) + cw_ref[1:2, :] * _shift_down(cv, 1, tail) + cw_ref[2:3, :] * cv
        mb = ((z * _sigmoid(z)) * (b * conv)).astype(BF16)
        h1 = hh + _dot(mb, wout_v[...])
        h1_ref[...] = h1
        if ple:
            h2, gate = _ple_tile(h1, refs[5], refs[6], refs[-2], refs[-1])
            refs[n_in + 2][...] = h2
            refs[n_in + 3][...] = gate.astype(BF16)

    row = lambda width: pl.BlockSpec((ts, width), lambda i: (i, 0))
    return _call(
        body, name=name, grid=(nt,),
        in_specs=[row(d), _full((1, d)), _full((8, e)), ANY, ANY] + (extra["in_specs"] if ple else []),
        out_specs=[row(d), row(4 * e)] + (extra["out_specs"] if ple else []),
        out_shape=[_sds((s, d), F32), _sds((s, 4 * e), BF16)] + (extra["out_shape"] if ple else []),
        scratch_shapes=[pltpu.VMEM((4, d, e), BF16), pltpu.VMEM((e, d), BF16), pltpu.VMEM((HALO, e), F32),
                        pltpu.SemaphoreType.DMA((4,))] + (extra["scratch"] if ple else []),
        operands=[h, gain, conv_w, w_in, w_out] + (extra["operands"] if ple else []), rider=rider)


def _fwd_mix_b(h, gain, scale, w_in, w_grp, w_out, name, rider=None, ple=None):
    s, d = h.shape
    e = MIX_WIDTH
    ts = min(TS_FWD, s)
    nt = s // ts
    extra = _ple_parts(ple, ts, d) if ple else None

    def body(*refs):
        h_ref, gain_ref, scale_ref, win_hbm, wgrp_hbm, wout_hbm = refs[:6]
        n_in = 10 if ple else 6
        h1_ref, z_ref, mx_ref, dd_ref = refs[n_in:n_in + 4]
        win_v, wgrp_v, wout_v, carry, sems = refs[n_in + (6 if ple else 4):][:5]
        i = pl.program_id(0)

        @pl.when(i == 0)
        def _():
            loads = [(win_hbm, win_v), (wout_hbm, wout_v)] + _grp_pairs(wgrp_hbm, wgrp_v)
            if ple:
                loads += [(refs[8], refs[-2]), (refs[9], refs[-1])]
            _copy_all(loads, sems)
            carry[...] = jnp.zeros_like(carry)

        hh = h_ref[...]
        xn, _ = _rms(hh)
        hnb = (xn * gain_ref[...]).astype(BF16)
        u = jnp.concatenate([_dot(hnb, win_v[0]), _dot(hnb, win_v[1])], axis=1)
        z = jnp.concatenate([_dot(hnb, win_v[2]), _dot(hnb, win_v[3])], axis=1)
        z_ref[...] = z.astype(BF16)
        diff = (_pool_fwd(u, carry, i, ts) - u).astype(BF16)
        dd_ref[...] = diff
        mx = jnp.concatenate(
            [_dot(diff[:, g * GROUP_DIM:(g + 1) * GROUP_DIM], wgrp_v[g]) for g in range(N_GROUPS)], axis=1)
        mx_ref[...] = mx.astype(BF16)
        mb = ((z * _sigmoid(z)) * (mx * scale_ref[...])).astype(BF16)
        h1 = hh + _dot(mb, wout_v[...])
        h1_ref[...] = h1
        if ple:
            h2, gate = _ple_tile(h1, refs[6], refs[7], refs[-2], refs[-1])
            refs[n_in + 4][...] = h2
            refs[n_in + 5][...] = gate.astype(BF16)

    row = lambda width: pl.BlockSpec((ts, width), lambda i: (i, 0))
    return _call(
        body, name=name, grid=(nt,),
        in_specs=[row(d), _full((1, d)), _full((1, e)), ANY, ANY, ANY] + (extra["in_specs"] if ple else []),
        out_specs=[row(d), row(e), row(e), row(e)] + (extra["out_specs"] if ple else []),
        out_shape=[_sds((s, d), F32)] + [_sds((s, e), BF16)] * 3 + (extra["out_shape"] if ple else []),
        scratch_shapes=[pltpu.VMEM((4, d, e // 2), BF16), pltpu.VMEM((N_GROUPS, GROUP_DIM, GROUP_DIM), BF16),
                        pltpu.VMEM((e, d), BF16), pltpu.VMEM((4, HALO, e), F32), pltpu.SemaphoreType.DMA((20,))]
        + (extra["scratch"] if ple else []),
        operands=[h, gain, scale, w_in, w_grp, w_out] + (extra["operands"] if ple else []), rider=rider)


def _fwd_ple(h1, p, gain, w_gate, w_proj, layer, rider=None):
    s, d = h1.shape
    pd = p.shape[-1]
    ts = min(TS_PLE, s)
    nt = s // ts

    def body(h1_ref, p_ref, gain_ref, wg_hbm, wp_hbm, h2_ref, gate_ref, wg_v, wp_v, sems):
        @pl.when(pl.program_id(0) == 0)
        def _():
            _copy_all([(wg_hbm, wg_v), (wp_hbm, wp_v)], sems)

        hh = h1_ref[...]
        xn, _ = _rms(hh)
        hpb = (xn * gain_ref[...]).astype(BF16)
        gate = _sigmoid(_dot(hpb, wg_v[...]))
        pb = p_ref[...].astype(BF16)
        pe = jnp.concatenate([_dot(pb, wp_v[k]) for k in range(4)], axis=1)
        gate_ref[...] = gate.astype(BF16)
        h2_ref[...] = hh + gate * pe

    row = lambda width: pl.BlockSpec((ts, width), lambda i: (i, 0))
    return _call(
        body, name=f"fwd_ple{layer}", grid=(nt,),
        in_specs=[row(d), pl.BlockSpec((None, ts, pd), lambda i: (layer, i, 0)), _full((1, d)), ANY, ANY],
        out_specs=[row(d), row(d)],
        out_shape=[_sds((s, d), F32), _sds((s, d), BF16)],
        scratch_shapes=[pltpu.VMEM((d, d), BF16), pltpu.VMEM((4, pd, d // 4), BF16), pltpu.SemaphoreType.DMA((2,))],
        operands=[h1, p, gain, w_gate, w_proj], rider=rider)


def _bwd_ple(dh2, h1, gate, p, gain, w_gate, w_proj, layer, rider=None, loss_head=None):
    s, d = dh2.shape
    pd = p.shape[-1]
    ts = min(TS_PLE, s)
    nt = s // ts
    qd = d // 4
    n_head = 0 if loss_head is None else 2

    def body(*refs):
        dh2_ref = refs[0]
        h1_ref, gate_ref, p_ref, gain_ref, wg_hbm, wp_hbm, dh1_ref, dgain_ref, dwg_hbm, dwp_hbm = refs[1 + n_head:11 + n_head]
        wg_v, wp_v, acc_g, acc_p, sems = refs[-5:]
        i = pl.program_id(0)

        @pl.when(i == 0)
        def _():
            _copy_all([(wg_hbm, wg_v), (wp_hbm, wp_v)], sems)
            dgain_ref[...] = jnp.zeros_like(dgain_ref)
            acc_g[...] = jnp.zeros_like(acc_g)
            acc_p[...] = jnp.zeros_like(acc_p)

        if loss_head is None:
            g2 = dh2_ref[...]
        else:
            t_ref, fgain_ref, loss_ref, dfgain_ref = refs[1], refs[2], refs[11 + n_head], refs[12 + n_head]

            @pl.when(i == 0)
            def _():
                loss_ref[...] = jnp.zeros_like(loss_ref)
                dfgain_ref[...] = jnp.zeros_like(dfgain_ref)

            xf, rf = _rms(dh2_ref[...])
            err = xf * fgain_ref[...] - t_ref[...]
            part = 0.5 * jnp.sum(jnp.mean(err * err, axis=-1, keepdims=True), axis=0, keepdims=True)
            loss_ref[...] += jnp.broadcast_to(part, loss_ref.shape)
            g2, dfgain = _rms_bwd(err * (1.0 / d), xf, rf, fgain_ref[...])
            dfgain_ref[...] += dfgain
        gate_f = gate_ref[...].astype(F32)
        xn, r = _rms(h1_ref[...])
        hpb = (xn * gain_ref[...]).astype(BF16)
        pb = p_ref[...].astype(BF16)
        pe = jnp.concatenate([_dot(pb, wp_v[k]) for k in range(4)], axis=1)
        dpeb = (g2 * gate_f).astype(BF16)
        dab = ((g2 * pe) * (gate_f * (1.0 - gate_f))).astype(BF16)
        acc_g[...] += _dot_tn(hpb, dab)
        for k in range(4):
            acc_p[k] += _dot_tn(pb, dpeb[:, k * qd:(k + 1) * qd])
        dhp = _dot_nt(dab, wg_v[...])
        dh, dgain = _rms_bwd(dhp, xn, r, gain_ref[...])
        dh1_ref[...] = g2 + dh
        dgain_ref[...] += dgain

        @pl.when(i == nt - 1)
        def _():
            _copy_all([(acc_g, dwg_hbm), (acc_p, dwp_hbm)], sems)

    row = pl.BlockSpec((ts, d), lambda i: (i, 0))
    head = loss_head is not None
    return _call(
        body, name=f"bwd_ple{layer}", grid=(nt,),
        in_specs=[row] + ([row, _full((1, d))] if head else [])
        + [row, row, pl.BlockSpec((None, ts, pd), lambda i: (layer, i, 0)), _full((1, d)), ANY, ANY],
        out_specs=[row, _full((1, d)), ANY, ANY] + ([_full((8, 128)), _full((1, d))] if head else []),
        out_shape=[_sds((s, d), F32), _sds((1, d), F32), _sds((d, d), F32), _sds((4, pd, qd), F32)]
        + ([_sds((8, 128), F32), _sds((1, d), F32)] if head else []),
        scratch_shapes=[pltpu.VMEM((d, d), BF16), pltpu.VMEM((4, pd, qd), BF16), pltpu.VMEM((d, d), F32),
                        pltpu.VMEM((4, pd, qd), F32), pltpu.SemaphoreType.DMA((2,))],
        operands=[dh2] + (list(loss_head) if head else []) + [h1, gate, p, gain, w_gate, w_proj], rider=rider)


def _mix_a_tile_grads(proj_ref, ch_ref, vh_ref, cw_ref, dh1b, wout_v, carry, dcw_ref, tile, hb):
    e = MIX_WIDTH
    width = e // CHANNEL_SPLIT
    pieces = []
    for n in range(CHANNEL_SPLIT):
        cols = slice(n * width, (n + 1) * width)
        b, c, v, z = [proj_ref[:, q * e + n * width:q * e + (n + 1) * width].astype(F32) for q in range(4)]
        taps = [cw_ref[k:k + 1, cols] for k in range(3)]
        cv = c * v
        prev = (ch_ref[:, cols].astype(F32) * vh_ref[:, cols].astype(F32))[hb - HALO:hb]
        tail = jnp.where(tile > 0, prev, jnp.zeros_like(prev))
        cv1 = _shift_down(cv, 1, tail)
        cv2 = _shift_down(cv, 2, tail)
        conv = taps[0] * cv2 + taps[1] * cv1 + taps[2] * cv
        sig = _sigmoid(z)
        sz = z * sig
        y = b * conv
        dm = _dot_nt(dh1b, wout_v[cols, :])
        dz = (dm * y) * (sig * (1.0 + z * (1.0 - sig)))
        dy = dm * sz
        db = dy * conv
        dconv = dy * b
        head = carry[:, cols]
        carry[:, cols] = dconv[0:HALO]
        dcv = taps[2] * dconv + taps[1] * _shift_up(dconv, 1, head) + taps[0] * _shift_up(dconv, 2, head)
        dcw_ref[0:1, cols] += jnp.sum(dconv * cv2, axis=0, keepdims=True)
        dcw_ref[1:2, cols] += jnp.sum(dconv * cv1, axis=0, keepdims=True)
        dcw_ref[2:3, cols] += jnp.sum(dconv * cv, axis=0, keepdims=True)
        pieces.append([db.astype(BF16), (dcv * v).astype(BF16), (dcv * c).astype(BF16), dz.astype(BF16),
                       (sz * y).astype(BF16)])
    whole = [jnp.concatenate([piece[q] for piece in pieces], axis=1) for q in range(5)]
    return whole[:4], whole[4]


def _bwd_mix_a(dh1, h, proj, gain, conv_w, w_in, w_out, name, rider=None):
    s, d = dh1.shape
    e = MIX_WIDTH
    ts = min(TS_MIX, s)
    nt = s // ts
    hb = 16
    per = ts // hb

    def body(dh1_ref, h_ref, proj_ref, ch_ref, vh_ref, gain_ref, cw_ref, win_hbm, wout_hbm,
             dh_ref, dcw_ref, dgain_ref, dwin_hbm, dwout_hbm, win_v, wout_v, acc_in, acc_out, carry, sems):
        i = pl.program_id(0)

        @pl.when(i == 0)
        def _():
            _copy_all([(win_hbm, win_v), (wout_hbm, wout_v)], sems)
            carry[...] = jnp.zeros_like(carry)
            dcw_ref[...] = jnp.zeros_like(dcw_ref)
            dgain_ref[...] = jnp.zeros_like(dgain_ref)
            acc_in[...] = jnp.zeros_like(acc_in)
            acc_out[...] = jnp.zeros_like(acc_out)

        dh1 = dh1_ref[...]
        dh1b = dh1.astype(BF16)
        parts, mb = _mix_a_tile_grads(proj_ref, ch_ref, vh_ref, cw_ref, dh1b, wout_v, carry, dcw_ref, nt - 1 - i, hb)
        acc_out[...] += _dot_tn(mb, dh1b)
        xn, r = _rms(h_ref[...])
        hnb = (xn * gain_ref[...]).astype(BF16)
        for q in range(4):
            acc_in[q] += _dot_tn(hnb, parts[q])
        dhn = _dot_nt(parts[0], win_v[0]) + _dot_nt(parts[1], win_v[1]) + _dot_nt(parts[2], win_v[2]) + _dot_nt(parts[3], win_v[3])
        dh, dgain = _rms_bwd(dhn, xn, r, gain_ref[...])
        dh_ref[...] = dh1 + dh
        dgain_ref[...] += dgain

        @pl.when(i == nt - 1)
        def _():
            _copy_all([(acc_in, dwin_hbm), (acc_out, dwout_hbm)], sems)

    row = lambda width: pl.BlockSpec((ts, width), lambda i: (nt - 1 - i, 0))
    halo = lambda col: pl.BlockSpec((hb, e), lambda i: (jnp.maximum((nt - 1 - i) * per - 1, 0), col))
    return _call(
        body, name=name, grid=(nt,),
        in_specs=[row(d), row(d), row(4 * e), halo(1), halo(2), _full((1, d)), _full((8, e)), ANY, ANY],
        out_specs=[row(d), _full((8, e)), _full((1, d)), ANY, ANY],
        out_shape=[_sds((s, d), F32), _sds((8, e), F32), _sds((1, d), F32), _sds((4, d, e), F32), _sds((e, d), F32)],
        scratch_shapes=[pltpu.VMEM((4, d, e), BF16), pltpu.VMEM((e, d), BF16), pltpu.VMEM((4, d, e), F32),
                        pltpu.VMEM((e, d), F32), pltpu.VMEM((HALO, e), F32), pltpu.SemaphoreType.DMA((2,))],
        operands=[dh1, h, proj, proj, proj, gain, conv_w, w_in, w_out], rider=rider)


def _bwd_mix_a_weights(dh1, h, proj, gain, conv_w, w_out, name, rider=None):
    s, d = dh1.shape
    e = MIX_WIDTH
    ts = min(TS_MIX, s)
    nt = s // ts
    hb = 16
    per = ts // hb

    def body(dh1_ref, h_ref, proj_ref, ch_ref, vh_ref, gain_ref, cw_ref, wout_hbm,
             dproj_ref, dcw_ref, dwin_hbm, dwout_hbm, wout_v, acc_in, acc_out, carry, sems):
        i = pl.program_id(0)

        @pl.when(i == 0)
        def _():
            _copy_all([(wout_hbm, wout_v)], sems)
            carry[...] = jnp.zeros_like(carry)
            dcw_ref[...] = jnp.zeros_like(dcw_ref)
            acc_in[...] = jnp.zeros_like(acc_in)
            acc_out[...] = jnp.zeros_like(acc_out)

        dh1b = dh1_ref[...].astype(BF16)
        parts, mb = _mix_a_tile_grads(proj_ref, ch_ref, vh_ref, cw_ref, dh1b, wout_v, carry, dcw_ref, nt - 1 - i, hb)
        acc_out[...] += _dot_tn(mb, dh1b)
        xn, _ = _rms(h_ref[...])
        hnb = (xn * gain_ref[...]).astype(BF16)
        for q in range(4):
            acc_in[q] += _dot_tn(hnb, parts[q])
            dproj_ref[:, q * e:(q + 1) * e] = parts[q]

        @pl.when(i == nt - 1)
        def _():
            _copy_all([(acc_in, dwin_hbm), (acc_out, dwout_hbm)], sems)

    row = lambda width: pl.BlockSpec((ts, width), lambda i: (nt - 1 - i, 0))
    halo = lambda col: pl.BlockSpec((hb, e), lambda i: (jnp.maximum((nt - 1 - i) * per - 1, 0), col))
    return _call(
        body, name=name, grid=(nt,),
        in_specs=[row(d), row(d), row(4 * e), halo(1), halo(2), _full((1, d)), _full((8, e)), ANY],
        out_specs=[row(4 * e), _full((8, e)), ANY, ANY],
        out_shape=[_sds((s, 4 * e), BF16), _sds((8, e), F32), _sds((4, d, e), F32), _sds((e, d), F32)],
        scratch_shapes=[pltpu.VMEM((e, d), BF16), pltpu.VMEM((4, d, e), F32), pltpu.VMEM((e, d), F32),
                        pltpu.VMEM((HALO, e), F32), pltpu.SemaphoreType.DMA((2,))],
        operands=[dh1, h, proj, proj, proj, gain, conv_w, w_out], rider=rider)


def _bwd_mix_a_input(dproj, h, dh1, gain, w_in, name, rider=None):
    s, d = dh1.shape
    e = MIX_WIDTH
    ts = min(TS_PLE, s)
    nt = s // ts

    def body(dproj_ref, h_ref, dh1_ref, gain_ref, win_hbm, dh_ref, dgain_ref, win_v, sems):
        @pl.when(pl.program_id(0) == 0)
        def _():
            _copy_all([(win_hbm, win_v)], sems)
            dgain_ref[...] = jnp.zeros_like(dgain_ref)

        dhn = _dot_nt(dproj_ref[:, 0:e], win_v[0])
        for q in range(1, 4):
            dhn = dhn + _dot_nt(dproj_ref[:, q * e:(q + 1) * e], win_v[q])
        xn, r = _rms(h_ref[...])
        dh, dgain = _rms_bwd(dhn, xn, r, gain_ref[...])
        dh_ref[...] = dh1_ref[...] + dh
        dgain_ref[...] += dgain

    row = lambda width: pl.BlockSpec((ts, width), lambda i: (i, 0))
    return _call(
        body, name=name, grid=(nt,),
        in_specs=[row(4 * e), row(d), row(d), _full((1, d)), ANY],
        out_specs=[row(d), _full((1, d))],
        out_shape=[_sds((s, d), F32), _sds((1, d), F32)],
        scratch_shapes=[pltpu.VMEM((4, d, e), BF16), pltpu.SemaphoreType.DMA((1,))],
        operands=[dproj, h, dh1, gain, w_in], rider=rider)


def _bwd_mix_b(dh1, h, z, mx, diff, gain, scale, w_in, w_grp, w_out, name, rider=None):
    s, d = dh1.shape
    e = MIX_WIDTH
    ts = min(TS_BWD_POOL, s)
    nt = s // ts
    half = e // 2

    def body(dh1_ref, h_ref, z_ref, mx_ref, dd_ref, gain_ref, scale_ref, win_hbm, wgrp_hbm, wout_hbm,
             dh_ref, dscale_ref, dgain_ref, dwin_hbm, dwgrp_hbm, dwout_hbm,
             win_v, wgrp_v, wout_v, acc_in, acc_grp, acc_out, carry, sems):
        i = pl.program_id(0)
        tile = nt - 1 - i

        @pl.when(i == 0)
        def _():
            _copy_all([(win_hbm, win_v), (wout_hbm, wout_v)] + _grp_pairs(wgrp_hbm, wgrp_v), sems)
            carry[...] = jnp.zeros_like(carry)
            dscale_ref[...] = jnp.zeros_like(dscale_ref)
            dgain_ref[...] = jnp.zeros_like(dgain_ref)
            acc_in[...] = jnp.zeros_like(acc_in)
            acc_grp[...] = jnp.zeros_like(acc_grp)
            acc_out[...] = jnp.zeros_like(acc_out)

        zf = z_ref[...].astype(F32)
        mxf = mx_ref[...].astype(F32)
        sig = _sigmoid(zf)
        sz = zf * sig
        mixed = mxf * scale_ref[...]
        dh1 = dh1_ref[...]
        dh1b = dh1.astype(BF16)
        acc_out[...] += _dot_tn((sz * mixed).astype(BF16), dh1b)
        dm = _dot_nt(dh1b, wout_v[...])
        dz = (dm * mixed) * (sig * (1.0 + zf * (1.0 - sig)))
        dmixed = dm * sz
        dscale_ref[...] += jnp.sum(dmixed * mxf, axis=0, keepdims=True)
        dmxb = (dmixed * scale_ref[...]).astype(BF16)
        diff = dd_ref[...]
        for g in range(N_GROUPS):
            cols = slice(g * GROUP_DIM, (g + 1) * GROUP_DIM)
            acc_grp[g] += _dot_tn(diff[:, cols], dmxb[:, cols])
        ddiff = jnp.concatenate(
            [_dot_nt(dmxb[:, g * GROUP_DIM:(g + 1) * GROUP_DIM], wgrp_v[g]) for g in range(N_GROUPS)], axis=1)
        dub = (_pool_bwd(ddiff, carry, tile, ts) - ddiff).astype(BF16)
        dzb = dz.astype(BF16)
        parts = [dub[:, 0:half], dub[:, half:e], dzb[:, 0:half], dzb[:, half:e]]
        xn, r = _rms(h_ref[...])
        hnb = (xn * gain_ref[...]).astype(BF16)
        for k in range(4):
            acc_in[k] += _dot_tn(hnb, parts[k])
        dhn = _dot_nt(parts[0], win_v[0]) + _dot_nt(parts[1], win_v[1]) + _dot_nt(parts[2], win_v[2]) + _dot_nt(parts[3], win_v[3])
        dh, dgain = _rms_bwd(dhn, xn, r, gain_ref[...])
        dh_ref[...] = dh1 + dh
        dgain_ref[...] += dgain

        @pl.when(i == nt - 1)
        def _():
            _copy_all([(acc_in, dwin_hbm), (acc_out, dwout_hbm)] + [(v, hb_) for hb_, v in _grp_pairs(dwgrp_hbm, acc_grp)], sems)

    row = lambda width: pl.BlockSpec((ts, width), lambda i: (nt - 1 - i, 0))
    return _call(
        body, name=name, grid=(nt,),
        in_specs=[row(d), row(d), row(e), row(e), row(e), _full((1, d)), _full((1, e)), ANY, ANY, ANY],
        out_specs=[row(d), _full((1, e)), _full((1, d)), ANY, ANY, ANY],
        out_shape=[_sds((s, d), F32), _sds((1, e), F32), _sds((1, d), F32), _sds((4, d, half), F32),
                   _sds((4, N_GROUPS, GROUP_DIM // 4, GROUP_DIM), F32), _sds((e, d), F32)],
        scratch_shapes=[pltpu.VMEM((4, d, half), BF16), pltpu.VMEM((N_GROUPS, GROUP_DIM, GROUP_DIM), BF16),
                        pltpu.VMEM((e, d), BF16), pltpu.VMEM((4, d, half), F32),
                        pltpu.VMEM((N_GROUPS, GROUP_DIM, GROUP_DIM), F32), pltpu.VMEM((e, d), F32),
                        pltpu.VMEM((4, HALO, e), F32), pltpu.SemaphoreType.DMA((18,))],
        operands=[dh1, h, z, mx, diff, gain, scale, w_in, w_grp, w_out], rider=rider)


def _first_gather(rider, small):
    shards = rider.inputs
    ni = len(shards)

    def body(*refs):
        rin, small_src = refs[:ni], refs[ni]
        rout, small_dst = refs[ni + 1:2 * ni + 1], refs[2 * ni + 1]
        send, recv, ssend, srecv = refs[2 * ni + 2:]
        x, y, c, chips = _place()
        me = 2 * x + y
        peers = [(cx, cy, c) for cx, cy in chips] + [(x, y, 1 - c)]
        vec = [_remote(small_src, small_dst.at[me], ssend, srecv, j, to) for j, to in enumerate(peers)]
        for cp in vec:
            cp.start()
        rider.start(rin, rout, send, recv)
        rider.middle(rin, rout, send, recv)
        rider.finish(rin, rout, send, recv)
        for j, (px, py, _) in enumerate(peers):
            _remote(small_src, small_dst.at[2 * px + py], ssend, srecv, j, peers[j]).wait_recv()
        for cp in vec:
            cp.wait_send()

    outs = pl.pallas_call(
        body, name="first_gather", in_specs=[ANY] * (ni + 1), out_specs=[ANY] * (ni + 1),
        out_shape=rider.out_shapes + [_sds((4,) + small.shape, small.dtype)],
        scratch_shapes=[pltpu.SemaphoreType.DMA((rider.n_sems,)), pltpu.SemaphoreType.DMA((rider.n_sems,)),
                        pltpu.SemaphoreType.DMA((4,)), pltpu.SemaphoreType.DMA((4,))],
    )(*shards, small)
    return list(outs[:ni]), outs[ni]


def _vector_rider(pack):
    flips = [(fx, fy, fc) for fx in (0, 1) for fy in (0, 1) for fc in (0, 1)][1:]

    def copies(rin, rout, send, recv, base):
        x, y, c, _ = _place()
        me = 4 * x + 2 * y + c
        peers = [(1 - x if fx else x, 1 - y if fy else y, 1 - c if fc else c) for fx, fy, fc in flips]
        own = pltpu.make_async_copy(rin[0], rout[0].at[me], send.at[base + 7])
        out = [_remote(rin[0], rout[0].at[me], send, recv, base + r, peer) for r, peer in enumerate(peers)]
        back = [_remote(rin[0], rout[0].at[4 * px + 2 * py + pc], send, recv, base + r, (px, py, pc))
                for r, (px, py, pc) in enumerate(peers)]
        return own, out, back

    def start(rin, rout, send, recv, base=0):
        own, out, _ = copies(rin, rout, send, recv, base)
        own.start()
        for cp in out:
            cp.start()

    def finish(rin, rout, send, recv, base=0):
        own, out, back = copies(rin, rout, send, recv, base)
        for cp in back:
            cp.wait_recv()
        for cp in out:
            cp.wait_send()
        own.wait()

    return _Rider([pack], [_sds((8,) + pack.shape, pack.dtype)], 8, start, finish)


def _vector_sum(landed, row_counts):
    _, rows, d = landed.shape

    def body(l_ref, *out_refs):
        first = 0
        for n, out_ref in zip(row_counts, out_refs):
            total = l_ref[0, first:first + n, :]
            for dev in range(1, 8):
                total = total + l_ref[dev, first:first + n, :]
            out_ref[...] = total
            first += n

    vmem = pl.BlockSpec(memory_space=pltpu.VMEM)
    return pl.pallas_call(body, name="vector_sum", in_specs=[vmem], out_specs=[vmem] * len(row_counts),
                          out_shape=[_sds((n, d), F32) for n in row_counts])(landed)


def _job_rows(rows, cols):
    return min(rows, max(8, JOB_BLOCK_BYTES // (4 * cols)))


def _pair_sum_job(grad, sibling_rows):
    _, _, rh, cols = grad.shape
    tr = _job_rows(rh, cols)
    nr = rh // tr

    def chip_of(j, pos):
        return jnp.bitwise_xor(pos[0], jnp.where(j == 2, 3, 2 - j))

    return dict(
        ins=[(grad, (None, None, tr, cols), lambda l, pos: (chip_of(l // nr, pos), pos[1], l % nr, 0)),
             (sibling_rows, (None, tr, cols), lambda l, pos: (chip_of(l // nr, pos), l % nr, 0))],
        outs=[((3, rh, cols), BF16, (None, tr, cols), lambda l, pos: (l // nr, l % nr, 0))],
        steps=3 * nr, fn=lambda g, sb: [(g + sb).astype(BF16)], alias=None)


def _final_sum_job(grad, sibling_rows, landed, stack, slot, n_slots):
    _, _, rh, cols = grad.shape
    tr = _job_rows(rh, cols)

    def fn(g, sb, ld):
        total = g + sb
        for j in range(3):
            total = total + ld[j].astype(F32)
        return [total]

    return dict(
        ins=[(grad, (None, None, tr, cols), lambda l, pos: (pos[0], pos[1], l, 0)),
             (sibling_rows, (None, tr, cols), lambda l, pos: (pos[0], l, 0)),
             (landed, (3, tr, cols), lambda l, pos: (0, l, 0))],
        outs=[((n_slots, 2, rh, cols), F32, (None, None, tr, cols), lambda l, pos: (slot, pos[1], l, 0))],
        steps=rh // tr, fn=fn, alias=stack)


def _adamw_job(g, w, m, v, block_bytes):
    rows, cols = g.shape
    tr = min(rows, max(8, block_bytes // (4 * cols)))

    def fn(gg, ww, mm, vv):
        nm = ADAM_B1 * mm + (1.0 - ADAM_B1) * gg
        nv = ADAM_B2 * vv + (1.0 - ADAM_B2) * (gg * gg)
        m_hat = nm / (1.0 - ADAM_B1 ** ADAM_STEP)
        v_hat = nv / (1.0 - ADAM_B2 ** ADAM_STEP)
        return [-ADAM_LR * (m_hat / (jnp.sqrt(v_hat) + ADAM_EPS) + ADAM_WD * ww), nm, nv, gg]

    block = lambda l, pos: (l, 0)
    return dict(ins=[(a, (tr, cols), block) for a in (g, w, m, v)],
                outs=[((rows, cols), F32, (tr, cols), block)] * 4, steps=rows // tr, fn=fn, alias=None)


def _run_jobs(jobs, place, name):
    starts, total = [], 0
    for jb in jobs:
        starts.append(total)
        total += jb["steps"]

    def clamped(fn, start, steps):
        return lambda s, pos: fn(jnp.clip(s - start, 0, steps - 1), pos)

    in_specs, operands = [], [place]
    for jb, start in zip(jobs, starts):
        for arr, block, fn in jb["ins"]:
            in_specs.append(pl.BlockSpec(block, clamped(fn, start, jb["steps"])))
            operands.append(arr)
    n_ins = len(in_specs)
    first_out, n_outs = [], 0
    for jb in jobs:
        first_out.append(n_outs)
        n_outs += len(jb["outs"])
    aliases = {}
    for t, jb in enumerate(jobs):
        if jb["alias"] is not None:
            in_specs.append(ANY)
            operands.append(jb["alias"])
            aliases[len(operands) - 1] = first_out[t]
    out_specs = [pl.BlockSpec(block, clamped(fn, start, jb["steps"]))
                 for jb, start in zip(jobs, starts) for _, _, block, fn in jb["outs"]]

    def body(place_ref, *refs):
        in_refs, out_refs = refs[:n_ins], refs[len(in_specs):]
        s = pl.program_id(0)
        first = 0
        for t, (jb, start) in enumerate(zip(jobs, starts)):
            mine = in_refs[first:first + len(jb["ins"])]
            first += len(jb["ins"])

            @pl.when((s >= start) & (s < start + jb["steps"]))
            def _(mine=mine, t=t, jb=jb):
                values = jb["fn"](*[r[...] for r in mine])
                for n, value in enumerate(values):
                    out_refs[first_out[t] + n][...] = value

    grid_spec = pltpu.PrefetchScalarGridSpec(num_scalar_prefetch=1, grid=(total,), in_specs=in_specs, out_specs=out_specs)
    outs = pl.pallas_call(body, name=name, grid_spec=grid_spec,
                          out_shape=[_sds(shape, dtype) for jb in jobs for shape, dtype, _, _ in jb["outs"]],
                          input_output_aliases=aliases, compiler_params=_params(1))(*operands)
    return [list(outs[first_out[t]:first_out[t] + len(jb["outs"])]) for t, jb in enumerate(jobs)]


BIG = ["a_w_in", "a_w_out", "b_w_in", "b_w_grp", "b_w_out", "ple_w_gate", "ple_w_proj"]

GATHER_PLAN = {
    "first": [("a_w_in", 0), ("a_w_out", 0)],
    "mix0": [("ple_w_gate", 0), ("ple_w_proj", 0), ("b_w_in", 0), ("b_w_grp", 0), ("b_w_out", 0)],
    "ple0": [("ple_w_gate", 1), ("ple_w_proj", 1), ("ple_w_proj", 2)],
    "mix1": [("a_w_in", 1), ("a_w_out", 1), ("ple_w_gate", 2)],
    "mix2": [("b_w_in", 1), ("b_w_grp", 1), ("b_w_out", 1), ("ple_w_gate", 3), ("ple_w_proj", 3)],
}
GATHER_LONGER_THAN_HOST = ("mix1",)


def _as_2d(name, a):
    if name == "b_w_grp":
        return a.reshape(a.shape[0], N_GROUPS * (GROUP_DIM // 4), GROUP_DIM)
    return a


def kernel(x, p, norm_mix, a_w_in, a_w_conv, a_w_out, b_w_in, b_w_grp, b_scale, b_w_out, ple_norm, ple_w_gate, ple_w_proj, final_norm, loss_target, m_norm_mix, m_a_w_in, m_a_w_conv, m_a_w_out, m_b_w_in, m_b_w_grp, m_b_scale, m_b_w_out, m_ple_norm, m_ple_w_gate, m_ple_w_proj, m_final_norm, v_norm_mix, v_a_w_in, v_a_w_conv, v_a_w_out, v_b_w_in, v_b_w_grp, v_b_scale, v_b_w_out, v_ple_norm, v_ple_w_gate, v_ple_w_proj, v_final_norm):
    d, e = D_MODEL, MIX_WIDTH
    s = x.shape[1]
    cx, cy, cc = lax.axis_index("x"), lax.axis_index("y"), lax.axis_index("c")
    chip = 2 * cx + cy
    place = jnp.stack([chip, cc]).astype(jnp.int32)

    weights = dict(a_w_in=a_w_in, a_w_out=a_w_out, b_w_in=b_w_in, b_w_grp=b_w_grp, b_w_out=b_w_out,
                   ple_w_gate=ple_w_gate, ple_w_proj=ple_w_proj)
    moms = dict(a_w_in=m_a_w_in, a_w_out=m_a_w_out, b_w_in=m_b_w_in, b_w_grp=m_b_w_grp, b_w_out=m_b_w_out,
                ple_w_gate=m_ple_w_gate, ple_w_proj=m_ple_w_proj)
    vars_ = dict(a_w_in=v_a_w_in, a_w_out=v_a_w_out, b_w_in=v_b_w_in, b_w_grp=v_b_w_grp, b_w_out=v_b_w_out,
                 ple_w_gate=v_ple_w_gate, ple_w_proj=v_ple_w_proj)
    w2d = {nm: _as_2d(nm, weights[nm]) for nm in BIG}
    bf = {nm: w2d[nm].astype(BF16).reshape(w2d[nm].shape[0], 2, w2d[nm].shape[1] // 2, w2d[nm].shape[2]) for nm in BIG}
    gathered = {}

    def gather_rider(host):
        keys = GATHER_PLAN.get(host)
        if not keys:
            return None
        rider = _gather_rider([bf[nm] for nm, _ in keys], [j for _, j in keys])
        rider.middle_at_end = host in GATHER_LONGER_THAN_HOST
        return rider

    def keep(host, landed):
        for k, a in zip(GATHER_PLAN.get(host, []), landed):
            gathered[k] = a

    def weight(nm, j):
        a = gathered[(nm, j)]
        shapes = {"a_w_in": (4, d, e), "a_w_out": (e, d), "b_w_in": (4, d, e // 2),
                  "b_w_grp": (4, N_GROUPS, GROUP_DIM // 4, GROUP_DIM), "b_w_out": (e, d), "ple_w_gate": (d, d),
                  "ple_w_proj": (4, PLE_DIM, d // 4)}
        return a.reshape(shapes[nm])

    pad = jnp.zeros((4, e // 4), F32)
    small = jnp.concatenate([a_w_conv[0], b_scale[0:1], pad, a_w_conv[1], b_scale[1:2], pad], axis=0)
    landed, small_full = _first_gather(gather_rider("first"), small)
    keep("first", landed)
    small_full = small_full.transpose(1, 0, 2).reshape(16, e)
    conv_w = [_Block(small_full.reshape(2, 8, e), j) for j in range(2)]
    scale_w = [_Block(small_full.reshape(16, 1, e), 8 * j + 3) for j in range(2)]

    p3 = p.reshape(DEPTH, s, PLE_DIM)
    mix_gain = [_Block(norm_mix.reshape(DEPTH, 1, d), i) for i in range(DEPTH)]
    ple_gain = [_Block(ple_norm.reshape(DEPTH, 1, d), i) for i in range(DEPTH)]

    h = x.reshape(s, d)
    saved = []
    for i in range(DEPTH):
        j = i // 2
        rider = gather_rider(f"mix{i}")
        ple = (p3, i, ple_gain[i], weight("ple_w_gate", i), weight("ple_w_proj", i)) if i > 0 else None
        if i % 2 == 0:
            outs, landed = _fwd_mix_a(h, mix_gain[i], conv_w[j], weight("a_w_in", j), weight("a_w_out", j),
                                      f"fwd_mix_a{j}", rider, ple)
            mix = dict(proj=outs[1])
        else:
            outs, landed = _fwd_mix_b(h, mix_gain[i], scale_w[j], weight("b_w_in", j), weight("b_w_grp", j),
                                      weight("b_w_out", j), f"fwd_mix_b{j}", rider, ple)
            mix = dict(z=outs[1], mx=outs[2], diff=outs[3])
        keep(f"mix{i}", landed)
        h1 = outs[0]
        if ple:
            h2, gate = outs[-2:]
        else:
            (h2, gate), landed = _fwd_ple(h1, p3, ple_gain[i], weight("ple_w_gate", i), weight("ple_w_proj", i), i,
                                          gather_rider(f"ple{i}"))
            keep(f"ple{i}", landed)
        saved.append(dict(h=h, h1=h1, gate=gate, **mix))
        h = h2

    n_slots = {nm: weights[nm].shape[0] for nm in BIG}
    stacks = {nm: None for nm in BIG}

    class Group:
        def __init__(self, keys, grads):
            self.keys, self.stage = keys, 0
            self.g32 = [g.reshape(4, 2, w2d[nm].shape[1] // 2, w2d[nm].shape[2]) for (nm, _), g in zip(keys, grads)]

        def rider(self):
            if self.stage == 0:
                return _pair_rider(self.g32)
            if self.stage == 1:
                return _ici_rider(self.pair_sums)
            return _final_rider([stacks[nm] for nm, _ in self.keys], [j for _, j in self.keys])

        def jobs_after(self, landed):
            if self.stage == 0:
                self.from_sibling = landed
                return [_pair_sum_job(g, sb) for g, sb in zip(self.g32, landed)]
            if self.stage == 1:
                return [_final_sum_job(g, sb, ld, stacks[nm], j, n_slots[nm])
                        for (nm, j), g, sb, ld in zip(self.keys, self.g32, self.from_sibling, landed)]
            return []

        def advance(self, landed, summed):
            if self.stage == 0:
                self.pair_sums = summed
            else:
                for (nm, _), a in zip(self.keys, summed if self.stage == 1 else landed):
                    stacks[nm] = a
            self.stage += 1

    active = []
    batches = [0]

    def riders_now():
        parts = [g.rider() for g in active]
        return parts, _merge(parts)

    def advance_all(parts, landed):
        groups = list(active)
        pieces = _split(landed, parts)
        jobs = [g.jobs_after(l) for g, l in zip(groups, pieces)]
        flat = sum(jobs, [])
        outs = [o[0] for o in _run_jobs(flat, place, f"reduce_sums{batches[0]}")] if flat else []
        batches[0] += 1
        for g, l, jb in zip(groups, pieces, jobs):
            g.advance(l, outs[:len(jb)])
            outs = outs[len(jb):]
            if g.stage == 3:
                active.remove(g)

    d_mix_gain, d_ple_gain = [None] * DEPTH, [None] * DEPTH
    d_conv, d_scale = [None] * 2, [None] * 2
    for i in reversed(range(DEPTH)):
        j = i // 2
        sv = saved[i]
        parts, rider = riders_now()
        if i == DEPTH - 1:
            (dh1, d_ple_gain[i], dwg, dwp, loss_part, d_final), landed = _bwd_ple(
                h, sv["h1"], sv["gate"], p3, ple_gain[i], weight("ple_w_gate", i), weight("ple_w_proj", i), i, rider,
                loss_head=(loss_target.reshape(s, d), final_norm.reshape(1, d)))
        else:
            (dh1, d_ple_gain[i], dwg, dwp), landed = _bwd_ple(
                dh, sv["h1"], sv["gate"], p3, ple_gain[i], weight("ple_w_gate", i), weight("ple_w_proj", i), i, rider)
        advance_all(parts, landed)
        active.append(Group([("ple_w_gate", i), ("ple_w_proj", i)], [dwg, dwp]))
        parts, rider = riders_now()
        if i == 0:
            (dproj, d_conv[0], dwin, dwout), landed = _bwd_mix_a_weights(
                dh1, sv["h"], sv["proj"], mix_gain[0], conv_w[0], weight("a_w_out", 0), "bwd_mix_a0_weights", rider)
            advance_all(parts, landed)
            active.append(Group([("a_w_in", 0), ("a_w_out", 0)], [dwin, dwout]))
            parts, rider = riders_now()
            advance_all(parts, _run_rider(rider, "pair_exchange0"))
            parts, rider = riders_now()
            (dh, d_mix_gain[0]), landed = _bwd_mix_a_input(dproj, sv["h"], dh1, mix_gain[0], weight("a_w_in", 0),
                                                          "bwd_mix_a0_input", rider)
            advance_all(parts, landed)
            continue
        if i % 2 == 0:
            (dh, d_conv[j], d_mix_gain[i], dwin, dwout), landed = _bwd_mix_a(
                dh1, sv["h"], sv["proj"], mix_gain[i], conv_w[j], weight("a_w_in", j), weight("a_w_out", j),
                f"bwd_mix_a{j}", rider)
            new = Group([("a_w_in", j), ("a_w_out", j)], [dwin, dwout])
        else:
            (dh, d_scale[j], d_mix_gain[i], dwin, dwgrp, dwout), landed = _bwd_mix_b(
                dh1, sv["h"], sv["z"], sv["mx"], sv["diff"], mix_gain[i], scale_w[j], weight("b_w_in", j),
                weight("b_w_grp", j), weight("b_w_out", j), f"bwd_mix_b{j}", rider)
            new = Group([("b_w_in", j), ("b_w_grp", j), ("b_w_out", j)], [dwin, dwgrp, dwout])
        advance_all(parts, landed)
        active.append(new)
    grad_x = dh.reshape(1, s, d)

    pieces = (d_mix_gain + d_ple_gain + [d_final, d_conv[0][0:3], d_conv[1][0:3]] + d_scale
              + [jnp.tile(loss_part[0:1], (1, d // 128))])
    used = sum(a.shape[0] for a in pieces)
    pack = jnp.concatenate(pieces + [jnp.zeros((-used % PACK_GROUP, d), F32)], axis=0)
    vectors = _vector_rider(pack)
    tail = 0
    while active:
        parts, _ = riders_now()
        extra = [vectors] if tail == 0 else []
        landed = _run_rider(_merge(parts + extra), f"tail_exchange{tail}")
        if extra:
            g_mix, g_ple, g_final, g_conv, g_scale, loss_row = _vector_sum(
                _split(landed, parts + extra)[-1][0], [DEPTH, DEPTH, 1, 6, 2, 1])
        advance_all(parts, landed)
        tail += 1
    loss = loss_row[0, 0]

    mine = lambda a: lax.dynamic_slice_in_dim(a, chip * (e // 4), e // 4, axis=1)
    row = lambda a: a.reshape(1, d)
    taps = lambda a: a.reshape(6, e // 4)
    flat = {nm: (w2d[nm].shape[0] * w2d[nm].shape[1], w2d[nm].shape[2]) for nm in BIG}
    tensors = {nm: (stacks[nm].reshape(flat[nm]), w2d[nm].reshape(flat[nm]), _as_2d(nm, moms[nm]).reshape(flat[nm]),
                    _as_2d(nm, vars_[nm]).reshape(flat[nm])) for nm in BIG}
    tensors.update(
        norm_mix=(g_mix, norm_mix, m_norm_mix, v_norm_mix), ple_norm=(g_ple, ple_norm, m_ple_norm, v_ple_norm),
        final_norm=(g_final, row(final_norm), row(m_final_norm), row(v_final_norm)),
        a_w_conv=(mine(g_conv), taps(a_w_conv), taps(m_a_w_conv), taps(v_a_w_conv)),
        b_scale=(mine(g_scale), b_scale, m_b_scale, v_b_scale))
    order = ["norm_mix", "a_w_in", "a_w_conv", "a_w_out", "b_w_in", "b_w_grp", "b_scale", "b_w_out", "ple_norm",
             "ple_w_gate", "ple_w_proj", "final_norm"]
    shapes = dict(norm_mix=norm_mix.shape, ple_norm=ple_norm.shape, final_norm=final_norm.shape,
                  a_w_conv=a_w_conv.shape, b_scale=b_scale.shape, **{nm: weights[nm].shape for nm in BIG})
    updates = {nm: _run_jobs([_adamw_job(*tensors[nm], ADAMW_BIG_BLOCK_BYTES)], place, f"adamw_{nm}")[0]
               for nm in ADAMW_ALONE}
    rest = [nm for nm in order if nm not in ADAMW_ALONE]
    updates.update(zip(rest, _run_jobs([_adamw_job(*tensors[nm], ADAMW_BLOCK_BYTES) for nm in rest], place, "adamw_rest")))
    outs = [loss, grad_x]
    for which in (3, 0, 1, 2):
        outs += [updates[nm][which].reshape(shapes[nm]) for nm in order]
    return tuple(outs)
```
